```python
import jax, jax.numpy as jnp
from jax import lax
import numpy as np

D_MODEL = 1024
BATCH = 8
SEQ = 8192
DEPTH = 1

D_FF = 2816
D_CONV = D_MODEL // 2
D_POOL = D_MODEL - D_CONV
CONV_WIDTH = 31
POOL_WINDOWS = (2, 4, 8, 16)
N_POOL_GROUPS = len(POOL_WINDOWS)
POOL_GROUP = D_POOL // N_POOL_GROUPS
D_IN = 2 * D_CONV + D_POOL
RMS_EPS = 1e-6
LN_EPS = 1e-5
FFN_RES_WEIGHT = 0.5

kernel_name = "macaron_conv_pool_hybrid_layer"


def _rmsnorm(x, g):
    xf = x.astype(jnp.float32)
    r = lax.rsqrt(jnp.mean(xf * xf, axis=-1, keepdims=True) + RMS_EPS)
    return (xf * r).astype(x.dtype) * g


def _layernorm(x, g, b):
    xf = x.astype(jnp.float32)
    mu = jnp.mean(xf, axis=-1, keepdims=True)
    var = jnp.mean(jnp.square(xf - mu), axis=-1, keepdims=True)
    return ((xf - mu) * lax.rsqrt(var + LN_EPS)).astype(x.dtype) * g + b


def _swiglu(h, w_gate, w_up, w_down):
    return (jax.nn.silu(h @ w_gate) * (h @ w_up)) @ w_down


def _causal_depthwise_conv(u, w, b):
    c = u.shape[-1]
    rhs = w.reshape(CONV_WIDTH, 1, c).astype(u.dtype)
    out = lax.conv_general_dilated(
        u, rhs, window_strides=(1,), padding=[(CONV_WIDTH - 1, 0)],
        dimension_numbers=("NWC", "WIO", "NWC"), feature_group_count=c)
    return out + b


def _multiscale_pool(p, pool_w, pool_scale):
    bsz, t, _ = p.shape
    pf = p.reshape(bsz, t, N_POOL_GROUPS, POOL_GROUP).astype(jnp.float32)
    cs = jnp.cumsum(pf, axis=1)
    pos = jnp.arange(t, dtype=jnp.float32)[None, :, None] + 1.0
    pooled = []
    for gi, w in enumerate(POOL_WINDOWS):
        csg = cs[:, :, gi]
        lag = jnp.pad(csg, ((0, 0), (w, 0), (0, 0)))[:, :t]
        cnt = jnp.minimum(pos, float(w))
        pooled.append((csg - lag) / cnt)
    mixed = (jnp.stack(pooled, axis=2) - pf).astype(p.dtype)
    out = jnp.einsum("btgc,gcd->btgd", mixed, pool_w)
    return out.reshape(bsz, t, D_POOL) * pool_scale


def _fwd_setup_inputs(seed: int = 0) -> dict:
    key = jax.random.key(seed)
    ks = jax.random.split(key, 24)
    f32 = jnp.float32

    def nrm(k, shape, fan_in):
        return jax.random.normal(k, shape, f32) * (fan_in ** -0.5)

    def gain(k, n):
        return 1.0 + 0.02 * jax.random.normal(k, (n,), f32)

    def bias(k, n):
        return 0.02 * jax.random.normal(k, (n,), f32)

    return {
        "x": jax.random.normal(ks[0], (BATCH, SEQ, D_MODEL), f32),
        "ffn1_norm": gain(ks[1], D_MODEL),
        "ffn1_w_gate": nrm(ks[2], (D_MODEL, D_FF), D_MODEL),
        "ffn1_w_up": nrm(ks[3], (D_MODEL, D_FF), D_MODEL),
        "ffn1_w_down": nrm(ks[4], (D_FF, D_MODEL), D_FF),
        "mix_norm": gain(ks[5], D_MODEL),
        "w_in": nrm(ks[6], (D_MODEL, D_IN), D_MODEL),
        "conv_dw": nrm(ks[7], (CONV_WIDTH, D_CONV), CONV_WIDTH),
        "conv_dw_b": bias(ks[8], D_CONV),
        "conv_ln_g": gain(ks[9], D_CONV),
        "conv_ln_b": bias(ks[10], D_CONV),
        "conv_pw": nrm(ks[11], (D_CONV, D_CONV), D_CONV),
        "pool_w": nrm(ks[12], (N_POOL_GROUPS, POOL_GROUP, POOL_GROUP), POOL_GROUP),
        "pool_scale": gain(ks[13], D_POOL),
        "w_out": nrm(ks[14], (D_CONV + D_POOL, D_MODEL), D_CONV + D_POOL),
        "ffn2_norm": gain(ks[15], D_MODEL),
        "ffn2_w_gate": nrm(ks[16], (D_MODEL, D_FF), D_MODEL),
        "ffn2_w_up": nrm(ks[17], (D_MODEL, D_FF), D_MODEL),
        "ffn2_w_down": nrm(ks[18], (D_FF, D_MODEL), D_FF),
        "final_norm": gain(ks[19], D_MODEL),
    }


def _fwd_reference(x, ffn1_norm, ffn1_w_gate, ffn1_w_up, ffn1_w_down, mix_norm, w_in,
              conv_dw, conv_dw_b, conv_ln_g, conv_ln_b, conv_pw, pool_w, pool_scale,
              w_out, ffn2_norm, ffn2_w_gate, ffn2_w_up, ffn2_w_down, final_norm):
    for _ in range(DEPTH):
        x = x + FFN_RES_WEIGHT * _swiglu(_rmsnorm(x, ffn1_norm), ffn1_w_gate, ffn1_w_up, ffn1_w_down)

        h = _rmsnorm(x, mix_norm)
        proj = h @ w_in
        a = proj[..., :D_CONV]
        g = proj[..., D_CONV:2 * D_CONV]
        p = proj[..., 2 * D_CONV:]

        u = a * jax.nn.sigmoid(g)
        u = _causal_depthwise_conv(u, conv_dw, conv_dw_b)
        u = jax.nn.silu(_layernorm(u, conv_ln_g, conv_ln_b))
        conv_out = u @ conv_pw

        pool_out = _multiscale_pool(p, pool_w, pool_scale)

        x = x + jnp.concatenate([conv_out, pool_out], axis=-1) @ w_out

        x = x + FFN_RES_WEIGHT * _swiglu(_rmsnorm(x, ffn2_norm), ffn2_w_gate, ffn2_w_up, ffn2_w_down)
    return _rmsnorm(x, final_norm)


import jax as _jax
import jax.numpy as _jnp

TWIN_FORMAT = 'train_step'
FWD_PARAMS = ['x', 'ffn1_norm', 'ffn1_w_gate', 'ffn1_w_up', 'ffn1_w_down', 'mix_norm', 'w_in', 'conv_dw', 'conv_dw_b', 'conv_ln_g', 'conv_ln_b', 'conv_pw', 'pool_w', 'pool_scale', 'w_out', 'ffn2_norm', 'ffn2_w_gate', 'ffn2_w_up', 'ffn2_w_down', 'final_norm']
TWIN_WEIGHTS = ['ffn1_norm', 'ffn1_w_gate', 'ffn1_w_up', 'ffn1_w_down', 'mix_norm', 'w_in', 'conv_dw', 'conv_dw_b', 'conv_ln_g', 'conv_ln_b', 'conv_pw', 'pool_w', 'pool_scale', 'w_out', 'ffn2_norm', 'ffn2_w_gate', 'ffn2_w_up', 'ffn2_w_down', 'final_norm']
TWIN_DIFF_INPUT = 'x'
TWIN_INPUTS = ['x', 'ffn1_norm', 'ffn1_w_gate', 'ffn1_w_up', 'ffn1_w_down', 'mix_norm', 'w_in', 'conv_dw', 'conv_dw_b', 'conv_ln_g', 'conv_ln_b', 'conv_pw', 'pool_w', 'pool_scale', 'w_out', 'ffn2_norm', 'ffn2_w_gate', 'ffn2_w_up', 'ffn2_w_down', 'final_norm', 'loss_target', 'm_ffn1_norm', 'm_ffn1_w_gate', 'm_ffn1_w_up', 'm_ffn1_w_down', 'm_mix_norm', 'm_w_in', 'm_conv_dw', 'm_conv_dw_b', 'm_conv_ln_g', 'm_conv_ln_b', 'm_conv_pw', 'm_pool_w', 'm_pool_scale', 'm_w_out', 'm_ffn2_norm', 'm_ffn2_w_gate', 'm_ffn2_w_up', 'm_ffn2_w_down', 'm_final_norm', 'v_ffn1_norm', 'v_ffn1_w_gate', 'v_ffn1_w_up', 'v_ffn1_w_down', 'v_mix_norm', 'v_w_in', 'v_conv_dw', 'v_conv_dw_b', 'v_conv_ln_g', 'v_conv_ln_b', 'v_conv_pw', 'v_pool_w', 'v_pool_scale', 'v_w_out', 'v_ffn2_norm', 'v_ffn2_w_gate', 'v_ffn2_w_up', 'v_ffn2_w_down', 'v_final_norm']
TWIN_OUTPUTS = ['loss', 'grad_x', 'grad_ffn1_norm', 'grad_ffn1_w_gate', 'grad_ffn1_w_up', 'grad_ffn1_w_down', 'grad_mix_norm', 'grad_w_in', 'grad_conv_dw', 'grad_conv_dw_b', 'grad_conv_ln_g', 'grad_conv_ln_b', 'grad_conv_pw', 'grad_pool_w', 'grad_pool_scale', 'grad_w_out', 'grad_ffn2_norm', 'grad_ffn2_w_gate', 'grad_ffn2_w_up', 'grad_ffn2_w_down', 'grad_final_norm', 'delta_ffn1_norm', 'delta_ffn1_w_gate', 'delta_ffn1_w_up', 'delta_ffn1_w_down', 'delta_mix_norm', 'delta_w_in', 'delta_conv_dw', 'delta_conv_dw_b', 'delta_conv_ln_g', 'delta_conv_ln_b', 'delta_conv_pw', 'delta_pool_w', 'delta_pool_scale', 'delta_w_out', 'delta_ffn2_norm', 'delta_ffn2_w_gate', 'delta_ffn2_w_up', 'delta_ffn2_w_down', 'delta_final_norm', 'new_m_ffn1_norm', 'new_m_ffn1_w_gate', 'new_m_ffn1_w_up', 'new_m_ffn1_w_down', 'new_m_mix_norm', 'new_m_w_in', 'new_m_conv_dw', 'new_m_conv_dw_b', 'new_m_conv_ln_g', 'new_m_conv_ln_b', 'new_m_conv_pw', 'new_m_pool_w', 'new_m_pool_scale', 'new_m_w_out', 'new_m_ffn2_norm', 'new_m_ffn2_w_gate', 'new_m_ffn2_w_up', 'new_m_ffn2_w_down', 'new_m_final_norm', 'new_v_ffn1_norm', 'new_v_ffn1_w_gate', 'new_v_ffn1_w_up', 'new_v_ffn1_w_down', 'new_v_mix_norm', 'new_v_w_in', 'new_v_conv_dw', 'new_v_conv_dw_b', 'new_v_conv_ln_g', 'new_v_conv_ln_b', 'new_v_conv_pw', 'new_v_pool_w', 'new_v_pool_scale', 'new_v_w_out', 'new_v_ffn2_norm', 'new_v_ffn2_w_gate', 'new_v_ffn2_w_up', 'new_v_ffn2_w_down', 'new_v_final_norm']
TWIN_LEAF_KINDS = {'loss': 'loss', 'grad_x': 'grad_x', 'grad_ffn1_norm': 'grad_w', 'grad_ffn1_w_gate': 'grad_w', 'grad_ffn1_w_up': 'grad_w', 'grad_ffn1_w_down': 'grad_w', 'grad_mix_norm': 'grad_w', 'grad_w_in': 'grad_w', 'grad_conv_dw': 'grad_w', 'grad_conv_dw_b': 'grad_w', 'grad_conv_ln_g': 'grad_w', 'grad_conv_ln_b': 'grad_w', 'grad_conv_pw': 'grad_w', 'grad_pool_w': 'grad_w', 'grad_pool_scale': 'grad_w', 'grad_w_out': 'grad_w', 'grad_ffn2_norm': 'grad_w', 'grad_ffn2_w_gate': 'grad_w', 'grad_ffn2_w_up': 'grad_w', 'grad_ffn2_w_down': 'grad_w', 'grad_final_norm': 'grad_w', 'delta_ffn1_norm': 'delta_w', 'delta_ffn1_w_gate': 'delta_w', 'delta_ffn1_w_up': 'delta_w', 'delta_ffn1_w_down': 'delta_w', 'delta_mix_norm': 'delta_w', 'delta_w_in': 'delta_w', 'delta_conv_dw': 'delta_w', 'delta_conv_dw_b': 'delta_w', 'delta_conv_ln_g': 'delta_w', 'delta_conv_ln_b': 'delta_w', 'delta_conv_pw': 'delta_w', 'delta_pool_w': 'delta_w', 'delta_pool_scale': 'delta_w', 'delta_w_out': 'delta_w', 'delta_ffn2_norm': 'delta_w', 'delta_ffn2_w_gate': 'delta_w', 'delta_ffn2_w_up': 'delta_w', 'delta_ffn2_w_down': 'delta_w', 'delta_final_norm': 'delta_w', 'new_m_ffn1_norm': 'new_m', 'new_m_ffn1_w_gate': 'new_m', 'new_m_ffn1_w_up': 'new_m', 'new_m_ffn1_w_down': 'new_m', 'new_m_mix_norm': 'new_m', 'new_m_w_in': 'new_m', 'new_m_conv_dw': 'new_m', 'new_m_conv_dw_b': 'new_m', 'new_m_conv_ln_g': 'new_m', 'new_m_conv_ln_b': 'new_m', 'new_m_conv_pw': 'new_m', 'new_m_pool_w': 'new_m', 'new_m_pool_scale': 'new_m', 'new_m_w_out': 'new_m', 'new_m_ffn2_norm': 'new_m', 'new_m_ffn2_w_gate': 'new_m', 'new_m_ffn2_w_up': 'new_m', 'new_m_ffn2_w_down': 'new_m', 'new_m_final_norm': 'new_m', 'new_v_ffn1_norm': 'new_v', 'new_v_ffn1_w_gate': 'new_v', 'new_v_ffn1_w_up': 'new_v', 'new_v_ffn1_w_down': 'new_v', 'new_v_mix_norm': 'new_v', 'new_v_w_in': 'new_v', 'new_v_conv_dw': 'new_v', 'new_v_conv_dw_b': 'new_v', 'new_v_conv_ln_g': 'new_v', 'new_v_conv_ln_b': 'new_v', 'new_v_conv_pw': 'new_v', 'new_v_pool_w': 'new_v', 'new_v_pool_scale': 'new_v', 'new_v_w_out': 'new_v', 'new_v_ffn2_norm': 'new_v', 'new_v_ffn2_w_gate': 'new_v', 'new_v_ffn2_w_up': 'new_v', 'new_v_ffn2_w_down': 'new_v', 'new_v_final_norm': 'new_v'}


def _forward(args):
    return _fwd_reference(*[args[k] for k in FWD_PARAMS])


def _output_shape():
    def fwd():
        inp = _fwd_setup_inputs(0)
        return _fwd_reference(*[inp[k] for k in FWD_PARAMS])
    out = _jax.eval_shape(fwd)
    return out.shape, out.dtype

N_MICROBATCH = 1
ADAM_LR = 0.001
ADAM_B1 = 0.9
ADAM_B2 = 0.999
ADAM_EPS = 1e-08
ADAM_WD = 0.01
ADAM_STEP = 10
PER_EXAMPLE_BATCH_AXIS = {'x': 0, 'loss_target': 0}
SHARED_INPUTS = []
_WEIGHT_DTYPES = {'ffn1_norm': _jnp.float32, 'ffn1_w_gate': _jnp.float32, 'ffn1_w_up': _jnp.float32, 'ffn1_w_down': _jnp.float32, 'mix_norm': _jnp.float32, 'w_in': _jnp.float32, 'conv_dw': _jnp.float32, 'conv_dw_b': _jnp.float32, 'conv_ln_g': _jnp.float32, 'conv_ln_b': _jnp.float32, 'conv_pw': _jnp.float32, 'pool_w': _jnp.float32, 'pool_scale': _jnp.float32, 'w_out': _jnp.float32, 'ffn2_norm': _jnp.float32, 'ffn2_w_gate': _jnp.float32, 'ffn2_w_up': _jnp.float32, 'ffn2_w_down': _jnp.float32, 'final_norm': _jnp.float32}
MOMENT_SCALE = {'ffn1_norm': 1.089035e-01, 'ffn1_w_gate': 4.633851e-02, 'ffn1_w_up': 4.486214e-02, 'ffn1_w_down': 7.455720e-02, 'mix_norm': 1.649507e-01, 'w_in': 1.286758e-01, 'conv_dw': 1.270567e-01, 'conv_dw_b': 2.759836e-01, 'conv_ln_g': 1.518145e-01, 'conv_ln_b': 1.266067e-01, 'conv_pw': 1.219101e-01, 'pool_w': 1.776380e-01, 'pool_scale': 1.773451e-01, 'w_out': 1.527769e-01, 'ffn2_norm': 8.424857e-02, 'ffn2_w_gate': 3.529299e-02, 'ffn2_w_up': 3.437218e-02, 'ffn2_w_down': 5.682175e-02, 'final_norm': 6.401790e+01}


def _to_microbatches(a, axis):
    t = _jnp.moveaxis(a, axis, 0)
    t = t.reshape((N_MICROBATCH, t.shape[0] // N_MICROBATCH) + t.shape[1:])
    return _jnp.moveaxis(t, 1, axis + 1)


def setup_inputs(seed: int = 0) -> dict:
    inp = _fwd_setup_inputs(seed)
    key = _jax.random.fold_in(_jax.random.key(seed), 7919)
    shape, _ = _output_shape()
    out = dict(inp)
    out["loss_target"] = _jax.random.normal(_jax.random.fold_in(key, 0), shape, _jnp.float32)
    for i, name in enumerate(TWIN_WEIGHTS):
        w = inp[name].astype(_jnp.float32)
        if MOMENT_SCALE is None:
            s = _jnp.sqrt(_jnp.mean(_jnp.square(w)) + 1e-30)
        else:
            s = MOMENT_SCALE[name]
        km, kv = _jax.random.split(_jax.random.fold_in(key, i + 1))
        out[name] = w
        out["m_" + name] = s * _jax.random.normal(km, w.shape, _jnp.float32)
        out["v_" + name] = (s * s) * _jax.random.uniform(kv, w.shape, _jnp.float32, 0.5, 1.5)
    if N_MICROBATCH > 1:
        for name, axis in PER_EXAMPLE_BATCH_AXIS.items():
            out[name] = _to_microbatches(out[name], axis)
    return {'x': out['x'], 'ffn1_norm': out['ffn1_norm'], 'ffn1_w_gate': out['ffn1_w_gate'], 'ffn1_w_up': out['ffn1_w_up'], 'ffn1_w_down': out['ffn1_w_down'], 'mix_norm': out['mix_norm'], 'w_in': out['w_in'], 'conv_dw': out['conv_dw'], 'conv_dw_b': out['conv_dw_b'], 'conv_ln_g': out['conv_ln_g'], 'conv_ln_b': out['conv_ln_b'], 'conv_pw': out['conv_pw'], 'pool_w': out['pool_w'], 'pool_scale': out['pool_scale'], 'w_out': out['w_out'], 'ffn2_norm': out['ffn2_norm'], 'ffn2_w_gate': out['ffn2_w_gate'], 'ffn2_w_up': out['ffn2_w_up'], 'ffn2_w_down': out['ffn2_w_down'], 'final_norm': out['final_norm'], 'loss_target': out['loss_target'], 'm_ffn1_norm': out['m_ffn1_norm'], 'm_ffn1_w_gate': out['m_ffn1_w_gate'], 'm_ffn1_w_up': out['m_ffn1_w_up'], 'm_ffn1_w_down': out['m_ffn1_w_down'], 'm_mix_norm': out['m_mix_norm'], 'm_w_in': out['m_w_in'], 'm_conv_dw': out['m_conv_dw'], 'm_conv_dw_b': out['m_conv_dw_b'], 'm_conv_ln_g': out['m_conv_ln_g'], 'm_conv_ln_b': out['m_conv_ln_b'], 'm_conv_pw': out['m_conv_pw'], 'm_pool_w': out['m_pool_w'], 'm_pool_scale': out['m_pool_scale'], 'm_w_out': out['m_w_out'], 'm_ffn2_norm': out['m_ffn2_norm'], 'm_ffn2_w_gate': out['m_ffn2_w_gate'], 'm_ffn2_w_up': out['m_ffn2_w_up'], 'm_ffn2_w_down': out['m_ffn2_w_down'], 'm_final_norm': out['m_final_norm'], 'v_ffn1_norm': out['v_ffn1_norm'], 'v_ffn1_w_gate': out['v_ffn1_w_gate'], 'v_ffn1_w_up': out['v_ffn1_w_up'], 'v_ffn1_w_down': out['v_ffn1_w_down'], 'v_mix_norm': out['v_mix_norm'], 'v_w_in': out['v_w_in'], 'v_conv_dw': out['v_conv_dw'], 'v_conv_dw_b': out['v_conv_dw_b'], 'v_conv_ln_g': out['v_conv_ln_g'], 'v_conv_ln_b': out['v_conv_ln_b'], 'v_conv_pw': out['v_conv_pw'], 'v_pool_w': out['v_pool_w'], 'v_pool_scale': out['v_pool_scale'], 'v_w_out': out['v_w_out'], 'v_ffn2_norm': out['v_ffn2_norm'], 'v_ffn2_w_gate': out['v_ffn2_w_gate'], 'v_ffn2_w_up': out['v_ffn2_w_up'], 'v_ffn2_w_down': out['v_ffn2_w_down'], 'v_final_norm': out['v_final_norm']}


def _loss(weights, diff, rest, loss_target):
    with _jax.named_scope("forward"):
        args = {**rest, TWIN_DIFF_INPUT: diff, **{k: w.astype(_WEIGHT_DTYPES[k]) for k, w in weights.items()}}
        y = _forward(args)
    with _jax.named_scope("loss_head"):
        err = _jnp.square(y.astype(_jnp.float32) - loss_target)
        return 0.5 * _jnp.sum(_jnp.mean(err, axis=-1)) if err.ndim else 0.5 * err


def _adamw(w, g, m, v):
    m = ADAM_B1 * m + (1.0 - ADAM_B1) * g
    v = ADAM_B2 * v + (1.0 - ADAM_B2) * _jnp.square(g)
    m_hat = m / (1.0 - ADAM_B1 ** ADAM_STEP)
    v_hat = v / (1.0 - ADAM_B2 ** ADAM_STEP)
    delta = -ADAM_LR * (m_hat / (_jnp.sqrt(v_hat) + ADAM_EPS) + ADAM_WD * w)
    return delta, m, v


def reference(x, ffn1_norm, ffn1_w_gate, ffn1_w_up, ffn1_w_down, mix_norm, w_in, conv_dw, conv_dw_b, conv_ln_g, conv_ln_b, conv_pw, pool_w, pool_scale, w_out, ffn2_norm, ffn2_w_gate, ffn2_w_up, ffn2_w_down, final_norm, loss_target, m_ffn1_norm, m_ffn1_w_gate, m_ffn1_w_up, m_ffn1_w_down, m_mix_norm, m_w_in, m_conv_dw, m_conv_dw_b, m_conv_ln_g, m_conv_ln_b, m_conv_pw, m_pool_w, m_pool_scale, m_w_out, m_ffn2_norm, m_ffn2_w_gate, m_ffn2_w_up, m_ffn2_w_down, m_final_norm, v_ffn1_norm, v_ffn1_w_gate, v_ffn1_w_up, v_ffn1_w_down, v_mix_norm, v_w_in, v_conv_dw, v_conv_dw_b, v_conv_ln_g, v_conv_ln_b, v_conv_pw, v_pool_w, v_pool_scale, v_w_out, v_ffn2_norm, v_ffn2_w_gate, v_ffn2_w_up, v_ffn2_w_down, v_final_norm):
    given = dict(x=x, ffn1_norm=ffn1_norm, ffn1_w_gate=ffn1_w_gate, ffn1_w_up=ffn1_w_up, ffn1_w_down=ffn1_w_down, mix_norm=mix_norm, w_in=w_in, conv_dw=conv_dw, conv_dw_b=conv_dw_b, conv_ln_g=conv_ln_g, conv_ln_b=conv_ln_b, conv_pw=conv_pw, pool_w=pool_w, pool_scale=pool_scale, w_out=w_out, ffn2_norm=ffn2_norm, ffn2_w_gate=ffn2_w_gate, ffn2_w_up=ffn2_w_up, ffn2_w_down=ffn2_w_down, final_norm=final_norm, loss_target=loss_target, m_ffn1_norm=m_ffn1_norm, m_ffn1_w_gate=m_ffn1_w_gate, m_ffn1_w_up=m_ffn1_w_up, m_ffn1_w_down=m_ffn1_w_down, m_mix_norm=m_mix_norm, m_w_in=m_w_in, m_conv_dw=m_conv_dw, m_conv_dw_b=m_conv_dw_b, m_conv_ln_g=m_conv_ln_g, m_conv_ln_b=m_conv_ln_b, m_conv_pw=m_conv_pw, m_pool_w=m_pool_w, m_pool_scale=m_pool_scale, m_w_out=m_w_out, m_ffn2_norm=m_ffn2_norm, m_ffn2_w_gate=m_ffn2_w_gate, m_ffn2_w_up=m_ffn2_w_up, m_ffn2_w_down=m_ffn2_w_down, m_final_norm=m_final_norm, v_ffn1_norm=v_ffn1_norm, v_ffn1_w_gate=v_ffn1_w_gate, v_ffn1_w_up=v_ffn1_w_up, v_ffn1_w_down=v_ffn1_w_down, v_mix_norm=v_mix_norm, v_w_in=v_w_in, v_conv_dw=v_conv_dw, v_conv_dw_b=v_conv_dw_b, v_conv_ln_g=v_conv_ln_g, v_conv_ln_b=v_conv_ln_b, v_conv_pw=v_conv_pw, v_pool_w=v_pool_w, v_pool_scale=v_pool_scale, v_w_out=v_w_out, v_ffn2_norm=v_ffn2_norm, v_ffn2_w_gate=v_ffn2_w_gate, v_ffn2_w_up=v_ffn2_w_up, v_ffn2_w_down=v_ffn2_w_down, v_final_norm=v_final_norm)
    weights = {n: given[n] for n in TWIN_WEIGHTS}
    shared = {n: given[n] for n in SHARED_INPUTS}
    per_example = {n: given[n] for n in ['x']}
    grad_fn = _jax.value_and_grad(_loss, argnums=(0, 1))

    def one_microbatch(ex, loss_target):
        ex = dict(ex)
        diff = ex.pop(TWIN_DIFF_INPUT)
        return grad_fn(weights, diff, {**shared, **ex}, loss_target)

    if N_MICROBATCH == 1:
        loss, (grad_w, grad_x) = one_microbatch(per_example, given["loss_target"])
    else:
        def body(carry, xs):
            loss_sum, grad_sum = carry
            l_k, (gw_k, gx_k) = one_microbatch(xs[0], xs[1])
            with _jax.named_scope("update"):
                return (loss_sum + l_k, _jax.tree.map(_jnp.add, grad_sum, gw_k)), gx_k

        init = (_jnp.zeros((), _jnp.float32), _jax.tree.map(_jnp.zeros_like, weights))
        (loss, grad_w), grad_x = _jax.lax.scan(body, init, (per_example, given["loss_target"]))
    with _jax.named_scope("update"):
        delta_w, new_m, new_v = {}, {}, {}
        for n in TWIN_WEIGHTS:
            delta_w[n], new_m[n], new_v[n] = _adamw(weights[n], grad_w[n], given["m_" + n], given["v_" + n])
    return (loss, grad_x, *[grad_w[n] for n in TWIN_WEIGHTS], *[delta_w[n] for n in TWIN_WEIGHTS],
            *[new_m[n] for n in TWIN_WEIGHTS], *[new_v[n] for n in TWIN_WEIGHTS])
```

```python
import functools

import jax
import jax.numpy as jnp
from jax import lax
from jax.experimental import pallas as pl
from jax.experimental.pallas import tpu as pltpu

F32 = jnp.float32
BF16 = jnp.bfloat16
MESH = pl.DeviceIdType.MESH

N_CHIPS = 4
N_DEV = 8
D_MODEL = 1024
D_CONV = 512
D_POOL = 512
CONV_WIDTH = 31
POOL_WINDOWS = (2, 4, 8, 16)
POOL_GROUP = 128
D_IN = 2 * D_CONV + D_POOL
HALO = 32
RMS_EPS = 1e-6
LN_EPS = 1e-5
FFN_RES_WEIGHT = 0.5
ADAM_LR = 0.001
ADAM_B1 = 0.9
ADAM_B2 = 0.999
ADAM_EPS = 1e-08
ADAM_WD = 0.01
ADAM_STEP = 10
VMEM_LIMIT_BYTES = 52 * 1024 * 1024
TM_FFN = 512
TM_MIX = 256
TT_WGRAD = 1024


def _dot(a, b):
    return jnp.dot(a, b, preferred_element_type=F32)


def _dot_nt(a, b):
    return lax.dot_general(a, b, (((1,), (1,)), ((), ())), preferred_element_type=F32)


def _dot_tn(a, b):
    return lax.dot_general(a, b, (((0,), (0,)), ((), ())), preferred_element_type=F32)


def _params(*semantics):
    return pltpu.CompilerParams(dimension_semantics=semantics, vmem_limit_bytes=VMEM_LIMIT_BYTES)


def _pcall(body, **kw):
    return pl.pallas_call(body, **kw)


def _sds(shape, dtype):
    return jax.ShapeDtypeStruct(shape, dtype)


def _rms_stats(xv):
    r = lax.rsqrt(jnp.mean(xv * xv, axis=-1, keepdims=True) + RMS_EPS)
    return r, xv * r


def _rms_bwd(dh, n, r, gain):
    dn = dh * gain
    return r * (dn - n * jnp.mean(dn * n, axis=-1, keepdims=True))


def _ffn_fwd(x, gain, wg, wu, wd, name):
    t_len, d = x.shape
    nq, _, fq = wg.shape
    tm = min(TM_FFN, t_len)

    def body(x_ref, g_ref, wg_ref, wu_ref, wd_ref, xo_ref, h_ref, gg_ref, uu_ref, h_s, acc):
        j = pl.program_id(1)

        @pl.when(j == 0)
        def _():
            _, n = _rms_stats(x_ref[...])
            h = (n * g_ref[...]).astype(BF16)
            h_s[...] = h
            h_ref[...] = h
            acc[...] = jnp.zeros_like(acc)

        h = h_s[...]
        gate = _dot(h, wg_ref[...])
        up = _dot(h, wu_ref[...])
        gg_ref[...] = gate.astype(BF16)
        uu_ref[...] = up.astype(BF16)
        act = (gate * jax.nn.sigmoid(gate)) * up
        acc[...] += _dot(act.astype(BF16), wd_ref[...])

        @pl.when(j == nq - 1)
        def _():
            xo_ref[...] = x_ref[...] + FFN_RES_WEIGHT * acc[...]

    tok = pl.BlockSpec((tm, d), lambda i, j: (i, 0))
    hid = pl.BlockSpec((None, tm, fq), lambda i, j: (j, i, 0))
    return _pcall(
        body, name=name, grid=(t_len // tm, nq),
        in_specs=[tok, pl.BlockSpec((1, d), lambda i, j: (0, 0)),
                  pl.BlockSpec((None, d, fq), lambda i, j: (j, 0, 0)),
                  pl.BlockSpec((None, d, fq), lambda i, j: (j, 0, 0)),
                  pl.BlockSpec((None, fq, d), lambda i, j: (j, 0, 0))],
        out_specs=[tok, tok, hid, hid],
        out_shape=[_sds((t_len, d), F32), _sds((t_len, d), BF16),
                   _sds((nq, t_len, fq), BF16), _sds((nq, t_len, fq), BF16)],
        scratch_shapes=[pltpu.VMEM((tm, d), BF16), pltpu.VMEM((tm, d), F32)],
        compiler_params=_params("arbitrary", "arbitrary"),
    )(x, gain, wg, wu, wd)


def _ffn_bwd(dy, x_in, gain, gg, uu, wg, wu, wd, name):
    t_len, d = dy.shape
    nq, _, fq = wg.shape
    tm = min(TM_FFN, t_len)

    def body(dy_ref, x_ref, g_ref, gg_ref, uu_ref, wg_ref, wu_ref, wd_ref,
             dx_ref, dgain_ref, df_ref, dg_ref, du_ref, a_ref, df_s, dh_acc):
        i = pl.program_id(0)
        j = pl.program_id(1)

        @pl.when((i == 0) & (j == 0))
        def _():
            dgain_ref[...] = jnp.zeros_like(dgain_ref)

        @pl.when(j == 0)
        def _():
            df = (FFN_RES_WEIGHT * dy_ref[...]).astype(BF16)
            df_s[...] = df
            df_ref[...] = df
            dh_acc[...] = jnp.zeros_like(dh_acc)

        dact = _dot_nt(df_s[...], wd_ref[...])
        gate = gg_ref[...].astype(F32)
        up = uu_ref[...].astype(F32)
        sig = jax.nn.sigmoid(gate)
        silu = gate * sig
        a_ref[...] = (silu * up).astype(BF16)
        dup = (dact * silu).astype(BF16)
        dgate = (dact * up * (sig * (1.0 + gate * (1.0 - sig)))).astype(BF16)
        dg_ref[...] = dgate
        du_ref[...] = dup
        dh_acc[...] += _dot_nt(dgate, wg_ref[...]) + _dot_nt(dup, wu_ref[...])

        @pl.when(j == nq - 1)
        def _():
            r, n = _rms_stats(x_ref[...])
            dh = dh_acc[...]
            dgain_ref[...] += jnp.sum(dh * n, axis=0, keepdims=True)
            dx_ref[...] = dy_ref[...] + _rms_bwd(dh, n, r, g_ref[...])

    tok = pl.BlockSpec((tm, d), lambda i, j: (i, 0))
    vec = pl.BlockSpec((1, d), lambda i, j: (0, 0))
    hid = pl.BlockSpec((None, tm, fq), lambda i, j: (j, i, 0))
    return _pcall(
        body, name=name, grid=(t_len // tm, nq),
        in_specs=[tok, tok, vec, hid, hid,
                  pl.BlockSpec((None, d, fq), lambda i, j: (j, 0, 0)),
                  pl.BlockSpec((None, d, fq), lambda i, j: (j, 0, 0)),
                  pl.BlockSpec((None, fq, d), lambda i, j: (j, 0, 0))],
        out_specs=[tok, vec, tok, hid, hid, hid],
        out_shape=[_sds((t_len, d), F32), _sds((1, d), F32), _sds((t_len, d), BF16),
                   _sds((nq, t_len, fq), BF16), _sds((nq, t_len, fq), BF16), _sds((nq, t_len, fq), BF16)],
        scratch_shapes=[pltpu.VMEM((tm, d), BF16), pltpu.VMEM((tm, d), F32)],
        compiler_params=_params("arbitrary", "arbitrary"),
    )(dy, x_in, gain, gg, uu, wg, wu, wd)


def _wgrad(lhs, rhs, l_spec, r_spec, out_shape, out_spec, acc_shape, grid, name):
    n_t = grid[-1]
    t_axis = len(grid) - 1

    def body(l_ref, r_ref, o_ref, acc):
        t = pl.program_id(t_axis)

        @pl.when(t == 0)
        def _():
            acc[...] = jnp.zeros_like(acc)

        acc[...] += _dot_tn(l_ref[...].astype(BF16), r_ref[...].astype(BF16))

        @pl.when(t == n_t - 1)
        def _():
            o_ref[...] = acc[...].astype(o_ref.dtype)

    return _pcall(
        body, name=name, grid=grid, in_specs=[l_spec, r_spec], out_specs=out_spec, out_shape=out_shape,
        scratch_shapes=[pltpu.VMEM(acc_shape, F32)],
        compiler_params=_params(*(("arbitrary",) * len(grid))),
    )(lhs, rhs)


def _wgrad_tok_hid(tok, hid, name):
    t_len, d = tok.shape
    nq, _, fq = hid.shape
    tt = min(TT_WGRAD, t_len)
    return _wgrad(tok, hid,
                  pl.BlockSpec((tt, d), lambda q, t: (t, 0)),
                  pl.BlockSpec((None, tt, fq), lambda q, t: (q, t, 0)),
                  _sds((nq, d, fq), BF16), pl.BlockSpec((None, d, fq), lambda q, t: (q, 0, 0)),
                  (d, fq), (nq, t_len // tt), name)


def _wgrad_hid_tok(hid, tok, name):
    t_len, d = tok.shape
    nq, _, fq = hid.shape
    tt = min(TT_WGRAD, t_len)
    return _wgrad(hid, tok,
                  pl.BlockSpec((None, tt, fq), lambda q, t: (q, t, 0)),
                  pl.BlockSpec((tt, d), lambda q, t: (t, 0)),
                  _sds((nq, fq, d), BF16), pl.BlockSpec((None, fq, d), lambda q, t: (q, 0, 0)),
                  (fq, d), (nq, t_len // tt), name)


def _wgrad_2d(lhs, rhs, n_col_blocks, out_dtype, name, group_diag=False):
    t_len, k = lhs.shape
    n = rhs.shape[1]
    nb = n // n_col_blocks
    kb = k // n_col_blocks if group_diag else k
    tt = min(TT_WGRAD, t_len)
    l_map = (lambda q, t: (t, q)) if group_diag else (lambda q, t: (t, 0))
    return _wgrad(lhs, rhs,
                  pl.BlockSpec((tt, kb), l_map),
                  pl.BlockSpec((tt, nb), lambda q, t: (t, q)),
                  _sds((n_col_blocks, kb, nb), out_dtype),
                  pl.BlockSpec((None, kb, nb), lambda q, t: (q, 0, 0)),
                  (kb, nb), (n_col_blocks, t_len // tt), name)


def _mix_in(x1, gain, w_in, name):
    t_len, d = x1.shape
    nq, _, nb = w_in.shape
    tm = min(TM_FFN, t_len)

    def body(x_ref, g_ref, w_ref, h_ref, p_ref):
        _, n = _rms_stats(x_ref[...])
        h = (n * g_ref[...]).astype(BF16)
        h_ref[...] = h
        for q in range(nq):
            p_ref[:, q * nb:(q + 1) * nb] = _dot(h, w_ref[q])

    return _pcall(
        body, name=name, grid=(t_len // tm,),
        in_specs=[pl.BlockSpec((tm, d), lambda i: (i, 0)), pl.BlockSpec((1, d), lambda i: (0, 0)),
                  pl.BlockSpec((nq, d, nb), lambda i: (0, 0, 0))],
        out_specs=[pl.BlockSpec((tm, d), lambda i: (i, 0)), pl.BlockSpec((tm, nq * nb), lambda i: (i, 0))],
        out_shape=[_sds((t_len, d), BF16), _sds((t_len, nq * nb), F32)],
        compiler_params=_params("arbitrary"),
    )(x1, gain, w_in)


def _layernorm_stats(u1):
    mu = jnp.mean(u1, axis=-1, keepdims=True)
    xc = u1 - mu
    rstd = lax.rsqrt(jnp.mean(xc * xc, axis=-1, keepdims=True) + LN_EPS)
    return rstd, xc * rstd


def _positions(i, tm, rows, offset=0):
    return (lax.broadcasted_iota(jnp.int32, (rows, 1), 0) + (i * tm + offset)).astype(F32)


def _mix_fwd(proj, x1, conv_dw, conv_b, ln_g, ln_b, conv_pw, pool_w, pool_scale, w_out, name):
    t_len, d = x1.shape
    tm = min(TM_MIX, t_len)
    hb = tm // HALO

    def body(p_ref, tail_ref, x_ref, dw_ref, cb_ref, lg_ref, lb_ref, pw_ref, plw_ref, ps_ref, wo_ref,
             x2_ref, u1_ref, u3_ref, mx_ref, cat_ref, ext_s, pext_s):
        i = pl.program_id(0)
        first = i == 0
        a = p_ref[:, 0:D_CONV]
        g = p_ref[:, D_CONV:2 * D_CONV]
        p = p_ref[:, 2 * D_CONV:]
        ta = tail_ref[:, 0:D_CONV]
        tg = tail_ref[:, D_CONV:2 * D_CONV]
        tp = tail_ref[:, 2 * D_CONV:]
        ext_s[0:HALO, :] = jnp.where(first, 0.0, ta * jax.nn.sigmoid(tg))
        ext_s[HALO:, :] = a * jax.nn.sigmoid(g)
        pext_s[0:HALO, :] = jnp.where(first, 0.0, tp)
        pext_s[HALO:, :] = p

        u1 = jnp.broadcast_to(cb_ref[...], (tm, D_CONV))
        for k in range(CONV_WIDTH):
            u1 = u1 + dw_ref[k:k + 1, :] * ext_s[pl.ds(HALO - (CONV_WIDTH - 1) + k, tm), :]
        u1_ref[...] = u1
        _, nhat = _layernorm_stats(u1)
        u2 = nhat * lg_ref[...] + lb_ref[...]
        u3 = (u2 * jax.nn.sigmoid(u2)).astype(BF16)
        u3_ref[...] = u3
        cat_ref[:, 0:D_CONV] = _dot(u3, pw_ref[...]).astype(BF16)

        pos1 = _positions(i, tm, tm) + 1.0
        for gi, w in enumerate(POOL_WINDOWS):
            cols = slice(gi * POOL_GROUP, (gi + 1) * POOL_GROUP)
            s = pext_s[pl.ds(HALO, tm), cols]
            for j in range(1, w):
                s = s + pext_s[pl.ds(HALO - j, tm), cols]
            mixed = (s / jnp.minimum(pos1, float(w)) - p[:, cols]).astype(BF16)
            mx_ref[:, cols] = mixed
            out = _dot(mixed, plw_ref[gi]) * ps_ref[:, cols]
            cat_ref[:, D_CONV + gi * POOL_GROUP:D_CONV + (gi + 1) * POOL_GROUP] = out.astype(BF16)

        x2_ref[...] = x_ref[...] + _dot(cat_ref[...], wo_ref[...])

    def tile(cols):
        return pl.BlockSpec((tm, cols), lambda i: (i, 0))

    def whole(shape):
        return pl.BlockSpec(shape, lambda i: (0,) * len(shape))

    return _pcall(
        body, name=name, grid=(t_len // tm,),
        in_specs=[tile(D_IN), pl.BlockSpec((HALO, D_IN), lambda i: (jnp.maximum(i * hb - 1, 0), 0)), tile(d),
                  whole((CONV_WIDTH + 1, D_CONV)), whole((1, D_CONV)), whole((1, D_CONV)), whole((1, D_CONV)),
                  whole((D_CONV, D_CONV)), whole((4, POOL_GROUP, POOL_GROUP)), whole((1, D_POOL)),
                  whole((D_CONV + D_POOL, d))],
        out_specs=[tile(d), tile(D_CONV), tile(D_CONV), tile(D_POOL), tile(D_CONV + D_POOL)],
        out_shape=[_sds((t_len, d), F32), _sds((t_len, D_CONV), F32), _sds((t_len, D_CONV), BF16),
                   _sds((t_len, D_POOL), BF16), _sds((t_len, D_CONV + D_POOL), BF16)],
        scratch_shapes=[pltpu.VMEM((tm + HALO, D_CONV), F32), pltpu.VMEM((tm + HALO, D_POOL), F32)],
        compiler_params=_params("arbitrary"),
    )(proj, proj, x1, conv_dw, conv_b, ln_g, ln_b, conv_pw, pool_w, pool_scale, w_out)


def _mix_bwd_local(dx2, u1, mixed, ln_g, ln_b, conv_pw, pool_w, pool_scale, w_out, name):
    t_len, d = dx2.shape
    tm = min(TM_MIX, t_len)

    def body(dx_ref, u1_ref, mx_ref, lg_ref, lb_ref, pw_ref, plw_ref, ps_ref, wo_ref,
             du1_ref, dmx_ref, dco_ref, dpo_ref, dlg_ref, dlb_ref, dps_ref):
        @pl.when(pl.program_id(0) == 0)
        def _():
            dlg_ref[...] = jnp.zeros_like(dlg_ref)
            dlb_ref[...] = jnp.zeros_like(dlb_ref)
            dps_ref[...] = jnp.zeros_like(dps_ref)

        dcat = _dot_nt(dx_ref[...].astype(BF16), wo_ref[...])
        dco = dcat[:, 0:D_CONV].astype(BF16)
        dco_ref[...] = dco
        du3 = _dot_nt(dco, pw_ref[...])
        rstd, nhat = _layernorm_stats(u1_ref[...])
        u2 = nhat * lg_ref[...] + lb_ref[...]
        sig = jax.nn.sigmoid(u2)
        du2 = du3 * (sig * (1.0 + u2 * (1.0 - sig)))
        dlg_ref[...] += jnp.sum(du2 * nhat, axis=0, keepdims=True)
        dlb_ref[...] += jnp.sum(du2, axis=0, keepdims=True)
        dnhat = du2 * lg_ref[...]
        du1_ref[...] = rstd * (dnhat - jnp.mean(dnhat, axis=-1, keepdims=True)
                               - nhat * jnp.mean(dnhat * nhat, axis=-1, keepdims=True))

        for gi in range(len(POOL_WINDOWS)):
            cols = slice(gi * POOL_GROUP, (gi + 1) * POOL_GROUP)
            dpo = dcat[:, D_CONV + gi * POOL_GROUP:D_CONV + (gi + 1) * POOL_GROUP]
            pre = _dot(mx_ref[:, cols], plw_ref[gi])
            dps_ref[:, cols] += jnp.sum(dpo * pre, axis=0, keepdims=True)
            dout = (dpo * ps_ref[:, cols]).astype(BF16)
            dpo_ref[:, cols] = dout
            dmx_ref[:, cols] = _dot_nt(dout, plw_ref[gi])

    def tile(cols):
        return pl.BlockSpec((tm, cols), lambda i: (i, 0))

    def whole(shape):
        return pl.BlockSpec(shape, lambda i: (0,) * len(shape))

    vec = whole((1, D_CONV))
    return _pcall(
        body, name=name, grid=(t_len // tm,),
        in_specs=[tile(d), tile(D_CONV), tile(D_POOL), vec, vec, whole((D_CONV, D_CONV)),
                  whole((4, POOL_GROUP, POOL_GROUP)), vec, whole((D_CONV + D_POOL, d))],
        out_specs=[tile(D_CONV), tile(D_POOL), tile(D_CONV), tile(D_POOL), vec, vec, vec],
        out_shape=[_sds((t_len, D_CONV), F32), _sds((t_len, D_POOL), F32), _sds((t_len, D_CONV), BF16),
                   _sds((t_len, D_POOL), BF16), _sds((1, D_CONV), F32), _sds((1, D_CONV), F32),
                   _sds((1, D_POOL), F32)],
        compiler_params=_params("arbitrary"),
    )(dx2, u1, mixed, ln_g, ln_b, conv_pw, pool_w, pool_scale, w_out)


def _mix_bwd_seq(du1, dmixed, proj, x1, dx2, gain, conv_dw, w_in, name):
    t_len, d = x1.shape
    nq, _, nb = w_in.shape
    tm = min(TM_MIX, t_len)
    hb = tm // HALO
    last_block = t_len // HALO - 1
    n_tiles = t_len // tm

    def body(du_ref, dun_ref, dm_ref, dmn_ref, p_ref, tail_ref, x_ref, dx2_ref, g_ref, dw_ref, wi_ref,
             dx1_ref, dp_ref, ddw_ref, dcb_ref, dgain_ref, uext_s, dext_s, mext_s):
        i = pl.program_id(0)
        first = i == 0
        last = i == n_tiles - 1

        @pl.when(first)
        def _():
            ddw_ref[...] = jnp.zeros_like(ddw_ref)
            dcb_ref[...] = jnp.zeros_like(dcb_ref)
            dgain_ref[...] = jnp.zeros_like(dgain_ref)

        a = p_ref[:, 0:D_CONV]
        g = p_ref[:, D_CONV:2 * D_CONV]
        sg = jax.nn.sigmoid(g)
        ta = tail_ref[:, 0:D_CONV]
        tg = tail_ref[:, D_CONV:2 * D_CONV]
        uext_s[0:HALO, :] = jnp.where(first, 0.0, ta * jax.nn.sigmoid(tg))
        uext_s[HALO:, :] = a * sg
        du1 = du_ref[...]
        dext_s[0:tm, :] = du1
        dext_s[tm:, :] = jnp.where(last, 0.0, dun_ref[...])

        du0 = jnp.zeros((tm, D_CONV), F32)
        for k in range(CONV_WIDTH):
            du0 = du0 + dw_ref[k:k + 1, :] * dext_s[pl.ds(CONV_WIDTH - 1 - k, tm), :]
            ddw_ref[k:k + 1, :] += jnp.sum(
                du1 * uext_s[pl.ds(HALO - (CONV_WIDTH - 1) + k, tm), :], axis=0, keepdims=True)
        dcb_ref[...] += jnp.sum(du1, axis=0, keepdims=True)
        dp_ref[:, 0:D_CONV] = (du0 * sg).astype(BF16)
        dp_ref[:, D_CONV:2 * D_CONV] = (du0 * a * sg * (1.0 - sg)).astype(BF16)

        pos1 = _positions(i, tm, tm) + 1.0
        pos1_next = _positions(i, tm, HALO, offset=tm) + 1.0
        for gi, w in enumerate(POOL_WINDOWS):
            cols = slice(gi * POOL_GROUP, (gi + 1) * POOL_GROUP)
            dm = dm_ref[:, cols]
            mext_s[0:tm, cols] = dm / jnp.minimum(pos1, float(w))
            mext_s[tm:, cols] = jnp.where(last, 0.0, dmn_ref[:, cols] / jnp.minimum(pos1_next, float(w)))
            s = mext_s[pl.ds(0, tm), cols]
            for j in range(1, w):
                s = s + mext_s[pl.ds(j, tm), cols]
            dp_ref[:, 2 * D_CONV + gi * POOL_GROUP:2 * D_CONV + (gi + 1) * POOL_GROUP] = (s - dm).astype(BF16)

        dh = _dot_nt(dp_ref[:, 0:nb], wi_ref[0])
        for q in range(1, nq):
            dh = dh + _dot_nt(dp_ref[:, q * nb:(q + 1) * nb], wi_ref[q])
        r, n = _rms_stats(x_ref[...])
        dgain_ref[...] += jnp.sum(dh * n, axis=0, keepdims=True)
        dx1_ref[...] = dx2_ref[...] + _rms_bwd(dh, n, r, g_ref[...])

    def tile(cols):
        return pl.BlockSpec((tm, cols), lambda i: (i, 0))

    def nxt(cols):
        return pl.BlockSpec((HALO, cols), lambda i: (jnp.minimum((i + 1) * hb, last_block), 0))

    def whole(shape):
        return pl.BlockSpec(shape, lambda i: (0,) * len(shape))

    return _pcall(
        body, name=name, grid=(n_tiles,),
        in_specs=[tile(D_CONV), nxt(D_CONV), tile(D_POOL), nxt(D_POOL), tile(D_IN),
                  pl.BlockSpec((HALO, D_IN), lambda i: (jnp.maximum(i * hb - 1, 0), 0)),
                  tile(d), tile(d), whole((1, d)), whole((CONV_WIDTH + 1, D_CONV)), whole((nq, d, nb))],
        out_specs=[tile(d), tile(D_IN), whole((CONV_WIDTH + 1, D_CONV)), whole((1, D_CONV)), whole((1, d))],
        out_shape=[_sds((t_len, d), F32), _sds((t_len, D_IN), BF16), _sds((CONV_WIDTH + 1, D_CONV), F32),
                   _sds((1, D_CONV), F32), _sds((1, d), F32)],
        scratch_shapes=[pltpu.VMEM((tm + HALO, D_CONV), F32), pltpu.VMEM((tm + HALO, D_CONV), F32),
                        pltpu.VMEM((tm + HALO, D_POOL), F32)],
        compiler_params=_params("arbitrary"),
    )(du1, du1, dmixed, dmixed, proj, proj, x1, dx2, gain, conv_dw, w_in)


def _final_norm_loss(x3, target, gain, name):
    t_len, d = x3.shape
    tm = min(TM_FFN, t_len)

    def body(x_ref, t_ref, g_ref, dx_ref, loss_ref, dgain_ref):
        @pl.when(pl.program_id(0) == 0)
        def _():
            loss_ref[...] = jnp.zeros_like(loss_ref)
            dgain_ref[...] = jnp.zeros_like(dgain_ref)

        r, n = _rms_stats(x_ref[...])
        err = n * g_ref[...] - t_ref[...]
        per_tok = jnp.sum(err * err, axis=-1, keepdims=True) * (1.0 / d)
        loss_ref[...] += 0.5 * jnp.sum(per_tok, axis=0, keepdims=True)
        dy = err * (1.0 / d)
        dgain_ref[...] += jnp.sum(dy * n, axis=0, keepdims=True)
        dx_ref[...] = _rms_bwd(dy, n, r, g_ref[...])

    tok = pl.BlockSpec((tm, d), lambda i: (i, 0))
    return _pcall(
        body, name=name, grid=(t_len // tm,),
        in_specs=[tok, tok, pl.BlockSpec((1, d), lambda i: (0, 0))],
        out_specs=[tok, pl.BlockSpec((1, 128), lambda i: (0, 0)), pl.BlockSpec((1, d), lambda i: (0, 0))],
        out_shape=[_sds((t_len, d), F32), _sds((1, 128), F32), _sds((1, d), F32)],
        compiler_params=_params("arbitrary"),
    )(x3, target, gain)


def _local_step(x, target, w):
    x1, h1, gg1, uu1 = _ffn_fwd(x, w["ffn1_norm"], w["ffn1_w_gate"], w["ffn1_w_up"], w["ffn1_w_down"], "ffn1_fwd")
    h2, proj = _mix_in(x1, w["mix_norm"], w["w_in"], "mix_in")
    x2, u1, u3, mixed, cat = _mix_fwd(proj, x1, w["conv_dw"], w["conv_dw_b"], w["conv_ln_g"], w["conv_ln_b"],
                                      w["conv_pw"], w["pool_w"], w["pool_scale"], w["w_out"], "mix_fwd")
    x3, h3, gg2, uu2 = _ffn_fwd(x2, w["ffn2_norm"], w["ffn2_w_gate"], w["ffn2_w_up"], w["ffn2_w_down"], "ffn2_fwd")
    dx3, loss, d_final = _final_norm_loss(x3, target, w["final_norm"], "final_norm_loss")

    g = {"final_norm": d_final}
    dx2, g["ffn2_norm"], df2, dg2, du2, a2 = _ffn_bwd(dx3, x2, w["ffn2_norm"], gg2, uu2, w["ffn2_w_gate"],
                                                       w["ffn2_w_up"], w["ffn2_w_down"], "ffn2_bwd")
    g["ffn2_w_gate"] = _wgrad_tok_hid(h3, dg2, "ffn2_dw_gate")
    g["ffn2_w_up"] = _wgrad_tok_hid(h3, du2, "ffn2_dw_up")
    g["ffn2_w_down"] = _wgrad_hid_tok(a2, df2, "ffn2_dw_down")

    du1, dmixed, dco, dpo, g["conv_ln_g"], g["conv_ln_b"], g["pool_scale"] = _mix_bwd_local(
        dx2, u1, mixed, w["conv_ln_g"], w["conv_ln_b"], w["conv_pw"], w["pool_w"], w["pool_scale"], w["w_out"],
        "mix_bwd_local")
    dx1, dproj, g["conv_dw"], g["conv_dw_b"], g["mix_norm"] = _mix_bwd_seq(
        du1, dmixed, proj, x1, dx2, w["mix_norm"], w["conv_dw"], w["w_in"], "mix_bwd_seq")
    g["w_out"] = _wgrad_2d(cat, dx2, 1, BF16, "dw_out")
    g["conv_pw"] = _wgrad_2d(u3, dco, 1, BF16, "dconv_pw")
    g["pool_w"] = _wgrad_2d(mixed, dpo, 4, F32, "dpool_w", group_diag=True)
    g["w_in"] = _wgrad_2d(h2, dproj, N_CHIPS, BF16, "dw_in")

    dx, g["ffn1_norm"], df1, dg1, du1_, a1 = _ffn_bwd(dx1, x, w["ffn1_norm"], gg1, uu1, w["ffn1_w_gate"],
                                                       w["ffn1_w_up"], w["ffn1_w_down"], "ffn1_bwd")
    g["ffn1_w_gate"] = _wgrad_tok_hid(h1, dg1, "ffn1_dw_gate")
    g["ffn1_w_up"] = _wgrad_tok_hid(h1, du1_, "ffn1_dw_up")
    g["ffn1_w_down"] = _wgrad_hid_tok(a1, df1, "ffn1_dw_down")
    return loss, dx, g


HBM = pl.BlockSpec(memory_space=pl.ANY)


def _place():
    x, y, c = lax.axis_index("x"), lax.axis_index("y"), lax.axis_index("c")
    return x, y, c, [(1 - x, y), (x, 1 - y), (1 - x, 1 - y)]


def _gather_over_chips(shards, name):
    n = len(shards)

    def body(*refs):
        ins, outs = refs[:n], refs[n:2 * n]
        send_sems, recv_sems, own_sems = refs[2 * n:]
        x, y, c, chips = _place()
        q = 2 * x + y
        own = []
        for k in range(n):
            cp = pltpu.make_async_copy(ins[k], outs[k].at[q], own_sems.at[k])
            cp.start()
            own.append(cp)
            for j, (px, py) in enumerate(chips):
                pltpu.make_async_remote_copy(
                    src_ref=ins[k], dst_ref=outs[k].at[q], send_sem=send_sems.at[3 * k + j],
                    recv_sem=recv_sems.at[3 * k + j], device_id=(px, py, c), device_id_type=MESH).start()
        for k in range(n):
            for j, (px, py) in enumerate(chips):
                pltpu.make_async_remote_copy(
                    src_ref=ins[k], dst_ref=outs[k].at[2 * px + py], send_sem=send_sems.at[3 * k + j],
                    recv_sem=recv_sems.at[3 * k + j], device_id=(px, py, c), device_id_type=MESH).wait()
            own[k].wait()

    return _pcall(
        body, name=name, in_specs=[HBM] * n, out_specs=[HBM] * n,
        out_shape=[_sds((N_CHIPS,) + s.shape, s.dtype) for s in shards],
        scratch_shapes=[pltpu.SemaphoreType.DMA((3 * n,)), pltpu.SemaphoreType.DMA((3 * n,)),
                        pltpu.SemaphoreType.DMA((n,))],
    )(*shards)


def _scatter_over_chips(slabs, name):
    n = len(slabs)

    def body(*refs):
        ins, outs = refs[:n], refs[n:2 * n]
        send_sems, recv_sems, own_sems = refs[2 * n:]
        x, y, c, chips = _place()
        q = 2 * x + y
        own = []
        for k in range(n):
            cp = pltpu.make_async_copy(ins[k].at[q], outs[k].at[q], own_sems.at[k])
            cp.start()
            own.append(cp)
            for j, (px, py) in enumerate(chips):
                pltpu.make_async_remote_copy(
                    src_ref=ins[k].at[2 * px + py], dst_ref=outs[k].at[q], send_sem=send_sems.at[3 * k + j],
                    recv_sem=recv_sems.at[3 * k + j], device_id=(px, py, c), device_id_type=MESH).start()
        for k in range(n):
            for j, (px, py) in enumerate(chips):
                pltpu.make_async_remote_copy(
                    src_ref=ins[k].at[2 * px + py], dst_ref=outs[k].at[2 * px + py],
                    send_sem=send_sems.at[3 * k + j], recv_sem=recv_sems.at[3 * k + j],
                    device_id=(px, py, c), device_id_type=MESH).wait()
            own[k].wait()

    return _pcall(
        body, name=name, in_specs=[HBM] * n, out_specs=[HBM] * n,
        out_shape=[_sds(s.shape, s.dtype) for s in slabs],
        scratch_shapes=[pltpu.SemaphoreType.DMA((3 * n,)), pltpu.SemaphoreType.DMA((3 * n,)),
                        pltpu.SemaphoreType.DMA((n,))],
    )(*slabs)


def _swap_with_sibling(arrays, name):
    n = len(arrays)

    def body(*refs):
        ins, outs = refs[:n], refs[n:2 * n]
        send_sems, recv_sems = refs[2 * n:]
        x, y, c, _ = _place()
        copies = [pltpu.make_async_remote_copy(
            src_ref=ins[k], dst_ref=outs[k], send_sem=send_sems.at[k], recv_sem=recv_sems.at[k],
            device_id=(x, y, 1 - c), device_id_type=MESH) for k in range(n)]
        for cp in copies:
            cp.start()
        for cp in copies:
            cp.wait()

    return _pcall(
        body, name=name, in_specs=[HBM] * n, out_specs=[HBM] * n,
        out_shape=[_sds(a.shape, a.dtype) for a in arrays],
        scratch_shapes=[pltpu.SemaphoreType.DMA((n,)), pltpu.SemaphoreType.DMA((n,))],
    )(*arrays)


def _gather_over_devices(block, name):
    def body(in_ref, out_ref, send_sems, recv_sems, own_sem):
        x, y, c, _ = _place()
        me = 4 * x + 2 * y + c
        own = pltpu.make_async_copy(in_ref, out_ref.at[me], own_sem)
        own.start()
        peers = [(x ^ (k >> 2 & 1), y ^ (k >> 1 & 1), c ^ (k & 1)) for k in range(1, N_DEV)]
        for k, (px, py, pc) in enumerate(peers):
            pltpu.make_async_remote_copy(
                src_ref=in_ref, dst_ref=out_ref.at[me], send_sem=send_sems.at[k], recv_sem=recv_sems.at[k],
                device_id=(px, py, pc), device_id_type=MESH).start()
        for k, (px, py, pc) in enumerate(peers):
            pltpu.make_async_remote_copy(
                src_ref=in_ref, dst_ref=out_ref.at[4 * px + 2 * py + pc], send_sem=send_sems.at[k],
                recv_sem=recv_sems.at[k], device_id=(px, py, pc), device_id_type=MESH).wait()
        own.wait()

    return _pcall(
        body, name=name, in_specs=[HBM], out_specs=HBM,
        out_shape=_sds((N_DEV,) + block.shape, block.dtype),
        scratch_shapes=[pltpu.SemaphoreType.DMA((N_DEV - 1,)), pltpu.SemaphoreType.DMA((N_DEV - 1,)),
                        pltpu.SemaphoreType.DMA(())],
    )(block)


def _row_tile(rows):
    return rows // 4 if rows % 64 == 0 else rows


def _sum_parts(parts, name):
    n, r, c = parts.shape
    tr = _row_tile(r)

    def body(p_ref, o_ref):
        s = p_ref[0].astype(F32)
        for k in range(1, n):
            s = s + p_ref[k].astype(F32)
        o_ref[...] = s

    return _pcall(
        body, name=name, grid=(r // tr,),
        in_specs=[pl.BlockSpec((n, tr, c), lambda i: (0, i, 0))],
        out_specs=pl.BlockSpec((tr, c), lambda i: (i, 0)),
        out_shape=_sds((r, c), F32),
        compiler_params=_params("arbitrary"),
    )(parts)


def _adamw_math(w, g, m, v):
    m = ADAM_B1 * m + (1.0 - ADAM_B1) * g
    v = ADAM_B2 * v + (1.0 - ADAM_B2) * (g * g)
    m_hat = m / (1.0 - ADAM_B1 ** ADAM_STEP)
    v_hat = v / (1.0 - ADAM_B2 ** ADAM_STEP)
    delta = -ADAM_LR * (m_hat / (jnp.sqrt(v_hat) + ADAM_EPS) + ADAM_WD * w)
    return delta, m, v


def _adamw(parts, w, m, v, name):
    r, c = w.shape
    n = len(parts)
    tr = _row_tile(r)

    def body(*refs):
        g = refs[0][...]
        for k in range(1, n):
            g = g + refs[k][...]
        w_ref, m_ref, v_ref, g_out, d_out, m_out, v_out = refs[n:]
        delta, nm, nv = _adamw_math(w_ref[...], g, m_ref[...], v_ref[...])
        g_out[...] = g
        d_out[...] = delta
        m_out[...] = nm
        v_out[...] = nv

    blk = pl.BlockSpec((tr, c), lambda i: (i, 0))
    return _pcall(
        body, name=name, grid=(r // tr,), in_specs=[blk] * (n + 3), out_specs=[blk] * 4,
        out_shape=[_sds((r, c), F32)] * 4,
        compiler_params=_params("arbitrary"),
    )(*parts, w, m, v)


def _adamw_gathered(parts, w, m, v, name):
    n, r, c = parts.shape

    def body(p_ref, w_ref, m_ref, v_ref, g_out, d_out, m_out, v_out):
        g = p_ref[0]
        for k in range(1, n):
            g = g + p_ref[k]
        delta, nm, nv = _adamw_math(w_ref[...], g, m_ref[...], v_ref[...])
        g_out[...] = g
        d_out[...] = delta
        m_out[...] = nm
        v_out[...] = nv

    return _pcall(body, name=name, out_shape=[_sds((r, c), F32)] * 4,
                  compiler_params=pltpu.CompilerParams(vmem_limit_bytes=VMEM_LIMIT_BYTES))(parts, w, m, v)


BIG = ("ffn1_w_gate", "ffn1_w_up", "ffn1_w_down", "w_in", "conv_dw", "conv_pw", "w_out",
       "ffn2_w_gate", "ffn2_w_up", "ffn2_w_down")
SMALL_1024 = ("ffn1_norm", "mix_norm", "ffn2_norm", "final_norm")
SMALL_512 = ("conv_dw_b", "conv_ln_g", "conv_ln_b", "pool_scale")
WEIGHTS = ("ffn1_norm", "ffn1_w_gate", "ffn1_w_up", "ffn1_w_down", "mix_norm", "w_in", "conv_dw", "conv_dw_b",
           "conv_ln_g", "conv_ln_b", "conv_pw", "pool_w", "pool_scale", "w_out", "ffn2_norm", "ffn2_w_gate",
           "ffn2_w_up", "ffn2_w_down", "final_norm")
PACK_ROWS = 72


def _pad_rows(a, rows):
    return jnp.pad(a, ((0, rows - a.shape[0]), (0, 0)))


def _pack_small(t):
    rows = [t[k].reshape(1, D_MODEL) for k in SMALL_1024]
    rows.append(jnp.concatenate([t["conv_dw_b"].reshape(1, -1), t["conv_ln_g"].reshape(1, -1)], axis=1))
    rows.append(jnp.concatenate([t["conv_ln_b"].reshape(1, -1), t["pool_scale"].reshape(1, -1)], axis=1))
    rows.append(t["pool_w"].reshape(64, D_MODEL))
    return _pad_rows(jnp.concatenate(rows, axis=0), PACK_ROWS)


def _unpack_small(p):
    out = {k: p[i] for i, k in enumerate(SMALL_1024)}
    out["conv_dw_b"], out["conv_ln_g"] = p[4, :D_CONV], p[4, D_CONV:]
    out["conv_ln_b"], out["pool_scale"] = p[5, :D_CONV], p[5, D_CONV:]
    out["pool_w"] = p[6:70].reshape(4, POOL_GROUP, POOL_GROUP)
    return out


def kernel(x, ffn1_norm, ffn1_w_gate, ffn1_w_up, ffn1_w_down, mix_norm, w_in, conv_dw, conv_dw_b, conv_ln_g, conv_ln_b, conv_pw, pool_w, pool_scale, w_out, ffn2_norm, ffn2_w_gate, ffn2_w_up, ffn2_w_down, final_norm, loss_target, m_ffn1_norm, m_ffn1_w_gate, m_ffn1_w_up, m_ffn1_w_down, m_mix_norm, m_w_in, m_conv_dw, m_conv_dw_b, m_conv_ln_g, m_conv_ln_b, m_conv_pw, m_pool_w, m_pool_scale, m_w_out, m_ffn2_norm, m_ffn2_w_gate, m_ffn2_w_up, m_ffn2_w_down, m_final_norm, v_ffn1_norm, v_ffn1_w_gate, v_ffn1_w_up, v_ffn1_w_down, v_mix_norm, v_w_in, v_conv_dw, v_conv_dw_b, v_conv_ln_g, v_conv_ln_b, v_conv_pw, v_pool_w, v_pool_scale, v_w_out, v_ffn2_norm, v_ffn2_w_gate, v_ffn2_w_up, v_ffn2_w_down, v_final_norm):
    given = dict(locals())
    wts = {k: given[k] for k in WEIGHTS}
    mom_m = {k: given["m_" + k] for k in WEIGHTS}
    mom_v = {k: given["v_" + k] for k in WEIGHTS}
    pad_dw = functools.partial(_pad_rows, rows=CONV_WIDTH + 1)

    shards = [pad_dw(wts[k]) if k == "conv_dw" else wts[k].astype(BF16) for k in BIG]
    full = dict(zip(BIG, _gather_over_chips(shards, "gather_weights")))
    w = {k: wts[k].reshape(1, -1) for k in SMALL_1024 + SMALL_512}
    w["pool_w"] = wts["pool_w"].astype(BF16)
    for k in ("ffn1_w_gate", "ffn1_w_up", "ffn1_w_down", "ffn2_w_gate", "ffn2_w_up", "ffn2_w_down", "w_in"):
        w[k] = full[k]
    w["conv_dw"] = full["conv_dw"].transpose(1, 0, 2).reshape(CONV_WIDTH + 1, D_CONV)
    w["conv_pw"] = full["conv_pw"].reshape(D_CONV, D_CONV)
    w["w_out"] = full["w_out"].reshape(D_CONV + D_POOL, D_MODEL)

    loss, grad_x, g = _local_step(x[0], loss_target[0], w)
    loss = lax.psum(loss[0, 0], ("x", "y", "c"))

    slabs = dict(g)
    slabs["conv_dw"] = g["conv_dw"].reshape(CONV_WIDTH + 1, N_CHIPS, D_CONV // N_CHIPS).transpose(1, 0, 2)
    slabs["conv_pw"] = g["conv_pw"].reshape(N_CHIPS, D_CONV // N_CHIPS, D_CONV)
    slabs["w_out"] = g["w_out"].reshape(N_CHIPS, (D_CONV + D_POOL) // N_CHIPS, D_MODEL)
    parts = _scatter_over_chips([slabs[k] for k in BIG], "scatter_grads")
    mine = [_sum_parts(p, "sum_chips_" + k) for k, p in zip(BIG, parts)]
    theirs = _swap_with_sibling(mine, "swap_sums")
    grads, deltas, new_m, new_v = {}, {}, {}, {}
    for k, a, b in zip(BIG, mine, theirs):
        fix = pad_dw if k == "conv_dw" else (lambda t: t)
        res = _adamw([a, b], fix(wts[k]), fix(mom_m[k]), fix(mom_v[k]), "adamw_" + k)
        grads[k], deltas[k], new_m[k], new_v[k] = [t[:wts[k].shape[0]] for t in res]

    small = dict(g)
    small["pool_w"] = g["pool_w"]
    gathered = _gather_over_devices(_pack_small(small), "gather_small_grads")
    res = _adamw_gathered(gathered, _pack_small(wts), _pack_small(mom_m), _pack_small(mom_v), "adamw_small")
    for dst, packed in zip((grads, deltas, new_m, new_v), res):
        dst.update(_unpack_small(packed))

    out = [loss, grad_x[None]]
    for group in (grads, deltas, new_m, new_v):
        out += [group[k] for k in WEIGHTS]
    return tuple(out)
```

```python
import functools

import jax
import jax.numpy as jnp
from jax import lax
from jax.experimental import pallas as pl
from jax.experimental.pallas import tpu as pltpu

F32 = jnp.float32
BF16 = jnp.bfloat16
MESH = pl.DeviceIdType.MESH

N_CHIPS = 4
N_DEV = 8
D_MODEL = 1024
D_CONV = 512
D_POOL = 512
CONV_WIDTH = 31
POOL_WINDOWS = (2, 4, 8, 16)
POOL_GROUP = 128
D_IN = 2 * D_CONV + D_POOL
HALO = 32
RMS_EPS = 1e-6
LN_EPS = 1e-5
FFN_RES_WEIGHT = 0.5
ADAM_LR = 0.001
ADAM_B1 = 0.9
ADAM_B2 = 0.999
ADAM_EPS = 1e-08
ADAM_WD = 0.01
ADAM_STEP = 10
VMEM_LIMIT_BYTES = 52 * 1024 * 1024
TM_FFN = 512
TM_MIX = 256
TT_WGRAD = 1024

HBM = pl.BlockSpec(memory_space=pl.ANY)


def _dot(a, b):
    return jnp.dot(a, b, preferred_element_type=F32)


def _dot_nt(a, b):
    return lax.dot_general(a, b, (((1,), (1,)), ((), ())), preferred_element_type=F32)


def _dot_tn(a, b):
    return lax.dot_general(a, b, (((0,), (0,)), ((), ())), preferred_element_type=F32)


def _sds(shape, dtype):
    return jax.ShapeDtypeStruct(shape, dtype)


def _rms_stats(xv):
    r = lax.rsqrt(jnp.mean(xv * xv, axis=-1, keepdims=True) + RMS_EPS)
    return r, xv * r


def _rms_bwd(dh, n, r, gain):
    dn = dh * gain
    return r * (dn - n * jnp.mean(dn * n, axis=-1, keepdims=True))


def _place():
    x, y, c = lax.axis_index("x"), lax.axis_index("y"), lax.axis_index("c")
    return x, y, c, [(1 - x, y), (x, 1 - y), (1 - x, 1 - y)]


class Cargo:
    def __init__(self, kind, arrays):
        self.kind, self.arrays = kind, list(arrays)
        n = len(self.arrays)
        if kind in ("gather_slots", "gather_chips"):
            self.out_shape = [_sds((N_CHIPS,) + a.shape, a.dtype) for a in self.arrays]
        elif kind == "gather_devices":
            self.out_shape = [_sds((N_DEV,) + a.shape, a.dtype) for a in self.arrays]
        else:
            self.out_shape = [_sds(a.shape, a.dtype) for a in self.arrays]
        per = {"swap": 1, "gather_devices": N_DEV - 1}.get(kind, N_CHIPS - 1)
        self.n_remote = per * n
        self.n_own = 0 if kind == "swap" else n
        self.scratch = [pltpu.SemaphoreType.DMA((self.n_remote,)), pltpu.SemaphoreType.DMA((self.n_remote,)),
                        pltpu.SemaphoreType.DMA((max(self.n_own, 1),))]

    def _plan(self, ins, outs):
        x, y, c, chips = _place()
        q = 2 * x + y
        own, remote = [], []
        for a, o in zip(ins, outs):
            if self.kind == "gather_slots":
                own.append((a, o.at[0]))
                remote += [(a, o.at[j + 1], o.at[j + 1], (px, py, c)) for j, (px, py) in enumerate(chips)]
            elif self.kind == "gather_chips":
                own.append((a, o.at[q]))
                remote += [(a, o.at[q], o.at[2 * px + py], (px, py, c)) for px, py in chips]
            elif self.kind == "scatter_slots":
                own.append((a.at[0], o.at[0]))
                remote += [(a.at[j + 1], o.at[j + 1], o.at[j + 1], (px, py, c)) for j, (px, py) in enumerate(chips)]
            elif self.kind == "scatter_chips":
                own.append((a.at[q], o.at[q]))
                remote += [(a.at[2 * px + py], o.at[q], o.at[2 * px + py], (px, py, c)) for px, py in chips]
            elif self.kind == "swap":
                remote.append((a, o, o, (x, y, 1 - c)))
            else:
                own.append((a, o.at[4 * x + 2 * y + c]))
                for k in range(1, N_DEV):
                    px, py, pc = x ^ (k >> 2 & 1), y ^ (k >> 1 & 1), c ^ (k & 1)
                    remote.append((a, o.at[4 * x + 2 * y + c], o.at[4 * px + 2 * py + pc], (px, py, pc)))
        return own, remote

    def start(self, ins, outs, sems):
        send_sems, recv_sems, own_sems = sems
        own, remote = self._plan(ins, outs)
        for k, (src, dst) in enumerate(own):
            pltpu.make_async_copy(src, dst, own_sems.at[k]).start()
        for k, (src, dst, _, peer) in enumerate(remote):
            pltpu.make_async_remote_copy(src_ref=src, dst_ref=dst, send_sem=send_sems.at[k], recv_sem=recv_sems.at[k],
                                         device_id=peer, device_id_type=MESH).start()

    def wait(self, ins, outs, sems):
        send_sems, recv_sems, own_sems = sems
        own, remote = self._plan(ins, outs)
        for k, (src, _, landed, peer) in enumerate(remote):
            pltpu.make_async_remote_copy(src_ref=src, dst_ref=landed, send_sem=send_sems.at[k],
                                         recv_sem=recv_sems.at[k], device_id=peer, device_id_type=MESH).wait()
        for k, (src, dst) in enumerate(own):
            pltpu.make_async_copy(src, dst, own_sems.at[k]).wait()


def _call(body, *, name, grid, in_specs, out_specs, out_shape, args, scratch_shapes=(), cargos=()):
    n_in, n_out, n_scr = len(in_specs), len(out_specs), len(scratch_shapes)
    c_in = [len(cg.arrays) for cg in cargos]
    n_cin = sum(c_in)

    def wrapped(*refs):
        ins = refs[:n_in]
        cins = refs[n_in:n_in + n_cin]
        outs = refs[n_in + n_cin:n_in + n_cin + n_out]
        couts = refs[n_in + n_cin + n_out:n_in + 2 * n_cin + n_out]
        scr = refs[n_in + 2 * n_cin + n_out:n_in + 2 * n_cin + n_out + n_scr]
        sems = refs[n_in + 2 * n_cin + n_out + n_scr:]
        first, last = True, True
        for ax, size in enumerate(grid):
            first = first & (pl.program_id(ax) == 0)
            last = last & (pl.program_id(ax) == size - 1)

        def each(method):
            at = 0
            for k, cg in enumerate(cargos):
                getattr(cg, method)(cins[at:at + c_in[k]], couts[at:at + c_in[k]], sems[3 * k:3 * k + 3])
                at += c_in[k]

        if cargos:
            pl.when(first)(lambda: each("start"))
        body(*ins, *outs, *scr)
        if cargos:
            pl.when(last)(lambda: each("wait"))

    res = pl.pallas_call(
        wrapped, name=name, grid=grid,
        in_specs=list(in_specs) + [HBM] * n_cin,
        out_specs=list(out_specs) + [HBM] * n_cin,
        out_shape=list(out_shape) + [s for cg in cargos for s in cg.out_shape],
        scratch_shapes=list(scratch_shapes) + [s for cg in cargos for s in cg.scratch],
        compiler_params=pltpu.CompilerParams(dimension_semantics=("arbitrary",) * len(grid),
                                             vmem_limit_bytes=VMEM_LIMIT_BYTES),
    )(*args, *[a for cg in cargos for a in cg.arrays])
    outs, rest = list(res[:n_out]), list(res[n_out:])
    cargo_outs = []
    for k in c_in:
        cargo_outs.append(rest[:k])
        rest = rest[k:]
    return outs, cargo_outs


def _exchange(cargo, name):
    _, (outs,) = _call(lambda: None, name=name, grid=(1,), in_specs=[], out_specs=[], out_shape=[], args=[],
                       cargos=[cargo])
    return outs


def _ffn_up_gather(x, gain, wg_t, wu_t, name, cargos=()):
    t_len, d = x.shape
    fq = wg_t.shape[0]
    tm = min(TM_FFN, t_len)
    n_tiles = t_len // tm

    def body(x_ref, g_ref, wg_in, wu_in, h_ref, gg_ref, uu_ref, wg_all, wu_all,
             wg_v, wu_v, send_sems, recv_sems, own_sems, load_sems):
        s = pl.program_id(0)
        i = pl.program_id(1)
        x_, y_, c_, chips = _place()
        shards = ((wg_in, wg_all, wg_v), (wu_in, wu_all, wu_v))

        def to_peer(k, j):
            w_in, w_all, _ = shards[k]
            return pltpu.make_async_remote_copy(
                src_ref=w_in, dst_ref=w_all.at[j + 1], send_sem=send_sems.at[3 * k + j],
                recv_sem=recv_sems.at[3 * k + j], device_id=(*chips[j], c_), device_id_type=MESH)

        def keep(k):
            return pltpu.make_async_copy(shards[k][0], shards[k][1].at[0], own_sems.at[k])

        @pl.when((s == 0) & (i == 0))
        def _():
            for k in range(2):
                for j in range(N_CHIPS - 1):
                    to_peer(k, j).start()
                keep(k).start()

        for slot in range(N_CHIPS):
            @pl.when((s == slot) & (i == 0))
            def _():
                loads = []
                for k in range(2):
                    if slot > 0:
                        to_peer(k, slot - 1).wait_recv()
                    src = shards[k][0] if slot == 0 else shards[k][1].at[slot]
                    loads.append(pltpu.make_async_copy(src, shards[k][2], load_sems.at[k]))
                    loads[-1].start()
                for ld in loads:
                    ld.wait()

        _, n = _rms_stats(x_ref[...])
        h = (n * g_ref[...]).astype(BF16)

        @pl.when(s == 0)
        def _():
            h_ref[...] = h

        gg_ref[...] = _dot_nt(h, wg_v[...]).astype(BF16)
        uu_ref[...] = _dot_nt(h, wu_v[...]).astype(BF16)

        @pl.when((s == N_CHIPS - 1) & (i == n_tiles - 1))
        def _():
            for k in range(2):
                for j in range(N_CHIPS - 1):
                    to_peer(k, j).wait_send()
                keep(k).wait()

    tok = pl.BlockSpec((tm, d), lambda s, i: (i, 0))
    h_out = pl.BlockSpec((tm, d), lambda s, i: (jnp.where(s == 0, i, n_tiles - 1), 0))
    hid = pl.BlockSpec((None, tm, fq), lambda s, i: (s, i, 0))
    outs, cargo_outs = _call(
        body, name=name, grid=(N_CHIPS, n_tiles),
        in_specs=[tok, pl.BlockSpec((1, d), lambda s, i: (0, 0)), HBM, HBM],
        out_specs=[h_out, hid, hid, HBM, HBM],
        out_shape=[_sds((t_len, d), BF16), _sds((N_CHIPS, t_len, fq), BF16), _sds((N_CHIPS, t_len, fq), BF16),
                   _sds((N_CHIPS, fq, d), BF16), _sds((N_CHIPS, fq, d), BF16)],
        scratch_shapes=[pltpu.VMEM((fq, d), BF16), pltpu.VMEM((fq, d), BF16),
                        pltpu.SemaphoreType.DMA((6,)), pltpu.SemaphoreType.DMA((6,)),
                        pltpu.SemaphoreType.DMA((2,)), pltpu.SemaphoreType.DMA((2,))],
        args=[x, gain, wg_t, wu_t], cargos=cargos)
    return outs, cargo_outs


def _ffn_down(x, gg, uu, wd, name, cargos=()):
    t_len, d = x.shape
    nq, fq, _ = wd.shape
    tm = min(TM_FFN, t_len)

    def body(x_ref, gg_ref, uu_ref, wd_ref, xo_ref, acc):
        j = pl.program_id(1)

        @pl.when(j == 0)
        def _():
            acc[...] = jnp.zeros_like(acc)

        gate = gg_ref[...].astype(F32)
        act = (gate * jax.nn.sigmoid(gate)) * uu_ref[...].astype(F32)
        acc[...] += _dot(act.astype(BF16), wd_ref[...])

        @pl.when(j == nq - 1)
        def _():
            xo_ref[...] = x_ref[...] + FFN_RES_WEIGHT * acc[...]

    tok = pl.BlockSpec((tm, d), lambda i, j: (i, 0))
    hid = pl.BlockSpec((None, tm, fq), lambda i, j: (j, i, 0))
    (xo,), cargo_outs = _call(
        body, name=name, grid=(t_len // tm, nq),
        in_specs=[tok, hid, hid, pl.BlockSpec((None, fq, d), lambda i, j: (j, 0, 0))],
        out_specs=[tok], out_shape=[_sds((t_len, d), F32)],
        scratch_shapes=[pltpu.VMEM((tm, d), F32)], args=[x, gg, uu, wd], cargos=cargos)
    return xo, cargo_outs


def _ffn_fwd(x, gain, wg_t, wu_t, wd, name):
    t_len, d = x.shape
    nq, fq, _ = wd.shape
    tm = min(TM_FFN, t_len)

    def body(x_ref, g_ref, wg_ref, wu_ref, wd_ref, xo_ref, h_ref, gg_ref, uu_ref, h_s, acc):
        j = pl.program_id(1)

        @pl.when(j == 0)
        def _():
            _, n = _rms_stats(x_ref[...])
            h = (n * g_ref[...]).astype(BF16)
            h_s[...] = h
            h_ref[...] = h
            acc[...] = jnp.zeros_like(acc)

        h = h_s[...]
        gate = _dot_nt(h, wg_ref[...])
        up = _dot_nt(h, wu_ref[...])
        gg_ref[...] = gate.astype(BF16)
        uu_ref[...] = up.astype(BF16)
        act = (gate * jax.nn.sigmoid(gate)) * up
        acc[...] += _dot(act.astype(BF16), wd_ref[...])

        @pl.when(j == nq - 1)
        def _():
            xo_ref[...] = x_ref[...] + FFN_RES_WEIGHT * acc[...]

    tok = pl.BlockSpec((tm, d), lambda i, j: (i, 0))
    hid = pl.BlockSpec((None, tm, fq), lambda i, j: (j, i, 0))
    wgt = pl.BlockSpec((None, fq, d), lambda i, j: (j, 0, 0))
    outs, _ = _call(
        body, name=name, grid=(t_len // tm, nq),
        in_specs=[tok, pl.BlockSpec((1, d), lambda i, j: (0, 0)), wgt, wgt, wgt],
        out_specs=[tok, tok, hid, hid],
        out_shape=[_sds((t_len, d), F32), _sds((t_len, d), BF16),
                   _sds((nq, t_len, fq), BF16), _sds((nq, t_len, fq), BF16)],
        scratch_shapes=[pltpu.VMEM((tm, d), BF16), pltpu.VMEM((tm, d), F32)],
        args=[x, gain, wg_t, wu_t, wd])
    return outs


def _ffn_bwd(dy, x_in, gain, gg, uu, wg_t, wu_t, wd, name):
    t_len, d = dy.shape
    nq, fq, _ = wd.shape
    tm = min(TM_FFN, t_len)

    def body(dy_ref, x_ref, g_ref, gg_ref, uu_ref, wg_ref, wu_ref, wd_ref,
             dx_ref, dgain_ref, df_ref, dg_ref, du_ref, a_ref, df_s, dh_acc):
        i = pl.program_id(0)
        j = pl.program_id(1)

        @pl.when((i == 0) & (j == 0))
        def _():
            dgain_ref[...] = jnp.zeros_like(dgain_ref)

        @pl.when(j == 0)
        def _():
            df = (FFN_RES_WEIGHT * dy_ref[...]).astype(BF16)
            df_s[...] = df
            df_ref[...] = df
            dh_acc[...] = jnp.zeros_like(dh_acc)

        dact = _dot_nt(df_s[...], wd_ref[...])
        gate = gg_ref[...].astype(F32)
        up = uu_ref[...].astype(F32)
        sig = jax.nn.sigmoid(gate)
        silu = gate * sig
        a_ref[...] = (silu * up).astype(BF16)
        dup = (dact * silu).astype(BF16)
        dgate = (dact * up * (sig * (1.0 + gate * (1.0 - sig)))).astype(BF16)
        dg_ref[...] = dgate
        du_ref[...] = dup
        dh_acc[...] += _dot(dgate, wg_ref[...]) + _dot(dup, wu_ref[...])

        @pl.when(j == nq - 1)
        def _():
            r, n = _rms_stats(x_ref[...])
            dh = dh_acc[...]
            dgain_ref[...] += jnp.sum(dh * n, axis=0, keepdims=True)
            dx_ref[...] = dy_ref[...] + _rms_bwd(dh, n, r, g_ref[...])

    tok = pl.BlockSpec((tm, d), lambda i, j: (i, 0))
    vec = pl.BlockSpec((1, d), lambda i, j: (0, 0))
    hid = pl.BlockSpec((None, tm, fq), lambda i, j: (j, i, 0))
    wgt = pl.BlockSpec((None, fq, d), lambda i, j: (j, 0, 0))
    outs, _ = _call(
        body, name=name, grid=(t_len // tm, nq),
        in_specs=[tok, tok, vec, hid, hid, wgt, wgt, wgt],
        out_specs=[tok, vec, tok, hid, hid, hid],
        out_shape=[_sds((t_len, d), F32), _sds((1, d), F32), _sds((t_len, d), BF16),
                   _sds((nq, t_len, fq), BF16), _sds((nq, t_len, fq), BF16), _sds((nq, t_len, fq), BF16)],
        scratch_shapes=[pltpu.VMEM((tm, d), BF16), pltpu.VMEM((tm, d), F32)],
        args=[dy, x_in, gain, gg, uu, wg_t, wu_t, wd])
    return outs


def _wgrad(lhs, rhs, l_spec, r_spec, out_shape, out_spec, acc_shape, grid, name, cargos=()):
    n_t = grid[-1]
    t_axis = len(grid) - 1

    def body(l_ref, r_ref, o_ref, acc):
        t = pl.program_id(t_axis)

        @pl.when(t == 0)
        def _():
            acc[...] = jnp.zeros_like(acc)

        acc[...] += _dot_tn(l_ref[...].astype(BF16), r_ref[...].astype(BF16))

        @pl.when(t == n_t - 1)
        def _():
            o_ref[...] = acc[...].astype(o_ref.dtype)

    (out,), cargo_outs = _call(
        body, name=name, grid=grid, in_specs=[l_spec, r_spec], out_specs=[out_spec], out_shape=[out_shape],
        scratch_shapes=[pltpu.VMEM(acc_shape, F32)], args=[lhs, rhs], cargos=cargos)
    return out, cargo_outs


def _wgrad_hid_tok(hid, tok, name, cargos=()):
    t_len, d = tok.shape
    nq, _, fq = hid.shape
    tt = min(TT_WGRAD, t_len)
    return _wgrad(hid, tok,
                  pl.BlockSpec((None, tt, fq), lambda q, t: (q, t, 0)),
                  pl.BlockSpec((tt, d), lambda q, t: (t, 0)),
                  _sds((nq, fq, d), BF16), pl.BlockSpec((None, fq, d), lambda q, t: (q, 0, 0)),
                  (fq, d), (nq, t_len // tt), name, cargos)


def _wgrad_2d(lhs, rhs, n_col_blocks, out_dtype, name, group_diag=False, cargos=()):
    t_len, k = lhs.shape
    n = rhs.shape[1]
    nb = n // n_col_blocks
    kb = k // n_col_blocks if group_diag else k
    tt = min(TT_WGRAD, t_len)
    l_map = (lambda q, t: (t, q)) if group_diag else (lambda q, t: (t, 0))
    return _wgrad(lhs, rhs,
                  pl.BlockSpec((tt, kb), l_map),
                  pl.BlockSpec((tt, nb), lambda q, t: (t, q)),
                  _sds((n_col_blocks, kb, nb), out_dtype),
                  pl.BlockSpec((None, kb, nb), lambda q, t: (q, 0, 0)),
                  (kb, nb), (n_col_blocks, t_len // tt), name, cargos)


def _mix_in(x1, gain, w_in, name, cargos=()):
    t_len, d = x1.shape
    nq, _, nb = w_in.shape
    tm = min(TM_FFN, t_len)

    def body(x_ref, g_ref, w_ref, h_ref, p_ref):
        _, n = _rms_stats(x_ref[...])
        h = (n * g_ref[...]).astype(BF16)
        h_ref[...] = h
        for q in range(nq):
            p_ref[:, q * nb:(q + 1) * nb] = _dot(h, w_ref[q])

    return _call(
        body, name=name, grid=(t_len // tm,),
        in_specs=[pl.BlockSpec((tm, d), lambda i: (i, 0)), pl.BlockSpec((1, d), lambda i: (0, 0)),
                  pl.BlockSpec((nq, d, nb), lambda i: (0, 0, 0))],
        out_specs=[pl.BlockSpec((tm, d), lambda i: (i, 0)), pl.BlockSpec((tm, nq * nb), lambda i: (i, 0))],
        out_shape=[_sds((t_len, d), BF16), _sds((t_len, nq * nb), F32)],
        args=[x1, gain, w_in], cargos=cargos)


def _layernorm_stats(u1):
    mu = jnp.mean(u1, axis=-1, keepdims=True)
    xc = u1 - mu
    rstd = lax.rsqrt(jnp.mean(xc * xc, axis=-1, keepdims=True) + LN_EPS)
    return rstd, xc * rstd


def _positions(i, tm, rows, offset=0):
    return (lax.broadcasted_iota(jnp.int32, (rows, 1), 0) + (i * tm + offset)).astype(F32)


def _tile(tm, cols):
    return pl.BlockSpec((tm, cols), lambda i: (i, 0))


def _whole(shape):
    return pl.BlockSpec(shape, lambda i: (0,) * len(shape))


def _mix_fwd(proj, x1, conv_dw, conv_b, ln_g, ln_b, conv_pw, pool_w, pool_scale, w_out, name, cargos=()):
    t_len, d = x1.shape
    tm = min(TM_MIX, t_len)
    hb = tm // HALO

    def body(p_ref, tail_ref, x_ref, dw_ref, cb_ref, lg_ref, lb_ref, pw_ref, plw_ref, ps_ref, wo_ref,
             x2_ref, u1_ref, u3_ref, mx_ref, cat_ref, ext_s, pext_s):
        i = pl.program_id(0)
        first = i == 0
        a = p_ref[:, 0:D_CONV]
        g = p_ref[:, D_CONV:2 * D_CONV]
        p = p_ref[:, 2 * D_CONV:]
        ta = tail_ref[:, 0:D_CONV]
        tg = tail_ref[:, D_CONV:2 * D_CONV]
        tp = tail_ref[:, 2 * D_CONV:]
        ext_s[0:HALO, :] = jnp.where(first, 0.0, ta * jax.nn.sigmoid(tg))
        ext_s[HALO:, :] = a * jax.nn.sigmoid(g)
        pext_s[0:HALO, :] = jnp.where(first, 0.0, tp)
        pext_s[HALO:, :] = p

        u1 = jnp.broadcast_to(cb_ref[...], (tm, D_CONV))
        for k in range(CONV_WIDTH):
            u1 = u1 + dw_ref[k:k + 1, :] * ext_s[pl.ds(HALO - (CONV_WIDTH - 1) + k, tm), :]
        u1_ref[...] = u1
        _, nhat = _layernorm_stats(u1)
        u2 = nhat * lg_ref[...] + lb_ref[...]
        u3 = (u2 * jax.nn.sigmoid(u2)).astype(BF16)
        u3_ref[...] = u3
        cat_ref[:, 0:D_CONV] = _dot(u3, pw_ref[...]).astype(BF16)

        pos1 = _positions(i, tm, tm) + 1.0
        for gi, w in enumerate(POOL_WINDOWS):
            cols = slice(gi * POOL_GROUP, (gi + 1) * POOL_GROUP)
            s = pext_s[pl.ds(HALO, tm), cols]
            for j in range(1, w):
                s = s + pext_s[pl.ds(HALO - j, tm), cols]
            mixed = (s / jnp.minimum(pos1, float(w)) - p[:, cols]).astype(BF16)
            mx_ref[:, cols] = mixed
            out = _dot(mixed, plw_ref[gi]) * ps_ref[:, cols]
            cat_ref[:, D_CONV + gi * POOL_GROUP:D_CONV + (gi + 1) * POOL_GROUP] = out.astype(BF16)

        x2_ref[...] = x_ref[...] + _dot(cat_ref[...], wo_ref[...])

    return _call(
        body, name=name, grid=(t_len // tm,),
        in_specs=[_tile(tm, D_IN), pl.BlockSpec((HALO, D_IN), lambda i: (jnp.maximum(i * hb - 1, 0), 0)),
                  _tile(tm, d), _whole((CONV_WIDTH + 1, D_CONV)), _whole((1, D_CONV)), _whole((1, D_CONV)),
                  _whole((1, D_CONV)), _whole((D_CONV, D_CONV)), _whole((4, POOL_GROUP, POOL_GROUP)),
                  _whole((1, D_POOL)), _whole((D_CONV + D_POOL, d))],
        out_specs=[_tile(tm, d), _tile(tm, D_CONV), _tile(tm, D_CONV), _tile(tm, D_POOL),
                   _tile(tm, D_CONV + D_POOL)],
        out_shape=[_sds((t_len, d), F32), _sds((t_len, D_CONV), F32), _sds((t_len, D_CONV), BF16),
                   _sds((t_len, D_POOL), BF16), _sds((t_len, D_CONV + D_POOL), BF16)],
        scratch_shapes=[pltpu.VMEM((tm + HALO, D_CONV), F32), pltpu.VMEM((tm + HALO, D_POOL), F32)],
        args=[proj, proj, x1, conv_dw, conv_b, ln_g, ln_b, conv_pw, pool_w, pool_scale, w_out], cargos=cargos)


def _mix_bwd_local(dx2, u1, mixed, ln_g, ln_b, conv_pw, pool_w, pool_scale, w_out, name, cargos=()):
    t_len, d = dx2.shape
    tm = min(TM_MIX, t_len)

    def body(dx_ref, u1_ref, mx_ref, lg_ref, lb_ref, pw_ref, plw_ref, ps_ref, wo_ref,
             du1_ref, dmx_ref, dco_ref, dpo_ref, dlg_ref, dlb_ref, dps_ref):
        @pl.when(pl.program_id(0) == 0)
        def _():
            dlg_ref[...] = jnp.zeros_like(dlg_ref)
            dlb_ref[...] = jnp.zeros_like(dlb_ref)
            dps_ref[...] = jnp.zeros_like(dps_ref)

        dcat = _dot_nt(dx_ref[...].astype(BF16), wo_ref[...])
        dco = dcat[:, 0:D_CONV].astype(BF16)
        dco_ref[...] = dco
        du3 = _dot_nt(dco, pw_ref[...])
        rstd, nhat = _layernorm_stats(u1_ref[...])
        u2 = nhat * lg_ref[...] + lb_ref[...]
        sig = jax.nn.sigmoid(u2)
        du2 = du3 * (sig * (1.0 + u2 * (1.0 - sig)))
        dlg_ref[...] += jnp.sum(du2 * nhat, axis=0, keepdims=True)
        dlb_ref[...] += jnp.sum(du2, axis=0, keepdims=True)
        dnhat = du2 * lg_ref[...]
        du1_ref[...] = rstd * (dnhat - jnp.mean(dnhat, axis=-1, keepdims=True)
                               - nhat * jnp.mean(dnhat * nhat, axis=-1, keepdims=True))

        for gi in range(len(POOL_WINDOWS)):
            cols = slice(gi * POOL_GROUP, (gi + 1) * POOL_GROUP)
            dpo = dcat[:, D_CONV + gi * POOL_GROUP:D_CONV + (gi + 1) * POOL_GROUP]
            pre = _dot(mx_ref[:, cols], plw_ref[gi])
            dps_ref[:, cols] += jnp.sum(dpo * pre, axis=0, keepdims=True)
            dout = (dpo * ps_ref[:, cols]).astype(BF16)
            dpo_ref[:, cols] = dout
            dmx_ref[:, cols] = _dot_nt(dout, plw_ref[gi])

    vec = _whole((1, D_CONV))
    return _call(
        body, name=name, grid=(t_len // tm,),
        in_specs=[_tile(tm, d), _tile(tm, D_CONV), _tile(tm, D_POOL), vec, vec, _whole((D_CONV, D_CONV)),
                  _whole((4, POOL_GROUP, POOL_GROUP)), vec, _whole((D_CONV + D_POOL, d))],
        out_specs=[_tile(tm, D_CONV), _tile(tm, D_POOL), _tile(tm, D_CONV), _tile(tm, D_POOL), vec, vec, vec],
        out_shape=[_sds((t_len, D_CONV), F32), _sds((t_len, D_POOL), F32), _sds((t_len, D_CONV), BF16),
                   _sds((t_len, D_POOL), BF16), _sds((1, D_CONV), F32), _sds((1, D_CONV), F32),
                   _sds((1, D_POOL), F32)],
        args=[dx2, u1, mixed, ln_g, ln_b, conv_pw, pool_w, pool_scale, w_out], cargos=cargos)


def _mix_bwd_seq(du1, dmixed, proj, x1, dx2, gain, conv_dw, w_in, name, cargos=()):
    t_len, d = x1.shape
    nq, _, nb = w_in.shape
    tm = min(TM_MIX, t_len)
    hb = tm // HALO
    last_block = t_len // HALO - 1
    n_tiles = t_len // tm

    def body(du_ref, dun_ref, dm_ref, dmn_ref, p_ref, tail_ref, x_ref, dx2_ref, g_ref, dw_ref, wi_ref,
             dx1_ref, dp_ref, ddw_ref, dcb_ref, dgain_ref, uext_s, dext_s, mext_s):
        i = pl.program_id(0)
        first = i == 0
        last = i == n_tiles - 1

        @pl.when(first)
        def _():
            ddw_ref[...] = jnp.zeros_like(ddw_ref)
            dcb_ref[...] = jnp.zeros_like(dcb_ref)
            dgain_ref[...] = jnp.zeros_like(dgain_ref)

        a = p_ref[:, 0:D_CONV]
        g = p_ref[:, D_CONV:2 * D_CONV]
        sg = jax.nn.sigmoid(g)
        ta = tail_ref[:, 0:D_CONV]
        tg = tail_ref[:, D_CONV:2 * D_CONV]
        uext_s[0:HALO, :] = jnp.where(first, 0.0, ta * jax.nn.sigmoid(tg))
        uext_s[HALO:, :] = a * sg
        du1 = du_ref[...]
        dext_s[0:tm, :] = du1
        dext_s[tm:, :] = jnp.where(last, 0.0, dun_ref[...])

        du0 = jnp.zeros((tm, D_CONV), F32)
        for k in range(CONV_WIDTH):
            du0 = du0 + dw_ref[k:k + 1, :] * dext_s[pl.ds(CONV_WIDTH - 1 - k, tm), :]
            ddw_ref[k:k + 1, :] += jnp.sum(
                du1 * uext_s[pl.ds(HALO - (CONV_WIDTH - 1) + k, tm), :], axis=0, keepdims=True)
        dcb_ref[...] += jnp.sum(du1, axis=0, keepdims=True)
        dp_ref[:, 0:D_CONV] = (du0 * sg).astype(BF16)
        dp_ref[:, D_CONV:2 * D_CONV] = (du0 * a * sg * (1.0 - sg)).astype(BF16)

        pos1 = _positions(i, tm, tm) + 1.0
        pos1_next = _positions(i, tm, HALO, offset=tm) + 1.0
        for gi, w in enumerate(POOL_WINDOWS):
            cols = slice(gi * POOL_GROUP, (gi + 1) * POOL_GROUP)
            dm = dm_ref[:, cols]
            mext_s[0:tm, cols] = dm / jnp.minimum(pos1, float(w))
            mext_s[tm:, cols] = jnp.where(last, 0.0, dmn_ref[:, cols] / jnp.minimum(pos1_next, float(w)))
            s = mext_s[pl.ds(0, tm), cols]
            for j in range(1, w):
                s = s + mext_s[pl.ds(j, tm), cols]
            dp_ref[:, 2 * D_CONV + gi * POOL_GROUP:2 * D_CONV + (gi + 1) * POOL_GROUP] = (s - dm).astype(BF16)

        dh = _dot_nt(dp_ref[:, 0:nb], wi_ref[0])
        for q in range(1, nq):
            dh = dh + _dot_nt(dp_ref[:, q * nb:(q + 1) * nb], wi_ref[q])
        r, n = _rms_stats(x_ref[...])
        dgain_ref[...] += jnp.sum(dh * n, axis=0, keepdims=True)
        dx1_ref[...] = dx2_ref[...] + _rms_bwd(dh, n, r, g_ref[...])

    def nxt(cols):
        return pl.BlockSpec((HALO, cols), lambda i: (jnp.minimum((i + 1) * hb, last_block), 0))

    return _call(
        body, name=name, grid=(n_tiles,),
        in_specs=[_tile(tm, D_CONV), nxt(D_CONV), _tile(tm, D_POOL), nxt(D_POOL), _tile(tm, D_IN),
                  pl.BlockSpec((HALO, D_IN), lambda i: (jnp.maximum(i * hb - 1, 0), 0)),
                  _tile(tm, d), _tile(tm, d), _whole((1, d)), _whole((CONV_WIDTH + 1, D_CONV)),
                  _whole((nq, d, nb))],
        out_specs=[_tile(tm, d), _tile(tm, D_IN), _whole((CONV_WIDTH + 1, D_CONV)), _whole((1, D_CONV)),
                   _whole((1, d))],
        out_shape=[_sds((t_len, d), F32), _sds((t_len, D_IN), BF16), _sds((CONV_WIDTH + 1, D_CONV), F32),
                   _sds((1, D_CONV), F32), _sds((1, d), F32)],
        scratch_shapes=[pltpu.VMEM((tm + HALO, D_CONV), F32), pltpu.VMEM((tm + HALO, D_CONV), F32),
                        pltpu.VMEM((tm + HALO, D_POOL), F32)],
        args=[du1, du1, dmixed, dmixed, proj, proj, x1, dx2, gain, conv_dw, w_in], cargos=cargos)


def _final_norm_loss(x3, target, gain, name):
    t_len, d = x3.shape
    tm = min(TM_FFN, t_len)

    def body(x_ref, t_ref, g_ref, dx_ref, loss_ref, dgain_ref):
        @pl.when(pl.program_id(0) == 0)
        def _():
            loss_ref[...] = jnp.zeros_like(loss_ref)
            dgain_ref[...] = jnp.zeros_like(dgain_ref)

        r, n = _rms_stats(x_ref[...])
        err = n * g_ref[...] - t_ref[...]
        per_tok = jnp.sum(err * err, axis=-1, keepdims=True) * (1.0 / d)
        loss_ref[...] += 0.5 * jnp.sum(per_tok, axis=0, keepdims=True)
        dy = err * (1.0 / d)
        dgain_ref[...] += jnp.sum(dy * n, axis=0, keepdims=True)
        dx_ref[...] = _rms_bwd(dy, n, r, g_ref[...])

    tok = pl.BlockSpec((tm, d), lambda i: (i, 0))
    outs, _ = _call(
        body, name=name, grid=(t_len // tm,),
        in_specs=[tok, tok, pl.BlockSpec((1, d), lambda i: (0, 0))],
        out_specs=[tok, pl.BlockSpec((1, 128), lambda i: (0, 0)), pl.BlockSpec((1, d), lambda i: (0, 0))],
        out_shape=[_sds((t_len, d), F32), _sds((1, 128), F32), _sds((1, d), F32)],
        args=[x3, target, gain])
    return outs


def _row_tile(rows):
    return rows // 4 if rows % 64 == 0 else rows


def _sum_parts(parts, name):
    n, r, c = parts.shape
    tr = _row_tile(r)

    def body(p_ref, o_ref):
        s = p_ref[0].astype(F32)
        for k in range(1, n):
            s = s + p_ref[k].astype(F32)
        o_ref[...] = s

    (out,), _ = _call(body, name=name, grid=(r // tr,),
                      in_specs=[pl.BlockSpec((n, tr, c), lambda i: (0, i, 0))],
                      out_specs=[pl.BlockSpec((tr, c), lambda i: (i, 0))], out_shape=[_sds((r, c), F32)],
                      args=[parts])
    return out


def _adamw_math(w, g, m, v):
    m = ADAM_B1 * m + (1.0 - ADAM_B1) * g
    v = ADAM_B2 * v + (1.0 - ADAM_B2) * (g * g)
    m_hat = m / (1.0 - ADAM_B1 ** ADAM_STEP)
    v_hat = v / (1.0 - ADAM_B2 ** ADAM_STEP)
    delta = -ADAM_LR * (m_hat / (jnp.sqrt(v_hat) + ADAM_EPS) + ADAM_WD * w)
    return delta, m, v


def _adamw(parts, w, m, v, name):
    r, c = w.shape
    n = len(parts)
    tr = _row_tile(r)

    def body(*refs):
        terms = []
        for p_ref in refs[:n]:
            terms += [p_ref[...]] if len(p_ref.shape) == 2 else [p_ref[k] for k in range(p_ref.shape[0])]
        w_ref, m_ref, v_ref, g_out, d_out, m_out, v_out = refs[n:]
        g = terms[0]
        for t in terms[1:]:
            g = g + t
        delta, nm, nv = _adamw_math(w_ref[...], g, m_ref[...], v_ref[...])
        g_out[...] = g
        d_out[...] = delta
        m_out[...] = nm
        v_out[...] = nv

    blk = pl.BlockSpec((tr, c), lambda i: (i, 0))
    p_specs = [blk if p.ndim == 2 else pl.BlockSpec((p.shape[0], tr, c), lambda i: (0, i, 0)) for p in parts]
    outs, _ = _call(body, name=name, grid=(r // tr,), in_specs=p_specs + [blk, blk, blk],
                    out_specs=[blk] * 4, out_shape=[_sds((r, c), F32)] * 4, args=[*parts, w, m, v])
    return outs


FFN_W = ("w_gate", "w_up", "w_down")
MID = ("w_in", "conv_dw", "conv_pw", "w_out")
SMALL_1024 = ("ffn1_norm", "mix_norm", "ffn2_norm", "final_norm")
SMALL_512 = ("conv_dw_b", "conv_ln_g", "conv_ln_b", "pool_scale")
WEIGHTS = ("ffn1_norm", "ffn1_w_gate", "ffn1_w_up", "ffn1_w_down", "mix_norm", "w_in", "conv_dw", "conv_dw_b",
           "conv_ln_g", "conv_ln_b", "conv_pw", "pool_w", "pool_scale", "w_out", "ffn2_norm", "ffn2_w_gate",
           "ffn2_w_up", "ffn2_w_down", "final_norm")
PACK_ROWS = 72


def _pad_rows(a, rows):
    return jnp.pad(a, ((0, rows - a.shape[0]), (0, 0)))


def _pack_small(t):
    rows = [t[k].reshape(1, D_MODEL) for k in SMALL_1024]
    rows.append(jnp.concatenate([t["conv_dw_b"].reshape(1, -1), t["conv_ln_g"].reshape(1, -1)], axis=1))
    rows.append(jnp.concatenate([t["conv_ln_b"].reshape(1, -1), t["pool_scale"].reshape(1, -1)], axis=1))
    rows.append(t["pool_w"].reshape(64, D_MODEL))
    return _pad_rows(jnp.concatenate(rows, axis=0), PACK_ROWS)


def _unpack_small(p):
    out = {k: p[i] for i, k in enumerate(SMALL_1024)}
    out["conv_dw_b"], out["conv_ln_g"] = p[4, :D_CONV], p[4, D_CONV:]
    out["conv_ln_b"], out["pool_scale"] = p[5, :D_CONV], p[5, D_CONV:]
    out["pool_w"] = p[6:70].reshape(4, POOL_GROUP, POOL_GROUP)
    return out


def _as_stored(name, a):
    if name.endswith(("w_gate", "w_up")):
        return a.T
    if name == "conv_dw":
        return _pad_rows(a, CONV_WIDTH + 1)
    return a


def _as_given(name, a):
    if name.endswith(("w_gate", "w_up")):
        return a.T
    if name == "conv_dw":
        return a[:CONV_WIDTH]
    return a


def kernel(x, ffn1_norm, ffn1_w_gate, ffn1_w_up, ffn1_w_down, mix_norm, w_in, conv_dw, conv_dw_b, conv_ln_g, conv_ln_b, conv_pw, pool_w, pool_scale, w_out, ffn2_norm, ffn2_w_gate, ffn2_w_up, ffn2_w_down, final_norm, loss_target, m_ffn1_norm, m_ffn1_w_gate, m_ffn1_w_up, m_ffn1_w_down, m_mix_norm, m_w_in, m_conv_dw, m_conv_dw_b, m_conv_ln_g, m_conv_ln_b, m_conv_pw, m_pool_w, m_pool_scale, m_w_out, m_ffn2_norm, m_ffn2_w_gate, m_ffn2_w_up, m_ffn2_w_down, m_final_norm, v_ffn1_norm, v_ffn1_w_gate, v_ffn1_w_up, v_ffn1_w_down, v_mix_norm, v_w_in, v_conv_dw, v_conv_dw_b, v_conv_ln_g, v_conv_ln_b, v_conv_pw, v_pool_w, v_pool_scale, v_w_out, v_ffn2_norm, v_ffn2_w_gate, v_ffn2_w_up, v_ffn2_w_down, v_final_norm):
    given = dict(locals())
    wts = {k: given[k] for k in WEIGHTS}
    mom_m = {k: given["m_" + k] for k in WEIGHTS}
    mom_v = {k: given["v_" + k] for k in WEIGHTS}
    xt, target = x[0], loss_target[0]

    shard = {k: _as_stored(k, wts[k]) if k == "conv_dw" else _as_stored(k, wts[k]).astype(BF16)
             for k in WEIGHTS if k.endswith(FFN_W) or k in MID}
    w = {k: wts[k].reshape(1, -1) for k in SMALL_1024 + SMALL_512}
    w["pool_w"] = wts["pool_w"].astype(BF16)

    (h1, gg1, uu1, w["ffn1_w_gate"], w["ffn1_w_up"]), ((w["ffn1_w_down"],),) = _ffn_up_gather(
        xt, w["ffn1_norm"], shard["ffn1_w_gate"], shard["ffn1_w_up"], "ffn1_up_gather",
        cargos=[Cargo("gather_slots", [shard["ffn1_w_down"]])])
    x1, (mid,) = _ffn_down(xt, gg1, uu1, w["ffn1_w_down"], "ffn1_down",
                           cargos=[Cargo("gather_chips", [shard[k] for k in MID])])
    w["w_in"] = mid[0]
    w["conv_dw"] = mid[1].transpose(1, 0, 2).reshape(CONV_WIDTH + 1, D_CONV)
    w["conv_pw"] = mid[2].reshape(D_CONV, D_CONV)
    w["w_out"] = mid[3].reshape(D_CONV + D_POOL, D_MODEL)
    (h2, proj), ((w["ffn2_w_down"],),) = _mix_in(x1, w["mix_norm"], w["w_in"], "mix_in",
                                                  cargos=[Cargo("gather_slots", [shard["ffn2_w_down"]])])
    (x2, u1, u3, mixed, cat), ((w["ffn2_w_gate"], w["ffn2_w_up"]),) = _mix_fwd(
        proj, x1, w["conv_dw"], w["conv_dw_b"], w["conv_ln_g"], w["conv_ln_b"], w["conv_pw"], w["pool_w"],
        w["pool_scale"], w["w_out"], "mix_fwd",
        cargos=[Cargo("gather_slots", [shard["ffn2_w_gate"], shard["ffn2_w_up"]])])
    x3, h3, gg2, uu2 = _ffn_fwd(x2, w["ffn2_norm"], w["ffn2_w_gate"], w["ffn2_w_up"], w["ffn2_w_down"], "ffn2_fwd")
    dx3, loss, d_final = _final_norm_loss(x3, target, w["final_norm"], "final_norm_loss")
    loss = lax.psum(loss[0, 0], ("x", "y", "c"))

    g = {"final_norm": d_final}
    sums = {}

    def landed(names, parts):
        for k, p in zip(names, parts):
            sums[k] = _sum_parts(p, "sum_chips_" + k)

    dx2, g["ffn2_norm"], df2, dg2, du2, a2 = _ffn_bwd(dx3, x2, w["ffn2_norm"], gg2, uu2, w["ffn2_w_gate"],
                                                       w["ffn2_w_up"], w["ffn2_w_down"], "ffn2_bwd")
    gw, _ = _wgrad_hid_tok(dg2, h3, "ffn2_dw_gate")
    gu, (parts,) = _wgrad_hid_tok(du2, h3, "ffn2_dw_up", cargos=[Cargo("scatter_slots", [gw])])
    landed(["ffn2_w_gate"], parts)
    gd, (parts,) = _wgrad_hid_tok(a2, df2, "ffn2_dw_down", cargos=[Cargo("scatter_slots", [gu])])
    landed(["ffn2_w_up"], parts)
    (du1, dmixed, dco, dpo, g["conv_ln_g"], g["conv_ln_b"], g["pool_scale"]), (parts,) = _mix_bwd_local(
        dx2, u1, mixed, w["conv_ln_g"], w["conv_ln_b"], w["conv_pw"], w["pool_w"], w["pool_scale"], w["w_out"],
        "mix_bwd_local", cargos=[Cargo("scatter_slots", [gd])])
    landed(["ffn2_w_down"], parts)
    g_out, _ = _wgrad_2d(cat, dx2, 1, BF16, "dw_out")
    g_pw, _ = _wgrad_2d(u3, dco, 1, BF16, "dconv_pw")
    g["pool_w"], _ = _wgrad_2d(mixed, dpo, 4, F32, "dpool_w", group_diag=True)
    slabs = [g_pw.reshape(N_CHIPS, D_CONV // N_CHIPS, D_CONV),
             g_out.reshape(N_CHIPS, (D_CONV + D_POOL) // N_CHIPS, D_MODEL)]
    (dx1, dproj, g_dw, g["conv_dw_b"], g["mix_norm"]), (parts, swapped2) = _mix_bwd_seq(
        du1, dmixed, proj, x1, dx2, w["mix_norm"], w["conv_dw"], w["w_in"], "mix_bwd_seq",
        cargos=[Cargo("scatter_chips", slabs),
                Cargo("swap", [sums[k] for k in ("ffn2_w_gate", "ffn2_w_up", "ffn2_w_down")])])
    landed(["conv_pw", "w_out"], parts)
    g_in, _ = _wgrad_2d(h2, dproj, N_CHIPS, BF16, "dw_in")
    dx, g["ffn1_norm"], df1, dg1, du1_, a1 = _ffn_bwd(dx1, xt, w["ffn1_norm"], gg1, uu1, w["ffn1_w_gate"],
                                                       w["ffn1_w_up"], w["ffn1_w_down"], "ffn1_bwd")
    slabs = [g_in, g_dw.reshape(CONV_WIDTH + 1, N_CHIPS, D_CONV // N_CHIPS).transpose(1, 0, 2)]
    gw, (parts, small_parts) = _wgrad_hid_tok(
        dg1, h1, "ffn1_dw_gate",
        cargos=[Cargo("scatter_chips", slabs), Cargo("gather_devices", [_pack_small(g)])])
    landed(["w_in", "conv_dw"], parts)
    gu, (parts, swapped_mid) = _wgrad_hid_tok(
        du1_, h1, "ffn1_dw_up",
        cargos=[Cargo("scatter_slots", [gw]), Cargo("swap", [sums[k] for k in MID])])
    landed(["ffn1_w_gate"], parts)
    gd, (parts,) = _wgrad_hid_tok(a1, df1, "ffn1_dw_down", cargos=[Cargo("scatter_slots", [gu])])
    landed(["ffn1_w_up"], parts)
    landed(["ffn1_w_down"], _exchange(Cargo("scatter_slots", [gd]), "scatter_last"))
    ffn1 = ["ffn1_" + k for k in FFN_W]
    swapped1 = _exchange(Cargo("swap", [sums[k] for k in ffn1]), "swap_last")

    theirs = dict(zip(["ffn2_" + k for k in FFN_W], swapped2))
    theirs.update(zip(MID, swapped_mid))
    theirs.update(zip(ffn1, swapped1))
    grads, deltas, new_m, new_v = {}, {}, {}, {}
    for k in theirs:
        res = _adamw([sums[k], theirs[k]], _as_stored(k, wts[k]), _as_stored(k, mom_m[k]),
                     _as_stored(k, mom_v[k]), "adamw_" + k)
        grads[k], deltas[k], new_m[k], new_v[k] = [_as_given(k, t) for t in res]
    res = _adamw(small_parts, _pack_small(wts), _pack_small(mom_m), _pack_small(mom_v), "adamw_small")
    for dst, packed in zip((grads, deltas, new_m, new_v), res):
        dst.update(_unpack_small(packed))

    out = [loss, dx[None]]
    for group in (grads, deltas, new_m, new_v):
        out += [group[k] for k in WEIGHTS]
    return tuple(out)
```

```python
import functools

import jax
import jax.numpy as jnp
from jax import lax
from jax.experimental import pallas as pl
from jax.experimental.pallas import tpu as pltpu

F32 = jnp.float32
BF16 = jnp.bfloat16
MESH = pl.DeviceIdType.MESH

N_CHIPS = 4
N_DEV = 8
D_MODEL = 1024
D_CONV = 512
D_POOL = 512
CONV_WIDTH = 31
POOL_WINDOWS = (2, 4, 8, 16)
POOL_GROUP = 128
D_IN = 2 * D_CONV + D_POOL
HALO = 32
RMS_EPS = 1e-6
LN_EPS = 1e-5
FFN_RES_WEIGHT = 0.5
ADAM_LR = 0.001
ADAM_B1 = 0.9
ADAM_B2 = 0.999
ADAM_EPS = 1e-08
ADAM_WD = 0.01
ADAM_STEP = 10
VMEM_LIMIT_BYTES = 52 * 1024 * 1024
TM_FFN = 512
TM_MIX = 256
TT_WGRAD = 2048
STRIP = 16

HBM = pl.BlockSpec(memory_space=pl.ANY)


def _dot(a, b):
    return jnp.dot(a, b, preferred_element_type=F32)


def _dot_nt(a, b):
    return lax.dot_general(a, b, (((1,), (1,)), ((), ())), preferred_element_type=F32)


def _dot_tn(a, b):
    return lax.dot_general(a, b, (((0,), (0,)), ((), ())), preferred_element_type=F32)


def _sds(shape, dtype):
    return jax.ShapeDtypeStruct(shape, dtype)


def _rms_stats(xv):
    r = lax.rsqrt(jnp.mean(xv * xv, axis=-1, keepdims=True) + RMS_EPS)
    return r, xv * r


def _swiglu_saved(gate, up):
    sig = jax.nn.sigmoid(gate)
    silu = gate * sig
    return silu, up * (sig * (1.0 + gate * (1.0 - sig))), silu * up


def _rms_bwd(dh, n, r, gain):
    dn = dh * gain
    return r * (dn - n * jnp.mean(dn * n, axis=-1, keepdims=True))


def _place():
    x, y, c = lax.axis_index("x"), lax.axis_index("y"), lax.axis_index("c")
    return x, y, c, [(1 - x, y), (x, 1 - y), (1 - x, 1 - y)]


class Cargo:
    def __init__(self, kind, arrays):
        self.kind, self.arrays = kind, list(arrays)
        n = len(self.arrays)
        if kind in ("gather_slots", "gather_chips"):
            self.out_shape = [_sds((N_CHIPS,) + a.shape, a.dtype) for a in self.arrays]
        elif kind == "gather_devices":
            self.out_shape = [_sds((N_DEV,) + a.shape, a.dtype) for a in self.arrays]
        else:
            self.out_shape = [_sds(a.shape, a.dtype) for a in self.arrays]
        per = {"swap": 1, "gather_devices": N_DEV - 1}.get(kind, N_CHIPS - 1)
        self.n_remote = per * n
        self.n_own = 0 if kind == "swap" else n
        self.scratch = [pltpu.SemaphoreType.DMA((self.n_remote,)), pltpu.SemaphoreType.DMA((self.n_remote,)),
                        pltpu.SemaphoreType.DMA((max(self.n_own, 1),))]

    def _plan(self, ins, outs):
        x, y, c, chips = _place()
        q = 2 * x + y
        own, remote = [], []
        for a, o in zip(ins, outs):
            if self.kind == "gather_slots":
                own.append((a, o.at[0]))
                remote += [(a, o.at[j + 1], o.at[j + 1], (px, py, c)) for j, (px, py) in enumerate(chips)]
            elif self.kind == "gather_chips":
                own.append((a, o.at[q]))
                remote += [(a, o.at[q], o.at[2 * px + py], (px, py, c)) for px, py in chips]
            elif self.kind == "scatter_slots":
                own.append((a.at[0], o.at[0]))
                remote += [(a.at[j + 1], o.at[j + 1], o.at[j + 1], (px, py, c)) for j, (px, py) in enumerate(chips)]
            elif self.kind == "scatter_chips":
                own.append((a.at[q], o.at[q]))
                remote += [(a.at[2 * px + py], o.at[q], o.at[2 * px + py], (px, py, c)) for px, py in chips]
            elif self.kind == "swap":
                remote.append((a, o, o, (x, y, 1 - c)))
            else:
                own.append((a, o.at[4 * x + 2 * y + c]))
                for k in range(1, N_DEV):
                    px, py, pc = x ^ (k >> 2 & 1), y ^ (k >> 1 & 1), c ^ (k & 1)
                    remote.append((a, o.at[4 * x + 2 * y + c], o.at[4 * px + 2 * py + pc], (px, py, pc)))
        return own, remote

    def start(self, ins, outs, sems):
        send_sems, recv_sems, own_sems = sems
        own, remote = self._plan(ins, outs)
        for k, (src, dst) in enumerate(own):
            pltpu.make_async_copy(src, dst, own_sems.at[k]).start()
        for k, (src, dst, _, peer) in enumerate(remote):
            pltpu.make_async_remote_copy(src_ref=src, dst_ref=dst, send_sem=send_sems.at[k], recv_sem=recv_sems.at[k],
                                         device_id=peer, device_id_type=MESH).start()

    def wait(self, ins, outs, sems):
        send_sems, recv_sems, own_sems = sems
        own, remote = self._plan(ins, outs)
        for k, (src, _, landed, peer) in enumerate(remote):
            pltpu.make_async_remote_copy(src_ref=src, dst_ref=landed, send_sem=send_sems.at[k],
                                         recv_sem=recv_sems.at[k], device_id=peer, device_id_type=MESH).wait()
        for k, (src, dst) in enumerate(own):
            pltpu.make_async_copy(src, dst, own_sems.at[k]).wait()


def _call(body, *, name, grid, in_specs, out_specs, out_shape, args, scratch_shapes=(), cargos=()):
    n_in, n_out, n_scr = len(in_specs), len(out_specs), len(scratch_shapes)
    c_in = [len(cg.arrays) for cg in cargos]
    n_cin = sum(c_in)

    def wrapped(*refs):
        ins = refs[:n_in]
        cins = refs[n_in:n_in + n_cin]
        outs = refs[n_in + n_cin:n_in + n_cin + n_out]
        couts = refs[n_in + n_cin + n_out:n_in + 2 * n_cin + n_out]
        scr = refs[n_in + 2 * n_cin + n_out:n_in + 2 * n_cin + n_out + n_scr]
        sems = refs[n_in + 2 * n_cin + n_out + n_scr:]
        first, last = True, True
        for ax, size in enumerate(grid):
            first = first & (pl.program_id(ax) == 0)
            last = last & (pl.program_id(ax) == size - 1)

        def each(method):
            at = 0
            for k, cg in enumerate(cargos):
                getattr(cg, method)(cins[at:at + c_in[k]], couts[at:at + c_in[k]], sems[3 * k:3 * k + 3])
                at += c_in[k]

        if cargos:
            pl.when(first)(lambda: each("start"))
        body(*ins, *outs, *scr)
        if cargos:
            pl.when(last)(lambda: each("wait"))

    res = pl.pallas_call(
        wrapped, name=name, grid=grid,
        in_specs=list(in_specs) + [HBM] * n_cin,
        out_specs=list(out_specs) + [HBM] * n_cin,
        out_shape=list(out_shape) + [s for cg in cargos for s in cg.out_shape],
        scratch_shapes=list(scratch_shapes) + [s for cg in cargos for s in cg.scratch],
        compiler_params=pltpu.CompilerParams(dimension_semantics=("arbitrary",) * len(grid),
                                             vmem_limit_bytes=VMEM_LIMIT_BYTES),
    )(*args, *[a for cg in cargos for a in cg.arrays])
    outs, rest = list(res[:n_out]), list(res[n_out:])
    cargo_outs = []
    for k in c_in:
        cargo_outs.append(rest[:k])
        rest = rest[k:]
    return outs, cargo_outs


def _exchange(cargo, name):
    _, (outs,) = _call(lambda: None, name=name, grid=(1,), in_specs=[], out_specs=[], out_shape=[], args=[],
                       cargos=[cargo])
    return outs


def _ffn_up_gather(x, gain, wg_t, wu_t, name, cargos=()):
    t_len, d = x.shape
    fq = wg_t.shape[0]
    tm = min(TM_FFN, t_len)
    n_tiles = t_len // tm

    def body(x_ref, g_ref, wg_in, wu_in, h_ref, s_ref, p_ref, a_ref, wg_all, wu_all,
             wg_v, wu_v, send_sems, recv_sems, own_sems, load_sems):
        s = pl.program_id(0)
        i = pl.program_id(1)
        x_, y_, c_, chips = _place()
        shards = ((wg_in, wg_all, wg_v), (wu_in, wu_all, wu_v))

        def to_peer(k, j):
            w_in, w_all, _ = shards[k]
            return pltpu.make_async_remote_copy(
                src_ref=w_in, dst_ref=w_all.at[j + 1], send_sem=send_sems.at[3 * k + j],
                recv_sem=recv_sems.at[3 * k + j], device_id=(*chips[j], c_), device_id_type=MESH)

        def keep(k):
            return pltpu.make_async_copy(shards[k][0], shards[k][1].at[0], own_sems.at[k])

        @pl.when((s == 0) & (i == 0))
        def _():
            for k in range(2):
                for j in range(N_CHIPS - 1):
                    to_peer(k, j).start()
                keep(k).start()

        for slot in range(N_CHIPS):
            @pl.when((s == slot) & (i == 0))
            def _():
                loads = []
                for k in range(2):
                    if slot > 0:
                        to_peer(k, slot - 1).wait_recv()
                    src = shards[k][0] if slot == 0 else shards[k][1].at[slot]
                    loads.append(pltpu.make_async_copy(src, shards[k][2], load_sems.at[k]))
                    loads[-1].start()
                for ld in loads:
                    ld.wait()

        _, n = _rms_stats(x_ref[...])
        h = (n * g_ref[...]).astype(BF16)

        @pl.when(s == 0)
        def _():
            h_ref[...] = h

        silu, dgate, act = _swiglu_saved(_dot_nt(h, wg_v[...]), _dot_nt(h, wu_v[...]))
        s_ref[...] = silu.astype(BF16)
        p_ref[...] = dgate.astype(BF16)
        a_ref[...] = act.astype(BF16)

        @pl.when((s == N_CHIPS - 1) & (i == n_tiles - 1))
        def _():
            for k in range(2):
                for j in range(N_CHIPS - 1):
                    to_peer(k, j).wait_send()
                keep(k).wait()

    tok = pl.BlockSpec((tm, d), lambda s, i: (i, 0))
    h_out = pl.BlockSpec((tm, d), lambda s, i: (jnp.where(s == 0, i, n_tiles - 1), 0))
    hid = pl.BlockSpec((None, tm, fq), lambda s, i: (s, i, 0))
    outs, cargo_outs = _call(
        body, name=name, grid=(N_CHIPS, n_tiles),
        in_specs=[tok, pl.BlockSpec((1, d), lambda s, i: (0, 0)), HBM, HBM],
        out_specs=[h_out, hid, hid, hid, HBM, HBM],
        out_shape=[_sds((t_len, d), BF16)] + [_sds((N_CHIPS, t_len, fq), BF16)] * 3
        + [_sds((N_CHIPS, fq, d), BF16)] * 2,
        scratch_shapes=[pltpu.VMEM((fq, d), BF16), pltpu.VMEM((fq, d), BF16),
                        pltpu.SemaphoreType.DMA((6,)), pltpu.SemaphoreType.DMA((6,)),
                        pltpu.SemaphoreType.DMA((2,)), pltpu.SemaphoreType.DMA((2,))],
        args=[x, gain, wg_t, wu_t], cargos=cargos)
    return outs, cargo_outs


def _ffn_down(x, act, wd, name, cargos=()):
    t_len, d = x.shape
    nq, fq, _ = wd.shape
    tm = min(TM_FFN, t_len)

    def body(x_ref, a_ref, wd_ref, xo_ref, acc):
        j = pl.program_id(1)

        @pl.when(j == 0)
        def _():
            acc[...] = jnp.zeros_like(acc)

        acc[...] += _dot(a_ref[...], wd_ref[...])

        @pl.when(j == nq - 1)
        def _():
            xo_ref[...] = x_ref[...] + FFN_RES_WEIGHT * acc[...]

    tok = pl.BlockSpec((tm, d), lambda i, j: (i, 0))
    hid = pl.BlockSpec((None, tm, fq), lambda i, j: (j, i, 0))
    (xo,), cargo_outs = _call(
        body, name=name, grid=(t_len // tm, nq),
        in_specs=[tok, hid, pl.BlockSpec((None, fq, d), lambda i, j: (j, 0, 0))],
        out_specs=[tok], out_shape=[_sds((t_len, d), F32)],
        scratch_shapes=[pltpu.VMEM((tm, d), F32)], args=[x, act, wd], cargos=cargos)
    return xo, cargo_outs


def _ffn_fwd(x, gain, wg_t, wu_t, wd, name):
    t_len, d = x.shape
    nq, fq, _ = wd.shape
    tm = min(TM_FFN, t_len)

    def body(x_ref, g_ref, wg_ref, wu_ref, wd_ref, xo_ref, h_ref, s_ref, p_ref, a_ref, h_s, acc):
        j = pl.program_id(1)

        @pl.when(j == 0)
        def _():
            _, n = _rms_stats(x_ref[...])
            h = (n * g_ref[...]).astype(BF16)
            h_s[...] = h
            h_ref[...] = h
            acc[...] = jnp.zeros_like(acc)

        h = h_s[...]
        silu, dgate, act = _swiglu_saved(_dot_nt(h, wg_ref[...]), _dot_nt(h, wu_ref[...]))
        s_ref[...] = silu.astype(BF16)
        p_ref[...] = dgate.astype(BF16)
        a_ref[...] = act.astype(BF16)
        acc[...] += _dot(a_ref[...], wd_ref[...])

        @pl.when(j == nq - 1)
        def _():
            xo_ref[...] = x_ref[...] + FFN_RES_WEIGHT * acc[...]

    tok = pl.BlockSpec((tm, d), lambda i, j: (i, 0))
    hid = pl.BlockSpec((None, tm, fq), lambda i, j: (j, i, 0))
    wgt = pl.BlockSpec((None, fq, d), lambda i, j: (j, 0, 0))
    outs, _ = _call(
        body, name=name, grid=(t_len // tm, nq),
        in_specs=[tok, pl.BlockSpec((1, d), lambda i, j: (0, 0)), wgt, wgt, wgt],
        out_specs=[tok, tok, hid, hid, hid],
        out_shape=[_sds((t_len, d), F32), _sds((t_len, d), BF16)] + [_sds((nq, t_len, fq), BF16)] * 3,
        scratch_shapes=[pltpu.VMEM((tm, d), BF16), pltpu.VMEM((tm, d), F32)],
        args=[x, gain, wg_t, wu_t, wd])
    return outs


def _ffn_bwd(dy, x_in, gain, silu, dgate_du, wg_t, wu_t, wd, name):
    t_len, d = dy.shape
    nq, fq, _ = wd.shape
    tm = min(TM_FFN, t_len)

    def body(dy_ref, x_ref, g_ref, s_ref, p_ref, wg_ref, wu_ref, wd_ref,
             dx_ref, dgain_ref, df_ref, dg_ref, du_ref, df_s, dh_acc, dact_s):
        i = pl.program_id(0)
        j = pl.program_id(1)

        @pl.when((i == 0) & (j == 0))
        def _():
            dgain_ref[...] = jnp.zeros_like(dgain_ref)

        @pl.when(j == 0)
        def _():
            df = (FFN_RES_WEIGHT * dy_ref[...]).astype(BF16)
            df_s[...] = df
            df_ref[...] = df
            dh_acc[...] = jnp.zeros_like(dh_acc)

        half = tm // 2
        for r0 in (0, half):
            dact_s[r0:r0 + half, :] = _dot_nt(df_s[r0:r0 + half, :], wd_ref[...])

        for r0 in range(0, tm, STRIP):
            dact = dact_s[r0:r0 + STRIP, :]
            dg_ref[r0:r0 + STRIP, :] = (dact * p_ref[r0:r0 + STRIP, :].astype(F32)).astype(BF16)
            du_ref[r0:r0 + STRIP, :] = (dact * s_ref[r0:r0 + STRIP, :].astype(F32)).astype(BF16)

        for r0 in (0, half):
            rows = slice(r0, r0 + half)
            dh_acc[rows, :] += _dot(dg_ref[rows, :], wg_ref[...]) + _dot(du_ref[rows, :], wu_ref[...])

        @pl.when(j == nq - 1)
        def _():
            r, n = _rms_stats(x_ref[...])
            dh = dh_acc[...]
            dgain_ref[...] += jnp.sum(dh * n, axis=0, keepdims=True)
            dx_ref[...] = dy_ref[...] + _rms_bwd(dh, n, r, g_ref[...])

    tok = pl.BlockSpec((tm, d), lambda i, j: (i, 0))
    vec = pl.BlockSpec((1, d), lambda i, j: (0, 0))
    hid = pl.BlockSpec((None, tm, fq), lambda i, j: (j, i, 0))
    wgt = pl.BlockSpec((None, fq, d), lambda i, j: (j, 0, 0))
    outs, _ = _call(
        body, name=name, grid=(t_len // tm, nq),
        in_specs=[tok, tok, vec, hid, hid, wgt, wgt, wgt],
        out_specs=[tok, vec, tok, hid, hid],
        out_shape=[_sds((t_len, d), F32), _sds((1, d), F32), _sds((t_len, d), BF16),
                   _sds((nq, t_len, fq), BF16), _sds((nq, t_len, fq), BF16)],
        scratch_shapes=[pltpu.VMEM((tm, d), BF16), pltpu.VMEM((tm, d), F32), pltpu.VMEM((tm, fq), F32)],
        args=[dy, x_in, gain, silu, dgate_du, wg_t, wu_t, wd])
    return outs


def _wgrad(lhs, rhs, l_spec, r_spec, out_shape, out_spec, acc_shape, grid, name, cargos=()):
    n_t = grid[-1]
    t_axis = len(grid) - 1

    def body(l_ref, r_ref, o_ref, acc):
        t = pl.program_id(t_axis)

        @pl.when(t == 0)
        def _():
            acc[...] = jnp.zeros_like(acc)

        acc[...] += _dot_tn(l_ref[...].astype(BF16), r_ref[...].astype(BF16))

        @pl.when(t == n_t - 1)
        def _():
            o_ref[...] = acc[...].astype(o_ref.dtype)

    (out,), cargo_outs = _call(
        body, name=name, grid=grid, in_specs=[l_spec, r_spec], out_specs=[out_spec], out_shape=[out_shape],
        scratch_shapes=[pltpu.VMEM(acc_shape, F32)], args=[lhs, rhs], cargos=cargos)
    return out, cargo_outs


def _wgrad_hid_tok(hid, tok, name, cargos=()):
    t_len, d = tok.shape
    nq, _, fq = hid.shape
    tt = min(TT_WGRAD, t_len)
    return _wgrad(hid, tok,
                  pl.BlockSpec((None, tt, fq), lambda q, t: (q, t, 0)),
                  pl.BlockSpec((tt, d), lambda q, t: (t, 0)),
                  _sds((nq, fq, d), BF16), pl.BlockSpec((None, fq, d), lambda q, t: (q, 0, 0)),
                  (fq, d), (nq, t_len // tt), name, cargos)


def _wgrad_2d(lhs, rhs, n_col_blocks, out_dtype, name, group_diag=False, cargos=()):
    t_len, k = lhs.shape
    n = rhs.shape[1]
    nb = n // n_col_blocks
    kb = k // n_col_blocks if group_diag else k
    tt = min(TT_WGRAD, t_len)
    l_map = (lambda q, t: (t, q)) if group_diag else (lambda q, t: (t, 0))
    return _wgrad(lhs, rhs,
                  pl.BlockSpec((tt, kb), l_map),
                  pl.BlockSpec((tt, nb), lambda q, t: (t, q)),
                  _sds((n_col_blocks, kb, nb), out_dtype),
                  pl.BlockSpec((None, kb, nb), lambda q, t: (q, 0, 0)),
                  (kb, nb), (n_col_blocks, t_len // tt), name, cargos)


def _mix_in(x1, gain, w_in, name, cargos=()):
    t_len, d = x1.shape
    nq, _, nb = w_in.shape
    tm = min(TM_FFN, t_len)

    def body(x_ref, g_ref, w_ref, h_ref, p_ref):
        _, n = _rms_stats(x_ref[...])
        h = (n * g_ref[...]).astype(BF16)
        h_ref[...] = h
        for q in range(nq):
            p_ref[:, q * nb:(q + 1) * nb] = _dot(h, w_ref[q])

    return _call(
        body, name=name, grid=(t_len // tm,),
        in_specs=[pl.BlockSpec((tm, d), lambda i: (i, 0)), pl.BlockSpec((1, d), lambda i: (0, 0)),
                  pl.BlockSpec((nq, d, nb), lambda i: (0, 0, 0))],
        out_specs=[pl.BlockSpec((tm, d), lambda i: (i, 0)), pl.BlockSpec((tm, nq * nb), lambda i: (i, 0))],
        out_shape=[_sds((t_len, d), BF16), _sds((t_len, nq * nb), F32)],
        args=[x1, gain, w_in], cargos=cargos)


def _layernorm_stats(u1):
    mu = jnp.mean(u1, axis=-1, keepdims=True)
    xc = u1 - mu
    rstd = lax.rsqrt(jnp.mean(xc * xc, axis=-1, keepdims=True) + LN_EPS)
    return rstd, xc * rstd


def _positions(i, tm, rows, offset=0):
    return (lax.broadcasted_iota(jnp.int32, (rows, 1), 0) + (i * tm + offset)).astype(F32)


def _tile(tm, cols):
    return pl.BlockSpec((tm, cols), lambda i: (i, 0))


def _whole(shape):
    return pl.BlockSpec(shape, lambda i: (0,) * len(shape))


def _mix_fwd(proj, x1, conv_dw, conv_b, ln_g, ln_b, conv_pw, pool_w, pool_scale, w_out, name, cargos=()):
    t_len, d = x1.shape
    tm = min(TM_MIX, t_len)
    hb = tm // HALO

    def body(p_ref, tail_ref, x_ref, dw_ref, cb_ref, lg_ref, lb_ref, pw_ref, plw_ref, ps_ref, wo_ref,
             x2_ref, u1_ref, u3_ref, mx_ref, cat_ref, ext_s, pext_s):
        i = pl.program_id(0)
        first = i == 0
        a = p_ref[:, 0:D_CONV]
        g = p_ref[:, D_CONV:2 * D_CONV]
        p = p_ref[:, 2 * D_CONV:]
        ta = tail_ref[:, 0:D_CONV]
        tg = tail_ref[:, D_CONV:2 * D_CONV]
        tp = tail_ref[:, 2 * D_CONV:]
        ext_s[0:HALO, :] = jnp.where(first, 0.0, ta * jax.nn.sigmoid(tg))
        ext_s[HALO:, :] = a * jax.nn.sigmoid(g)
        pext_s[0:HALO, :] = jnp.where(first, 0.0, tp)
        pext_s[HALO:, :] = p

        u1 = jnp.broadcast_to(cb_ref[...], (tm, D_CONV))
        for k in range(CONV_WIDTH):
            u1 = u1 + dw_ref[k:k + 1, :] * ext_s[pl.ds(HALO - (CONV_WIDTH - 1) + k, tm), :]
        u1_ref[...] = u1
        _, nhat = _layernorm_stats(u1)
        u2 = nhat * lg_ref[...] + lb_ref[...]
        u3 = (u2 * jax.nn.sigmoid(u2)).astype(BF16)
        u3_ref[...] = u3
        cat_ref[:, 0:D_CONV] = _dot(u3, pw_ref[...]).astype(BF16)

        pos1 = _positions(i, tm, tm) + 1.0
        for gi, w in enumerate(POOL_WINDOWS):
            cols = slice(gi * POOL_GROUP, (gi + 1) * POOL_GROUP)
            s = pext_s[pl.ds(HALO, tm), cols]
            for j in range(1, w):
                s = s + pext_s[pl.ds(HALO - j, tm), cols]
            mixed = (s / jnp.minimum(pos1, float(w)) - p[:, cols]).astype(BF16)
            mx_ref[:, cols] = mixed
            out = _dot(mixed, plw_ref[gi]) * ps_ref[:, cols]
            cat_ref[:, D_CONV + gi * POOL_GROUP:D_CONV + (gi + 1) * POOL_GROUP] = out.astype(BF16)

        x2_ref[...] = x_ref[...] + _dot(cat_ref[...], wo_ref[...])

    return _call(
        body, name=name, grid=(t_len // tm,),
        in_specs=[_tile(tm, D_IN), pl.BlockSpec((HALO, D_IN), lambda i: (jnp.maximum(i * hb - 1, 0), 0)),
                  _tile(tm, d), _whole((CONV_WIDTH + 1, D_CONV)), _whole((1, D_CONV)), _whole((1, D_CONV)),
                  _whole((1, D_CONV)), _whole((D_CONV, D_CONV)), _whole((4, POOL_GROUP, POOL_GROUP)),
                  _whole((1, D_POOL)), _whole((D_CONV + D_POOL, d))],
        out_specs=[_tile(tm, d), _tile(tm, D_CONV), _tile(tm, D_CONV), _tile(tm, D_POOL),
                   _tile(tm, D_CONV + D_POOL)],
        out_shape=[_sds((t_len, d), F32), _sds((t_len, D_CONV), F32), _sds((t_len, D_CONV), BF16),
                   _sds((t_len, D_POOL), BF16), _sds((t_len, D_CONV + D_POOL), BF16)],
        scratch_shapes=[pltpu.VMEM((tm + HALO, D_CONV), F32), pltpu.VMEM((tm + HALO, D_POOL), F32)],
        args=[proj, proj, x1, conv_dw, conv_b, ln_g, ln_b, conv_pw, pool_w, pool_scale, w_out], cargos=cargos)


def _mix_bwd_local(dx2, u1, mixed, ln_g, ln_b, conv_pw, pool_w, pool_scale, w_out, name, cargos=()):
    t_len, d = dx2.shape
    tm = min(TM_MIX, t_len)

    def body(dx_ref, u1_ref, mx_ref, lg_ref, lb_ref, pw_ref, plw_ref, ps_ref, wo_ref,
             du1_ref, dmx_ref, dco_ref, dpo_ref, dlg_ref, dlb_ref, dps_ref):
        @pl.when(pl.program_id(0) == 0)
        def _():
            dlg_ref[...] = jnp.zeros_like(dlg_ref)
            dlb_ref[...] = jnp.zeros_like(dlb_ref)
            dps_ref[...] = jnp.zeros_like(dps_ref)

        dcat = _dot_nt(dx_ref[...].astype(BF16), wo_ref[...])
        dco = dcat[:, 0:D_CONV].astype(BF16)
        dco_ref[...] = dco
        du3 = _dot_nt(dco, pw_ref[...])
        rstd, nhat = _layernorm_stats(u1_ref[...])
        u2 = nhat * lg_ref[...] + lb_ref[...]
        sig = jax.nn.sigmoid(u2)
        du2 = du3 * (sig * (1.0 + u2 * (1.0 - sig)))
        dlg_ref[...] += jnp.sum(du2 * nhat, axis=0, keepdims=True)
        dlb_ref[...] += jnp.sum(du2, axis=0, keepdims=True)
        dnhat = du2 * lg_ref[...]
        du1_ref[...] = rstd * (dnhat - jnp.mean(dnhat, axis=-1, keepdims=True)
                               - nhat * jnp.mean(dnhat * nhat, axis=-1, keepdims=True))

        for gi in range(len(POOL_WINDOWS)):
            cols = slice(gi * POOL_GROUP, (gi + 1) * POOL_GROUP)
            dpo = dcat[:, D_CONV + gi * POOL_GROUP:D_CONV + (gi + 1) * POOL_GROUP]
            pre = _dot(mx_ref[:, cols], plw_ref[gi])
            dps_ref[:, cols] += jnp.sum(dpo * pre, axis=0, keepdims=True)
            dout = (dpo * ps_ref[:, cols]).astype(BF16)
            dpo_ref[:, cols] = dout
            dmx_ref[:, cols] = _dot_nt(dout, plw_ref[gi])

    vec = _whole((1, D_CONV))
    return _call(
        body, name=name, grid=(t_len // tm,),
        in_specs=[_tile(tm, d), _tile(tm, D_CONV), _tile(tm, D_POOL), vec, vec, _whole((D_CONV, D_CONV)),
                  _whole((4, POOL_GROUP, POOL_GROUP)), vec, _whole((D_CONV + D_POOL, d))],
        out_specs=[_tile(tm, D_CONV), _tile(tm, D_POOL), _tile(tm, D_CONV), _tile(tm, D_POOL), vec, vec, vec],
        out_shape=[_sds((t_len, D_CONV), F32), _sds((t_len, D_POOL), F32), _sds((t_len, D_CONV), BF16),
                   _sds((t_len, D_POOL), BF16), _sds((1, D_CONV), F32), _sds((1, D_CONV), F32),
                   _sds((1, D_POOL), F32)],
        args=[dx2, u1, mixed, ln_g, ln_b, conv_pw, pool_w, pool_scale, w_out], cargos=cargos)


def _mix_bwd_seq(du1, dmixed, proj, x1, dx2, gain, conv_dw, w_in, name, cargos=()):
    t_len, d = x1.shape
    nq, _, nb = w_in.shape
    tm = min(TM_MIX, t_len)
    hb = tm // HALO
    last_block = t_len // HALO - 1
    n_tiles = t_len // tm

    def body(du_ref, dun_ref, dm_ref, dmn_ref, p_ref, tail_ref, x_ref, dx2_ref, g_ref, dw_ref, wi_ref,
             dx1_ref, dp_ref, ddw_ref, dcb_ref, dgain_ref, uext_s, dext_s, mext_s):
        i = pl.program_id(0)
        first = i == 0
        last = i == n_tiles - 1

        @pl.when(first)
        def _():
            ddw_ref[...] = jnp.zeros_like(ddw_ref)
            dcb_ref[...] = jnp.zeros_like(dcb_ref)
            dgain_ref[...] = jnp.zeros_like(dgain_ref)

        a = p_ref[:, 0:D_CONV]
        g = p_ref[:, D_CONV:2 * D_CONV]
        sg = jax.nn.sigmoid(g)
        ta = tail_ref[:, 0:D_CONV]
        tg = tail_ref[:, D_CONV:2 * D_CONV]
        uext_s[0:HALO, :] = jnp.where(first, 0.0, ta * jax.nn.sigmoid(tg))
        uext_s[HALO:, :] = a * sg
        du1 = du_ref[...]
        dext_s[0:tm, :] = du1
        dext_s[tm:, :] = jnp.where(last, 0.0, dun_ref[...])

        du0 = jnp.zeros((tm, D_CONV), F32)
        for k in range(CONV_WIDTH):
            du0 = du0 + dw_ref[k:k + 1, :] * dext_s[pl.ds(CONV_WIDTH - 1 - k, tm), :]
            ddw_ref[k:k + 1, :] += jnp.sum(
                du1 * uext_s[pl.ds(HALO - (CONV_WIDTH - 1) + k, tm), :], axis=0, keepdims=True)
        dcb_ref[...] += jnp.sum(du1, axis=0, keepdims=True)
        dp_ref[:, 0:D_CONV] = (du0 * sg).astype(BF16)
        dp_ref[:, D_CONV:2 * D_CONV] = (du0 * a * sg * (1.0 - sg)).astype(BF16)

        pos1 = _positions(i, tm, tm) + 1.0
        pos1_next = _positions(i, tm, HALO, offset=tm) + 1.0
        for gi, w in enumerate(POOL_WINDOWS):
            cols = slice(gi * POOL_GROUP, (gi + 1) * POOL_GROUP)
            dm = dm_ref[:, cols]
            mext_s[0:tm, cols] = dm / jnp.minimum(pos1, float(w))
            mext_s[tm:, cols] = jnp.where(last, 0.0, dmn_ref[:, cols] / jnp.minimum(pos1_next, float(w)))
            s = mext_s[pl.ds(0, tm), cols]
            for j in range(1, w):
                s = s + mext_s[pl.ds(j, tm), cols]
            dp_ref[:, 2 * D_CONV + gi * POOL_GROUP:2 * D_CONV + (gi + 1) * POOL_GROUP] = (s - dm).astype(BF16)

        dh = _dot_nt(dp_ref[:, 0:nb], wi_ref[0])
        for q in range(1, nq):
            dh = dh + _dot_nt(dp_ref[:, q * nb:(q + 1) * nb], wi_ref[q])
        r, n = _rms_stats(x_ref[...])
        dgain_ref[...] += jnp.sum(dh * n, axis=0, keepdims=True)
        dx1_ref[...] = dx2_ref[...] + _rms_bwd(dh, n, r, g_ref[...])

    def nxt(cols):
        return pl.BlockSpec((HALO, cols), lambda i: (jnp.minimum((i + 1) * hb, last_block), 0))

    return _call(
        body, name=name, grid=(n_tiles,),
        in_specs=[_tile(tm, D_CONV), nxt(D_CONV), _tile(tm, D_POOL), nxt(D_POOL), _tile(tm, D_IN),
                  pl.BlockSpec((HALO, D_IN), lambda i: (jnp.maximum(i * hb - 1, 0), 0)),
                  _tile(tm, d), _tile(tm, d), _whole((1, d)), _whole((CONV_WIDTH + 1, D_CONV)),
                  _whole((nq, d, nb))],
        out_specs=[_tile(tm, d), _tile(tm, D_IN), _whole((CONV_WIDTH + 1, D_CONV)), _whole((1, D_CONV)),
                   _whole((1, d))],
        out_shape=[_sds((t_len, d), F32), _sds((t_len, D_IN), BF16), _sds((CONV_WIDTH + 1, D_CONV), F32),
                   _sds((1, D_CONV), F32), _sds((1, d), F32)],
        scratch_shapes=[pltpu.VMEM((tm + HALO, D_CONV), F32), pltpu.VMEM((tm + HALO, D_CONV), F32),
                        pltpu.VMEM((tm + HALO, D_POOL), F32)],
        args=[du1, du1, dmixed, dmixed, proj, proj, x1, dx2, gain, conv_dw, w_in], cargos=cargos)


def _final_norm_loss(x3, target, gain, name):
    t_len, d = x3.shape
    tm = min(TM_FFN, t_len)

    def body(x_ref, t_ref, g_ref, dx_ref, loss_ref, dgain_ref):
        @pl.when(pl.program_id(0) == 0)
        def _():
            loss_ref[...] = jnp.zeros_like(loss_ref)
            dgain_ref[...] = jnp.zeros_like(dgain_ref)

        r, n = _rms_stats(x_ref[...])
        err = n * g_ref[...] - t_ref[...]
        per_tok = jnp.sum(err * err, axis=-1, keepdims=True) * (1.0 / d)
        loss_ref[...] += 0.5 * jnp.sum(per_tok, axis=0, keepdims=True)
        dy = err * (1.0 / d)
        dgain_ref[...] += jnp.sum(dy * n, axis=0, keepdims=True)
        dx_ref[...] = _rms_bwd(dy, n, r, g_ref[...])

    tok = pl.BlockSpec((tm, d), lambda i: (i, 0))
    outs, _ = _call(
        body, name=name, grid=(t_len // tm,),
        in_specs=[tok, tok, pl.BlockSpec((1, d), lambda i: (0, 0))],
        out_specs=[tok, pl.BlockSpec((1, 128), lambda i: (0, 0)), pl.BlockSpec((1, d), lambda i: (0, 0))],
        out_shape=[_sds((t_len, d), F32), _sds((1, 128), F32), _sds((1, d), F32)],
        args=[x3, target, gain])
    return outs


def _row_tile(rows):
    return rows // 4 if rows % 64 == 0 else rows


def _sum_parts(parts, name):
    n, r, c = parts.shape
    tr = _row_tile(r)

    def body(p_ref, o_ref):
        s = p_ref[0].astype(F32)
        for k in range(1, n):
            s = s + p_ref[k].astype(F32)
        o_ref[...] = s

    (out,), _ = _call(body, name=name, grid=(r // tr,),
                      in_specs=[pl.BlockSpec((n, tr, c), lambda i: (0, i, 0))],
                      out_specs=[pl.BlockSpec((tr, c), lambda i: (i, 0))], out_shape=[_sds((r, c), F32)],
                      args=[parts])
    return out


def _adamw_math(w, g, m, v):
    m = ADAM_B1 * m + (1.0 - ADAM_B1) * g
    v = ADAM_B2 * v + (1.0 - ADAM_B2) * (g * g)
    m_hat = m / (1.0 - ADAM_B1 ** ADAM_STEP)
    v_hat = v / (1.0 - ADAM_B2 ** ADAM_STEP)
    delta = -ADAM_LR * (m_hat / (jnp.sqrt(v_hat) + ADAM_EPS) + ADAM_WD * w)
    return delta, m, v


def _adamw(parts, w, m, v, name):
    r, c = w.shape
    n = len(parts)
    tr = _row_tile(r)

    def body(*refs):
        terms = []
        for p_ref in refs[:n]:
            terms += [p_ref[...]] if len(p_ref.shape) == 2 else [p_ref[k] for k in range(p_ref.shape[0])]
        w_ref, m_ref, v_ref, g_out, d_out, m_out, v_out = refs[n:]
        g = terms[0]
        for t in terms[1:]:
            g = g + t
        delta, nm, nv = _adamw_math(w_ref[...], g, m_ref[...], v_ref[...])
        g_out[...] = g
        d_out[...] = delta
        m_out[...] = nm
        v_out[...] = nv

    blk = pl.BlockSpec((tr, c), lambda i: (i, 0))
    p_specs = [blk if p.ndim == 2 else pl.BlockSpec((p.shape[0], tr, c), lambda i: (0, i, 0)) for p in parts]
    outs, _ = _call(body, name=name, grid=(r // tr,), in_specs=p_specs + [blk, blk, blk],
                    out_specs=[blk] * 4, out_shape=[_sds((r, c), F32)] * 4, args=[*parts, w, m, v])
    return outs


FFN_W = ("w_gate", "w_up", "w_down")
MID = ("w_in", "conv_dw", "conv_pw", "w_out")
SMALL_1024 = ("ffn1_norm", "mix_norm", "ffn2_norm", "final_norm")
SMALL_512 = ("conv_dw_b", "conv_ln_g", "conv_ln_b", "pool_scale")
WEIGHTS = ("ffn1_norm", "ffn1_w_gate", "ffn1_w_up", "ffn1_w_down", "mix_norm", "w_in", "conv_dw", "conv_dw_b",
           "conv_ln_g", "conv_ln_b", "conv_pw", "pool_w", "pool_scale", "w_out", "ffn2_norm", "ffn2_w_gate",
           "ffn2_w_up", "ffn2_w_down", "final_norm")
PACK_ROWS = 72


def _pad_rows(a, rows):
    return jnp.pad(a, ((0, rows - a.shape[0]), (0, 0)))


def _pack_small(t):
    rows = [t[k].reshape(1, D_MODEL) for k in SMALL_1024]
    rows.append(jnp.concatenate([t["conv_dw_b"].reshape(1, -1), t["conv_ln_g"].reshape(1, -1)], axis=1))
    rows.append(jnp.concatenate([t["conv_ln_b"].reshape(1, -1), t["pool_scale"].reshape(1, -1)], axis=1))
    rows.append(t["pool_w"].reshape(64, D_MODEL))
    return _pad_rows(jnp.concatenate(rows, axis=0), PACK_ROWS)


def _unpack_small(p):
    out = {k: p[i] for i, k in enumerate(SMALL_1024)}
    out["conv_dw_b"], out["conv_ln_g"] = p[4, :D_CONV], p[4, D_CONV:]
    out["conv_ln_b"], out["pool_scale"] = p[5, :D_CONV], p[5, D_CONV:]
    out["pool_w"] = p[6:70].reshape(4, POOL_GROUP, POOL_GROUP)
    return out


def _as_stored(name, a):
    if name.endswith(("w_gate", "w_up")):
        return a.T
    if name == "conv_dw":
        return _pad_rows(a, CONV_WIDTH + 1)
    return a


def _as_given(name, a):
    if name.endswith(("w_gate", "w_up")):
        return a.T
    if name == "conv_dw":
        return a[:CONV_WIDTH]
    return a


def kernel(x, ffn1_norm, ffn1_w_gate, ffn1_w_up, ffn1_w_down, mix_norm, w_in, conv_dw, conv_dw_b, conv_ln_g, conv_ln_b, conv_pw, pool_w, pool_scale, w_out, ffn2_norm, ffn2_w_gate, ffn2_w_up, ffn2_w_down, final_norm, loss_target, m_ffn1_norm, m_ffn1_w_gate, m_ffn1_w_up, m_ffn1_w_down, m_mix_norm, m_w_in, m_conv_dw, m_conv_dw_b, m_conv_ln_g, m_conv_ln_b, m_conv_pw, m_pool_w, m_pool_scale, m_w_out, m_ffn2_norm, m_ffn2_w_gate, m_ffn2_w_up, m_ffn2_w_down, m_final_norm, v_ffn1_norm, v_ffn1_w_gate, v_ffn1_w_up, v_ffn1_w_down, v_mix_norm, v_w_in, v_conv_dw, v_conv_dw_b, v_conv_ln_g, v_conv_ln_b, v_conv_pw, v_pool_w, v_pool_scale, v_w_out, v_ffn2_norm, v_ffn2_w_gate, v_ffn2_w_up, v_ffn2_w_down, v_final_norm):
    given = dict(locals())
    wts = {k: given[k] for k in WEIGHTS}
    mom_m = {k: given["m_" + k] for k in WEIGHTS}
    mom_v = {k: given["v_" + k] for k in WEIGHTS}
    xt, target = x[0], loss_target[0]

    shard = {k: _as_stored(k, wts[k]) if k == "conv_dw" else _as_stored(k, wts[k]).astype(BF16)
             for k in WEIGHTS if k.endswith(FFN_W) or k in MID}
    w = {k: wts[k].reshape(1, -1) for k in SMALL_1024 + SMALL_512}
    w["pool_w"] = wts["pool_w"].astype(BF16)

    (h1, s1, p1, a1, w["ffn1_w_gate"], w["ffn1_w_up"]), ((w["ffn1_w_down"],),) = _ffn_up_gather(
        xt, w["ffn1_norm"], shard["ffn1_w_gate"], shard["ffn1_w_up"], "ffn1_up_gather",
        cargos=[Cargo("gather_slots", [shard["ffn1_w_down"]])])
    x1, (mid,) = _ffn_down(xt, a1, w["ffn1_w_down"], "ffn1_down",
                           cargos=[Cargo("gather_chips", [shard[k] for k in MID])])
    w["w_in"] = mid[0]
    w["conv_dw"] = mid[1].transpose(1, 0, 2).reshape(CONV_WIDTH + 1, D_CONV)
    w["conv_pw"] = mid[2].reshape(D_CONV, D_CONV)
    w["w_out"] = mid[3].reshape(D_CONV + D_POOL, D_MODEL)
    (h2, proj), ((w["ffn2_w_down"],),) = _mix_in(x1, w["mix_norm"], w["w_in"], "mix_in",
                                                  cargos=[Cargo("gather_slots", [shard["ffn2_w_down"]])])
    (x2, u1, u3, mixed, cat), ((w["ffn2_w_gate"], w["ffn2_w_up"]),) = _mix_fwd(
        proj, x1, w["conv_dw"], w["conv_dw_b"], w["conv_ln_g"], w["conv_ln_b"], w["conv_pw"], w["pool_w"],
        w["pool_scale"], w["w_out"], "mix_fwd",
        cargos=[Cargo("gather_slots", [shard["ffn2_w_gate"], shard["ffn2_w_up"]])])
    x3, h3, s2, p2, a2 = _ffn_fwd(x2, w["ffn2_norm"], w["ffn2_w_gate"], w["ffn2_w_up"], w["ffn2_w_down"], "ffn2_fwd")
    dx3, loss, d_final = _final_norm_loss(x3, target, w["final_norm"], "final_norm_loss")
    loss = lax.psum(loss[0, 0], ("x", "y", "c"))

    g = {"final_norm": d_final}
    sums = {}

    def landed(names, parts):
        for k, p in zip(names, parts):
            sums[k] = _sum_parts(p, "sum_chips_" + k)

    dx2, g["ffn2_norm"], df2, dg2, du2 = _ffn_bwd(dx3, x2, w["ffn2_norm"], s2, p2, w["ffn2_w_gate"],
                                                   w["ffn2_w_up"], w["ffn2_w_down"], "ffn2_bwd")
    gw, _ = _wgrad_hid_tok(dg2, h3, "ffn2_dw_gate")
    gu, (parts,) = _wgrad_hid_tok(du2, h3, "ffn2_dw_up", cargos=[Cargo("scatter_slots", [gw])])
    landed(["ffn2_w_gate"], parts)
    gd, (parts,) = _wgrad_hid_tok(a2, df2, "ffn2_dw_down", cargos=[Cargo("scatter_slots", [gu])])
    landed(["ffn2_w_up"], parts)
    (du1, dmixed, dco, dpo, g["conv_ln_g"], g["conv_ln_b"], g["pool_scale"]), (parts,) = _mix_bwd_local(
        dx2, u1, mixed, w["conv_ln_g"], w["conv_ln_b"], w["conv_pw"], w["pool_w"], w["pool_scale"], w["w_out"],
        "mix_bwd_local", cargos=[Cargo("scatter_slots", [gd])])
    landed(["ffn2_w_down"], parts)
    g_out, _ = _wgrad_2d(cat, dx2, 1, BF16, "dw_out")
    g_pw, _ = _wgrad_2d(u3, dco, 1, BF16, "dconv_pw")
    g["pool_w"], _ = _wgrad_2d(mixed, dpo, 4, F32, "dpool_w", group_diag=True)
    slabs = [g_pw.reshape(N_CHIPS, D_CONV // N_CHIPS, D_CONV),
             g_out.reshape(N_CHIPS, (D_CONV + D_POOL) // N_CHIPS, D_MODEL)]
    (dx1, dproj, g_dw, g["conv_dw_b"], g["mix_norm"]), (parts, swapped2) = _mix_bwd_seq(
        du1, dmixed, proj, x1, dx2, w["mix_norm"], w["conv_dw"], w["w_in"], "mix_bwd_seq",
        cargos=[Cargo("scatter_chips", slabs),
                Cargo("swap", [sums[k] for k in ("ffn2_w_gate", "ffn2_w_up", "ffn2_w_down")])])
    landed(["conv_pw", "w_out"], parts)
    g_in, _ = _wgrad_2d(h2, dproj, N_CHIPS, BF16, "dw_in")
    dx, g["ffn1_norm"], df1, dg1, du1_ = _ffn_bwd(dx1, xt, w["ffn1_norm"], s1, p1, w["ffn1_w_gate"],
                                                   w["ffn1_w_up"], w["ffn1_w_down"], "ffn1_bwd")
    slabs = [g_in, g_dw.reshape(CONV_WIDTH + 1, N_CHIPS, D_CONV // N_CHIPS).transpose(1, 0, 2)]
    gw, (parts, small_parts) = _wgrad_hid_tok(
        dg1, h1, "ffn1_dw_gate",
        cargos=[Cargo("scatter_chips", slabs), Cargo("gather_devices", [_pack_small(g)])])
    landed(["w_in", "conv_dw"], parts)
    gu, (parts, swapped_mid) = _wgrad_hid_tok(
        du1_, h1, "ffn1_dw_up",
        cargos=[Cargo("scatter_slots", [gw]), Cargo("swap", [sums[k] for k in MID])])
    landed(["ffn1_w_gate"], parts)
    gd, (parts,) = _wgrad_hid_tok(a1, df1, "ffn1_dw_down", cargos=[Cargo("scatter_slots", [gu])])
    landed(["ffn1_w_up"], parts)
    landed(["ffn1_w_down"], _exchange(Cargo("scatter_slots", [gd]), "scatter_last"))
    ffn1 = ["ffn1_" + k for k in FFN_W]
    swapped1 = _exchange(Cargo("swap", [sums[k] for k in ffn1]), "swap_last")

    theirs = dict(zip(["ffn2_" + k for k in FFN_W], swapped2))
    theirs.update(zip(MID, swapped_mid))
    theirs.update(zip(ffn1, swapped1))
    grads, deltas, new_m, new_v = {}, {}, {}, {}
    for k in theirs:
        res = _adamw([sums[k], theirs[k]], _as_stored(k, wts[k]), _as_stored(k, mom_m[k]),
                     _as_stored(k, mom_v[k]), "adamw_" + k)
        grads[k], deltas[k], new_m[k], new_v[k] = [_as_given(k, t) for t in res]
    res = _adamw(small_parts, _pack_small(wts), _pack_small(mom_m), _pack_small(mom_v), "adamw_small")
    for dst, packed in zip((grads, deltas, new_m, new_v), res):
        dst.update(_unpack_small(packed))

    out = [loss, dx[None]]
    for group in (grads, deltas, new_m, new_v):
        out += [group[k] for k in WEIGHTS]
    return tuple(out)
```

```python
import functools

import jax
import jax.numpy as jnp
from jax import lax
from jax.experimental import pallas as pl
from jax.experimental.pallas import tpu as pltpu

F32 = jnp.float32
BF16 = jnp.bfloat16
MESH = pl.DeviceIdType.MESH

N_CHIPS = 4
N_DEV = 8
D_MODEL = 1024
D_CONV = 512
D_POOL = 512
CONV_WIDTH = 31
POOL_WINDOWS = (2, 4, 8, 16)
POOL_GROUP = 128
D_IN = 2 * D_CONV + D_POOL
HALO = 32
RMS_EPS = 1e-6
LN_EPS = 1e-5
FFN_RES_WEIGHT = 0.5
ADAM_LR = 0.001
ADAM_B1 = 0.9
ADAM_B2 = 0.999
ADAM_EPS = 1e-08
ADAM_WD = 0.01
ADAM_STEP = 10
VMEM_LIMIT_BYTES = 52 * 1024 * 1024
TM_FFN = 512
TM_MIX = 256
TT_WGRAD = 2048
STRIP = 16
SUBLANES = 8
RELAY_AT_EIGHTHS = 5

HBM = pl.BlockSpec(memory_space=pl.ANY)


def _dot(a, b):
    return jnp.dot(a, b, preferred_element_type=F32)


def _dot_nt(a, b):
    return lax.dot_general(a, b, (((1,), (1,)), ((), ())), preferred_element_type=F32)


def _dot_tn(a, b):
    return lax.dot_general(a, b, (((0,), (0,)), ((), ())), preferred_element_type=F32)


def _sds(shape, dtype):
    return jax.ShapeDtypeStruct(shape, dtype)


def _rms_stats(xv):
    r = lax.rsqrt(jnp.mean(xv * xv, axis=-1, keepdims=True) + RMS_EPS)
    return r, xv * r


def _swiglu_saved(gate, up):
    sig = jax.nn.sigmoid(gate)
    silu = gate * sig
    return silu, up * (sig * (1.0 + gate * (1.0 - sig))), silu * up


def _rms_bwd(dh, n, r, gain):
    dn = dh * gain
    return r * (dn - n * jnp.mean(dn * n, axis=-1, keepdims=True))


def _place():
    x, y, c = lax.axis_index("x"), lax.axis_index("y"), lax.axis_index("c")
    return x, y, c, [(1 - x, y), (x, 1 - y), (1 - x, 1 - y)]


class Cargo:
    def __init__(self, kind, arrays):
        self.kind, self.arrays = kind, list(arrays)
        n = len(self.arrays)
        self.two_level = kind in ("gather_slots", "gather_chips")
        if self.two_level:
            self.out_shape = [_sds((N_CHIPS,) + a.shape, a.dtype) for a in self.arrays]
        elif kind == "gather_devices":
            self.out_shape = [_sds((N_DEV,) + a.shape, a.dtype) for a in self.arrays]
        else:
            self.out_shape = [_sds(a.shape, a.dtype) for a in self.arrays]
        n_remote = n * {"swap": 1, "gather_devices": N_DEV - 1}.get(kind, N_CHIPS - 1)
        n_own = 0 if kind == "swap" else n
        n_relay = n_remote if self.two_level else 0
        dma = pltpu.SemaphoreType.DMA
        self.scratch = [dma((n_remote,)), dma((n_remote,)), dma((max(n_own, 1),)),
                        dma((max(n_relay, 1),)), dma((max(n_relay, 1),))]

    def _plan(self, ins, outs):
        x, y, c, chips = _place()
        q = 2 * x + y
        sibling = (x, y, 1 - c)
        own, remote, relays = [], [], []
        for a, o in zip(ins, outs):
            if self.two_level:
                half = a.shape[0] // 2
                mine = pl.ds(pl.multiple_of(c * half, SUBLANES), half)
                theirs = pl.ds(pl.multiple_of((1 - c) * half, SUBLANES), half)
                own.append((a, o.at[0 if self.kind == "gather_slots" else q]))
                for j, (px, py) in enumerate(chips):
                    there, here = (j + 1, j + 1) if self.kind == "gather_slots" else (q, 2 * px + py)
                    remote.append((a.at[mine], o.at[there, mine], o.at[here, mine], (px, py, c)))
                    relays.append((o.at[here, mine], o.at[here, mine], o.at[here, theirs], sibling))
            elif self.kind == "scatter_slots":
                own.append((a.at[0], o.at[0]))
                remote += [(a.at[j + 1], o.at[j + 1], o.at[j + 1], (px, py, c)) for j, (px, py) in enumerate(chips)]
            elif self.kind == "scatter_chips":
                own.append((a.at[q], o.at[q]))
                remote += [(a.at[2 * px + py], o.at[q], o.at[2 * px + py], (px, py, c)) for px, py in chips]
            elif self.kind == "swap":
                remote.append((a, o, o, sibling))
            else:
                own.append((a, o.at[4 * x + 2 * y + c]))
                for k in range(1, N_DEV):
                    px, py, pc = x ^ (k >> 2 & 1), y ^ (k >> 1 & 1), c ^ (k & 1)
                    remote.append((a, o.at[4 * x + 2 * y + c], o.at[4 * px + 2 * py + pc], (px, py, pc)))
        return own, remote, relays

    @staticmethod
    def _copies(entries, send_sems, recv_sems):
        out = []
        for k, (src, dst, landed, peer) in enumerate(entries):
            def make(dst_ref, k=k, src=src, peer=peer):
                return pltpu.make_async_remote_copy(src_ref=src, dst_ref=dst_ref, send_sem=send_sems.at[k],
                                                    recv_sem=recv_sems.at[k], device_id=peer, device_id_type=MESH)
            out.append((make(dst), make(landed)))
        return out

    def start(self, ins, outs, sems):
        own, remote, _ = self._plan(ins, outs)
        for k, (src, dst) in enumerate(own):
            pltpu.make_async_copy(src, dst, sems[2].at[k]).start()
        for mine, _ in self._copies(remote, sems[0], sems[1]):
            mine.start()

    def relay(self, ins, outs, sems):
        _, remote, relays = self._plan(ins, outs)
        passed = self._copies(relays, sems[3], sems[4])
        for (_, arriving), (mine, _) in zip(self._copies(remote, sems[0], sems[1]), passed):
            arriving.wait_recv()
            mine.start()

    def wait(self, ins, outs, sems):
        own, remote, relays = self._plan(ins, outs)
        for mine, arriving in self._copies(remote, sems[0], sems[1]):
            mine.wait_send()
            if not self.two_level:
                arriving.wait_recv()
        for mine, arriving in self._copies(relays, sems[3], sems[4]):
            mine.wait_send()
            arriving.wait_recv()
        for k, (src, dst) in enumerate(own):
            pltpu.make_async_copy(src, dst, sems[2].at[k]).wait()


N_CARGO_SEMS = 5


def _call(body, *, name, grid, in_specs, out_specs, out_shape, args, scratch_shapes=(), cargos=()):
    n_in, n_out, n_scr = len(in_specs), len(out_specs), len(scratch_shapes)
    c_in = [len(cg.arrays) for cg in cargos]
    n_cin = sum(c_in)

    def wrapped(*refs):
        ins = refs[:n_in]
        cins = refs[n_in:n_in + n_cin]
        outs = refs[n_in + n_cin:n_in + n_cin + n_out]
        couts = refs[n_in + n_cin + n_out:n_in + 2 * n_cin + n_out]
        scr = refs[n_in + 2 * n_cin + n_out:n_in + 2 * n_cin + n_out + n_scr]
        sems = refs[n_in + 2 * n_cin + n_out + n_scr:]
        step, n_steps = 0, 1
        for ax, size in enumerate(grid):
            step = step * size + pl.program_id(ax)
            n_steps *= size

        def each(method, only_two_level=False):
            at = 0
            for k, cg in enumerate(cargos):
                if cg.two_level or not only_two_level:
                    getattr(cg, method)(cins[at:at + c_in[k]], couts[at:at + c_in[k]],
                                        sems[N_CARGO_SEMS * k:N_CARGO_SEMS * (k + 1)])
                at += c_in[k]

        if cargos:
            pl.when(step == 0)(lambda: each("start"))
        body(*ins, *outs, *scr)
        if any(cg.two_level for cg in cargos):
            pl.when(step == (RELAY_AT_EIGHTHS * n_steps) // 8)(lambda: each("relay", only_two_level=True))
        if cargos:
            pl.when(step == n_steps - 1)(lambda: each("wait"))

    res = pl.pallas_call(
        wrapped, name=name, grid=grid,
        in_specs=list(in_specs) + [HBM] * n_cin,
        out_specs=list(out_specs) + [HBM] * n_cin,
        out_shape=list(out_shape) + [s for cg in cargos for s in cg.out_shape],
        scratch_shapes=list(scratch_shapes) + [s for cg in cargos for s in cg.scratch],
        compiler_params=pltpu.CompilerParams(dimension_semantics=("arbitrary",) * len(grid),
                                             vmem_limit_bytes=VMEM_LIMIT_BYTES),
    )(*args, *[a for cg in cargos for a in cg.arrays])
    outs, rest = list(res[:n_out]), list(res[n_out:])
    cargo_outs = []
    for k in c_in:
        cargo_outs.append(rest[:k])
        rest = rest[k:]
    return outs, cargo_outs


def _exchange(cargo, name):
    _, (outs,) = _call(lambda: None, name=name, grid=(1,), in_specs=[], out_specs=[], out_shape=[], args=[],
                       cargos=[cargo])
    return outs


def _ffn_up_gather(x, gain, wg_t, wu_t, name, cargos=()):
    t_len, d = x.shape
    fq = wg_t.shape[0]
    tm = min(TM_FFN, t_len)
    n_tiles = t_len // tm

    def body(x_ref, g_ref, wg_in, wu_in, h_ref, s_ref, p_ref, a_ref, wg_all, wu_all,
             wg_v, wu_v, send_sems, recv_sems, pass_send_sems, pass_recv_sems, own_sems, load_sems):
        s = pl.program_id(0)
        i = pl.program_id(1)
        x_, y_, c_, chips = _place()
        shards = ((wg_in, wg_all, wg_v), (wu_in, wu_all, wu_v))
        mine = pl.ds(pl.multiple_of(c_ * (fq // 2), SUBLANES), fq // 2)
        theirs = pl.ds(pl.multiple_of((1 - c_) * (fq // 2), SUBLANES), fq // 2)

        def to_peer(k, j):
            w_in, w_all, _ = shards[k]
            return pltpu.make_async_remote_copy(
                src_ref=w_in.at[mine], dst_ref=w_all.at[j + 1, mine], send_sem=send_sems.at[3 * k + j],
                recv_sem=recv_sems.at[3 * k + j], device_id=(*chips[j], c_), device_id_type=MESH)

        def to_sibling(k, j, landing=False):
            w_all = shards[k][1]
            return pltpu.make_async_remote_copy(
                src_ref=w_all.at[j + 1, mine], dst_ref=w_all.at[j + 1, theirs if landing else mine],
                send_sem=pass_send_sems.at[3 * k + j], recv_sem=pass_recv_sems.at[3 * k + j],
                device_id=(x_, y_, 1 - c_), device_id_type=MESH)

        def keep(k):
            return pltpu.make_async_copy(shards[k][0], shards[k][1].at[0], own_sems.at[k])

        @pl.when((s == 0) & (i == 0))
        def _():
            for k in range(2):
                for j in range(N_CHIPS - 1):
                    to_peer(k, j).start()
                keep(k).start()

        for slot in range(N_CHIPS):
            @pl.when((s == slot) & (i == 0))
            def _():
                if slot > 0:
                    for k in range(2):
                        to_peer(k, slot - 1).wait_recv()
                        to_sibling(k, slot - 1).start()
                    for k in range(2):
                        to_sibling(k, slot - 1, landing=True).wait_recv()
                loads = []
                for k in range(2):
                    src = shards[k][0] if slot == 0 else shards[k][1].at[slot]
                    loads.append(pltpu.make_async_copy(src, shards[k][2], load_sems.at[k]))
                    loads[-1].start()
                for ld in loads:
                    ld.wait()

        _, n = _rms_stats(x_ref[...])
        h = (n * g_ref[...]).astype(BF16)

        @pl.when(s == 0)
        def _():
            h_ref[...] = h

        silu, dgate, act = _swiglu_saved(_dot_nt(h, wg_v[...]), _dot_nt(h, wu_v[...]))
        s_ref[...] = silu.astype(BF16)
        p_ref[...] = dgate.astype(BF16)
        a_ref[...] = act.astype(BF16)

        @pl.when((s == N_CHIPS - 1) & (i == n_tiles - 1))
        def _():
            for k in range(2):
                for j in range(N_CHIPS - 1):
                    to_peer(k, j).wait_send()
                    to_sibling(k, j).wait_send()
                keep(k).wait()

    tok = pl.BlockSpec((tm, d), lambda s, i: (i, 0))
    h_out = pl.BlockSpec((tm, d), lambda s, i: (jnp.where(s == 0, i, n_tiles - 1), 0))
    hid = pl.BlockSpec((None, tm, fq), lambda s, i: (s, i, 0))
    outs, cargo_outs = _call(
        body, name=name, grid=(N_CHIPS, n_tiles),
        in_specs=[tok, pl.BlockSpec((1, d), lambda s, i: (0, 0)), HBM, HBM],
        out_specs=[h_out, hid, hid, hid, HBM, HBM],
        out_shape=[_sds((t_len, d), BF16)] + [_sds((N_CHIPS, t_len, fq), BF16)] * 3
        + [_sds((N_CHIPS, fq, d), BF16)] * 2,
        scratch_shapes=[pltpu.VMEM((fq, d), BF16), pltpu.VMEM((fq, d), BF16)]
        + [pltpu.SemaphoreType.DMA((6,))] * 4 + [pltpu.SemaphoreType.DMA((2,))] * 2,
        args=[x, gain, wg_t, wu_t], cargos=cargos)
    return outs, cargo_outs


def _load_once(hbm_refs, vmem_refs, sems, first):
    @pl.when(first)
    def _():
        copies = [pltpu.make_async_copy(src, dst, sems.at[k]) for k, (src, dst) in enumerate(zip(hbm_refs, vmem_refs))]
        for cp in copies:
            cp.start()
        for cp in copies:
            cp.wait()


def _ffn_down(x, act, wd, name, cargos=()):
    t_len, d = x.shape
    nq, fq, _ = wd.shape
    tm = min(TM_FFN, t_len)

    def body(x_ref, a_ref, wd_ref, xo_ref):
        y = _dot(a_ref[0], wd_ref[0])
        for j in range(1, nq):
            y = y + _dot(a_ref[j], wd_ref[j])
        xo_ref[...] = x_ref[...] + FFN_RES_WEIGHT * y

    tok = pl.BlockSpec((tm, d), lambda i: (i, 0))
    (xo,), cargo_outs = _call(
        body, name=name, grid=(t_len // tm,),
        in_specs=[tok, pl.BlockSpec((nq, tm, fq), lambda i: (0, i, 0)), pl.BlockSpec((nq, fq, d), lambda i: (0, 0, 0))],
        out_specs=[tok], out_shape=[_sds((t_len, d), F32)], args=[x, act, wd], cargos=cargos)
    return xo, cargo_outs


def _ffn_fwd(x, gain, wg_t, wu_t, wd, name):
    t_len, d = x.shape
    nq, fq, _ = wd.shape
    tm = min(TM_FFN, t_len)

    def body(x_ref, g_ref, wg_hbm, wu_hbm, wd_hbm, xo_ref, h_ref, s_ref, p_ref, a_ref,
             h_s, acc, wg_v, wu_v, wd_v, load_sems):
        i = pl.program_id(0)
        j = pl.program_id(1)
        _load_once((wg_hbm, wu_hbm, wd_hbm), (wg_v, wu_v, wd_v), load_sems, (i == 0) & (j == 0))

        @pl.when(j == 0)
        def _():
            _, n = _rms_stats(x_ref[...])
            h = (n * g_ref[...]).astype(BF16)
            h_s[...] = h
            h_ref[...] = h
            acc[...] = jnp.zeros_like(acc)

        h = h_s[...]
        silu, dgate, act = _swiglu_saved(_dot_nt(h, wg_v[j]), _dot_nt(h, wu_v[j]))
        s_ref[...] = silu.astype(BF16)
        p_ref[...] = dgate.astype(BF16)
        a_ref[...] = act.astype(BF16)
        acc[...] += _dot(a_ref[...], wd_v[j])

        @pl.when(j == nq - 1)
        def _():
            xo_ref[...] = x_ref[...] + FFN_RES_WEIGHT * acc[...]

    tok = pl.BlockSpec((tm, d), lambda i, j: (i, 0))
    hid = pl.BlockSpec((None, tm, fq), lambda i, j: (j, i, 0))
    outs, _ = _call(
        body, name=name, grid=(t_len // tm, nq),
        in_specs=[tok, pl.BlockSpec((1, d), lambda i, j: (0, 0)), HBM, HBM, HBM],
        out_specs=[tok, tok, hid, hid, hid],
        out_shape=[_sds((t_len, d), F32), _sds((t_len, d), BF16)] + [_sds((nq, t_len, fq), BF16)] * 3,
        scratch_shapes=[pltpu.VMEM((tm, d), BF16), pltpu.VMEM((tm, d), F32)]
        + [pltpu.VMEM((nq, fq, d), BF16)] * 3 + [pltpu.SemaphoreType.DMA((3,))],
        args=[x, gain, wg_t, wu_t, wd])
    return outs


def _ffn_bwd(dy, x_in, gain, silu, dgate_du, wg_t, wu_t, wd, name):
    t_len, d = dy.shape
    nq, fq, _ = wd.shape
    tm = min(TM_FFN, t_len)

    def body(dy_ref, x_ref, g_ref, s_ref, p_ref, wg_hbm, wu_hbm, wd_hbm,
             dx_ref, dgain_ref, df_ref, dg_ref, du_ref, df_s, dh_acc, dact_s, wg_v, wu_v, wd_v, load_sems):
        i = pl.program_id(0)
        j = pl.program_id(1)
        _load_once((wg_hbm, wu_hbm, wd_hbm), (wg_v, wu_v, wd_v), load_sems, (i == 0) & (j == 0))

        @pl.when((i == 0) & (j == 0))
        def _():
            dgain_ref[...] = jnp.zeros_like(dgain_ref)

        @pl.when(j == 0)
        def _():
            df = (FFN_RES_WEIGHT * dy_ref[...]).astype(BF16)
            df_s[...] = df
            df_ref[...] = df
            dh_acc[...] = jnp.zeros_like(dh_acc)

        half = tm // 2
        for r0 in (0, half):
            dact_s[r0:r0 + half, :] = _dot_nt(df_s[r0:r0 + half, :], wd_v[j])

        for r0 in range(0, tm, STRIP):
            dact = dact_s[r0:r0 + STRIP, :]
            dg_ref[r0:r0 + STRIP, :] = (dact * p_ref[r0:r0 + STRIP, :].astype(F32)).astype(BF16)
            du_ref[r0:r0 + STRIP, :] = (dact * s_ref[r0:r0 + STRIP, :].astype(F32)).astype(BF16)

        for r0 in (0, half):
            rows = slice(r0, r0 + half)
            dh_acc[rows, :] += _dot(dg_ref[rows, :], wg_v[j]) + _dot(du_ref[rows, :], wu_v[j])

        @pl.when(j == nq - 1)
        def _():
            r, n = _rms_stats(x_ref[...])
            dh = dh_acc[...]
            dgain_ref[...] += jnp.sum(dh * n, axis=0, keepdims=True)
            dx_ref[...] = dy_ref[...] + _rms_bwd(dh, n, r, g_ref[...])

    tok = pl.BlockSpec((tm, d), lambda i, j: (i, 0))
    vec = pl.BlockSpec((1, d), lambda i, j: (0, 0))
    hid = pl.BlockSpec((None, tm, fq), lambda i, j: (j, i, 0))
    outs, _ = _call(
        body, name=name, grid=(t_len // tm, nq),
        in_specs=[tok, tok, vec, hid, hid, HBM, HBM, HBM],
        out_specs=[tok, vec, tok, hid, hid],
        out_shape=[_sds((t_len, d), F32), _sds((1, d), F32), _sds((t_len, d), BF16),
                   _sds((nq, t_len, fq), BF16), _sds((nq, t_len, fq), BF16)],
        scratch_shapes=[pltpu.VMEM((tm, d), BF16), pltpu.VMEM((tm, d), F32), pltpu.VMEM((tm, fq), F32)]
        + [pltpu.VMEM((nq, fq, d), BF16)] * 3 + [pltpu.SemaphoreType.DMA((3,))],
        args=[dy, x_in, gain, silu, dgate_du, wg_t, wu_t, wd])
    return outs


def _wgrad(lhs, rhs, l_spec, r_spec, out_shape, out_spec, acc_shape, grid, name, cargos=()):
    n_t = grid[-1]
    t_axis = len(grid) - 1

    def body(l_ref, r_ref, o_ref, acc):
        t = pl.program_id(t_axis)

        @pl.when(t == 0)
        def _():
            acc[...] = jnp.zeros_like(acc)

        acc[...] += _dot_tn(l_ref[...].astype(BF16), r_ref[...].astype(BF16))

        @pl.when(t == n_t - 1)
        def _():
            o_ref[...] = acc[...].astype(o_ref.dtype)

    (out,), cargo_outs = _call(
        body, name=name, grid=grid, in_specs=[l_spec, r_spec], out_specs=[out_spec], out_shape=[out_shape],
        scratch_shapes=[pltpu.VMEM(acc_shape, F32)], args=[lhs, rhs], cargos=cargos)
    return out, cargo_outs


def _wgrad_hid_tok(hid, tok, name, cargos=()):
    t_len, d = tok.shape
    nq, _, fq = hid.shape
    tt = min(TT_WGRAD, t_len)
    return _wgrad(hid, tok,
                  pl.BlockSpec((None, tt, fq), lambda q, t: (q, t, 0)),
                  pl.BlockSpec((tt, d), lambda q, t: (t, 0)),
                  _sds((nq, fq, d), BF16), pl.BlockSpec((None, fq, d), lambda q, t: (q, 0, 0)),
                  (fq, d), (nq, t_len // tt), name, cargos)


def _wgrad_2d(lhs, rhs, n_col_blocks, out_dtype, name, group_diag=False, cargos=()):
    t_len, k = lhs.shape
    n = rhs.shape[1]
    nb = n // n_col_blocks
    kb = k // n_col_blocks if group_diag else k
    tt = min(TT_WGRAD, t_len)
    l_map = (lambda q, t: (t, q)) if group_diag else (lambda q, t: (t, 0))
    return _wgrad(lhs, rhs,
                  pl.BlockSpec((tt, kb), l_map),
                  pl.BlockSpec((tt, nb), lambda q, t: (t, q)),
                  _sds((n_col_blocks, kb, nb), out_dtype),
                  pl.BlockSpec((None, kb, nb), lambda q, t: (q, 0, 0)),
                  (kb, nb), (n_col_blocks, t_len // tt), name, cargos)


def _mix_in(x1, gain, w_in, name, cargos=()):
    t_len, d = x1.shape
    nq, _, nb = w_in.shape
    tm = min(TM_FFN, t_len)

    def body(x_ref, g_ref, w_ref, h_ref, p_ref):
        _, n = _rms_stats(x_ref[...])
        h = (n * g_ref[...]).astype(BF16)
        h_ref[...] = h
        for q in range(nq):
            p_ref[:, q * nb:(q + 1) * nb] = _dot(h, w_ref[q])

    return _call(
        body, name=name, grid=(t_len // tm,),
        in_specs=[pl.BlockSpec((tm, d), lambda i: (i, 0)), pl.BlockSpec((1, d), lambda i: (0, 0)),
                  pl.BlockSpec((nq, d, nb), lambda i: (0, 0, 0))],
        out_specs=[pl.BlockSpec((tm, d), lambda i: (i, 0)), pl.BlockSpec((tm, nq * nb), lambda i: (i, 0))],
        out_shape=[_sds((t_len, d), BF16), _sds((t_len, nq * nb), F32)],
        args=[x1, gain, w_in], cargos=cargos)


def _layernorm_stats(u1):
    mu = jnp.mean(u1, axis=-1, keepdims=True)
    xc = u1 - mu
    rstd = lax.rsqrt(jnp.mean(xc * xc, axis=-1, keepdims=True) + LN_EPS)
    return rstd, xc * rstd


def _positions(i, tm, rows, offset=0):
    return (lax.broadcasted_iota(jnp.int32, (rows, 1), 0) + (i * tm + offset)).astype(F32)


SHIFT_ROWS = HALO - SUBLANES


def _fill_shifted(ext_s, sh_s, tm):
    for b in range(1, SUBLANES):
        sh_s[b - 1] = ext_s[pl.ds(b, tm + SHIFT_ROWS), :]


def _window(ext_s, sh_s, shift, tm):
    a, b = divmod(shift, SUBLANES)
    if b == 0:
        return ext_s[pl.ds(shift, tm), :]
    return sh_s[b - 1, pl.ds(a * SUBLANES, tm), :]


def _tile(tm, cols):
    return pl.BlockSpec((tm, cols), lambda i: (i, 0))


def _whole(shape):
    return pl.BlockSpec(shape, lambda i: (0,) * len(shape))


def _mix_fwd(proj, x1, conv_dw, conv_b, ln_g, ln_b, conv_pw, pool_w, pool_scale, w_out, name, cargos=()):
    t_len, d = x1.shape
    tm = min(TM_MIX, t_len)
    hb = tm // HALO

    def body(p_ref, tail_ref, x_ref, dw_ref, cb_ref, lg_ref, lb_ref, pw_ref, plw_ref, ps_ref, wo_ref,
             x2_ref, u1_ref, u3_ref, mx_ref, cat_ref, ext_s, pext_s, sh_s):
        i = pl.program_id(0)
        first = i == 0
        a = p_ref[:, 0:D_CONV]
        g = p_ref[:, D_CONV:2 * D_CONV]
        p = p_ref[:, 2 * D_CONV:]
        ta = tail_ref[:, 0:D_CONV]
        tg = tail_ref[:, D_CONV:2 * D_CONV]
        tp = tail_ref[:, 2 * D_CONV:]
        ext_s[0:HALO, :] = jnp.where(first, 0.0, ta * jax.nn.sigmoid(tg))
        ext_s[HALO:, :] = a * jax.nn.sigmoid(g)
        pext_s[0:HALO, :] = jnp.where(first, 0.0, tp)
        pext_s[HALO:, :] = p

        _fill_shifted(ext_s, sh_s, tm)
        u1 = jnp.broadcast_to(cb_ref[...], (tm, D_CONV))
        for k in range(CONV_WIDTH):
            u1 = u1 + dw_ref[k:k + 1, :] * _window(ext_s, sh_s, HALO - (CONV_WIDTH - 1) + k, tm)
        u1_ref[...] = u1
        _, nhat = _layernorm_stats(u1)
        u2 = nhat * lg_ref[...] + lb_ref[...]
        u3 = (u2 * jax.nn.sigmoid(u2)).astype(BF16)
        u3_ref[...] = u3
        cat_ref[:, 0:D_CONV] = _dot(u3, pw_ref[...]).astype(BF16)

        pos1 = _positions(i, tm, tm) + 1.0
        for gi, w in enumerate(POOL_WINDOWS):
            cols = slice(gi * POOL_GROUP, (gi + 1) * POOL_GROUP)
            s = pext_s[pl.ds(HALO, tm), cols]
            for j in range(1, w):
                s = s + pext_s[pl.ds(HALO - j, tm), cols]
            mixed = (s / jnp.minimum(pos1, float(w)) - p[:, cols]).astype(BF16)
            mx_ref[:, cols] = mixed
            out = _dot(mixed, plw_ref[gi]) * ps_ref[:, cols]
            cat_ref[:, D_CONV + gi * POOL_GROUP:D_CONV + (gi + 1) * POOL_GROUP] = out.astype(BF16)

        x2_ref[...] = x_ref[...] + _dot(cat_ref[...], wo_ref[...])

    return _call(
        body, name=name, grid=(t_len // tm,),
        in_specs=[_tile(tm, D_IN), pl.BlockSpec((HALO, D_IN), lambda i: (jnp.maximum(i * hb - 1, 0), 0)),
                  _tile(tm, d), _whole((CONV_WIDTH + 1, D_CONV)), _whole((1, D_CONV)), _whole((1, D_CONV)),
                  _whole((1, D_CONV)), _whole((D_CONV, D_CONV)), _whole((4, POOL_GROUP, POOL_GROUP)),
                  _whole((1, D_POOL)), _whole((D_CONV + D_POOL, d))],
        out_specs=[_tile(tm, d), _tile(tm, D_CONV), _tile(tm, D_CONV), _tile(tm, D_POOL),
                   _tile(tm, D_CONV + D_POOL)],
        out_shape=[_sds((t_len, d), F32), _sds((t_len, D_CONV), F32), _sds((t_len, D_CONV), BF16),
                   _sds((t_len, D_POOL), BF16), _sds((t_len, D_CONV + D_POOL), BF16)],
        scratch_shapes=[pltpu.VMEM((tm + HALO, D_CONV), F32), pltpu.VMEM((tm + HALO, D_POOL), F32),
                        pltpu.VMEM((SUBLANES - 1, tm + SHIFT_ROWS, D_CONV), F32)],
        args=[proj, proj, x1, conv_dw, conv_b, ln_g, ln_b, conv_pw, pool_w, pool_scale, w_out], cargos=cargos)


def _mix_bwd_local(dx2, u1, mixed, ln_g, ln_b, conv_pw, pool_w, pool_scale, w_out, name, cargos=()):
    t_len, d = dx2.shape
    tm = min(TM_MIX, t_len)

    def body(dx_ref, u1_ref, mx_ref, lg_ref, lb_ref, pw_ref, plw_ref, ps_ref, wo_ref,
             du1_ref, dmx_ref, dco_ref, dpo_ref, dlg_ref, dlb_ref, dps_ref):
        @pl.when(pl.program_id(0) == 0)
        def _():
            dlg_ref[...] = jnp.zeros_like(dlg_ref)
            dlb_ref[...] = jnp.zeros_like(dlb_ref)
            dps_ref[...] = jnp.zeros_like(dps_ref)

        dcat = _dot_nt(dx_ref[...].astype(BF16), wo_ref[...])
        dco = dcat[:, 0:D_CONV].astype(BF16)
        dco_ref[...] = dco
        du3 = _dot_nt(dco, pw_ref[...])
        rstd, nhat = _layernorm_stats(u1_ref[...])
        u2 = nhat * lg_ref[...] + lb_ref[...]
        sig = jax.nn.sigmoid(u2)
        du2 = du3 * (sig * (1.0 + u2 * (1.0 - sig)))
        dlg_ref[...] += jnp.sum(du2 * nhat, axis=0, keepdims=True)
        dlb_ref[...] += jnp.sum(du2, axis=0, keepdims=True)
        dnhat = du2 * lg_ref[...]
        du1_ref[...] = rstd * (dnhat - jnp.mean(dnhat, axis=-1, keepdims=True)
                               - nhat * jnp.mean(dnhat * nhat, axis=-1, keepdims=True))

        for gi in range(len(POOL_WINDOWS)):
            cols = slice(gi * POOL_GROUP, (gi + 1) * POOL_GROUP)
            dpo = dcat[:, D_CONV + gi * POOL_GROUP:D_CONV + (gi + 1) * POOL_GROUP]
            pre = _dot(mx_ref[:, cols], plw_ref[gi])
            dps_ref[:, cols] += jnp.sum(dpo * pre, axis=0, keepdims=True)
            dout = (dpo * ps_ref[:, cols]).astype(BF16)
            dpo_ref[:, cols] = dout
            dmx_ref[:, cols] = _dot_nt(dout, plw_ref[gi])

    vec = _whole((1, D_CONV))
    return _call(
        body, name=name, grid=(t_len // tm,),
        in_specs=[_tile(tm, d), _tile(tm, D_CONV), _tile(tm, D_POOL), vec, vec, _whole((D_CONV, D_CONV)),
                  _whole((4, POOL_GROUP, POOL_GROUP)), vec, _whole((D_CONV + D_POOL, d))],
        out_specs=[_tile(tm, D_CONV), _tile(tm, D_POOL), _tile(tm, D_CONV), _tile(tm, D_POOL), vec, vec, vec],
        out_shape=[_sds((t_len, D_CONV), F32), _sds((t_len, D_POOL), F32), _sds((t_len, D_CONV), BF16),
                   _sds((t_len, D_POOL), BF16), _sds((1, D_CONV), F32), _sds((1, D_CONV), F32),
                   _sds((1, D_POOL), F32)],
        args=[dx2, u1, mixed, ln_g, ln_b, conv_pw, pool_w, pool_scale, w_out], cargos=cargos)


def _mix_bwd_seq(du1, dmixed, proj, x1, dx2, gain, conv_dw, w_in, name, cargos=()):
    t_len, d = x1.shape
    nq, _, nb = w_in.shape
    tm = min(TM_MIX, t_len)
    hb = tm // HALO
    last_block = t_len // HALO - 1
    n_tiles = t_len // tm

    def body(du_ref, dun_ref, dm_ref, dmn_ref, p_ref, tail_ref, x_ref, dx2_ref, g_ref, dw_ref, wi_ref,
             dx1_ref, dp_ref, ddw_ref, dcb_ref, dgain_ref, uext_s, dext_s, mext_s, ush_s, dsh_s):
        i = pl.program_id(0)
        first = i == 0
        last = i == n_tiles - 1

        @pl.when(first)
        def _():
            ddw_ref[...] = jnp.zeros_like(ddw_ref)
            dcb_ref[...] = jnp.zeros_like(dcb_ref)
            dgain_ref[...] = jnp.zeros_like(dgain_ref)

        a = p_ref[:, 0:D_CONV]
        g = p_ref[:, D_CONV:2 * D_CONV]
        sg = jax.nn.sigmoid(g)
        ta = tail_ref[:, 0:D_CONV]
        tg = tail_ref[:, D_CONV:2 * D_CONV]
        uext_s[0:HALO, :] = jnp.where(first, 0.0, ta * jax.nn.sigmoid(tg))
        uext_s[HALO:, :] = a * sg
        du1 = du_ref[...]
        dext_s[0:tm, :] = du1
        dext_s[tm:, :] = jnp.where(last, 0.0, dun_ref[...])

        _fill_shifted(uext_s, ush_s, tm)
        _fill_shifted(dext_s, dsh_s, tm)
        du0 = jnp.zeros((tm, D_CONV), F32)
        for k in range(CONV_WIDTH):
            du0 = du0 + dw_ref[k:k + 1, :] * _window(dext_s, dsh_s, CONV_WIDTH - 1 - k, tm)
            ddw_ref[k:k + 1, :] += jnp.sum(
                du1 * _window(uext_s, ush_s, HALO - (CONV_WIDTH - 1) + k, tm), axis=0, keepdims=True)
        dcb_ref[...] += jnp.sum(du1, axis=0, keepdims=True)
        dp_ref[:, 0:D_CONV] = (du0 * sg).astype(BF16)
        dp_ref[:, D_CONV:2 * D_CONV] = (du0 * a * sg * (1.0 - sg)).astype(BF16)

        pos1 = _positions(i, tm, tm) + 1.0
        pos1_next = _positions(i, tm, HALO, offset=tm) + 1.0
        for gi, w in enumerate(POOL_WINDOWS):
            cols = slice(gi * POOL_GROUP, (gi + 1) * POOL_GROUP)
            dm = dm_ref[:, cols]
            mext_s[0:tm, cols] = dm / jnp.minimum(pos1, float(w))
            mext_s[tm:, cols] = jnp.where(last, 0.0, dmn_ref[:, cols] / jnp.minimum(pos1_next, float(w)))
            s = mext_s[pl.ds(0, tm), cols]
            for j in range(1, w):
                s = s + mext_s[pl.ds(j, tm), cols]
            dp_ref[:, 2 * D_CONV + gi * POOL_GROUP:2 * D_CONV + (gi + 1) * POOL_GROUP] = (s - dm).astype(BF16)

        dh = _dot_nt(dp_ref[:, 0:nb], wi_ref[0])
        for q in range(1, nq):
            dh = dh + _dot_nt(dp_ref[:, q * nb:(q + 1) * nb], wi_ref[q])
        r, n = _rms_stats(x_ref[...])
        dgain_ref[...] += jnp.sum(dh * n, axis=0, keepdims=True)
        dx1_ref[...] = dx2_ref[...] + _rms_bwd(dh, n, r, g_ref[...])

    def nxt(cols):
        return pl.BlockSpec((HALO, cols), lambda i: (jnp.minimum((i + 1) * hb, last_block), 0))

    return _call(
        body, name=name, grid=(n_tiles,),
        in_specs=[_tile(tm, D_CONV), nxt(D_CONV), _tile(tm, D_POOL), nxt(D_POOL), _tile(tm, D_IN),
                  pl.BlockSpec((HALO, D_IN), lambda i: (jnp.maximum(i * hb - 1, 0), 0)),
                  _tile(tm, d), _tile(tm, d), _whole((1, d)), _whole((CONV_WIDTH + 1, D_CONV)),
                  _whole((nq, d, nb))],
        out_specs=[_tile(tm, d), _tile(tm, D_IN), _whole((CONV_WIDTH + 1, D_CONV)), _whole((1, D_CONV)),
                   _whole((1, d))],
        out_shape=[_sds((t_len, d), F32), _sds((t_len, D_IN), BF16), _sds((CONV_WIDTH + 1, D_CONV), F32),
                   _sds((1, D_CONV), F32), _sds((1, d), F32)],
        scratch_shapes=[pltpu.VMEM((tm + HALO, D_CONV), F32), pltpu.VMEM((tm + HALO, D_CONV), F32),
                        pltpu.VMEM((tm + HALO, D_POOL), F32),
                        pltpu.VMEM((SUBLANES - 1, tm + SHIFT_ROWS, D_CONV), F32),
                        pltpu.VMEM((SUBLANES - 1, tm + SHIFT_ROWS, D_CONV), F32)],
        args=[du1, du1, dmixed, dmixed, proj, proj, x1, dx2, gain, conv_dw, w_in], cargos=cargos)


def _final_norm_loss(x3, target, gain, name):
    t_len, d = x3.shape
    tm = min(TM_FFN, t_len)

    def body(x_ref, t_ref, g_ref, dx_ref, loss_ref, dgain_ref):
        @pl.when(pl.program_id(0) == 0)
        def _():
            loss_ref[...] = jnp.zeros_like(loss_ref)
            dgain_ref[...] = jnp.zeros_like(dgain_ref)

        r, n = _rms_stats(x_ref[...])
        err = n * g_ref[...] - t_ref[...]
        per_tok = jnp.sum(err * err, axis=-1, keepdims=True) * (1.0 / d)
        loss_ref[...] += 0.5 * jnp.sum(per_tok, axis=0, keepdims=True)
        dy = err * (1.0 / d)
        dgain_ref[...] += jnp.sum(dy * n, axis=0, keepdims=True)
        dx_ref[...] = _rms_bwd(dy, n, r, g_ref[...])

    tok = pl.BlockSpec((tm, d), lambda i: (i, 0))
    outs, _ = _call(
        body, name=name, grid=(t_len // tm,),
        in_specs=[tok, tok, pl.BlockSpec((1, d), lambda i: (0, 0))],
        out_specs=[tok, pl.BlockSpec((1, 128), lambda i: (0, 0)), pl.BlockSpec((1, d), lambda i: (0, 0))],
        out_shape=[_sds((t_len, d), F32), _sds((1, 128), F32), _sds((1, d), F32)],
        args=[x3, target, gain])
    return outs


def _row_tile(rows):
    return rows // 4 if rows % 64 == 0 else rows


def _sum_parts(parts, name):
    n, r, c = parts.shape
    tr = _row_tile(r)

    def body(p_ref, o_ref):
        s = p_ref[0].astype(F32)
        for k in range(1, n):
            s = s + p_ref[k].astype(F32)
        o_ref[...] = s

    (out,), _ = _call(body, name=name, grid=(r // tr,),
                      in_specs=[pl.BlockSpec((n, tr, c), lambda i: (0, i, 0))],
                      out_specs=[pl.BlockSpec((tr, c), lambda i: (i, 0))], out_shape=[_sds((r, c), F32)],
                      args=[parts])
    return out


def _adamw_math(w, g, m, v):
    m = ADAM_B1 * m + (1.0 - ADAM_B1) * g
    v = ADAM_B2 * v + (1.0 - ADAM_B2) * (g * g)
    m_hat = m / (1.0 - ADAM_B1 ** ADAM_STEP)
    v_hat = v / (1.0 - ADAM_B2 ** ADAM_STEP)
    delta = -ADAM_LR * (m_hat / (jnp.sqrt(v_hat) + ADAM_EPS) + ADAM_WD * w)
    return delta, m, v


def _adamw(parts, w, m, v, name):
    r, c = w.shape
    n = len(parts)
    tr = _row_tile(r)

    def body(*refs):
        terms = []
        for p_ref in refs[:n]:
            terms += [p_ref[...]] if len(p_ref.shape) == 2 else [p_ref[k] for k in range(p_ref.shape[0])]
        w_ref, m_ref, v_ref, g_out, d_out, m_out, v_out = refs[n:]
        g = terms[0]
        for t in terms[1:]:
            g = g + t
        delta, nm, nv = _adamw_math(w_ref[...], g, m_ref[...], v_ref[...])
        g_out[...] = g
        d_out[...] = delta
        m_out[...] = nm
        v_out[...] = nv

    blk = pl.BlockSpec((tr, c), lambda i: (i, 0))
    p_specs = [blk if p.ndim == 2 else pl.BlockSpec((p.shape[0], tr, c), lambda i: (0, i, 0)) for p in parts]
    outs, _ = _call(body, name=name, grid=(r // tr,), in_specs=p_specs + [blk, blk, blk],
                    out_specs=[blk] * 4, out_shape=[_sds((r, c), F32)] * 4, args=[*parts, w, m, v])
    return outs


FFN_W = ("w_gate", "w_up", "w_down")
MID = ("w_in", "conv_dw", "conv_pw", "w_out")
SMALL_1024 = ("ffn1_norm", "mix_norm", "ffn2_norm", "final_norm")
SMALL_512 = ("conv_dw_b", "conv_ln_g", "conv_ln_b", "pool_scale")
WEIGHTS = ("ffn1_norm", "ffn1_w_gate", "ffn1_w_up", "ffn1_w_down", "mix_norm", "w_in", "conv_dw", "conv_dw_b",
           "conv_ln_g", "conv_ln_b", "conv_pw", "pool_w", "pool_scale", "w_out", "ffn2_norm", "ffn2_w_gate",
           "ffn2_w_up", "ffn2_w_down", "final_norm")
PACK_ROWS = 72


def _pad_rows(a, rows):
    return jnp.pad(a, ((0, rows - a.shape[0]), (0, 0)))


def _pack_small(t):
    rows = [t[k].reshape(1, D_MODEL) for k in SMALL_1024]
    rows.append(jnp.concatenate([t["conv_dw_b"].reshape(1, -1), t["conv_ln_g"].reshape(1, -1)], axis=1))
    rows.append(jnp.concatenate([t["conv_ln_b"].reshape(1, -1), t["pool_scale"].reshape(1, -1)], axis=1))
    rows.append(t["pool_w"].reshape(64, D_MODEL))
    return _pad_rows(jnp.concatenate(rows, axis=0), PACK_ROWS)


def _unpack_small(p):
    out = {k: p[i] for i, k in enumerate(SMALL_1024)}
    out["conv_dw_b"], out["conv_ln_g"] = p[4, :D_CONV], p[4, D_CONV:]
    out["conv_ln_b"], out["pool_scale"] = p[5, :D_CONV], p[5, D_CONV:]
    out["pool_w"] = p[6:70].reshape(4, POOL_GROUP, POOL_GROUP)
    return out


def _as_stored(name, a):
    if name.endswith(("w_gate", "w_up")):
        return a.T
    if name == "conv_dw":
        return _pad_rows(a, CONV_WIDTH + 1)
    return a


def _as_given(name, a):
    if name.endswith(("w_gate", "w_up")):
        return a.T
    if name == "conv_dw":
        return a[:CONV_WIDTH]
    return a


def kernel(x, ffn1_norm, ffn1_w_gate, ffn1_w_up, ffn1_w_down, mix_norm, w_in, conv_dw, conv_dw_b, conv_ln_g, conv_ln_b, conv_pw, pool_w, pool_scale, w_out, ffn2_norm, ffn2_w_gate, ffn2_w_up, ffn2_w_down, final_norm, loss_target, m_ffn1_norm, m_ffn1_w_gate, m_ffn1_w_up, m_ffn1_w_down, m_mix_norm, m_w_in, m_conv_dw, m_conv_dw_b, m_conv_ln_g, m_conv_ln_b, m_conv_pw, m_pool_w, m_pool_scale, m_w_out, m_ffn2_norm, m_ffn2_w_gate, m_ffn2_w_up, m_ffn2_w_down, m_final_norm, v_ffn1_norm, v_ffn1_w_gate, v_ffn1_w_up, v_ffn1_w_down, v_mix_norm, v_w_in, v_conv_dw, v_conv_dw_b, v_conv_ln_g, v_conv_ln_b, v_conv_pw, v_pool_w, v_pool_scale, v_w_out, v_ffn2_norm, v_ffn2_w_gate, v_ffn2_w_up, v_ffn2_w_down, v_final_norm):
    given = dict(locals())
    wts = {k: given[k] for k in WEIGHTS}
    mom_m = {k: given["m_" + k] for k in WEIGHTS}
    mom_v = {k: given["v_" + k] for k in WEIGHTS}
    xt, target = x[0], loss_target[0]

    shard = {k: _as_stored(k, wts[k]) if k == "conv_dw" else _as_stored(k, wts[k]).astype(BF16)
             for k in WEIGHTS if k.endswith(FFN_W) or k in MID}
    w = {k: wts[k].reshape(1, -1) for k in SMALL_1024 + SMALL_512}
    w["pool_w"] = wts["pool_w"].astype(BF16)

    (h1, s1, p1, a1, w["ffn1_w_gate"], w["ffn1_w_up"]), ((w["ffn1_w_down"],),) = _ffn_up_gather(
        xt, w["ffn1_norm"], shard["ffn1_w_gate"], shard["ffn1_w_up"], "ffn1_up_gather",
        cargos=[Cargo("gather_slots", [shard["ffn1_w_down"]])])
    x1, (mid,) = _ffn_down(xt, a1, w["ffn1_w_down"], "ffn1_down",
                           cargos=[Cargo("gather_chips", [shard[k] for k in MID])])
    w["w_in"] = mid[0]
    w["conv_dw"] = mid[1].transpose(1, 0, 2).reshape(CONV_WIDTH + 1, D_CONV)
    w["conv_pw"] = mid[2].reshape(D_CONV, D_CONV)
    w["w_out"] = mid[3].reshape(D_CONV + D_POOL, D_MODEL)
    (h2, proj), ((w["ffn2_w_down"],),) = _mix_in(x1, w["mix_norm"], w["w_in"], "mix_in",
                                                  cargos=[Cargo("gather_slots", [shard["ffn2_w_down"]])])
    (x2, u1, u3, mixed, cat), ((w["ffn2_w_gate"], w["ffn2_w_up"]),) = _mix_fwd(
        proj, x1, w["conv_dw"], w["conv_dw_b"], w["conv_ln_g"], w["conv_ln_b"], w["conv_pw"], w["pool_w"],
        w["pool_scale"], w["w_out"], "mix_fwd",
        cargos=[Cargo("gather_slots", [shard["ffn2_w_gate"], shard["ffn2_w_up"]])])
    x3, h3, s2, p2, a2 = _ffn_fwd(x2, w["ffn2_norm"], w["ffn2_w_gate"], w["ffn2_w_up"], w["ffn2_w_down"], "ffn2_fwd")
    dx3, loss, d_final = _final_norm_loss(x3, target, w["final_norm"], "final_norm_loss")
    loss = lax.psum(loss[0, 0], ("x", "y", "c"))

    g = {"final_norm": d_final}
    sums = {}

    def landed(names, parts):
        for k, p in zip(names, parts):
            sums[k] = _sum_parts(p, "sum_chips_" + k)

    dx2, g["ffn2_norm"], df2, dg2, du2 = _ffn_bwd(dx3, x2, w["ffn2_norm"], s2, p2, w["ffn2_w_gate"],
                                                   w["ffn2_w_up"], w["ffn2_w_down"], "ffn2_bwd")
    gw, _ = _wgrad_hid_tok(dg2, h3, "ffn2_dw_gate")
    gu, (parts,) = _wgrad_hid_tok(du2, h3, "ffn2_dw_up", cargos=[Cargo("scatter_slots", [gw])])
    landed(["ffn2_w_gate"], parts)
    gd, (parts,) = _wgrad_hid_tok(a2, df2, "ffn2_dw_down", cargos=[Cargo("scatter_slots", [gu])])
    landed(["ffn2_w_up"], parts)
    (du1, dmixed, dco, dpo, g["conv_ln_g"], g["conv_ln_b"], g["pool_scale"]), (parts,) = _mix_bwd_local(
        dx2, u1, mixed, w["conv_ln_g"], w["conv_ln_b"], w["conv_pw"], w["pool_w"], w["pool_scale"], w["w_out"],
        "mix_bwd_local", cargos=[Cargo("scatter_slots", [gd])])
    landed(["ffn2_w_down"], parts)
    g_out, _ = _wgrad_2d(cat, dx2, 1, BF16, "dw_out")
    g_pw, _ = _wgrad_2d(u3, dco, 1, BF16, "dconv_pw")
    g["pool_w"], _ = _wgrad_2d(mixed, dpo, 4, F32, "dpool_w", group_diag=True)
    slabs = [g_pw.reshape(N_CHIPS, D_CONV // N_CHIPS, D_CONV),
             g_out.reshape(N_CHIPS, (D_CONV + D_POOL) // N_CHIPS, D_MODEL)]
    (dx1, dproj, g_dw, g["conv_dw_b"], g["mix_norm"]), (parts, swapped2) = _mix_bwd_seq(
        du1, dmixed, proj, x1, dx2, w["mix_norm"], w["conv_dw"], w["w_in"], "mix_bwd_seq",
        cargos=[Cargo("scatter_chips", slabs),
                Cargo("swap", [sums[k] for k in ("ffn2_w_gate", "ffn2_w_up", "ffn2_w_down")])])
    landed(["conv_pw", "w_out"], parts)
    g_in, _ = _wgrad_2d(h2, dproj, N_CHIPS, BF16, "dw_in")
    dx, g["ffn1_norm"], df1, dg1, du1_ = _ffn_bwd(dx1, xt, w["ffn1_norm"], s1, p1, w["ffn1_w_gate"],
                                                   w["ffn1_w_up"], w["ffn1_w_down"], "ffn1_bwd")
    slabs = [g_in, g_dw.reshape(CONV_WIDTH + 1, N_CHIPS, D_CONV // N_CHIPS).transpose(1, 0, 2)]
    gw, (parts, small_parts) = _wgrad_hid_tok(
        dg1, h1, "ffn1_dw_gate",
        cargos=[Cargo("scatter_chips", slabs), Cargo("gather_devices", [_pack_small(g)])])
    landed(["w_in", "conv_dw"], parts)
    gu, (parts, swapped_mid) = _wgrad_hid_tok(
        du1_, h1, "ffn1_dw_up",
        cargos=[Cargo("scatter_slots", [gw]), Cargo("swap", [sums[k] for k in MID])])
    landed(["ffn1_w_gate"], parts)
    gd, (parts,) = _wgrad_hid_tok(a1, df1, "ffn1_dw_down", cargos=[Cargo("scatter_slots", [gu])])
    landed(["ffn1_w_up"], parts)
    landed(["ffn1_w_down"], _exchange(Cargo("scatter_slots", [gd]), "scatter_last"))
    ffn1 = ["ffn1_" + k for k in FFN_W]
    swapped1 = _exchange(Cargo("swap", [sums[k] for k in ffn1]), "swap_last")

    theirs = dict(zip(["ffn2_" + k for k in FFN_W], swapped2))
    theirs.update(zip(MID, swapped_mid))
    theirs.update(zip(ffn1, swapped1))
    grads, deltas, new_m, new_v = {}, {}, {}, {}
    for k in theirs:
        res = _adamw([sums[k], theirs[k]], _as_stored(k, wts[k]), _as_stored(k, mom_m[k]),
                     _as_stored(k, mom_v[k]), "adamw_" + k)
        grads[k], deltas[k], new_m[k], new_v[k] = [_as_given(k, t) for t in res]
    res = _adamw(small_parts, _pack_small(wts), _pack_small(mom_m), _pack_small(mom_v), "adamw_small")
    for dst, packed in zip((grads, deltas, new_m, new_v), res):
        dst.update(_unpack_small(packed))

    out = [loss, dx[None]]
    for group in (grads, deltas, new_m, new_v):
        out += [group[k] for k in WEIGHTS]
    return tuple(out)
```

```python
import functools

import jax
import jax.numpy as jnp
from jax import lax
from jax.experimental import pallas as pl
from jax.experimental.pallas import tpu as pltpu

F32 = jnp.float32
BF16 = jnp.bfloat16
MESH = pl.DeviceIdType.MESH

N_CHIPS = 4
N_DEV = 8
D_MODEL = 1024
D_CONV = 512
D_POOL = 512
CONV_WIDTH = 31
POOL_WINDOWS = (2, 4, 8, 16)
POOL_GROUP = 128
D_IN = 2 * D_CONV + D_POOL
HALO = 32
RMS_EPS = 1e-6
LN_EPS = 1e-5
FFN_RES_WEIGHT = 0.5
ADAM_LR = 0.001
ADAM_B1 = 0.9
ADAM_B2 = 0.999
ADAM_EPS = 1e-08
ADAM_WD = 0.01
ADAM_STEP = 10
VMEM_LIMIT_BYTES = 52 * 1024 * 1024
TM_FFN = 512
TM_MIX = 256
TT_WGRAD = 2048
STRIP = 16
SUBLANES = 8
RELAY_AT_EIGHTHS = 5

HBM = pl.BlockSpec(memory_space=pl.ANY)


def _dot(a, b):
    return jnp.dot(a, b, preferred_element_type=F32)


def _dot_nt(a, b):
    return lax.dot_general(a, b, (((1,), (1,)), ((), ())), preferred_element_type=F32)


def _dot_tn(a, b):
    return lax.dot_general(a, b, (((0,), (0,)), ((), ())), preferred_element_type=F32)


def _sds(shape, dtype):
    return jax.ShapeDtypeStruct(shape, dtype)


def _rms_stats(xv):
    r = lax.rsqrt(jnp.mean(xv * xv, axis=-1, keepdims=True) + RMS_EPS)
    return r, xv * r


def _swiglu_saved(gate, up):
    sig = jax.nn.sigmoid(gate)
    silu = gate * sig
    return silu, up * (sig * (1.0 + gate * (1.0 - sig))), silu * up


def _rms_bwd(dh, n, r, gain):
    dn = dh * gain
    return r * (dn - n * jnp.mean(dn * n, axis=-1, keepdims=True))


def _place():
    x, y, c = lax.axis_index("x"), lax.axis_index("y"), lax.axis_index("c")
    return x, y, c, [(1 - x, y), (x, 1 - y), (1 - x, 1 - y)]


class Cargo:
    def __init__(self, kind, arrays):
        self.kind, self.arrays = kind, list(arrays)
        n = len(self.arrays)
        self.two_level = kind in ("gather_slots", "gather_chips")
        if self.two_level:
            self.out_shape = [_sds((N_CHIPS,) + a.shape, a.dtype) for a in self.arrays]
        elif kind == "gather_devices":
            self.out_shape = [_sds((N_DEV,) + a.shape, a.dtype) for a in self.arrays]
        else:
            self.out_shape = [_sds(a.shape, a.dtype) for a in self.arrays]
        n_remote = n * {"swap": 1, "gather_devices": N_DEV - 1}.get(kind, N_CHIPS - 1)
        n_own = 0 if kind == "swap" else n
        n_relay = n_remote if self.two_level else 0
        dma = pltpu.SemaphoreType.DMA
        self.scratch = [dma((n_remote,)), dma((n_remote,)), dma((max(n_own, 1),)),
                        dma((max(n_relay, 1),)), dma((max(n_relay, 1),))]

    def _plan(self, ins, outs):
        x, y, c, chips = _place()
        q = 2 * x + y
        sibling = (x, y, 1 - c)
        own, remote, relays = [], [], []
        for a, o in zip(ins, outs):
            if self.two_level:
                half = a.shape[0] // 2
                mine = pl.ds(pl.multiple_of(c * half, SUBLANES), half)
                theirs = pl.ds(pl.multiple_of((1 - c) * half, SUBLANES), half)
                own.append((a, o.at[0 if self.kind == "gather_slots" else q]))
                for j, (px, py) in enumerate(chips):
                    there, here = (j + 1, j + 1) if self.kind == "gather_slots" else (q, 2 * px + py)
                    remote.append((a.at[mine], o.at[there, mine], o.at[here, mine], (px, py, c)))
                    relays.append((o.at[here, mine], o.at[here, mine], o.at[here, theirs], sibling))
            elif self.kind == "scatter_slots":
                own.append((a.at[0], o.at[0]))
                remote += [(a.at[j + 1], o.at[j + 1], o.at[j + 1], (px, py, c)) for j, (px, py) in enumerate(chips)]
            elif self.kind == "scatter_chips":
                own.append((a.at[q], o.at[q]))
                remote += [(a.at[2 * px + py], o.at[q], o.at[2 * px + py], (px, py, c)) for px, py in chips]
            elif self.kind == "swap":
                remote.append((a, o, o, sibling))
            else:
                own.append((a, o.at[4 * x + 2 * y + c]))
                for k in range(1, N_DEV):
                    px, py, pc = x ^ (k >> 2 & 1), y ^ (k >> 1 & 1), c ^ (k & 1)
                    remote.append((a, o.at[4 * x + 2 * y + c], o.at[4 * px + 2 * py + pc], (px, py, pc)))
        return own, remote, relays

    @staticmethod
    def _copies(entries, send_sems, recv_sems):
        out = []
        for k, (src, dst, landed, peer) in enumerate(entries):
            def make(dst_ref, k=k, src=src, peer=peer):
                return pltpu.make_async_remote_copy(src_ref=src, dst_ref=dst_ref, send_sem=send_sems.at[k],
                                                    recv_sem=recv_sems.at[k], device_id=peer, device_id_type=MESH)
            out.append((make(dst), make(landed)))
        return out

    def start(self, ins, outs, sems):
        own, remote, _ = self._plan(ins, outs)
        for k, (src, dst) in enumerate(own):
            pltpu.make_async_copy(src, dst, sems[2].at[k]).start()
        for mine, _ in self._copies(remote, sems[0], sems[1]):
            mine.start()

    def relay(self, ins, outs, sems):
        _, remote, relays = self._plan(ins, outs)
        passed = self._copies(relays, sems[3], sems[4])
        for (_, arriving), (mine, _) in zip(self._copies(remote, sems[0], sems[1]), passed):
            arriving.wait_recv()
            mine.start()

    def wait(self, ins, outs, sems):
        own, remote, relays = self._plan(ins, outs)
        for mine, arriving in self._copies(remote, sems[0], sems[1]):
            mine.wait_send()
            if not self.two_level:
                arriving.wait_recv()
        for mine, arriving in self._copies(relays, sems[3], sems[4]):
            mine.wait_send()
            arriving.wait_recv()
        for k, (src, dst) in enumerate(own):
            pltpu.make_async_copy(src, dst, sems[2].at[k]).wait()


N_CARGO_SEMS = 5


def _call(body, *, name, grid, in_specs, out_specs, out_shape, args, scratch_shapes=(), cargos=()):
    n_in, n_out, n_scr = len(in_specs), len(out_specs), len(scratch_shapes)
    c_in = [len(cg.arrays) for cg in cargos]
    n_cin = sum(c_in)

    def wrapped(*refs):
        ins = refs[:n_in]
        cins = refs[n_in:n_in + n_cin]
        outs = refs[n_in + n_cin:n_in + n_cin + n_out]
        couts = refs[n_in + n_cin + n_out:n_in + 2 * n_cin + n_out]
        scr = refs[n_in + 2 * n_cin + n_out:n_in + 2 * n_cin + n_out + n_scr]
        sems = refs[n_in + 2 * n_cin + n_out + n_scr:]
        step, n_steps = 0, 1
        for ax, size in enumerate(grid):
            step = step * size + pl.program_id(ax)
            n_steps *= size

        def each(method, only_two_level=False):
            at = 0
            for k, cg in enumerate(cargos):
                if cg.two_level or not only_two_level:
                    getattr(cg, method)(cins[at:at + c_in[k]], couts[at:at + c_in[k]],
                                        sems[N_CARGO_SEMS * k:N_CARGO_SEMS * (k + 1)])
                at += c_in[k]

        if cargos:
            pl.when(step == 0)(lambda: each("start"))
        body(*ins, *outs, *scr)
        if any(cg.two_level for cg in cargos):
            pl.when(step == (RELAY_AT_EIGHTHS * n_steps) // 8)(lambda: each("relay", only_two_level=True))
        if cargos:
            pl.when(step == n_steps - 1)(lambda: each("wait"))

    res = pl.pallas_call(
        wrapped, name=name, grid=grid,
        in_specs=list(in_specs) + [HBM] * n_cin,
        out_specs=list(out_specs) + [HBM] * n_cin,
        out_shape=list(out_shape) + [s for cg in cargos for s in cg.out_shape],
        scratch_shapes=list(scratch_shapes) + [s for cg in cargos for s in cg.scratch],
        compiler_params=pltpu.CompilerParams(dimension_semantics=("arbitrary",) * len(grid),
                                             vmem_limit_bytes=VMEM_LIMIT_BYTES),
    )(*args, *[a for cg in cargos for a in cg.arrays])
    outs, rest = list(res[:n_out]), list(res[n_out:])
    cargo_outs = []
    for k in c_in:
        cargo_outs.append(rest[:k])
        rest = rest[k:]
    return outs, cargo_outs


def _exchange(cargo, name):
    _, (outs,) = _call(lambda: None, name=name, grid=(1,), in_specs=[], out_specs=[], out_shape=[], args=[],
                       cargos=[cargo])
    return outs


def _ffn_up_gather(x, gain, wg_t, wu_t, name, cargos=()):
    t_len, d = x.shape
    fq = wg_t.shape[0]
    tm = min(TM_FFN, t_len)
    n_tiles = t_len // tm

    def body(x_ref, g_ref, wg_in, wu_in, h_ref, s_ref, p_ref, a_ref, wg_all, wu_all,
             wg_v, wu_v, h_all, send_sems, recv_sems, pass_send_sems, pass_recv_sems, own_sems, load_sems):
        s = pl.program_id(0)
        i = pl.program_id(1)
        x_, y_, c_, chips = _place()
        shards = ((wg_in, wg_all, wg_v), (wu_in, wu_all, wu_v))
        mine = pl.ds(pl.multiple_of(c_ * (fq // 2), SUBLANES), fq // 2)
        theirs = pl.ds(pl.multiple_of((1 - c_) * (fq // 2), SUBLANES), fq // 2)

        def to_peer(k, j):
            w_in, w_all, _ = shards[k]
            return pltpu.make_async_remote_copy(
                src_ref=w_in.at[mine], dst_ref=w_all.at[j + 1, mine], send_sem=send_sems.at[3 * k + j],
                recv_sem=recv_sems.at[3 * k + j], device_id=(*chips[j], c_), device_id_type=MESH)

        def to_sibling(k, j, landing=False):
            w_all = shards[k][1]
            return pltpu.make_async_remote_copy(
                src_ref=w_all.at[j + 1, mine], dst_ref=w_all.at[j + 1, theirs if landing else mine],
                send_sem=pass_send_sems.at[3 * k + j], recv_sem=pass_recv_sems.at[3 * k + j],
                device_id=(x_, y_, 1 - c_), device_id_type=MESH)

        def keep(k):
            return pltpu.make_async_copy(shards[k][0], shards[k][1].at[0], own_sems.at[k])

        @pl.when((s == 0) & (i == 0))
        def _():
            for k in range(2):
                for j in range(N_CHIPS - 1):
                    to_peer(k, j).start()
                keep(k).start()

        for slot in range(N_CHIPS):
            @pl.when((s == slot) & (i == 0))
            def _():
                if slot > 0:
                    for k in range(2):
                        to_peer(k, slot - 1).wait_recv()
                        to_sibling(k, slot - 1).start()
                    for k in range(2):
                        to_sibling(k, slot - 1, landing=True).wait_recv()
                loads = []
                for k in range(2):
                    src = shards[k][0] if slot == 0 else shards[k][1].at[slot]
                    loads.append(pltpu.make_async_copy(src, shards[k][2], load_sems.at[k]))
                    loads[-1].start()
                for ld in loads:
                    ld.wait()

        @pl.when(s == 0)
        def _():
            _, n = _rms_stats(x_ref[...])
            h_new = (n * g_ref[...]).astype(BF16)
            h_ref[...] = h_new
            h_all[i] = h_new

        h = h_all[i]
        silu, dgate, act = _swiglu_saved(_dot_nt(h, wg_v[...]), _dot_nt(h, wu_v[...]))
        s_ref[...] = silu.astype(BF16)
        p_ref[...] = dgate.astype(BF16)
        a_ref[...] = act.astype(BF16)

        @pl.when((s == N_CHIPS - 1) & (i == n_tiles - 1))
        def _():
            for k in range(2):
                for j in range(N_CHIPS - 1):
                    to_peer(k, j).wait_send()
                    to_sibling(k, j).wait_send()
                keep(k).wait()

    tok = pl.BlockSpec((tm, d), lambda s, i: (jnp.where(s == 0, i, n_tiles - 1), 0))
    hid = pl.BlockSpec((None, tm, fq), lambda s, i: (s, i, 0))
    outs, cargo_outs = _call(
        body, name=name, grid=(N_CHIPS, n_tiles),
        in_specs=[tok, pl.BlockSpec((1, d), lambda s, i: (0, 0)), HBM, HBM],
        out_specs=[tok, hid, hid, hid, HBM, HBM],
        out_shape=[_sds((t_len, d), BF16)] + [_sds((N_CHIPS, t_len, fq), BF16)] * 3
        + [_sds((N_CHIPS, fq, d), BF16)] * 2,
        scratch_shapes=[pltpu.VMEM((fq, d), BF16), pltpu.VMEM((fq, d), BF16), pltpu.VMEM((n_tiles, tm, d), BF16)]
        + [pltpu.SemaphoreType.DMA((6,))] * 4 + [pltpu.SemaphoreType.DMA((2,))] * 2,
        args=[x, gain, wg_t, wu_t], cargos=cargos)
    return outs, cargo_outs


def _load_once(hbm_refs, vmem_refs, sems, first):
    @pl.when(first)
    def _():
        copies = [pltpu.make_async_copy(src, dst, sems.at[k]) for k, (src, dst) in enumerate(zip(hbm_refs, vmem_refs))]
        for cp in copies:
            cp.start()
        for cp in copies:
            cp.wait()


def _ffn_down(x, act, wd, name, cargos=()):
    t_len, d = x.shape
    nq, fq, _ = wd.shape
    tm = min(TM_FFN, t_len)

    def body(x_ref, a_ref, wd_ref, xo_ref):
        y = _dot(a_ref[0], wd_ref[0])
        for j in range(1, nq):
            y = y + _dot(a_ref[j], wd_ref[j])
        xo_ref[...] = x_ref[...] + FFN_RES_WEIGHT * y

    tok = pl.BlockSpec((tm, d), lambda i: (i, 0))
    (xo,), cargo_outs = _call(
        body, name=name, grid=(t_len // tm,),
        in_specs=[tok, pl.BlockSpec((nq, tm, fq), lambda i: (0, i, 0)), pl.BlockSpec((nq, fq, d), lambda i: (0, 0, 0))],
        out_specs=[tok], out_shape=[_sds((t_len, d), F32)], args=[x, act, wd], cargos=cargos)
    return xo, cargo_outs


def _ffn_fwd(x, gain, wg_t, wu_t, wd, name):
    t_len, d = x.shape
    nq, fq, _ = wd.shape
    tm = min(TM_FFN, t_len)

    def body(x_ref, g_ref, wg_hbm, wu_hbm, wd_hbm, xo_ref, h_ref, s_ref, p_ref, a_ref,
             h_s, acc, wg_v, wu_v, wd_v, load_sems):
        i = pl.program_id(0)
        j = pl.program_id(1)
        _load_once((wg_hbm, wu_hbm, wd_hbm), (wg_v, wu_v, wd_v), load_sems, (i == 0) & (j == 0))

        @pl.when(j == 0)
        def _():
            _, n = _rms_stats(x_ref[...])
            h = (n * g_ref[...]).astype(BF16)
            h_s[...] = h
            h_ref[...] = h
            acc[...] = jnp.zeros_like(acc)

        h = h_s[...]
        silu, dgate, act = _swiglu_saved(_dot_nt(h, wg_v[j]), _dot_nt(h, wu_v[j]))
        s_ref[...] = silu.astype(BF16)
        p_ref[...] = dgate.astype(BF16)
        a_ref[...] = act.astype(BF16)
        acc[...] += _dot(a_ref[...], wd_v[j])

        @pl.when(j == nq - 1)
        def _():
            xo_ref[...] = x_ref[...] + FFN_RES_WEIGHT * acc[...]

    tok = pl.BlockSpec((tm, d), lambda i, j: (i, 0))
    hid = pl.BlockSpec((None, tm, fq), lambda i, j: (j, i, 0))
    outs, _ = _call(
        body, name=name, grid=(t_len // tm, nq),
        in_specs=[tok, pl.BlockSpec((1, d), lambda i, j: (0, 0)), HBM, HBM, HBM],
        out_specs=[tok, tok, hid, hid, hid],
        out_shape=[_sds((t_len, d), F32), _sds((t_len, d), BF16)] + [_sds((nq, t_len, fq), BF16)] * 3,
        scratch_shapes=[pltpu.VMEM((tm, d), BF16), pltpu.VMEM((tm, d), F32)]
        + [pltpu.VMEM((nq, fq, d), BF16)] * 3 + [pltpu.SemaphoreType.DMA((3,))],
        args=[x, gain, wg_t, wu_t, wd])
    return outs


def _ffn_bwd(dy, x_in, gain, silu, dgate_du, wg_t, wu_t, wd, name):
    t_len, d = dy.shape
    nq, fq, _ = wd.shape
    tm = min(TM_FFN, t_len)

    def body(dy_ref, x_ref, g_ref, s_ref, p_ref, wg_hbm, wu_hbm, wd_hbm,
             dx_ref, dgain_ref, df_ref, dg_ref, du_ref, df_s, dh_acc, dact_s, wg_v, wu_v, wd_v, load_sems):
        i = pl.program_id(0)
        j = pl.program_id(1)
        _load_once((wg_hbm, wu_hbm, wd_hbm), (wg_v, wu_v, wd_v), load_sems, (i == 0) & (j == 0))

        @pl.when((i == 0) & (j == 0))
        def _():
            dgain_ref[...] = jnp.zeros_like(dgain_ref)

        @pl.when(j == 0)
        def _():
            df = (FFN_RES_WEIGHT * dy_ref[...]).astype(BF16)
            df_s[...] = df
            df_ref[...] = df
            dh_acc[...] = jnp.zeros_like(dh_acc)

        half = tm // 2
        for r0 in (0, half):
            dact_s[r0:r0 + half, :] = _dot_nt(df_s[r0:r0 + half, :], wd_v[j])

        for r0 in range(0, tm, STRIP):
            dact = dact_s[r0:r0 + STRIP, :]
            dg_ref[r0:r0 + STRIP, :] = (dact * p_ref[r0:r0 + STRIP, :].astype(F32)).astype(BF16)
            du_ref[r0:r0 + STRIP, :] = (dact * s_ref[r0:r0 + STRIP, :].astype(F32)).astype(BF16)

        for r0 in (0, half):
            rows = slice(r0, r0 + half)
            dh_acc[rows, :] += _dot(dg_ref[rows, :], wg_v[j]) + _dot(du_ref[rows, :], wu_v[j])

        @pl.when(j == nq - 1)
        def _():
            r, n = _rms_stats(x_ref[...])
            dh = dh_acc[...]
            dgain_ref[...] += jnp.sum(dh * n, axis=0, keepdims=True)
            dx_ref[...] = dy_ref[...] + _rms_bwd(dh, n, r, g_ref[...])

    tok = pl.BlockSpec((tm, d), lambda i, j: (i, 0))
    vec = pl.BlockSpec((1, d), lambda i, j: (0, 0))
    hid = pl.BlockSpec((None, tm, fq), lambda i, j: (j, i, 0))
    outs, _ = _call(
        body, name=name, grid=(t_len // tm, nq),
        in_specs=[tok, tok, vec, hid, hid, HBM, HBM, HBM],
        out_specs=[tok, vec, tok, hid, hid],
        out_shape=[_sds((t_len, d), F32), _sds((1, d), F32), _sds((t_len, d), BF16),
                   _sds((nq, t_len, fq), BF16), _sds((nq, t_len, fq), BF16)],
        scratch_shapes=[pltpu.VMEM((tm, d), BF16), pltpu.VMEM((tm, d), F32), pltpu.VMEM((tm, fq), F32)]
        + [pltpu.VMEM((nq, fq, d), BF16)] * 3 + [pltpu.SemaphoreType.DMA((3,))],
        args=[dy, x_in, gain, silu, dgate_du, wg_t, wu_t, wd])
    return outs


def _wgrad(lhs, rhs, l_spec, r_spec, out_shape, out_spec, acc_shape, grid, name, cargos=()):
    n_t = grid[-1]
    t_axis = len(grid) - 1

    def body(l_ref, r_ref, o_ref, acc):
        t = pl.program_id(t_axis)

        @pl.when(t == 0)
        def _():
            acc[...] = jnp.zeros_like(acc)

        acc[...] += _dot_tn(l_ref[...].astype(BF16), r_ref[...].astype(BF16))

        @pl.when(t == n_t - 1)
        def _():
            o_ref[...] = acc[...].astype(o_ref.dtype)

    (out,), cargo_outs = _call(
        body, name=name, grid=grid, in_specs=[l_spec, r_spec], out_specs=[out_spec], out_shape=[out_shape],
        scratch_shapes=[pltpu.VMEM(acc_shape, F32)], args=[lhs, rhs], cargos=cargos)
    return out, cargo_outs


def _wgrad_hid_tok(hid, tok, name, cargos=()):
    t_len, d = tok.shape
    nq, _, fq = hid.shape
    tt = min(TT_WGRAD, t_len)
    return _wgrad(hid, tok,
                  pl.BlockSpec((None, tt, fq), lambda q, t: (q, t, 0)),
                  pl.BlockSpec((tt, d), lambda q, t: (t, 0)),
                  _sds((nq, fq, d), BF16), pl.BlockSpec((None, fq, d), lambda q, t: (q, 0, 0)),
                  (fq, d), (nq, t_len // tt), name, cargos)


def _wgrad_hid_tok_scatter(hid, tok, name, cargos=()):
    t_len, d = tok.shape
    nq, _, fq = hid.shape
    tt = min(TT_WGRAD, t_len)
    n_t = t_len // tt

    def body(l_ref, r_ref, parts_ref, acc, stage, send_sems, recv_sems, own_sem):
        g = pl.program_id(0)
        t = pl.program_id(1)
        x_, y_, c_, chips = _place()

        def to_peer(j):
            return pltpu.make_async_remote_copy(
                src_ref=stage.at[j + 1], dst_ref=parts_ref.at[j + 1], send_sem=send_sems.at[j],
                recv_sem=recv_sems.at[j], device_id=(*chips[j], c_), device_id_type=MESH)

        keep = pltpu.make_async_copy(stage.at[0], parts_ref.at[0], own_sem)

        @pl.when(t == 0)
        def _():
            acc[...] = jnp.zeros_like(acc)

        acc[...] += _dot_tn(l_ref[...], r_ref[...])

        for j in range(N_CHIPS - 1):
            @pl.when((g == j) & (t == n_t - 1))
            def _():
                stage[j + 1] = acc[...].astype(BF16)
                to_peer(j).start()

        @pl.when((g == N_CHIPS - 1) & (t == n_t - 1))
        def _():
            stage[0] = acc[...].astype(BF16)
            keep.start()
            for j in range(N_CHIPS - 1):
                to_peer(j).wait()
            keep.wait()

    def slot(g):
        return (g + 1) % N_CHIPS

    (parts,), cargo_outs = _call(
        body, name=name, grid=(nq, n_t),
        in_specs=[pl.BlockSpec((None, tt, fq), lambda g, t: (slot(g), t, 0)), pl.BlockSpec((tt, d), lambda g, t: (t, 0))],
        out_specs=[HBM], out_shape=[_sds((nq, fq, d), BF16)],
        scratch_shapes=[pltpu.VMEM((fq, d), F32), pltpu.VMEM((nq, fq, d), BF16),
                        pltpu.SemaphoreType.DMA((N_CHIPS - 1,)), pltpu.SemaphoreType.DMA((N_CHIPS - 1,)),
                        pltpu.SemaphoreType.DMA(())],
        args=[hid, tok], cargos=cargos)
    return parts, cargo_outs


def _wgrad_2d(lhs, rhs, n_col_blocks, out_dtype, name, group_diag=False, cargos=()):
    t_len, k = lhs.shape
    n = rhs.shape[1]
    nb = n // n_col_blocks
    kb = k // n_col_blocks if group_diag else k
    tt = min(TT_WGRAD, t_len)
    l_map = (lambda q, t: (t, q)) if group_diag else (lambda q, t: (t, 0))
    return _wgrad(lhs, rhs,
                  pl.BlockSpec((tt, kb), l_map),
                  pl.BlockSpec((tt, nb), lambda q, t: (t, q)),
                  _sds((n_col_blocks, kb, nb), out_dtype),
                  pl.BlockSpec((None, kb, nb), lambda q, t: (q, 0, 0)),
                  (kb, nb), (n_col_blocks, t_len // tt), name, cargos)


def _mix_in(x1, gain, w_in, name, cargos=()):
    t_len, d = x1.shape
    nq, _, nb = w_in.shape
    tm = min(TM_FFN, t_len)

    def body(x_ref, g_ref, w_ref, h_ref, p_ref):
        _, n = _rms_stats(x_ref[...])
        h = (n * g_ref[...]).astype(BF16)
        h_ref[...] = h
        for q in range(nq):
            p_ref[:, q * nb:(q + 1) * nb] = _dot(h, w_ref[q])

    return _call(
        body, name=name, grid=(t_len // tm,),
        in_specs=[pl.BlockSpec((tm, d), lambda i: (i, 0)), pl.BlockSpec((1, d), lambda i: (0, 0)),
                  pl.BlockSpec((nq, d, nb), lambda i: (0, 0, 0))],
        out_specs=[pl.BlockSpec((tm, d), lambda i: (i, 0)), pl.BlockSpec((tm, nq * nb), lambda i: (i, 0))],
        out_shape=[_sds((t_len, d), BF16), _sds((t_len, nq * nb), F32)],
        args=[x1, gain, w_in], cargos=cargos)


def _layernorm_stats(u1):
    mu = jnp.mean(u1, axis=-1, keepdims=True)
    xc = u1 - mu
    rstd = lax.rsqrt(jnp.mean(xc * xc, axis=-1, keepdims=True) + LN_EPS)
    return rstd, xc * rstd


def _positions(i, tm, rows, offset=0):
    return (lax.broadcasted_iota(jnp.int32, (rows, 1), 0) + (i * tm + offset)).astype(F32)


SHIFT_ROWS = HALO - SUBLANES


def _fill_shifted(ext_s, sh_s, tm):
    for b in range(1, SUBLANES):
        sh_s[b - 1] = ext_s[pl.ds(b, tm + SHIFT_ROWS), :]


def _window(ext_s, sh_s, shift, tm):
    a, b = divmod(shift, SUBLANES)
    if b == 0:
        return ext_s[pl.ds(shift, tm), :]
    return sh_s[b - 1, pl.ds(a * SUBLANES, tm), :]


def _tile(tm, cols):
    return pl.BlockSpec((tm, cols), lambda i: (i, 0))


def _whole(shape):
    return pl.BlockSpec(shape, lambda i: (0,) * len(shape))


def _mix_fwd(proj, x1, conv_dw, conv_b, ln_g, ln_b, conv_pw, pool_w, pool_scale, w_out, name, cargos=()):
    t_len, d = x1.shape
    tm = min(TM_MIX, t_len)
    hb = tm // HALO

    def body(p_ref, tail_ref, x_ref, dw_ref, cb_ref, lg_ref, lb_ref, pw_ref, plw_ref, ps_ref, wo_ref,
             x2_ref, u1_ref, u3_ref, mx_ref, cat_ref, ext_s, pext_s, sh_s):
        i = pl.program_id(0)
        first = i == 0
        a = p_ref[:, 0:D_CONV]
        g = p_ref[:, D_CONV:2 * D_CONV]
        p = p_ref[:, 2 * D_CONV:]
        ta = tail_ref[:, 0:D_CONV]
        tg = tail_ref[:, D_CONV:2 * D_CONV]
        tp = tail_ref[:, 2 * D_CONV:]
        ext_s[0:HALO, :] = jnp.where(first, 0.0, ta * jax.nn.sigmoid(tg))
        ext_s[HALO:, :] = a * jax.nn.sigmoid(g)
        pext_s[0:HALO, :] = jnp.where(first, 0.0, tp)
        pext_s[HALO:, :] = p

        _fill_shifted(ext_s, sh_s, tm)
        u1 = jnp.broadcast_to(cb_ref[...], (tm, D_CONV))
        for k in range(CONV_WIDTH):
            u1 = u1 + dw_ref[k:k + 1, :] * _window(ext_s, sh_s, HALO - (CONV_WIDTH - 1) + k, tm)
        u1_ref[...] = u1
        _, nhat = _layernorm_stats(u1)
        u2 = nhat * lg_ref[...] + lb_ref[...]
        u3 = (u2 * jax.nn.sigmoid(u2)).astype(BF16)
        u3_ref[...] = u3
        cat_ref[:, 0:D_CONV] = _dot(u3, pw_ref[...]).astype(BF16)

        pos1 = _positions(i, tm, tm) + 1.0
        for gi, w in enumerate(POOL_WINDOWS):
            cols = slice(gi * POOL_GROUP, (gi + 1) * POOL_GROUP)
            s = pext_s[pl.ds(HALO, tm), cols]
            for j in range(1, w):
                s = s + pext_s[pl.ds(HALO - j, tm), cols]
            mixed = (s / jnp.minimum(pos1, float(w)) - p[:, cols]).astype(BF16)
            mx_ref[:, cols] = mixed
            out = _dot(mixed, plw_ref[gi]) * ps_ref[:, cols]
            cat_ref[:, D_CONV + gi * POOL_GROUP:D_CONV + (gi + 1) * POOL_GROUP] = out.astype(BF16)

        x2_ref[...] = x_ref[...] + _dot(cat_ref[...], wo_ref[...])

    return _call(
        body, name=name, grid=(t_len // tm,),
        in_specs=[_tile(tm, D_IN), pl.BlockSpec((HALO, D_IN), lambda i: (jnp.maximum(i * hb - 1, 0), 0)),
                  _tile(tm, d), _whole((CONV_WIDTH + 1, D_CONV)), _whole((1, D_CONV)), _whole((1, D_CONV)),
                  _whole((1, D_CONV)), _whole((D_CONV, D_CONV)), _whole((4, POOL_GROUP, POOL_GROUP)),
                  _whole((1, D_POOL)), _whole((D_CONV + D_POOL, d))],
        out_specs=[_tile(tm, d), _tile(tm, D_CONV), _tile(tm, D_CONV), _tile(tm, D_POOL),
                   _tile(tm, D_CONV + D_POOL)],
        out_shape=[_sds((t_len, d), F32), _sds((t_len, D_CONV), F32), _sds((t_len, D_CONV), BF16),
                   _sds((t_len, D_POOL), BF16), _sds((t_len, D_CONV + D_POOL), BF16)],
        scratch_shapes=[pltpu.VMEM((tm + HALO, D_CONV), F32), pltpu.VMEM((tm + HALO, D_POOL), F32),
                        pltpu.VMEM((SUBLANES - 1, tm + SHIFT_ROWS, D_CONV), F32)],
        args=[proj, proj, x1, conv_dw, conv_b, ln_g, ln_b, conv_pw, pool_w, pool_scale, w_out], cargos=cargos)


def _mix_bwd_local(dx2, u1, mixed, ln_g, ln_b, conv_pw, pool_w, pool_scale, w_out, name, cargos=()):
    t_len, d = dx2.shape
    tm = min(TM_MIX, t_len)

    def body(dx_ref, u1_ref, mx_ref, lg_ref, lb_ref, pw_ref, plw_ref, ps_ref, wo_ref,
             du1_ref, dmx_ref, dco_ref, dpo_ref, dlg_ref, dlb_ref, dps_ref):
        @pl.when(pl.program_id(0) == 0)
        def _():
            dlg_ref[...] = jnp.zeros_like(dlg_ref)
            dlb_ref[...] = jnp.zeros_like(dlb_ref)
            dps_ref[...] = jnp.zeros_like(dps_ref)

        dcat = _dot_nt(dx_ref[...].astype(BF16), wo_ref[...])
        dco = dcat[:, 0:D_CONV].astype(BF16)
        dco_ref[...] = dco
        du3 = _dot_nt(dco, pw_ref[...])
        rstd, nhat = _layernorm_stats(u1_ref[...])
        u2 = nhat * lg_ref[...] + lb_ref[...]
        sig = jax.nn.sigmoid(u2)
        du2 = du3 * (sig * (1.0 + u2 * (1.0 - sig)))
        dlg_ref[...] += jnp.sum(du2 * nhat, axis=0, keepdims=True)
        dlb_ref[...] += jnp.sum(du2, axis=0, keepdims=True)
        dnhat = du2 * lg_ref[...]
        du1_ref[...] = rstd * (dnhat - jnp.mean(dnhat, axis=-1, keepdims=True)
                               - nhat * jnp.mean(dnhat * nhat, axis=-1, keepdims=True))

        for gi in range(len(POOL_WINDOWS)):
            cols = slice(gi * POOL_GROUP, (gi + 1) * POOL_GROUP)
            dpo = dcat[:, D_CONV + gi * POOL_GROUP:D_CONV + (gi + 1) * POOL_GROUP]
            pre = _dot(mx_ref[:, cols], plw_ref[gi])
            dps_ref[:, cols] += jnp.sum(dpo * pre, axis=0, keepdims=True)
            dout = (dpo * ps_ref[:, cols]).astype(BF16)
            dpo_ref[:, cols] = dout
            dmx_ref[:, cols] = _dot_nt(dout, plw_ref[gi])

    vec = _whole((1, D_CONV))
    return _call(
        body, name=name, grid=(t_len // tm,),
        in_specs=[_tile(tm, d), _tile(tm, D_CONV), _tile(tm, D_POOL), vec, vec, _whole((D_CONV, D_CONV)),
                  _whole((4, POOL_GROUP, POOL_GROUP)), vec, _whole((D_CONV + D_POOL, d))],
        out_specs=[_tile(tm, D_CONV), _tile(tm, D_POOL), _tile(tm, D_CONV), _tile(tm, D_POOL), vec, vec, vec],
        out_shape=[_sds((t_len, D_CONV), F32), _sds((t_len, D_POOL), F32), _sds((t_len, D_CONV), BF16),
                   _sds((t_len, D_POOL), BF16), _sds((1, D_CONV), F32), _sds((1, D_CONV), F32),
                   _sds((1, D_POOL), F32)],
        args=[dx2, u1, mixed, ln_g, ln_b, conv_pw, pool_w, pool_scale, w_out], cargos=cargos)


def _mix_bwd_seq(du1, dmixed, proj, x1, dx2, gain, conv_dw, w_in, name, cargos=()):
    t_len, d = x1.shape
    nq, _, nb = w_in.shape
    tm = min(TM_MIX, t_len)
    hb = tm // HALO
    last_block = t_len // HALO - 1
    n_tiles = t_len // tm

    def body(du_ref, dun_ref, dm_ref, dmn_ref, p_ref, tail_ref, x_ref, dx2_ref, g_ref, dw_ref, wi_ref,
             dx1_ref, dp_ref, ddw_ref, dcb_ref, dgain_ref, uext_s, dext_s, mext_s, ush_s, dsh_s):
        i = pl.program_id(0)
        first = i == 0
        last = i == n_tiles - 1

        @pl.when(first)
        def _():
            ddw_ref[...] = jnp.zeros_like(ddw_ref)
            dcb_ref[...] = jnp.zeros_like(dcb_ref)
            dgain_ref[...] = jnp.zeros_like(dgain_ref)

        a = p_ref[:, 0:D_CONV]
        g = p_ref[:, D_CONV:2 * D_CONV]
        sg = jax.nn.sigmoid(g)
        ta = tail_ref[:, 0:D_CONV]
        tg = tail_ref[:, D_CONV:2 * D_CONV]
        uext_s[0:HALO, :] = jnp.where(first, 0.0, ta * jax.nn.sigmoid(tg))
        uext_s[HALO:, :] = a * sg
        du1 = du_ref[...]
        dext_s[0:tm, :] = du1
        dext_s[tm:, :] = jnp.where(last, 0.0, dun_ref[...])

        _fill_shifted(uext_s, ush_s, tm)
        _fill_shifted(dext_s, dsh_s, tm)
        du0 = jnp.zeros((tm, D_CONV), F32)
        for k in range(CONV_WIDTH):
            du0 = du0 + dw_ref[k:k + 1, :] * _window(dext_s, dsh_s, CONV_WIDTH - 1 - k, tm)
            ddw_ref[k:k + 1, :] += jnp.sum(
                du1 * _window(uext_s, ush_s, HALO - (CONV_WIDTH - 1) + k, tm), axis=0, keepdims=True)
        dcb_ref[...] += jnp.sum(du1, axis=0, keepdims=True)
        dp_ref[:, 0:D_CONV] = (du0 * sg).astype(BF16)
        dp_ref[:, D_CONV:2 * D_CONV] = (du0 * a * sg * (1.0 - sg)).astype(BF16)

        pos1 = _positions(i, tm, tm) + 1.0
        pos1_next = _positions(i, tm, HALO, offset=tm) + 1.0
        for gi, w in enumerate(POOL_WINDOWS):
            cols = slice(gi * POOL_GROUP, (gi + 1) * POOL_GROUP)
            dm = dm_ref[:, cols]
            mext_s[0:tm, cols] = dm / jnp.minimum(pos1, float(w))
            mext_s[tm:, cols] = jnp.where(last, 0.0, dmn_ref[:, cols] / jnp.minimum(pos1_next, float(w)))
            s = mext_s[pl.ds(0, tm), cols]
            for j in range(1, w):
                s = s + mext_s[pl.ds(j, tm), cols]
            dp_ref[:, 2 * D_CONV + gi * POOL_GROUP:2 * D_CONV + (gi + 1) * POOL_GROUP] = (s - dm).astype(BF16)

        dh = _dot_nt(dp_ref[:, 0:nb], wi_ref[0])
        for q in range(1, nq):
            dh = dh + _dot_nt(dp_ref[:, q * nb:(q + 1) * nb], wi_ref[q])
        r, n = _rms_stats(x_ref[...])
        dgain_ref[...] += jnp.sum(dh * n, axis=0, keepdims=True)
        dx1_ref[...] = dx2_ref[...] + _rms_bwd(dh, n, r, g_ref[...])

    def nxt(cols):
        return pl.BlockSpec((HALO, cols), lambda i: (jnp.minimum((i + 1) * hb, last_block), 0))

    return _call(
        body, name=name, grid=(n_tiles,),
        in_specs=[_tile(tm, D_CONV), nxt(D_CONV), _tile(tm, D_POOL), nxt(D_POOL), _tile(tm, D_IN),
                  pl.BlockSpec((HALO, D_IN), lambda i: (jnp.maximum(i * hb - 1, 0), 0)),
                  _tile(tm, d), _tile(tm, d), _whole((1, d)), _whole((CONV_WIDTH + 1, D_CONV)),
                  _whole((nq, d, nb))],
        out_specs=[_tile(tm, d), _tile(tm, D_IN), _whole((CONV_WIDTH + 1, D_CONV)), _whole((1, D_CONV)),
                   _whole((1, d))],
        out_shape=[_sds((t_len, d), F32), _sds((t_len, D_IN), BF16), _sds((CONV_WIDTH + 1, D_CONV), F32),
                   _sds((1, D_CONV), F32), _sds((1, d), F32)],
        scratch_shapes=[pltpu.VMEM((tm + HALO, D_CONV), F32), pltpu.VMEM((tm + HALO, D_CONV), F32),
                        pltpu.VMEM((tm + HALO, D_POOL), F32),
                        pltpu.VMEM((SUBLANES - 1, tm + SHIFT_ROWS, D_CONV), F32),
                        pltpu.VMEM((SUBLANES - 1, tm + SHIFT_ROWS, D_CONV), F32)],
        args=[du1, du1, dmixed, dmixed, proj, proj, x1, dx2, gain, conv_dw, w_in], cargos=cargos)


def _final_norm_loss(x3, target, gain, name):
    t_len, d = x3.shape
    tm = min(TM_FFN, t_len)

    def body(x_ref, t_ref, g_ref, dx_ref, loss_ref, dgain_ref):
        @pl.when(pl.program_id(0) == 0)
        def _():
            loss_ref[...] = jnp.zeros_like(loss_ref)
            dgain_ref[...] = jnp.zeros_like(dgain_ref)

        r, n = _rms_stats(x_ref[...])
        err = n * g_ref[...] - t_ref[...]
        per_tok = jnp.sum(err * err, axis=-1, keepdims=True) * (1.0 / d)
        loss_ref[...] += 0.5 * jnp.sum(per_tok, axis=0, keepdims=True)
        dy = err * (1.0 / d)
        dgain_ref[...] += jnp.sum(dy * n, axis=0, keepdims=True)
        dx_ref[...] = _rms_bwd(dy, n, r, g_ref[...])

    tok = pl.BlockSpec((tm, d), lambda i: (i, 0))
    outs, _ = _call(
        body, name=name, grid=(t_len // tm,),
        in_specs=[tok, tok, pl.BlockSpec((1, d), lambda i: (0, 0))],
        out_specs=[tok, pl.BlockSpec((1, 128), lambda i: (0, 0)), pl.BlockSpec((1, d), lambda i: (0, 0))],
        out_shape=[_sds((t_len, d), F32), _sds((1, 128), F32), _sds((1, d), F32)],
        args=[x3, target, gain])
    return outs


def _row_tile(rows):
    return rows // 4 if rows % 64 == 0 else rows


def _sum_parts(parts, name):
    n, r, c = parts.shape
    tr = _row_tile(r)

    def body(p_ref, o_ref):
        s = p_ref[0].astype(F32)
        for k in range(1, n):
            s = s + p_ref[k].astype(F32)
        o_ref[...] = s

    (out,), _ = _call(body, name=name, grid=(r // tr,),
                      in_specs=[pl.BlockSpec((n, tr, c), lambda i: (0, i, 0))],
                      out_specs=[pl.BlockSpec((tr, c), lambda i: (i, 0))], out_shape=[_sds((r, c), F32)],
                      args=[parts])
    return out


def _adamw_math(w, g, m, v):
    m = ADAM_B1 * m + (1.0 - ADAM_B1) * g
    v = ADAM_B2 * v + (1.0 - ADAM_B2) * (g * g)
    m_hat = m / (1.0 - ADAM_B1 ** ADAM_STEP)
    v_hat = v / (1.0 - ADAM_B2 ** ADAM_STEP)
    delta = -ADAM_LR * (m_hat / (jnp.sqrt(v_hat) + ADAM_EPS) + ADAM_WD * w)
    return delta, m, v


def _adamw(parts, w, m, v, name):
    r, c = w.shape
    n = len(parts)
    tr = _row_tile(r)

    def body(*refs):
        terms = []
        for p_ref in refs[:n]:
            terms += [p_ref[...]] if len(p_ref.shape) == 2 else [p_ref[k] for k in range(p_ref.shape[0])]
        w_ref, m_ref, v_ref, g_out, d_out, m_out, v_out = refs[n:]
        g = terms[0]
        for t in terms[1:]:
            g = g + t
        delta, nm, nv = _adamw_math(w_ref[...], g, m_ref[...], v_ref[...])
        g_out[...] = g
        d_out[...] = delta
        m_out[...] = nm
        v_out[...] = nv

    blk = pl.BlockSpec((tr, c), lambda i: (i, 0))
    p_specs = [blk if p.ndim == 2 else pl.BlockSpec((p.shape[0], tr, c), lambda i: (0, i, 0)) for p in parts]
    outs, _ = _call(body, name=name, grid=(r // tr,), in_specs=p_specs + [blk, blk, blk],
                    out_specs=[blk] * 4, out_shape=[_sds((r, c), F32)] * 4, args=[*parts, w, m, v])
    return outs


FFN_W = ("w_gate", "w_up", "w_down")
MID = ("w_in", "conv_dw", "conv_pw", "w_out")
SMALL_1024 = ("ffn1_norm", "mix_norm", "ffn2_norm", "final_norm")
SMALL_512 = ("conv_dw_b", "conv_ln_g", "conv_ln_b", "pool_scale")
WEIGHTS = ("ffn1_norm", "ffn1_w_gate", "ffn1_w_up", "ffn1_w_down", "mix_norm", "w_in", "conv_dw", "conv_dw_b",
           "conv_ln_g", "conv_ln_b", "conv_pw", "pool_w", "pool_scale", "w_out", "ffn2_norm", "ffn2_w_gate",
           "ffn2_w_up", "ffn2_w_down", "final_norm")
PACK_ROWS = 72


def _pad_rows(a, rows):
    return jnp.pad(a, ((0, rows - a.shape[0]), (0, 0)))


def _pack_small(t):
    rows = [t[k].reshape(1, D_MODEL) for k in SMALL_1024]
    rows.append(jnp.concatenate([t["conv_dw_b"].reshape(1, -1), t["conv_ln_g"].reshape(1, -1)], axis=1))
    rows.append(jnp.concatenate([t["conv_ln_b"].reshape(1, -1), t["pool_scale"].reshape(1, -1)], axis=1))
    rows.append(t["pool_w"].reshape(64, D_MODEL))
    return _pad_rows(jnp.concatenate(rows, axis=0), PACK_ROWS)


def _unpack_small(p):
    out = {k: p[i] for i, k in enumerate(SMALL_1024)}
    out["conv_dw_b"], out["conv_ln_g"] = p[4, :D_CONV], p[4, D_CONV:]
    out["conv_ln_b"], out["pool_scale"] = p[5, :D_CONV], p[5, D_CONV:]
    out["pool_w"] = p[6:70].reshape(4, POOL_GROUP, POOL_GROUP)
    return out


def _as_stored(name, a):
    if name.endswith(("w_gate", "w_up")):
        return a.T
    if name == "conv_dw":
        return _pad_rows(a, CONV_WIDTH + 1)
    return a


def _as_given(name, a):
    if name.endswith(("w_gate", "w_up")):
        return a.T
    if name == "conv_dw":
        return a[:CONV_WIDTH]
    return a


def kernel(x, ffn1_norm, ffn1_w_gate, ffn1_w_up, ffn1_w_down, mix_norm, w_in, conv_dw, conv_dw_b, conv_ln_g, conv_ln_b, conv_pw, pool_w, pool_scale, w_out, ffn2_norm, ffn2_w_gate, ffn2_w_up, ffn2_w_down, final_norm, loss_target, m_ffn1_norm, m_ffn1_w_gate, m_ffn1_w_up, m_ffn1_w_down, m_mix_norm, m_w_in, m_conv_dw, m_conv_dw_b, m_conv_ln_g, m_conv_ln_b, m_conv_pw, m_pool_w, m_pool_scale, m_w_out, m_ffn2_norm, m_ffn2_w_gate, m_ffn2_w_up, m_ffn2_w_down, m_final_norm, v_ffn1_norm, v_ffn1_w_gate, v_ffn1_w_up, v_ffn1_w_down, v_mix_norm, v_w_in, v_conv_dw, v_conv_dw_b, v_conv_ln_g, v_conv_ln_b, v_conv_pw, v_pool_w, v_pool_scale, v_w_out, v_ffn2_norm, v_ffn2_w_gate, v_ffn2_w_up, v_ffn2_w_down, v_final_norm):
    given = dict(locals())
    wts = {k: given[k] for k in WEIGHTS}
    mom_m = {k: given["m_" + k] for k in WEIGHTS}
    mom_v = {k: given["v_" + k] for k in WEIGHTS}
    xt, target = x[0], loss_target[0]

    shard = {k: _as_stored(k, wts[k]) if k == "conv_dw" else _as_stored(k, wts[k]).astype(BF16)
             for k in WEIGHTS if k.endswith(FFN_W) or k in MID}
    w = {k: wts[k].reshape(1, -1) for k in SMALL_1024 + SMALL_512}
    w["pool_w"] = wts["pool_w"].astype(BF16)

    (h1, s1, p1, a1, w["ffn1_w_gate"], w["ffn1_w_up"]), ((w["ffn1_w_down"],),) = _ffn_up_gather(
        xt, w["ffn1_norm"], shard["ffn1_w_gate"], shard["ffn1_w_up"], "ffn1_up_gather",
        cargos=[Cargo("gather_slots", [shard["ffn1_w_down"]])])
    x1, (mid,) = _ffn_down(xt, a1, w["ffn1_w_down"], "ffn1_down",
                           cargos=[Cargo("gather_chips", [shard[k] for k in MID])])
    w["w_in"] = mid[0]
    w["conv_dw"] = mid[1].transpose(1, 0, 2).reshape(CONV_WIDTH + 1, D_CONV)
    w["conv_pw"] = mid[2].reshape(D_CONV, D_CONV)
    w["w_out"] = mid[3].reshape(D_CONV + D_POOL, D_MODEL)
    (h2, proj), ((w["ffn2_w_down"],),) = _mix_in(x1, w["mix_norm"], w["w_in"], "mix_in",
                                                  cargos=[Cargo("gather_slots", [shard["ffn2_w_down"]])])
    (x2, u1, u3, mixed, cat), ((w["ffn2_w_gate"], w["ffn2_w_up"]),) = _mix_fwd(
        proj, x1, w["conv_dw"], w["conv_dw_b"], w["conv_ln_g"], w["conv_ln_b"], w["conv_pw"], w["pool_w"],
        w["pool_scale"], w["w_out"], "mix_fwd",
        cargos=[Cargo("gather_slots", [shard["ffn2_w_gate"], shard["ffn2_w_up"]])])
    x3, h3, s2, p2, a2 = _ffn_fwd(x2, w["ffn2_norm"], w["ffn2_w_gate"], w["ffn2_w_up"], w["ffn2_w_down"], "ffn2_fwd")
    dx3, loss, d_final = _final_norm_loss(x3, target, w["final_norm"], "final_norm_loss")
    loss = lax.psum(loss[0, 0], ("x", "y", "c"))

    g = {"final_norm": d_final}
    sums = {}

    def landed(names, parts):
        for k, p in zip(names, parts):
            sums[k] = _sum_parts(p, "sum_chips_" + k)

    dx2, g["ffn2_norm"], df2, dg2, du2 = _ffn_bwd(dx3, x2, w["ffn2_norm"], s2, p2, w["ffn2_w_gate"],
                                                   w["ffn2_w_up"], w["ffn2_w_down"], "ffn2_bwd")
    gw, _ = _wgrad_hid_tok(dg2, h3, "ffn2_dw_gate")
    gu, (parts,) = _wgrad_hid_tok(du2, h3, "ffn2_dw_up", cargos=[Cargo("scatter_slots", [gw])])
    landed(["ffn2_w_gate"], parts)
    gd, (parts,) = _wgrad_hid_tok(a2, df2, "ffn2_dw_down", cargos=[Cargo("scatter_slots", [gu])])
    landed(["ffn2_w_up"], parts)
    (du1, dmixed, dco, dpo, g["conv_ln_g"], g["conv_ln_b"], g["pool_scale"]), (parts,) = _mix_bwd_local(
        dx2, u1, mixed, w["conv_ln_g"], w["conv_ln_b"], w["conv_pw"], w["pool_w"], w["pool_scale"], w["w_out"],
        "mix_bwd_local", cargos=[Cargo("scatter_slots", [gd])])
    landed(["ffn2_w_down"], parts)
    g_out, _ = _wgrad_2d(cat, dx2, 1, BF16, "dw_out")
    g_pw, _ = _wgrad_2d(u3, dco, 1, BF16, "dconv_pw")
    g["pool_w"], _ = _wgrad_2d(mixed, dpo, 4, F32, "dpool_w", group_diag=True)
    slabs = [g_pw.reshape(N_CHIPS, D_CONV // N_CHIPS, D_CONV),
             g_out.reshape(N_CHIPS, (D_CONV + D_POOL) // N_CHIPS, D_MODEL)]
    (dx1, dproj, g_dw, g["conv_dw_b"], g["mix_norm"]), (parts, swapped2) = _mix_bwd_seq(
        du1, dmixed, proj, x1, dx2, w["mix_norm"], w["conv_dw"], w["w_in"], "mix_bwd_seq",
        cargos=[Cargo("scatter_chips", slabs),
                Cargo("swap", [sums[k] for k in ("ffn2_w_gate", "ffn2_w_up", "ffn2_w_down")])])
    landed(["conv_pw", "w_out"], parts)
    g_in, _ = _wgrad_2d(h2, dproj, N_CHIPS, BF16, "dw_in")
    dx, g["ffn1_norm"], df1, dg1, du1_ = _ffn_bwd(dx1, xt, w["ffn1_norm"], s1, p1, w["ffn1_w_gate"],
                                                   w["ffn1_w_up"], w["ffn1_w_down"], "ffn1_bwd")
    slabs = [g_in, g_dw.reshape(CONV_WIDTH + 1, N_CHIPS, D_CONV // N_CHIPS).transpose(1, 0, 2)]
    gw, (parts, small_parts) = _wgrad_hid_tok(
        dg1, h1, "ffn1_dw_gate",
        cargos=[Cargo("scatter_chips", slabs), Cargo("gather_devices", [_pack_small(g)])])
    landed(["w_in", "conv_dw"], parts)
    gu, (parts, swapped_mid) = _wgrad_hid_tok(
        du1_, h1, "ffn1_dw_up",
        cargos=[Cargo("scatter_slots", [gw]), Cargo("swap", [sums[k] for k in MID])])
    landed(["ffn1_w_gate"], parts)
    parts_down, (parts, swapped_gate) = _wgrad_hid_tok_scatter(
        a1, df1, "ffn1_dw_down",
        cargos=[Cargo("scatter_slots", [gu]), Cargo("swap", [sums["ffn1_w_gate"]])])
    landed(["ffn1_w_up", "ffn1_w_down"], parts + [parts_down])
    last = ["ffn1_w_up", "ffn1_w_down"]
    swapped_last = _exchange(Cargo("swap", [sums[k] for k in last]), "swap_last")

    theirs = dict(zip(["ffn2_" + k for k in FFN_W], swapped2))
    theirs.update(zip(MID, swapped_mid))
    theirs["ffn1_w_gate"] = swapped_gate[0]
    theirs.update(zip(last, swapped_last))
    grads, deltas, new_m, new_v = {}, {}, {}, {}
    for k in theirs:
        res = _adamw([sums[k], theirs[k]], _as_stored(k, wts[k]), _as_stored(k, mom_m[k]),
                     _as_stored(k, mom_v[k]), "adamw_" + k)
        grads[k], deltas[k], new_m[k], new_v[k] = [_as_given(k, t) for t in res]
    res = _adamw(small_parts, _pack_small(wts), _pack_small(mom_m), _pack_small(mom_v), "adamw_small")
    for dst, packed in zip((grads, deltas, new_m, new_v), res):
        dst.update(_unpack_small(packed))

    out = [loss, dx[None]]
    for group in (grads, deltas, new_m, new_v):
        out += [group[k] for k in WEIGHTS]
    return tuple(out)
```

```python
import functools

import jax
import jax.numpy as jnp
from jax import lax
from jax.experimental import pallas as pl
from jax.experimental.pallas import tpu as pltpu

F32 = jnp.float32
BF16 = jnp.bfloat16
MESH = pl.DeviceIdType.MESH

N_CHIPS = 4
N_DEV = 8
D_MODEL = 1024
D_CONV = 512
D_POOL = 512
CONV_WIDTH = 31
POOL_WINDOWS = (2, 4, 8, 16)
POOL_GROUP = 128
D_IN = 2 * D_CONV + D_POOL
HALO = 32
RMS_EPS = 1e-6
LN_EPS = 1e-5
FFN_RES_WEIGHT = 0.5
ADAM_LR = 0.001
ADAM_B1 = 0.9
ADAM_B2 = 0.999
ADAM_EPS = 1e-08
ADAM_WD = 0.01
ADAM_STEP = 10
VMEM_LIMIT_BYTES = 52 * 1024 * 1024
TM_FFN = 512
TM_MIX = 256
TT_WGRAD = 2048
STRIP = 16
SUBLANES = 8
RELAY_AT_EIGHTHS = 5

HBM = pl.BlockSpec(memory_space=pl.ANY)


def _dot(a, b):
    return jnp.dot(a, b, preferred_element_type=F32)


def _dot_nt(a, b):
    return lax.dot_general(a, b, (((1,), (1,)), ((), ())), preferred_element_type=F32)


def _dot_tn(a, b):
    return lax.dot_general(a, b, (((0,), (0,)), ((), ())), preferred_element_type=F32)


def _sds(shape, dtype):
    return jax.ShapeDtypeStruct(shape, dtype)


def _rms_stats(xv):
    r = lax.rsqrt(jnp.mean(xv * xv, axis=-1, keepdims=True) + RMS_EPS)
    return r, xv * r


def _swiglu_saved(gate, up):
    sig = jax.nn.sigmoid(gate)
    silu = gate * sig
    return silu, up * (sig * (1.0 + gate * (1.0 - sig))), silu * up


def _rms_bwd(dh, n, r, gain):
    dn = dh * gain
    return r * (dn - n * jnp.mean(dn * n, axis=-1, keepdims=True))


def _place():
    x, y, c = lax.axis_index("x"), lax.axis_index("y"), lax.axis_index("c")
    return x, y, c, [(1 - x, y), (x, 1 - y), (1 - x, 1 - y)]


class Cargo:
    def __init__(self, kind, arrays):
        self.kind, self.arrays = kind, list(arrays)
        n = len(self.arrays)
        self.two_level = kind in ("gather_slots", "gather_chips")
        if self.two_level:
            self.out_shape = [_sds((N_CHIPS,) + a.shape, a.dtype) for a in self.arrays]
        elif kind == "gather_devices":
            self.out_shape = [_sds((N_DEV,) + a.shape, a.dtype) for a in self.arrays]
        else:
            self.out_shape = [_sds(a.shape, a.dtype) for a in self.arrays]
        n_remote = n * {"swap": 1, "gather_devices": N_DEV - 1}.get(kind, N_CHIPS - 1)
        n_own = 0 if kind == "swap" else n
        n_relay = n_remote if self.two_level else 0
        dma = pltpu.SemaphoreType.DMA
        self.scratch = [dma((n_remote,)), dma((n_remote,)), dma((max(n_own, 1),)),
                        dma((max(n_relay, 1),)), dma((max(n_relay, 1),))]

    def _plan(self, ins, outs):
        x, y, c, chips = _place()
        q = 2 * x + y
        sibling = (x, y, 1 - c)
        own, remote, relays = [], [], []
        for a, o in zip(ins, outs):
            if self.two_level:
                half = a.shape[0] // 2
                mine = pl.ds(pl.multiple_of(c * half, SUBLANES), half)
                theirs = pl.ds(pl.multiple_of((1 - c) * half, SUBLANES), half)
                own.append((a, o.at[0 if self.kind == "gather_slots" else q]))
                for j, (px, py) in enumerate(chips):
                    there, here = (j + 1, j + 1) if self.kind == "gather_slots" else (q, 2 * px + py)
                    remote.append((a.at[mine], o.at[there, mine], o.at[here, mine], (px, py, c)))
                    relays.append((o.at[here, mine], o.at[here, mine], o.at[here, theirs], sibling))
            elif self.kind == "scatter_chips":
                own.append((a.at[q], o.at[q]))
                remote += [(a.at[2 * px + py], o.at[q], o.at[2 * px + py], (px, py, c)) for px, py in chips]
            elif self.kind == "swap":
                remote.append((a, o, o, sibling))
            else:
                own.append((a, o.at[4 * x + 2 * y + c]))
                for k in range(1, N_DEV):
                    px, py, pc = x ^ (k >> 2 & 1), y ^ (k >> 1 & 1), c ^ (k & 1)
                    remote.append((a, o.at[4 * x + 2 * y + c], o.at[4 * px + 2 * py + pc], (px, py, pc)))
        return own, remote, relays

    @staticmethod
    def _copies(entries, send_sems, recv_sems):
        out = []
        for k, (src, dst, landed, peer) in enumerate(entries):
            def make(dst_ref, k=k, src=src, peer=peer):
                return pltpu.make_async_remote_copy(src_ref=src, dst_ref=dst_ref, send_sem=send_sems.at[k],
                                                    recv_sem=recv_sems.at[k], device_id=peer, device_id_type=MESH)
            out.append((make(dst), make(landed)))
        return out

    def start(self, ins, outs, sems):
        own, remote, _ = self._plan(ins, outs)
        for k, (src, dst) in enumerate(own):
            pltpu.make_async_copy(src, dst, sems[2].at[k]).start()
        for mine, _ in self._copies(remote, sems[0], sems[1]):
            mine.start()

    def relay(self, ins, outs, sems):
        _, remote, relays = self._plan(ins, outs)
        passed = self._copies(relays, sems[3], sems[4])
        for (_, arriving), (mine, _) in zip(self._copies(remote, sems[0], sems[1]), passed):
            arriving.wait_recv()
            mine.start()

    def wait(self, ins, outs, sems):
        own, remote, relays = self._plan(ins, outs)
        for mine, arriving in self._copies(remote, sems[0], sems[1]):
            mine.wait_send()
            if not self.two_level:
                arriving.wait_recv()
        for mine, arriving in self._copies(relays, sems[3], sems[4]):
            mine.wait_send()
            arriving.wait_recv()
        for k, (src, dst) in enumerate(own):
            pltpu.make_async_copy(src, dst, sems[2].at[k]).wait()


N_CARGO_SEMS = 5


def _call(body, *, name, grid, in_specs, out_specs, out_shape, args, scratch_shapes=(), cargos=()):
    n_in, n_out, n_scr = len(in_specs), len(out_specs), len(scratch_shapes)
    c_in = [len(cg.arrays) for cg in cargos]
    n_cin = sum(c_in)

    def wrapped(*refs):
        ins = refs[:n_in]
        cins = refs[n_in:n_in + n_cin]
        outs = refs[n_in + n_cin:n_in + n_cin + n_out]
        couts = refs[n_in + n_cin + n_out:n_in + 2 * n_cin + n_out]
        scr = refs[n_in + 2 * n_cin + n_out:n_in + 2 * n_cin + n_out + n_scr]
        sems = refs[n_in + 2 * n_cin + n_out + n_scr:]
        step, n_steps = 0, 1
        for ax, size in enumerate(grid):
            step = step * size + pl.program_id(ax)
            n_steps *= size

        def each(method, only_two_level=False):
            at = 0
            for k, cg in enumerate(cargos):
                if cg.two_level or not only_two_level:
                    getattr(cg, method)(cins[at:at + c_in[k]], couts[at:at + c_in[k]],
                                        sems[N_CARGO_SEMS * k:N_CARGO_SEMS * (k + 1)])
                at += c_in[k]

        body(*ins, *outs, *scr)
        if cargos:
            pl.when(step == 0)(lambda: each("start"))
        if any(cg.two_level for cg in cargos):
            pl.when(step == (RELAY_AT_EIGHTHS * n_steps) // 8)(lambda: each("relay", only_two_level=True))
        if cargos:
            pl.when(step == n_steps - 1)(lambda: each("wait"))

    res = pl.pallas_call(
        wrapped, name=name, grid=grid,
        in_specs=list(in_specs) + [HBM] * n_cin,
        out_specs=list(out_specs) + [HBM] * n_cin,
        out_shape=list(out_shape) + [s for cg in cargos for s in cg.out_shape],
        scratch_shapes=list(scratch_shapes) + [s for cg in cargos for s in cg.scratch],
        compiler_params=pltpu.CompilerParams(dimension_semantics=("arbitrary",) * len(grid),
                                             vmem_limit_bytes=VMEM_LIMIT_BYTES),
    )(*args, *[a for cg in cargos for a in cg.arrays])
    outs, rest = list(res[:n_out]), list(res[n_out:])
    cargo_outs = []
    for k in c_in:
        cargo_outs.append(rest[:k])
        rest = rest[k:]
    return outs, cargo_outs


def _exchange(cargo, name):
    _, (outs,) = _call(lambda: None, name=name, grid=(1,), in_specs=[], out_specs=[], out_shape=[], args=[],
                       cargos=[cargo])
    return outs


def _ffn_up_gather(x, gain, wg_t, wu_t, name, cargos=()):
    t_len, d = x.shape
    fq = wg_t.shape[0]
    tm = min(TM_FFN, t_len)
    n_tiles = t_len // tm

    def body(x_ref, g_ref, wg_in, wu_in, h_ref, s_ref, p_ref, a_ref, wg_all, wu_all,
             wg_v, wu_v, h_all, send_sems, recv_sems, pass_send_sems, pass_recv_sems, own_sems, load_sems):
        s = pl.program_id(0)
        i = pl.program_id(1)
        x_, y_, c_, chips = _place()
        shards = ((wg_in, wg_all, wg_v), (wu_in, wu_all, wu_v))
        mine = pl.ds(pl.multiple_of(c_ * (fq // 2), SUBLANES), fq // 2)
        theirs = pl.ds(pl.multiple_of((1 - c_) * (fq // 2), SUBLANES), fq // 2)

        def to_peer(k, j):
            w_in, w_all, _ = shards[k]
            return pltpu.make_async_remote_copy(
                src_ref=w_in.at[mine], dst_ref=w_all.at[j + 1, mine], send_sem=send_sems.at[3 * k + j],
                recv_sem=recv_sems.at[3 * k + j], device_id=(*chips[j], c_), device_id_type=MESH)

        def to_sibling(k, j, landing=False):
            w_all = shards[k][1]
            return pltpu.make_async_remote_copy(
                src_ref=w_all.at[j + 1, mine], dst_ref=w_all.at[j + 1, theirs if landing else mine],
                send_sem=pass_send_sems.at[3 * k + j], recv_sem=pass_recv_sems.at[3 * k + j],
                device_id=(x_, y_, 1 - c_), device_id_type=MESH)

        def keep(k):
            return pltpu.make_async_copy(shards[k][0], shards[k][1].at[0], own_sems.at[k])

        @pl.when((s == 0) & (i == 0))
        def _():
            for k in range(2):
                for j in range(N_CHIPS - 1):
                    to_peer(k, j).start()
                keep(k).start()

        for slot in range(N_CHIPS):
            @pl.when((s == slot) & (i == 0))
            def _():
                if slot > 0:
                    for k in range(2):
                        to_peer(k, slot - 1).wait_recv()
                        to_sibling(k, slot - 1).start()
                    for k in range(2):
                        to_sibling(k, slot - 1, landing=True).wait_recv()
                loads = []
                for k in range(2):
                    src = shards[k][0] if slot == 0 else shards[k][1].at[slot]
                    loads.append(pltpu.make_async_copy(src, shards[k][2], load_sems.at[k]))
                    loads[-1].start()
                for ld in loads:
                    ld.wait()

        @pl.when(s == 0)
        def _():
            _, n = _rms_stats(x_ref[...])
            h_new = (n * g_ref[...]).astype(BF16)
            h_ref[...] = h_new
            h_all[i] = h_new

        h = h_all[i]
        silu, dgate, act = _swiglu_saved(_dot_nt(h, wg_v[...]), _dot_nt(h, wu_v[...]))
        s_ref[...] = silu.astype(BF16)
        p_ref[...] = dgate.astype(BF16)
        a_ref[...] = act.astype(BF16)

        @pl.when((s == N_CHIPS - 1) & (i == n_tiles - 1))
        def _():
            for k in range(2):
                for j in range(N_CHIPS - 1):
                    to_peer(k, j).wait_send()
                    to_sibling(k, j).wait_send()
                keep(k).wait()

    tok = pl.BlockSpec((tm, d), lambda s, i: (jnp.where(s == 0, i, n_tiles - 1), 0))
    hid = pl.BlockSpec((None, tm, fq), lambda s, i: (s, i, 0))
    outs, cargo_outs = _call(
        body, name=name, grid=(N_CHIPS, n_tiles),
        in_specs=[tok, pl.BlockSpec((1, d), lambda s, i: (0, 0)), HBM, HBM],
        out_specs=[tok, hid, hid, hid, HBM, HBM],
        out_shape=[_sds((t_len, d), BF16)] + [_sds((N_CHIPS, t_len, fq), BF16)] * 3
        + [_sds((N_CHIPS, fq, d), BF16)] * 2,
        scratch_shapes=[pltpu.VMEM((fq, d), BF16), pltpu.VMEM((fq, d), BF16), pltpu.VMEM((n_tiles, tm, d), BF16)]
        + [pltpu.SemaphoreType.DMA((6,))] * 4 + [pltpu.SemaphoreType.DMA((2,))] * 2,
        args=[x, gain, wg_t, wu_t], cargos=cargos)
    return outs, cargo_outs


def _load_once(hbm_refs, vmem_refs, sems, first):
    @pl.when(first)
    def _():
        copies = [pltpu.make_async_copy(src, dst, sems.at[k]) for k, (src, dst) in enumerate(zip(hbm_refs, vmem_refs))]
        for cp in copies:
            cp.start()
        for cp in copies:
            cp.wait()


def _ffn_down(x, act, wd, name, cargos=()):
    t_len, d = x.shape
    nq, fq, _ = wd.shape
    tm = min(TM_FFN, t_len)

    def body(x_ref, a_ref, wd_ref, xo_ref):
        y = _dot(a_ref[0], wd_ref[0])
        for j in range(1, nq):
            y = y + _dot(a_ref[j], wd_ref[j])
        xo_ref[...] = x_ref[...] + FFN_RES_WEIGHT * y

    tok = pl.BlockSpec((tm, d), lambda i: (i, 0))
    (xo,), cargo_outs = _call(
        body, name=name, grid=(t_len // tm,),
        in_specs=[tok, pl.BlockSpec((nq, tm, fq), lambda i: (0, i, 0)), pl.BlockSpec((nq, fq, d), lambda i: (0, 0, 0))],
        out_specs=[tok], out_shape=[_sds((t_len, d), F32)], args=[x, act, wd], cargos=cargos)
    return xo, cargo_outs


def _ffn_fwd(x, gain, wg_t, wu_t, wd, name):
    t_len, d = x.shape
    nq, fq, _ = wd.shape
    tm = min(TM_FFN, t_len)

    def body(x_ref, g_ref, wg_hbm, wu_hbm, wd_hbm, xo_ref, h_ref, s_ref, p_ref, a_ref,
             h_s, acc, wg_v, wu_v, wd_v, load_sems):
        i = pl.program_id(0)
        j = pl.program_id(1)
        _load_once((wg_hbm, wu_hbm, wd_hbm), (wg_v, wu_v, wd_v), load_sems, (i == 0) & (j == 0))

        @pl.when(j == 0)
        def _():
            _, n = _rms_stats(x_ref[...])
            h = (n * g_ref[...]).astype(BF16)
            h_s[...] = h
            h_ref[...] = h
            acc[...] = jnp.zeros_like(acc)

        h = h_s[...]
        silu, dgate, act = _swiglu_saved(_dot_nt(h, wg_v[j]), _dot_nt(h, wu_v[j]))
        s_ref[...] = silu.astype(BF16)
        p_ref[...] = dgate.astype(BF16)
        a_ref[...] = act.astype(BF16)
        acc[...] += _dot(a_ref[...], wd_v[j])

        @pl.when(j == nq - 1)
        def _():
            xo_ref[...] = x_ref[...] + FFN_RES_WEIGHT * acc[...]

    tok = pl.BlockSpec((tm, d), lambda i, j: (i, 0))
    hid = pl.BlockSpec((None, tm, fq), lambda i, j: (j, i, 0))
    outs, _ = _call(
        body, name=name, grid=(t_len // tm, nq),
        in_specs=[tok, pl.BlockSpec((1, d), lambda i, j: (0, 0)), HBM, HBM, HBM],
        out_specs=[tok, tok, hid, hid, hid],
        out_shape=[_sds((t_len, d), F32), _sds((t_len, d), BF16)] + [_sds((nq, t_len, fq), BF16)] * 3,
        scratch_shapes=[pltpu.VMEM((tm, d), BF16), pltpu.VMEM((tm, d), F32)]
        + [pltpu.VMEM((nq, fq, d), BF16)] * 3 + [pltpu.SemaphoreType.DMA((3,))],
        args=[x, gain, wg_t, wu_t, wd])
    return outs


def _ffn_bwd(dy, x_in, gain, silu, dgate_du, wg_t, wu_t, wd, name):
    t_len, d = dy.shape
    nq, fq, _ = wd.shape
    tm = min(TM_FFN, t_len)

    def body(dy_ref, x_ref, g_ref, s_ref, p_ref, wg_hbm, wu_hbm, wd_hbm,
             dx_ref, dgain_ref, df_ref, dg_ref, du_ref, df_s, dh_acc, dact_s, wg_v, wu_v, wd_v, load_sems):
        i = pl.program_id(0)
        j = pl.program_id(1)
        _load_once((wg_hbm, wu_hbm, wd_hbm), (wg_v, wu_v, wd_v), load_sems, (i == 0) & (j == 0))

        @pl.when((i == 0) & (j == 0))
        def _():
            dgain_ref[...] = jnp.zeros_like(dgain_ref)

        @pl.when(j == 0)
        def _():
            df = (FFN_RES_WEIGHT * dy_ref[...]).astype(BF16)
            df_s[...] = df
            df_ref[...] = df
            dh_acc[...] = jnp.zeros_like(dh_acc)

        half = tm // 2
        for r0 in (0, half):
            dact_s[r0:r0 + half, :] = _dot_nt(df_s[r0:r0 + half, :], wd_v[j])

        for r0 in range(0, tm, STRIP):
            dact = dact_s[r0:r0 + STRIP, :]
            dg_ref[r0:r0 + STRIP, :] = (dact * p_ref[r0:r0 + STRIP, :].astype(F32)).astype(BF16)
            du_ref[r0:r0 + STRIP, :] = (dact * s_ref[r0:r0 + STRIP, :].astype(F32)).astype(BF16)

        for r0 in (0, half):
            rows = slice(r0, r0 + half)
            dh_acc[rows, :] += _dot(dg_ref[rows, :], wg_v[j]) + _dot(du_ref[rows, :], wu_v[j])

        @pl.when(j == nq - 1)
        def _():
            r, n = _rms_stats(x_ref[...])
            dh = dh_acc[...]
            dgain_ref[...] += jnp.sum(dh * n, axis=0, keepdims=True)
            dx_ref[...] = dy_ref[...] + _rms_bwd(dh, n, r, g_ref[...])

    tok = pl.BlockSpec((tm, d), lambda i, j: (i, 0))
    vec = pl.BlockSpec((1, d), lambda i, j: (0, 0))
    hid = pl.BlockSpec((None, tm, fq), lambda i, j: (j, i, 0))
    outs, _ = _call(
        body, name=name, grid=(t_len // tm, nq),
        in_specs=[tok, tok, vec, hid, hid, HBM, HBM, HBM],
        out_specs=[tok, vec, tok, hid, hid],
        out_shape=[_sds((t_len, d), F32), _sds((1, d), F32), _sds((t_len, d), BF16),
                   _sds((nq, t_len, fq), BF16), _sds((nq, t_len, fq), BF16)],
        scratch_shapes=[pltpu.VMEM((tm, d), BF16), pltpu.VMEM((tm, d), F32), pltpu.VMEM((tm, fq), F32)]
        + [pltpu.VMEM((nq, fq, d), BF16)] * 3 + [pltpu.SemaphoreType.DMA((3,))],
        args=[dy, x_in, gain, silu, dgate_du, wg_t, wu_t, wd])
    return outs


def _wgrad(lhs, rhs, l_spec, r_spec, out_shape, out_spec, acc_shape, grid, name, cargos=()):
    n_t = grid[-1]
    t_axis = len(grid) - 1

    def body(l_ref, r_ref, o_ref, acc):
        t = pl.program_id(t_axis)

        @pl.when(t == 0)
        def _():
            acc[...] = jnp.zeros_like(acc)

        acc[...] += _dot_tn(l_ref[...].astype(BF16), r_ref[...].astype(BF16))

        @pl.when(t == n_t - 1)
        def _():
            o_ref[...] = acc[...].astype(o_ref.dtype)

    (out,), cargo_outs = _call(
        body, name=name, grid=grid, in_specs=[l_spec, r_spec], out_specs=[out_spec], out_shape=[out_shape],
        scratch_shapes=[pltpu.VMEM(acc_shape, F32)], args=[lhs, rhs], cargos=cargos)
    return out, cargo_outs


def _wgrad_hid_tok_scatter(hid, tok, name, cargos=()):
    t_len, d = tok.shape
    nq, _, fq = hid.shape
    half = fq // 2
    tt = min(TT_WGRAD, t_len)
    n_t = t_len // tt

    def body(l_ref, r_ref, parts_ref, acc, stage, pair, summed, zeros,
             pair_send_sems, pair_recv_sems, send_sems, recv_sems, own_sem, zero_sems):
        g = pl.program_id(0)
        t = pl.program_id(1)
        x_, y_, c_, chips = _place()
        mine = pl.ds(pl.multiple_of(c_ * half, STRIP), half)
        theirs = pl.ds(pl.multiple_of((1 - c_) * half, STRIP), half)

        def to_sibling(slot):
            return pltpu.make_async_remote_copy(
                src_ref=stage.at[slot, theirs], dst_ref=pair.at[slot], send_sem=pair_send_sems.at[slot],
                recv_sem=pair_recv_sems.at[slot], device_id=(x_, y_, 1 - c_), device_id_type=MESH)

        def to_peer(j):
            return pltpu.make_async_remote_copy(
                src_ref=summed.at[j + 1], dst_ref=parts_ref.at[j + 1, mine], send_sem=send_sems.at[j],
                recv_sem=recv_sems.at[j], device_id=(*chips[j], c_), device_id_type=MESH)

        keep = pltpu.make_async_copy(summed.at[0], parts_ref.at[0, mine], own_sem)

        def blank(slot):
            return pltpu.make_async_copy(zeros, parts_ref.at[slot, theirs], zero_sems.at[slot])

        @pl.when((g == 0) & (t == 0))
        def _():
            zeros[...] = jnp.zeros_like(zeros)
            for slot in range(nq):
                blank(slot).start()

        @pl.when(t == 0)
        def _():
            acc[...] = jnp.zeros_like(acc)

        acc[...] += _dot_tn(l_ref[...], r_ref[...])

        for step in range(nq):
            slot = (step + 1) % nq

            @pl.when((g == step) & (t == n_t - 1))
            def _():
                stage[slot] = acc[...].astype(BF16)
                to_sibling(slot).start()
                to_sibling(slot).wait_recv()
                summed[slot] = (stage[slot, mine, :].astype(F32) + pair[slot].astype(F32)).astype(BF16)
                if slot > 0:
                    to_peer(slot - 1).start()
                else:
                    keep.start()

        @pl.when((g == nq - 1) & (t == n_t - 1))
        def _():
            for j in range(N_CHIPS - 1):
                to_peer(j).wait()
            keep.wait()
            for slot in range(nq):
                to_sibling(slot).wait_send()
                blank(slot).wait()

    (parts,), cargo_outs = _call(
        body, name=name, grid=(nq, n_t),
        in_specs=[pl.BlockSpec((None, tt, fq), lambda g, t: ((g + 1) % nq, t, 0)),
                  pl.BlockSpec((tt, d), lambda g, t: (t, 0))],
        out_specs=[HBM], out_shape=[_sds((nq, fq, d), BF16)],
        scratch_shapes=[pltpu.VMEM((fq, d), F32), pltpu.VMEM((nq, fq, d), BF16), pltpu.VMEM((nq, half, d), BF16),
                        pltpu.VMEM((nq, half, d), BF16), pltpu.VMEM((half, d), BF16),
                        pltpu.SemaphoreType.DMA((nq,)), pltpu.SemaphoreType.DMA((nq,)),
                        pltpu.SemaphoreType.DMA((N_CHIPS - 1,)), pltpu.SemaphoreType.DMA((N_CHIPS - 1,)),
                        pltpu.SemaphoreType.DMA(()), pltpu.SemaphoreType.DMA((nq,))],
        args=[hid, tok], cargos=cargos)
    return parts, cargo_outs


def _wgrad_2d(lhs, rhs, n_col_blocks, out_dtype, name, group_diag=False, cargos=()):
    t_len, k = lhs.shape
    n = rhs.shape[1]
    nb = n // n_col_blocks
    kb = k // n_col_blocks if group_diag else k
    tt = min(TT_WGRAD, t_len)
    l_map = (lambda q, t: (t, q)) if group_diag else (lambda q, t: (t, 0))
    return _wgrad(lhs, rhs,
                  pl.BlockSpec((tt, kb), l_map),
                  pl.BlockSpec((tt, nb), lambda q, t: (t, q)),
                  _sds((n_col_blocks, kb, nb), out_dtype),
                  pl.BlockSpec((None, kb, nb), lambda q, t: (q, 0, 0)),
                  (kb, nb), (n_col_blocks, t_len // tt), name, cargos)


def _mix_in(x1, gain, w_in, name, cargos=()):
    t_len, d = x1.shape
    nq, _, nb = w_in.shape
    tm = min(TM_FFN, t_len)

    def body(x_ref, g_ref, w_ref, h_ref, p_ref):
        _, n = _rms_stats(x_ref[...])
        h = (n * g_ref[...]).astype(BF16)
        h_ref[...] = h
        for q in range(nq):
            p_ref[:, q * nb:(q + 1) * nb] = _dot(h, w_ref[q])

    return _call(
        body, name=name, grid=(t_len // tm,),
        in_specs=[pl.BlockSpec((tm, d), lambda i: (i, 0)), pl.BlockSpec((1, d), lambda i: (0, 0)),
                  pl.BlockSpec((nq, d, nb), lambda i: (0, 0, 0))],
        out_specs=[pl.BlockSpec((tm, d), lambda i: (i, 0)), pl.BlockSpec((tm, nq * nb), lambda i: (i, 0))],
        out_shape=[_sds((t_len, d), BF16), _sds((t_len, nq * nb), F32)],
        args=[x1, gain, w_in], cargos=cargos)


def _layernorm_stats(u1):
    mu = jnp.mean(u1, axis=-1, keepdims=True)
    xc = u1 - mu
    rstd = lax.rsqrt(jnp.mean(xc * xc, axis=-1, keepdims=True) + LN_EPS)
    return rstd, xc * rstd


def _positions(i, tm, rows, offset=0):
    return (lax.broadcasted_iota(jnp.int32, (rows, 1), 0) + (i * tm + offset)).astype(F32)


SHIFT_ROWS = HALO - SUBLANES


def _fill_shifted(ext_s, sh_s, tm):
    for b in range(1, SUBLANES):
        sh_s[b - 1] = ext_s[pl.ds(b, tm + SHIFT_ROWS), :]


def _window(ext_s, sh_s, shift, tm):
    a, b = divmod(shift, SUBLANES)
    if b == 0:
        return ext_s[pl.ds(shift, tm), :]
    return sh_s[b - 1, pl.ds(a * SUBLANES, tm), :]


def _tile(tm, cols):
    return pl.BlockSpec((tm, cols), lambda i: (i, 0))


def _whole(shape):
    return pl.BlockSpec(shape, lambda i: (0,) * len(shape))


def _mix_fwd(proj, x1, conv_dw, conv_b, ln_g, ln_b, conv_pw, pool_w, pool_scale, w_out, name, cargos=()):
    t_len, d = x1.shape
    tm = min(TM_MIX, t_len)
    hb = tm // HALO

    def body(p_ref, tail_ref, x_ref, dw_ref, cb_ref, lg_ref, lb_ref, pw_ref, plw_ref, ps_ref, wo_ref,
             x2_ref, u1_ref, u3_ref, mx_ref, cat_ref, ext_s, pext_s, sh_s):
        i = pl.program_id(0)
        first = i == 0
        a = p_ref[:, 0:D_CONV]
        g = p_ref[:, D_CONV:2 * D_CONV]
        p = p_ref[:, 2 * D_CONV:]
        ta = tail_ref[:, 0:D_CONV]
        tg = tail_ref[:, D_CONV:2 * D_CONV]
        tp = tail_ref[:, 2 * D_CONV:]
        ext_s[0:HALO, :] = jnp.where(first, 0.0, ta * jax.nn.sigmoid(tg))
        ext_s[HALO:, :] = a * jax.nn.sigmoid(g)
        pext_s[0:HALO, :] = jnp.where(first, 0.0, tp)
        pext_s[HALO:, :] = p

        _fill_shifted(ext_s, sh_s, tm)
        u1 = jnp.broadcast_to(cb_ref[...], (tm, D_CONV))
        for k in range(CONV_WIDTH):
            u1 = u1 + dw_ref[k:k + 1, :] * _window(ext_s, sh_s, HALO - (CONV_WIDTH - 1) + k, tm)
        u1_ref[...] = u1
        _, nhat = _layernorm_stats(u1)
        u2 = nhat * lg_ref[...] + lb_ref[...]
        u3 = (u2 * jax.nn.sigmoid(u2)).astype(BF16)
        u3_ref[...] = u3
        cat_ref[:, 0:D_CONV] = _dot(u3, pw_ref[...]).astype(BF16)

        pos1 = _positions(i, tm, tm) + 1.0
        for gi, w in enumerate(POOL_WINDOWS):
            cols = slice(gi * POOL_GROUP, (gi + 1) * POOL_GROUP)
            s = pext_s[pl.ds(HALO, tm), cols]
            for j in range(1, w):
                s = s + pext_s[pl.ds(HALO - j, tm), cols]
            mixed = (s / jnp.minimum(pos1, float(w)) - p[:, cols]).astype(BF16)
            mx_ref[:, cols] = mixed
            out = _dot(mixed, plw_ref[gi]) * ps_ref[:, cols]
            cat_ref[:, D_CONV + gi * POOL_GROUP:D_CONV + (gi + 1) * POOL_GROUP] = out.astype(BF16)

        x2_ref[...] = x_ref[...] + _dot(cat_ref[...], wo_ref[...])

    return _call(
        body, name=name, grid=(t_len // tm,),
        in_specs=[_tile(tm, D_IN), pl.BlockSpec((HALO, D_IN), lambda i: (jnp.maximum(i * hb - 1, 0), 0)),
                  _tile(tm, d), _whole((CONV_WIDTH + 1, D_CONV)), _whole((1, D_CONV)), _whole((1, D_CONV)),
                  _whole((1, D_CONV)), _whole((D_CONV, D_CONV)), _whole((4, POOL_GROUP, POOL_GROUP)),
                  _whole((1, D_POOL)), _whole((D_CONV + D_POOL, d))],
        out_specs=[_tile(tm, d), _tile(tm, D_CONV), _tile(tm, D_CONV), _tile(tm, D_POOL),
                   _tile(tm, D_CONV + D_POOL)],
        out_shape=[_sds((t_len, d), F32), _sds((t_len, D_CONV), F32), _sds((t_len, D_CONV), BF16),
                   _sds((t_len, D_POOL), BF16), _sds((t_len, D_CONV + D_POOL), BF16)],
        scratch_shapes=[pltpu.VMEM((tm + HALO, D_CONV), F32), pltpu.VMEM((tm + HALO, D_POOL), F32),
                        pltpu.VMEM((SUBLANES - 1, tm + SHIFT_ROWS, D_CONV), F32)],
        args=[proj, proj, x1, conv_dw, conv_b, ln_g, ln_b, conv_pw, pool_w, pool_scale, w_out], cargos=cargos)


def _mix_bwd_local(dx2, u1, mixed, ln_g, ln_b, conv_pw, pool_w, pool_scale, w_out, name, cargos=()):
    t_len, d = dx2.shape
    tm = min(TM_MIX, t_len)

    def body(dx_ref, u1_ref, mx_ref, lg_ref, lb_ref, pw_ref, plw_ref, ps_ref, wo_ref,
             du1_ref, dmx_ref, dco_ref, dpo_ref, dlg_ref, dlb_ref, dps_ref):
        @pl.when(pl.program_id(0) == 0)
        def _():
            dlg_ref[...] = jnp.zeros_like(dlg_ref)
            dlb_ref[...] = jnp.zeros_like(dlb_ref)
            dps_ref[...] = jnp.zeros_like(dps_ref)

        dcat = _dot_nt(dx_ref[...].astype(BF16), wo_ref[...])
        dco = dcat[:, 0:D_CONV].astype(BF16)
        dco_ref[...] = dco
        du3 = _dot_nt(dco, pw_ref[...])
        rstd, nhat = _layernorm_stats(u1_ref[...])
        u2 = nhat * lg_ref[...] + lb_ref[...]
        sig = jax.nn.sigmoid(u2)
        du2 = du3 * (sig * (1.0 + u2 * (1.0 - sig)))
        dlg_ref[...] += jnp.sum(du2 * nhat, axis=0, keepdims=True)
        dlb_ref[...] += jnp.sum(du2, axis=0, keepdims=True)
        dnhat = du2 * lg_ref[...]
        du1_ref[...] = rstd * (dnhat - jnp.mean(dnhat, axis=-1, keepdims=True)
                               - nhat * jnp.mean(dnhat * nhat, axis=-1, keepdims=True))

        for gi in range(len(POOL_WINDOWS)):
            cols = slice(gi * POOL_GROUP, (gi + 1) * POOL_GROUP)
            dpo = dcat[:, D_CONV + gi * POOL_GROUP:D_CONV + (gi + 1) * POOL_GROUP]
            pre = _dot(mx_ref[:, cols], plw_ref[gi])
            dps_ref[:, cols] += jnp.sum(dpo * pre, axis=0, keepdims=True)
            dout = (dpo * ps_ref[:, cols]).astype(BF16)
            dpo_ref[:, cols] = dout
            dmx_ref[:, cols] = _dot_nt(dout, plw_ref[gi])

    vec = _whole((1, D_CONV))
    return _call(
        body, name=name, grid=(t_len // tm,),
        in_specs=[_tile(tm, d), _tile(tm, D_CONV), _tile(tm, D_POOL), vec, vec, _whole((D_CONV, D_CONV)),
                  _whole((4, POOL_GROUP, POOL_GROUP)), vec, _whole((D_CONV + D_POOL, d))],
        out_specs=[_tile(tm, D_CONV), _tile(tm, D_POOL), _tile(tm, D_CONV), _tile(tm, D_POOL), vec, vec, vec],
        out_shape=[_sds((t_len, D_CONV), F32), _sds((t_len, D_POOL), F32), _sds((t_len, D_CONV), BF16),
                   _sds((t_len, D_POOL), BF16), _sds((1, D_CONV), F32), _sds((1, D_CONV), F32),
                   _sds((1, D_POOL), F32)],
        args=[dx2, u1, mixed, ln_g, ln_b, conv_pw, pool_w, pool_scale, w_out], cargos=cargos)


def _mix_bwd_seq(du1, dmixed, proj, x1, dx2, gain, conv_dw, w_in, name, cargos=()):
    t_len, d = x1.shape
    nq, _, nb = w_in.shape
    tm = min(TM_MIX, t_len)
    hb = tm // HALO
    last_block = t_len // HALO - 1
    n_tiles = t_len // tm

    def body(du_ref, dun_ref, dm_ref, dmn_ref, p_ref, tail_ref, x_ref, dx2_ref, g_ref, dw_ref, wi_ref,
             dx1_ref, dp_ref, ddw_ref, dcb_ref, dgain_ref, uext_s, dext_s, mext_s, ush_s, dsh_s):
        i = pl.program_id(0)
        first = i == 0
        last = i == n_tiles - 1

        @pl.when(first)
        def _():
            ddw_ref[...] = jnp.zeros_like(ddw_ref)
            dcb_ref[...] = jnp.zeros_like(dcb_ref)
            dgain_ref[...] = jnp.zeros_like(dgain_ref)

        a = p_ref[:, 0:D_CONV]
        g = p_ref[:, D_CONV:2 * D_CONV]
        sg = jax.nn.sigmoid(g)
        ta = tail_ref[:, 0:D_CONV]
        tg = tail_ref[:, D_CONV:2 * D_CONV]
        uext_s[0:HALO, :] = jnp.where(first, 0.0, ta * jax.nn.sigmoid(tg))
        uext_s[HALO:, :] = a * sg
        du1 = du_ref[...]
        dext_s[0:tm, :] = du1
        dext_s[tm:, :] = jnp.where(last, 0.0, dun_ref[...])

        _fill_shifted(uext_s, ush_s, tm)
        _fill_shifted(dext_s, dsh_s, tm)
        du0 = jnp.zeros((tm, D_CONV), F32)
        for k in range(CONV_WIDTH):
            du0 = du0 + dw_ref[k:k + 1, :] * _window(dext_s, dsh_s, CONV_WIDTH - 1 - k, tm)
            ddw_ref[k:k + 1, :] += jnp.sum(
                du1 * _window(uext_s, ush_s, HALO - (CONV_WIDTH - 1) + k, tm), axis=0, keepdims=True)
        dcb_ref[...] += jnp.sum(du1, axis=0, keepdims=True)
        dp_ref[:, 0:D_CONV] = (du0 * sg).astype(BF16)
        dp_ref[:, D_CONV:2 * D_CONV] = (du0 * a * sg * (1.0 - sg)).astype(BF16)

        pos1 = _positions(i, tm, tm) + 1.0
        pos1_next = _positions(i, tm, HALO, offset=tm) + 1.0
        for gi, w in enumerate(POOL_WINDOWS):
            cols = slice(gi * POOL_GROUP, (gi + 1) * POOL_GROUP)
            dm = dm_ref[:, cols]
            mext_s[0:tm, cols] = dm / jnp.minimum(pos1, float(w))
            mext_s[tm:, cols] = jnp.where(last, 0.0, dmn_ref[:, cols] / jnp.minimum(pos1_next, float(w)))
            s = mext_s[pl.ds(0, tm), cols]
            for j in range(1, w):
                s = s + mext_s[pl.ds(j, tm), cols]
            dp_ref[:, 2 * D_CONV + gi * POOL_GROUP:2 * D_CONV + (gi + 1) * POOL_GROUP] = (s - dm).astype(BF16)

        dh = _dot_nt(dp_ref[:, 0:nb], wi_ref[0])
        for q in range(1, nq):
            dh = dh + _dot_nt(dp_ref[:, q * nb:(q + 1) * nb], wi_ref[q])
        r, n = _rms_stats(x_ref[...])
        dgain_ref[...] += jnp.sum(dh * n, axis=0, keepdims=True)
        dx1_ref[...] = dx2_ref[...] + _rms_bwd(dh, n, r, g_ref[...])

    def nxt(cols):
        return pl.BlockSpec((HALO, cols), lambda i: (jnp.minimum((i + 1) * hb, last_block), 0))

    return _call(
        body, name=name, grid=(n_tiles,),
        in_specs=[_tile(tm, D_CONV), nxt(D_CONV), _tile(tm, D_POOL), nxt(D_POOL), _tile(tm, D_IN),
                  pl.BlockSpec((HALO, D_IN), lambda i: (jnp.maximum(i * hb - 1, 0), 0)),
                  _tile(tm, d), _tile(tm, d), _whole((1, d)), _whole((CONV_WIDTH + 1, D_CONV)),
                  _whole((nq, d, nb))],
        out_specs=[_tile(tm, d), _tile(tm, D_IN), _whole((CONV_WIDTH + 1, D_CONV)), _whole((1, D_CONV)),
                   _whole((1, d))],
        out_shape=[_sds((t_len, d), F32), _sds((t_len, D_IN), BF16), _sds((CONV_WIDTH + 1, D_CONV), F32),
                   _sds((1, D_CONV), F32), _sds((1, d), F32)],
        scratch_shapes=[pltpu.VMEM((tm + HALO, D_CONV), F32), pltpu.VMEM((tm + HALO, D_CONV), F32),
                        pltpu.VMEM((tm + HALO, D_POOL), F32),
                        pltpu.VMEM((SUBLANES - 1, tm + SHIFT_ROWS, D_CONV), F32),
                        pltpu.VMEM((SUBLANES - 1, tm + SHIFT_ROWS, D_CONV), F32)],
        args=[du1, du1, dmixed, dmixed, proj, proj, x1, dx2, gain, conv_dw, w_in], cargos=cargos)


def _final_norm_loss(x3, target, gain, name):
    t_len, d = x3.shape
    tm = min(TM_FFN, t_len)

    def body(x_ref, t_ref, g_ref, dx_ref, loss_ref, dgain_ref):
        @pl.when(pl.program_id(0) == 0)
        def _():
            loss_ref[...] = jnp.zeros_like(loss_ref)
            dgain_ref[...] = jnp.zeros_like(dgain_ref)

        r, n = _rms_stats(x_ref[...])
        err = n * g_ref[...] - t_ref[...]
        per_tok = jnp.sum(err * err, axis=-1, keepdims=True) * (1.0 / d)
        loss_ref[...] += 0.5 * jnp.sum(per_tok, axis=0, keepdims=True)
        dy = err * (1.0 / d)
        dgain_ref[...] += jnp.sum(dy * n, axis=0, keepdims=True)
        dx_ref[...] = _rms_bwd(dy, n, r, g_ref[...])

    tok = pl.BlockSpec((tm, d), lambda i: (i, 0))
    outs, _ = _call(
        body, name=name, grid=(t_len // tm,),
        in_specs=[tok, tok, pl.BlockSpec((1, d), lambda i: (0, 0))],
        out_specs=[tok, pl.BlockSpec((1, 128), lambda i: (0, 0)), pl.BlockSpec((1, d), lambda i: (0, 0))],
        out_shape=[_sds((t_len, d), F32), _sds((1, 128), F32), _sds((1, d), F32)],
        args=[x3, target, gain])
    return outs


def _row_tile(rows):
    return rows // 4 if rows % 64 == 0 else rows


def _sum_parts(parts, name):
    n, r, c = parts.shape
    tr = _row_tile(r)

    def body(p_ref, o_ref):
        s = p_ref[0].astype(F32)
        for k in range(1, n):
            s = s + p_ref[k].astype(F32)
        o_ref[...] = s

    (out,), _ = _call(body, name=name, grid=(r // tr,),
                      in_specs=[pl.BlockSpec((n, tr, c), lambda i: (0, i, 0))],
                      out_specs=[pl.BlockSpec((tr, c), lambda i: (i, 0))], out_shape=[_sds((r, c), F32)],
                      args=[parts])
    return out


def _adamw_math(w, g, m, v):
    m = ADAM_B1 * m + (1.0 - ADAM_B1) * g
    v = ADAM_B2 * v + (1.0 - ADAM_B2) * (g * g)
    m_hat = m / (1.0 - ADAM_B1 ** ADAM_STEP)
    v_hat = v / (1.0 - ADAM_B2 ** ADAM_STEP)
    delta = -ADAM_LR * (m_hat / (jnp.sqrt(v_hat) + ADAM_EPS) + ADAM_WD * w)
    return delta, m, v


def _adamw(parts, w, m, v, name):
    r, c = w.shape
    n = len(parts)
    tr = _row_tile(r)

    def body(*refs):
        terms = []
        for p_ref in refs[:n]:
            terms += [p_ref[...]] if len(p_ref.shape) == 2 else [p_ref[k] for k in range(p_ref.shape[0])]
        w_ref, m_ref, v_ref, g_out, d_out, m_out, v_out = refs[n:]
        g = terms[0]
        for t in terms[1:]:
            g = g + t
        delta, nm, nv = _adamw_math(w_ref[...], g, m_ref[...], v_ref[...])
        g_out[...] = g
        d_out[...] = delta
        m_out[...] = nm
        v_out[...] = nv

    blk = pl.BlockSpec((tr, c), lambda i: (i, 0))
    p_specs = [blk if p.ndim == 2 else pl.BlockSpec((p.shape[0], tr, c), lambda i: (0, i, 0)) for p in parts]
    outs, _ = _call(body, name=name, grid=(r // tr,), in_specs=p_specs + [blk, blk, blk],
                    out_specs=[blk] * 4, out_shape=[_sds((r, c), F32)] * 4, args=[*parts, w, m, v])
    return outs


FFN_W = ("w_gate", "w_up", "w_down")
MID = ("w_in", "conv_dw", "conv_pw", "w_out")
SMALL_1024 = ("ffn1_norm", "mix_norm", "ffn2_norm", "final_norm")
SMALL_512 = ("conv_dw_b", "conv_ln_g", "conv_ln_b", "pool_scale")
WEIGHTS = ("ffn1_norm", "ffn1_w_gate", "ffn1_w_up", "ffn1_w_down", "mix_norm", "w_in", "conv_dw", "conv_dw_b",
           "conv_ln_g", "conv_ln_b", "conv_pw", "pool_w", "pool_scale", "w_out", "ffn2_norm", "ffn2_w_gate",
           "ffn2_w_up", "ffn2_w_down", "final_norm")
PACK_ROWS = 72


def _pad_rows(a, rows):
    return jnp.pad(a, ((0, rows - a.shape[0]), (0, 0)))


def _pack_small(t):
    rows = [t[k].reshape(1, D_MODEL) for k in SMALL_1024]
    rows.append(jnp.concatenate([t["conv_dw_b"].reshape(1, -1), t["conv_ln_g"].reshape(1, -1)], axis=1))
    rows.append(jnp.concatenate([t["conv_ln_b"].reshape(1, -1), t["pool_scale"].reshape(1, -1)], axis=1))
    rows.append(t["pool_w"].reshape(64, D_MODEL))
    return _pad_rows(jnp.concatenate(rows, axis=0), PACK_ROWS)


def _unpack_small(p):
    out = {k: p[i] for i, k in enumerate(SMALL_1024)}
    out["conv_dw_b"], out["conv_ln_g"] = p[4, :D_CONV], p[4, D_CONV:]
    out["conv_ln_b"], out["pool_scale"] = p[5, :D_CONV], p[5, D_CONV:]
    out["pool_w"] = p[6:70].reshape(4, POOL_GROUP, POOL_GROUP)
    return out


def _as_stored(name, a):
    if name.endswith(("w_gate", "w_up")):
        return a.T
    if name == "conv_dw":
        return _pad_rows(a, CONV_WIDTH + 1)
    return a


def _as_given(name, a):
    if name.endswith(("w_gate", "w_up")):
        return a.T
    if name == "conv_dw":
        return a[:CONV_WIDTH]
    return a


def kernel(x, ffn1_norm, ffn1_w_gate, ffn1_w_up, ffn1_w_down, mix_norm, w_in, conv_dw, conv_dw_b, conv_ln_g, conv_ln_b, conv_pw, pool_w, pool_scale, w_out, ffn2_norm, ffn2_w_gate, ffn2_w_up, ffn2_w_down, final_norm, loss_target, m_ffn1_norm, m_ffn1_w_gate, m_ffn1_w_up, m_ffn1_w_down, m_mix_norm, m_w_in, m_conv_dw, m_conv_dw_b, m_conv_ln_g, m_conv_ln_b, m_conv_pw, m_pool_w, m_pool_scale, m_w_out, m_ffn2_norm, m_ffn2_w_gate, m_ffn2_w_up, m_ffn2_w_down, m_final_norm, v_ffn1_norm, v_ffn1_w_gate, v_ffn1_w_up, v_ffn1_w_down, v_mix_norm, v_w_in, v_conv_dw, v_conv_dw_b, v_conv_ln_g, v_conv_ln_b, v_conv_pw, v_pool_w, v_pool_scale, v_w_out, v_ffn2_norm, v_ffn2_w_gate, v_ffn2_w_up, v_ffn2_w_down, v_final_norm):
    given = dict(locals())
    wts = {k: given[k] for k in WEIGHTS}
    mom_m = {k: given["m_" + k] for k in WEIGHTS}
    mom_v = {k: given["v_" + k] for k in WEIGHTS}
    xt, target = x[0], loss_target[0]

    shard = {k: _as_stored(k, wts[k]) if k == "conv_dw" else _as_stored(k, wts[k]).astype(BF16)
             for k in WEIGHTS if k.endswith(FFN_W) or k in MID}
    w = {k: wts[k].reshape(1, -1) for k in SMALL_1024 + SMALL_512}
    w["pool_w"] = wts["pool_w"].astype(BF16)

    (h1, s1, p1, a1, w["ffn1_w_gate"], w["ffn1_w_up"]), ((w["ffn1_w_down"],),) = _ffn_up_gather(
        xt, w["ffn1_norm"], shard["ffn1_w_gate"], shard["ffn1_w_up"], "ffn1_up_gather",
        cargos=[Cargo("gather_slots", [shard["ffn1_w_down"]])])
    x1, (mid,) = _ffn_down(xt, a1, w["ffn1_w_down"], "ffn1_down",
                           cargos=[Cargo("gather_chips", [shard[k] for k in MID])])
    w["w_in"] = mid[0]
    w["conv_dw"] = mid[1].transpose(1, 0, 2).reshape(CONV_WIDTH + 1, D_CONV)
    w["conv_pw"] = mid[2].reshape(D_CONV, D_CONV)
    w["w_out"] = mid[3].reshape(D_CONV + D_POOL, D_MODEL)
    (h2, proj), ((w["ffn2_w_down"],),) = _mix_in(x1, w["mix_norm"], w["w_in"], "mix_in",
                                                  cargos=[Cargo("gather_slots", [shard["ffn2_w_down"]])])
    (x2, u1, u3, mixed, cat), ((w["ffn2_w_gate"], w["ffn2_w_up"]),) = _mix_fwd(
        proj, x1, w["conv_dw"], w["conv_dw_b"], w["conv_ln_g"], w["conv_ln_b"], w["conv_pw"], w["pool_w"],
        w["pool_scale"], w["w_out"], "mix_fwd",
        cargos=[Cargo("gather_slots", [shard["ffn2_w_gate"], shard["ffn2_w_up"]])])
    x3, h3, s2, p2, a2 = _ffn_fwd(x2, w["ffn2_norm"], w["ffn2_w_gate"], w["ffn2_w_up"], w["ffn2_w_down"], "ffn2_fwd")
    dx3, loss, d_final = _final_norm_loss(x3, target, w["final_norm"], "final_norm_loss")
    loss = lax.psum(loss[0, 0], ("x", "y", "c"))

    g = {"final_norm": d_final}
    sums = {}

    def landed(names, parts):
        for k, p in zip(names, parts):
            sums[k] = _sum_parts(p, "sum_chips_" + k)

    dx2, g["ffn2_norm"], df2, dg2, du2 = _ffn_bwd(dx3, x2, w["ffn2_norm"], s2, p2, w["ffn2_w_gate"],
                                                   w["ffn2_w_up"], w["ffn2_w_down"], "ffn2_bwd")
    def ffn_wgrad(name, hid, tok, cargos=()):
        parts, cargo_outs = _wgrad_hid_tok_scatter(hid, tok, name.replace("_w_", "_dw_"), cargos=cargos)
        landed([name], [parts])
        return cargo_outs

    ffn_wgrad("ffn2_w_gate", dg2, h3)
    ffn_wgrad("ffn2_w_up", du2, h3)
    ffn_wgrad("ffn2_w_down", a2, df2)
    (du1, dmixed, dco, dpo, g["conv_ln_g"], g["conv_ln_b"], g["pool_scale"]), (swapped2,) = _mix_bwd_local(
        dx2, u1, mixed, w["conv_ln_g"], w["conv_ln_b"], w["conv_pw"], w["pool_w"], w["pool_scale"], w["w_out"],
        "mix_bwd_local", cargos=[Cargo("swap", [sums["ffn2_" + k] for k in FFN_W])])
    g_out, _ = _wgrad_2d(cat, dx2, 1, BF16, "dw_out")
    g_pw, _ = _wgrad_2d(u3, dco, 1, BF16, "dconv_pw")
    g["pool_w"], _ = _wgrad_2d(mixed, dpo, 4, F32, "dpool_w", group_diag=True)
    slabs = [g_pw.reshape(N_CHIPS, D_CONV // N_CHIPS, D_CONV),
             g_out.reshape(N_CHIPS, (D_CONV + D_POOL) // N_CHIPS, D_MODEL)]
    (dx1, dproj, g_dw, g["conv_dw_b"], g["mix_norm"]), (parts,) = _mix_bwd_seq(
        du1, dmixed, proj, x1, dx2, w["mix_norm"], w["conv_dw"], w["w_in"], "mix_bwd_seq",
        cargos=[Cargo("scatter_chips", slabs)])
    landed(["conv_pw", "w_out"], parts)
    g_in, _ = _wgrad_2d(h2, dproj, N_CHIPS, BF16, "dw_in")
    dx, g["ffn1_norm"], df1, dg1, du1_ = _ffn_bwd(dx1, xt, w["ffn1_norm"], s1, p1, w["ffn1_w_gate"],
                                                   w["ffn1_w_up"], w["ffn1_w_down"], "ffn1_bwd")
    slabs = [g_in, g_dw.reshape(CONV_WIDTH + 1, N_CHIPS, D_CONV // N_CHIPS).transpose(1, 0, 2)]
    parts, small_parts = ffn_wgrad(
        "ffn1_w_gate", dg1, h1,
        cargos=[Cargo("scatter_chips", slabs), Cargo("gather_devices", [_pack_small(g)])])
    landed(["w_in", "conv_dw"], parts)
    swapped_mid, swapped_gate = ffn_wgrad(
        "ffn1_w_up", du1_, h1,
        cargos=[Cargo("swap", [sums[k] for k in MID]), Cargo("swap", [sums["ffn1_w_gate"]])])
    (swapped_up,) = ffn_wgrad("ffn1_w_down", a1, df1, cargos=[Cargo("swap", [sums["ffn1_w_up"]])])
    swapped_down = _exchange(Cargo("swap", [sums["ffn1_w_down"]]), "swap_last")

    theirs = dict(zip(["ffn2_" + k for k in FFN_W], swapped2))
    theirs.update(zip(MID, swapped_mid))
    theirs.update(ffn1_w_gate=swapped_gate[0], ffn1_w_up=swapped_up[0], ffn1_w_down=swapped_down[0])
    grads, deltas, new_m, new_v = {}, {}, {}, {}
    for k in theirs:
        res = _adamw([sums[k], theirs[k]], _as_stored(k, wts[k]), _as_stored(k, mom_m[k]),
                     _as_stored(k, mom_v[k]), "adamw_" + k)
        grads[k], deltas[k], new_m[k], new_v[k] = [_as_given(k, t) for t in res]
    res = _adamw(small_parts, _pack_small(wts), _pack_small(mom_m), _pack_small(mom_v), "adamw_small")
    for dst, packed in zip((grads, deltas, new_m, new_v), res):
        dst.update(_unpack_small(packed))

    out = [loss, dx[None]]
    for group in (grads, deltas, new_m, new_v):
        out += [group[k] for k in WEIGHTS]
    return tuple(out)
```

```python
import functools

import jax
import jax.numpy as jnp
from jax import lax
from jax.experimental import pallas as pl
from jax.experimental.pallas import tpu as pltpu

F32 = jnp.float32
BF16 = jnp.bfloat16
MESH = pl.DeviceIdType.MESH

N_CHIPS = 4
N_DEV = 8
D_MODEL = 1024
D_CONV = 512
D_POOL = 512
CONV_WIDTH = 31
POOL_WINDOWS = (2, 4, 8, 16)
POOL_GROUP = 128
D_IN = 2 * D_CONV + D_POOL
HALO = 32
RMS_EPS = 1e-6
LN_EPS = 1e-5
FFN_RES_WEIGHT = 0.5
ADAM_LR = 0.001
ADAM_B1 = 0.9
ADAM_B2 = 0.999
ADAM_EPS = 1e-08
ADAM_WD = 0.01
ADAM_STEP = 10
VMEM_LIMIT_BYTES = 52 * 1024 * 1024
TM_FFN = 512
TM_MIX = 256
TT_WGRAD = 2048
STRIP = 16
SUBLANES = 8
RELAY_AT_EIGHTHS = 5

HBM = pl.BlockSpec(memory_space=pl.ANY)


def _dot(a, b):
    return jnp.dot(a, b, preferred_element_type=F32)


def _dot_nt(a, b):
    return lax.dot_general(a, b, (((1,), (1,)), ((), ())), preferred_element_type=F32)


def _dot_tn(a, b):
    return lax.dot_general(a, b, (((0,), (0,)), ((), ())), preferred_element_type=F32)


def _sds(shape, dtype):
    return jax.ShapeDtypeStruct(shape, dtype)


def _rms_stats(xv):
    r = lax.rsqrt(jnp.mean(xv * xv, axis=-1, keepdims=True) + RMS_EPS)
    return r, xv * r


def _swiglu_saved(gate, up):
    sig = jax.nn.sigmoid(gate)
    silu = gate * sig
    return silu, up * (sig * (1.0 + gate * (1.0 - sig))), silu * up


def _rms_bwd(dh, n, r, gain):
    dn = dh * gain
    return r * (dn - n * jnp.mean(dn * n, axis=-1, keepdims=True))


def _place():
    x, y, c = lax.axis_index("x"), lax.axis_index("y"), lax.axis_index("c")
    return x, y, c, [(1 - x, y), (x, 1 - y), (1 - x, 1 - y)]


class Cargo:
    def __init__(self, kind, arrays):
        self.kind, self.arrays = kind, list(arrays)
        n = len(self.arrays)
        self.two_level = kind in ("gather_slots", "gather_chips")
        if self.two_level:
            self.out_shape = [_sds((N_CHIPS,) + a.shape, a.dtype) for a in self.arrays]
        elif kind == "gather_devices":
            self.out_shape = [_sds((N_DEV,) + a.shape, a.dtype) for a in self.arrays]
        else:
            self.out_shape = [_sds(a.shape, a.dtype) for a in self.arrays]
        n_remote = n * {"swap": 1, "gather_devices": N_DEV - 1}.get(kind, N_CHIPS - 1)
        n_own = 0 if kind == "swap" else n
        n_relay = n_remote if self.two_level else 0
        dma = pltpu.SemaphoreType.DMA
        self.scratch = [dma((n_remote,)), dma((n_remote,)), dma((max(n_own, 1),)),
                        dma((max(n_relay, 1),)), dma((max(n_relay, 1),))]

    def _plan(self, ins, outs):
        x, y, c, chips = _place()
        q = 2 * x + y
        sibling = (x, y, 1 - c)
        own, remote, relays = [], [], []
        for a, o in zip(ins, outs):
            if self.two_level:
                half = a.shape[0] // 2
                mine = pl.ds(pl.multiple_of(c * half, SUBLANES), half)
                theirs = pl.ds(pl.multiple_of((1 - c) * half, SUBLANES), half)
                own.append((a, o.at[0 if self.kind == "gather_slots" else q]))
                for j, (px, py) in enumerate(chips):
                    there, here = (j + 1, j + 1) if self.kind == "gather_slots" else (q, 2 * px + py)
                    remote.append((a.at[mine], o.at[there, mine], o.at[here, mine], (px, py, c)))
                    relays.append((o.at[here, mine], o.at[here, mine], o.at[here, theirs], sibling))
            elif self.kind == "scatter_chips":
                own.append((a.at[q], o.at[q]))
                remote += [(a.at[2 * px + py], o.at[q], o.at[2 * px + py], (px, py, c)) for px, py in chips]
            elif self.kind == "swap":
                remote.append((a, o, o, sibling))
            else:
                own.append((a, o.at[4 * x + 2 * y + c]))
                for k in range(1, N_DEV):
                    px, py, pc = x ^ (k >> 2 & 1), y ^ (k >> 1 & 1), c ^ (k & 1)
                    remote.append((a, o.at[4 * x + 2 * y + c], o.at[4 * px + 2 * py + pc], (px, py, pc)))
        return own, remote, relays

    @staticmethod
    def _copies(entries, send_sems, recv_sems):
        out = []
        for k, (src, dst, landed, peer) in enumerate(entries):
            def make(dst_ref, k=k, src=src, peer=peer):
                return pltpu.make_async_remote_copy(src_ref=src, dst_ref=dst_ref, send_sem=send_sems.at[k],
                                                    recv_sem=recv_sems.at[k], device_id=peer, device_id_type=MESH)
            out.append((make(dst), make(landed)))
        return out

    def start(self, ins, outs, sems):
        own, remote, _ = self._plan(ins, outs)
        for k, (src, dst) in enumerate(own):
            pltpu.make_async_copy(src, dst, sems[2].at[k]).start()
        for mine, _ in self._copies(remote, sems[0], sems[1]):
            mine.start()

    def relay(self, ins, outs, sems):
        _, remote, relays = self._plan(ins, outs)
        passed = self._copies(relays, sems[3], sems[4])
        for (_, arriving), (mine, _) in zip(self._copies(remote, sems[0], sems[1]), passed):
            arriving.wait_recv()
            mine.start()

    def wait(self, ins, outs, sems):
        own, remote, relays = self._plan(ins, outs)
        for mine, arriving in self._copies(remote, sems[0], sems[1]):
            mine.wait_send()
            if not self.two_level:
                arriving.wait_recv()
        for mine, arriving in self._copies(relays, sems[3], sems[4]):
            mine.wait_send()
            arriving.wait_recv()
        for k, (src, dst) in enumerate(own):
            pltpu.make_async_copy(src, dst, sems[2].at[k]).wait()


N_CARGO_SEMS = 5


def _call(body, *, name, grid, in_specs, out_specs, out_shape, args, scratch_shapes=(), cargos=()):
    n_in, n_out, n_scr = len(in_specs), len(out_specs), len(scratch_shapes)
    c_in = [len(cg.arrays) for cg in cargos]
    n_cin = sum(c_in)

    def wrapped(*refs):
        ins = refs[:n_in]
        cins = refs[n_in:n_in + n_cin]
        outs = refs[n_in + n_cin:n_in + n_cin + n_out]
        couts = refs[n_in + n_cin + n_out:n_in + 2 * n_cin + n_out]
        scr = refs[n_in + 2 * n_cin + n_out:n_in + 2 * n_cin + n_out + n_scr]
        sems = refs[n_in + 2 * n_cin + n_out + n_scr:]
        step, n_steps = 0, 1
        for ax, size in enumerate(grid):
            step = step * size + pl.program_id(ax)
            n_steps *= size

        def each(method, only_two_level=False):
            at = 0
            for k, cg in enumerate(cargos):
                if cg.two_level or not only_two_level:
                    getattr(cg, method)(cins[at:at + c_in[k]], couts[at:at + c_in[k]],
                                        sems[N_CARGO_SEMS * k:N_CARGO_SEMS * (k + 1)])
                at += c_in[k]

        body(*ins, *outs, *scr)
        if cargos:
            pl.when(step == 0)(lambda: each("start"))
        if any(cg.two_level for cg in cargos):
            pl.when(step == (RELAY_AT_EIGHTHS * n_steps) // 8)(lambda: each("relay", only_two_level=True))
        if cargos:
            pl.when(step == n_steps - 1)(lambda: each("wait"))

    res = pl.pallas_call(
        wrapped, name=name, grid=grid,
        in_specs=list(in_specs) + [HBM] * n_cin,
        out_specs=list(out_specs) + [HBM] * n_cin,
        out_shape=list(out_shape) + [s for cg in cargos for s in cg.out_shape],
        scratch_shapes=list(scratch_shapes) + [s for cg in cargos for s in cg.scratch],
        compiler_params=pltpu.CompilerParams(dimension_semantics=("arbitrary",) * len(grid),
                                             vmem_limit_bytes=VMEM_LIMIT_BYTES),
    )(*args, *[a for cg in cargos for a in cg.arrays])
    outs, rest = list(res[:n_out]), list(res[n_out:])
    cargo_outs = []
    for k in c_in:
        cargo_outs.append(rest[:k])
        rest = rest[k:]
    return outs, cargo_outs


def _exchange(cargo, name):
    _, (outs,) = _call(lambda: None, name=name, grid=(1,), in_specs=[], out_specs=[], out_shape=[], args=[],
                       cargos=[cargo])
    return outs


def _ffn_up_gather(x, gain, wg_t, wu_t, name, cargos=()):
    t_len, d = x.shape
    fq = wg_t.shape[0]
    tm = min(TM_FFN, t_len)
    n_tiles = t_len // tm

    def body(x_ref, g_ref, wg_in, wu_in, h_ref, s_ref, p_ref, a_ref, wg_all, wu_all,
             wg_v, wu_v, h_all, send_sems, recv_sems, pass_send_sems, pass_recv_sems, own_sems, load_sems):
        s = pl.program_id(0)
        i = pl.program_id(1)
        x_, y_, c_, chips = _place()
        shards = ((wg_in, wg_all, wg_v), (wu_in, wu_all, wu_v))
        mine = pl.ds(pl.multiple_of(c_ * (fq // 2), SUBLANES), fq // 2)
        theirs = pl.ds(pl.multiple_of((1 - c_) * (fq // 2), SUBLANES), fq // 2)

        def to_peer(k, j):
            w_in, w_all, _ = shards[k]
            return pltpu.make_async_remote_copy(
                src_ref=w_in.at[mine], dst_ref=w_all.at[j + 1, mine], send_sem=send_sems.at[3 * k + j],
                recv_sem=recv_sems.at[3 * k + j], device_id=(*chips[j], c_), device_id_type=MESH)

        def to_sibling(k, j, landing=False):
            w_all = shards[k][1]
            return pltpu.make_async_remote_copy(
                src_ref=w_all.at[j + 1, mine], dst_ref=w_all.at[j + 1, theirs if landing else mine],
                send_sem=pass_send_sems.at[3 * k + j], recv_sem=pass_recv_sems.at[3 * k + j],
                device_id=(x_, y_, 1 - c_), device_id_type=MESH)

        def keep(k):
            return pltpu.make_async_copy(shards[k][0], shards[k][1].at[0], own_sems.at[k])

        @pl.when((s == 0) & (i == 0))
        def _():
            for j in range(N_CHIPS - 1):
                for k in range(2):
                    to_peer(k, j).start()
            for k in range(2):
                keep(k).start()

        for slot in range(N_CHIPS):
            @pl.when((s == slot) & (i == 0))
            def _():
                if slot > 0:
                    for k in range(2):
                        to_peer(k, slot - 1).wait_recv()
                        to_sibling(k, slot - 1).start()
                    for k in range(2):
                        to_sibling(k, slot - 1, landing=True).wait_recv()
                loads = []
                for k in range(2):
                    src = shards[k][0] if slot == 0 else shards[k][1].at[slot]
                    loads.append(pltpu.make_async_copy(src, shards[k][2], load_sems.at[k]))
                    loads[-1].start()
                for ld in loads:
                    ld.wait()

        @pl.when(s == 0)
        def _():
            _, n = _rms_stats(x_ref[...])
            h_new = (n * g_ref[...]).astype(BF16)
            h_ref[...] = h_new
            h_all[i] = h_new

        h = h_all[i]
        silu, dgate, act = _swiglu_saved(_dot_nt(h, wg_v[...]), _dot_nt(h, wu_v[...]))
        s_ref[...] = silu.astype(BF16)
        p_ref[...] = dgate.astype(BF16)
        a_ref[...] = act.astype(BF16)

        @pl.when((s == N_CHIPS - 1) & (i == n_tiles - 1))
        def _():
            for k in range(2):
                for j in range(N_CHIPS - 1):
                    to_peer(k, j).wait_send()
                    to_sibling(k, j).wait_send()
                keep(k).wait()

    tok = pl.BlockSpec((tm, d), lambda s, i: (jnp.where(s == 0, i, n_tiles - 1), 0))
    hid = pl.BlockSpec((None, tm, fq), lambda s, i: (s, i, 0))
    outs, cargo_outs = _call(
        body, name=name, grid=(N_CHIPS, n_tiles),
        in_specs=[tok, pl.BlockSpec((1, d), lambda s, i: (0, 0)), HBM, HBM],
        out_specs=[tok, hid, hid, hid, HBM, HBM],
        out_shape=[_sds((t_len, d), BF16)] + [_sds((N_CHIPS, t_len, fq), BF16)] * 3
        + [_sds((N_CHIPS, fq, d), BF16)] * 2,
        scratch_shapes=[pltpu.VMEM((fq, d), BF16), pltpu.VMEM((fq, d), BF16), pltpu.VMEM((n_tiles, tm, d), BF16)]
        + [pltpu.SemaphoreType.DMA((6,))] * 4 + [pltpu.SemaphoreType.DMA((2,))] * 2,
        args=[x, gain, wg_t, wu_t], cargos=cargos)
    return outs, cargo_outs


def _load_once(hbm_refs, vmem_refs, sems, first):
    @pl.when(first)
    def _():
        copies = [pltpu.make_async_copy(src, dst, sems.at[k]) for k, (src, dst) in enumerate(zip(hbm_refs, vmem_refs))]
        for cp in copies:
            cp.start()
        for cp in copies:
            cp.wait()


def _ffn_down(x, act, wd, name, cargos=()):
    t_len, d = x.shape
    nq, fq, _ = wd.shape
    tm = min(TM_FFN, t_len)

    def body(x_ref, a_ref, wd_ref, xo_ref):
        y = _dot(a_ref[0], wd_ref[0])
        for j in range(1, nq):
            y = y + _dot(a_ref[j], wd_ref[j])
        xo_ref[...] = x_ref[...] + FFN_RES_WEIGHT * y

    tok = pl.BlockSpec((tm, d), lambda i: (i, 0))
    (xo,), cargo_outs = _call(
        body, name=name, grid=(t_len // tm,),
        in_specs=[tok, pl.BlockSpec((nq, tm, fq), lambda i: (0, i, 0)), pl.BlockSpec((nq, fq, d), lambda i: (0, 0, 0))],
        out_specs=[tok], out_shape=[_sds((t_len, d), F32)], args=[x, act, wd], cargos=cargos)
    return xo, cargo_outs


def _ffn_fwd(x, gain, wg_t, wu_t, wd, name):
    t_len, d = x.shape
    nq, fq, _ = wd.shape
    tm = min(TM_FFN, t_len)

    def body(x_ref, g_ref, wg_hbm, wu_hbm, wd_hbm, xo_ref, h_ref, s_ref, p_ref, a_ref,
             h_s, acc, wg_v, wu_v, wd_v, load_sems):
        i = pl.program_id(0)
        j = pl.program_id(1)
        _load_once((wg_hbm, wu_hbm, wd_hbm), (wg_v, wu_v, wd_v), load_sems, (i == 0) & (j == 0))

        @pl.when(j == 0)
        def _():
            _, n = _rms_stats(x_ref[...])
            h = (n * g_ref[...]).astype(BF16)
            h_s[...] = h
            h_ref[...] = h
            acc[...] = jnp.zeros_like(acc)

        h = h_s[...]
        silu, dgate, act = _swiglu_saved(_dot_nt(h, wg_v[j]), _dot_nt(h, wu_v[j]))
        s_ref[...] = silu.astype(BF16)
        p_ref[...] = dgate.astype(BF16)
        a_ref[...] = act.astype(BF16)
        acc[...] += _dot(a_ref[...], wd_v[j])

        @pl.when(j == nq - 1)
        def _():
            xo_ref[...] = x_ref[...] + FFN_RES_WEIGHT * acc[...]

    tok = pl.BlockSpec((tm, d), lambda i, j: (i, 0))
    hid = pl.BlockSpec((None, tm, fq), lambda i, j: (j, i, 0))
    outs, _ = _call(
        body, name=name, grid=(t_len // tm, nq),
        in_specs=[tok, pl.BlockSpec((1, d), lambda i, j: (0, 0)), HBM, HBM, HBM],
        out_specs=[tok, tok, hid, hid, hid],
        out_shape=[_sds((t_len, d), F32), _sds((t_len, d), BF16)] + [_sds((nq, t_len, fq), BF16)] * 3,
        scratch_shapes=[pltpu.VMEM((tm, d), BF16), pltpu.VMEM((tm, d), F32)]
        + [pltpu.VMEM((nq, fq, d), BF16)] * 3 + [pltpu.SemaphoreType.DMA((3,))],
        args=[x, gain, wg_t, wu_t, wd])
    return outs


def _ffn_bwd(dy, x_in, gain, silu, dgate_du, wg_t, wu_t, wd, name):
    t_len, d = dy.shape
    nq, fq, _ = wd.shape
    tm = min(TM_FFN, t_len)

    def body(dy_ref, x_ref, g_ref, s_ref, p_ref, wg_hbm, wu_hbm, wd_hbm,
             dx_ref, dgain_ref, df_ref, dg_ref, du_ref, df_s, dh_acc, dact_s, wg_v, wu_v, wd_v, load_sems):
        i = pl.program_id(0)
        j = pl.program_id(1)
        _load_once((wg_hbm, wu_hbm, wd_hbm), (wg_v, wu_v, wd_v), load_sems, (i == 0) & (j == 0))

        @pl.when((i == 0) & (j == 0))
        def _():
            dgain_ref[...] = jnp.zeros_like(dgain_ref)

        @pl.when(j == 0)
        def _():
            df = (FFN_RES_WEIGHT * dy_ref[...]).astype(BF16)
            df_s[...] = df
            df_ref[...] = df
            dh_acc[...] = jnp.zeros_like(dh_acc)

        half = tm // 2
        for r0 in (0, half):
            dact_s[r0:r0 + half, :] = _dot_nt(df_s[r0:r0 + half, :], wd_v[j])

        for r0 in range(0, tm, STRIP):
            dact = dact_s[r0:r0 + STRIP, :]
            dg_ref[r0:r0 + STRIP, :] = (dact * p_ref[r0:r0 + STRIP, :].astype(F32)).astype(BF16)
            du_ref[r0:r0 + STRIP, :] = (dact * s_ref[r0:r0 + STRIP, :].astype(F32)).astype(BF16)

        for r0 in (0, half):
            rows = slice(r0, r0 + half)
            dh_acc[rows, :] += _dot(dg_ref[rows, :], wg_v[j]) + _dot(du_ref[rows, :], wu_v[j])

        @pl.when(j == nq - 1)
        def _():
            r, n = _rms_stats(x_ref[...])
            dh = dh_acc[...]
            dgain_ref[...] += jnp.sum(dh * n, axis=0, keepdims=True)
            dx_ref[...] = dy_ref[...] + _rms_bwd(dh, n, r, g_ref[...])

    tok = pl.BlockSpec((tm, d), lambda i, j: (i, 0))
    vec = pl.BlockSpec((1, d), lambda i, j: (0, 0))
    hid = pl.BlockSpec((None, tm, fq), lambda i, j: (j, i, 0))
    outs, _ = _call(
        body, name=name, grid=(t_len // tm, nq),
        in_specs=[tok, tok, vec, hid, hid, HBM, HBM, HBM],
        out_specs=[tok, vec, tok, hid, hid],
        out_shape=[_sds((t_len, d), F32), _sds((1, d), F32), _sds((t_len, d), BF16),
                   _sds((nq, t_len, fq), BF16), _sds((nq, t_len, fq), BF16)],
        scratch_shapes=[pltpu.VMEM((tm, d), BF16), pltpu.VMEM((tm, d), F32), pltpu.VMEM((tm, fq), F32)]
        + [pltpu.VMEM((nq, fq, d), BF16)] * 3 + [pltpu.SemaphoreType.DMA((3,))],
        args=[dy, x_in, gain, silu, dgate_du, wg_t, wu_t, wd])
    return outs


def _wgrad(lhs, rhs, l_spec, r_spec, out_shape, out_spec, acc_shape, grid, name, cargos=()):
    n_t = grid[-1]
    t_axis = len(grid) - 1

    def body(l_ref, r_ref, o_ref, acc):
        t = pl.program_id(t_axis)

        @pl.when(t == 0)
        def _():
            acc[...] = jnp.zeros_like(acc)

        acc[...] += _dot_tn(l_ref[...].astype(BF16), r_ref[...].astype(BF16))

        @pl.when(t == n_t - 1)
        def _():
            o_ref[...] = acc[...].astype(o_ref.dtype)

    (out,), cargo_outs = _call(
        body, name=name, grid=grid, in_specs=[l_spec, r_spec], out_specs=[out_spec], out_shape=[out_shape],
        scratch_shapes=[pltpu.VMEM(acc_shape, F32)], args=[lhs, rhs], cargos=cargos)
    return out, cargo_outs


def _wgrad_hid_tok_scatter(hid, tok, name, cargos=()):
    t_len, d = tok.shape
    nq, _, fq = hid.shape
    half = fq // 2
    tt = min(TT_WGRAD, t_len)
    n_t = t_len // tt

    def body(l_ref, r_ref, parts_ref, acc, stage, pair, summed, zeros,
             pair_send_sems, pair_recv_sems, send_sems, recv_sems, own_sem, zero_sems):
        g = pl.program_id(0)
        t = pl.program_id(1)
        x_, y_, c_, chips = _place()
        mine = pl.ds(pl.multiple_of(c_ * half, STRIP), half)
        theirs = pl.ds(pl.multiple_of((1 - c_) * half, STRIP), half)

        def to_sibling(slot):
            return pltpu.make_async_remote_copy(
                src_ref=stage.at[slot, theirs], dst_ref=pair.at[slot], send_sem=pair_send_sems.at[slot],
                recv_sem=pair_recv_sems.at[slot], device_id=(x_, y_, 1 - c_), device_id_type=MESH)

        def to_peer(j):
            return pltpu.make_async_remote_copy(
                src_ref=summed.at[j + 1], dst_ref=parts_ref.at[j + 1, mine], send_sem=send_sems.at[j],
                recv_sem=recv_sems.at[j], device_id=(*chips[j], c_), device_id_type=MESH)

        keep = pltpu.make_async_copy(summed.at[0], parts_ref.at[0, mine], own_sem)

        def blank(slot):
            return pltpu.make_async_copy(zeros, parts_ref.at[slot, theirs], zero_sems.at[slot])

        @pl.when((g == 0) & (t == 0))
        def _():
            zeros[...] = jnp.zeros_like(zeros)
            for slot in range(nq):
                blank(slot).start()

        @pl.when(t == 0)
        def _():
            acc[...] = jnp.zeros_like(acc)

        acc[...] += _dot_tn(l_ref[...], r_ref[...])

        for step in range(nq):
            slot = (step + 1) % nq

            @pl.when((g == step) & (t == n_t - 1))
            def _():
                stage[slot] = acc[...].astype(BF16)
                to_sibling(slot).start()
                to_sibling(slot).wait_recv()
                summed[slot] = (stage[slot, mine, :].astype(F32) + pair[slot].astype(F32)).astype(BF16)
                if slot > 0:
                    to_peer(slot - 1).start()
                else:
                    keep.start()

        @pl.when((g == nq - 1) & (t == n_t - 1))
        def _():
            for j in range(N_CHIPS - 1):
                to_peer(j).wait()
            keep.wait()
            for slot in range(nq):
                to_sibling(slot).wait_send()
                blank(slot).wait()

    (parts,), cargo_outs = _call(
        body, name=name, grid=(nq, n_t),
        in_specs=[pl.BlockSpec((None, tt, fq), lambda g, t: ((g + 1) % nq, t, 0)),
                  pl.BlockSpec((tt, d), lambda g, t: (t, 0))],
        out_specs=[HBM], out_shape=[_sds((nq, fq, d), BF16)],
        scratch_shapes=[pltpu.VMEM((fq, d), F32), pltpu.VMEM((nq, fq, d), BF16), pltpu.VMEM((nq, half, d), BF16),
                        pltpu.VMEM((nq, half, d), BF16), pltpu.VMEM((half, d), BF16),
                        pltpu.SemaphoreType.DMA((nq,)), pltpu.SemaphoreType.DMA((nq,)),
                        pltpu.SemaphoreType.DMA((N_CHIPS - 1,)), pltpu.SemaphoreType.DMA((N_CHIPS - 1,)),
                        pltpu.SemaphoreType.DMA(()), pltpu.SemaphoreType.DMA((nq,))],
        args=[hid, tok], cargos=cargos)
    return parts, cargo_outs


def _wgrad_2d(lhs, rhs, n_col_blocks, out_dtype, name, group_diag=False, cargos=()):
    t_len, k = lhs.shape
    n = rhs.shape[1]
    nb = n // n_col_blocks
    kb = k // n_col_blocks if group_diag else k
    tt = min(TT_WGRAD, t_len)
    l_map = (lambda q, t: (t, q)) if group_diag else (lambda q, t: (t, 0))
    return _wgrad(lhs, rhs,
                  pl.BlockSpec((tt, kb), l_map),
                  pl.BlockSpec((tt, nb), lambda q, t: (t, q)),
                  _sds((n_col_blocks, kb, nb), out_dtype),
                  pl.BlockSpec((None, kb, nb), lambda q, t: (q, 0, 0)),
                  (kb, nb), (n_col_blocks, t_len // tt), name, cargos)


def _layernorm_stats(u1):
    mu = jnp.mean(u1, axis=-1, keepdims=True)
    xc = u1 - mu
    rstd = lax.rsqrt(jnp.mean(xc * xc, axis=-1, keepdims=True) + LN_EPS)
    return rstd, xc * rstd


def _positions(i, tm, rows, offset=0):
    return (lax.broadcasted_iota(jnp.int32, (rows, 1), 0) + (i * tm + offset)).astype(F32)


SHIFT_ROWS = HALO - SUBLANES


def _fill_shifted(ext_s, sh_s, tm):
    for b in range(1, SUBLANES):
        sh_s[b - 1] = ext_s[pl.ds(b, tm + SHIFT_ROWS), :]


def _window(ext_s, sh_s, shift, tm):
    a, b = divmod(shift, SUBLANES)
    if b == 0:
        return ext_s[pl.ds(shift, tm), :]
    return sh_s[b - 1, pl.ds(a * SUBLANES, tm), :]


def _tile(tm, cols):
    return pl.BlockSpec((tm, cols), lambda i: (i, 0))


def _whole(shape):
    return pl.BlockSpec(shape, lambda i: (0,) * len(shape))


def _mix_fwd(x1, gain, w_in, conv_dw, conv_b, ln_g, ln_b, conv_pw, pool_w, pool_scale, w_out, name, cargos=()):
    t_len, d = x1.shape
    nq, _, nb = w_in.shape
    tm = min(TM_MIX, t_len)

    def body(x_ref, g_ref, wi_ref, dw_ref, cb_ref, lg_ref, lb_ref, pw_ref, plw_ref, ps_ref, wo_ref,
             x2_ref, h_ref, p_ref, u1_ref, u3_ref, mx_ref, cat_ref, ext_s, pext_s, sh_s, tail_s):
        i = pl.program_id(0)

        @pl.when(i == 0)
        def _():
            tail_s[...] = jnp.zeros_like(tail_s)

        _, n = _rms_stats(x_ref[...])
        h = (n * g_ref[...]).astype(BF16)
        h_ref[...] = h
        for q in range(nq):
            p_ref[:, q * nb:(q + 1) * nb] = _dot(h, wi_ref[q])

        a = p_ref[:, 0:D_CONV]
        g = p_ref[:, D_CONV:2 * D_CONV]
        p = p_ref[:, 2 * D_CONV:]
        ext_s[0:HALO, :] = tail_s[:, 0:D_CONV] * jax.nn.sigmoid(tail_s[:, D_CONV:2 * D_CONV])
        ext_s[HALO:, :] = a * jax.nn.sigmoid(g)
        pext_s[0:HALO, :] = tail_s[:, 2 * D_CONV:]
        pext_s[HALO:, :] = p
        tail_s[...] = p_ref[tm - HALO:tm, :]

        _fill_shifted(ext_s, sh_s, tm)
        u1 = jnp.broadcast_to(cb_ref[...], (tm, D_CONV))
        for k in range(CONV_WIDTH):
            u1 = u1 + dw_ref[k:k + 1, :] * _window(ext_s, sh_s, HALO - (CONV_WIDTH - 1) + k, tm)
        u1_ref[...] = u1
        _, nhat = _layernorm_stats(u1)
        u2 = nhat * lg_ref[...] + lb_ref[...]
        u3 = (u2 * jax.nn.sigmoid(u2)).astype(BF16)
        u3_ref[...] = u3
        cat_ref[:, 0:D_CONV] = _dot(u3, pw_ref[...]).astype(BF16)

        pos1 = _positions(i, tm, tm) + 1.0
        for gi, w in enumerate(POOL_WINDOWS):
            cols = slice(gi * POOL_GROUP, (gi + 1) * POOL_GROUP)
            s = pext_s[pl.ds(HALO, tm), cols]
            for j in range(1, w):
                s = s + pext_s[pl.ds(HALO - j, tm), cols]
            mixed = (s / jnp.minimum(pos1, float(w)) - p[:, cols]).astype(BF16)
            mx_ref[:, cols] = mixed
            out = _dot(mixed, plw_ref[gi]) * ps_ref[:, cols]
            cat_ref[:, D_CONV + gi * POOL_GROUP:D_CONV + (gi + 1) * POOL_GROUP] = out.astype(BF16)

        x2_ref[...] = x_ref[...] + _dot(cat_ref[...], wo_ref[...])

    return _call(
        body, name=name, grid=(t_len // tm,),
        in_specs=[_tile(tm, d), _whole((1, d)), _whole((nq, d, nb)), _whole((CONV_WIDTH + 1, D_CONV)),
                  _whole((1, D_CONV)), _whole((1, D_CONV)), _whole((1, D_CONV)), _whole((D_CONV, D_CONV)),
                  _whole((4, POOL_GROUP, POOL_GROUP)), _whole((1, D_POOL)), _whole((D_CONV + D_POOL, d))],
        out_specs=[_tile(tm, d), _tile(tm, d), _tile(tm, D_IN), _tile(tm, D_CONV), _tile(tm, D_CONV),
                   _tile(tm, D_POOL), _tile(tm, D_CONV + D_POOL)],
        out_shape=[_sds((t_len, d), F32), _sds((t_len, d), BF16), _sds((t_len, D_IN), F32),
                   _sds((t_len, D_CONV), F32), _sds((t_len, D_CONV), BF16), _sds((t_len, D_POOL), BF16),
                   _sds((t_len, D_CONV + D_POOL), BF16)],
        scratch_shapes=[pltpu.VMEM((tm + HALO, D_CONV), F32), pltpu.VMEM((tm + HALO, D_POOL), F32),
                        pltpu.VMEM((SUBLANES - 1, tm + SHIFT_ROWS, D_CONV), F32), pltpu.VMEM((HALO, D_IN), F32)],
        args=[x1, gain, w_in, conv_dw, conv_b, ln_g, ln_b, conv_pw, pool_w, pool_scale, w_out], cargos=cargos)


def _mix_bwd_local(dx2, u1, mixed, ln_g, ln_b, conv_pw, pool_w, pool_scale, w_out, name, cargos=()):
    t_len, d = dx2.shape
    tm = min(TM_MIX, t_len)

    def body(dx_ref, u1_ref, mx_ref, lg_ref, lb_ref, pw_ref, plw_ref, ps_ref, wo_ref,
             du1_ref, dmx_ref, dco_ref, dpo_ref, dlg_ref, dlb_ref, dps_ref):
        @pl.when(pl.program_id(0) == 0)
        def _():
            dlg_ref[...] = jnp.zeros_like(dlg_ref)
            dlb_ref[...] = jnp.zeros_like(dlb_ref)
            dps_ref[...] = jnp.zeros_like(dps_ref)

        dcat = _dot_nt(dx_ref[...].astype(BF16), wo_ref[...])
        dco = dcat[:, 0:D_CONV].astype(BF16)
        dco_ref[...] = dco
        du3 = _dot_nt(dco, pw_ref[...])
        rstd, nhat = _layernorm_stats(u1_ref[...])
        u2 = nhat * lg_ref[...] + lb_ref[...]
        sig = jax.nn.sigmoid(u2)
        du2 = du3 * (sig * (1.0 + u2 * (1.0 - sig)))
        dlg_ref[...] += jnp.sum(du2 * nhat, axis=0, keepdims=True)
        dlb_ref[...] += jnp.sum(du2, axis=0, keepdims=True)
        dnhat = du2 * lg_ref[...]
        du1_ref[...] = rstd * (dnhat - jnp.mean(dnhat, axis=-1, keepdims=True)
                               - nhat * jnp.mean(dnhat * nhat, axis=-1, keepdims=True))

        for gi in range(len(POOL_WINDOWS)):
            cols = slice(gi * POOL_GROUP, (gi + 1) * POOL_GROUP)
            dpo = dcat[:, D_CONV + gi * POOL_GROUP:D_CONV + (gi + 1) * POOL_GROUP]
            pre = _dot(mx_ref[:, cols], plw_ref[gi])
            dps_ref[:, cols] += jnp.sum(dpo * pre, axis=0, keepdims=True)
            dout = (dpo * ps_ref[:, cols]).astype(BF16)
            dpo_ref[:, cols] = dout
            dmx_ref[:, cols] = _dot_nt(dout, plw_ref[gi])

    vec = _whole((1, D_CONV))
    return _call(
        body, name=name, grid=(t_len // tm,),
        in_specs=[_tile(tm, d), _tile(tm, D_CONV), _tile(tm, D_POOL), vec, vec, _whole((D_CONV, D_CONV)),
                  _whole((4, POOL_GROUP, POOL_GROUP)), vec, _whole((D_CONV + D_POOL, d))],
        out_specs=[_tile(tm, D_CONV), _tile(tm, D_POOL), _tile(tm, D_CONV), _tile(tm, D_POOL), vec, vec, vec],
        out_shape=[_sds((t_len, D_CONV), F32), _sds((t_len, D_POOL), F32), _sds((t_len, D_CONV), BF16),
                   _sds((t_len, D_POOL), BF16), _sds((1, D_CONV), F32), _sds((1, D_CONV), F32),
                   _sds((1, D_POOL), F32)],
        args=[dx2, u1, mixed, ln_g, ln_b, conv_pw, pool_w, pool_scale, w_out], cargos=cargos)


def _mix_bwd_seq(du1, dmixed, proj, x1, dx2, gain, conv_dw, w_in, name, cargos=()):
    t_len, d = x1.shape
    nq, _, nb = w_in.shape
    tm = min(TM_MIX, t_len)
    hb = tm // HALO
    last_block = t_len // HALO - 1
    n_tiles = t_len // tm

    def body(du_ref, dun_ref, dm_ref, dmn_ref, p_ref, tail_ref, x_ref, dx2_ref, g_ref, dw_ref, wi_ref,
             dx1_ref, dp_ref, ddw_ref, dcb_ref, dgain_ref, uext_s, dext_s, mext_s, ush_s, dsh_s):
        i = pl.program_id(0)
        first = i == 0
        last = i == n_tiles - 1

        @pl.when(first)
        def _():
            ddw_ref[...] = jnp.zeros_like(ddw_ref)
            dcb_ref[...] = jnp.zeros_like(dcb_ref)
            dgain_ref[...] = jnp.zeros_like(dgain_ref)

        a = p_ref[:, 0:D_CONV]
        g = p_ref[:, D_CONV:2 * D_CONV]
        sg = jax.nn.sigmoid(g)
        ta = tail_ref[:, 0:D_CONV]
        tg = tail_ref[:, D_CONV:2 * D_CONV]
        uext_s[0:HALO, :] = jnp.where(first, 0.0, ta * jax.nn.sigmoid(tg))
        uext_s[HALO:, :] = a * sg
        du1 = du_ref[...]
        dext_s[0:tm, :] = du1
        dext_s[tm:, :] = jnp.where(last, 0.0, dun_ref[...])

        _fill_shifted(uext_s, ush_s, tm)
        _fill_shifted(dext_s, dsh_s, tm)
        du0 = jnp.zeros((tm, D_CONV), F32)
        for k in range(CONV_WIDTH):
            du0 = du0 + dw_ref[k:k + 1, :] * _window(dext_s, dsh_s, CONV_WIDTH - 1 - k, tm)
            ddw_ref[k:k + 1, :] += jnp.sum(
                du1 * _window(uext_s, ush_s, HALO - (CONV_WIDTH - 1) + k, tm), axis=0, keepdims=True)
        dcb_ref[...] += jnp.sum(du1, axis=0, keepdims=True)
        dp_ref[:, 0:D_CONV] = (du0 * sg).astype(BF16)
        dp_ref[:, D_CONV:2 * D_CONV] = (du0 * a * sg * (1.0 - sg)).astype(BF16)

        pos1 = _positions(i, tm, tm) + 1.0
        pos1_next = _positions(i, tm, HALO, offset=tm) + 1.0
        for gi, w in enumerate(POOL_WINDOWS):
            cols = slice(gi * POOL_GROUP, (gi + 1) * POOL_GROUP)
            dm = dm_ref[:, cols]
            mext_s[0:tm, cols] = dm / jnp.minimum(pos1, float(w))
            mext_s[tm:, cols] = jnp.where(last, 0.0, dmn_ref[:, cols] / jnp.minimum(pos1_next, float(w)))
            s = mext_s[pl.ds(0, tm), cols]
            for j in range(1, w):
                s = s + mext_s[pl.ds(j, tm), cols]
            dp_ref[:, 2 * D_CONV + gi * POOL_GROUP:2 * D_CONV + (gi + 1) * POOL_GROUP] = (s - dm).astype(BF16)

        dh = _dot_nt(dp_ref[:, 0:nb], wi_ref[0])
        for q in range(1, nq):
            dh = dh + _dot_nt(dp_ref[:, q * nb:(q + 1) * nb], wi_ref[q])
        r, n = _rms_stats(x_ref[...])
        dgain_ref[...] += jnp.sum(dh * n, axis=0, keepdims=True)
        dx1_ref[...] = dx2_ref[...] + _rms_bwd(dh, n, r, g_ref[...])

    def nxt(cols):
        return pl.BlockSpec((HALO, cols), lambda i: (jnp.minimum((i + 1) * hb, last_block), 0))

    return _call(
        body, name=name, grid=(n_tiles,),
        in_specs=[_tile(tm, D_CONV), nxt(D_CONV), _tile(tm, D_POOL), nxt(D_POOL), _tile(tm, D_IN),
                  pl.BlockSpec((HALO, D_IN), lambda i: (jnp.maximum(i * hb - 1, 0), 0)),
                  _tile(tm, d), _tile(tm, d), _whole((1, d)), _whole((CONV_WIDTH + 1, D_CONV)),
                  _whole((nq, d, nb))],
        out_specs=[_tile(tm, d), _tile(tm, D_IN), _whole((CONV_WIDTH + 1, D_CONV)), _whole((1, D_CONV)),
                   _whole((1, d))],
        out_shape=[_sds((t_len, d), F32), _sds((t_len, D_IN), BF16), _sds((CONV_WIDTH + 1, D_CONV), F32),
                   _sds((1, D_CONV), F32), _sds((1, d), F32)],
        scratch_shapes=[pltpu.VMEM((tm + HALO, D_CONV), F32), pltpu.VMEM((tm + HALO, D_CONV), F32),
                        pltpu.VMEM((tm + HALO, D_POOL), F32),
                        pltpu.VMEM((SUBLANES - 1, tm + SHIFT_ROWS, D_CONV), F32),
                        pltpu.VMEM((SUBLANES - 1, tm + SHIFT_ROWS, D_CONV), F32)],
        args=[du1, du1, dmixed, dmixed, proj, proj, x1, dx2, gain, conv_dw, w_in], cargos=cargos)


def _final_norm_loss(x3, target, gain, name):
    t_len, d = x3.shape
    tm = min(TM_FFN, t_len)

    def body(x_ref, t_ref, g_ref, dx_ref, loss_ref, dgain_ref):
        @pl.when(pl.program_id(0) == 0)
        def _():
            loss_ref[...] = jnp.zeros_like(loss_ref)
            dgain_ref[...] = jnp.zeros_like(dgain_ref)

        r, n = _rms_stats(x_ref[...])
        err = n * g_ref[...] - t_ref[...]
        per_tok = jnp.sum(err * err, axis=-1, keepdims=True) * (1.0 / d)
        loss_ref[...] += 0.5 * jnp.sum(per_tok, axis=0, keepdims=True)
        dy = err * (1.0 / d)
        dgain_ref[...] += jnp.sum(dy * n, axis=0, keepdims=True)
        dx_ref[...] = _rms_bwd(dy, n, r, g_ref[...])

    tok = pl.BlockSpec((tm, d), lambda i: (i, 0))
    outs, _ = _call(
        body, name=name, grid=(t_len // tm,),
        in_specs=[tok, tok, pl.BlockSpec((1, d), lambda i: (0, 0))],
        out_specs=[tok, pl.BlockSpec((1, 128), lambda i: (0, 0)), pl.BlockSpec((1, d), lambda i: (0, 0))],
        out_shape=[_sds((t_len, d), F32), _sds((1, 128), F32), _sds((1, d), F32)],
        args=[x3, target, gain])
    return outs


def _row_tile(rows):
    return rows // 4 if rows % 64 == 0 else rows


def _sum_parts(parts, name):
    n, r, c = parts.shape
    tr = _row_tile(r)

    def body(p_ref, o_ref):
        s = p_ref[0].astype(F32)
        for k in range(1, n):
            s = s + p_ref[k].astype(F32)
        o_ref[...] = s

    (out,), _ = _call(body, name=name, grid=(r // tr,),
                      in_specs=[pl.BlockSpec((n, tr, c), lambda i: (0, i, 0))],
                      out_specs=[pl.BlockSpec((tr, c), lambda i: (i, 0))], out_shape=[_sds((r, c), F32)],
                      args=[parts])
    return out


def _adamw_math(w, g, m, v):
    m = ADAM_B1 * m + (1.0 - ADAM_B1) * g
    v = ADAM_B2 * v + (1.0 - ADAM_B2) * (g * g)
    m_hat = m / (1.0 - ADAM_B1 ** ADAM_STEP)
    v_hat = v / (1.0 - ADAM_B2 ** ADAM_STEP)
    delta = -ADAM_LR * (m_hat / (jnp.sqrt(v_hat) + ADAM_EPS) + ADAM_WD * w)
    return delta, m, v


def _adamw(parts, w, m, v, name):
    r, c = w.shape
    n = len(parts)
    tr = _row_tile(r)

    def body(*refs):
        terms = []
        for p_ref in refs[:n]:
            terms += [p_ref[...]] if len(p_ref.shape) == 2 else [p_ref[k] for k in range(p_ref.shape[0])]
        w_ref, m_ref, v_ref, g_out, d_out, m_out, v_out = refs[n:]
        g = terms[0]
        for t in terms[1:]:
            g = g + t
        delta, nm, nv = _adamw_math(w_ref[...], g, m_ref[...], v_ref[...])
        g_out[...] = g
        d_out[...] = delta
        m_out[...] = nm
        v_out[...] = nv

    blk = pl.BlockSpec((tr, c), lambda i: (i, 0))
    p_specs = [blk if p.ndim == 2 else pl.BlockSpec((p.shape[0], tr, c), lambda i: (0, i, 0)) for p in parts]
    outs, _ = _call(body, name=name, grid=(r // tr,), in_specs=p_specs + [blk, blk, blk],
                    out_specs=[blk] * 4, out_shape=[_sds((r, c), F32)] * 4, args=[*parts, w, m, v])
    return outs


FFN_W = ("w_gate", "w_up", "w_down")
MID = ("w_in", "conv_dw", "conv_pw", "w_out")
SMALL_1024 = ("ffn1_norm", "mix_norm", "ffn2_norm", "final_norm")
SMALL_512 = ("conv_dw_b", "conv_ln_g", "conv_ln_b", "pool_scale")
WEIGHTS = ("ffn1_norm", "ffn1_w_gate", "ffn1_w_up", "ffn1_w_down", "mix_norm", "w_in", "conv_dw", "conv_dw_b",
           "conv_ln_g", "conv_ln_b", "conv_pw", "pool_w", "pool_scale", "w_out", "ffn2_norm", "ffn2_w_gate",
           "ffn2_w_up", "ffn2_w_down", "final_norm")
PACK_ROWS = 72


def _pad_rows(a, rows):
    return jnp.pad(a, ((0, rows - a.shape[0]), (0, 0)))


def _pack_small(t):
    rows = [t[k].reshape(1, D_MODEL) for k in SMALL_1024]
    rows.append(jnp.concatenate([t["conv_dw_b"].reshape(1, -1), t["conv_ln_g"].reshape(1, -1)], axis=1))
    rows.append(jnp.concatenate([t["conv_ln_b"].reshape(1, -1), t["pool_scale"].reshape(1, -1)], axis=1))
    rows.append(t["pool_w"].reshape(64, D_MODEL))
    return _pad_rows(jnp.concatenate(rows, axis=0), PACK_ROWS)


def _unpack_small(p):
    out = {k: p[i] for i, k in enumerate(SMALL_1024)}
    out["conv_dw_b"], out["conv_ln_g"] = p[4, :D_CONV], p[4, D_CONV:]
    out["conv_ln_b"], out["pool_scale"] = p[5, :D_CONV], p[5, D_CONV:]
    out["pool_w"] = p[6:70].reshape(4, POOL_GROUP, POOL_GROUP)
    return out


def _as_stored(name, a):
    if name.endswith(("w_gate", "w_up")):
        return a.T
    if name == "conv_dw":
        return _pad_rows(a, CONV_WIDTH + 1)
    return a


def _as_given(name, a):
    if name.endswith(("w_gate", "w_up")):
        return a.T
    if name == "conv_dw":
        return a[:CONV_WIDTH]
    return a


def kernel(x, ffn1_norm, ffn1_w_gate, ffn1_w_up, ffn1_w_down, mix_norm, w_in, conv_dw, conv_dw_b, conv_ln_g, conv_ln_b, conv_pw, pool_w, pool_scale, w_out, ffn2_norm, ffn2_w_gate, ffn2_w_up, ffn2_w_down, final_norm, loss_target, m_ffn1_norm, m_ffn1_w_gate, m_ffn1_w_up, m_ffn1_w_down, m_mix_norm, m_w_in, m_conv_dw, m_conv_dw_b, m_conv_ln_g, m_conv_ln_b, m_conv_pw, m_pool_w, m_pool_scale, m_w_out, m_ffn2_norm, m_ffn2_w_gate, m_ffn2_w_up, m_ffn2_w_down, m_final_norm, v_ffn1_norm, v_ffn1_w_gate, v_ffn1_w_up, v_ffn1_w_down, v_mix_norm, v_w_in, v_conv_dw, v_conv_dw_b, v_conv_ln_g, v_conv_ln_b, v_conv_pw, v_pool_w, v_pool_scale, v_w_out, v_ffn2_norm, v_ffn2_w_gate, v_ffn2_w_up, v_ffn2_w_down, v_final_norm):
    given = dict(locals())
    wts = {k: given[k] for k in WEIGHTS}
    mom_m = {k: given["m_" + k] for k in WEIGHTS}
    mom_v = {k: given["v_" + k] for k in WEIGHTS}
    xt, target = x[0], loss_target[0]

    shard = {k: _as_stored(k, wts[k]) if k == "conv_dw" else _as_stored(k, wts[k]).astype(BF16)
             for k in WEIGHTS if k.endswith(FFN_W) or k in MID}
    w = {k: wts[k].reshape(1, -1) for k in SMALL_1024 + SMALL_512}
    w["pool_w"] = wts["pool_w"].astype(BF16)

    (h1, s1, p1, a1, w["ffn1_w_gate"], w["ffn1_w_up"]), ((w["ffn1_w_down"],),) = _ffn_up_gather(
        xt, w["ffn1_norm"], shard["ffn1_w_gate"], shard["ffn1_w_up"], "ffn1_up_gather",
        cargos=[Cargo("gather_slots", [shard["ffn1_w_down"]])])
    x1, (mid, (w["ffn2_w_down"],)) = _ffn_down(
        xt, a1, w["ffn1_w_down"], "ffn1_down",
        cargos=[Cargo("gather_chips", [shard[k] for k in MID]), Cargo("gather_slots", [shard["ffn2_w_down"]])])
    w["w_in"] = mid[0]
    w["conv_dw"] = mid[1].transpose(1, 0, 2).reshape(CONV_WIDTH + 1, D_CONV)
    w["conv_pw"] = mid[2].reshape(D_CONV, D_CONV)
    w["w_out"] = mid[3].reshape(D_CONV + D_POOL, D_MODEL)
    (x2, h2, proj, u1, u3, mixed, cat), ((w["ffn2_w_gate"], w["ffn2_w_up"]),) = _mix_fwd(
        x1, w["mix_norm"], w["w_in"], w["conv_dw"], w["conv_dw_b"], w["conv_ln_g"], w["conv_ln_b"], w["conv_pw"],
        w["pool_w"], w["pool_scale"], w["w_out"], "mix_fwd",
        cargos=[Cargo("gather_slots", [shard["ffn2_w_gate"], shard["ffn2_w_up"]])])
    x3, h3, s2, p2, a2 = _ffn_fwd(x2, w["ffn2_norm"], w["ffn2_w_gate"], w["ffn2_w_up"], w["ffn2_w_down"], "ffn2_fwd")
    dx3, loss, d_final = _final_norm_loss(x3, target, w["final_norm"], "final_norm_loss")
    loss = lax.psum(loss[0, 0], ("x", "y", "c"))

    g = {"final_norm": d_final}
    sums = {}

    def landed(names, parts):
        for k, p in zip(names, parts):
            sums[k] = _sum_parts(p, "sum_chips_" + k)

    dx2, g["ffn2_norm"], df2, dg2, du2 = _ffn_bwd(dx3, x2, w["ffn2_norm"], s2, p2, w["ffn2_w_gate"],
                                                   w["ffn2_w_up"], w["ffn2_w_down"], "ffn2_bwd")
    def ffn_wgrad(name, hid, tok, cargos=()):
        parts, cargo_outs = _wgrad_hid_tok_scatter(hid, tok, name.replace("_w_", "_dw_"), cargos=cargos)
        landed([name], [parts])
        return cargo_outs

    ffn_wgrad("ffn2_w_gate", dg2, h3)
    ffn_wgrad("ffn2_w_up", du2, h3)
    ffn_wgrad("ffn2_w_down", a2, df2)
    (du1, dmixed, dco, dpo, g["conv_ln_g"], g["conv_ln_b"], g["pool_scale"]), (swapped2,) = _mix_bwd_local(
        dx2, u1, mixed, w["conv_ln_g"], w["conv_ln_b"], w["conv_pw"], w["pool_w"], w["pool_scale"], w["w_out"],
        "mix_bwd_local", cargos=[Cargo("swap", [sums["ffn2_" + k] for k in FFN_W])])
    g_out, _ = _wgrad_2d(cat, dx2, 1, BF16, "dw_out")
    g_pw, _ = _wgrad_2d(u3, dco, 1, BF16, "dconv_pw")
    g["pool_w"], _ = _wgrad_2d(mixed, dpo, 4, F32, "dpool_w", group_diag=True)
    slabs = [g_pw.reshape(N_CHIPS, D_CONV // N_CHIPS, D_CONV),
             g_out.reshape(N_CHIPS, (D_CONV + D_POOL) // N_CHIPS, D_MODEL)]
    (dx1, dproj, g_dw, g["conv_dw_b"], g["mix_norm"]), (parts,) = _mix_bwd_seq(
        du1, dmixed, proj, x1, dx2, w["mix_norm"], w["conv_dw"], w["w_in"], "mix_bwd_seq",
        cargos=[Cargo("scatter_chips", slabs)])
    landed(["conv_pw", "w_out"], parts)
    g_in, _ = _wgrad_2d(h2, dproj, N_CHIPS, BF16, "dw_in")
    dx, g["ffn1_norm"], df1, dg1, du1_ = _ffn_bwd(dx1, xt, w["ffn1_norm"], s1, p1, w["ffn1_w_gate"],
                                                   w["ffn1_w_up"], w["ffn1_w_down"], "ffn1_bwd")
    slabs = [g_in, g_dw.reshape(CONV_WIDTH + 1, N_CHIPS, D_CONV // N_CHIPS).transpose(1, 0, 2)]
    (parts,) = ffn_wgrad("ffn1_w_gate", dg1, h1, cargos=[Cargo("scatter_chips", slabs)])
    landed(["w_in", "conv_dw"], parts)
    swapped_mid, swapped_gate, small_parts = ffn_wgrad(
        "ffn1_w_up", du1_, h1,
        cargos=[Cargo("swap", [sums[k] for k in MID]), Cargo("swap", [sums["ffn1_w_gate"]]),
                Cargo("gather_devices", [_pack_small(g)])])
    (swapped_up,) = ffn_wgrad("ffn1_w_down", a1, df1, cargos=[Cargo("swap", [sums["ffn1_w_up"]])])
    swapped_down = _exchange(Cargo("swap", [sums["ffn1_w_down"]]), "swap_last")

    theirs = dict(zip(["ffn2_" + k for k in FFN_W], swapped2))
    theirs.update(zip(MID, swapped_mid))
    theirs.update(ffn1_w_gate=swapped_gate[0], ffn1_w_up=swapped_up[0], ffn1_w_down=swapped_down[0])
    grads, deltas, new_m, new_v = {}, {}, {}, {}
    for k in theirs:
        res = _adamw([sums[k], theirs[k]], _as_stored(k, wts[k]), _as_stored(k, mom_m[k]),
                     _as_stored(k, mom_v[k]), "adamw_" + k)
        grads[k], deltas[k], new_m[k], new_v[k] = [_as_given(k, t) for t in res]
    res = _adamw(small_parts, _pack_small(wts), _pack_small(mom_m), _pack_small(mom_v), "adamw_small")
    for dst, packed in zip((grads, deltas, new_m, new_v), res):
        dst.update(_unpack_small(packed))

    out = [loss, dx[None]]
    for group in (grads, deltas, new_m, new_v):
        out += [group[k] for k in WEIGHTS]
    return tuple(out)
```

```python
import functools

import jax
import jax.numpy as jnp
from jax import lax
from jax.experimental import pallas as pl
from jax.experimental.pallas import tpu as pltpu

F32 = jnp.float32
BF16 = jnp.bfloat16
MESH = pl.DeviceIdType.MESH

N_CHIPS = 4
N_DEV = 8
D_MODEL = 1024
D_CONV = 512
D_POOL = 512
CONV_WIDTH = 31
POOL_WINDOWS = (2, 4, 8, 16)
POOL_GROUP = 128
D_IN = 2 * D_CONV + D_POOL
HALO = 32
RMS_EPS = 1e-6
LN_EPS = 1e-5
FFN_RES_WEIGHT = 0.5
ADAM_LR = 0.001
ADAM_B1 = 0.9
ADAM_B2 = 0.999
ADAM_EPS = 1e-08
ADAM_WD = 0.01
ADAM_STEP = 10
VMEM_LIMIT_BYTES = 52 * 1024 * 1024
TM_FFN = 512
TM_MIX = 256
TT_WGRAD = 2048
STRIP = 16
SLOTS_PER_STEP = 2
SUBLANES = 8
RELAY_AT_EIGHTHS = 5

HBM = pl.BlockSpec(memory_space=pl.ANY)


def _dot(a, b):
    return jnp.dot(a, b, preferred_element_type=F32)


def _dot_nt(a, b):
    return lax.dot_general(a, b, (((1,), (1,)), ((), ())), preferred_element_type=F32)


def _dot_tn(a, b):
    return lax.dot_general(a, b, (((0,), (0,)), ((), ())), preferred_element_type=F32)


def _sds(shape, dtype):
    return jax.ShapeDtypeStruct(shape, dtype)


def _rms_stats(xv):
    r = lax.rsqrt(jnp.mean(xv * xv, axis=-1, keepdims=True) + RMS_EPS)
    return r, xv * r


def _swiglu_saved(gate, up):
    sig = jax.nn.sigmoid(gate)
    silu = gate * sig
    return silu, up * (sig * (1.0 + gate * (1.0 - sig))), silu * up


def _rms_bwd(dh, n, r, gain):
    dn = dh * gain
    return r * (dn - n * jnp.mean(dn * n, axis=-1, keepdims=True))


def _place():
    x, y, c = lax.axis_index("x"), lax.axis_index("y"), lax.axis_index("c")
    return x, y, c, [(1 - x, y), (x, 1 - y), (1 - x, 1 - y)]


class Cargo:
    def __init__(self, kind, arrays):
        self.kind, self.arrays = kind, list(arrays)
        n = len(self.arrays)
        self.two_level = kind in ("gather_slots", "gather_chips")
        if self.two_level:
            self.out_shape = [_sds((N_CHIPS,) + a.shape, a.dtype) for a in self.arrays]
        elif kind == "gather_devices":
            self.out_shape = [_sds((N_DEV,) + a.shape, a.dtype) for a in self.arrays]
        else:
            self.out_shape = [_sds(a.shape, a.dtype) for a in self.arrays]
        n_remote = n * {"swap": 1, "gather_devices": N_DEV - 1}.get(kind, N_CHIPS - 1)
        n_own = 0 if kind == "swap" else n
        n_relay = n_remote if self.two_level else 0
        dma = pltpu.SemaphoreType.DMA
        self.scratch = [dma((n_remote,)), dma((n_remote,)), dma((max(n_own, 1),)),
                        dma((max(n_relay, 1),)), dma((max(n_relay, 1),))]

    def _plan(self, ins, outs):
        x, y, c, chips = _place()
        q = 2 * x + y
        sibling = (x, y, 1 - c)
        own, remote, relays = [], [], []
        for a, o in zip(ins, outs):
            if self.two_level:
                half = a.shape[0] // 2
                mine = pl.ds(pl.multiple_of(c * half, SUBLANES), half)
                theirs = pl.ds(pl.multiple_of((1 - c) * half, SUBLANES), half)
                own.append((a, o.at[0 if self.kind == "gather_slots" else q]))
                for j, (px, py) in enumerate(chips):
                    there, here = (j + 1, j + 1) if self.kind == "gather_slots" else (q, 2 * px + py)
                    remote.append((a.at[mine], o.at[there, mine], o.at[here, mine], (px, py, c)))
                    relays.append((o.at[here, mine], o.at[here, mine], o.at[here, theirs], sibling))
            elif self.kind == "scatter_chips":
                own.append((a.at[q], o.at[q]))
                remote += [(a.at[2 * px + py], o.at[q], o.at[2 * px + py], (px, py, c)) for px, py in chips]
            elif self.kind == "swap":
                remote.append((a, o, o, sibling))
            else:
                own.append((a, o.at[4 * x + 2 * y + c]))
                for k in range(1, N_DEV):
                    px, py, pc = x ^ (k >> 2 & 1), y ^ (k >> 1 & 1), c ^ (k & 1)
                    remote.append((a, o.at[4 * x + 2 * y + c], o.at[4 * px + 2 * py + pc], (px, py, pc)))
        return own, remote, relays

    @staticmethod
    def _copies(entries, send_sems, recv_sems):
        out = []
        for k, (src, dst, landed, peer) in enumerate(entries):
            def make(dst_ref, k=k, src=src, peer=peer):
                return pltpu.make_async_remote_copy(src_ref=src, dst_ref=dst_ref, send_sem=send_sems.at[k],
                                                    recv_sem=recv_sems.at[k], device_id=peer, device_id_type=MESH)
            out.append((make(dst), make(landed)))
        return out

    def start(self, ins, outs, sems):
        own, remote, _ = self._plan(ins, outs)
        for k, (src, dst) in enumerate(own):
            pltpu.make_async_copy(src, dst, sems[2].at[k]).start()
        for mine, _ in self._copies(remote, sems[0], sems[1]):
            mine.start()

    def relay(self, ins, outs, sems):
        _, remote, relays = self._plan(ins, outs)
        passed = self._copies(relays, sems[3], sems[4])
        for (_, arriving), (mine, _) in zip(self._copies(remote, sems[0], sems[1]), passed):
            arriving.wait_recv()
            mine.start()

    def wait(self, ins, outs, sems):
        own, remote, relays = self._plan(ins, outs)
        for mine, arriving in self._copies(remote, sems[0], sems[1]):
            mine.wait_send()
            if not self.two_level:
                arriving.wait_recv()
        for mine, arriving in self._copies(relays, sems[3], sems[4]):
            mine.wait_send()
            arriving.wait_recv()
        for k, (src, dst) in enumerate(own):
            pltpu.make_async_copy(src, dst, sems[2].at[k]).wait()


N_CARGO_SEMS = 5


def _call(body, *, name, grid, in_specs, out_specs, out_shape, args, scratch_shapes=(), cargos=()):
    n_in, n_out, n_scr = len(in_specs), len(out_specs), len(scratch_shapes)
    c_in = [len(cg.arrays) for cg in cargos]
    n_cin = sum(c_in)

    def wrapped(*refs):
        ins = refs[:n_in]
        cins = refs[n_in:n_in + n_cin]
        outs = refs[n_in + n_cin:n_in + n_cin + n_out]
        couts = refs[n_in + n_cin + n_out:n_in + 2 * n_cin + n_out]
        scr = refs[n_in + 2 * n_cin + n_out:n_in + 2 * n_cin + n_out + n_scr]
        sems = refs[n_in + 2 * n_cin + n_out + n_scr:]
        step, n_steps = 0, 1
        for ax, size in enumerate(grid):
            step = step * size + pl.program_id(ax)
            n_steps *= size

        def each(method, only_two_level=False):
            at = 0
            for k, cg in enumerate(cargos):
                if cg.two_level or not only_two_level:
                    getattr(cg, method)(cins[at:at + c_in[k]], couts[at:at + c_in[k]],
                                        sems[N_CARGO_SEMS * k:N_CARGO_SEMS * (k + 1)])
                at += c_in[k]

        body(*ins, *outs, *scr)
        if cargos:
            pl.when(step == 0)(lambda: each("start"))
        if any(cg.two_level for cg in cargos):
            pl.when(step == (RELAY_AT_EIGHTHS * n_steps) // 8)(lambda: each("relay", only_two_level=True))
        if cargos:
            pl.when(step == n_steps - 1)(lambda: each("wait"))

    res = pl.pallas_call(
        wrapped, name=name, grid=grid,
        in_specs=list(in_specs) + [HBM] * n_cin,
        out_specs=list(out_specs) + [HBM] * n_cin,
        out_shape=list(out_shape) + [s for cg in cargos for s in cg.out_shape],
        scratch_shapes=list(scratch_shapes) + [s for cg in cargos for s in cg.scratch],
        compiler_params=pltpu.CompilerParams(dimension_semantics=("arbitrary",) * len(grid),
                                             vmem_limit_bytes=VMEM_LIMIT_BYTES),
    )(*args, *[a for cg in cargos for a in cg.arrays])
    outs, rest = list(res[:n_out]), list(res[n_out:])
    cargo_outs = []
    for k in c_in:
        cargo_outs.append(rest[:k])
        rest = rest[k:]
    return outs, cargo_outs


def _exchange(cargo, name):
    _, (outs,) = _call(lambda: None, name=name, grid=(1,), in_specs=[], out_specs=[], out_shape=[], args=[],
                       cargos=[cargo])
    return outs


def _ffn_up_gather(x, gain, wg_t, wu_t, name, cargos=()):
    t_len, d = x.shape
    fq = wg_t.shape[0]
    tm = min(TM_FFN, t_len)
    n_tiles = t_len // tm
    relay_tile = n_tiles // 2
    fetch_tile = min(relay_tile + 1, n_tiles - 1)

    def body(x_ref, g_ref, wg_in, wu_in, h_ref, s_ref, p_ref, a_ref, wg_all, wu_all,
             wg_v, wu_v, h_all, send_sems, recv_sems, pass_send_sems, pass_recv_sems, own_sems, load_sems):
        s = pl.program_id(0)
        i = pl.program_id(1)
        x_, y_, c_, chips = _place()
        shards = ((wg_in, wg_all, wg_v), (wu_in, wu_all, wu_v))
        mine = pl.ds(pl.multiple_of(c_ * (fq // 2), SUBLANES), fq // 2)
        theirs = pl.ds(pl.multiple_of((1 - c_) * (fq // 2), SUBLANES), fq // 2)

        def to_peer(k, j):
            w_in, w_all, _ = shards[k]
            return pltpu.make_async_remote_copy(
                src_ref=w_in.at[mine], dst_ref=w_all.at[j + 1, mine], send_sem=send_sems.at[3 * k + j],
                recv_sem=recv_sems.at[3 * k + j], device_id=(*chips[j], c_), device_id_type=MESH)

        def to_sibling(k, j, landing=False):
            w_all = shards[k][1]
            return pltpu.make_async_remote_copy(
                src_ref=w_all.at[j + 1, mine], dst_ref=w_all.at[j + 1, theirs if landing else mine],
                send_sem=pass_send_sems.at[3 * k + j], recv_sem=pass_recv_sems.at[3 * k + j],
                device_id=(x_, y_, 1 - c_), device_id_type=MESH)

        def keep(k):
            return pltpu.make_async_copy(shards[k][0], shards[k][1].at[0], own_sems.at[k])

        @pl.when((s == 0) & (i == 0))
        def _():
            for j in range(N_CHIPS - 1):
                for k in range(2):
                    to_peer(k, j).start()
            for k in range(2):
                keep(k).start()

        def load(k, slot):
            src = shards[k][0] if slot == 0 else shards[k][1].at[slot]
            return pltpu.make_async_copy(src, shards[k][2].at[slot % 2], load_sems.at[k])

        @pl.when((s == 0) & (i == 0))
        def _():
            for k in range(2):
                load(k, 0).start()
            for k in range(2):
                load(k, 0).wait()

        def pass_on(slot):
            for k in range(2):
                to_peer(k, slot - 1).wait_recv()
                to_sibling(k, slot - 1).start()

        def fetch(slot):
            for k in range(2):
                to_sibling(k, slot - 1, landing=True).wait_recv()
                load(k, slot).start()

        for slot in range(1, N_CHIPS):
            pl.when((s == slot - 1) & (i == relay_tile))(functools.partial(pass_on, slot))
            pl.when((s == slot - 1) & (i == fetch_tile))(functools.partial(fetch, slot))

            @pl.when((s == slot) & (i == 0))
            def _():
                for k in range(2):
                    load(k, slot).wait()

        @pl.when(s == 0)
        def _():
            _, n = _rms_stats(x_ref[...])
            h_new = (n * g_ref[...]).astype(BF16)
            h_ref[...] = h_new
            h_all[i] = h_new

        h = h_all[i]
        silu, dgate, act = _swiglu_saved(_dot_nt(h, wg_v[s % 2]), _dot_nt(h, wu_v[s % 2]))
        s_ref[...] = silu.astype(BF16)
        p_ref[...] = dgate.astype(BF16)
        a_ref[...] = act.astype(BF16)

        @pl.when((s == N_CHIPS - 1) & (i == n_tiles - 1))
        def _():
            for k in range(2):
                for j in range(N_CHIPS - 1):
                    to_peer(k, j).wait_send()
                    to_sibling(k, j).wait_send()
                keep(k).wait()

    tok = pl.BlockSpec((tm, d), lambda s, i: (jnp.where(s == 0, i, n_tiles - 1), 0))
    hid = pl.BlockSpec((None, tm, fq), lambda s, i: (s, i, 0))
    outs, cargo_outs = _call(
        body, name=name, grid=(N_CHIPS, n_tiles),
        in_specs=[tok, pl.BlockSpec((1, d), lambda s, i: (0, 0)), HBM, HBM],
        out_specs=[tok, hid, hid, hid, HBM, HBM],
        out_shape=[_sds((t_len, d), BF16)] + [_sds((N_CHIPS, t_len, fq), BF16)] * 3
        + [_sds((N_CHIPS, fq, d), BF16)] * 2,
        scratch_shapes=[pltpu.VMEM((2, fq, d), BF16), pltpu.VMEM((2, fq, d), BF16),
                        pltpu.VMEM((n_tiles, tm, d), BF16)]
        + [pltpu.SemaphoreType.DMA((6,))] * 4 + [pltpu.SemaphoreType.DMA((2,))] * 2,
        args=[x, gain, wg_t, wu_t], cargos=cargos)
    return outs, cargo_outs


def _load_once(hbm_refs, vmem_refs, sems, first):
    @pl.when(first)
    def _():
        copies = [pltpu.make_async_copy(src, dst, sems.at[k]) for k, (src, dst) in enumerate(zip(hbm_refs, vmem_refs))]
        for cp in copies:
            cp.start()
        for cp in copies:
            cp.wait()


def _ffn_down(x, act, wd, name, cargos=()):
    t_len, d = x.shape
    nq, fq, _ = wd.shape
    tm = min(TM_FFN, t_len)

    def body(x_ref, a_ref, wd_ref, xo_ref):
        y = _dot(a_ref[0], wd_ref[0])
        for j in range(1, nq):
            y = y + _dot(a_ref[j], wd_ref[j])
        xo_ref[...] = x_ref[...] + FFN_RES_WEIGHT * y

    tok = pl.BlockSpec((tm, d), lambda i: (i, 0))
    (xo,), cargo_outs = _call(
        body, name=name, grid=(t_len // tm,),
        in_specs=[tok, pl.BlockSpec((nq, tm, fq), lambda i: (0, i, 0)), pl.BlockSpec((nq, fq, d), lambda i: (0, 0, 0))],
        out_specs=[tok], out_shape=[_sds((t_len, d), F32)], args=[x, act, wd], cargos=cargos)
    return xo, cargo_outs


def _ffn_fwd(x, gain, wg_t, wu_t, wd, name):
    t_len, d = x.shape
    nq, fq, _ = wd.shape
    tm = min(TM_FFN, t_len)

    def body(x_ref, g_ref, wg_hbm, wu_hbm, wd_hbm, xo_ref, h_ref, s_ref, p_ref, a_ref,
             h_s, acc, wg_v, wu_v, wd_v, load_sems):
        i = pl.program_id(0)
        j = pl.program_id(1)
        _load_once((wg_hbm, wu_hbm, wd_hbm), (wg_v, wu_v, wd_v), load_sems, (i == 0) & (j == 0))

        @pl.when(j == 0)
        def _():
            _, n = _rms_stats(x_ref[...])
            h = (n * g_ref[...]).astype(BF16)
            h_s[...] = h
            h_ref[...] = h
            acc[...] = jnp.zeros_like(acc)

        h = h_s[...]
        y = None
        for jj in range(SLOTS_PER_STEP):
            slot = j * SLOTS_PER_STEP + jj
            silu, dgate, act = _swiglu_saved(_dot_nt(h, wg_v[slot]), _dot_nt(h, wu_v[slot]))
            s_ref[jj] = silu.astype(BF16)
            p_ref[jj] = dgate.astype(BF16)
            a_ref[jj] = act.astype(BF16)
            part = _dot(a_ref[jj], wd_v[slot])
            y = part if y is None else y + part
        acc[...] += y

        @pl.when(j == nq // SLOTS_PER_STEP - 1)
        def _():
            xo_ref[...] = x_ref[...] + FFN_RES_WEIGHT * acc[...]

    tok = pl.BlockSpec((tm, d), lambda i, j: (i, 0))
    hid = pl.BlockSpec((SLOTS_PER_STEP, tm, fq), lambda i, j: (j, i, 0))
    outs, _ = _call(
        body, name=name, grid=(t_len // tm, nq // SLOTS_PER_STEP),
        in_specs=[tok, pl.BlockSpec((1, d), lambda i, j: (0, 0)), HBM, HBM, HBM],
        out_specs=[tok, tok, hid, hid, hid],
        out_shape=[_sds((t_len, d), F32), _sds((t_len, d), BF16)] + [_sds((nq, t_len, fq), BF16)] * 3,
        scratch_shapes=[pltpu.VMEM((tm, d), BF16), pltpu.VMEM((tm, d), F32)]
        + [pltpu.VMEM((nq, fq, d), BF16)] * 3 + [pltpu.SemaphoreType.DMA((3,))],
        args=[x, gain, wg_t, wu_t, wd])
    return outs


def _ffn_bwd(dy, x_in, gain, silu, dgate_du, wg_t, wu_t, wd, name):
    t_len, d = dy.shape
    nq, fq, _ = wd.shape
    tm = min(TM_FFN, t_len)

    def body(dy_ref, x_ref, g_ref, s_ref, p_ref, wg_hbm, wu_hbm, wd_hbm,
             dx_ref, dgain_ref, df_ref, dg_ref, du_ref, df_s, dh_acc, dact_s, wg_v, wu_v, wd_v, load_sems):
        i = pl.program_id(0)
        j = pl.program_id(1)
        _load_once((wg_hbm, wu_hbm, wd_hbm), (wg_v, wu_v, wd_v), load_sems, (i == 0) & (j == 0))

        @pl.when((i == 0) & (j == 0))
        def _():
            dgain_ref[...] = jnp.zeros_like(dgain_ref)

        @pl.when(j == 0)
        def _():
            df = (FFN_RES_WEIGHT * dy_ref[...]).astype(BF16)
            df_s[...] = df
            df_ref[...] = df
            dh_acc[...] = jnp.zeros_like(dh_acc)

        half = tm // 2
        for r0 in (0, half):
            dact_s[r0:r0 + half, :] = _dot_nt(df_s[r0:r0 + half, :], wd_v[j])

        for r0 in range(0, tm, STRIP):
            dact = dact_s[r0:r0 + STRIP, :]
            dg_ref[r0:r0 + STRIP, :] = (dact * p_ref[r0:r0 + STRIP, :].astype(F32)).astype(BF16)
            du_ref[r0:r0 + STRIP, :] = (dact * s_ref[r0:r0 + STRIP, :].astype(F32)).astype(BF16)

        for r0 in (0, half):
            rows = slice(r0, r0 + half)
            dh_acc[rows, :] += _dot(dg_ref[rows, :], wg_v[j]) + _dot(du_ref[rows, :], wu_v[j])

        @pl.when(j == nq - 1)
        def _():
            r, n = _rms_stats(x_ref[...])
            dh = dh_acc[...]
            dgain_ref[...] += jnp.sum(dh * n, axis=0, keepdims=True)
            dx_ref[...] = dy_ref[...] + _rms_bwd(dh, n, r, g_ref[...])

    tok = pl.BlockSpec((tm, d), lambda i, j: (i, 0))
    vec = pl.BlockSpec((1, d), lambda i, j: (0, 0))
    hid = pl.BlockSpec((None, tm, fq), lambda i, j: (j, i, 0))
    outs, _ = _call(
        body, name=name, grid=(t_len // tm, nq),
        in_specs=[tok, tok, vec, hid, hid, HBM, HBM, HBM],
        out_specs=[tok, vec, tok, hid, hid],
        out_shape=[_sds((t_len, d), F32), _sds((1, d), F32), _sds((t_len, d), BF16),
                   _sds((nq, t_len, fq), BF16), _sds((nq, t_len, fq), BF16)],
        scratch_shapes=[pltpu.VMEM((tm, d), BF16), pltpu.VMEM((tm, d), F32), pltpu.VMEM((tm, fq), F32)]
        + [pltpu.VMEM((nq, fq, d), BF16)] * 3 + [pltpu.SemaphoreType.DMA((3,))],
        args=[dy, x_in, gain, silu, dgate_du, wg_t, wu_t, wd])
    return outs


def _wgrad(lhs, rhs, l_spec, r_spec, out_shape, out_spec, acc_shape, grid, name, cargos=()):
    n_t = grid[-1]
    t_axis = len(grid) - 1

    def body(l_ref, r_ref, o_ref, acc):
        t = pl.program_id(t_axis)

        @pl.when(t == 0)
        def _():
            acc[...] = jnp.zeros_like(acc)

        acc[...] += _dot_tn(l_ref[...].astype(BF16), r_ref[...].astype(BF16))

        @pl.when(t == n_t - 1)
        def _():
            o_ref[...] = acc[...].astype(o_ref.dtype)

    (out,), cargo_outs = _call(
        body, name=name, grid=grid, in_specs=[l_spec, r_spec], out_specs=[out_spec], out_shape=[out_shape],
        scratch_shapes=[pltpu.VMEM(acc_shape, F32)], args=[lhs, rhs], cargos=cargos)
    return out, cargo_outs


def _wgrad_hid_tok_scatter(hid, tok, name, cargos=()):
    t_len, d = tok.shape
    nq, _, fq = hid.shape
    half = fq // 2
    tt = min(TT_WGRAD, t_len)
    n_t = t_len // tt

    def body(l_ref, r_ref, parts_ref, acc, stage, pair, summed, zeros,
             pair_send_sems, pair_recv_sems, send_sems, recv_sems, own_sem, zero_sems):
        g = pl.program_id(0)
        t = pl.program_id(1)
        x_, y_, c_, chips = _place()
        mine = pl.ds(pl.multiple_of(c_ * half, STRIP), half)
        theirs = pl.ds(pl.multiple_of((1 - c_) * half, STRIP), half)

        def to_sibling(slot):
            return pltpu.make_async_remote_copy(
                src_ref=stage.at[slot, theirs], dst_ref=pair.at[slot], send_sem=pair_send_sems.at[slot],
                recv_sem=pair_recv_sems.at[slot], device_id=(x_, y_, 1 - c_), device_id_type=MESH)

        def to_peer(j):
            return pltpu.make_async_remote_copy(
                src_ref=summed.at[j + 1], dst_ref=parts_ref.at[j + 1, mine], send_sem=send_sems.at[j],
                recv_sem=recv_sems.at[j], device_id=(*chips[j], c_), device_id_type=MESH)

        keep = pltpu.make_async_copy(summed.at[0], parts_ref.at[0, mine], own_sem)

        def blank(slot):
            return pltpu.make_async_copy(zeros, parts_ref.at[slot, theirs], zero_sems.at[slot])

        @pl.when((g == 0) & (t == 0))
        def _():
            zeros[...] = jnp.zeros_like(zeros)
            for slot in range(nq):
                blank(slot).start()

        @pl.when(t == 0)
        def _():
            acc[...] = jnp.zeros_like(acc)

        acc[...] += _dot_tn(l_ref[...], r_ref[...])

        for step in range(nq):
            slot = (step + 1) % nq

            @pl.when((g == step) & (t == n_t - 1))
            def _():
                stage[slot] = acc[...].astype(BF16)
                to_sibling(slot).start()
                to_sibling(slot).wait_recv()
                summed[slot] = (stage[slot, mine, :].astype(F32) + pair[slot].astype(F32)).astype(BF16)
                if slot > 0:
                    to_peer(slot - 1).start()
                else:
                    keep.start()

        @pl.when((g == nq - 1) & (t == n_t - 1))
        def _():
            for j in range(N_CHIPS - 1):
                to_peer(j).wait()
            keep.wait()
            for slot in range(nq):
                to_sibling(slot).wait_send()
                blank(slot).wait()

    (parts,), cargo_outs = _call(
        body, name=name, grid=(nq, n_t),
        in_specs=[pl.BlockSpec((None, tt, fq), lambda g, t: ((g + 1) % nq, t, 0)),
                  pl.BlockSpec((tt, d), lambda g, t: (t, 0))],
        out_specs=[HBM], out_shape=[_sds((nq, fq, d), BF16)],
        scratch_shapes=[pltpu.VMEM((fq, d), F32), pltpu.VMEM((nq, fq, d), BF16), pltpu.VMEM((nq, half, d), BF16),
                        pltpu.VMEM((nq, half, d), BF16), pltpu.VMEM((half, d), BF16),
                        pltpu.SemaphoreType.DMA((nq,)), pltpu.SemaphoreType.DMA((nq,)),
                        pltpu.SemaphoreType.DMA((N_CHIPS - 1,)), pltpu.SemaphoreType.DMA((N_CHIPS - 1,)),
                        pltpu.SemaphoreType.DMA(()), pltpu.SemaphoreType.DMA((nq,))],
        args=[hid, tok], cargos=cargos)
    return parts, cargo_outs


def _wgrad_2d(lhs, rhs, n_col_blocks, out_dtype, name, group_diag=False, cargos=()):
    t_len, k = lhs.shape
    n = rhs.shape[1]
    nb = n // n_col_blocks
    kb = k // n_col_blocks if group_diag else k
    tt = min(TT_WGRAD, t_len)
    l_map = (lambda q, t: (t, q)) if group_diag else (lambda q, t: (t, 0))
    return _wgrad(lhs, rhs,
                  pl.BlockSpec((tt, kb), l_map),
                  pl.BlockSpec((tt, nb), lambda q, t: (t, q)),
                  _sds((n_col_blocks, kb, nb), out_dtype),
                  pl.BlockSpec((None, kb, nb), lambda q, t: (q, 0, 0)),
                  (kb, nb), (n_col_blocks, t_len // tt), name, cargos)


def _layernorm_stats(u1):
    mu = jnp.mean(u1, axis=-1, keepdims=True)
    xc = u1 - mu
    rstd = lax.rsqrt(jnp.mean(xc * xc, axis=-1, keepdims=True) + LN_EPS)
    return rstd, xc * rstd


def _positions(i, tm, rows, offset=0):
    return (lax.broadcasted_iota(jnp.int32, (rows, 1), 0) + (i * tm + offset)).astype(F32)


SHIFT_ROWS = HALO - SUBLANES


def _fill_shifted(ext_s, sh_s, tm):
    for b in range(1, SUBLANES):
        sh_s[b - 1] = ext_s[pl.ds(b, tm + SHIFT_ROWS), :]


def _window(ext_s, sh_s, shift, tm):
    a, b = divmod(shift, SUBLANES)
    if b == 0:
        return ext_s[pl.ds(shift, tm), :]
    return sh_s[b - 1, pl.ds(a * SUBLANES, tm), :]


def _tile(tm, cols):
    return pl.BlockSpec((tm, cols), lambda i: (i, 0))


def _whole(shape):
    return pl.BlockSpec(shape, lambda i: (0,) * len(shape))


def _mix_fwd(x1, gain, w_in, conv_dw, conv_b, ln_g, ln_b, conv_pw, pool_w, pool_scale, w_out, name, cargos=()):
    t_len, d = x1.shape
    nq, _, nb = w_in.shape
    tm = min(TM_MIX, t_len)

    def body(x_ref, g_ref, wi_ref, dw_ref, cb_ref, lg_ref, lb_ref, pw_ref, plw_ref, ps_ref, wo_ref,
             x2_ref, h_ref, p_ref, u1_ref, u3_ref, mx_ref, cat_ref, ext_s, pext_s, sh_s, tail_s):
        i = pl.program_id(0)

        @pl.when(i == 0)
        def _():
            tail_s[...] = jnp.zeros_like(tail_s)

        _, n = _rms_stats(x_ref[...])
        h = (n * g_ref[...]).astype(BF16)
        h_ref[...] = h
        for q in range(nq):
            p_ref[:, q * nb:(q + 1) * nb] = _dot(h, wi_ref[q])

        a = p_ref[:, 0:D_CONV]
        g = p_ref[:, D_CONV:2 * D_CONV]
        p = p_ref[:, 2 * D_CONV:]
        ext_s[0:HALO, :] = tail_s[:, 0:D_CONV] * jax.nn.sigmoid(tail_s[:, D_CONV:2 * D_CONV])
        ext_s[HALO:, :] = a * jax.nn.sigmoid(g)
        pext_s[0:HALO, :] = tail_s[:, 2 * D_CONV:]
        pext_s[HALO:, :] = p
        tail_s[...] = p_ref[tm - HALO:tm, :]

        _fill_shifted(ext_s, sh_s, tm)
        u1 = jnp.broadcast_to(cb_ref[...], (tm, D_CONV))
        for k in range(CONV_WIDTH):
            u1 = u1 + dw_ref[k:k + 1, :] * _window(ext_s, sh_s, HALO - (CONV_WIDTH - 1) + k, tm)
        u1_ref[...] = u1
        _, nhat = _layernorm_stats(u1)
        u2 = nhat * lg_ref[...] + lb_ref[...]
        u3 = (u2 * jax.nn.sigmoid(u2)).astype(BF16)
        u3_ref[...] = u3
        cat_ref[:, 0:D_CONV] = _dot(u3, pw_ref[...]).astype(BF16)

        pos1 = _positions(i, tm, tm) + 1.0
        for gi, w in enumerate(POOL_WINDOWS):
            cols = slice(gi * POOL_GROUP, (gi + 1) * POOL_GROUP)
            s = pext_s[pl.ds(HALO, tm), cols]
            for j in range(1, w):
                s = s + pext_s[pl.ds(HALO - j, tm), cols]
            mixed = (s / jnp.minimum(pos1, float(w)) - p[:, cols]).astype(BF16)
            mx_ref[:, cols] = mixed
            out = _dot(mixed, plw_ref[gi]) * ps_ref[:, cols]
            cat_ref[:, D_CONV + gi * POOL_GROUP:D_CONV + (gi + 1) * POOL_GROUP] = out.astype(BF16)

        x2_ref[...] = x_ref[...] + _dot(cat_ref[...], wo_ref[...])

    return _call(
        body, name=name, grid=(t_len // tm,),
        in_specs=[_tile(tm, d), _whole((1, d)), _whole((nq, d, nb)), _whole((CONV_WIDTH + 1, D_CONV)),
                  _whole((1, D_CONV)), _whole((1, D_CONV)), _whole((1, D_CONV)), _whole((D_CONV, D_CONV)),
                  _whole((4, POOL_GROUP, POOL_GROUP)), _whole((1, D_POOL)), _whole((D_CONV + D_POOL, d))],
        out_specs=[_tile(tm, d), _tile(tm, d), _tile(tm, D_IN), _tile(tm, D_CONV), _tile(tm, D_CONV),
                   _tile(tm, D_POOL), _tile(tm, D_CONV + D_POOL)],
        out_shape=[_sds((t_len, d), F32), _sds((t_len, d), BF16), _sds((t_len, D_IN), F32),
                   _sds((t_len, D_CONV), F32), _sds((t_len, D_CONV), BF16), _sds((t_len, D_POOL), BF16),
                   _sds((t_len, D_CONV + D_POOL), BF16)],
        scratch_shapes=[pltpu.VMEM((tm + HALO, D_CONV), F32), pltpu.VMEM((tm + HALO, D_POOL), F32),
                        pltpu.VMEM((SUBLANES - 1, tm + SHIFT_ROWS, D_CONV), F32), pltpu.VMEM((HALO, D_IN), F32)],
        args=[x1, gain, w_in, conv_dw, conv_b, ln_g, ln_b, conv_pw, pool_w, pool_scale, w_out], cargos=cargos)


def _mix_bwd_local(dx2, u1, mixed, ln_g, ln_b, conv_pw, pool_w, pool_scale, w_out, name, cargos=()):
    t_len, d = dx2.shape
    tm = min(TM_MIX, t_len)

    def body(dx_ref, u1_ref, mx_ref, lg_ref, lb_ref, pw_ref, plw_ref, ps_ref, wo_ref,
             du1_ref, dmx_ref, dco_ref, dpo_ref, dlg_ref, dlb_ref, dps_ref):
        @pl.when(pl.program_id(0) == 0)
        def _():
            dlg_ref[...] = jnp.zeros_like(dlg_ref)
            dlb_ref[...] = jnp.zeros_like(dlb_ref)
            dps_ref[...] = jnp.zeros_like(dps_ref)

        dcat = _dot_nt(dx_ref[...].astype(BF16), wo_ref[...])
        dco = dcat[:, 0:D_CONV].astype(BF16)
        dco_ref[...] = dco
        du3 = _dot_nt(dco, pw_ref[...])
        rstd, nhat = _layernorm_stats(u1_ref[...])
        u2 = nhat * lg_ref[...] + lb_ref[...]
        sig = jax.nn.sigmoid(u2)
        du2 = du3 * (sig * (1.0 + u2 * (1.0 - sig)))
        dlg_ref[...] += jnp.sum(du2 * nhat, axis=0, keepdims=True)
        dlb_ref[...] += jnp.sum(du2, axis=0, keepdims=True)
        dnhat = du2 * lg_ref[...]
        du1_ref[...] = rstd * (dnhat - jnp.mean(dnhat, axis=-1, keepdims=True)
                               - nhat * jnp.mean(dnhat * nhat, axis=-1, keepdims=True))

        for gi in range(len(POOL_WINDOWS)):
            cols = slice(gi * POOL_GROUP, (gi + 1) * POOL_GROUP)
            dpo = dcat[:, D_CONV + gi * POOL_GROUP:D_CONV + (gi + 1) * POOL_GROUP]
            pre = _dot(mx_ref[:, cols], plw_ref[gi])
            dps_ref[:, cols] += jnp.sum(dpo * pre, axis=0, keepdims=True)
            dout = (dpo * ps_ref[:, cols]).astype(BF16)
            dpo_ref[:, cols] = dout
            dmx_ref[:, cols] = _dot_nt(dout, plw_ref[gi])

    vec = _whole((1, D_CONV))
    return _call(
        body, name=name, grid=(t_len // tm,),
        in_specs=[_tile(tm, d), _tile(tm, D_CONV), _tile(tm, D_POOL), vec, vec, _whole((D_CONV, D_CONV)),
                  _whole((4, POOL_GROUP, POOL_GROUP)), vec, _whole((D_CONV + D_POOL, d))],
        out_specs=[_tile(tm, D_CONV), _tile(tm, D_POOL), _tile(tm, D_CONV), _tile(tm, D_POOL), vec, vec, vec],
        out_shape=[_sds((t_len, D_CONV), F32), _sds((t_len, D_POOL), F32), _sds((t_len, D_CONV), BF16),
                   _sds((t_len, D_POOL), BF16), _sds((1, D_CONV), F32), _sds((1, D_CONV), F32),
                   _sds((1, D_POOL), F32)],
        args=[dx2, u1, mixed, ln_g, ln_b, conv_pw, pool_w, pool_scale, w_out], cargos=cargos)


def _mix_bwd_seq(du1, dmixed, proj, x1, dx2, gain, conv_dw, w_in, name, cargos=()):
    t_len, d = x1.shape
    nq, _, nb = w_in.shape
    tm = min(TM_MIX, t_len)
    hb = tm // HALO
    last_block = t_len // HALO - 1
    n_tiles = t_len // tm

    def body(du_ref, dun_ref, dm_ref, dmn_ref, p_ref, tail_ref, x_ref, dx2_ref, g_ref, dw_ref, wi_ref,
             dx1_ref, dp_ref, ddw_ref, dcb_ref, dgain_ref, uext_s, dext_s, mext_s, ush_s, dsh_s):
        i = pl.program_id(0)
        first = i == 0
        last = i == n_tiles - 1

        @pl.when(first)
        def _():
            ddw_ref[...] = jnp.zeros_like(ddw_ref)
            dcb_ref[...] = jnp.zeros_like(dcb_ref)
            dgain_ref[...] = jnp.zeros_like(dgain_ref)

        a = p_ref[:, 0:D_CONV]
        g = p_ref[:, D_CONV:2 * D_CONV]
        sg = jax.nn.sigmoid(g)
        ta = tail_ref[:, 0:D_CONV]
        tg = tail_ref[:, D_CONV:2 * D_CONV]
        uext_s[0:HALO, :] = jnp.where(first, 0.0, ta * jax.nn.sigmoid(tg))
        uext_s[HALO:, :] = a * sg
        du1 = du_ref[...]
        dext_s[0:tm, :] = du1
        dext_s[tm:, :] = jnp.where(last, 0.0, dun_ref[...])

        _fill_shifted(uext_s, ush_s, tm)
        _fill_shifted(dext_s, dsh_s, tm)
        du0 = jnp.zeros((tm, D_CONV), F32)
        for k in range(CONV_WIDTH):
            du0 = du0 + dw_ref[k:k + 1, :] * _window(dext_s, dsh_s, CONV_WIDTH - 1 - k, tm)
            ddw_ref[k:k + 1, :] += jnp.sum(
                du1 * _window(uext_s, ush_s, HALO - (CONV_WIDTH - 1) + k, tm), axis=0, keepdims=True)
        dcb_ref[...] += jnp.sum(du1, axis=0, keepdims=True)
        dp_ref[:, 0:D_CONV] = (du0 * sg).astype(BF16)
        dp_ref[:, D_CONV:2 * D_CONV] = (du0 * a * sg * (1.0 - sg)).astype(BF16)

        pos1 = _positions(i, tm, tm) + 1.0
        pos1_next = _positions(i, tm, HALO, offset=tm) + 1.0
        for gi, w in enumerate(POOL_WINDOWS):
            cols = slice(gi * POOL_GROUP, (gi + 1) * POOL_GROUP)
            dm = dm_ref[:, cols]
            mext_s[0:tm, cols] = dm / jnp.minimum(pos1, float(w))
            mext_s[tm:, cols] = jnp.where(last, 0.0, dmn_ref[:, cols] / jnp.minimum(pos1_next, float(w)))
            s = mext_s[pl.ds(0, tm), cols]
            for j in range(1, w):
                s = s + mext_s[pl.ds(j, tm), cols]
            dp_ref[:, 2 * D_CONV + gi * POOL_GROUP:2 * D_CONV + (gi + 1) * POOL_GROUP] = (s - dm).astype(BF16)

        dh = _dot_nt(dp_ref[:, 0:nb], wi_ref[0])
        for q in range(1, nq):
            dh = dh + _dot_nt(dp_ref[:, q * nb:(q + 1) * nb], wi_ref[q])
        r, n = _rms_stats(x_ref[...])
        dgain_ref[...] += jnp.sum(dh * n, axis=0, keepdims=True)
        dx1_ref[...] = dx2_ref[...] + _rms_bwd(dh, n, r, g_ref[...])

    def nxt(cols):
        return pl.BlockSpec((HALO, cols), lambda i: (jnp.minimum((i + 1) * hb, last_block), 0))

    return _call(
        body, name=name, grid=(n_tiles,),
        in_specs=[_tile(tm, D_CONV), nxt(D_CONV), _tile(tm, D_POOL), nxt(D_POOL), _tile(tm, D_IN),
                  pl.BlockSpec((HALO, D_IN), lambda i: (jnp.maximum(i * hb - 1, 0), 0)),
                  _tile(tm, d), _tile(tm, d), _whole((1, d)), _whole((CONV_WIDTH + 1, D_CONV)),
                  _whole((nq, d, nb))],
        out_specs=[_tile(tm, d), _tile(tm, D_IN), _whole((CONV_WIDTH + 1, D_CONV)), _whole((1, D_CONV)),
                   _whole((1, d))],
        out_shape=[_sds((t_len, d), F32), _sds((t_len, D_IN), BF16), _sds((CONV_WIDTH + 1, D_CONV), F32),
                   _sds((1, D_CONV), F32), _sds((1, d), F32)],
        scratch_shapes=[pltpu.VMEM((tm + HALO, D_CONV), F32), pltpu.VMEM((tm + HALO, D_CONV), F32),
                        pltpu.VMEM((tm + HALO, D_POOL), F32),
                        pltpu.VMEM((SUBLANES - 1, tm + SHIFT_ROWS, D_CONV), F32),
                        pltpu.VMEM((SUBLANES - 1, tm + SHIFT_ROWS, D_CONV), F32)],
        args=[du1, du1, dmixed, dmixed, proj, proj, x1, dx2, gain, conv_dw, w_in], cargos=cargos)


def _final_norm_loss(x3, target, gain, name):
    t_len, d = x3.shape
    tm = min(TM_FFN, t_len)

    def body(x_ref, t_ref, g_ref, dx_ref, loss_ref, dgain_ref):
        @pl.when(pl.program_id(0) == 0)
        def _():
            loss_ref[...] = jnp.zeros_like(loss_ref)
            dgain_ref[...] = jnp.zeros_like(dgain_ref)

        r, n = _rms_stats(x_ref[...])
        err = n * g_ref[...] - t_ref[...]
        per_tok = jnp.sum(err * err, axis=-1, keepdims=True) * (1.0 / d)
        loss_ref[...] += 0.5 * jnp.sum(per_tok, axis=0, keepdims=True)
        dy = err * (1.0 / d)
        dgain_ref[...] += jnp.sum(dy * n, axis=0, keepdims=True)
        dx_ref[...] = _rms_bwd(dy, n, r, g_ref[...])

    tok = pl.BlockSpec((tm, d), lambda i: (i, 0))
    outs, _ = _call(
        body, name=name, grid=(t_len // tm,),
        in_specs=[tok, tok, pl.BlockSpec((1, d), lambda i: (0, 0))],
        out_specs=[tok, pl.BlockSpec((1, 128), lambda i: (0, 0)), pl.BlockSpec((1, d), lambda i: (0, 0))],
        out_shape=[_sds((t_len, d), F32), _sds((1, 128), F32), _sds((1, d), F32)],
        args=[x3, target, gain])
    return outs


def _row_tile(rows):
    return rows // 4 if rows % 64 == 0 else rows


def _sum_parts(parts, name):
    n, r, c = parts.shape
    tr = _row_tile(r)

    def body(p_ref, o_ref):
        s = p_ref[0].astype(F32)
        for k in range(1, n):
            s = s + p_ref[k].astype(F32)
        o_ref[...] = s

    (out,), _ = _call(body, name=name, grid=(r // tr,),
                      in_specs=[pl.BlockSpec((n, tr, c), lambda i: (0, i, 0))],
                      out_specs=[pl.BlockSpec((tr, c), lambda i: (i, 0))], out_shape=[_sds((r, c), F32)],
                      args=[parts])
    return out


def _adamw_math(w, g, m, v):
    m = ADAM_B1 * m + (1.0 - ADAM_B1) * g
    v = ADAM_B2 * v + (1.0 - ADAM_B2) * (g * g)
    m_hat = m / (1.0 - ADAM_B1 ** ADAM_STEP)
    v_hat = v / (1.0 - ADAM_B2 ** ADAM_STEP)
    delta = -ADAM_LR * (m_hat / (jnp.sqrt(v_hat) + ADAM_EPS) + ADAM_WD * w)
    return delta, m, v


def _adamw(parts, w, m, v, name):
    r, c = w.shape
    n = len(parts)
    tr = _row_tile(r)

    def body(*refs):
        terms = []
        for p_ref in refs[:n]:
            terms += [p_ref[...]] if len(p_ref.shape) == 2 else [p_ref[k] for k in range(p_ref.shape[0])]
        w_ref, m_ref, v_ref, g_out, d_out, m_out, v_out = refs[n:]
        g = terms[0]
        for t in terms[1:]:
            g = g + t
        delta, nm, nv = _adamw_math(w_ref[...], g, m_ref[...], v_ref[...])
        g_out[...] = g
        d_out[...] = delta
        m_out[...] = nm
        v_out[...] = nv

    blk = pl.BlockSpec((tr, c), lambda i: (i, 0))
    p_specs = [blk if p.ndim == 2 else pl.BlockSpec((p.shape[0], tr, c), lambda i: (0, i, 0)) for p in parts]
    outs, _ = _call(body, name=name, grid=(r // tr,), in_specs=p_specs + [blk, blk, blk],
                    out_specs=[blk] * 4, out_shape=[_sds((r, c), F32)] * 4, args=[*parts, w, m, v])
    return outs


FFN_W = ("w_gate", "w_up", "w_down")
MID = ("w_in", "conv_dw", "conv_pw", "w_out")
SMALL_1024 = ("ffn1_norm", "mix_norm", "ffn2_norm", "final_norm")
SMALL_512 = ("conv_dw_b", "conv_ln_g", "conv_ln_b", "pool_scale")
WEIGHTS = ("ffn1_norm", "ffn1_w_gate", "ffn1_w_up", "ffn1_w_down", "mix_norm", "w_in", "conv_dw", "conv_dw_b",
           "conv_ln_g", "conv_ln_b", "conv_pw", "pool_w", "pool_scale", "w_out", "ffn2_norm", "ffn2_w_gate",
           "ffn2_w_up", "ffn2_w_down", "final_norm")
PACK_ROWS = 72


def _pad_rows(a, rows):
    return jnp.pad(a, ((0, rows - a.shape[0]), (0, 0)))


def _pack_small(t):
    rows = [t[k].reshape(1, D_MODEL) for k in SMALL_1024]
    rows.append(jnp.concatenate([t["conv_dw_b"].reshape(1, -1), t["conv_ln_g"].reshape(1, -1)], axis=1))
    rows.append(jnp.concatenate([t["conv_ln_b"].reshape(1, -1), t["pool_scale"].reshape(1, -1)], axis=1))
    rows.append(t["pool_w"].reshape(64, D_MODEL))
    return _pad_rows(jnp.concatenate(rows, axis=0), PACK_ROWS)


def _unpack_small(p):
    out = {k: p[i] for i, k in enumerate(SMALL_1024)}
    out["conv_dw_b"], out["conv_ln_g"] = p[4, :D_CONV], p[4, D_CONV:]
    out["conv_ln_b"], out["pool_scale"] = p[5, :D_CONV], p[5, D_CONV:]
    out["pool_w"] = p[6:70].reshape(4, POOL_GROUP, POOL_GROUP)
    return out


def _as_stored(name, a):
    if name.endswith(("w_gate", "w_up")):
        return a.T
    if name == "conv_dw":
        return _pad_rows(a, CONV_WIDTH + 1)
    return a


def _as_given(name, a):
    if name.endswith(("w_gate", "w_up")):
        return a.T
    if name == "conv_dw":
        return a[:CONV_WIDTH]
    return a


def kernel(x, ffn1_norm, ffn1_w_gate, ffn1_w_up, ffn1_w_down, mix_norm, w_in, conv_dw, conv_dw_b, conv_ln_g, conv_ln_b, conv_pw, pool_w, pool_scale, w_out, ffn2_norm, ffn2_w_gate, ffn2_w_up, ffn2_w_down, final_norm, loss_target, m_ffn1_norm, m_ffn1_w_gate, m_ffn1_w_up, m_ffn1_w_down, m_mix_norm, m_w_in, m_conv_dw, m_conv_dw_b, m_conv_ln_g, m_conv_ln_b, m_conv_pw, m_pool_w, m_pool_scale, m_w_out, m_ffn2_norm, m_ffn2_w_gate, m_ffn2_w_up, m_ffn2_w_down, m_final_norm, v_ffn1_norm, v_ffn1_w_gate, v_ffn1_w_up, v_ffn1_w_down, v_mix_norm, v_w_in, v_conv_dw, v_conv_dw_b, v_conv_ln_g, v_conv_ln_b, v_conv_pw, v_pool_w, v_pool_scale, v_w_out, v_ffn2_norm, v_ffn2_w_gate, v_ffn2_w_up, v_ffn2_w_down, v_final_norm):
    given = dict(locals())
    wts = {k: given[k] for k in WEIGHTS}
    mom_m = {k: given["m_" + k] for k in WEIGHTS}
    mom_v = {k: given["v_" + k] for k in WEIGHTS}
    xt, target = x[0], loss_target[0]

    shard = {k: _as_stored(k, wts[k]) if k == "conv_dw" else _as_stored(k, wts[k]).astype(BF16)
             for k in WEIGHTS if k.endswith(FFN_W) or k in MID}
    w = {k: wts[k].reshape(1, -1) for k in SMALL_1024 + SMALL_512}
    w["pool_w"] = wts["pool_w"].astype(BF16)

    (h1, s1, p1, a1, w["ffn1_w_gate"], w["ffn1_w_up"]), ((w["ffn1_w_down"],),) = _ffn_up_gather(
        xt, w["ffn1_norm"], shard["ffn1_w_gate"], shard["ffn1_w_up"], "ffn1_up_gather",
        cargos=[Cargo("gather_slots", [shard["ffn1_w_down"]])])
    x1, (mid, (w["ffn2_w_down"],)) = _ffn_down(
        xt, a1, w["ffn1_w_down"], "ffn1_down",
        cargos=[Cargo("gather_chips", [shard[k] for k in MID]), Cargo("gather_slots", [shard["ffn2_w_down"]])])
    w["w_in"] = mid[0]
    w["conv_dw"] = mid[1].transpose(1, 0, 2).reshape(CONV_WIDTH + 1, D_CONV)
    w["conv_pw"] = mid[2].reshape(D_CONV, D_CONV)
    w["w_out"] = mid[3].reshape(D_CONV + D_POOL, D_MODEL)
    (x2, h2, proj, u1, u3, mixed, cat), ((w["ffn2_w_gate"], w["ffn2_w_up"]),) = _mix_fwd(
        x1, w["mix_norm"], w["w_in"], w["conv_dw"], w["conv_dw_b"], w["conv_ln_g"], w["conv_ln_b"], w["conv_pw"],
        w["pool_w"], w["pool_scale"], w["w_out"], "mix_fwd",
        cargos=[Cargo("gather_slots", [shard["ffn2_w_gate"], shard["ffn2_w_up"]])])
    x3, h3, s2, p2, a2 = _ffn_fwd(x2, w["ffn2_norm"], w["ffn2_w_gate"], w["ffn2_w_up"], w["ffn2_w_down"], "ffn2_fwd")
    dx3, loss, d_final = _final_norm_loss(x3, target, w["final_norm"], "final_norm_loss")
    loss = lax.psum(loss[0, 0], ("x", "y", "c"))

    g = {"final_norm": d_final}
    sums = {}

    def landed(names, parts):
        for k, p in zip(names, parts):
            sums[k] = _sum_parts(p, "sum_chips_" + k)

    dx2, g["ffn2_norm"], df2, dg2, du2 = _ffn_bwd(dx3, x2, w["ffn2_norm"], s2, p2, w["ffn2_w_gate"],
                                                   w["ffn2_w_up"], w["ffn2_w_down"], "ffn2_bwd")
    def ffn_wgrad(name, hid, tok, cargos=()):
        parts, cargo_outs = _wgrad_hid_tok_scatter(hid, tok, name.replace("_w_", "_dw_"), cargos=cargos)
        landed([name], [parts])
        return cargo_outs

    ffn_wgrad("ffn2_w_gate", dg2, h3)
    ffn_wgrad("ffn2_w_up", du2, h3)
    ffn_wgrad("ffn2_w_down", a2, df2)
    (du1, dmixed, dco, dpo, g["conv_ln_g"], g["conv_ln_b"], g["pool_scale"]), (swapped2,) = _mix_bwd_local(
        dx2, u1, mixed, w["conv_ln_g"], w["conv_ln_b"], w["conv_pw"], w["pool_w"], w["pool_scale"], w["w_out"],
        "mix_bwd_local", cargos=[Cargo("swap", [sums["ffn2_" + k] for k in FFN_W])])
    g_out, _ = _wgrad_2d(cat, dx2, 1, BF16, "dw_out")
    g_pw, _ = _wgrad_2d(u3, dco, 1, BF16, "dconv_pw")
    g["pool_w"], _ = _wgrad_2d(mixed, dpo, 4, F32, "dpool_w", group_diag=True)
    slabs = [g_pw.reshape(N_CHIPS, D_CONV // N_CHIPS, D_CONV),
             g_out.reshape(N_CHIPS, (D_CONV + D_POOL) // N_CHIPS, D_MODEL)]
    (dx1, dproj, g_dw, g["conv_dw_b"], g["mix_norm"]), (parts,) = _mix_bwd_seq(
        du1, dmixed, proj, x1, dx2, w["mix_norm"], w["conv_dw"], w["w_in"], "mix_bwd_seq",
        cargos=[Cargo("scatter_chips", slabs)])
    landed(["conv_pw", "w_out"], parts)
    g_in, _ = _wgrad_2d(h2, dproj, N_CHIPS, BF16, "dw_in")
    dx, g["ffn1_norm"], df1, dg1, du1_ = _ffn_bwd(dx1, xt, w["ffn1_norm"], s1, p1, w["ffn1_w_gate"],
                                                   w["ffn1_w_up"], w["ffn1_w_down"], "ffn1_bwd")
    slabs = [g_in, g_dw.reshape(CONV_WIDTH + 1, N_CHIPS, D_CONV // N_CHIPS).transpose(1, 0, 2)]
    (parts,) = ffn_wgrad("ffn1_w_gate", dg1, h1, cargos=[Cargo("scatter_chips", slabs)])
    landed(["w_in", "conv_dw"], parts)
    swapped_mid, swapped_gate, small_parts = ffn_wgrad(
        "ffn1_w_up", du1_, h1,
        cargos=[Cargo("swap", [sums[k] for k in MID]), Cargo("swap", [sums["ffn1_w_gate"]]),
                Cargo("gather_devices", [_pack_small(g)])])
    (swapped_up,) = ffn_wgrad("ffn1_w_down", a1, df1, cargos=[Cargo("swap", [sums["ffn1_w_up"]])])
    swapped_down = _exchange(Cargo("swap", [sums["ffn1_w_down"]]), "swap_last")

    theirs = dict(zip(["ffn2_" + k for k in FFN_W], swapped2))
    theirs.update(zip(MID, swapped_mid))
    theirs.update(ffn1_w_gate=swapped_gate[0], ffn1_w_up=swapped_up[0], ffn1_w_down=swapped_down[0])
    grads, deltas, new_m, new_v = {}, {}, {}, {}
    for k in theirs:
        res = _adamw([sums[k], theirs[k]], _as_stored(k, wts[k]), _as_stored(k, mom_m[k]),
                     _as_stored(k, mom_v[k]), "adamw_" + k)
        grads[k], deltas[k], new_m[k], new_v[k] = [_as_given(k, t) for t in res]
    res = _adamw(small_parts, _pack_small(wts), _pack_small(mom_m), _pack_small(mom_v), "adamw_small")
    for dst, packed in zip((grads, deltas, new_m, new_v), res):
        dst.update(_unpack_small(packed))

    out = [loss, dx[None]]
    for group in (grads, deltas, new_m, new_v):
        out += [group[k] for k in WEIGHTS]
    return tuple(out)
```

```python
import functools

import jax
import jax.numpy as jnp
from jax import lax
from jax.experimental import pallas as pl
from jax.experimental.pallas import tpu as pltpu

F32 = jnp.float32
BF16 = jnp.bfloat16
MESH = pl.DeviceIdType.MESH

N_CHIPS = 4
N_DEV = 8
D_MODEL = 1024
D_CONV = 512
D_POOL = 512
CONV_WIDTH = 31
POOL_WINDOWS = (2, 4, 8, 16)
POOL_GROUP = 128
D_IN = 2 * D_CONV + D_POOL
HALO = 32
RMS_EPS = 1e-6
LN_EPS = 1e-5
FFN_RES_WEIGHT = 0.5
ADAM_LR = 0.001
ADAM_B1 = 0.9
ADAM_B2 = 0.999
ADAM_EPS = 1e-08
ADAM_WD = 0.01
ADAM_STEP = 10
VMEM_LIMIT_BYTES = 52 * 1024 * 1024
TM_FFN = 512
TM_MIX = 256
TT_WGRAD = 2048
STRIP = 16
SLOTS_PER_STEP = 2
SUBLANES = 8
RELAY_AT_EIGHTHS = 7

HBM = pl.BlockSpec(memory_space=pl.ANY)


def _dot(a, b):
    return jnp.dot(a, b, preferred_element_type=F32)


def _dot_nt(a, b):
    return lax.dot_general(a, b, (((1,), (1,)), ((), ())), preferred_element_type=F32)


def _dot_tn(a, b):
    return lax.dot_general(a, b, (((0,), (0,)), ((), ())), preferred_element_type=F32)


def _sds(shape, dtype):
    return jax.ShapeDtypeStruct(shape, dtype)


def _rms_stats(xv):
    r = lax.rsqrt(jnp.mean(xv * xv, axis=-1, keepdims=True) + RMS_EPS)
    return r, xv * r


def _swiglu_saved(gate, up):
    sig = jax.nn.sigmoid(gate)
    silu = gate * sig
    return silu, up * (sig * (1.0 + gate * (1.0 - sig))), silu * up


def _rms_bwd(dh, n, r, gain):
    dn = dh * gain
    return r * (dn - n * jnp.mean(dn * n, axis=-1, keepdims=True))


def _place():
    x, y, c = lax.axis_index("x"), lax.axis_index("y"), lax.axis_index("c")
    return x, y, c, [(1 - x, y), (x, 1 - y), (1 - x, 1 - y)]


class Cargo:
    def __init__(self, kind, arrays):
        self.kind, self.arrays = kind, list(arrays)
        n = len(self.arrays)
        self.two_level = kind in ("gather_slots", "gather_chips")
        if self.two_level:
            self.out_shape = [_sds((N_CHIPS,) + a.shape, a.dtype) for a in self.arrays]
        elif kind == "gather_devices":
            self.out_shape = [_sds((N_DEV,) + a.shape, a.dtype) for a in self.arrays]
        else:
            self.out_shape = [_sds(a.shape, a.dtype) for a in self.arrays]
        n_remote = n * {"swap": 1, "gather_devices": N_DEV - 1}.get(kind, N_CHIPS - 1)
        n_own = 0 if kind == "swap" else n
        n_relay = n_remote if self.two_level else 0
        dma = pltpu.SemaphoreType.DMA
        self.scratch = [dma((n_remote,)), dma((n_remote,)), dma((max(n_own, 1),)),
                        dma((max(n_relay, 1),)), dma((max(n_relay, 1),))]

    def _plan(self, ins, outs):
        x, y, c, chips = _place()
        q = 2 * x + y
        sibling = (x, y, 1 - c)
        own, remote, relays = [], [], []
        for a, o in zip(ins, outs):
            if self.two_level:
                half = a.shape[0] // 2
                mine = pl.ds(pl.multiple_of(c * half, SUBLANES), half)
                theirs = pl.ds(pl.multiple_of((1 - c) * half, SUBLANES), half)
                own.append((a, o.at[0 if self.kind == "gather_slots" else q]))
                for j, (px, py) in enumerate(chips):
                    there, here = (j + 1, j + 1) if self.kind == "gather_slots" else (q, 2 * px + py)
                    remote.append((a.at[mine], o.at[there, mine], o.at[here, mine], (px, py, c)))
                    relays.append((o.at[here, mine], o.at[here, mine], o.at[here, theirs], sibling))
            elif self.kind == "scatter_chips":
                own.append((a.at[q], o.at[q]))
                remote += [(a.at[2 * px + py], o.at[q], o.at[2 * px + py], (px, py, c)) for px, py in chips]
            elif self.kind == "swap":
                remote.append((a, o, o, sibling))
            else:
                own.append((a, o.at[4 * x + 2 * y + c]))
                for k in range(1, N_DEV):
                    px, py, pc = x ^ (k >> 2 & 1), y ^ (k >> 1 & 1), c ^ (k & 1)
                    remote.append((a, o.at[4 * x + 2 * y + c], o.at[4 * px + 2 * py + pc], (px, py, pc)))
        return own, remote, relays

    @staticmethod
    def _copies(entries, send_sems, recv_sems):
        out = []
        for k, (src, dst, landed, peer) in enumerate(entries):
            def make(dst_ref, k=k, src=src, peer=peer):
                return pltpu.make_async_remote_copy(src_ref=src, dst_ref=dst_ref, send_sem=send_sems.at[k],
                                                    recv_sem=recv_sems.at[k], device_id=peer, device_id_type=MESH)
            out.append((make(dst), make(landed)))
        return out

    def start(self, ins, outs, sems):
        own, remote, _ = self._plan(ins, outs)
        for k, (src, dst) in enumerate(own):
            pltpu.make_async_copy(src, dst, sems[2].at[k]).start()
        for mine, _ in self._copies(remote, sems[0], sems[1]):
            mine.start()

    def relay(self, ins, outs, sems):
        _, remote, relays = self._plan(ins, outs)
        passed = self._copies(relays, sems[3], sems[4])
        for (_, arriving), (mine, _) in zip(self._copies(remote, sems[0], sems[1]), passed):
            arriving.wait_recv()
            mine.start()

    def wait(self, ins, outs, sems):
        own, remote, relays = self._plan(ins, outs)
        for mine, arriving in self._copies(remote, sems[0], sems[1]):
            mine.wait_send()
            if not self.two_level:
                arriving.wait_recv()
        for mine, arriving in self._copies(relays, sems[3], sems[4]):
            mine.wait_send()
            arriving.wait_recv()
        for k, (src, dst) in enumerate(own):
            pltpu.make_async_copy(src, dst, sems[2].at[k]).wait()


N_CARGO_SEMS = 5


def _call(body, *, name, grid, in_specs, out_specs, out_shape, args, scratch_shapes=(), cargos=()):
    n_in, n_out, n_scr = len(in_specs), len(out_specs), len(scratch_shapes)
    c_in = [len(cg.arrays) for cg in cargos]
    n_cin = sum(c_in)

    def wrapped(*refs):
        ins = refs[:n_in]
        cins = refs[n_in:n_in + n_cin]
        outs = refs[n_in + n_cin:n_in + n_cin + n_out]
        couts = refs[n_in + n_cin + n_out:n_in + 2 * n_cin + n_out]
        scr = refs[n_in + 2 * n_cin + n_out:n_in + 2 * n_cin + n_out + n_scr]
        sems = refs[n_in + 2 * n_cin + n_out + n_scr:]
        step, n_steps = 0, 1
        for ax, size in enumerate(grid):
            step = step * size + pl.program_id(ax)
            n_steps *= size

        def each(method, only_two_level=False):
            at = 0
            for k, cg in enumerate(cargos):
                if cg.two_level or not only_two_level:
                    getattr(cg, method)(cins[at:at + c_in[k]], couts[at:at + c_in[k]],
                                        sems[N_CARGO_SEMS * k:N_CARGO_SEMS * (k + 1)])
                at += c_in[k]

        body(*ins, *outs, *scr)
        if cargos:
            pl.when(step == 0)(lambda: each("start"))
        if any(cg.two_level for cg in cargos):
            pl.when(step == (RELAY_AT_EIGHTHS * n_steps) // 8)(lambda: each("relay", only_two_level=True))
        if cargos:
            pl.when(step == n_steps - 1)(lambda: each("wait"))

    res = pl.pallas_call(
        wrapped, name=name, grid=grid,
        in_specs=list(in_specs) + [HBM] * n_cin,
        out_specs=list(out_specs) + [HBM] * n_cin,
        out_shape=list(out_shape) + [s for cg in cargos for s in cg.out_shape],
        scratch_shapes=list(scratch_shapes) + [s for cg in cargos for s in cg.scratch],
        compiler_params=pltpu.CompilerParams(dimension_semantics=("arbitrary",) * len(grid),
                                             vmem_limit_bytes=VMEM_LIMIT_BYTES),
    )(*args, *[a for cg in cargos for a in cg.arrays])
    outs, rest = list(res[:n_out]), list(res[n_out:])
    cargo_outs = []
    for k in c_in:
        cargo_outs.append(rest[:k])
        rest = rest[k:]
    return outs, cargo_outs


def _exchange(cargo, name):
    _, (outs,) = _call(lambda: None, name=name, grid=(1,), in_specs=[], out_specs=[], out_shape=[], args=[],
                       cargos=[cargo])
    return outs


def _ffn_up_gather(x, gain, wg_t, wu_t, name, cargos=()):
    t_len, d = x.shape
    fq = wg_t.shape[0]
    tm = min(TM_FFN, t_len)
    n_tiles = t_len // tm
    relay_tile = n_tiles // 2
    fetch_tile = min(relay_tile + 1, n_tiles - 1)

    def body(x_ref, g_ref, wg_in, wu_in, h_ref, s_ref, p_ref, a_ref, wg_all, wu_all,
             wg_v, wu_v, h_all, send_sems, recv_sems, pass_send_sems, pass_recv_sems, own_sems, load_sems):
        s = pl.program_id(0)
        i = pl.program_id(1)
        x_, y_, c_, chips = _place()
        shards = ((wg_in, wg_all, wg_v), (wu_in, wu_all, wu_v))
        mine = pl.ds(pl.multiple_of(c_ * (fq // 2), SUBLANES), fq // 2)
        theirs = pl.ds(pl.multiple_of((1 - c_) * (fq // 2), SUBLANES), fq // 2)

        def to_peer(k, j):
            w_in, w_all, _ = shards[k]
            return pltpu.make_async_remote_copy(
                src_ref=w_in.at[mine], dst_ref=w_all.at[j + 1, mine], send_sem=send_sems.at[3 * k + j],
                recv_sem=recv_sems.at[3 * k + j], device_id=(*chips[j], c_), device_id_type=MESH)

        def to_sibling(k, j, landing=False):
            w_all = shards[k][1]
            return pltpu.make_async_remote_copy(
                src_ref=w_all.at[j + 1, mine], dst_ref=w_all.at[j + 1, theirs if landing else mine],
                send_sem=pass_send_sems.at[3 * k + j], recv_sem=pass_recv_sems.at[3 * k + j],
                device_id=(x_, y_, 1 - c_), device_id_type=MESH)

        def keep(k):
            return pltpu.make_async_copy(shards[k][0], shards[k][1].at[0], own_sems.at[k])

        @pl.when((s == 0) & (i == 0))
        def _():
            for j in range(N_CHIPS - 1):
                for k in range(2):
                    to_peer(k, j).start()
            for k in range(2):
                keep(k).start()

        def load(k, slot):
            src = shards[k][0] if slot == 0 else shards[k][1].at[slot]
            return pltpu.make_async_copy(src, shards[k][2].at[slot % 2], load_sems.at[k])

        @pl.when((s == 0) & (i == 0))
        def _():
            for k in range(2):
                load(k, 0).start()
            for k in range(2):
                load(k, 0).wait()

        def pass_on(slot):
            for k in range(2):
                to_peer(k, slot - 1).wait_recv()
                to_sibling(k, slot - 1).start()

        def fetch(slot):
            for k in range(2):
                to_sibling(k, slot - 1, landing=True).wait_recv()
                load(k, slot).start()

        for slot in range(1, N_CHIPS):
            pl.when((s == slot - 1) & (i == relay_tile))(functools.partial(pass_on, slot))
            pl.when((s == slot - 1) & (i == fetch_tile))(functools.partial(fetch, slot))

            @pl.when((s == slot) & (i == 0))
            def _():
                for k in range(2):
                    load(k, slot).wait()

        @pl.when(s == 0)
        def _():
            _, n = _rms_stats(x_ref[...])
            h_new = (n * g_ref[...]).astype(BF16)
            h_ref[...] = h_new
            h_all[i] = h_new

        h = h_all[i]
        silu, dgate, act = _swiglu_saved(_dot_nt(h, wg_v[s % 2]), _dot_nt(h, wu_v[s % 2]))
        s_ref[...] = silu.astype(BF16)
        p_ref[...] = dgate.astype(BF16)
        a_ref[...] = act.astype(BF16)

        @pl.when((s == N_CHIPS - 1) & (i == n_tiles - 1))
        def _():
            for k in range(2):
                for j in range(N_CHIPS - 1):
                    to_peer(k, j).wait_send()
                    to_sibling(k, j).wait_send()
                keep(k).wait()

    tok = pl.BlockSpec((tm, d), lambda s, i: (jnp.where(s == 0, i, n_tiles - 1), 0))
    hid = pl.BlockSpec((None, tm, fq), lambda s, i: (s, i, 0))
    outs, cargo_outs = _call(
        body, name=name, grid=(N_CHIPS, n_tiles),
        in_specs=[tok, pl.BlockSpec((1, d), lambda s, i: (0, 0)), HBM, HBM],
        out_specs=[tok, hid, hid, hid, HBM, HBM],
        out_shape=[_sds((t_len, d), BF16)] + [_sds((N_CHIPS, t_len, fq), BF16)] * 3
        + [_sds((N_CHIPS, fq, d), BF16)] * 2,
        scratch_shapes=[pltpu.VMEM((2, fq, d), BF16), pltpu.VMEM((2, fq, d), BF16),
                        pltpu.VMEM((n_tiles, tm, d), BF16)]
        + [pltpu.SemaphoreType.DMA((6,))] * 4 + [pltpu.SemaphoreType.DMA((2,))] * 2,
        args=[x, gain, wg_t, wu_t], cargos=cargos)
    return outs, cargo_outs


def _load_once(hbm_refs, vmem_refs, sems, first):
    @pl.when(first)
    def _():
        copies = [pltpu.make_async_copy(src, dst, sems.at[k]) for k, (src, dst) in enumerate(zip(hbm_refs, vmem_refs))]
        for cp in copies:
            cp.start()
        for cp in copies:
            cp.wait()


def _ffn_down(x, act, wd, name, cargos=()):
    t_len, d = x.shape
    nq, fq, _ = wd.shape
    tm = min(TM_FFN, t_len)

    def body(x_ref, a_ref, wd_ref, xo_ref):
        y = _dot(a_ref[0], wd_ref[0])
        for j in range(1, nq):
            y = y + _dot(a_ref[j], wd_ref[j])
        xo_ref[...] = x_ref[...] + FFN_RES_WEIGHT * y

    tok = pl.BlockSpec((tm, d), lambda i: (i, 0))
    (xo,), cargo_outs = _call(
        body, name=name, grid=(t_len // tm,),
        in_specs=[tok, pl.BlockSpec((nq, tm, fq), lambda i: (0, i, 0)), pl.BlockSpec((nq, fq, d), lambda i: (0, 0, 0))],
        out_specs=[tok], out_shape=[_sds((t_len, d), F32)], args=[x, act, wd], cargos=cargos)
    return xo, cargo_outs


def _ffn_fwd(x, gain, wg_t, wu_t, wd, name):
    t_len, d = x.shape
    nq, fq, _ = wd.shape
    tm = min(TM_FFN, t_len)

    def body(x_ref, g_ref, wg_hbm, wu_hbm, wd_hbm, xo_ref, h_ref, s_ref, p_ref, a_ref,
             h_s, acc, wg_v, wu_v, wd_v, load_sems):
        i = pl.program_id(0)
        j = pl.program_id(1)
        _load_once((wg_hbm, wu_hbm, wd_hbm), (wg_v, wu_v, wd_v), load_sems, (i == 0) & (j == 0))

        @pl.when(j == 0)
        def _():
            _, n = _rms_stats(x_ref[...])
            h = (n * g_ref[...]).astype(BF16)
            h_s[...] = h
            h_ref[...] = h
            acc[...] = jnp.zeros_like(acc)

        h = h_s[...]
        y = None
        for jj in range(SLOTS_PER_STEP):
            slot = j * SLOTS_PER_STEP + jj
            silu, dgate, act = _swiglu_saved(_dot_nt(h, wg_v[slot]), _dot_nt(h, wu_v[slot]))
            s_ref[jj] = silu.astype(BF16)
            p_ref[jj] = dgate.astype(BF16)
            a_ref[jj] = act.astype(BF16)
            part = _dot(a_ref[jj], wd_v[slot])
            y = part if y is None else y + part
        acc[...] += y

        @pl.when(j == nq // SLOTS_PER_STEP - 1)
        def _():
            xo_ref[...] = x_ref[...] + FFN_RES_WEIGHT * acc[...]

    tok = pl.BlockSpec((tm, d), lambda i, j: (i, 0))
    hid = pl.BlockSpec((SLOTS_PER_STEP, tm, fq), lambda i, j: (j, i, 0))
    outs, _ = _call(
        body, name=name, grid=(t_len // tm, nq // SLOTS_PER_STEP),
        in_specs=[tok, pl.BlockSpec((1, d), lambda i, j: (0, 0)), HBM, HBM, HBM],
        out_specs=[tok, tok, hid, hid, hid],
        out_shape=[_sds((t_len, d), F32), _sds((t_len, d), BF16)] + [_sds((nq, t_len, fq), BF16)] * 3,
        scratch_shapes=[pltpu.VMEM((tm, d), BF16), pltpu.VMEM((tm, d), F32)]
        + [pltpu.VMEM((nq, fq, d), BF16)] * 3 + [pltpu.SemaphoreType.DMA((3,))],
        args=[x, gain, wg_t, wu_t, wd])
    return outs


def _ffn_bwd(dy, x_in, gain, silu, dgate_du, wg_t, wu_t, wd, name):
    t_len, d = dy.shape
    nq, fq, _ = wd.shape
    tm = min(TM_FFN, t_len)

    def body(dy_ref, x_ref, g_ref, s_ref, p_ref, wg_hbm, wu_hbm, wd_hbm,
             dx_ref, dgain_ref, df_ref, dg_ref, du_ref, df_s, dh_acc, dact_s, wg_v, wu_v, wd_v, load_sems):
        i = pl.program_id(0)
        j = pl.program_id(1)
        _load_once((wg_hbm, wu_hbm, wd_hbm), (wg_v, wu_v, wd_v), load_sems, (i == 0) & (j == 0))

        @pl.when((i == 0) & (j == 0))
        def _():
            dgain_ref[...] = jnp.zeros_like(dgain_ref)

        @pl.when(j == 0)
        def _():
            df = (FFN_RES_WEIGHT * dy_ref[...]).astype(BF16)
            df_s[...] = df
            df_ref[...] = df
            dh_acc[...] = jnp.zeros_like(dh_acc)

        half = tm // 2
        for r0 in (0, half):
            dact_s[r0:r0 + half, :] = _dot_nt(df_s[r0:r0 + half, :], wd_v[j])

        for r0 in range(0, tm, STRIP):
            dact = dact_s[r0:r0 + STRIP, :]
            dg_ref[r0:r0 + STRIP, :] = (dact * p_ref[r0:r0 + STRIP, :].astype(F32)).astype(BF16)
            du_ref[r0:r0 + STRIP, :] = (dact * s_ref[r0:r0 + STRIP, :].astype(F32)).astype(BF16)

        for r0 in (0, half):
            rows = slice(r0, r0 + half)
            dh_acc[rows, :] += _dot(dg_ref[rows, :], wg_v[j]) + _dot(du_ref[rows, :], wu_v[j])

        @pl.when(j == nq - 1)
        def _():
            r, n = _rms_stats(x_ref[...])
            dh = dh_acc[...]
            dgain_ref[...] += jnp.sum(dh * n, axis=0, keepdims=True)
            dx_ref[...] = dy_ref[...] + _rms_bwd(dh, n, r, g_ref[...])

    tok = pl.BlockSpec((tm, d), lambda i, j: (i, 0))
    vec = pl.BlockSpec((1, d), lambda i, j: (0, 0))
    hid = pl.BlockSpec((None, tm, fq), lambda i, j: (j, i, 0))
    outs, _ = _call(
        body, name=name, grid=(t_len // tm, nq),
        in_specs=[tok, tok, vec, hid, hid, HBM, HBM, HBM],
        out_specs=[tok, vec, tok, hid, hid],
        out_shape=[_sds((t_len, d), F32), _sds((1, d), F32), _sds((t_len, d), BF16),
                   _sds((nq, t_len, fq), BF16), _sds((nq, t_len, fq), BF16)],
        scratch_shapes=[pltpu.VMEM((tm, d), BF16), pltpu.VMEM((tm, d), F32), pltpu.VMEM((tm, fq), F32)]
        + [pltpu.VMEM((nq, fq, d), BF16)] * 3 + [pltpu.SemaphoreType.DMA((3,))],
        args=[dy, x_in, gain, silu, dgate_du, wg_t, wu_t, wd])
    return outs


def _wgrad(lhs, rhs, l_spec, r_spec, out_shape, out_spec, acc_shape, grid, name, cargos=()):
    n_t = grid[-1]
    t_axis = len(grid) - 1

    def body(l_ref, r_ref, o_ref, acc):
        t = pl.program_id(t_axis)

        @pl.when(t == 0)
        def _():
            acc[...] = jnp.zeros_like(acc)

        acc[...] += _dot_tn(l_ref[...].astype(BF16), r_ref[...].astype(BF16))

        @pl.when(t == n_t - 1)
        def _():
            o_ref[...] = acc[...].astype(o_ref.dtype)

    (out,), cargo_outs = _call(
        body, name=name, grid=grid, in_specs=[l_spec, r_spec], out_specs=[out_spec], out_shape=[out_shape],
        scratch_shapes=[pltpu.VMEM(acc_shape, F32)], args=[lhs, rhs], cargos=cargos)
    return out, cargo_outs


def _wgrad_hid_tok_scatter(hid, tok, name, cargos=()):
    t_len, d = tok.shape
    nq, _, fq = hid.shape
    half = fq // 2
    tt = min(TT_WGRAD, t_len)
    n_t = t_len // tt

    def body(l_ref, r_ref, parts_ref, acc, stage, pair, summed, zeros,
             pair_send_sems, pair_recv_sems, send_sems, recv_sems, own_sem, zero_sems):
        g = pl.program_id(0)
        t = pl.program_id(1)
        x_, y_, c_, chips = _place()
        mine = pl.ds(pl.multiple_of(c_ * half, STRIP), half)
        theirs = pl.ds(pl.multiple_of((1 - c_) * half, STRIP), half)

        def to_sibling(slot):
            return pltpu.make_async_remote_copy(
                src_ref=stage.at[slot, theirs], dst_ref=pair.at[slot], send_sem=pair_send_sems.at[slot],
                recv_sem=pair_recv_sems.at[slot], device_id=(x_, y_, 1 - c_), device_id_type=MESH)

        def to_peer(j):
            return pltpu.make_async_remote_copy(
                src_ref=summed.at[j + 1], dst_ref=parts_ref.at[j + 1, mine], send_sem=send_sems.at[j],
                recv_sem=recv_sems.at[j], device_id=(*chips[j], c_), device_id_type=MESH)

        keep = pltpu.make_async_copy(summed.at[0], parts_ref.at[0, mine], own_sem)

        def blank(slot):
            return pltpu.make_async_copy(zeros, parts_ref.at[slot, theirs], zero_sems.at[slot])

        @pl.when((g == 0) & (t == 0))
        def _():
            zeros[...] = jnp.zeros_like(zeros)
            for slot in range(nq):
                blank(slot).start()

        @pl.when(t == 0)
        def _():
            acc[...] = jnp.zeros_like(acc)

        acc[...] += _dot_tn(l_ref[...], r_ref[...])

        def send_on(slot):
            to_sibling(slot).wait_recv()
            summed[slot] = (stage[slot, mine, :].astype(F32) + pair[slot].astype(F32)).astype(BF16)
            if slot > 0:
                to_peer(slot - 1).start()
            else:
                keep.start()

        for step in range(nq):
            slot = (step + 1) % nq

            @pl.when((g == step) & (t == n_t - 1))
            def _():
                stage[slot] = acc[...].astype(BF16)
                to_sibling(slot).start()
                if step == nq - 1:
                    send_on(slot)

            if step < nq - 1:
                pl.when((g == step + 1) & (t == min(1, n_t - 1)))(functools.partial(send_on, slot))

        @pl.when((g == nq - 1) & (t == n_t - 1))
        def _():
            for j in range(N_CHIPS - 1):
                to_peer(j).wait()
            keep.wait()
            for slot in range(nq):
                to_sibling(slot).wait_send()
                blank(slot).wait()

    (parts,), cargo_outs = _call(
        body, name=name, grid=(nq, n_t),
        in_specs=[pl.BlockSpec((None, tt, fq), lambda g, t: ((g + 1) % nq, t, 0)),
                  pl.BlockSpec((tt, d), lambda g, t: (t, 0))],
        out_specs=[HBM], out_shape=[_sds((nq, fq, d), BF16)],
        scratch_shapes=[pltpu.VMEM((fq, d), F32), pltpu.VMEM((nq, fq, d), BF16), pltpu.VMEM((nq, half, d), BF16),
                        pltpu.VMEM((nq, half, d), BF16), pltpu.VMEM((half, d), BF16),
                        pltpu.SemaphoreType.DMA((nq,)), pltpu.SemaphoreType.DMA((nq,)),
                        pltpu.SemaphoreType.DMA((N_CHIPS - 1,)), pltpu.SemaphoreType.DMA((N_CHIPS - 1,)),
                        pltpu.SemaphoreType.DMA(()), pltpu.SemaphoreType.DMA((nq,))],
        args=[hid, tok], cargos=cargos)
    return parts, cargo_outs


def _wgrad_2d(lhs, rhs, n_col_blocks, out_dtype, name, group_diag=False, cargos=()):
    t_len, k = lhs.shape
    n = rhs.shape[1]
    nb = n // n_col_blocks
    kb = k // n_col_blocks if group_diag else k
    tt = min(TT_WGRAD, t_len)
    l_map = (lambda q, t: (t, q)) if group_diag else (lambda q, t: (t, 0))
    return _wgrad(lhs, rhs,
                  pl.BlockSpec((tt, kb), l_map),
                  pl.BlockSpec((tt, nb), lambda q, t: (t, q)),
                  _sds((n_col_blocks, kb, nb), out_dtype),
                  pl.BlockSpec((None, kb, nb), lambda q, t: (q, 0, 0)),
                  (kb, nb), (n_col_blocks, t_len // tt), name, cargos)


def _layernorm_stats(u1):
    mu = jnp.mean(u1, axis=-1, keepdims=True)
    xc = u1 - mu
    rstd = lax.rsqrt(jnp.mean(xc * xc, axis=-1, keepdims=True) + LN_EPS)
    return rstd, xc * rstd


def _positions(i, tm, rows, offset=0):
    return (lax.broadcasted_iota(jnp.int32, (rows, 1), 0) + (i * tm + offset)).astype(F32)


SHIFT_ROWS = HALO - SUBLANES


def _fill_shifted(ext_s, sh_s, tm):
    for b in range(1, SUBLANES):
        sh_s[b - 1] = ext_s[pl.ds(b, tm + SHIFT_ROWS), :]


def _window(ext_s, sh_s, shift, tm):
    a, b = divmod(shift, SUBLANES)
    if b == 0:
        return ext_s[pl.ds(shift, tm), :]
    return sh_s[b - 1, pl.ds(a * SUBLANES, tm), :]


def _tile(tm, cols):
    return pl.BlockSpec((tm, cols), lambda i: (i, 0))


def _whole(shape):
    return pl.BlockSpec(shape, lambda i: (0,) * len(shape))


def _mix_fwd(x1, gain, w_in, conv_dw, conv_b, ln_g, ln_b, conv_pw, pool_w, pool_scale, w_out, name, cargos=()):
    t_len, d = x1.shape
    nq, _, nb = w_in.shape
    tm = min(TM_MIX, t_len)

    def body(x_ref, g_ref, wi_ref, dw_ref, cb_ref, lg_ref, lb_ref, pw_ref, plw_ref, ps_ref, wo_ref,
             x2_ref, h_ref, p_ref, u1_ref, u3_ref, mx_ref, cat_ref, ext_s, pext_s, sh_s, tail_s):
        i = pl.program_id(0)

        @pl.when(i == 0)
        def _():
            tail_s[...] = jnp.zeros_like(tail_s)

        _, n = _rms_stats(x_ref[...])
        h = (n * g_ref[...]).astype(BF16)
        h_ref[...] = h
        for q in range(nq):
            p_ref[:, q * nb:(q + 1) * nb] = _dot(h, wi_ref[q])

        a = p_ref[:, 0:D_CONV]
        g = p_ref[:, D_CONV:2 * D_CONV]
        p = p_ref[:, 2 * D_CONV:]
        ext_s[0:HALO, :] = tail_s[:, 0:D_CONV] * jax.nn.sigmoid(tail_s[:, D_CONV:2 * D_CONV])
        ext_s[HALO:, :] = a * jax.nn.sigmoid(g)
        pext_s[0:HALO, :] = tail_s[:, 2 * D_CONV:]
        pext_s[HALO:, :] = p
        tail_s[...] = p_ref[tm - HALO:tm, :]

        _fill_shifted(ext_s, sh_s, tm)
        u1 = jnp.broadcast_to(cb_ref[...], (tm, D_CONV))
        for k in range(CONV_WIDTH):
            u1 = u1 + dw_ref[k:k + 1, :] * _window(ext_s, sh_s, HALO - (CONV_WIDTH - 1) + k, tm)
        u1_ref[...] = u1
        _, nhat = _layernorm_stats(u1)
        u2 = nhat * lg_ref[...] + lb_ref[...]
        u3 = (u2 * jax.nn.sigmoid(u2)).astype(BF16)
        u3_ref[...] = u3
        cat_ref[:, 0:D_CONV] = _dot(u3, pw_ref[...]).astype(BF16)

        pos1 = _positions(i, tm, tm) + 1.0
        for gi, w in enumerate(POOL_WINDOWS):
            cols = slice(gi * POOL_GROUP, (gi + 1) * POOL_GROUP)
            s = pext_s[pl.ds(HALO, tm), cols]
            for j in range(1, w):
                s = s + pext_s[pl.ds(HALO - j, tm), cols]
            mixed = (s / jnp.minimum(pos1, float(w)) - p[:, cols]).astype(BF16)
            mx_ref[:, cols] = mixed
            out = _dot(mixed, plw_ref[gi]) * ps_ref[:, cols]
            cat_ref[:, D_CONV + gi * POOL_GROUP:D_CONV + (gi + 1) * POOL_GROUP] = out.astype(BF16)

        x2_ref[...] = x_ref[...] + _dot(cat_ref[...], wo_ref[...])

    return _call(
        body, name=name, grid=(t_len // tm,),
        in_specs=[_tile(tm, d), _whole((1, d)), _whole((nq, d, nb)), _whole((CONV_WIDTH + 1, D_CONV)),
                  _whole((1, D_CONV)), _whole((1, D_CONV)), _whole((1, D_CONV)), _whole((D_CONV, D_CONV)),
                  _whole((4, POOL_GROUP, POOL_GROUP)), _whole((1, D_POOL)), _whole((D_CONV + D_POOL, d))],
        out_specs=[_tile(tm, d), _tile(tm, d), _tile(tm, D_IN), _tile(tm, D_CONV), _tile(tm, D_CONV),
                   _tile(tm, D_POOL), _tile(tm, D_CONV + D_POOL)],
        out_shape=[_sds((t_len, d), F32), _sds((t_len, d), BF16), _sds((t_len, D_IN), F32),
                   _sds((t_len, D_CONV), F32), _sds((t_len, D_CONV), BF16), _sds((t_len, D_POOL), BF16),
                   _sds((t_len, D_CONV + D_POOL), BF16)],
        scratch_shapes=[pltpu.VMEM((tm + HALO, D_CONV), F32), pltpu.VMEM((tm + HALO, D_POOL), F32),
                        pltpu.VMEM((SUBLANES - 1, tm + SHIFT_ROWS, D_CONV), F32), pltpu.VMEM((HALO, D_IN), F32)],
        args=[x1, gain, w_in, conv_dw, conv_b, ln_g, ln_b, conv_pw, pool_w, pool_scale, w_out], cargos=cargos)


def _mix_bwd_local(dx2, u1, mixed, ln_g, ln_b, conv_pw, pool_w, pool_scale, w_out, name, cargos=()):
    t_len, d = dx2.shape
    tm = min(TM_MIX, t_len)

    def body(dx_ref, u1_ref, mx_ref, lg_ref, lb_ref, pw_ref, plw_ref, ps_ref, wo_ref,
             du1_ref, dmx_ref, dco_ref, dpo_ref, dlg_ref, dlb_ref, dps_ref):
        @pl.when(pl.program_id(0) == 0)
        def _():
            dlg_ref[...] = jnp.zeros_like(dlg_ref)
            dlb_ref[...] = jnp.zeros_like(dlb_ref)
            dps_ref[...] = jnp.zeros_like(dps_ref)

        dcat = _dot_nt(dx_ref[...].astype(BF16), wo_ref[...])
        dco = dcat[:, 0:D_CONV].astype(BF16)
        dco_ref[...] = dco
        du3 = _dot_nt(dco, pw_ref[...])
        rstd, nhat = _layernorm_stats(u1_ref[...])
        u2 = nhat * lg_ref[...] + lb_ref[...]
        sig = jax.nn.sigmoid(u2)
        du2 = du3 * (sig * (1.0 + u2 * (1.0 - sig)))
        dlg_ref[...] += jnp.sum(du2 * nhat, axis=0, keepdims=True)
        dlb_ref[...] += jnp.sum(du2, axis=0, keepdims=True)
        dnhat = du2 * lg_ref[...]
        du1_ref[...] = rstd * (dnhat - jnp.mean(dnhat, axis=-1, keepdims=True)
                               - nhat * jnp.mean(dnhat * nhat, axis=-1, keepdims=True))

        for gi in range(len(POOL_WINDOWS)):
            cols = slice(gi * POOL_GROUP, (gi + 1) * POOL_GROUP)
            dpo = dcat[:, D_CONV + gi * POOL_GROUP:D_CONV + (gi + 1) * POOL_GROUP]
            pre = _dot(mx_ref[:, cols], plw_ref[gi])
            dps_ref[:, cols] += jnp.sum(dpo * pre, axis=0, keepdims=True)
            dout = (dpo * ps_ref[:, cols]).astype(BF16)
            dpo_ref[:, cols] = dout
            dmx_ref[:, cols] = _dot_nt(dout, plw_ref[gi])

    vec = _whole((1, D_CONV))
    return _call(
        body, name=name, grid=(t_len // tm,),
        in_specs=[_tile(tm, d), _tile(tm, D_CONV), _tile(tm, D_POOL), vec, vec, _whole((D_CONV, D_CONV)),
                  _whole((4, POOL_GROUP, POOL_GROUP)), vec, _whole((D_CONV + D_POOL, d))],
        out_specs=[_tile(tm, D_CONV), _tile(tm, D_POOL), _tile(tm, D_CONV), _tile(tm, D_POOL), vec, vec, vec],
        out_shape=[_sds((t_len, D_CONV), F32), _sds((t_len, D_POOL), F32), _sds((t_len, D_CONV), BF16),
                   _sds((t_len, D_POOL), BF16), _sds((1, D_CONV), F32), _sds((1, D_CONV), F32),
                   _sds((1, D_POOL), F32)],
        args=[dx2, u1, mixed, ln_g, ln_b, conv_pw, pool_w, pool_scale, w_out], cargos=cargos)


def _mix_bwd_seq(du1, dmixed, proj, x1, dx2, gain, conv_dw, w_in, name, cargos=()):
    t_len, d = x1.shape
    nq, _, nb = w_in.shape
    tm = min(TM_MIX, t_len)
    hb = tm // HALO
    last_block = t_len // HALO - 1
    n_tiles = t_len // tm

    def body(du_ref, dun_ref, dm_ref, dmn_ref, p_ref, tail_ref, x_ref, dx2_ref, g_ref, dw_ref, wi_ref,
             dx1_ref, dp_ref, ddw_ref, dcb_ref, dgain_ref, uext_s, dext_s, mext_s, ush_s, dsh_s):
        i = pl.program_id(0)
        first = i == 0
        last = i == n_tiles - 1

        @pl.when(first)
        def _():
            ddw_ref[...] = jnp.zeros_like(ddw_ref)
            dcb_ref[...] = jnp.zeros_like(dcb_ref)
            dgain_ref[...] = jnp.zeros_like(dgain_ref)

        a = p_ref[:, 0:D_CONV]
        g = p_ref[:, D_CONV:2 * D_CONV]
        sg = jax.nn.sigmoid(g)
        ta = tail_ref[:, 0:D_CONV]
        tg = tail_ref[:, D_CONV:2 * D_CONV]
        uext_s[0:HALO, :] = jnp.where(first, 0.0, ta * jax.nn.sigmoid(tg))
        uext_s[HALO:, :] = a * sg
        du1 = du_ref[...]
        dext_s[0:tm, :] = du1
        dext_s[tm:, :] = jnp.where(last, 0.0, dun_ref[...])

        _fill_shifted(uext_s, ush_s, tm)
        _fill_shifted(dext_s, dsh_s, tm)
        du0 = jnp.zeros((tm, D_CONV), F32)
        for k in range(CONV_WIDTH):
            du0 = du0 + dw_ref[k:k + 1, :] * _window(dext_s, dsh_s, CONV_WIDTH - 1 - k, tm)
            ddw_ref[k:k + 1, :] += jnp.sum(
                du1 * _window(uext_s, ush_s, HALO - (CONV_WIDTH - 1) + k, tm), axis=0, keepdims=True)
        dcb_ref[...] += jnp.sum(du1, axis=0, keepdims=True)
        dp_ref[:, 0:D_CONV] = (du0 * sg).astype(BF16)
        dp_ref[:, D_CONV:2 * D_CONV] = (du0 * a * sg * (1.0 - sg)).astype(BF16)

        pos1 = _positions(i, tm, tm) + 1.0
        pos1_next = _positions(i, tm, HALO, offset=tm) + 1.0
        for gi, w in enumerate(POOL_WINDOWS):
            cols = slice(gi * POOL_GROUP, (gi + 1) * POOL_GROUP)
            dm = dm_ref[:, cols]
            mext_s[0:tm, cols] = dm / jnp.minimum(pos1, float(w))
            mext_s[tm:, cols] = jnp.where(last, 0.0, dmn_ref[:, cols] / jnp.minimum(pos1_next, float(w)))
            s = mext_s[pl.ds(0, tm), cols]
            for j in range(1, w):
                s = s + mext_s[pl.ds(j, tm), cols]
            dp_ref[:, 2 * D_CONV + gi * POOL_GROUP:2 * D_CONV + (gi + 1) * POOL_GROUP] = (s - dm).astype(BF16)

        dh = _dot_nt(dp_ref[:, 0:nb], wi_ref[0])
        for q in range(1, nq):
            dh = dh + _dot_nt(dp_ref[:, q * nb:(q + 1) * nb], wi_ref[q])
        r, n = _rms_stats(x_ref[...])
        dgain_ref[...] += jnp.sum(dh * n, axis=0, keepdims=True)
        dx1_ref[...] = dx2_ref[...] + _rms_bwd(dh, n, r, g_ref[...])

    def nxt(cols):
        return pl.BlockSpec((HALO, cols), lambda i: (jnp.minimum((i + 1) * hb, last_block), 0))

    return _call(
        body, name=name, grid=(n_tiles,),
        in_specs=[_tile(tm, D_CONV), nxt(D_CONV), _tile(tm, D_POOL), nxt(D_POOL), _tile(tm, D_IN),
                  pl.BlockSpec((HALO, D_IN), lambda i: (jnp.maximum(i * hb - 1, 0), 0)),
                  _tile(tm, d), _tile(tm, d), _whole((1, d)), _whole((CONV_WIDTH + 1, D_CONV)),
                  _whole((nq, d, nb))],
        out_specs=[_tile(tm, d), _tile(tm, D_IN), _whole((CONV_WIDTH + 1, D_CONV)), _whole((1, D_CONV)),
                   _whole((1, d))],
        out_shape=[_sds((t_len, d), F32), _sds((t_len, D_IN), BF16), _sds((CONV_WIDTH + 1, D_CONV), F32),
                   _sds((1, D_CONV), F32), _sds((1, d), F32)],
        scratch_shapes=[pltpu.VMEM((tm + HALO, D_CONV), F32), pltpu.VMEM((tm + HALO, D_CONV), F32),
                        pltpu.VMEM((tm + HALO, D_POOL), F32),
                        pltpu.VMEM((SUBLANES - 1, tm + SHIFT_ROWS, D_CONV), F32),
                        pltpu.VMEM((SUBLANES - 1, tm + SHIFT_ROWS, D_CONV), F32)],
        args=[du1, du1, dmixed, dmixed, proj, proj, x1, dx2, gain, conv_dw, w_in], cargos=cargos)


def _final_norm_loss(x3, target, gain, name):
    t_len, d = x3.shape
    tm = min(TM_FFN, t_len)

    def body(x_ref, t_ref, g_ref, dx_ref, loss_ref, dgain_ref):
        @pl.when(pl.program_id(0) == 0)
        def _():
            loss_ref[...] = jnp.zeros_like(loss_ref)
            dgain_ref[...] = jnp.zeros_like(dgain_ref)

        r, n = _rms_stats(x_ref[...])
        err = n * g_ref[...] - t_ref[...]
        per_tok = jnp.sum(err * err, axis=-1, keepdims=True) * (1.0 / d)
        loss_ref[...] += 0.5 * jnp.sum(per_tok, axis=0, keepdims=True)
        dy = err * (1.0 / d)
        dgain_ref[...] += jnp.sum(dy * n, axis=0, keepdims=True)
        dx_ref[...] = _rms_bwd(dy, n, r, g_ref[...])

    tok = pl.BlockSpec((tm, d), lambda i: (i, 0))
    outs, _ = _call(
        body, name=name, grid=(t_len // tm,),
        in_specs=[tok, tok, pl.BlockSpec((1, d), lambda i: (0, 0))],
        out_specs=[tok, pl.BlockSpec((1, 128), lambda i: (0, 0)), pl.BlockSpec((1, d), lambda i: (0, 0))],
        out_shape=[_sds((t_len, d), F32), _sds((1, 128), F32), _sds((1, d), F32)],
        args=[x3, target, gain])
    return outs


def _row_tile(rows):
    return rows // 4 if rows % 64 == 0 else rows


def _sum_parts(parts, name):
    n, r, c = parts.shape
    tr = _row_tile(r)

    def body(p_ref, o_ref):
        s = p_ref[0].astype(F32)
        for k in range(1, n):
            s = s + p_ref[k].astype(F32)
        o_ref[...] = s

    (out,), _ = _call(body, name=name, grid=(r // tr,),
                      in_specs=[pl.BlockSpec((n, tr, c), lambda i: (0, i, 0))],
                      out_specs=[pl.BlockSpec((tr, c), lambda i: (i, 0))], out_shape=[_sds((r, c), F32)],
                      args=[parts])
    return out


def _adamw_math(w, g, m, v):
    m = ADAM_B1 * m + (1.0 - ADAM_B1) * g
    v = ADAM_B2 * v + (1.0 - ADAM_B2) * (g * g)
    m_hat = m / (1.0 - ADAM_B1 ** ADAM_STEP)
    v_hat = v / (1.0 - ADAM_B2 ** ADAM_STEP)
    delta = -ADAM_LR * (m_hat / (jnp.sqrt(v_hat) + ADAM_EPS) + ADAM_WD * w)
    return delta, m, v


def _adamw(parts, w, m, v, name):
    r, c = w.shape
    n = len(parts)
    tr = _row_tile(r)

    def body(*refs):
        terms = []
        for p_ref in refs[:n]:
            terms += [p_ref[...]] if len(p_ref.shape) == 2 else [p_ref[k] for k in range(p_ref.shape[0])]
        w_ref, m_ref, v_ref, g_out, d_out, m_out, v_out = refs[n:]
        g = terms[0]
        for t in terms[1:]:
            g = g + t
        delta, nm, nv = _adamw_math(w_ref[...], g, m_ref[...], v_ref[...])
        g_out[...] = g
        d_out[...] = delta
        m_out[...] = nm
        v_out[...] = nv

    blk = pl.BlockSpec((tr, c), lambda i: (i, 0))
    p_specs = [blk if p.ndim == 2 else pl.BlockSpec((p.shape[0], tr, c), lambda i: (0, i, 0)) for p in parts]
    outs, _ = _call(body, name=name, grid=(r // tr,), in_specs=p_specs + [blk, blk, blk],
                    out_specs=[blk] * 4, out_shape=[_sds((r, c), F32)] * 4, args=[*parts, w, m, v])
    return outs


FFN_W = ("w_gate", "w_up", "w_down")
MID = ("w_in", "conv_dw", "conv_pw", "w_out")
SMALL_1024 = ("ffn1_norm", "mix_norm", "ffn2_norm", "final_norm")
SMALL_512 = ("conv_dw_b", "conv_ln_g", "conv_ln_b", "pool_scale")
WEIGHTS = ("ffn1_norm", "ffn1_w_gate", "ffn1_w_up", "ffn1_w_down", "mix_norm", "w_in", "conv_dw", "conv_dw_b",
           "conv_ln_g", "conv_ln_b", "conv_pw", "pool_w", "pool_scale", "w_out", "ffn2_norm", "ffn2_w_gate",
           "ffn2_w_up", "ffn2_w_down", "final_norm")
PACK_ROWS = 72


def _pad_rows(a, rows):
    return jnp.pad(a, ((0, rows - a.shape[0]), (0, 0)))


def _pack_small(t):
    rows = [t[k].reshape(1, D_MODEL) for k in SMALL_1024]
    rows.append(jnp.concatenate([t["conv_dw_b"].reshape(1, -1), t["conv_ln_g"].reshape(1, -1)], axis=1))
    rows.append(jnp.concatenate([t["conv_ln_b"].reshape(1, -1), t["pool_scale"].reshape(1, -1)], axis=1))
    rows.append(t["pool_w"].reshape(64, D_MODEL))
    return _pad_rows(jnp.concatenate(rows, axis=0), PACK_ROWS)


def _unpack_small(p):
    out = {k: p[i] for i, k in enumerate(SMALL_1024)}
    out["conv_dw_b"], out["conv_ln_g"] = p[4, :D_CONV], p[4, D_CONV:]
    out["conv_ln_b"], out["pool_scale"] = p[5, :D_CONV], p[5, D_CONV:]
    out["pool_w"] = p[6:70].reshape(4, POOL_GROUP, POOL_GROUP)
    return out


def _as_stored(name, a):
    if name.endswith(("w_gate", "w_up")):
        return a.T
    if name == "conv_dw":
        return _pad_rows(a, CONV_WIDTH + 1)
    return a


def _as_given(name, a):
    if name.endswith(("w_gate", "w_up")):
        return a.T
    if name == "conv_dw":
        return a[:CONV_WIDTH]
    return a


def kernel(x, ffn1_norm, ffn1_w_gate, ffn1_w_up, ffn1_w_down, mix_norm, w_in, conv_dw, conv_dw_b, conv_ln_g, conv_ln_b, conv_pw, pool_w, pool_scale, w_out, ffn2_norm, ffn2_w_gate, ffn2_w_up, ffn2_w_down, final_norm, loss_target, m_ffn1_norm, m_ffn1_w_gate, m_ffn1_w_up, m_ffn1_w_down, m_mix_norm, m_w_in, m_conv_dw, m_conv_dw_b, m_conv_ln_g, m_conv_ln_b, m_conv_pw, m_pool_w, m_pool_scale, m_w_out, m_ffn2_norm, m_ffn2_w_gate, m_ffn2_w_up, m_ffn2_w_down, m_final_norm, v_ffn1_norm, v_ffn1_w_gate, v_ffn1_w_up, v_ffn1_w_down, v_mix_norm, v_w_in, v_conv_dw, v_conv_dw_b, v_conv_ln_g, v_conv_ln_b, v_conv_pw, v_pool_w, v_pool_scale, v_w_out, v_ffn2_norm, v_ffn2_w_gate, v_ffn2_w_up, v_ffn2_w_down, v_final_norm):
    given = dict(locals())
    wts = {k: given[k] for k in WEIGHTS}
    mom_m = {k: given["m_" + k] for k in WEIGHTS}
    mom_v = {k: given["v_" + k] for k in WEIGHTS}
    xt, target = x[0], loss_target[0]

    shard = {k: _as_stored(k, wts[k]) if k == "conv_dw" else _as_stored(k, wts[k]).astype(BF16)
             for k in WEIGHTS if k.endswith(FFN_W) or k in MID}
    w = {k: wts[k].reshape(1, -1) for k in SMALL_1024 + SMALL_512}
    w["pool_w"] = wts["pool_w"].astype(BF16)

    (h1, s1, p1, a1, w["ffn1_w_gate"], w["ffn1_w_up"]), ((w["ffn1_w_down"],),) = _ffn_up_gather(
        xt, w["ffn1_norm"], shard["ffn1_w_gate"], shard["ffn1_w_up"], "ffn1_up_gather",
        cargos=[Cargo("gather_slots", [shard["ffn1_w_down"]])])
    x1, (mid, (w["ffn2_w_down"],)) = _ffn_down(
        xt, a1, w["ffn1_w_down"], "ffn1_down",
        cargos=[Cargo("gather_chips", [shard[k] for k in MID]), Cargo("gather_slots", [shard["ffn2_w_down"]])])
    w["w_in"] = mid[0]
    w["conv_dw"] = mid[1].transpose(1, 0, 2).reshape(CONV_WIDTH + 1, D_CONV)
    w["conv_pw"] = mid[2].reshape(D_CONV, D_CONV)
    w["w_out"] = mid[3].reshape(D_CONV + D_POOL, D_MODEL)
    (x2, h2, proj, u1, u3, mixed, cat), ((w["ffn2_w_gate"], w["ffn2_w_up"]),) = _mix_fwd(
        x1, w["mix_norm"], w["w_in"], w["conv_dw"], w["conv_dw_b"], w["conv_ln_g"], w["conv_ln_b"], w["conv_pw"],
        w["pool_w"], w["pool_scale"], w["w_out"], "mix_fwd",
        cargos=[Cargo("gather_slots", [shard["ffn2_w_gate"], shard["ffn2_w_up"]])])
    x3, h3, s2, p2, a2 = _ffn_fwd(x2, w["ffn2_norm"], w["ffn2_w_gate"], w["ffn2_w_up"], w["ffn2_w_down"], "ffn2_fwd")
    dx3, loss, d_final = _final_norm_loss(x3, target, w["final_norm"], "final_norm_loss")
    loss = lax.psum(loss[0, 0], ("x", "y", "c"))

    g = {"final_norm": d_final}
    sums = {}

    def landed(names, parts):
        for k, p in zip(names, parts):
            sums[k] = _sum_parts(p, "sum_chips_" + k)

    dx2, g["ffn2_norm"], df2, dg2, du2 = _ffn_bwd(dx3, x2, w["ffn2_norm"], s2, p2, w["ffn2_w_gate"],
                                                   w["ffn2_w_up"], w["ffn2_w_down"], "ffn2_bwd")
    def ffn_wgrad(name, hid, tok, cargos=()):
        parts, cargo_outs = _wgrad_hid_tok_scatter(hid, tok, name.replace("_w_", "_dw_"), cargos=cargos)
        landed([name], [parts])
        return cargo_outs

    ffn_wgrad("ffn2_w_gate", dg2, h3)
    ffn_wgrad("ffn2_w_up", du2, h3)
    ffn_wgrad("ffn2_w_down", a2, df2)
    (du1, dmixed, dco, dpo, g["conv_ln_g"], g["conv_ln_b"], g["pool_scale"]), (swapped2,) = _mix_bwd_local(
        dx2, u1, mixed, w["conv_ln_g"], w["conv_ln_b"], w["conv_pw"], w["pool_w"], w["pool_scale"], w["w_out"],
        "mix_bwd_local", cargos=[Cargo("swap", [sums["ffn2_" + k] for k in FFN_W])])
    g_out, _ = _wgrad_2d(cat, dx2, 1, BF16, "dw_out")
    g_pw, _ = _wgrad_2d(u3, dco, 1, BF16, "dconv_pw")
    g["pool_w"], _ = _wgrad_2d(mixed, dpo, 4, F32, "dpool_w", group_diag=True)
    slabs = [g_pw.reshape(N_CHIPS, D_CONV // N_CHIPS, D_CONV),
             g_out.reshape(N_CHIPS, (D_CONV + D_POOL) // N_CHIPS, D_MODEL)]
    (dx1, dproj, g_dw, g["conv_dw_b"], g["mix_norm"]), (parts,) = _mix_bwd_seq(
        du1, dmixed, proj, x1, dx2, w["mix_norm"], w["conv_dw"], w["w_in"], "mix_bwd_seq",
        cargos=[Cargo("scatter_chips", slabs)])
    landed(["conv_pw", "w_out"], parts)
    g_in, _ = _wgrad_2d(h2, dproj, N_CHIPS, BF16, "dw_in")
    dx, g["ffn1_norm"], df1, dg1, du1_ = _ffn_bwd(dx1, xt, w["ffn1_norm"], s1, p1, w["ffn1_w_gate"],
                                                   w["ffn1_w_up"], w["ffn1_w_down"], "ffn1_bwd")
    slabs = [g_in, g_dw.reshape(CONV_WIDTH + 1, N_CHIPS, D_CONV // N_CHIPS).transpose(1, 0, 2)]
    (parts,) = ffn_wgrad("ffn1_w_gate", dg1, h1, cargos=[Cargo("scatter_chips", slabs)])
    landed(["w_in", "conv_dw"], parts)
    swapped_mid, swapped_gate, small_parts = ffn_wgrad(
        "ffn1_w_up", du1_, h1,
        cargos=[Cargo("swap", [sums[k] for k in MID]), Cargo("swap", [sums["ffn1_w_gate"]]),
                Cargo("gather_devices", [_pack_small(g)])])
    (swapped_up,) = ffn_wgrad("ffn1_w_down", a1, df1, cargos=[Cargo("swap", [sums["ffn1_w_up"]])])
    swapped_down = _exchange(Cargo("swap", [sums["ffn1_w_down"]]), "swap_last")

    theirs = dict(zip(["ffn2_" + k for k in FFN_W], swapped2))
    theirs.update(zip(MID, swapped_mid))
    theirs.update(ffn1_w_gate=swapped_gate[0], ffn1_w_up=swapped_up[0], ffn1_w_down=swapped_down[0])
    grads, deltas, new_m, new_v = {}, {}, {}, {}
    for k in theirs:
        res = _adamw([sums[k], theirs[k]], _as_stored(k, wts[k]), _as_stored(k, mom_m[k]),
                     _as_stored(k, mom_v[k]), "adamw_" + k)
        grads[k], deltas[k], new_m[k], new_v[k] = [_as_given(k, t) for t in res]
    res = _adamw(small_parts, _pack_small(wts), _pack_small(mom_m), _pack_small(mom_v), "adamw_small")
    for dst, packed in zip((grads, deltas, new_m, new_v), res):
        dst.update(_unpack_small(packed))

    out = [loss, dx[None]]
    for group in (grads, deltas, new_m, new_v):
        out += [group[k] for k in WEIGHTS]
    return tuple(out)
```

```python
import functools

import jax
import jax.numpy as jnp
from jax import lax
from jax.experimental import pallas as pl
from jax.experimental.pallas import tpu as pltpu

F32 = jnp.float32
BF16 = jnp.bfloat16
MESH = pl.DeviceIdType.MESH

N_CHIPS = 4
N_DEV = 8
D_MODEL = 1024
D_CONV = 512
D_POOL = 512
CONV_WIDTH = 31
POOL_WINDOWS = (2, 4, 8, 16)
POOL_GROUP = 128
D_IN = 2 * D_CONV + D_POOL
HALO = 32
RMS_EPS = 1e-6
LN_EPS = 1e-5
FFN_RES_WEIGHT = 0.5
ADAM_LR = 0.001
ADAM_B1 = 0.9
ADAM_B2 = 0.999
ADAM_EPS = 1e-08
ADAM_WD = 0.01
ADAM_STEP = 10
VMEM_LIMIT_BYTES = 52 * 1024 * 1024
TM_FFN = 512
TM_MIX = 256
TT_WGRAD = 2048
STRIP = 16
SLOTS_PER_STEP = 2
SUBLANES = 8
RELAY_AT_EIGHTHS = 7

HBM = pl.BlockSpec(memory_space=pl.ANY)


def _dot(a, b):
    return jnp.dot(a, b, preferred_element_type=F32)


def _dot_nt(a, b):
    return lax.dot_general(a, b, (((1,), (1,)), ((), ())), preferred_element_type=F32)


def _dot_tn(a, b):
    return lax.dot_general(a, b, (((0,), (0,)), ((), ())), preferred_element_type=F32)


def _sds(shape, dtype):
    return jax.ShapeDtypeStruct(shape, dtype)


def _rms_stats(xv):
    r = lax.rsqrt(jnp.mean(xv * xv, axis=-1, keepdims=True) + RMS_EPS)
    return r, xv * r


def _swiglu_saved(gate, up):
    sig = jax.nn.sigmoid(gate)
    silu = gate * sig
    return silu, up * (sig * (1.0 + gate * (1.0 - sig))), silu * up


def _rms_bwd(dh, n, r, gain):
    dn = dh * gain
    return r * (dn - n * jnp.mean(dn * n, axis=-1, keepdims=True))


def _place():
    x, y, c = lax.axis_index("x"), lax.axis_index("y"), lax.axis_index("c")
    return x, y, c, [(1 - x, y), (x, 1 - y), (1 - x, 1 - y)]


class Cargo:
    def __init__(self, kind, arrays):
        self.kind, self.arrays = kind, list(arrays)
        n = len(self.arrays)
        self.two_level = kind in ("gather_slots", "gather_chips")
        if self.two_level:
            self.out_shape = [_sds((N_CHIPS,) + a.shape, a.dtype) for a in self.arrays]
        elif kind == "gather_devices":
            self.out_shape = [_sds((N_DEV,) + a.shape, a.dtype) for a in self.arrays]
        else:
            self.out_shape = [_sds(a.shape, a.dtype) for a in self.arrays]
        n_remote = n * {"swap": 1, "gather_devices": N_DEV - 1}.get(kind, N_CHIPS - 1)
        n_own = 0 if kind == "swap" else n
        n_relay = n_remote if self.two_level else 0
        dma = pltpu.SemaphoreType.DMA
        self.scratch = [dma((n_remote,)), dma((n_remote,)), dma((max(n_own, 1),)),
                        dma((max(n_relay, 1),)), dma((max(n_relay, 1),))]

    def _plan(self, ins, outs):
        x, y, c, chips = _place()
        q = 2 * x + y
        sibling = (x, y, 1 - c)
        own, remote, relays = [], [], []
        for a, o in zip(ins, outs):
            if self.two_level:
                half = a.shape[0] // 2
                mine = pl.ds(pl.multiple_of(c * half, SUBLANES), half)
                theirs = pl.ds(pl.multiple_of((1 - c) * half, SUBLANES), half)
                own.append((a, o.at[0 if self.kind == "gather_slots" else q]))
                for j, (px, py) in enumerate(chips):
                    there, here = (j + 1, j + 1) if self.kind == "gather_slots" else (q, 2 * px + py)
                    remote.append((a.at[mine], o.at[there, mine], o.at[here, mine], (px, py, c)))
                    relays.append((o.at[here, mine], o.at[here, mine], o.at[here, theirs], sibling))
            elif self.kind == "scatter_chips":
                own.append((a.at[q], o.at[q]))
                remote += [(a.at[2 * px + py], o.at[q], o.at[2 * px + py], (px, py, c)) for px, py in chips]
            elif self.kind == "swap":
                remote.append((a, o, o, sibling))
            else:
                own.append((a, o.at[4 * x + 2 * y + c]))
                for k in range(1, N_DEV):
                    px, py, pc = x ^ (k >> 2 & 1), y ^ (k >> 1 & 1), c ^ (k & 1)
                    remote.append((a, o.at[4 * x + 2 * y + c], o.at[4 * px + 2 * py + pc], (px, py, pc)))
        return own, remote, relays

    @staticmethod
    def _copies(entries, send_sems, recv_sems):
        out = []
        for k, (src, dst, landed, peer) in enumerate(entries):
            def make(dst_ref, k=k, src=src, peer=peer):
                return pltpu.make_async_remote_copy(src_ref=src, dst_ref=dst_ref, send_sem=send_sems.at[k],
                                                    recv_sem=recv_sems.at[k], device_id=peer, device_id_type=MESH)
            out.append((make(dst), make(landed)))
        return out

    def start(self, ins, outs, sems):
        own, remote, _ = self._plan(ins, outs)
        for k, (src, dst) in enumerate(own):
            pltpu.make_async_copy(src, dst, sems[2].at[k]).start()
        for mine, _ in self._copies(remote, sems[0], sems[1]):
            mine.start()

    def relay(self, ins, outs, sems):
        _, remote, relays = self._plan(ins, outs)
        passed = self._copies(relays, sems[3], sems[4])
        for (_, arriving), (mine, _) in zip(self._copies(remote, sems[0], sems[1]), passed):
            arriving.wait_recv()
            mine.start()

    def wait(self, ins, outs, sems):
        own, remote, relays = self._plan(ins, outs)
        for mine, arriving in self._copies(remote, sems[0], sems[1]):
            mine.wait_send()
            if not self.two_level:
                arriving.wait_recv()
        for mine, arriving in self._copies(relays, sems[3], sems[4]):
            mine.wait_send()
            arriving.wait_recv()
        for k, (src, dst) in enumerate(own):
            pltpu.make_async_copy(src, dst, sems[2].at[k]).wait()


N_CARGO_SEMS = 5


def _call(body, *, name, grid, in_specs, out_specs, out_shape, args, scratch_shapes=(), cargos=()):
    n_in, n_out, n_scr = len(in_specs), len(out_specs), len(scratch_shapes)
    c_in = [len(cg.arrays) for cg in cargos]
    n_cin = sum(c_in)

    def wrapped(*refs):
        ins = refs[:n_in]
        cins = refs[n_in:n_in + n_cin]
        outs = refs[n_in + n_cin:n_in + n_cin + n_out]
        couts = refs[n_in + n_cin + n_out:n_in + 2 * n_cin + n_out]
        scr = refs[n_in + 2 * n_cin + n_out:n_in + 2 * n_cin + n_out + n_scr]
        sems = refs[n_in + 2 * n_cin + n_out + n_scr:]
        step, n_steps = 0, 1
        for ax, size in enumerate(grid):
            step = step * size + pl.program_id(ax)
            n_steps *= size

        def each(method, only_two_level=False):
            at = 0
            for k, cg in enumerate(cargos):
                if cg.two_level or not only_two_level:
                    getattr(cg, method)(cins[at:at + c_in[k]], couts[at:at + c_in[k]],
                                        sems[N_CARGO_SEMS * k:N_CARGO_SEMS * (k + 1)])
                at += c_in[k]

        body(*ins, *outs, *scr)
        if cargos:
            pl.when(step == 0)(lambda: each("start"))
        if any(cg.two_level for cg in cargos):
            pl.when(step == (RELAY_AT_EIGHTHS * n_steps) // 8)(lambda: each("relay", only_two_level=True))
        if cargos:
            pl.when(step == n_steps - 1)(lambda: each("wait"))

    res = pl.pallas_call(
        wrapped, name=name, grid=grid,
        in_specs=list(in_specs) + [HBM] * n_cin,
        out_specs=list(out_specs) + [HBM] * n_cin,
        out_shape=list(out_shape) + [s for cg in cargos for s in cg.out_shape],
        scratch_shapes=list(scratch_shapes) + [s for cg in cargos for s in cg.scratch],
        compiler_params=pltpu.CompilerParams(dimension_semantics=("arbitrary",) * len(grid),
                                             vmem_limit_bytes=VMEM_LIMIT_BYTES),
    )(*args, *[a for cg in cargos for a in cg.arrays])
    outs, rest = list(res[:n_out]), list(res[n_out:])
    cargo_outs = []
    for k in c_in:
        cargo_outs.append(rest[:k])
        rest = rest[k:]
    return outs, cargo_outs


def _exchange(cargo, name):
    _, (outs,) = _call(lambda: None, name=name, grid=(1,), in_specs=[], out_specs=[], out_shape=[], args=[],
                       cargos=[cargo])
    return outs


def _ffn_up_gather(x, gain, wg_t, wu_t, name, cargos=()):
    t_len, d = x.shape
    fq = wg_t.shape[0]
    tm = min(TM_FFN, t_len)
    n_tiles = t_len // tm
    relay_tile = n_tiles // 2
    fetch_tile = min(relay_tile + 1, n_tiles - 1)

    def body(x_ref, g_ref, wg_in, wu_in, h_ref, s_ref, p_ref, a_ref, wg_all, wu_all,
             wg_v, wu_v, h_all, send_sems, recv_sems, pass_send_sems, pass_recv_sems, own_sems, load_sems):
        s = pl.program_id(0)
        i = pl.program_id(1)
        x_, y_, c_, chips = _place()
        shards = ((wg_in, wg_all, wg_v), (wu_in, wu_all, wu_v))
        mine = pl.ds(pl.multiple_of(c_ * (fq // 2), SUBLANES), fq // 2)
        theirs = pl.ds(pl.multiple_of((1 - c_) * (fq // 2), SUBLANES), fq // 2)

        def to_peer(k, j):
            w_in, w_all, _ = shards[k]
            return pltpu.make_async_remote_copy(
                src_ref=w_in.at[mine], dst_ref=w_all.at[j + 1, mine], send_sem=send_sems.at[3 * k + j],
                recv_sem=recv_sems.at[3 * k + j], device_id=(*chips[j], c_), device_id_type=MESH)

        def to_sibling(k, j, landing=False):
            w_all = shards[k][1]
            return pltpu.make_async_remote_copy(
                src_ref=w_all.at[j + 1, mine], dst_ref=w_all.at[j + 1, theirs if landing else mine],
                send_sem=pass_send_sems.at[3 * k + j], recv_sem=pass_recv_sems.at[3 * k + j],
                device_id=(x_, y_, 1 - c_), device_id_type=MESH)

        def keep(k):
            return pltpu.make_async_copy(shards[k][0], shards[k][1].at[0], own_sems.at[k])

        @pl.when((s == 0) & (i == 0))
        def _():
            for j in range(N_CHIPS - 1):
                for k in range(2):
                    to_peer(k, j).start()
            for k in range(2):
                keep(k).start()

        def load(k, slot):
            src = shards[k][0] if slot == 0 else shards[k][1].at[slot]
            return pltpu.make_async_copy(src, shards[k][2].at[slot % 2], load_sems.at[k])

        @pl.when((s == 0) & (i == 0))
        def _():
            for k in range(2):
                load(k, 0).start()
            for k in range(2):
                load(k, 0).wait()

        def pass_on(slot):
            for k in range(2):
                to_peer(k, slot - 1).wait_recv()
                to_sibling(k, slot - 1).start()

        def fetch(slot):
            for k in range(2):
                to_sibling(k, slot - 1, landing=True).wait_recv()
                load(k, slot).start()

        for slot in range(1, N_CHIPS):
            pl.when((s == slot - 1) & (i == relay_tile))(functools.partial(pass_on, slot))
            pl.when((s == slot - 1) & (i == fetch_tile))(functools.partial(fetch, slot))

            @pl.when((s == slot) & (i == 0))
            def _():
                for k in range(2):
                    load(k, slot).wait()

        @pl.when(s == 0)
        def _():
            _, n = _rms_stats(x_ref[...])
            h_new = (n * g_ref[...]).astype(BF16)
            h_ref[...] = h_new
            h_all[i] = h_new

        h = h_all[i]
        silu, dgate, act = _swiglu_saved(_dot_nt(h, wg_v[s % 2]), _dot_nt(h, wu_v[s % 2]))
        s_ref[...] = silu.astype(BF16)
        p_ref[...] = dgate.astype(BF16)
        a_ref[...] = act.astype(BF16)

        @pl.when((s == N_CHIPS - 1) & (i == n_tiles - 1))
        def _():
            for k in range(2):
                for j in range(N_CHIPS - 1):
                    to_peer(k, j).wait_send()
                    to_sibling(k, j).wait_send()
                keep(k).wait()

    tok = pl.BlockSpec((tm, d), lambda s, i: (jnp.where(s == 0, i, n_tiles - 1), 0))
    hid = pl.BlockSpec((None, tm, fq), lambda s, i: (s, i, 0))
    outs, cargo_outs = _call(
        body, name=name, grid=(N_CHIPS, n_tiles),
        in_specs=[tok, pl.BlockSpec((1, d), lambda s, i: (0, 0)), HBM, HBM],
        out_specs=[tok, hid, hid, hid, HBM, HBM],
        out_shape=[_sds((t_len, d), BF16)] + [_sds((N_CHIPS, t_len, fq), BF16)] * 3
        + [_sds((N_CHIPS, fq, d), BF16)] * 2,
        scratch_shapes=[pltpu.VMEM((2, fq, d), BF16), pltpu.VMEM((2, fq, d), BF16),
                        pltpu.VMEM((n_tiles, tm, d), BF16)]
        + [pltpu.SemaphoreType.DMA((6,))] * 4 + [pltpu.SemaphoreType.DMA((2,))] * 2,
        args=[x, gain, wg_t, wu_t], cargos=cargos)
    return outs, cargo_outs


def _load_once(hbm_refs, vmem_refs, sems, first):
    @pl.when(first)
    def _():
        copies = [pltpu.make_async_copy(src, dst, sems.at[k]) for k, (src, dst) in enumerate(zip(hbm_refs, vmem_refs))]
        for cp in copies:
            cp.start()
        for cp in copies:
            cp.wait()


def _ffn_down(x, act, wd, name, cargos=()):
    t_len, d = x.shape
    nq, fq, _ = wd.shape
    tm = min(TM_FFN, t_len)

    def body(x_ref, a_ref, wd_ref, xo_ref):
        y = _dot(a_ref[0], wd_ref[0])
        for j in range(1, nq):
            y = y + _dot(a_ref[j], wd_ref[j])
        xo_ref[...] = x_ref[...] + FFN_RES_WEIGHT * y

    tok = pl.BlockSpec((tm, d), lambda i: (i, 0))
    (xo,), cargo_outs = _call(
        body, name=name, grid=(t_len // tm,),
        in_specs=[tok, pl.BlockSpec((nq, tm, fq), lambda i: (0, i, 0)), pl.BlockSpec((nq, fq, d), lambda i: (0, 0, 0))],
        out_specs=[tok], out_shape=[_sds((t_len, d), F32)], args=[x, act, wd], cargos=cargos)
    return xo, cargo_outs


def _ffn_fwd(x, gain, wg_t, wu_t, wd, name):
    t_len, d = x.shape
    nq, fq, _ = wd.shape
    tm = min(TM_FFN, t_len)

    def body(x_ref, g_ref, wg_hbm, wu_hbm, wd_hbm, xo_ref, h_ref, s_ref, p_ref, a_ref,
             h_s, acc, wg_v, wu_v, wd_v, load_sems):
        i = pl.program_id(0)
        j = pl.program_id(1)
        _load_once((wg_hbm, wu_hbm, wd_hbm), (wg_v, wu_v, wd_v), load_sems, (i == 0) & (j == 0))

        @pl.when(j == 0)
        def _():
            _, n = _rms_stats(x_ref[...])
            h = (n * g_ref[...]).astype(BF16)
            h_s[...] = h
            h_ref[...] = h
            acc[...] = jnp.zeros_like(acc)

        h = h_s[...]
        y = None
        for jj in range(SLOTS_PER_STEP):
            slot = j * SLOTS_PER_STEP + jj
            silu, dgate, act = _swiglu_saved(_dot_nt(h, wg_v[slot]), _dot_nt(h, wu_v[slot]))
            s_ref[jj] = silu.astype(BF16)
            p_ref[jj] = dgate.astype(BF16)
            a_ref[jj] = act.astype(BF16)
            part = _dot(a_ref[jj], wd_v[slot])
            y = part if y is None else y + part
        acc[...] += y

        @pl.when(j == nq // SLOTS_PER_STEP - 1)
        def _():
            xo_ref[...] = x_ref[...] + FFN_RES_WEIGHT * acc[...]

    tok = pl.BlockSpec((tm, d), lambda i, j: (i, 0))
    hid = pl.BlockSpec((SLOTS_PER_STEP, tm, fq), lambda i, j: (j, i, 0))
    outs, _ = _call(
        body, name=name, grid=(t_len // tm, nq // SLOTS_PER_STEP),
        in_specs=[tok, pl.BlockSpec((1, d), lambda i, j: (0, 0)), HBM, HBM, HBM],
        out_specs=[tok, tok, hid, hid, hid],
        out_shape=[_sds((t_len, d), F32), _sds((t_len, d), BF16)] + [_sds((nq, t_len, fq), BF16)] * 3,
        scratch_shapes=[pltpu.VMEM((tm, d), BF16), pltpu.VMEM((tm, d), F32)]
        + [pltpu.VMEM((nq, fq, d), BF16)] * 3 + [pltpu.SemaphoreType.DMA((3,))],
        args=[x, gain, wg_t, wu_t, wd])
    return outs


def _ffn_bwd(dy, x_in, gain, silu, dgate_du, wg_t, wu_t, wd, name):
    t_len, d = dy.shape
    nq, fq, _ = wd.shape
    tm = min(TM_FFN, t_len)

    def body(dy_ref, x_ref, g_ref, s_ref, p_ref, wg_hbm, wu_hbm, wd_hbm,
             dx_ref, dgain_ref, df_ref, dg_ref, du_ref, df_s, dh_acc, dact_s, wg_v, wu_v, wd_v, load_sems):
        i = pl.program_id(0)
        j = pl.program_id(1)
        _load_once((wg_hbm, wu_hbm, wd_hbm), (wg_v, wu_v, wd_v), load_sems, (i == 0) & (j == 0))

        @pl.when((i == 0) & (j == 0))
        def _():
            dgain_ref[...] = jnp.zeros_like(dgain_ref)

        @pl.when(j == 0)
        def _():
            df = (FFN_RES_WEIGHT * dy_ref[...]).astype(BF16)
            df_s[...] = df
            df_ref[...] = df
            dh_acc[...] = jnp.zeros_like(dh_acc)

        half = tm // 2
        for r0 in (0, half):
            dact_s[r0:r0 + half, :] = _dot_nt(df_s[r0:r0 + half, :], wd_v[j])

        for r0 in range(0, tm, STRIP):
            dact = dact_s[r0:r0 + STRIP, :]
            dg_ref[r0:r0 + STRIP, :] = (dact * p_ref[r0:r0 + STRIP, :].astype(F32)).astype(BF16)
            du_ref[r0:r0 + STRIP, :] = (dact * s_ref[r0:r0 + STRIP, :].astype(F32)).astype(BF16)

        for r0 in (0, half):
            rows = slice(r0, r0 + half)
            dh_acc[rows, :] += _dot(dg_ref[rows, :], wg_v[j]) + _dot(du_ref[rows, :], wu_v[j])

        @pl.when(j == nq - 1)
        def _():
            r, n = _rms_stats(x_ref[...])
            dh = dh_acc[...]
            dgain_ref[...] += jnp.sum(dh * n, axis=0, keepdims=True)
            dx_ref[...] = dy_ref[...] + _rms_bwd(dh, n, r, g_ref[...])

    tok = pl.BlockSpec((tm, d), lambda i, j: (i, 0))
    vec = pl.BlockSpec((1, d), lambda i, j: (0, 0))
    hid = pl.BlockSpec((None, tm, fq), lambda i, j: (j, i, 0))
    outs, _ = _call(
        body, name=name, grid=(t_len // tm, nq),
        in_specs=[tok, tok, vec, hid, hid, HBM, HBM, HBM],
        out_specs=[tok, vec, tok, hid, hid],
        out_shape=[_sds((t_len, d), F32), _sds((1, d), F32), _sds((t_len, d), BF16),
                   _sds((nq, t_len, fq), BF16), _sds((nq, t_len, fq), BF16)],
        scratch_shapes=[pltpu.VMEM((tm, d), BF16), pltpu.VMEM((tm, d), F32), pltpu.VMEM((tm, fq), F32)]
        + [pltpu.VMEM((nq, fq, d), BF16)] * 3 + [pltpu.SemaphoreType.DMA((3,))],
        args=[dy, x_in, gain, silu, dgate_du, wg_t, wu_t, wd])
    return outs


def _wgrad(lhs, rhs, l_spec, r_spec, out_shape, out_spec, acc_shape, grid, name, cargos=()):
    n_t = grid[-1]
    t_axis = len(grid) - 1

    def body(l_ref, r_ref, o_ref, acc):
        t = pl.program_id(t_axis)

        @pl.when(t == 0)
        def _():
            acc[...] = jnp.zeros_like(acc)

        acc[...] += _dot_tn(l_ref[...].astype(BF16), r_ref[...].astype(BF16))

        @pl.when(t == n_t - 1)
        def _():
            o_ref[...] = acc[...].astype(o_ref.dtype)

    (out,), cargo_outs = _call(
        body, name=name, grid=grid, in_specs=[l_spec, r_spec], out_specs=[out_spec], out_shape=[out_shape],
        scratch_shapes=[pltpu.VMEM(acc_shape, F32)], args=[lhs, rhs], cargos=cargos)
    return out, cargo_outs


def _wgrad_hid_tok_scatter(hid, tok, name, cargos=()):
    t_len, d = tok.shape
    nq, _, fq = hid.shape
    half = fq // 2
    tt = min(TT_WGRAD, t_len)
    n_t = t_len // tt

    def body(l_ref, r_ref, parts_ref, acc, stage, pair, summed, zeros,
             pair_send_sems, pair_recv_sems, send_sems, recv_sems, own_sem, zero_sems):
        g = pl.program_id(0)
        t = pl.program_id(1)
        x_, y_, c_, chips = _place()
        mine = pl.ds(pl.multiple_of(c_ * half, STRIP), half)
        theirs = pl.ds(pl.multiple_of((1 - c_) * half, STRIP), half)

        def to_sibling(slot):
            return pltpu.make_async_remote_copy(
                src_ref=stage.at[slot, theirs], dst_ref=pair.at[slot], send_sem=pair_send_sems.at[slot],
                recv_sem=pair_recv_sems.at[slot], device_id=(x_, y_, 1 - c_), device_id_type=MESH)

        def to_peer(j):
            return pltpu.make_async_remote_copy(
                src_ref=summed.at[j + 1], dst_ref=parts_ref.at[j + 1, mine], send_sem=send_sems.at[j],
                recv_sem=recv_sems.at[j], device_id=(*chips[j], c_), device_id_type=MESH)

        keep = pltpu.make_async_copy(summed.at[0], parts_ref.at[0, mine], own_sem)

        def blank(slot):
            return pltpu.make_async_copy(zeros, parts_ref.at[slot, theirs], zero_sems.at[slot])

        @pl.when((g == 0) & (t == 0))
        def _():
            zeros[...] = jnp.zeros_like(zeros)
            for slot in range(nq):
                blank(slot).start()

        @pl.when(t == 0)
        def _():
            acc[...] = jnp.zeros_like(acc)

        acc[...] += _dot_tn(l_ref[...], r_ref[...])

        for step in range(nq):
            slot = (step + 1) % nq

            @pl.when((g == step) & (t == n_t - 1))
            def _():
                stage[slot] = acc[...].astype(BF16)
                to_sibling(slot).start()
                to_sibling(slot).wait_recv()
                summed[slot] = (stage[slot, mine, :].astype(F32) + pair[slot].astype(F32)).astype(BF16)
                if slot > 0:
                    to_peer(slot - 1).start()
                else:
                    keep.start()

        @pl.when((g == nq - 1) & (t == n_t - 1))
        def _():
            for j in range(N_CHIPS - 1):
                to_peer(j).wait()
            keep.wait()
            for slot in range(nq):
                to_sibling(slot).wait_send()
                blank(slot).wait()

    (parts,), cargo_outs = _call(
        body, name=name, grid=(nq, n_t),
        in_specs=[pl.BlockSpec((None, tt, fq), lambda g, t: ((g + 1) % nq, t, 0)),
                  pl.BlockSpec((tt, d), lambda g, t: (t, 0))],
        out_specs=[HBM], out_shape=[_sds((nq, fq, d), BF16)],
        scratch_shapes=[pltpu.VMEM((fq, d), F32), pltpu.VMEM((nq, fq, d), BF16), pltpu.VMEM((nq, half, d), BF16),
                        pltpu.VMEM((nq, half, d), BF16), pltpu.VMEM((half, d), BF16),
                        pltpu.SemaphoreType.DMA((nq,)), pltpu.SemaphoreType.DMA((nq,)),
                        pltpu.SemaphoreType.DMA((N_CHIPS - 1,)), pltpu.SemaphoreType.DMA((N_CHIPS - 1,)),
                        pltpu.SemaphoreType.DMA(()), pltpu.SemaphoreType.DMA((nq,))],
        args=[hid, tok], cargos=cargos)
    return parts, cargo_outs


def _wgrad_2d(lhs, rhs, n_col_blocks, out_dtype, name, group_diag=False, cargos=()):
    t_len, k = lhs.shape
    n = rhs.shape[1]
    nb = n // n_col_blocks
    kb = k // n_col_blocks if group_diag else k
    tt = min(TT_WGRAD, t_len)
    l_map = (lambda q, t: (t, q)) if group_diag else (lambda q, t: (t, 0))
    return _wgrad(lhs, rhs,
                  pl.BlockSpec((tt, kb), l_map),
                  pl.BlockSpec((tt, nb), lambda q, t: (t, q)),
                  _sds((n_col_blocks, kb, nb), out_dtype),
                  pl.BlockSpec((None, kb, nb), lambda q, t: (q, 0, 0)),
                  (kb, nb), (n_col_blocks, t_len // tt), name, cargos)


def _layernorm_stats(u1):
    mu = jnp.mean(u1, axis=-1, keepdims=True)
    xc = u1 - mu
    rstd = lax.rsqrt(jnp.mean(xc * xc, axis=-1, keepdims=True) + LN_EPS)
    return rstd, xc * rstd


def _positions(i, tm, rows, offset=0):
    return (lax.broadcasted_iota(jnp.int32, (rows, 1), 0) + (i * tm + offset)).astype(F32)


SHIFT_ROWS = HALO - SUBLANES


def _fill_shifted(ext_s, sh_s, tm):
    for b in range(1, SUBLANES):
        sh_s[b - 1] = ext_s[pl.ds(b, tm + SHIFT_ROWS), :]


def _window(ext_s, sh_s, shift, tm):
    a, b = divmod(shift, SUBLANES)
    if b == 0:
        return ext_s[pl.ds(shift, tm), :]
    return sh_s[b - 1, pl.ds(a * SUBLANES, tm), :]


def _tile(tm, cols):
    return pl.BlockSpec((tm, cols), lambda i: (i, 0))


def _whole(shape):
    return pl.BlockSpec(shape, lambda i: (0,) * len(shape))


def _mix_fwd(x1, gain, w_in, conv_dw, conv_b, ln_g, ln_b, conv_pw, pool_w, pool_scale, w_out, name, cargos=()):
    t_len, d = x1.shape
    nq, _, nb = w_in.shape
    tm = min(TM_MIX, t_len)

    def body(x_ref, g_ref, wi_ref, dw_ref, cb_ref, lg_ref, lb_ref, pw_ref, plw_ref, ps_ref, wo_ref,
             x2_ref, h_ref, p_ref, u1_ref, u3_ref, mx_ref, cat_ref, ext_s, pext_s, sh_s, tail_s):
        i = pl.program_id(0)

        @pl.when(i == 0)
        def _():
            tail_s[...] = jnp.zeros_like(tail_s)

        _, n = _rms_stats(x_ref[...])
        h = (n * g_ref[...]).astype(BF16)
        h_ref[...] = h
        for q in range(nq):
            p_ref[:, q * nb:(q + 1) * nb] = _dot(h, wi_ref[q])

        a = p_ref[:, 0:D_CONV]
        g = p_ref[:, D_CONV:2 * D_CONV]
        p = p_ref[:, 2 * D_CONV:]
        ext_s[0:HALO, :] = tail_s[:, 0:D_CONV] * jax.nn.sigmoid(tail_s[:, D_CONV:2 * D_CONV])
        ext_s[HALO:, :] = a * jax.nn.sigmoid(g)
        pext_s[0:HALO, :] = tail_s[:, 2 * D_CONV:]
        pext_s[HALO:, :] = p
        tail_s[...] = p_ref[tm - HALO:tm, :]

        _fill_shifted(ext_s, sh_s, tm)
        u1 = jnp.broadcast_to(cb_ref[...], (tm, D_CONV))
        for k in range(CONV_WIDTH):
            u1 = u1 + dw_ref[k:k + 1, :] * _window(ext_s, sh_s, HALO - (CONV_WIDTH - 1) + k, tm)
        u1_ref[...] = u1
        _, nhat = _layernorm_stats(u1)
        u2 = nhat * lg_ref[...] + lb_ref[...]
        u3 = (u2 * jax.nn.sigmoid(u2)).astype(BF16)
        u3_ref[...] = u3
        cat_ref[:, 0:D_CONV] = _dot(u3, pw_ref[...]).astype(BF16)

        pos1 = _positions(i, tm, tm) + 1.0
        for gi, w in enumerate(POOL_WINDOWS):
            cols = slice(gi * POOL_GROUP, (gi + 1) * POOL_GROUP)
            s = pext_s[pl.ds(HALO, tm), cols]
            for j in range(1, w):
                s = s + pext_s[pl.ds(HALO - j, tm), cols]
            mixed = (s / jnp.minimum(pos1, float(w)) - p[:, cols]).astype(BF16)
            mx_ref[:, cols] = mixed
            out = _dot(mixed, plw_ref[gi]) * ps_ref[:, cols]
            cat_ref[:, D_CONV + gi * POOL_GROUP:D_CONV + (gi + 1) * POOL_GROUP] = out.astype(BF16)

        x2_ref[...] = x_ref[...] + _dot(cat_ref[...], wo_ref[...])

    return _call(
        body, name=name, grid=(t_len // tm,),
        in_specs=[_tile(tm, d), _whole((1, d)), _whole((nq, d, nb)), _whole((CONV_WIDTH + 1, D_CONV)),
                  _whole((1, D_CONV)), _whole((1, D_CONV)), _whole((1, D_CONV)), _whole((D_CONV, D_CONV)),
                  _whole((4, POOL_GROUP, POOL_GROUP)), _whole((1, D_POOL)), _whole((D_CONV + D_POOL, d))],
        out_specs=[_tile(tm, d), _tile(tm, d), _tile(tm, D_IN), _tile(tm, D_CONV), _tile(tm, D_CONV),
                   _tile(tm, D_POOL), _tile(tm, D_CONV + D_POOL)],
        out_shape=[_sds((t_len, d), F32), _sds((t_len, d), BF16), _sds((t_len, D_IN), F32),
                   _sds((t_len, D_CONV), F32), _sds((t_len, D_CONV), BF16), _sds((t_len, D_POOL), BF16),
                   _sds((t_len, D_CONV + D_POOL), BF16)],
        scratch_shapes=[pltpu.VMEM((tm + HALO, D_CONV), F32), pltpu.VMEM((tm + HALO, D_POOL), F32),
                        pltpu.VMEM((SUBLANES - 1, tm + SHIFT_ROWS, D_CONV), F32), pltpu.VMEM((HALO, D_IN), F32)],
        args=[x1, gain, w_in, conv_dw, conv_b, ln_g, ln_b, conv_pw, pool_w, pool_scale, w_out], cargos=cargos)


def _mix_bwd_local(dx2, u1, mixed, ln_g, ln_b, conv_pw, pool_w, pool_scale, w_out, name, cargos=()):
    t_len, d = dx2.shape
    tm = min(TM_MIX, t_len)

    def body(dx_ref, u1_ref, mx_ref, lg_ref, lb_ref, pw_ref, plw_ref, ps_ref, wo_ref,
             du1_ref, dmx_ref, dco_ref, dpo_ref, dlg_ref, dlb_ref, dps_ref):
        @pl.when(pl.program_id(0) == 0)
        def _():
            dlg_ref[...] = jnp.zeros_like(dlg_ref)
            dlb_ref[...] = jnp.zeros_like(dlb_ref)
            dps_ref[...] = jnp.zeros_like(dps_ref)

        dcat = _dot_nt(dx_ref[...].astype(BF16), wo_ref[...])
        dco = dcat[:, 0:D_CONV].astype(BF16)
        dco_ref[...] = dco
        du3 = _dot_nt(dco, pw_ref[...])
        rstd, nhat = _layernorm_stats(u1_ref[...])
        u2 = nhat * lg_ref[...] + lb_ref[...]
        sig = jax.nn.sigmoid(u2)
        du2 = du3 * (sig * (1.0 + u2 * (1.0 - sig)))
        dlg_ref[...] += jnp.sum(du2 * nhat, axis=0, keepdims=True)
        dlb_ref[...] += jnp.sum(du2, axis=0, keepdims=True)
        dnhat = du2 * lg_ref[...]
        du1_ref[...] = rstd * (dnhat - jnp.mean(dnhat, axis=-1, keepdims=True)
                               - nhat * jnp.mean(dnhat * nhat, axis=-1, keepdims=True))

        for gi in range(len(POOL_WINDOWS)):
            cols = slice(gi * POOL_GROUP, (gi + 1) * POOL_GROUP)
            dpo = dcat[:, D_CONV + gi * POOL_GROUP:D_CONV + (gi + 1) * POOL_GROUP]
            pre = _dot(mx_ref[:, cols], plw_ref[gi])
            dps_ref[:, cols] += jnp.sum(dpo * pre, axis=0, keepdims=True)
            dout = (dpo * ps_ref[:, cols]).astype(BF16)
            dpo_ref[:, cols] = dout
            dmx_ref[:, cols] = _dot_nt(dout, plw_ref[gi])

    vec = _whole((1, D_CONV))
    return _call(
        body, name=name, grid=(t_len // tm,),
        in_specs=[_tile(tm, d), _tile(tm, D_CONV), _tile(tm, D_POOL), vec, vec, _whole((D_CONV, D_CONV)),
                  _whole((4, POOL_GROUP, POOL_GROUP)), vec, _whole((D_CONV + D_POOL, d))],
        out_specs=[_tile(tm, D_CONV), _tile(tm, D_POOL), _tile(tm, D_CONV), _tile(tm, D_POOL), vec, vec, vec],
        out_shape=[_sds((t_len, D_CONV), F32), _sds((t_len, D_POOL), F32), _sds((t_len, D_CONV), BF16),
                   _sds((t_len, D_POOL), BF16), _sds((1, D_CONV), F32), _sds((1, D_CONV), F32),
                   _sds((1, D_POOL), F32)],
        args=[dx2, u1, mixed, ln_g, ln_b, conv_pw, pool_w, pool_scale, w_out], cargos=cargos)


def _mix_bwd_seq(du1, dmixed, proj, x1, dx2, gain, conv_dw, w_in, name, cargos=()):
    t_len, d = x1.shape
    nq, _, nb = w_in.shape
    tm = min(TM_MIX, t_len)
    hb = tm // HALO
    last_block = t_len // HALO - 1
    n_tiles = t_len // tm

    def body(du_ref, dun_ref, dm_ref, dmn_ref, p_ref, tail_ref, x_ref, dx2_ref, g_ref, dw_ref, wi_ref,
             dx1_ref, dp_ref, ddw_ref, dcb_ref, dgain_ref, uext_s, dext_s, mext_s, ush_s, dsh_s):
        i = pl.program_id(0)
        first = i == 0
        last = i == n_tiles - 1

        @pl.when(first)
        def _():
            ddw_ref[...] = jnp.zeros_like(ddw_ref)
            dcb_ref[...] = jnp.zeros_like(dcb_ref)
            dgain_ref[...] = jnp.zeros_like(dgain_ref)

        a = p_ref[:, 0:D_CONV]
        g = p_ref[:, D_CONV:2 * D_CONV]
        sg = jax.nn.sigmoid(g)
        ta = tail_ref[:, 0:D_CONV]
        tg = tail_ref[:, D_CONV:2 * D_CONV]
        uext_s[0:HALO, :] = jnp.where(first, 0.0, ta * jax.nn.sigmoid(tg))
        uext_s[HALO:, :] = a * sg
        du1 = du_ref[...]
        dext_s[0:tm, :] = du1
        dext_s[tm:, :] = jnp.where(last, 0.0, dun_ref[...])

        _fill_shifted(uext_s, ush_s, tm)
        _fill_shifted(dext_s, dsh_s, tm)
        du0 = jnp.zeros((tm, D_CONV), F32)
        for k in range(CONV_WIDTH):
            du0 = du0 + dw_ref[k:k + 1, :] * _window(dext_s, dsh_s, CONV_WIDTH - 1 - k, tm)
            ddw_ref[k:k + 1, :] += jnp.sum(
                du1 * _window(uext_s, ush_s, HALO - (CONV_WIDTH - 1) + k, tm), axis=0, keepdims=True)
        dcb_ref[...] += jnp.sum(du1, axis=0, keepdims=True)
        dp_ref[:, 0:D_CONV] = (du0 * sg).astype(BF16)
        dp_ref[:, D_CONV:2 * D_CONV] = (du0 * a * sg * (1.0 - sg)).astype(BF16)

        pos1 = _positions(i, tm, tm) + 1.0
        pos1_next = _positions(i, tm, HALO, offset=tm) + 1.0
        for gi, w in enumerate(POOL_WINDOWS):
            cols = slice(gi * POOL_GROUP, (gi + 1) * POOL_GROUP)
            dm = dm_ref[:, cols]
            mext_s[0:tm, cols] = dm / jnp.minimum(pos1, float(w))
            mext_s[tm:, cols] = jnp.where(last, 0.0, dmn_ref[:, cols] / jnp.minimum(pos1_next, float(w)))
            s = mext_s[pl.ds(0, tm), cols]
            for j in range(1, w):
                s = s + mext_s[pl.ds(j, tm), cols]
            dp_ref[:, 2 * D_CONV + gi * POOL_GROUP:2 * D_CONV + (gi + 1) * POOL_GROUP] = (s - dm).astype(BF16)

        dh = _dot_nt(dp_ref[:, 0:nb], wi_ref[0])
        for q in range(1, nq):
            dh = dh + _dot_nt(dp_ref[:, q * nb:(q + 1) * nb], wi_ref[q])
        r, n = _rms_stats(x_ref[...])
        dgain_ref[...] += jnp.sum(dh * n, axis=0, keepdims=True)
        dx1_ref[...] = dx2_ref[...] + _rms_bwd(dh, n, r, g_ref[...])

    def nxt(cols):
        return pl.BlockSpec((HALO, cols), lambda i: (jnp.minimum((i + 1) * hb, last_block), 0))

    return _call(
        body, name=name, grid=(n_tiles,),
        in_specs=[_tile(tm, D_CONV), nxt(D_CONV), _tile(tm, D_POOL), nxt(D_POOL), _tile(tm, D_IN),
                  pl.BlockSpec((HALO, D_IN), lambda i: (jnp.maximum(i * hb - 1, 0), 0)),
                  _tile(tm, d), _tile(tm, d), _whole((1, d)), _whole((CONV_WIDTH + 1, D_CONV)),
                  _whole((nq, d, nb))],
        out_specs=[_tile(tm, d), _tile(tm, D_IN), _whole((CONV_WIDTH + 1, D_CONV)), _whole((1, D_CONV)),
                   _whole((1, d))],
        out_shape=[_sds((t_len, d), F32), _sds((t_len, D_IN), BF16), _sds((CONV_WIDTH + 1, D_CONV), F32),
                   _sds((1, D_CONV), F32), _sds((1, d), F32)],
        scratch_shapes=[pltpu.VMEM((tm + HALO, D_CONV), F32), pltpu.VMEM((tm + HALO, D_CONV), F32),
                        pltpu.VMEM((tm + HALO, D_POOL), F32),
                        pltpu.VMEM((SUBLANES - 1, tm + SHIFT_ROWS, D_CONV), F32),
                        pltpu.VMEM((SUBLANES - 1, tm + SHIFT_ROWS, D_CONV), F32)],
        args=[du1, du1, dmixed, dmixed, proj, proj, x1, dx2, gain, conv_dw, w_in], cargos=cargos)


def _final_norm_loss(x3, target, gain, name):
    t_len, d = x3.shape
    tm = min(TM_FFN, t_len)

    def body(x_ref, t_ref, g_ref, dx_ref, loss_ref, dgain_ref):
        @pl.when(pl.program_id(0) == 0)
        def _():
            loss_ref[...] = jnp.zeros_like(loss_ref)
            dgain_ref[...] = jnp.zeros_like(dgain_ref)

        r, n = _rms_stats(x_ref[...])
        err = n * g_ref[...] - t_ref[...]
        per_tok = jnp.sum(err * err, axis=-1, keepdims=True) * (1.0 / d)
        loss_ref[...] += 0.5 * jnp.sum(per_tok, axis=0, keepdims=True)
        dy = err * (1.0 / d)
        dgain_ref[...] += jnp.sum(dy * n, axis=0, keepdims=True)
        dx_ref[...] = _rms_bwd(dy, n, r, g_ref[...])

    tok = pl.BlockSpec((tm, d), lambda i: (i, 0))
    outs, _ = _call(
        body, name=name, grid=(t_len // tm,),
        in_specs=[tok, tok, pl.BlockSpec((1, d), lambda i: (0, 0))],
        out_specs=[tok, pl.BlockSpec((1, 128), lambda i: (0, 0)), pl.BlockSpec((1, d), lambda i: (0, 0))],
        out_shape=[_sds((t_len, d), F32), _sds((1, 128), F32), _sds((1, d), F32)],
        args=[x3, target, gain])
    return outs


def _row_tile(rows):
    return rows // 4 if rows % 64 == 0 else rows


def _adamw_math(w, g, m, v):
    m = ADAM_B1 * m + (1.0 - ADAM_B1) * g
    v = ADAM_B2 * v + (1.0 - ADAM_B2) * (g * g)
    m_hat = m / (1.0 - ADAM_B1 ** ADAM_STEP)
    v_hat = v / (1.0 - ADAM_B2 ** ADAM_STEP)
    delta = -ADAM_LR * (m_hat / (jnp.sqrt(v_hat) + ADAM_EPS) + ADAM_WD * w)
    return delta, m, v


def _adamw(parts, w, m, v, name):
    r, c = w.shape
    n = len(parts)
    tr = _row_tile(r)

    def body(*refs):
        g = None
        for p_ref in refs[:n]:
            s = p_ref[0].astype(F32)
            for k in range(1, p_ref.shape[0]):
                s = s + p_ref[k].astype(F32)
            g = s if g is None else g + s
        w_ref, m_ref, v_ref, g_out, d_out, m_out, v_out = refs[n:]
        delta, nm, nv = _adamw_math(w_ref[...], g, m_ref[...], v_ref[...])
        g_out[...] = g
        d_out[...] = delta
        m_out[...] = nm
        v_out[...] = nv

    blk = pl.BlockSpec((tr, c), lambda i: (i, 0))
    p_specs = [pl.BlockSpec((p.shape[0], tr, c), lambda i: (0, i, 0)) for p in parts]
    outs, _ = _call(body, name=name, grid=(r // tr,), in_specs=p_specs + [blk, blk, blk],
                    out_specs=[blk] * 4, out_shape=[_sds((r, c), F32)] * 4, args=[*parts, w, m, v])
    return outs


FFN_W = ("w_gate", "w_up", "w_down")
MID = ("w_in", "conv_dw", "conv_pw", "w_out")
SMALL_1024 = ("ffn1_norm", "mix_norm", "ffn2_norm", "final_norm")
SMALL_512 = ("conv_dw_b", "conv_ln_g", "conv_ln_b", "pool_scale")
WEIGHTS = ("ffn1_norm", "ffn1_w_gate", "ffn1_w_up", "ffn1_w_down", "mix_norm", "w_in", "conv_dw", "conv_dw_b",
           "conv_ln_g", "conv_ln_b", "conv_pw", "pool_w", "pool_scale", "w_out", "ffn2_norm", "ffn2_w_gate",
           "ffn2_w_up", "ffn2_w_down", "final_norm")
PACK_ROWS = 72
PACK_LOSS_ROW = 70


def _pad_rows(a, rows):
    return jnp.pad(a, ((0, rows - a.shape[0]), (0, 0)))


def _pack_small(t, spare=None):
    rows = [t[k].reshape(1, D_MODEL) for k in SMALL_1024]
    rows.append(jnp.concatenate([t["conv_dw_b"].reshape(1, -1), t["conv_ln_g"].reshape(1, -1)], axis=1))
    rows.append(jnp.concatenate([t["conv_ln_b"].reshape(1, -1), t["pool_scale"].reshape(1, -1)], axis=1))
    rows.append(t["pool_w"].reshape(64, D_MODEL))
    if spare is not None:
        rows.append(jnp.pad(spare, ((0, 0), (0, D_MODEL - spare.shape[1]))))
    return _pad_rows(jnp.concatenate(rows, axis=0), PACK_ROWS)


def _unpack_small(p):
    out = {k: p[i] for i, k in enumerate(SMALL_1024)}
    out["conv_dw_b"], out["conv_ln_g"] = p[4, :D_CONV], p[4, D_CONV:]
    out["conv_ln_b"], out["pool_scale"] = p[5, :D_CONV], p[5, D_CONV:]
    out["pool_w"] = p[6:70].reshape(4, POOL_GROUP, POOL_GROUP)
    return out


def _as_stored(name, a):
    if name.endswith(("w_gate", "w_up")):
        return a.T
    if name == "conv_dw":
        return _pad_rows(a, CONV_WIDTH + 1)
    return a


def _as_given(name, a):
    if name.endswith(("w_gate", "w_up")):
        return a.T
    if name == "conv_dw":
        return a[:CONV_WIDTH]
    return a


def kernel(x, ffn1_norm, ffn1_w_gate, ffn1_w_up, ffn1_w_down, mix_norm, w_in, conv_dw, conv_dw_b, conv_ln_g, conv_ln_b, conv_pw, pool_w, pool_scale, w_out, ffn2_norm, ffn2_w_gate, ffn2_w_up, ffn2_w_down, final_norm, loss_target, m_ffn1_norm, m_ffn1_w_gate, m_ffn1_w_up, m_ffn1_w_down, m_mix_norm, m_w_in, m_conv_dw, m_conv_dw_b, m_conv_ln_g, m_conv_ln_b, m_conv_pw, m_pool_w, m_pool_scale, m_w_out, m_ffn2_norm, m_ffn2_w_gate, m_ffn2_w_up, m_ffn2_w_down, m_final_norm, v_ffn1_norm, v_ffn1_w_gate, v_ffn1_w_up, v_ffn1_w_down, v_mix_norm, v_w_in, v_conv_dw, v_conv_dw_b, v_conv_ln_g, v_conv_ln_b, v_conv_pw, v_pool_w, v_pool_scale, v_w_out, v_ffn2_norm, v_ffn2_w_gate, v_ffn2_w_up, v_ffn2_w_down, v_final_norm):
    given = dict(locals())
    wts = {k: given[k] for k in WEIGHTS}
    mom_m = {k: given["m_" + k] for k in WEIGHTS}
    mom_v = {k: given["v_" + k] for k in WEIGHTS}
    xt, target = x[0], loss_target[0]

    shard = {k: _as_stored(k, wts[k]) if k == "conv_dw" else _as_stored(k, wts[k]).astype(BF16)
             for k in WEIGHTS if k.endswith(FFN_W) or k in MID}
    w = {k: wts[k].reshape(1, -1) for k in SMALL_1024 + SMALL_512}
    w["pool_w"] = wts["pool_w"].astype(BF16)

    (h1, s1, p1, a1, w["ffn1_w_gate"], w["ffn1_w_up"]), ((w["ffn1_w_down"],),) = _ffn_up_gather(
        xt, w["ffn1_norm"], shard["ffn1_w_gate"], shard["ffn1_w_up"], "ffn1_up_gather",
        cargos=[Cargo("gather_slots", [shard["ffn1_w_down"]])])
    x1, (mid, (w["ffn2_w_down"],)) = _ffn_down(
        xt, a1, w["ffn1_w_down"], "ffn1_down",
        cargos=[Cargo("gather_chips", [shard[k] for k in MID]), Cargo("gather_slots", [shard["ffn2_w_down"]])])
    w["w_in"] = mid[0]
    w["conv_dw"] = mid[1].transpose(1, 0, 2).reshape(CONV_WIDTH + 1, D_CONV)
    w["conv_pw"] = mid[2].reshape(D_CONV, D_CONV)
    w["w_out"] = mid[3].reshape(D_CONV + D_POOL, D_MODEL)
    (x2, h2, proj, u1, u3, mixed, cat), ((w["ffn2_w_gate"], w["ffn2_w_up"]),) = _mix_fwd(
        x1, w["mix_norm"], w["w_in"], w["conv_dw"], w["conv_dw_b"], w["conv_ln_g"], w["conv_ln_b"], w["conv_pw"],
        w["pool_w"], w["pool_scale"], w["w_out"], "mix_fwd",
        cargos=[Cargo("gather_slots", [shard["ffn2_w_gate"], shard["ffn2_w_up"]])])
    x3, h3, s2, p2, a2 = _ffn_fwd(x2, w["ffn2_norm"], w["ffn2_w_gate"], w["ffn2_w_up"], w["ffn2_w_down"], "ffn2_fwd")
    dx3, loss_share, d_final = _final_norm_loss(x3, target, w["final_norm"], "final_norm_loss")

    g = {"final_norm": d_final}
    sums = {}

    def landed(names, parts):
        sums.update(zip(names, parts))

    dx2, g["ffn2_norm"], df2, dg2, du2 = _ffn_bwd(dx3, x2, w["ffn2_norm"], s2, p2, w["ffn2_w_gate"],
                                                   w["ffn2_w_up"], w["ffn2_w_down"], "ffn2_bwd")
    def ffn_wgrad(name, hid, tok, cargos=()):
        parts, cargo_outs = _wgrad_hid_tok_scatter(hid, tok, name.replace("_w_", "_dw_"), cargos=cargos)
        landed([name], [parts])
        return cargo_outs

    ffn_wgrad("ffn2_w_gate", dg2, h3)
    ffn_wgrad("ffn2_w_up", du2, h3)
    ffn_wgrad("ffn2_w_down", a2, df2)
    (du1, dmixed, dco, dpo, g["conv_ln_g"], g["conv_ln_b"], g["pool_scale"]), (swapped2,) = _mix_bwd_local(
        dx2, u1, mixed, w["conv_ln_g"], w["conv_ln_b"], w["conv_pw"], w["pool_w"], w["pool_scale"], w["w_out"],
        "mix_bwd_local", cargos=[Cargo("swap", [sums["ffn2_" + k] for k in FFN_W])])
    g_out, _ = _wgrad_2d(cat, dx2, 1, BF16, "dw_out")
    g_pw, _ = _wgrad_2d(u3, dco, 1, BF16, "dconv_pw")
    g["pool_w"], _ = _wgrad_2d(mixed, dpo, 4, F32, "dpool_w", group_diag=True)
    slabs = [g_pw.reshape(N_CHIPS, D_CONV // N_CHIPS, D_CONV),
             g_out.reshape(N_CHIPS, (D_CONV + D_POOL) // N_CHIPS, D_MODEL)]
    (dx1, dproj, g_dw, g["conv_dw_b"], g["mix_norm"]), (parts,) = _mix_bwd_seq(
        du1, dmixed, proj, x1, dx2, w["mix_norm"], w["conv_dw"], w["w_in"], "mix_bwd_seq",
        cargos=[Cargo("scatter_chips", slabs)])
    landed(["conv_pw", "w_out"], parts)
    g_in, _ = _wgrad_2d(h2, dproj, N_CHIPS, BF16, "dw_in")
    dx, g["ffn1_norm"], df1, dg1, du1_ = _ffn_bwd(dx1, xt, w["ffn1_norm"], s1, p1, w["ffn1_w_gate"],
                                                   w["ffn1_w_up"], w["ffn1_w_down"], "ffn1_bwd")
    slabs = [g_in, g_dw.reshape(CONV_WIDTH + 1, N_CHIPS, D_CONV // N_CHIPS).transpose(1, 0, 2)]
    (parts,) = ffn_wgrad("ffn1_w_gate", dg1, h1, cargos=[Cargo("scatter_chips", slabs)])
    landed(["w_in", "conv_dw"], parts)
    swapped_mid, swapped_gate, small_parts = ffn_wgrad(
        "ffn1_w_up", du1_, h1,
        cargos=[Cargo("swap", [sums[k] for k in MID]), Cargo("swap", [sums["ffn1_w_gate"]]),
                Cargo("gather_devices", [_pack_small(g, spare=loss_share)])])
    (swapped_up,) = ffn_wgrad("ffn1_w_down", a1, df1, cargos=[Cargo("swap", [sums["ffn1_w_up"]])])
    swapped_down = _exchange(Cargo("swap", [sums["ffn1_w_down"]]), "swap_last")

    theirs = dict(zip(["ffn2_" + k for k in FFN_W], swapped2))
    theirs.update(zip(MID, swapped_mid))
    theirs.update(ffn1_w_gate=swapped_gate[0], ffn1_w_up=swapped_up[0], ffn1_w_down=swapped_down[0])
    grads, deltas, new_m, new_v = {}, {}, {}, {}
    for k in theirs:
        res = _adamw([sums[k], theirs[k]], _as_stored(k, wts[k]), _as_stored(k, mom_m[k]),
                     _as_stored(k, mom_v[k]), "adamw_" + k)
        grads[k], deltas[k], new_m[k], new_v[k] = [_as_given(k, t) for t in res]
    res = _adamw(small_parts, _pack_small(wts), _pack_small(mom_m), _pack_small(mom_v), "adamw_small")
    for dst, packed in zip((grads, deltas, new_m, new_v), res):
        dst.update(_unpack_small(packed))
    loss = res[0][PACK_LOSS_ROW, 0]

    out = [loss, dx[None]]
    for group in (grads, deltas, new_m, new_v):
        out += [group[k] for k in WEIGHTS]
    return tuple(out)
```

```python
import functools

import jax
import jax.numpy as jnp
from jax import lax
from jax.experimental import pallas as pl
from jax.experimental.pallas import tpu as pltpu

F32 = jnp.float32
BF16 = jnp.bfloat16
MESH = pl.DeviceIdType.MESH

N_CHIPS = 4
N_DEV = 8
D_MODEL = 1024
D_CONV = 512
D_POOL = 512
CONV_WIDTH = 31
POOL_WINDOWS = (2, 4, 8, 16)
POOL_GROUP = 128
D_IN = 2 * D_CONV + D_POOL
HALO = 32
RMS_EPS = 1e-6
LN_EPS = 1e-5
FFN_RES_WEIGHT = 0.5
ADAM_LR = 0.001
ADAM_B1 = 0.9
ADAM_B2 = 0.999
ADAM_EPS = 1e-08
ADAM_WD = 0.01
ADAM_STEP = 10
VMEM_LIMIT_BYTES = 52 * 1024 * 1024
TM_FFN = 512
TM_MIX = 256
TT_WGRAD = 2048
STRIP = 16
SLOTS_PER_STEP = 2
SUBLANES = 8
RELAY_AT_EIGHTHS = 7

HBM = pl.BlockSpec(memory_space=pl.ANY)


def _dot(a, b):
    return jnp.dot(a, b, preferred_element_type=F32)


def _dot_nt(a, b):
    return lax.dot_general(a, b, (((1,), (1,)), ((), ())), preferred_element_type=F32)


def _dot_tn(a, b):
    return lax.dot_general(a, b, (((0,), (0,)), ((), ())), preferred_element_type=F32)


def _sds(shape, dtype):
    return jax.ShapeDtypeStruct(shape, dtype)


def _rms_stats(xv):
    r = lax.rsqrt(jnp.mean(xv * xv, axis=-1, keepdims=True) + RMS_EPS)
    return r, xv * r


def _swiglu_saved(gate, up):
    sig = jax.nn.sigmoid(gate)
    silu = gate * sig
    return silu, up * (sig * (1.0 + gate * (1.0 - sig))), silu * up


def _rms_bwd(dh, n, r, gain):
    dn = dh * gain
    return r * (dn - n * jnp.mean(dn * n, axis=-1, keepdims=True))


def _place():
    x, y, c = lax.axis_index("x"), lax.axis_index("y"), lax.axis_index("c")
    return x, y, c, [(1 - x, y), (x, 1 - y), (1 - x, 1 - y)]


class Cargo:
    def __init__(self, kind, arrays):
        self.kind, self.arrays = kind, list(arrays)
        n = len(self.arrays)
        self.two_level = kind in ("gather_slots", "gather_chips")
        if self.two_level:
            self.out_shape = [_sds((N_CHIPS,) + a.shape, a.dtype) for a in self.arrays]
        elif kind == "gather_devices":
            self.out_shape = [_sds((N_DEV,) + a.shape, a.dtype) for a in self.arrays]
        else:
            self.out_shape = [_sds(a.shape, a.dtype) for a in self.arrays]
        n_remote = n * {"swap": 1, "gather_devices": N_DEV - 1}.get(kind, N_CHIPS - 1)
        n_own = 0 if kind == "swap" else n
        n_relay = n_remote if self.two_level else 0
        dma = pltpu.SemaphoreType.DMA
        self.scratch = [dma((n_remote,)), dma((n_remote,)), dma((max(n_own, 1),)),
                        dma((max(n_relay, 1),)), dma((max(n_relay, 1),))]

    def _plan(self, ins, outs):
        x, y, c, chips = _place()
        q = 2 * x + y
        sibling = (x, y, 1 - c)
        own, remote, relays = [], [], []
        for a, o in zip(ins, outs):
            if self.two_level:
                half = a.shape[0] // 2
                mine = pl.ds(pl.multiple_of(c * half, SUBLANES), half)
                theirs = pl.ds(pl.multiple_of((1 - c) * half, SUBLANES), half)
                own.append((a, o.at[0 if self.kind == "gather_slots" else q]))
                for j, (px, py) in enumerate(chips):
                    there, here = (j + 1, j + 1) if self.kind == "gather_slots" else (q, 2 * px + py)
                    remote.append((a.at[mine], o.at[there, mine], o.at[here, mine], (px, py, c)))
                    relays.append((o.at[here, mine], o.at[here, mine], o.at[here, theirs], sibling))
            elif self.kind == "scatter_chips":
                own.append((a.at[q], o.at[q]))
                remote += [(a.at[2 * px + py], o.at[q], o.at[2 * px + py], (px, py, c)) for px, py in chips]
            elif self.kind == "swap":
                remote.append((a, o, o, sibling))
            else:
                own.append((a, o.at[4 * x + 2 * y + c]))
                for k in range(1, N_DEV):
                    px, py, pc = x ^ (k >> 2 & 1), y ^ (k >> 1 & 1), c ^ (k & 1)
                    remote.append((a, o.at[4 * x + 2 * y + c], o.at[4 * px + 2 * py + pc], (px, py, pc)))
        return own, remote, relays

    @staticmethod
    def _copies(entries, send_sems, recv_sems):
        out = []
        for k, (src, dst, landed, peer) in enumerate(entries):
            def make(dst_ref, k=k, src=src, peer=peer):
                return pltpu.make_async_remote_copy(src_ref=src, dst_ref=dst_ref, send_sem=send_sems.at[k],
                                                    recv_sem=recv_sems.at[k], device_id=peer, device_id_type=MESH)
            out.append((make(dst), make(landed)))
        return out

    def start(self, ins, outs, sems):
        own, remote, _ = self._plan(ins, outs)
        for k, (src, dst) in enumerate(own):
            pltpu.make_async_copy(src, dst, sems[2].at[k]).start()
        for mine, _ in self._copies(remote, sems[0], sems[1]):
            mine.start()

    def relay(self, ins, outs, sems):
        _, remote, relays = self._plan(ins, outs)
        passed = self._copies(relays, sems[3], sems[4])
        for (_, arriving), (mine, _) in zip(self._copies(remote, sems[0], sems[1]), passed):
            arriving.wait_recv()
            mine.start()

    def wait(self, ins, outs, sems):
        own, remote, relays = self._plan(ins, outs)
        for mine, arriving in self._copies(remote, sems[0], sems[1]):
            mine.wait_send()
            if not self.two_level:
                arriving.wait_recv()
        for mine, arriving in self._copies(relays, sems[3], sems[4]):
            mine.wait_send()
            arriving.wait_recv()
        for k, (src, dst) in enumerate(own):
            pltpu.make_async_copy(src, dst, sems[2].at[k]).wait()


N_CARGO_SEMS = 5


def _call(body, *, name, grid, in_specs, out_specs, out_shape, args, scratch_shapes=(), cargos=()):
    n_in, n_out, n_scr = len(in_specs), len(out_specs), len(scratch_shapes)
    c_in = [len(cg.arrays) for cg in cargos]
    n_cin = sum(c_in)

    def wrapped(*refs):
        ins = refs[:n_in]
        cins = refs[n_in:n_in + n_cin]
        outs = refs[n_in + n_cin:n_in + n_cin + n_out]
        couts = refs[n_in + n_cin + n_out:n_in + 2 * n_cin + n_out]
        scr = refs[n_in + 2 * n_cin + n_out:n_in + 2 * n_cin + n_out + n_scr]
        sems = refs[n_in + 2 * n_cin + n_out + n_scr:]
        step, n_steps = 0, 1
        for ax, size in enumerate(grid):
            step = step * size + pl.program_id(ax)
            n_steps *= size

        def each(method, only_two_level=False):
            at = 0
            for k, cg in enumerate(cargos):
                if cg.two_level or not only_two_level:
                    getattr(cg, method)(cins[at:at + c_in[k]], couts[at:at + c_in[k]],
                                        sems[N_CARGO_SEMS * k:N_CARGO_SEMS * (k + 1)])
                at += c_in[k]

        body(*ins, *outs, *scr)
        if cargos:
            pl.when(step == 0)(lambda: each("start"))
        if any(cg.two_level for cg in cargos):
            pl.when(step == (RELAY_AT_EIGHTHS * n_steps) // 8)(lambda: each("relay", only_two_level=True))
        if cargos:
            pl.when(step == n_steps - 1)(lambda: each("wait"))

    res = pl.pallas_call(
        wrapped, name=name, grid=grid,
        in_specs=list(in_specs) + [HBM] * n_cin,
        out_specs=list(out_specs) + [HBM] * n_cin,
        out_shape=list(out_shape) + [s for cg in cargos for s in cg.out_shape],
        scratch_shapes=list(scratch_shapes) + [s for cg in cargos for s in cg.scratch],
        compiler_params=pltpu.CompilerParams(dimension_semantics=("arbitrary",) * len(grid),
                                             vmem_limit_bytes=VMEM_LIMIT_BYTES),
    )(*args, *[a for cg in cargos for a in cg.arrays])
    outs, rest = list(res[:n_out]), list(res[n_out:])
    cargo_outs = []
    for k in c_in:
        cargo_outs.append(rest[:k])
        rest = rest[k:]
    return outs, cargo_outs


def _exchange(cargo, name):
    _, (outs,) = _call(lambda: None, name=name, grid=(1,), in_specs=[], out_specs=[], out_shape=[], args=[],
                       cargos=[cargo])
    return outs


def _ffn_up_gather(x, gain, wg_t, wu_t, name, cargos=()):
    t_len, d = x.shape
    fq = wg_t.shape[0]
    tm = min(TM_FFN, t_len)
    n_tiles = t_len // tm
    relay_tile = n_tiles // 2
    fetch_tile = min(relay_tile + 1, n_tiles - 1)

    def body(x_ref, g_ref, wg_in, wu_in, h_ref, s_ref, p_ref, a_ref, wg_all, wu_all,
             wg_v, wu_v, h_all, send_sems, recv_sems, pass_send_sems, pass_recv_sems, own_sems, load_sems):
        s = pl.program_id(0)
        i = pl.program_id(1)
        x_, y_, c_, chips = _place()
        shards = ((wg_in, wg_all, wg_v), (wu_in, wu_all, wu_v))
        mine = pl.ds(pl.multiple_of(c_ * (fq // 2), SUBLANES), fq // 2)
        theirs = pl.ds(pl.multiple_of((1 - c_) * (fq // 2), SUBLANES), fq // 2)

        def to_peer(k, j):
            w_in, w_all, _ = shards[k]
            return pltpu.make_async_remote_copy(
                src_ref=w_in.at[mine], dst_ref=w_all.at[j + 1, mine], send_sem=send_sems.at[3 * k + j],
                recv_sem=recv_sems.at[3 * k + j], device_id=(*chips[j], c_), device_id_type=MESH)

        def to_sibling(k, j, landing=False):
            w_all = shards[k][1]
            return pltpu.make_async_remote_copy(
                src_ref=w_all.at[j + 1, mine], dst_ref=w_all.at[j + 1, theirs if landing else mine],
                send_sem=pass_send_sems.at[3 * k + j], recv_sem=pass_recv_sems.at[3 * k + j],
                device_id=(x_, y_, 1 - c_), device_id_type=MESH)

        def keep(k):
            return pltpu.make_async_copy(shards[k][0], shards[k][1].at[0], own_sems.at[k])

        @pl.when((s == 0) & (i == 0))
        def _():
            for j in range(N_CHIPS - 1):
                for k in range(2):
                    to_peer(k, j).start()
            for k in range(2):
                keep(k).start()

        def load(k, slot):
            src = shards[k][0] if slot == 0 else shards[k][1].at[slot]
            return pltpu.make_async_copy(src, shards[k][2].at[slot % 2], load_sems.at[k])

        @pl.when((s == 0) & (i == 0))
        def _():
            for k in range(2):
                load(k, 0).start()
            for k in range(2):
                load(k, 0).wait()

        def pass_on(slot):
            for k in range(2):
                to_peer(k, slot - 1).wait_recv()
                to_sibling(k, slot - 1).start()

        def fetch(slot):
            for k in range(2):
                to_sibling(k, slot - 1, landing=True).wait_recv()
                load(k, slot).start()

        for slot in range(1, N_CHIPS):
            pl.when((s == slot - 1) & (i == relay_tile))(functools.partial(pass_on, slot))
            pl.when((s == slot - 1) & (i == fetch_tile))(functools.partial(fetch, slot))

            @pl.when((s == slot) & (i == 0))
            def _():
                for k in range(2):
                    load(k, slot).wait()

        @pl.when(s == 0)
        def _():
            _, n = _rms_stats(x_ref[...])
            h_new = (n * g_ref[...]).astype(BF16)
            h_ref[...] = h_new
            h_all[i] = h_new

        h = h_all[i]
        silu, dgate, act = _swiglu_saved(_dot_nt(h, wg_v[s % 2]), _dot_nt(h, wu_v[s % 2]))
        s_ref[...] = silu.astype(BF16)
        p_ref[...] = dgate.astype(BF16)
        a_ref[...] = act.astype(BF16)

        @pl.when((s == N_CHIPS - 1) & (i == n_tiles - 1))
        def _():
            for k in range(2):
                for j in range(N_CHIPS - 1):
                    to_peer(k, j).wait_send()
                    to_sibling(k, j).wait_send()
                keep(k).wait()

    tok = pl.BlockSpec((tm, d), lambda s, i: (jnp.where(s == 0, i, n_tiles - 1), 0))
    hid = pl.BlockSpec((None, tm, fq), lambda s, i: (s, i, 0))
    outs, cargo_outs = _call(
        body, name=name, grid=(N_CHIPS, n_tiles),
        in_specs=[tok, pl.BlockSpec((1, d), lambda s, i: (0, 0)), HBM, HBM],
        out_specs=[tok, hid, hid, hid, HBM, HBM],
        out_shape=[_sds((t_len, d), BF16)] + [_sds((N_CHIPS, t_len, fq), BF16)] * 3
        + [_sds((N_CHIPS, fq, d), BF16)] * 2,
        scratch_shapes=[pltpu.VMEM((2, fq, d), BF16), pltpu.VMEM((2, fq, d), BF16),
                        pltpu.VMEM((n_tiles, tm, d), BF16)]
        + [pltpu.SemaphoreType.DMA((6,))] * 4 + [pltpu.SemaphoreType.DMA((2,))] * 2,
        args=[x, gain, wg_t, wu_t], cargos=cargos)
    return outs, cargo_outs


def _load_once(hbm_refs, vmem_refs, sems, first):
    @pl.when(first)
    def _():
        copies = [pltpu.make_async_copy(src, dst, sems.at[k]) for k, (src, dst) in enumerate(zip(hbm_refs, vmem_refs))]
        for cp in copies:
            cp.start()
        for cp in copies:
            cp.wait()


def _ffn_down(x, act, wd, name, cargos=()):
    t_len, d = x.shape
    nq, fq, _ = wd.shape
    tm = min(TM_FFN, t_len)

    def body(x_ref, a_ref, wd_ref, xo_ref):
        y = _dot(a_ref[0], wd_ref[0])
        for j in range(1, nq):
            y = y + _dot(a_ref[j], wd_ref[j])
        xo_ref[...] = x_ref[...] + FFN_RES_WEIGHT * y

    tok = pl.BlockSpec((tm, d), lambda i: (i, 0))
    (xo,), cargo_outs = _call(
        body, name=name, grid=(t_len // tm,),
        in_specs=[tok, pl.BlockSpec((nq, tm, fq), lambda i: (0, i, 0)), pl.BlockSpec((nq, fq, d), lambda i: (0, 0, 0))],
        out_specs=[tok], out_shape=[_sds((t_len, d), F32)], args=[x, act, wd], cargos=cargos)
    return xo, cargo_outs


def _ffn_fwd(x, gain, wg_t, wu_t, wd, name):
    t_len, d = x.shape
    nq, fq, _ = wd.shape
    tm = min(TM_FFN, t_len)

    def body(x_ref, g_ref, wg_hbm, wu_hbm, wd_hbm, xo_ref, h_ref, s_ref, p_ref, a_ref,
             h_s, acc, wg_v, wu_v, wd_v, load_sems):
        i = pl.program_id(0)
        j = pl.program_id(1)
        _load_once((wg_hbm, wu_hbm, wd_hbm), (wg_v, wu_v, wd_v), load_sems, (i == 0) & (j == 0))

        @pl.when(j == 0)
        def _():
            _, n = _rms_stats(x_ref[...])
            h = (n * g_ref[...]).astype(BF16)
            h_s[...] = h
            h_ref[...] = h
            acc[...] = jnp.zeros_like(acc)

        h = h_s[...]
        y = None
        for jj in range(SLOTS_PER_STEP):
            slot = j * SLOTS_PER_STEP + jj
            silu, dgate, act = _swiglu_saved(_dot_nt(h, wg_v[slot]), _dot_nt(h, wu_v[slot]))
            s_ref[jj] = silu.astype(BF16)
            p_ref[jj] = dgate.astype(BF16)
            a_ref[jj] = act.astype(BF16)
            part = _dot(a_ref[jj], wd_v[slot])
            y = part if y is None else y + part
        acc[...] += y

        @pl.when(j == nq // SLOTS_PER_STEP - 1)
        def _():
            xo_ref[...] = x_ref[...] + FFN_RES_WEIGHT * acc[...]

    tok = pl.BlockSpec((tm, d), lambda i, j: (i, 0))
    hid = pl.BlockSpec((SLOTS_PER_STEP, tm, fq), lambda i, j: (j, i, 0))
    outs, _ = _call(
        body, name=name, grid=(t_len // tm, nq // SLOTS_PER_STEP),
        in_specs=[tok, pl.BlockSpec((1, d), lambda i, j: (0, 0)), HBM, HBM, HBM],
        out_specs=[tok, tok, hid, hid, hid],
        out_shape=[_sds((t_len, d), F32), _sds((t_len, d), BF16)] + [_sds((nq, t_len, fq), BF16)] * 3,
        scratch_shapes=[pltpu.VMEM((tm, d), BF16), pltpu.VMEM((tm, d), F32)]
        + [pltpu.VMEM((nq, fq, d), BF16)] * 3 + [pltpu.SemaphoreType.DMA((3,))],
        args=[x, gain, wg_t, wu_t, wd])
    return outs


def _ffn_bwd(dy, x_in, gain, silu, dgate_du, wg_t, wu_t, wd, name):
    t_len, d = dy.shape
    nq, fq, _ = wd.shape
    tm = min(TM_FFN, t_len)

    def body(dy_ref, x_ref, g_ref, s_ref, p_ref, wg_hbm, wu_hbm, wd_hbm,
             dx_ref, dgain_ref, df_ref, dg_ref, du_ref, df_s, dh_acc, dact_s, wg_v, wu_v, wd_v, load_sems):
        i = pl.program_id(0)
        j = pl.program_id(1)
        _load_once((wg_hbm, wu_hbm, wd_hbm), (wg_v, wu_v, wd_v), load_sems, (i == 0) & (j == 0))

        @pl.when((i == 0) & (j == 0))
        def _():
            dgain_ref[...] = jnp.zeros_like(dgain_ref)

        @pl.when(j == 0)
        def _():
            df = (FFN_RES_WEIGHT * dy_ref[...]).astype(BF16)
            df_s[...] = df
            df_ref[...] = df
            dh_acc[...] = jnp.zeros_like(dh_acc)

        half = tm // 2
        for r0 in (0, half):
            dact_s[r0:r0 + half, :] = _dot_nt(df_s[r0:r0 + half, :], wd_v[j])

        for r0 in range(0, tm, STRIP):
            dact = dact_s[r0:r0 + STRIP, :]
            dg_ref[r0:r0 + STRIP, :] = (dact * p_ref[r0:r0 + STRIP, :].astype(F32)).astype(BF16)
            du_ref[r0:r0 + STRIP, :] = (dact * s_ref[r0:r0 + STRIP, :].astype(F32)).astype(BF16)

        for r0 in (0, half):
            rows = slice(r0, r0 + half)
            dh_acc[rows, :] += _dot(dg_ref[rows, :], wg_v[j]) + _dot(du_ref[rows, :], wu_v[j])

        @pl.when(j == nq - 1)
        def _():
            r, n = _rms_stats(x_ref[...])
            dh = dh_acc[...]
            dgain_ref[...] += jnp.sum(dh * n, axis=0, keepdims=True)
            dx_ref[...] = dy_ref[...] + _rms_bwd(dh, n, r, g_ref[...])

    tok = pl.BlockSpec((tm, d), lambda i, j: (i, 0))
    vec = pl.BlockSpec((1, d), lambda i, j: (0, 0))
    hid = pl.BlockSpec((None, tm, fq), lambda i, j: (j, i, 0))
    outs, _ = _call(
        body, name=name, grid=(t_len // tm, nq),
        in_specs=[tok, tok, vec, hid, hid, HBM, HBM, HBM],
        out_specs=[tok, vec, tok, hid, hid],
        out_shape=[_sds((t_len, d), F32), _sds((1, d), F32), _sds((t_len, d), BF16),
                   _sds((nq, t_len, fq), BF16), _sds((nq, t_len, fq), BF16)],
        scratch_shapes=[pltpu.VMEM((tm, d), BF16), pltpu.VMEM((tm, d), F32), pltpu.VMEM((tm, fq), F32)]
        + [pltpu.VMEM((nq, fq, d), BF16)] * 3 + [pltpu.SemaphoreType.DMA((3,))],
        args=[dy, x_in, gain, silu, dgate_du, wg_t, wu_t, wd])
    return outs


def _wgrad(lhs, rhs, l_spec, r_spec, out_shape, out_spec, acc_shape, grid, name, cargos=()):
    n_t = grid[-1]
    t_axis = len(grid) - 1

    def body(l_ref, r_ref, o_ref, acc):
        t = pl.program_id(t_axis)

        @pl.when(t == 0)
        def _():
            acc[...] = jnp.zeros_like(acc)

        acc[...] += _dot_tn(l_ref[...].astype(BF16), r_ref[...].astype(BF16))

        @pl.when(t == n_t - 1)
        def _():
            o_ref[...] = acc[...].astype(o_ref.dtype)

    (out,), cargo_outs = _call(
        body, name=name, grid=grid, in_specs=[l_spec, r_spec], out_specs=[out_spec], out_shape=[out_shape],
        scratch_shapes=[pltpu.VMEM(acc_shape, F32)], args=[lhs, rhs], cargos=cargos)
    return out, cargo_outs


def _wgrad_hid_tok_scatter(hid, tok, name, cargos=()):
    t_len, d = tok.shape
    nq, _, fq = hid.shape
    half = fq // 2
    tt = min(TT_WGRAD, t_len)
    n_t = t_len // tt

    def body(l_ref, r_ref, parts_ref, acc, stage, pair, summed, zeros,
             pair_send_sems, pair_recv_sems, send_sems, recv_sems, own_sem, zero_sems):
        g = pl.program_id(0)
        t = pl.program_id(1)
        x_, y_, c_, chips = _place()
        mine = pl.ds(pl.multiple_of(c_ * half, STRIP), half)
        theirs = pl.ds(pl.multiple_of((1 - c_) * half, STRIP), half)

        def to_sibling(slot):
            return pltpu.make_async_remote_copy(
                src_ref=stage.at[slot, theirs], dst_ref=pair.at[slot], send_sem=pair_send_sems.at[slot],
                recv_sem=pair_recv_sems.at[slot], device_id=(x_, y_, 1 - c_), device_id_type=MESH)

        def to_peer(j):
            return pltpu.make_async_remote_copy(
                src_ref=summed.at[j + 1], dst_ref=parts_ref.at[j + 1, mine], send_sem=send_sems.at[j],
                recv_sem=recv_sems.at[j], device_id=(*chips[j], c_), device_id_type=MESH)

        keep = pltpu.make_async_copy(summed.at[0], parts_ref.at[0, mine], own_sem)

        def blank(slot):
            return pltpu.make_async_copy(zeros, parts_ref.at[slot, theirs], zero_sems.at[slot])

        @pl.when((g == 0) & (t == 0))
        def _():
            zeros[...] = jnp.zeros_like(zeros)
            for slot in range(nq):
                blank(slot).start()

        @pl.when(t == 0)
        def _():
            acc[...] = jnp.zeros_like(acc)

        acc[...] += _dot_tn(l_ref[...], r_ref[...])

        for step in range(nq):
            slot = (step + 1) % nq

            @pl.when((g == step) & (t == n_t - 1))
            def _():
                stage[slot] = acc[...].astype(BF16)
                to_sibling(slot).start()
                to_sibling(slot).wait_recv()
                summed[slot] = (stage[slot, mine, :].astype(F32) + pair[slot].astype(F32)).astype(BF16)
                if slot > 0:
                    to_peer(slot - 1).start()
                else:
                    keep.start()

        @pl.when((g == nq - 1) & (t == n_t - 1))
        def _():
            for j in range(N_CHIPS - 1):
                to_peer(j).wait()
            keep.wait()
            for slot in range(nq):
                to_sibling(slot).wait_send()
                blank(slot).wait()

    (parts,), cargo_outs = _call(
        body, name=name, grid=(nq, n_t),
        in_specs=[pl.BlockSpec((None, tt, fq), lambda g, t: ((g + 1) % nq, t, 0)),
                  pl.BlockSpec((tt, d), lambda g, t: (t, 0))],
        out_specs=[HBM], out_shape=[_sds((nq, fq, d), BF16)],
        scratch_shapes=[pltpu.VMEM((fq, d), F32), pltpu.VMEM((nq, fq, d), BF16), pltpu.VMEM((nq, half, d), BF16),
                        pltpu.VMEM((nq, half, d), BF16), pltpu.VMEM((half, d), BF16),
                        pltpu.SemaphoreType.DMA((nq,)), pltpu.SemaphoreType.DMA((nq,)),
                        pltpu.SemaphoreType.DMA((N_CHIPS - 1,)), pltpu.SemaphoreType.DMA((N_CHIPS - 1,)),
                        pltpu.SemaphoreType.DMA(()), pltpu.SemaphoreType.DMA((nq,))],
        args=[hid, tok], cargos=cargos)
    return parts, cargo_outs


def _wgrad_2d(lhs, rhs, n_col_blocks, out_dtype, name, group_diag=False, cargos=()):
    t_len, k = lhs.shape
    n = rhs.shape[1]
    nb = n // n_col_blocks
    kb = k // n_col_blocks if group_diag else k
    tt = min(TT_WGRAD, t_len)
    l_map = (lambda q, t: (t, q)) if group_diag else (lambda q, t: (t, 0))
    return _wgrad(lhs, rhs,
                  pl.BlockSpec((tt, kb), l_map),
                  pl.BlockSpec((tt, nb), lambda q, t: (t, q)),
                  _sds((n_col_blocks, kb, nb), out_dtype),
                  pl.BlockSpec((None, kb, nb), lambda q, t: (q, 0, 0)),
                  (kb, nb), (n_col_blocks, t_len // tt), name, cargos)


def _layernorm_stats(u1):
    mu = jnp.mean(u1, axis=-1, keepdims=True)
    xc = u1 - mu
    rstd = lax.rsqrt(jnp.mean(xc * xc, axis=-1, keepdims=True) + LN_EPS)
    return rstd, xc * rstd


def _positions(i, tm, rows, offset=0):
    return (lax.broadcasted_iota(jnp.int32, (rows, 1), 0) + (i * tm + offset)).astype(F32)


SHIFT_ROWS = HALO - SUBLANES


def _fill_shifted(ext_s, sh_s, tm):
    for b in range(1, SUBLANES):
        sh_s[b - 1] = ext_s[pl.ds(b, tm + SHIFT_ROWS), :]


def _window(ext_s, sh_s, shift, tm):
    a, b = divmod(shift, SUBLANES)
    if b == 0:
        return ext_s[pl.ds(shift, tm), :]
    return sh_s[b - 1, pl.ds(a * SUBLANES, tm), :]


def _tile(tm, cols):
    return pl.BlockSpec((tm, cols), lambda i: (i, 0))


def _whole(shape):
    return pl.BlockSpec(shape, lambda i: (0,) * len(shape))


def _mix_fwd(x1, gain, w_in, conv_dw, conv_b, ln_g, ln_b, conv_pw, pool_w, pool_scale, w_out, name, cargos=()):
    t_len, d = x1.shape
    nq, _, nb = w_in.shape
    tm = min(TM_MIX, t_len)

    def body(x_ref, g_ref, wi_ref, dw_ref, cb_ref, lg_ref, lb_ref, pw_ref, plw_ref, ps_ref, wo_ref,
             x2_ref, h_ref, p_ref, u1_ref, u3_ref, mx_ref, cat_ref, ext_s, pext_s, sh_s, tail_s):
        i = pl.program_id(0)

        @pl.when(i == 0)
        def _():
            tail_s[...] = jnp.zeros_like(tail_s)

        _, n = _rms_stats(x_ref[...])
        h = (n * g_ref[...]).astype(BF16)
        h_ref[...] = h
        for q in range(nq):
            p_ref[:, q * nb:(q + 1) * nb] = _dot(h, wi_ref[q])

        a = p_ref[:, 0:D_CONV]
        g = p_ref[:, D_CONV:2 * D_CONV]
        p = p_ref[:, 2 * D_CONV:]
        ext_s[0:HALO, :] = tail_s[:, 0:D_CONV] * jax.nn.sigmoid(tail_s[:, D_CONV:2 * D_CONV])
        ext_s[HALO:, :] = a * jax.nn.sigmoid(g)
        pext_s[0:HALO, :] = tail_s[:, 2 * D_CONV:]
        pext_s[HALO:, :] = p
        tail_s[...] = p_ref[tm - HALO:tm, :]

        _fill_shifted(ext_s, sh_s, tm)
        u1 = jnp.broadcast_to(cb_ref[...], (tm, D_CONV))
        for k in range(CONV_WIDTH):
            u1 = u1 + dw_ref[k:k + 1, :] * _window(ext_s, sh_s, HALO - (CONV_WIDTH - 1) + k, tm)
        u1_ref[...] = u1
        _, nhat = _layernorm_stats(u1)
        u2 = nhat * lg_ref[...] + lb_ref[...]
        u3 = (u2 * jax.nn.sigmoid(u2)).astype(BF16)
        u3_ref[...] = u3
        cat_ref[:, 0:D_CONV] = _dot(u3, pw_ref[...]).astype(BF16)

        pos1 = _positions(i, tm, tm) + 1.0
        for gi, w in enumerate(POOL_WINDOWS):
            cols = slice(gi * POOL_GROUP, (gi + 1) * POOL_GROUP)
            s = pext_s[pl.ds(HALO, tm), cols]
            for j in range(1, w):
                s = s + pext_s[pl.ds(HALO - j, tm), cols]
            mixed = (s / jnp.minimum(pos1, float(w)) - p[:, cols]).astype(BF16)
            mx_ref[:, cols] = mixed
            out = _dot(mixed, plw_ref[gi]) * ps_ref[:, cols]
            cat_ref[:, D_CONV + gi * POOL_GROUP:D_CONV + (gi + 1) * POOL_GROUP] = out.astype(BF16)

        x2_ref[...] = x_ref[...] + _dot(cat_ref[...], wo_ref[...])

    return _call(
        body, name=name, grid=(t_len // tm,),
        in_specs=[_tile(tm, d), _whole((1, d)), _whole((nq, d, nb)), _whole((CONV_WIDTH + 1, D_CONV)),
                  _whole((1, D_CONV)), _whole((1, D_CONV)), _whole((1, D_CONV)), _whole((D_CONV, D_CONV)),
                  _whole((4, POOL_GROUP, POOL_GROUP)), _whole((1, D_POOL)), _whole((D_CONV + D_POOL, d))],
        out_specs=[_tile(tm, d), _tile(tm, d), _tile(tm, D_IN), _tile(tm, D_CONV), _tile(tm, D_CONV),
                   _tile(tm, D_POOL), _tile(tm, D_CONV + D_POOL)],
        out_shape=[_sds((t_len, d), F32), _sds((t_len, d), BF16), _sds((t_len, D_IN), F32),
                   _sds((t_len, D_CONV), F32), _sds((t_len, D_CONV), BF16), _sds((t_len, D_POOL), BF16),
                   _sds((t_len, D_CONV + D_POOL), BF16)],
        scratch_shapes=[pltpu.VMEM((tm + HALO, D_CONV), F32), pltpu.VMEM((tm + HALO, D_POOL), F32),
                        pltpu.VMEM((SUBLANES - 1, tm + SHIFT_ROWS, D_CONV), F32), pltpu.VMEM((HALO, D_IN), F32)],
        args=[x1, gain, w_in, conv_dw, conv_b, ln_g, ln_b, conv_pw, pool_w, pool_scale, w_out], cargos=cargos)


def _mix_bwd(dx2, u1, mixed, proj, x1, gain, conv_dw, ln_g, ln_b, conv_pw, pool_w, pool_scale, w_out, w_in,
             name, cargos=()):
    t_len, d = x1.shape
    nq, _, nb = w_in.shape
    tm = min(TM_MIX, t_len)
    hb = tm // HALO
    n_tiles = t_len // tm

    def body(dxn_ref, u1_ref, mx_ref, p_ref, tail_ref, x_ref, dx2_ref, g_ref, dw_ref, lg_ref, lb_ref, pw_ref,
             plw_ref, ps_ref, wo_ref, wi_ref,
             dx1_ref, dp_ref, dco_ref, dpo_ref, ddw_ref, dcb_ref, dlg_ref, dlb_ref, dps_ref, dgain_ref,
             du_s, dm_s, uext_s, dext_s, mext_s, ush_s, dsh_s):
        k = pl.program_id(0)

        @pl.when(k == 0)
        def _():
            for ref in (ddw_ref, dcb_ref, dlg_ref, dlb_ref, dps_ref, dgain_ref, du_s, dm_s):
                ref[...] = jnp.zeros_like(ref)

        counts = jnp.where(k < n_tiles, 1.0, 0.0)
        dcat = _dot_nt(dxn_ref[...].astype(BF16), wo_ref[...])
        dco = dcat[:, 0:D_CONV].astype(BF16)
        dco_ref[...] = dco
        du3 = _dot_nt(dco, pw_ref[...])
        rstd, nhat = _layernorm_stats(u1_ref[...])
        u2 = nhat * lg_ref[...] + lb_ref[...]
        sig = jax.nn.sigmoid(u2)
        du2 = du3 * (sig * (1.0 + u2 * (1.0 - sig)))
        dlg_ref[...] += counts * jnp.sum(du2 * nhat, axis=0, keepdims=True)
        dlb_ref[...] += counts * jnp.sum(du2, axis=0, keepdims=True)
        dnhat = du2 * lg_ref[...]
        du_s[k % 2] = rstd * (dnhat - jnp.mean(dnhat, axis=-1, keepdims=True)
                              - nhat * jnp.mean(dnhat * nhat, axis=-1, keepdims=True))
        for gi in range(len(POOL_WINDOWS)):
            cols = slice(gi * POOL_GROUP, (gi + 1) * POOL_GROUP)
            dpo = dcat[:, D_CONV + gi * POOL_GROUP:D_CONV + (gi + 1) * POOL_GROUP]
            pre = _dot(mx_ref[:, cols], plw_ref[gi])
            dps_ref[:, cols] += counts * jnp.sum(dpo * pre, axis=0, keepdims=True)
            dout = (dpo * ps_ref[:, cols]).astype(BF16)
            dpo_ref[:, cols] = dout
            dm_s[k % 2, :, cols] = _dot_nt(dout, plw_ref[gi])

        i = jnp.maximum(k - 1, 0)
        cur, nxt = (k + 1) % 2, k % 2
        first = k <= 1
        last = (k == n_tiles) | (k == 0)
        a = p_ref[:, 0:D_CONV]
        g = p_ref[:, D_CONV:2 * D_CONV]
        sg = jax.nn.sigmoid(g)
        ta = tail_ref[:, 0:D_CONV]
        tg = tail_ref[:, D_CONV:2 * D_CONV]
        uext_s[0:HALO, :] = jnp.where(first, 0.0, ta * jax.nn.sigmoid(tg))
        uext_s[HALO:, :] = a * sg
        du1 = du_s[cur]
        dext_s[0:tm, :] = du1
        dext_s[tm:, :] = jnp.where(last, 0.0, du_s[nxt, 0:HALO, :])

        _fill_shifted(uext_s, ush_s, tm)
        _fill_shifted(dext_s, dsh_s, tm)
        du0 = jnp.zeros((tm, D_CONV), F32)
        for tap in range(CONV_WIDTH):
            du0 = du0 + dw_ref[tap:tap + 1, :] * _window(dext_s, dsh_s, CONV_WIDTH - 1 - tap, tm)
            ddw_ref[tap:tap + 1, :] += jnp.sum(
                du1 * _window(uext_s, ush_s, HALO - (CONV_WIDTH - 1) + tap, tm), axis=0, keepdims=True)
        dcb_ref[...] += jnp.sum(du1, axis=0, keepdims=True)
        dp_ref[:, 0:D_CONV] = (du0 * sg).astype(BF16)
        dp_ref[:, D_CONV:2 * D_CONV] = (du0 * a * sg * (1.0 - sg)).astype(BF16)

        pos1 = _positions(i, tm, tm) + 1.0
        pos1_next = _positions(i, tm, HALO, offset=tm) + 1.0
        for gi, w in enumerate(POOL_WINDOWS):
            cols = slice(gi * POOL_GROUP, (gi + 1) * POOL_GROUP)
            dm = dm_s[cur, :, cols]
            mext_s[0:tm, cols] = dm / jnp.minimum(pos1, float(w))
            mext_s[tm:, cols] = jnp.where(last, 0.0, dm_s[nxt, 0:HALO, cols] / jnp.minimum(pos1_next, float(w)))
            s = mext_s[pl.ds(0, tm), cols]
            for j in range(1, w):
                s = s + mext_s[pl.ds(j, tm), cols]
            dp_ref[:, 2 * D_CONV + gi * POOL_GROUP:2 * D_CONV + (gi + 1) * POOL_GROUP] = (s - dm).astype(BF16)

        dh = _dot_nt(dp_ref[:, 0:nb], wi_ref[0])
        for q in range(1, nq):
            dh = dh + _dot_nt(dp_ref[:, q * nb:(q + 1) * nb], wi_ref[q])
        r, n = _rms_stats(x_ref[...])
        dgain_ref[...] += jnp.sum(dh * n, axis=0, keepdims=True)
        dx1_ref[...] = dx2_ref[...] + _rms_bwd(dh, n, r, g_ref[...])

    def ahead(cols):
        return pl.BlockSpec((tm, cols), lambda k: (jnp.minimum(k, n_tiles - 1), 0))

    def behind(cols):
        return pl.BlockSpec((tm, cols), lambda k: (jnp.maximum(k - 1, 0), 0))

    vec = _whole((1, D_CONV))
    return _call(
        body, name=name, grid=(n_tiles + 1,),
        in_specs=[ahead(d), ahead(D_CONV), ahead(D_POOL), behind(D_IN),
                  pl.BlockSpec((HALO, D_IN), lambda k: (jnp.maximum(jnp.maximum(k - 1, 0) * hb - 1, 0), 0)),
                  behind(d), behind(d), _whole((1, d)), _whole((CONV_WIDTH + 1, D_CONV)), vec, vec,
                  _whole((D_CONV, D_CONV)), _whole((4, POOL_GROUP, POOL_GROUP)), vec,
                  _whole((D_CONV + D_POOL, d)), _whole((nq, d, nb))],
        out_specs=[behind(d), behind(D_IN), ahead(D_CONV), ahead(D_POOL), _whole((CONV_WIDTH + 1, D_CONV)), vec,
                   vec, vec, vec, _whole((1, d))],
        out_shape=[_sds((t_len, d), F32), _sds((t_len, D_IN), BF16), _sds((t_len, D_CONV), BF16),
                   _sds((t_len, D_POOL), BF16), _sds((CONV_WIDTH + 1, D_CONV), F32), _sds((1, D_CONV), F32),
                   _sds((1, D_CONV), F32), _sds((1, D_CONV), F32), _sds((1, D_POOL), F32), _sds((1, d), F32)],
        scratch_shapes=[pltpu.VMEM((2, tm, D_CONV), F32), pltpu.VMEM((2, tm, D_POOL), F32),
                        pltpu.VMEM((tm + HALO, D_CONV), F32), pltpu.VMEM((tm + HALO, D_CONV), F32),
                        pltpu.VMEM((tm + HALO, D_POOL), F32),
                        pltpu.VMEM((SUBLANES - 1, tm + SHIFT_ROWS, D_CONV), F32),
                        pltpu.VMEM((SUBLANES - 1, tm + SHIFT_ROWS, D_CONV), F32)],
        args=[dx2, u1, mixed, proj, proj, x1, dx2, gain, conv_dw, ln_g, ln_b, conv_pw, pool_w, pool_scale, w_out,
              w_in], cargos=cargos)


def _final_norm_loss(x3, target, gain, name):
    t_len, d = x3.shape
    tm = min(TM_FFN, t_len)

    def body(x_ref, t_ref, g_ref, dx_ref, loss_ref, dgain_ref):
        @pl.when(pl.program_id(0) == 0)
        def _():
            loss_ref[...] = jnp.zeros_like(loss_ref)
            dgain_ref[...] = jnp.zeros_like(dgain_ref)

        r, n = _rms_stats(x_ref[...])
        err = n * g_ref[...] - t_ref[...]
        per_tok = jnp.sum(err * err, axis=-1, keepdims=True) * (1.0 / d)
        loss_ref[...] += 0.5 * jnp.sum(per_tok, axis=0, keepdims=True)
        dy = err * (1.0 / d)
        dgain_ref[...] += jnp.sum(dy * n, axis=0, keepdims=True)
        dx_ref[...] = _rms_bwd(dy, n, r, g_ref[...])

    tok = pl.BlockSpec((tm, d), lambda i: (i, 0))
    outs, _ = _call(
        body, name=name, grid=(t_len // tm,),
        in_specs=[tok, tok, pl.BlockSpec((1, d), lambda i: (0, 0))],
        out_specs=[tok, pl.BlockSpec((1, 128), lambda i: (0, 0)), pl.BlockSpec((1, d), lambda i: (0, 0))],
        out_shape=[_sds((t_len, d), F32), _sds((1, 128), F32), _sds((1, d), F32)],
        args=[x3, target, gain])
    return outs


def _row_tile(rows):
    return rows // 4 if rows % 64 == 0 else rows


def _adamw_math(w, g, m, v):
    m = ADAM_B1 * m + (1.0 - ADAM_B1) * g
    v = ADAM_B2 * v + (1.0 - ADAM_B2) * (g * g)
    m_hat = m / (1.0 - ADAM_B1 ** ADAM_STEP)
    v_hat = v / (1.0 - ADAM_B2 ** ADAM_STEP)
    delta = -ADAM_LR * (m_hat / (jnp.sqrt(v_hat) + ADAM_EPS) + ADAM_WD * w)
    return delta, m, v


def _adamw(parts, w, m, v, name):
    r, c = w.shape
    n = len(parts)
    tr = _row_tile(r)

    def body(*refs):
        g = None
        for p_ref in refs[:n]:
            s = p_ref[0].astype(F32)
            for k in range(1, p_ref.shape[0]):
                s = s + p_ref[k].astype(F32)
            g = s if g is None else g + s
        w_ref, m_ref, v_ref, g_out, d_out, m_out, v_out = refs[n:]
        delta, nm, nv = _adamw_math(w_ref[...], g, m_ref[...], v_ref[...])
        g_out[...] = g
        d_out[...] = delta
        m_out[...] = nm
        v_out[...] = nv

    blk = pl.BlockSpec((tr, c), lambda i: (i, 0))
    p_specs = [pl.BlockSpec((p.shape[0], tr, c), lambda i: (0, i, 0)) for p in parts]
    outs, _ = _call(body, name=name, grid=(r // tr,), in_specs=p_specs + [blk, blk, blk],
                    out_specs=[blk] * 4, out_shape=[_sds((r, c), F32)] * 4, args=[*parts, w, m, v])
    return outs


FFN_W = ("w_gate", "w_up", "w_down")
MID = ("w_in", "conv_dw", "conv_pw", "w_out")
SMALL_1024 = ("ffn1_norm", "mix_norm", "ffn2_norm", "final_norm")
SMALL_512 = ("conv_dw_b", "conv_ln_g", "conv_ln_b", "pool_scale")
WEIGHTS = ("ffn1_norm", "ffn1_w_gate", "ffn1_w_up", "ffn1_w_down", "mix_norm", "w_in", "conv_dw", "conv_dw_b",
           "conv_ln_g", "conv_ln_b", "conv_pw", "pool_w", "pool_scale", "w_out", "ffn2_norm", "ffn2_w_gate",
           "ffn2_w_up", "ffn2_w_down", "final_norm")
PACK_ROWS = 72
PACK_LOSS_ROW = 70


def _pad_rows(a, rows):
    return jnp.pad(a, ((0, rows - a.shape[0]), (0, 0)))


def _pack_small(t, spare=None):
    rows = [t[k].reshape(1, D_MODEL) for k in SMALL_1024]
    rows.append(jnp.concatenate([t["conv_dw_b"].reshape(1, -1), t["conv_ln_g"].reshape(1, -1)], axis=1))
    rows.append(jnp.concatenate([t["conv_ln_b"].reshape(1, -1), t["pool_scale"].reshape(1, -1)], axis=1))
    rows.append(t["pool_w"].reshape(64, D_MODEL))
    if spare is not None:
        rows.append(jnp.pad(spare, ((0, 0), (0, D_MODEL - spare.shape[1]))))
    return _pad_rows(jnp.concatenate(rows, axis=0), PACK_ROWS)


def _unpack_small(p):
    out = {k: p[i] for i, k in enumerate(SMALL_1024)}
    out["conv_dw_b"], out["conv_ln_g"] = p[4, :D_CONV], p[4, D_CONV:]
    out["conv_ln_b"], out["pool_scale"] = p[5, :D_CONV], p[5, D_CONV:]
    out["pool_w"] = p[6:70].reshape(4, POOL_GROUP, POOL_GROUP)
    return out


def _as_stored(name, a):
    if name.endswith(("w_gate", "w_up")):
        return a.T
    if name == "conv_dw":
        return _pad_rows(a, CONV_WIDTH + 1)
    return a


def _as_given(name, a):
    if name.endswith(("w_gate", "w_up")):
        return a.T
    if name == "conv_dw":
        return a[:CONV_WIDTH]
    return a


def kernel(x, ffn1_norm, ffn1_w_gate, ffn1_w_up, ffn1_w_down, mix_norm, w_in, conv_dw, conv_dw_b, conv_ln_g, conv_ln_b, conv_pw, pool_w, pool_scale, w_out, ffn2_norm, ffn2_w_gate, ffn2_w_up, ffn2_w_down, final_norm, loss_target, m_ffn1_norm, m_ffn1_w_gate, m_ffn1_w_up, m_ffn1_w_down, m_mix_norm, m_w_in, m_conv_dw, m_conv_dw_b, m_conv_ln_g, m_conv_ln_b, m_conv_pw, m_pool_w, m_pool_scale, m_w_out, m_ffn2_norm, m_ffn2_w_gate, m_ffn2_w_up, m_ffn2_w_down, m_final_norm, v_ffn1_norm, v_ffn1_w_gate, v_ffn1_w_up, v_ffn1_w_down, v_mix_norm, v_w_in, v_conv_dw, v_conv_dw_b, v_conv_ln_g, v_conv_ln_b, v_conv_pw, v_pool_w, v_pool_scale, v_w_out, v_ffn2_norm, v_ffn2_w_gate, v_ffn2_w_up, v_ffn2_w_down, v_final_norm):
    given = dict(locals())
    wts = {k: given[k] for k in WEIGHTS}
    mom_m = {k: given["m_" + k] for k in WEIGHTS}
    mom_v = {k: given["v_" + k] for k in WEIGHTS}
    xt, target = x[0], loss_target[0]

    shard = {k: _as_stored(k, wts[k]) if k == "conv_dw" else _as_stored(k, wts[k]).astype(BF16)
             for k in WEIGHTS if k.endswith(FFN_W) or k in MID}
    w = {k: wts[k].reshape(1, -1) for k in SMALL_1024 + SMALL_512}
    w["pool_w"] = wts["pool_w"].astype(BF16)

    (h1, s1, p1, a1, w["ffn1_w_gate"], w["ffn1_w_up"]), ((w["ffn1_w_down"],),) = _ffn_up_gather(
        xt, w["ffn1_norm"], shard["ffn1_w_gate"], shard["ffn1_w_up"], "ffn1_up_gather",
        cargos=[Cargo("gather_slots", [shard["ffn1_w_down"]])])
    x1, (mid, (w["ffn2_w_down"],)) = _ffn_down(
        xt, a1, w["ffn1_w_down"], "ffn1_down",
        cargos=[Cargo("gather_chips", [shard[k] for k in MID]), Cargo("gather_slots", [shard["ffn2_w_down"]])])
    w["w_in"] = mid[0]
    w["conv_dw"] = mid[1].transpose(1, 0, 2).reshape(CONV_WIDTH + 1, D_CONV)
    w["conv_pw"] = mid[2].reshape(D_CONV, D_CONV)
    w["w_out"] = mid[3].reshape(D_CONV + D_POOL, D_MODEL)
    (x2, h2, proj, u1, u3, mixed, cat), ((w["ffn2_w_gate"], w["ffn2_w_up"]),) = _mix_fwd(
        x1, w["mix_norm"], w["w_in"], w["conv_dw"], w["conv_dw_b"], w["conv_ln_g"], w["conv_ln_b"], w["conv_pw"],
        w["pool_w"], w["pool_scale"], w["w_out"], "mix_fwd",
        cargos=[Cargo("gather_slots", [shard["ffn2_w_gate"], shard["ffn2_w_up"]])])
    x3, h3, s2, p2, a2 = _ffn_fwd(x2, w["ffn2_norm"], w["ffn2_w_gate"], w["ffn2_w_up"], w["ffn2_w_down"], "ffn2_fwd")
    dx3, loss_share, d_final = _final_norm_loss(x3, target, w["final_norm"], "final_norm_loss")

    g = {"final_norm": d_final}
    sums = {}

    def landed(names, parts):
        sums.update(zip(names, parts))

    dx2, g["ffn2_norm"], df2, dg2, du2 = _ffn_bwd(dx3, x2, w["ffn2_norm"], s2, p2, w["ffn2_w_gate"],
                                                   w["ffn2_w_up"], w["ffn2_w_down"], "ffn2_bwd")
    def ffn_wgrad(name, hid, tok, cargos=()):
        parts, cargo_outs = _wgrad_hid_tok_scatter(hid, tok, name.replace("_w_", "_dw_"), cargos=cargos)
        landed([name], [parts])
        return cargo_outs

    ffn_wgrad("ffn2_w_gate", dg2, h3)
    ffn_wgrad("ffn2_w_up", du2, h3)
    ffn_wgrad("ffn2_w_down", a2, df2)
    (dx1, dproj, dco, dpo, g_dw, g["conv_dw_b"], g["conv_ln_g"], g["conv_ln_b"], g["pool_scale"],
     g["mix_norm"]), (swapped2,) = _mix_bwd(
        dx2, u1, mixed, proj, x1, w["mix_norm"], w["conv_dw"], w["conv_ln_g"], w["conv_ln_b"], w["conv_pw"],
        w["pool_w"], w["pool_scale"], w["w_out"], w["w_in"], "mix_bwd",
        cargos=[Cargo("swap", [sums["ffn2_" + k] for k in FFN_W])])
    g_out, _ = _wgrad_2d(cat, dx2, 1, BF16, "dw_out")
    g_pw, _ = _wgrad_2d(u3, dco, 1, BF16, "dconv_pw")
    g["pool_w"], _ = _wgrad_2d(mixed, dpo, 4, F32, "dpool_w", group_diag=True)
    slabs = [g_pw.reshape(N_CHIPS, D_CONV // N_CHIPS, D_CONV),
             g_out.reshape(N_CHIPS, (D_CONV + D_POOL) // N_CHIPS, D_MODEL)]
    g_in, (parts,) = _wgrad_2d(h2, dproj, N_CHIPS, BF16, "dw_in", cargos=[Cargo("scatter_chips", slabs)])
    landed(["conv_pw", "w_out"], parts)
    dx, g["ffn1_norm"], df1, dg1, du1_ = _ffn_bwd(dx1, xt, w["ffn1_norm"], s1, p1, w["ffn1_w_gate"],
                                                   w["ffn1_w_up"], w["ffn1_w_down"], "ffn1_bwd")
    slabs = [g_in, g_dw.reshape(CONV_WIDTH + 1, N_CHIPS, D_CONV // N_CHIPS).transpose(1, 0, 2)]
    (parts,) = ffn_wgrad("ffn1_w_gate", dg1, h1, cargos=[Cargo("scatter_chips", slabs)])
    landed(["w_in", "conv_dw"], parts)
    swapped_mid, swapped_gate, small_parts = ffn_wgrad(
        "ffn1_w_up", du1_, h1,
        cargos=[Cargo("swap", [sums[k] for k in MID]), Cargo("swap", [sums["ffn1_w_gate"]]),
                Cargo("gather_devices", [_pack_small(g, spare=loss_share)])])
    (swapped_up,) = ffn_wgrad("ffn1_w_down", a1, df1, cargos=[Cargo("swap", [sums["ffn1_w_up"]])])
    swapped_down = _exchange(Cargo("swap", [sums["ffn1_w_down"]]), "swap_last")

    theirs = dict(zip(["ffn2_" + k for k in FFN_W], swapped2))
    theirs.update(zip(MID, swapped_mid))
    theirs.update(ffn1_w_gate=swapped_gate[0], ffn1_w_up=swapped_up[0], ffn1_w_down=swapped_down[0])
    grads, deltas, new_m, new_v = {}, {}, {}, {}
    for k in theirs:
        res = _adamw([sums[k], theirs[k]], _as_stored(k, wts[k]), _as_stored(k, mom_m[k]),
                     _as_stored(k, mom_v[k]), "adamw_" + k)
        grads[k], deltas[k], new_m[k], new_v[k] = [_as_given(k, t) for t in res]
    res = _adamw(small_parts, _pack_small(wts), _pack_small(mom_m), _pack_small(mom_v), "adamw_small")
    for dst, packed in zip((grads, deltas, new_m, new_v), res):
        dst.update(_unpack_small(packed))
    loss = res[0][PACK_LOSS_ROW, 0]

    out = [loss, dx[None]]
    for group in (grads, deltas, new_m, new_v):
        out += [group[k] for k in WEIGHTS]
    return tuple(out)
```

```python
import functools

import jax
import jax.numpy as jnp
from jax import lax
from jax.experimental import pallas as pl
from jax.experimental.pallas import tpu as pltpu

F32 = jnp.float32
BF16 = jnp.bfloat16
MESH = pl.DeviceIdType.MESH

N_CHIPS = 4
N_DEV = 8
D_MODEL = 1024
D_CONV = 512
D_POOL = 512
CONV_WIDTH = 31
POOL_WINDOWS = (2, 4, 8, 16)
POOL_GROUP = 128
D_IN = 2 * D_CONV + D_POOL
HALO = 32
RMS_EPS = 1e-6
LN_EPS = 1e-5
FFN_RES_WEIGHT = 0.5
ADAM_LR = 0.001
ADAM_B1 = 0.9
ADAM_B2 = 0.999
ADAM_EPS = 1e-08
ADAM_WD = 0.01
ADAM_STEP = 10
VMEM_LIMIT_BYTES = 52 * 1024 * 1024
TM_FFN = 512
TM_MIX = 256
TT_WGRAD = 2048
STRIP = 16
SLOTS_PER_STEP = 2
SUBLANES = 8
RELAY_AT_EIGHTHS = 7

HBM = pl.BlockSpec(memory_space=pl.ANY)


def _dot(a, b):
    return jnp.dot(a, b, preferred_element_type=F32)


def _dot_nt(a, b):
    return lax.dot_general(a, b, (((1,), (1,)), ((), ())), preferred_element_type=F32)


def _dot_tn(a, b):
    return lax.dot_general(a, b, (((0,), (0,)), ((), ())), preferred_element_type=F32)


def _sds(shape, dtype):
    return jax.ShapeDtypeStruct(shape, dtype)


def _rms_stats(xv):
    r = lax.rsqrt(jnp.mean(xv * xv, axis=-1, keepdims=True) + RMS_EPS)
    return r, xv * r


def _swiglu_saved(gate, up):
    sig = jax.nn.sigmoid(gate)
    silu = gate * sig
    return silu, up * (sig * (1.0 + gate * (1.0 - sig))), silu * up


def _rms_bwd(dh, n, r, gain):
    dn = dh * gain
    return r * (dn - n * jnp.mean(dn * n, axis=-1, keepdims=True))


def _place():
    x, y, c = lax.axis_index("x"), lax.axis_index("y"), lax.axis_index("c")
    return x, y, c, [(1 - x, y), (x, 1 - y), (1 - x, 1 - y)]


class Cargo:
    def __init__(self, kind, arrays):
        self.kind, self.arrays = kind, list(arrays)
        n = len(self.arrays)
        self.two_level = kind in ("gather_slots", "gather_chips")
        if self.two_level:
            self.out_shape = [_sds((N_CHIPS,) + a.shape, a.dtype) for a in self.arrays]
        elif kind == "gather_devices":
            self.out_shape = [_sds((N_DEV,) + a.shape, a.dtype) for a in self.arrays]
        else:
            self.out_shape = [_sds(a.shape, a.dtype) for a in self.arrays]
        n_remote = n * {"swap": 1, "gather_devices": N_DEV - 1}.get(kind, N_CHIPS - 1)
        n_own = 0 if kind == "swap" else n
        n_relay = n_remote if self.two_level else 0
        dma = pltpu.SemaphoreType.DMA
        self.scratch = [dma((n_remote,)), dma((n_remote,)), dma((max(n_own, 1),)),
                        dma((max(n_relay, 1),)), dma((max(n_relay, 1),))]

    def _plan(self, ins, outs):
        x, y, c, chips = _place()
        q = 2 * x + y
        sibling = (x, y, 1 - c)
        own, remote, relays = [], [], []
        for a, o in zip(ins, outs):
            if self.two_level:
                half = a.shape[0] // 2
                mine = pl.ds(pl.multiple_of(c * half, SUBLANES), half)
                theirs = pl.ds(pl.multiple_of((1 - c) * half, SUBLANES), half)
                own.append((a, o.at[0 if self.kind == "gather_slots" else q]))
                for j, (px, py) in enumerate(chips):
                    there, here = (j + 1, j + 1) if self.kind == "gather_slots" else (q, 2 * px + py)
                    remote.append((a.at[mine], o.at[there, mine], o.at[here, mine], (px, py, c)))
                    relays.append((o.at[here, mine], o.at[here, mine], o.at[here, theirs], sibling))
            elif self.kind == "scatter_chips":
                own.append((a.at[q], o.at[q]))
                remote += [(a.at[2 * px + py], o.at[q], o.at[2 * px + py], (px, py, c)) for px, py in chips]
            elif self.kind == "swap":
                remote.append((a, o, o, sibling))
            else:
                own.append((a, o.at[4 * x + 2 * y + c]))
                for k in range(1, N_DEV):
                    px, py, pc = x ^ (k >> 2 & 1), y ^ (k >> 1 & 1), c ^ (k & 1)
                    remote.append((a, o.at[4 * x + 2 * y + c], o.at[4 * px + 2 * py + pc], (px, py, pc)))
        return own, remote, relays

    @staticmethod
    def _copies(entries, send_sems, recv_sems):
        out = []
        for k, (src, dst, landed, peer) in enumerate(entries):
            def make(dst_ref, k=k, src=src, peer=peer):
                return pltpu.make_async_remote_copy(src_ref=src, dst_ref=dst_ref, send_sem=send_sems.at[k],
                                                    recv_sem=recv_sems.at[k], device_id=peer, device_id_type=MESH)
            out.append((make(dst), make(landed)))
        return out

    def start(self, ins, outs, sems):
        own, remote, _ = self._plan(ins, outs)
        for k, (src, dst) in enumerate(own):
            pltpu.make_async_copy(src, dst, sems[2].at[k]).start()
        for mine, _ in self._copies(remote, sems[0], sems[1]):
            mine.start()

    def relay(self, ins, outs, sems):
        _, remote, relays = self._plan(ins, outs)
        passed = self._copies(relays, sems[3], sems[4])
        for (_, arriving), (mine, _) in zip(self._copies(remote, sems[0], sems[1]), passed):
            arriving.wait_recv()
            mine.start()

    def wait(self, ins, outs, sems):
        own, remote, relays = self._plan(ins, outs)
        for mine, arriving in self._copies(remote, sems[0], sems[1]):
            mine.wait_send()
            if not self.two_level:
                arriving.wait_recv()
        for mine, arriving in self._copies(relays, sems[3], sems[4]):
            mine.wait_send()
            arriving.wait_recv()
        for k, (src, dst) in enumerate(own):
            pltpu.make_async_copy(src, dst, sems[2].at[k]).wait()


N_CARGO_SEMS = 5


def _call(body, *, name, grid, in_specs, out_specs, out_shape, args, scratch_shapes=(), cargos=()):
    n_in, n_out, n_scr = len(in_specs), len(out_specs), len(scratch_shapes)
    c_in = [len(cg.arrays) for cg in cargos]
    n_cin = sum(c_in)

    def wrapped(*refs):
        ins = refs[:n_in]
        cins = refs[n_in:n_in + n_cin]
        outs = refs[n_in + n_cin:n_in + n_cin + n_out]
        couts = refs[n_in + n_cin + n_out:n_in + 2 * n_cin + n_out]
        scr = refs[n_in + 2 * n_cin + n_out:n_in + 2 * n_cin + n_out + n_scr]
        sems = refs[n_in + 2 * n_cin + n_out + n_scr:]
        step, n_steps = 0, 1
        for ax, size in enumerate(grid):
            step = step * size + pl.program_id(ax)
            n_steps *= size

        def each(method, only_two_level=False):
            at = 0
            for k, cg in enumerate(cargos):
                if cg.two_level or not only_two_level:
                    getattr(cg, method)(cins[at:at + c_in[k]], couts[at:at + c_in[k]],
                                        sems[N_CARGO_SEMS * k:N_CARGO_SEMS * (k + 1)])
                at += c_in[k]

        body(*ins, *outs, *scr)
        if cargos:
            pl.when(step == 0)(lambda: each("start"))
        if any(cg.two_level for cg in cargos):
            pl.when(step == (RELAY_AT_EIGHTHS * n_steps) // 8)(lambda: each("relay", only_two_level=True))
        if cargos:
            pl.when(step == n_steps - 1)(lambda: each("wait"))

    res = pl.pallas_call(
        wrapped, name=name, grid=grid,
        in_specs=list(in_specs) + [HBM] * n_cin,
        out_specs=list(out_specs) + [HBM] * n_cin,
        out_shape=list(out_shape) + [s for cg in cargos for s in cg.out_shape],
        scratch_shapes=list(scratch_shapes) + [s for cg in cargos for s in cg.scratch],
        compiler_params=pltpu.CompilerParams(dimension_semantics=("arbitrary",) * len(grid),
                                             vmem_limit_bytes=VMEM_LIMIT_BYTES),
    )(*args, *[a for cg in cargos for a in cg.arrays])
    outs, rest = list(res[:n_out]), list(res[n_out:])
    cargo_outs = []
    for k in c_in:
        cargo_outs.append(rest[:k])
        rest = rest[k:]
    return outs, cargo_outs


def _exchange(cargo, name):
    _, (outs,) = _call(lambda: None, name=name, grid=(1,), in_specs=[], out_specs=[], out_shape=[], args=[],
                       cargos=[cargo])
    return outs


def _ffn_up_gather(x, gain, wg_t, wu_t, name, cargos=()):
    t_len, d = x.shape
    fq = wg_t.shape[0]
    tm = min(TM_FFN, t_len)
    n_tiles = t_len // tm
    relay_tile = n_tiles // 2
    fetch_tile = min(relay_tile + 1, n_tiles - 1)

    def body(x_ref, g_ref, wg_in, wu_in, h_ref, s_ref, p_ref, a_ref, wg_all, wu_all,
             wg_v, wu_v, h_all, send_sems, recv_sems, pass_send_sems, pass_recv_sems, own_sems, load_sems):
        s = pl.program_id(0)
        i = pl.program_id(1)
        x_, y_, c_, chips = _place()
        shards = ((wg_in, wg_all, wg_v), (wu_in, wu_all, wu_v))
        mine = pl.ds(pl.multiple_of(c_ * (fq // 2), SUBLANES), fq // 2)
        theirs = pl.ds(pl.multiple_of((1 - c_) * (fq // 2), SUBLANES), fq // 2)

        def to_peer(k, j):
            w_in, w_all, _ = shards[k]
            return pltpu.make_async_remote_copy(
                src_ref=w_in.at[mine], dst_ref=w_all.at[j + 1, mine], send_sem=send_sems.at[3 * k + j],
                recv_sem=recv_sems.at[3 * k + j], device_id=(*chips[j], c_), device_id_type=MESH)

        def to_sibling(k, j, landing=False):
            w_all = shards[k][1]
            return pltpu.make_async_remote_copy(
                src_ref=w_all.at[j + 1, mine], dst_ref=w_all.at[j + 1, theirs if landing else mine],
                send_sem=pass_send_sems.at[3 * k + j], recv_sem=pass_recv_sems.at[3 * k + j],
                device_id=(x_, y_, 1 - c_), device_id_type=MESH)

        def keep(k):
            return pltpu.make_async_copy(shards[k][0], shards[k][1].at[0], own_sems.at[k])

        @pl.when((s == 0) & (i == 0))
        def _():
            for j in range(N_CHIPS - 1):
                for k in range(2):
                    to_peer(k, j).start()
            for k in range(2):
                keep(k).start()

        def load(k, slot):
            src = shards[k][0] if slot == 0 else shards[k][1].at[slot]
            return pltpu.make_async_copy(src, shards[k][2].at[slot % 2], load_sems.at[k])

        @pl.when((s == 0) & (i == 0))
        def _():
            for k in range(2):
                load(k, 0).start()
            for k in range(2):
                load(k, 0).wait()

        def pass_on(slot):
            for k in range(2):
                to_peer(k, slot - 1).wait_recv()
                to_sibling(k, slot - 1).start()

        def fetch(slot):
            for k in range(2):
                to_sibling(k, slot - 1, landing=True).wait_recv()
                load(k, slot).start()

        for slot in range(1, N_CHIPS):
            pl.when((s == slot - 1) & (i == relay_tile))(functools.partial(pass_on, slot))
            pl.when((s == slot - 1) & (i == fetch_tile))(functools.partial(fetch, slot))

            @pl.when((s == slot) & (i == 0))
            def _():
                for k in range(2):
                    load(k, slot).wait()

        @pl.when(s == 0)
        def _():
            _, n = _rms_stats(x_ref[...])
            h_new = (n * g_ref[...]).astype(BF16)
            h_ref[...] = h_new
            h_all[i] = h_new

        h = h_all[i]
        silu, dgate, act = _swiglu_saved(_dot_nt(h, wg_v[s % 2]), _dot_nt(h, wu_v[s % 2]))
        s_ref[...] = silu.astype(BF16)
        p_ref[...] = dgate.astype(BF16)
        a_ref[...] = act.astype(BF16)

        @pl.when((s == N_CHIPS - 1) & (i == n_tiles - 1))
        def _():
            for k in range(2):
                for j in range(N_CHIPS - 1):
                    to_peer(k, j).wait_send()
                    to_sibling(k, j).wait_send()
                keep(k).wait()

    tok = pl.BlockSpec((tm, d), lambda s, i: (jnp.where(s == 0, i, n_tiles - 1), 0))
    hid = pl.BlockSpec((None, tm, fq), lambda s, i: (s, i, 0))
    outs, cargo_outs = _call(
        body, name=name, grid=(N_CHIPS, n_tiles),
        in_specs=[tok, pl.BlockSpec((1, d), lambda s, i: (0, 0)), HBM, HBM],
        out_specs=[tok, hid, hid, hid, HBM, HBM],
        out_shape=[_sds((t_len, d), BF16)] + [_sds((N_CHIPS, t_len, fq), BF16)] * 3
        + [_sds((N_CHIPS, fq, d), BF16)] * 2,
        scratch_shapes=[pltpu.VMEM((2, fq, d), BF16), pltpu.VMEM((2, fq, d), BF16),
                        pltpu.VMEM((n_tiles, tm, d), BF16)]
        + [pltpu.SemaphoreType.DMA((6,))] * 4 + [pltpu.SemaphoreType.DMA((2,))] * 2,
        args=[x, gain, wg_t, wu_t], cargos=cargos)
    return outs, cargo_outs


def _load_once(hbm_refs, vmem_refs, sems, first):
    @pl.when(first)
    def _():
        copies = [pltpu.make_async_copy(src, dst, sems.at[k]) for k, (src, dst) in enumerate(zip(hbm_refs, vmem_refs))]
        for cp in copies:
            cp.start()
        for cp in copies:
            cp.wait()


def _ffn_down(x, act, wd, name, cargos=()):
    t_len, d = x.shape
    nq, fq, _ = wd.shape
    tm = min(TM_FFN, t_len)

    def body(x_ref, a_ref, wd_ref, xo_ref):
        y = _dot(a_ref[0], wd_ref[0])
        for j in range(1, nq):
            y = y + _dot(a_ref[j], wd_ref[j])
        xo_ref[...] = x_ref[...] + FFN_RES_WEIGHT * y

    tok = pl.BlockSpec((tm, d), lambda i: (i, 0))
    (xo,), cargo_outs = _call(
        body, name=name, grid=(t_len // tm,),
        in_specs=[tok, pl.BlockSpec((nq, tm, fq), lambda i: (0, i, 0)), pl.BlockSpec((nq, fq, d), lambda i: (0, 0, 0))],
        out_specs=[tok], out_shape=[_sds((t_len, d), F32)], args=[x, act, wd], cargos=cargos)
    return xo, cargo_outs


def _ffn_fwd(x, gain, wg_t, wu_t, wd, name):
    t_len, d = x.shape
    nq, fq, _ = wd.shape
    tm = min(TM_FFN, t_len)

    def body(x_ref, g_ref, wg_hbm, wu_hbm, wd_hbm, xo_ref, h_ref, s_ref, p_ref, a_ref,
             h_s, acc, wg_v, wu_v, wd_v, load_sems):
        i = pl.program_id(0)
        j = pl.program_id(1)
        _load_once((wg_hbm, wu_hbm, wd_hbm), (wg_v, wu_v, wd_v), load_sems, (i == 0) & (j == 0))

        @pl.when(j == 0)
        def _():
            _, n = _rms_stats(x_ref[...])
            h = (n * g_ref[...]).astype(BF16)
            h_s[...] = h
            h_ref[...] = h
            acc[...] = jnp.zeros_like(acc)

        h = h_s[...]
        y = None
        for jj in range(SLOTS_PER_STEP):
            slot = j * SLOTS_PER_STEP + jj
            silu, dgate, act = _swiglu_saved(_dot_nt(h, wg_v[slot]), _dot_nt(h, wu_v[slot]))
            s_ref[jj] = silu.astype(BF16)
            p_ref[jj] = dgate.astype(BF16)
            a_ref[jj] = act.astype(BF16)
            part = _dot(a_ref[jj], wd_v[slot])
            y = part if y is None else y + part
        acc[...] += y

        @pl.when(j == nq // SLOTS_PER_STEP - 1)
        def _():
            xo_ref[...] = x_ref[...] + FFN_RES_WEIGHT * acc[...]

    tok = pl.BlockSpec((tm, d), lambda i, j: (i, 0))
    hid = pl.BlockSpec((SLOTS_PER_STEP, tm, fq), lambda i, j: (j, i, 0))
    outs, _ = _call(
        body, name=name, grid=(t_len // tm, nq // SLOTS_PER_STEP),
        in_specs=[tok, pl.BlockSpec((1, d), lambda i, j: (0, 0)), HBM, HBM, HBM],
        out_specs=[tok, tok, hid, hid, hid],
        out_shape=[_sds((t_len, d), F32), _sds((t_len, d), BF16)] + [_sds((nq, t_len, fq), BF16)] * 3,
        scratch_shapes=[pltpu.VMEM((tm, d), BF16), pltpu.VMEM((tm, d), F32)]
        + [pltpu.VMEM((nq, fq, d), BF16)] * 3 + [pltpu.SemaphoreType.DMA((3,))],
        args=[x, gain, wg_t, wu_t, wd])
    return outs


def _ffn_bwd(dy, x_in, gain, silu, dgate_du, wg_t, wu_t, wd, name):
    t_len, d = dy.shape
    nq, fq, _ = wd.shape
    tm = min(TM_FFN, t_len)

    def body(dy_ref, x_ref, g_ref, s_ref, p_ref, wg_hbm, wu_hbm, wd_hbm,
             dx_ref, dgain_ref, df_ref, dg_ref, du_ref, df_s, dh_acc, dact_s, wg_v, wu_v, wd_v, load_sems):
        i = pl.program_id(0)
        j = pl.program_id(1)
        _load_once((wg_hbm, wu_hbm, wd_hbm), (wg_v, wu_v, wd_v), load_sems, (i == 0) & (j == 0))

        @pl.when((i == 0) & (j == 0))
        def _():
            dgain_ref[...] = jnp.zeros_like(dgain_ref)

        @pl.when(j == 0)
        def _():
            df = (FFN_RES_WEIGHT * dy_ref[...]).astype(BF16)
            df_s[...] = df
            df_ref[...] = df
            dh_acc[...] = jnp.zeros_like(dh_acc)

        half = tm // 2
        for r0 in (0, half):
            dact_s[r0:r0 + half, :] = _dot_nt(df_s[r0:r0 + half, :], wd_v[j])

        for r0 in range(0, tm, STRIP):
            dact = dact_s[r0:r0 + STRIP, :]
            dg_ref[r0:r0 + STRIP, :] = (dact * p_ref[r0:r0 + STRIP, :].astype(F32)).astype(BF16)
            du_ref[r0:r0 + STRIP, :] = (dact * s_ref[r0:r0 + STRIP, :].astype(F32)).astype(BF16)

        for r0 in (0, half):
            rows = slice(r0, r0 + half)
            dh_acc[rows, :] += _dot(dg_ref[rows, :], wg_v[j]) + _dot(du_ref[rows, :], wu_v[j])

        @pl.when(j == nq - 1)
        def _():
            r, n = _rms_stats(x_ref[...])
            dh = dh_acc[...]
            dgain_ref[...] += jnp.sum(dh * n, axis=0, keepdims=True)
            dx_ref[...] = dy_ref[...] + _rms_bwd(dh, n, r, g_ref[...])

    tok = pl.BlockSpec((tm, d), lambda i, j: (i, 0))
    vec = pl.BlockSpec((1, d), lambda i, j: (0, 0))
    hid = pl.BlockSpec((None, tm, fq), lambda i, j: (j, i, 0))
    outs, _ = _call(
        body, name=name, grid=(t_len // tm, nq),
        in_specs=[tok, tok, vec, hid, hid, HBM, HBM, HBM],
        out_specs=[tok, vec, tok, hid, hid],
        out_shape=[_sds((t_len, d), F32), _sds((1, d), F32), _sds((t_len, d), BF16),
                   _sds((nq, t_len, fq), BF16), _sds((nq, t_len, fq), BF16)],
        scratch_shapes=[pltpu.VMEM((tm, d), BF16), pltpu.VMEM((tm, d), F32), pltpu.VMEM((tm, fq), F32)]
        + [pltpu.VMEM((nq, fq, d), BF16)] * 3 + [pltpu.SemaphoreType.DMA((3,))],
        args=[dy, x_in, gain, silu, dgate_du, wg_t, wu_t, wd])
    return outs


def _wgrad(lhs, rhs, l_spec, r_spec, out_shape, out_spec, acc_shape, grid, name, cargos=()):
    n_t = grid[-1]
    t_axis = len(grid) - 1

    def body(l_ref, r_ref, o_ref, acc):
        t = pl.program_id(t_axis)

        @pl.when(t == 0)
        def _():
            acc[...] = jnp.zeros_like(acc)

        acc[...] += _dot_tn(l_ref[...].astype(BF16), r_ref[...].astype(BF16))

        @pl.when(t == n_t - 1)
        def _():
            o_ref[...] = acc[...].astype(o_ref.dtype)

    (out,), cargo_outs = _call(
        body, name=name, grid=grid, in_specs=[l_spec, r_spec], out_specs=[out_spec], out_shape=[out_shape],
        scratch_shapes=[pltpu.VMEM(acc_shape, F32)], args=[lhs, rhs], cargos=cargos)
    return out, cargo_outs


def _wgrad_hid_tok_scatter(hids, tok, name, cargos=()):
    t_len, d = tok.shape
    n_w = len(hids)
    nq, _, fq = hids[0].shape
    half = fq // 2
    tt = min(TT_WGRAD, t_len)
    n_t = t_len // tt
    per_w = 4
    n_sem = 6

    def body(*refs):
        l_refs, r_ref, parts_refs = refs[:n_w], refs[n_w], refs[n_w + 1:2 * n_w + 1]
        scr = refs[2 * n_w + 1:]
        bufs = [scr[per_w * w:per_w * (w + 1)] for w in range(n_w)]
        zeros = scr[per_w * n_w]
        sems = [scr[per_w * n_w + 1 + n_sem * w:per_w * n_w + 1 + n_sem * (w + 1)] for w in range(n_w)]
        g = pl.program_id(0)
        t = pl.program_id(1)
        x_, y_, c_, chips = _place()
        mine = pl.ds(pl.multiple_of(c_ * half, STRIP), half)
        theirs = pl.ds(pl.multiple_of((1 - c_) * half, STRIP), half)

        def to_sibling(w, slot):
            return pltpu.make_async_remote_copy(
                src_ref=bufs[w][1].at[theirs], dst_ref=bufs[w][2].at[slot], send_sem=sems[w][0].at[slot],
                recv_sem=sems[w][1].at[slot], device_id=(x_, y_, 1 - c_), device_id_type=MESH)

        def to_peer(w, j):
            return pltpu.make_async_remote_copy(
                src_ref=bufs[w][3].at[j + 1], dst_ref=parts_refs[w].at[j + 1, mine], send_sem=sems[w][2].at[j],
                recv_sem=sems[w][3].at[j], device_id=(*chips[j], c_), device_id_type=MESH)

        def keep(w):
            return pltpu.make_async_copy(bufs[w][3].at[0], parts_refs[w].at[0, mine], sems[w][4])

        def blank(w, slot):
            return pltpu.make_async_copy(zeros, parts_refs[w].at[slot, theirs], sems[w][5].at[slot])

        @pl.when((g == 0) & (t == 0))
        def _():
            zeros[...] = jnp.zeros_like(zeros)
            for w in range(n_w):
                for slot in range(nq):
                    blank(w, slot).start()

        @pl.when(t == 0)
        def _():
            for w in range(n_w):
                bufs[w][0][...] = jnp.zeros_like(bufs[w][0])

        rhs = r_ref[...]
        for w in range(n_w):
            bufs[w][0][...] += _dot_tn(l_refs[w][...], rhs)

        for step in range(nq):
            slot = (step + 1) % nq

            @pl.when((g == step) & (t == n_t - 1))
            def _():
                for w in range(n_w):
                    acc, stage, _, _ = bufs[w]
                    if step > 0:
                        to_sibling(w, step).wait_send()
                    stage[...] = acc[...].astype(BF16)
                    to_sibling(w, slot).start()
                for w in range(n_w):
                    _, stage, pair, summed = bufs[w]
                    to_sibling(w, slot).wait_recv()
                    summed[slot] = (stage[mine, :].astype(F32) + pair[slot].astype(F32)).astype(BF16)
                    if slot > 0:
                        to_peer(w, slot - 1).start()
                    else:
                        keep(w).start()

        @pl.when((g == nq - 1) & (t == n_t - 1))
        def _():
            for w in range(n_w):
                for j in range(N_CHIPS - 1):
                    to_peer(w, j).wait()
                keep(w).wait()
                to_sibling(w, 0).wait_send()
                for slot in range(nq):
                    blank(w, slot).wait()

    dma = pltpu.SemaphoreType.DMA
    scratch = []
    for _ in range(n_w):
        scratch += [pltpu.VMEM((fq, d), F32), pltpu.VMEM((fq, d), BF16), pltpu.VMEM((nq, half, d), BF16),
                    pltpu.VMEM((nq, half, d), BF16)]
    scratch.append(pltpu.VMEM((half, d), BF16))
    for _ in range(n_w):
        scratch += [dma((nq,)), dma((nq,)), dma((N_CHIPS - 1,)), dma((N_CHIPS - 1,)), dma(()), dma((nq,))]
    parts, cargo_outs = _call(
        body, name=name, grid=(nq, n_t),
        in_specs=[pl.BlockSpec((None, tt, fq), lambda g, t: ((g + 1) % nq, t, 0))] * n_w
        + [pl.BlockSpec((tt, d), lambda g, t: (t, 0))],
        out_specs=[HBM] * n_w, out_shape=[_sds((nq, fq, d), BF16)] * n_w,
        scratch_shapes=scratch, args=[*hids, tok], cargos=cargos)
    return parts, cargo_outs


def _wgrad_2d(lhs, rhs, n_col_blocks, out_dtype, name, group_diag=False, cargos=()):
    t_len, k = lhs.shape
    n = rhs.shape[1]
    nb = n // n_col_blocks
    kb = k // n_col_blocks if group_diag else k
    tt = min(TT_WGRAD, t_len)
    l_map = (lambda q, t: (t, q)) if group_diag else (lambda q, t: (t, 0))
    return _wgrad(lhs, rhs,
                  pl.BlockSpec((tt, kb), l_map),
                  pl.BlockSpec((tt, nb), lambda q, t: (t, q)),
                  _sds((n_col_blocks, kb, nb), out_dtype),
                  pl.BlockSpec((None, kb, nb), lambda q, t: (q, 0, 0)),
                  (kb, nb), (n_col_blocks, t_len // tt), name, cargos)


def _layernorm_stats(u1):
    mu = jnp.mean(u1, axis=-1, keepdims=True)
    xc = u1 - mu
    rstd = lax.rsqrt(jnp.mean(xc * xc, axis=-1, keepdims=True) + LN_EPS)
    return rstd, xc * rstd


def _positions(i, tm, rows, offset=0):
    return (lax.broadcasted_iota(jnp.int32, (rows, 1), 0) + (i * tm + offset)).astype(F32)


SHIFT_ROWS = HALO - SUBLANES


def _fill_shifted(ext_s, sh_s, tm):
    for b in range(1, SUBLANES):
        sh_s[b - 1] = ext_s[pl.ds(b, tm + SHIFT_ROWS), :]


def _window(ext_s, sh_s, shift, tm):
    a, b = divmod(shift, SUBLANES)
    if b == 0:
        return ext_s[pl.ds(shift, tm), :]
    return sh_s[b - 1, pl.ds(a * SUBLANES, tm), :]


def _tile(tm, cols):
    return pl.BlockSpec((tm, cols), lambda i: (i, 0))


def _whole(shape):
    return pl.BlockSpec(shape, lambda i: (0,) * len(shape))


def _mix_fwd(x1, gain, w_in, conv_dw, conv_b, ln_g, ln_b, conv_pw, pool_w, pool_scale, w_out, name, cargos=()):
    t_len, d = x1.shape
    nq, _, nb = w_in.shape
    tm = min(TM_MIX, t_len)

    def body(x_ref, g_ref, wi_ref, dw_ref, cb_ref, lg_ref, lb_ref, pw_ref, plw_ref, ps_ref, wo_ref,
             x2_ref, h_ref, p_ref, u1_ref, u3_ref, mx_ref, cat_ref, ext_s, pext_s, sh_s, tail_s):
        i = pl.program_id(0)

        @pl.when(i == 0)
        def _():
            tail_s[...] = jnp.zeros_like(tail_s)

        _, n = _rms_stats(x_ref[...])
        h = (n * g_ref[...]).astype(BF16)
        h_ref[...] = h
        for q in range(nq):
            p_ref[:, q * nb:(q + 1) * nb] = _dot(h, wi_ref[q])

        a = p_ref[:, 0:D_CONV]
        g = p_ref[:, D_CONV:2 * D_CONV]
        p = p_ref[:, 2 * D_CONV:]
        ext_s[0:HALO, :] = tail_s[:, 0:D_CONV] * jax.nn.sigmoid(tail_s[:, D_CONV:2 * D_CONV])
        ext_s[HALO:, :] = a * jax.nn.sigmoid(g)
        pext_s[0:HALO, :] = tail_s[:, 2 * D_CONV:]
        pext_s[HALO:, :] = p
        tail_s[...] = p_ref[tm - HALO:tm, :]

        _fill_shifted(ext_s, sh_s, tm)
        u1 = jnp.broadcast_to(cb_ref[...], (tm, D_CONV))
        for k in range(CONV_WIDTH):
            u1 = u1 + dw_ref[k:k + 1, :] * _window(ext_s, sh_s, HALO - (CONV_WIDTH - 1) + k, tm)
        u1_ref[...] = u1
        _, nhat = _layernorm_stats(u1)
        u2 = nhat * lg_ref[...] + lb_ref[...]
        u3 = (u2 * jax.nn.sigmoid(u2)).astype(BF16)
        u3_ref[...] = u3
        cat_ref[:, 0:D_CONV] = _dot(u3, pw_ref[...]).astype(BF16)

        pos1 = _positions(i, tm, tm) + 1.0
        for gi, w in enumerate(POOL_WINDOWS):
            cols = slice(gi * POOL_GROUP, (gi + 1) * POOL_GROUP)
            s = pext_s[pl.ds(HALO, tm), cols]
            for j in range(1, w):
                s = s + pext_s[pl.ds(HALO - j, tm), cols]
            mixed = (s / jnp.minimum(pos1, float(w)) - p[:, cols]).astype(BF16)
            mx_ref[:, cols] = mixed
            out = _dot(mixed, plw_ref[gi]) * ps_ref[:, cols]
            cat_ref[:, D_CONV + gi * POOL_GROUP:D_CONV + (gi + 1) * POOL_GROUP] = out.astype(BF16)

        x2_ref[...] = x_ref[...] + _dot(cat_ref[...], wo_ref[...])

    return _call(
        body, name=name, grid=(t_len // tm,),
        in_specs=[_tile(tm, d), _whole((1, d)), _whole((nq, d, nb)), _whole((CONV_WIDTH + 1, D_CONV)),
                  _whole((1, D_CONV)), _whole((1, D_CONV)), _whole((1, D_CONV)), _whole((D_CONV, D_CONV)),
                  _whole((4, POOL_GROUP, POOL_GROUP)), _whole((1, D_POOL)), _whole((D_CONV + D_POOL, d))],
        out_specs=[_tile(tm, d), _tile(tm, d), _tile(tm, D_IN), _tile(tm, D_CONV), _tile(tm, D_CONV),
                   _tile(tm, D_POOL), _tile(tm, D_CONV + D_POOL)],
        out_shape=[_sds((t_len, d), F32), _sds((t_len, d), BF16), _sds((t_len, D_IN), F32),
                   _sds((t_len, D_CONV), F32), _sds((t_len, D_CONV), BF16), _sds((t_len, D_POOL), BF16),
                   _sds((t_len, D_CONV + D_POOL), BF16)],
        scratch_shapes=[pltpu.VMEM((tm + HALO, D_CONV), F32), pltpu.VMEM((tm + HALO, D_POOL), F32),
                        pltpu.VMEM((SUBLANES - 1, tm + SHIFT_ROWS, D_CONV), F32), pltpu.VMEM((HALO, D_IN), F32)],
        args=[x1, gain, w_in, conv_dw, conv_b, ln_g, ln_b, conv_pw, pool_w, pool_scale, w_out], cargos=cargos)


def _mix_bwd(dx2, u1, mixed, proj, x1, gain, conv_dw, ln_g, ln_b, conv_pw, pool_w, pool_scale, w_out, w_in,
             name, cargos=()):
    t_len, d = x1.shape
    nq, _, nb = w_in.shape
    tm = min(TM_MIX, t_len)
    hb = tm // HALO
    n_tiles = t_len // tm

    def body(dxn_ref, u1_ref, mx_ref, p_ref, tail_ref, x_ref, dx2_ref, g_ref, dw_ref, lg_ref, lb_ref, pw_ref,
             plw_ref, ps_ref, wo_ref, wi_ref,
             dx1_ref, dp_ref, dco_ref, dpo_ref, ddw_ref, dcb_ref, dlg_ref, dlb_ref, dps_ref, dgain_ref,
             du_s, dm_s, uext_s, dext_s, mext_s, ush_s, dsh_s):
        k = pl.program_id(0)

        @pl.when(k == 0)
        def _():
            for ref in (ddw_ref, dcb_ref, dlg_ref, dlb_ref, dps_ref, dgain_ref, du_s, dm_s):
                ref[...] = jnp.zeros_like(ref)

        counts = jnp.where(k < n_tiles, 1.0, 0.0)
        dcat = _dot_nt(dxn_ref[...].astype(BF16), wo_ref[...])
        dco = dcat[:, 0:D_CONV].astype(BF16)
        dco_ref[...] = dco
        du3 = _dot_nt(dco, pw_ref[...])
        rstd, nhat = _layernorm_stats(u1_ref[...])
        u2 = nhat * lg_ref[...] + lb_ref[...]
        sig = jax.nn.sigmoid(u2)
        du2 = du3 * (sig * (1.0 + u2 * (1.0 - sig)))
        dlg_ref[...] += counts * jnp.sum(du2 * nhat, axis=0, keepdims=True)
        dlb_ref[...] += counts * jnp.sum(du2, axis=0, keepdims=True)
        dnhat = du2 * lg_ref[...]
        du_s[k % 2] = rstd * (dnhat - jnp.mean(dnhat, axis=-1, keepdims=True)
                              - nhat * jnp.mean(dnhat * nhat, axis=-1, keepdims=True))
        for gi in range(len(POOL_WINDOWS)):
            cols = slice(gi * POOL_GROUP, (gi + 1) * POOL_GROUP)
            dpo = dcat[:, D_CONV + gi * POOL_GROUP:D_CONV + (gi + 1) * POOL_GROUP]
            pre = _dot(mx_ref[:, cols], plw_ref[gi])
            dps_ref[:, cols] += counts * jnp.sum(dpo * pre, axis=0, keepdims=True)
            dout = (dpo * ps_ref[:, cols]).astype(BF16)
            dpo_ref[:, cols] = dout
            dm_s[k % 2, :, cols] = _dot_nt(dout, plw_ref[gi])

        i = jnp.maximum(k - 1, 0)
        cur, nxt = (k + 1) % 2, k % 2
        first = k <= 1
        last = (k == n_tiles) | (k == 0)
        a = p_ref[:, 0:D_CONV]
        g = p_ref[:, D_CONV:2 * D_CONV]
        sg = jax.nn.sigmoid(g)
        ta = tail_ref[:, 0:D_CONV]
        tg = tail_ref[:, D_CONV:2 * D_CONV]
        uext_s[0:HALO, :] = jnp.where(first, 0.0, ta * jax.nn.sigmoid(tg))
        uext_s[HALO:, :] = a * sg
        du1 = du_s[cur]
        dext_s[0:tm, :] = du1
        dext_s[tm:, :] = jnp.where(last, 0.0, du_s[nxt, 0:HALO, :])

        _fill_shifted(uext_s, ush_s, tm)
        _fill_shifted(dext_s, dsh_s, tm)
        du0 = jnp.zeros((tm, D_CONV), F32)
        for tap in range(CONV_WIDTH):
            du0 = du0 + dw_ref[tap:tap + 1, :] * _window(dext_s, dsh_s, CONV_WIDTH - 1 - tap, tm)
            ddw_ref[tap:tap + 1, :] += jnp.sum(
                du1 * _window(uext_s, ush_s, HALO - (CONV_WIDTH - 1) + tap, tm), axis=0, keepdims=True)
        dcb_ref[...] += jnp.sum(du1, axis=0, keepdims=True)
        dp_ref[:, 0:D_CONV] = (du0 * sg).astype(BF16)
        dp_ref[:, D_CONV:2 * D_CONV] = (du0 * a * sg * (1.0 - sg)).astype(BF16)

        pos1 = _positions(i, tm, tm) + 1.0
        pos1_next = _positions(i, tm, HALO, offset=tm) + 1.0
        for gi, w in enumerate(POOL_WINDOWS):
            cols = slice(gi * POOL_GROUP, (gi + 1) * POOL_GROUP)
            dm = dm_s[cur, :, cols]
            mext_s[0:tm, cols] = dm / jnp.minimum(pos1, float(w))
            mext_s[tm:, cols] = jnp.where(last, 0.0, dm_s[nxt, 0:HALO, cols] / jnp.minimum(pos1_next, float(w)))
            s = mext_s[pl.ds(0, tm), cols]
            for j in range(1, w):
                s = s + mext_s[pl.ds(j, tm), cols]
            dp_ref[:, 2 * D_CONV + gi * POOL_GROUP:2 * D_CONV + (gi + 1) * POOL_GROUP] = (s - dm).astype(BF16)

        dh = _dot_nt(dp_ref[:, 0:nb], wi_ref[0])
        for q in range(1, nq):
            dh = dh + _dot_nt(dp_ref[:, q * nb:(q + 1) * nb], wi_ref[q])
        r, n = _rms_stats(x_ref[...])
        dgain_ref[...] += jnp.sum(dh * n, axis=0, keepdims=True)
        dx1_ref[...] = dx2_ref[...] + _rms_bwd(dh, n, r, g_ref[...])

    def ahead(cols):
        return pl.BlockSpec((tm, cols), lambda k: (jnp.minimum(k, n_tiles - 1), 0))

    def behind(cols):
        return pl.BlockSpec((tm, cols), lambda k: (jnp.maximum(k - 1, 0), 0))

    vec = _whole((1, D_CONV))
    return _call(
        body, name=name, grid=(n_tiles + 1,),
        in_specs=[ahead(d), ahead(D_CONV), ahead(D_POOL), behind(D_IN),
                  pl.BlockSpec((HALO, D_IN), lambda k: (jnp.maximum(jnp.maximum(k - 1, 0) * hb - 1, 0), 0)),
                  behind(d), behind(d), _whole((1, d)), _whole((CONV_WIDTH + 1, D_CONV)), vec, vec,
                  _whole((D_CONV, D_CONV)), _whole((4, POOL_GROUP, POOL_GROUP)), vec,
                  _whole((D_CONV + D_POOL, d)), _whole((nq, d, nb))],
        out_specs=[behind(d), behind(D_IN), ahead(D_CONV), ahead(D_POOL), _whole((CONV_WIDTH + 1, D_CONV)), vec,
                   vec, vec, vec, _whole((1, d))],
        out_shape=[_sds((t_len, d), F32), _sds((t_len, D_IN), BF16), _sds((t_len, D_CONV), BF16),
                   _sds((t_len, D_POOL), BF16), _sds((CONV_WIDTH + 1, D_CONV), F32), _sds((1, D_CONV), F32),
                   _sds((1, D_CONV), F32), _sds((1, D_CONV), F32), _sds((1, D_POOL), F32), _sds((1, d), F32)],
        scratch_shapes=[pltpu.VMEM((2, tm, D_CONV), F32), pltpu.VMEM((2, tm, D_POOL), F32),
                        pltpu.VMEM((tm + HALO, D_CONV), F32), pltpu.VMEM((tm + HALO, D_CONV), F32),
                        pltpu.VMEM((tm + HALO, D_POOL), F32),
                        pltpu.VMEM((SUBLANES - 1, tm + SHIFT_ROWS, D_CONV), F32),
                        pltpu.VMEM((SUBLANES - 1, tm + SHIFT_ROWS, D_CONV), F32)],
        args=[dx2, u1, mixed, proj, proj, x1, dx2, gain, conv_dw, ln_g, ln_b, conv_pw, pool_w, pool_scale, w_out,
              w_in], cargos=cargos)


def _final_norm_loss(x3, target, gain, name):
    t_len, d = x3.shape
    tm = min(TM_FFN, t_len)

    def body(x_ref, t_ref, g_ref, dx_ref, loss_ref, dgain_ref):
        @pl.when(pl.program_id(0) == 0)
        def _():
            loss_ref[...] = jnp.zeros_like(loss_ref)
            dgain_ref[...] = jnp.zeros_like(dgain_ref)

        r, n = _rms_stats(x_ref[...])
        err = n * g_ref[...] - t_ref[...]
        per_tok = jnp.sum(err * err, axis=-1, keepdims=True) * (1.0 / d)
        loss_ref[...] += 0.5 * jnp.sum(per_tok, axis=0, keepdims=True)
        dy = err * (1.0 / d)
        dgain_ref[...] += jnp.sum(dy * n, axis=0, keepdims=True)
        dx_ref[...] = _rms_bwd(dy, n, r, g_ref[...])

    tok = pl.BlockSpec((tm, d), lambda i: (i, 0))
    outs, _ = _call(
        body, name=name, grid=(t_len // tm,),
        in_specs=[tok, tok, pl.BlockSpec((1, d), lambda i: (0, 0))],
        out_specs=[tok, pl.BlockSpec((1, 128), lambda i: (0, 0)), pl.BlockSpec((1, d), lambda i: (0, 0))],
        out_shape=[_sds((t_len, d), F32), _sds((1, 128), F32), _sds((1, d), F32)],
        args=[x3, target, gain])
    return outs


def _row_tile(rows):
    return rows // 4 if rows % 64 == 0 else rows


def _adamw_math(w, g, m, v):
    m = ADAM_B1 * m + (1.0 - ADAM_B1) * g
    v = ADAM_B2 * v + (1.0 - ADAM_B2) * (g * g)
    m_hat = m / (1.0 - ADAM_B1 ** ADAM_STEP)
    v_hat = v / (1.0 - ADAM_B2 ** ADAM_STEP)
    delta = -ADAM_LR * (m_hat / (jnp.sqrt(v_hat) + ADAM_EPS) + ADAM_WD * w)
    return delta, m, v


def _adamw(parts, w, m, v, name):
    r, c = w.shape
    n = len(parts)
    tr = _row_tile(r)

    def body(*refs):
        g = None
        for p_ref in refs[:n]:
            s = p_ref[0].astype(F32)
            for k in range(1, p_ref.shape[0]):
                s = s + p_ref[k].astype(F32)
            g = s if g is None else g + s
        w_ref, m_ref, v_ref, g_out, d_out, m_out, v_out = refs[n:]
        delta, nm, nv = _adamw_math(w_ref[...], g, m_ref[...], v_ref[...])
        g_out[...] = g
        d_out[...] = delta
        m_out[...] = nm
        v_out[...] = nv

    blk = pl.BlockSpec((tr, c), lambda i: (i, 0))
    p_specs = [pl.BlockSpec((p.shape[0], tr, c), lambda i: (0, i, 0)) for p in parts]
    outs, _ = _call(body, name=name, grid=(r // tr,), in_specs=p_specs + [blk, blk, blk],
                    out_specs=[blk] * 4, out_shape=[_sds((r, c), F32)] * 4, args=[*parts, w, m, v])
    return outs


FFN_W = ("w_gate", "w_up", "w_down")
MID = ("w_in", "conv_dw", "conv_pw", "w_out")
SMALL_1024 = ("ffn1_norm", "mix_norm", "ffn2_norm", "final_norm")
SMALL_512 = ("conv_dw_b", "conv_ln_g", "conv_ln_b", "pool_scale")
WEIGHTS = ("ffn1_norm", "ffn1_w_gate", "ffn1_w_up", "ffn1_w_down", "mix_norm", "w_in", "conv_dw", "conv_dw_b",
           "conv_ln_g", "conv_ln_b", "conv_pw", "pool_w", "pool_scale", "w_out", "ffn2_norm", "ffn2_w_gate",
           "ffn2_w_up", "ffn2_w_down", "final_norm")
PACK_ROWS = 72
PACK_LOSS_ROW = 70


def _pad_rows(a, rows):
    return jnp.pad(a, ((0, rows - a.shape[0]), (0, 0)))


def _pack_small(t, spare=None):
    rows = [t[k].reshape(1, D_MODEL) for k in SMALL_1024]
    rows.append(jnp.concatenate([t["conv_dw_b"].reshape(1, -1), t["conv_ln_g"].reshape(1, -1)], axis=1))
    rows.append(jnp.concatenate([t["conv_ln_b"].reshape(1, -1), t["pool_scale"].reshape(1, -1)], axis=1))
    rows.append(t["pool_w"].reshape(64, D_MODEL))
    if spare is not None:
        rows.append(jnp.pad(spare, ((0, 0), (0, D_MODEL - spare.shape[1]))))
    return _pad_rows(jnp.concatenate(rows, axis=0), PACK_ROWS)


def _unpack_small(p):
    out = {k: p[i] for i, k in enumerate(SMALL_1024)}
    out["conv_dw_b"], out["conv_ln_g"] = p[4, :D_CONV], p[4, D_CONV:]
    out["conv_ln_b"], out["pool_scale"] = p[5, :D_CONV], p[5, D_CONV:]
    out["pool_w"] = p[6:70].reshape(4, POOL_GROUP, POOL_GROUP)
    return out


def _as_stored(name, a):
    if name.endswith(("w_gate", "w_up")):
        return a.T
    if name == "conv_dw":
        return _pad_rows(a, CONV_WIDTH + 1)
    return a


def _as_given(name, a):
    if name.endswith(("w_gate", "w_up")):
        return a.T
    if name == "conv_dw":
        return a[:CONV_WIDTH]
    return a


def kernel(x, ffn1_norm, ffn1_w_gate, ffn1_w_up, ffn1_w_down, mix_norm, w_in, conv_dw, conv_dw_b, conv_ln_g, conv_ln_b, conv_pw, pool_w, pool_scale, w_out, ffn2_norm, ffn2_w_gate, ffn2_w_up, ffn2_w_down, final_norm, loss_target, m_ffn1_norm, m_ffn1_w_gate, m_ffn1_w_up, m_ffn1_w_down, m_mix_norm, m_w_in, m_conv_dw, m_conv_dw_b, m_conv_ln_g, m_conv_ln_b, m_conv_pw, m_pool_w, m_pool_scale, m_w_out, m_ffn2_norm, m_ffn2_w_gate, m_ffn2_w_up, m_ffn2_w_down, m_final_norm, v_ffn1_norm, v_ffn1_w_gate, v_ffn1_w_up, v_ffn1_w_down, v_mix_norm, v_w_in, v_conv_dw, v_conv_dw_b, v_conv_ln_g, v_conv_ln_b, v_conv_pw, v_pool_w, v_pool_scale, v_w_out, v_ffn2_norm, v_ffn2_w_gate, v_ffn2_w_up, v_ffn2_w_down, v_final_norm):
    given = dict(locals())
    wts = {k: given[k] for k in WEIGHTS}
    mom_m = {k: given["m_" + k] for k in WEIGHTS}
    mom_v = {k: given["v_" + k] for k in WEIGHTS}
    xt, target = x[0], loss_target[0]

    shard = {k: _as_stored(k, wts[k]) if k == "conv_dw" else _as_stored(k, wts[k]).astype(BF16)
             for k in WEIGHTS if k.endswith(FFN_W) or k in MID}
    w = {k: wts[k].reshape(1, -1) for k in SMALL_1024 + SMALL_512}
    w["pool_w"] = wts["pool_w"].astype(BF16)

    (h1, s1, p1, a1, w["ffn1_w_gate"], w["ffn1_w_up"]), ((w["ffn1_w_down"],),) = _ffn_up_gather(
        xt, w["ffn1_norm"], shard["ffn1_w_gate"], shard["ffn1_w_up"], "ffn1_up_gather",
        cargos=[Cargo("gather_slots", [shard["ffn1_w_down"]])])
    x1, (mid, (w["ffn2_w_down"],)) = _ffn_down(
        xt, a1, w["ffn1_w_down"], "ffn1_down",
        cargos=[Cargo("gather_chips", [shard[k] for k in MID]), Cargo("gather_slots", [shard["ffn2_w_down"]])])
    w["w_in"] = mid[0]
    w["conv_dw"] = mid[1].transpose(1, 0, 2).reshape(CONV_WIDTH + 1, D_CONV)
    w["conv_pw"] = mid[2].reshape(D_CONV, D_CONV)
    w["w_out"] = mid[3].reshape(D_CONV + D_POOL, D_MODEL)
    (x2, h2, proj, u1, u3, mixed, cat), ((w["ffn2_w_gate"], w["ffn2_w_up"]),) = _mix_fwd(
        x1, w["mix_norm"], w["w_in"], w["conv_dw"], w["conv_dw_b"], w["conv_ln_g"], w["conv_ln_b"], w["conv_pw"],
        w["pool_w"], w["pool_scale"], w["w_out"], "mix_fwd",
        cargos=[Cargo("gather_slots", [shard["ffn2_w_gate"], shard["ffn2_w_up"]])])
    x3, h3, s2, p2, a2 = _ffn_fwd(x2, w["ffn2_norm"], w["ffn2_w_gate"], w["ffn2_w_up"], w["ffn2_w_down"], "ffn2_fwd")
    dx3, loss_share, d_final = _final_norm_loss(x3, target, w["final_norm"], "final_norm_loss")

    g = {"final_norm": d_final}
    sums = {}

    def landed(names, parts):
        sums.update(zip(names, parts))

    dx2, g["ffn2_norm"], df2, dg2, du2 = _ffn_bwd(dx3, x2, w["ffn2_norm"], s2, p2, w["ffn2_w_gate"],
                                                   w["ffn2_w_up"], w["ffn2_w_down"], "ffn2_bwd")
    def ffn_wgrad(names, hids, tok, kernel_name, cargos=()):
        parts, cargo_outs = _wgrad_hid_tok_scatter(hids, tok, kernel_name, cargos=cargos)
        landed(names, parts)
        return cargo_outs

    ffn_wgrad(["ffn2_w_gate", "ffn2_w_up"], [dg2, du2], h3, "ffn2_dw_gate_up")
    ffn_wgrad(["ffn2_w_down"], [a2], df2, "ffn2_dw_down")
    (dx1, dproj, dco, dpo, g_dw, g["conv_dw_b"], g["conv_ln_g"], g["conv_ln_b"], g["pool_scale"],
     g["mix_norm"]), (swapped2,) = _mix_bwd(
        dx2, u1, mixed, proj, x1, w["mix_norm"], w["conv_dw"], w["conv_ln_g"], w["conv_ln_b"], w["conv_pw"],
        w["pool_w"], w["pool_scale"], w["w_out"], w["w_in"], "mix_bwd",
        cargos=[Cargo("swap", [sums["ffn2_" + k] for k in FFN_W])])
    g_out, _ = _wgrad_2d(cat, dx2, 1, BF16, "dw_out")
    g_pw, _ = _wgrad_2d(u3, dco, 1, BF16, "dconv_pw")
    g["pool_w"], _ = _wgrad_2d(mixed, dpo, 4, F32, "dpool_w", group_diag=True)
    slabs = [g_pw.reshape(N_CHIPS, D_CONV // N_CHIPS, D_CONV),
             g_out.reshape(N_CHIPS, (D_CONV + D_POOL) // N_CHIPS, D_MODEL)]
    g_in, (parts,) = _wgrad_2d(h2, dproj, N_CHIPS, BF16, "dw_in", cargos=[Cargo("scatter_chips", slabs)])
    landed(["conv_pw", "w_out"], parts)
    dx, g["ffn1_norm"], df1, dg1, du1_ = _ffn_bwd(dx1, xt, w["ffn1_norm"], s1, p1, w["ffn1_w_gate"],
                                                   w["ffn1_w_up"], w["ffn1_w_down"], "ffn1_bwd")
    slabs = [g_in, g_dw.reshape(CONV_WIDTH + 1, N_CHIPS, D_CONV // N_CHIPS).transpose(1, 0, 2)]
    parts, small_parts = ffn_wgrad(
        ["ffn1_w_gate", "ffn1_w_up"], [dg1, du1_], h1, "ffn1_dw_gate_up",
        cargos=[Cargo("scatter_chips", slabs), Cargo("gather_devices", [_pack_small(g, spare=loss_share)])])
    landed(["w_in", "conv_dw"], parts)
    swapped_mid, swapped_gate_up = ffn_wgrad(
        ["ffn1_w_down"], [a1], df1, "ffn1_dw_down",
        cargos=[Cargo("swap", [sums[k] for k in MID]), Cargo("swap", [sums["ffn1_w_gate"], sums["ffn1_w_up"]])])
    swapped_down = _exchange(Cargo("swap", [sums["ffn1_w_down"]]), "swap_last")

    theirs = dict(zip(["ffn2_" + k for k in FFN_W], swapped2))
    theirs.update(zip(MID, swapped_mid))
    theirs.update(ffn1_w_gate=swapped_gate_up[0], ffn1_w_up=swapped_gate_up[1], ffn1_w_down=swapped_down[0])
    grads, deltas, new_m, new_v = {}, {}, {}, {}
    for k in theirs:
        res = _adamw([sums[k], theirs[k]], _as_stored(k, wts[k]), _as_stored(k, mom_m[k]),
                     _as_stored(k, mom_v[k]), "adamw_" + k)
        grads[k], deltas[k], new_m[k], new_v[k] = [_as_given(k, t) for t in res]
    res = _adamw(small_parts, _pack_small(wts), _pack_small(mom_m), _pack_small(mom_v), "adamw_small")
    for dst, packed in zip((grads, deltas, new_m, new_v), res):
        dst.update(_unpack_small(packed))
    loss = res[0][PACK_LOSS_ROW, 0]

    out = [loss, dx[None]]
    for group in (grads, deltas, new_m, new_v):
        out += [group[k] for k in WEIGHTS]
    return tuple(out)
```

```python
import functools

import jax
import jax.numpy as jnp
from jax import lax
from jax.experimental import pallas as pl
from jax.experimental.pallas import tpu as pltpu

F32 = jnp.float32
BF16 = jnp.bfloat16
MESH = pl.DeviceIdType.MESH

N_CHIPS = 4
N_DEV = 8
D_MODEL = 1024
D_CONV = 512
D_POOL = 512
CONV_WIDTH = 31
POOL_WINDOWS = (2, 4, 8, 16)
POOL_GROUP = 128
D_IN = 2 * D_CONV + D_POOL
HALO = 32
RMS_EPS = 1e-6
LN_EPS = 1e-5
FFN_RES_WEIGHT = 0.5
ADAM_LR = 0.001
ADAM_B1 = 0.9
ADAM_B2 = 0.999
ADAM_EPS = 1e-08
ADAM_WD = 0.01
ADAM_STEP = 10
VMEM_LIMIT_BYTES = 52 * 1024 * 1024
TM_FFN = 512
TM_MIX = 256
TT_WGRAD = 2048
STRIP = 16
SLOTS_PER_STEP = 2
SUBLANES = 8
RELAY_AT_EIGHTHS = 7

HBM = pl.BlockSpec(memory_space=pl.ANY)


def _dot(a, b):
    return jnp.dot(a, b, preferred_element_type=F32)


def _dot_nt(a, b):
    return lax.dot_general(a, b, (((1,), (1,)), ((), ())), preferred_element_type=F32)


def _dot_tn(a, b):
    return lax.dot_general(a, b, (((0,), (0,)), ((), ())), preferred_element_type=F32)


def _sds(shape, dtype):
    return jax.ShapeDtypeStruct(shape, dtype)


def _rms_stats(xv):
    r = lax.rsqrt(jnp.mean(xv * xv, axis=-1, keepdims=True) + RMS_EPS)
    return r, xv * r


def _swiglu_saved(gate, up):
    sig = jax.nn.sigmoid(gate)
    silu = gate * sig
    return silu, up * (sig * (1.0 + gate * (1.0 - sig))), silu * up


def _rms_bwd(dh, n, r, gain):
    dn = dh * gain
    return r * (dn - n * jnp.mean(dn * n, axis=-1, keepdims=True))


def _place():
    x, y, c = lax.axis_index("x"), lax.axis_index("y"), lax.axis_index("c")
    return x, y, c, [(1 - x, y), (x, 1 - y), (1 - x, 1 - y)]


class Cargo:
    def __init__(self, kind, arrays):
        self.kind, self.arrays = kind, list(arrays)
        n = len(self.arrays)
        self.two_level = kind in ("gather_slots", "gather_chips")
        if self.two_level:
            self.out_shape = [_sds((N_CHIPS,) + a.shape, a.dtype) for a in self.arrays]
        elif kind == "gather_devices":
            self.out_shape = [_sds((N_DEV,) + a.shape, a.dtype) for a in self.arrays]
        else:
            self.out_shape = [_sds(a.shape, a.dtype) for a in self.arrays]
        n_remote = n * {"swap": 1, "gather_devices": N_DEV - 1}.get(kind, N_CHIPS - 1)
        n_own = 0 if kind == "swap" else n
        n_relay = n_remote if self.two_level else 0
        dma = pltpu.SemaphoreType.DMA
        self.scratch = [dma((n_remote,)), dma((n_remote,)), dma((max(n_own, 1),)),
                        dma((max(n_relay, 1),)), dma((max(n_relay, 1),))]

    def _plan(self, ins, outs):
        x, y, c, chips = _place()
        q = 2 * x + y
        sibling = (x, y, 1 - c)
        own, remote, relays = [], [], []
        for a, o in zip(ins, outs):
            if self.two_level:
                half = a.shape[0] // 2
                mine = pl.ds(pl.multiple_of(c * half, SUBLANES), half)
                theirs = pl.ds(pl.multiple_of((1 - c) * half, SUBLANES), half)
                own.append((a, o.at[0 if self.kind == "gather_slots" else q]))
                for j, (px, py) in enumerate(chips):
                    there, here = (j + 1, j + 1) if self.kind == "gather_slots" else (q, 2 * px + py)
                    remote.append((a.at[mine], o.at[there, mine], o.at[here, mine], (px, py, c)))
                    relays.append((o.at[here, mine], o.at[here, mine], o.at[here, theirs], sibling))
            elif self.kind == "scatter_chips":
                own.append((a.at[q], o.at[q]))
                remote += [(a.at[2 * px + py], o.at[q], o.at[2 * px + py], (px, py, c)) for px, py in chips]
            elif self.kind == "swap":
                remote.append((a, o, o, sibling))
            else:
                own.append((a, o.at[4 * x + 2 * y + c]))
                for k in range(1, N_DEV):
                    px, py, pc = x ^ (k >> 2 & 1), y ^ (k >> 1 & 1), c ^ (k & 1)
                    remote.append((a, o.at[4 * x + 2 * y + c], o.at[4 * px + 2 * py + pc], (px, py, pc)))
        return own, remote, relays

    @staticmethod
    def _copies(entries, send_sems, recv_sems):
        out = []
        for k, (src, dst, landed, peer) in enumerate(entries):
            def make(dst_ref, k=k, src=src, peer=peer):
                return pltpu.make_async_remote_copy(src_ref=src, dst_ref=dst_ref, send_sem=send_sems.at[k],
                                                    recv_sem=recv_sems.at[k], device_id=peer, device_id_type=MESH)
            out.append((make(dst), make(landed)))
        return out

    def start(self, ins, outs, sems):
        own, remote, _ = self._plan(ins, outs)
        for k, (src, dst) in enumerate(own):
            pltpu.make_async_copy(src, dst, sems[2].at[k]).start()
        for mine, _ in self._copies(remote, sems[0], sems[1]):
            mine.start()

    def relay(self, ins, outs, sems):
        _, remote, relays = self._plan(ins, outs)
        passed = self._copies(relays, sems[3], sems[4])
        for (_, arriving), (mine, _) in zip(self._copies(remote, sems[0], sems[1]), passed):
            arriving.wait_recv()
            mine.start()

    def wait(self, ins, outs, sems):
        own, remote, relays = self._plan(ins, outs)
        for mine, arriving in self._copies(remote, sems[0], sems[1]):
            mine.wait_send()
            if not self.two_level:
                arriving.wait_recv()
        for mine, arriving in self._copies(relays, sems[3], sems[4]):
            mine.wait_send()
            arriving.wait_recv()
        for k, (src, dst) in enumerate(own):
            pltpu.make_async_copy(src, dst, sems[2].at[k]).wait()


N_CARGO_SEMS = 5


def _call(body, *, name, grid, in_specs, out_specs, out_shape, args, scratch_shapes=(), cargos=()):
    n_in, n_out, n_scr = len(in_specs), len(out_specs), len(scratch_shapes)
    c_in = [len(cg.arrays) for cg in cargos]
    n_cin = sum(c_in)

    def wrapped(*refs):
        ins = refs[:n_in]
        cins = refs[n_in:n_in + n_cin]
        outs = refs[n_in + n_cin:n_in + n_cin + n_out]
        couts = refs[n_in + n_cin + n_out:n_in + 2 * n_cin + n_out]
        scr = refs[n_in + 2 * n_cin + n_out:n_in + 2 * n_cin + n_out + n_scr]
        sems = refs[n_in + 2 * n_cin + n_out + n_scr:]
        step, n_steps = 0, 1
        for ax, size in enumerate(grid):
            step = step * size + pl.program_id(ax)
            n_steps *= size

        def each(method, only_two_level=False):
            at = 0
            for k, cg in enumerate(cargos):
                if cg.two_level or not only_two_level:
                    getattr(cg, method)(cins[at:at + c_in[k]], couts[at:at + c_in[k]],
                                        sems[N_CARGO_SEMS * k:N_CARGO_SEMS * (k + 1)])
                at += c_in[k]

        body(*ins, *outs, *scr)
        if cargos:
            pl.when(step == 0)(lambda: each("start"))
        if any(cg.two_level for cg in cargos):
            pl.when(step == (RELAY_AT_EIGHTHS * n_steps) // 8)(lambda: each("relay", only_two_level=True))
        if cargos:
            pl.when(step == n_steps - 1)(lambda: each("wait"))

    res = pl.pallas_call(
        wrapped, name=name, grid=grid,
        in_specs=list(in_specs) + [HBM] * n_cin,
        out_specs=list(out_specs) + [HBM] * n_cin,
        out_shape=list(out_shape) + [s for cg in cargos for s in cg.out_shape],
        scratch_shapes=list(scratch_shapes) + [s for cg in cargos for s in cg.scratch],
        compiler_params=pltpu.CompilerParams(dimension_semantics=("arbitrary",) * len(grid),
                                             vmem_limit_bytes=VMEM_LIMIT_BYTES),
    )(*args, *[a for cg in cargos for a in cg.arrays])
    outs, rest = list(res[:n_out]), list(res[n_out:])
    cargo_outs = []
    for k in c_in:
        cargo_outs.append(rest[:k])
        rest = rest[k:]
    return outs, cargo_outs


def _exchange(cargo, name):
    _, (outs,) = _call(lambda: None, name=name, grid=(1,), in_specs=[], out_specs=[], out_shape=[], args=[],
                       cargos=[cargo])
    return outs


def _ffn_up_gather(x, gain, wg_t, wu_t, name, cargos=()):
    t_len, d = x.shape
    fq = wg_t.shape[0]
    tm = min(TM_FFN, t_len)
    n_tiles = t_len // tm
    relay_tile = n_tiles // 2
    fetch_tile = min(relay_tile + 1, n_tiles - 1)

    def body(x_ref, g_ref, wg_in, wu_in, h_ref, s_ref, p_ref, a_ref, wg_all, wu_all,
             wg_v, wu_v, h_all, send_sems, recv_sems, pass_send_sems, pass_recv_sems, own_sems, load_sems):
        s = pl.program_id(0)
        i = pl.program_id(1)
        x_, y_, c_, chips = _place()
        shards = ((wg_in, wg_all, wg_v), (wu_in, wu_all, wu_v))
        mine = pl.ds(pl.multiple_of(c_ * (fq // 2), SUBLANES), fq // 2)
        theirs = pl.ds(pl.multiple_of((1 - c_) * (fq // 2), SUBLANES), fq // 2)

        def to_peer(k, j):
            w_in, w_all, _ = shards[k]
            return pltpu.make_async_remote_copy(
                src_ref=w_in.at[mine], dst_ref=w_all.at[j + 1, mine], send_sem=send_sems.at[3 * k + j],
                recv_sem=recv_sems.at[3 * k + j], device_id=(*chips[j], c_), device_id_type=MESH)

        def to_sibling(k, j, landing=False):
            w_all = shards[k][1]
            return pltpu.make_async_remote_copy(
                src_ref=w_all.at[j + 1, mine], dst_ref=w_all.at[j + 1, theirs if landing else mine],
                send_sem=pass_send_sems.at[3 * k + j], recv_sem=pass_recv_sems.at[3 * k + j],
                device_id=(x_, y_, 1 - c_), device_id_type=MESH)

        def keep(k):
            return pltpu.make_async_copy(shards[k][0], shards[k][1].at[0], own_sems.at[k])

        @pl.when((s == 0) & (i == 0))
        def _():
            for j in range(N_CHIPS - 1):
                for k in range(2):
                    to_peer(k, j).start()
            for k in range(2):
                keep(k).start()

        def load(k, slot):
            src = shards[k][0] if slot == 0 else shards[k][1].at[slot]
            return pltpu.make_async_copy(src, shards[k][2].at[slot % 2], load_sems.at[k])

        @pl.when((s == 0) & (i == 0))
        def _():
            for k in range(2):
                load(k, 0).start()
            for k in range(2):
                load(k, 0).wait()

        def pass_on(slot):
            for k in range(2):
                to_peer(k, slot - 1).wait_recv()
                to_sibling(k, slot - 1).start()

        def fetch(slot):
            for k in range(2):
                to_sibling(k, slot - 1, landing=True).wait_recv()
                load(k, slot).start()

        for slot in range(1, N_CHIPS):
            pl.when((s == slot - 1) & (i == relay_tile))(functools.partial(pass_on, slot))
            pl.when((s == slot - 1) & (i == fetch_tile))(functools.partial(fetch, slot))

            @pl.when((s == slot) & (i == 0))
            def _():
                for k in range(2):
                    load(k, slot).wait()

        @pl.when(s == 0)
        def _():
            _, n = _rms_stats(x_ref[...])
            h_new = (n * g_ref[...]).astype(BF16)
            h_ref[...] = h_new
            h_all[i] = h_new

        h = h_all[i]
        silu, dgate, act = _swiglu_saved(_dot_nt(h, wg_v[s % 2]), _dot_nt(h, wu_v[s % 2]))
        s_ref[...] = silu.astype(BF16)
        p_ref[...] = dgate.astype(BF16)
        a_ref[...] = act.astype(BF16)

        @pl.when((s == N_CHIPS - 1) & (i == n_tiles - 1))
        def _():
            for k in range(2):
                for j in range(N_CHIPS - 1):
                    to_peer(k, j).wait_send()
                    to_sibling(k, j).wait_send()
                keep(k).wait()

    tok = pl.BlockSpec((tm, d), lambda s, i: (jnp.where(s == 0, i, n_tiles - 1), 0))
    hid = pl.BlockSpec((None, tm, fq), lambda s, i: (s, i, 0))
    outs, cargo_outs = _call(
        body, name=name, grid=(N_CHIPS, n_tiles),
        in_specs=[tok, pl.BlockSpec((1, d), lambda s, i: (0, 0)), HBM, HBM],
        out_specs=[tok, hid, hid, hid, HBM, HBM],
        out_shape=[_sds((t_len, d), BF16)] + [_sds((N_CHIPS, t_len, fq), BF16)] * 3
        + [_sds((N_CHIPS, fq, d), BF16)] * 2,
        scratch_shapes=[pltpu.VMEM((2, fq, d), BF16), pltpu.VMEM((2, fq, d), BF16),
                        pltpu.VMEM((n_tiles, tm, d), BF16)]
        + [pltpu.SemaphoreType.DMA((6,))] * 4 + [pltpu.SemaphoreType.DMA((2,))] * 2,
        args=[x, gain, wg_t, wu_t], cargos=cargos)
    return outs, cargo_outs


def _load_once(hbm_refs, vmem_refs, sems, first):
    @pl.when(first)
    def _():
        copies = [pltpu.make_async_copy(src, dst, sems.at[k]) for k, (src, dst) in enumerate(zip(hbm_refs, vmem_refs))]
        for cp in copies:
            cp.start()
        for cp in copies:
            cp.wait()


def _ffn_down(x, act, wd, name, cargos=()):
    t_len, d = x.shape
    nq, fq, _ = wd.shape
    tm = min(TM_FFN, t_len)

    def body(x_ref, a_ref, wd_ref, xo_ref):
        y = _dot(a_ref[0], wd_ref[0])
        for j in range(1, nq):
            y = y + _dot(a_ref[j], wd_ref[j])
        xo_ref[...] = x_ref[...] + FFN_RES_WEIGHT * y

    tok = pl.BlockSpec((tm, d), lambda i: (i, 0))
    (xo,), cargo_outs = _call(
        body, name=name, grid=(t_len // tm,),
        in_specs=[tok, pl.BlockSpec((nq, tm, fq), lambda i: (0, i, 0)), pl.BlockSpec((nq, fq, d), lambda i: (0, 0, 0))],
        out_specs=[tok], out_shape=[_sds((t_len, d), F32)], args=[x, act, wd], cargos=cargos)
    return xo, cargo_outs


def _ffn_fwd(x, gain, wg_t, wu_t, wd, name):
    t_len, d = x.shape
    nq, fq, _ = wd.shape
    tm = min(TM_FFN, t_len)

    def body(x_ref, g_ref, wg_hbm, wu_hbm, wd_hbm, xo_ref, h_ref, s_ref, p_ref, a_ref,
             h_s, acc, wg_v, wu_v, wd_v, load_sems):
        i = pl.program_id(0)
        j = pl.program_id(1)
        _load_once((wg_hbm, wu_hbm, wd_hbm), (wg_v, wu_v, wd_v), load_sems, (i == 0) & (j == 0))

        @pl.when(j == 0)
        def _():
            _, n = _rms_stats(x_ref[...])
            h = (n * g_ref[...]).astype(BF16)
            h_s[...] = h
            h_ref[...] = h
            acc[...] = jnp.zeros_like(acc)

        h = h_s[...]
        y = None
        for jj in range(SLOTS_PER_STEP):
            slot = j * SLOTS_PER_STEP + jj
            silu, dgate, act = _swiglu_saved(_dot_nt(h, wg_v[slot]), _dot_nt(h, wu_v[slot]))
            s_ref[jj] = silu.astype(BF16)
            p_ref[jj] = dgate.astype(BF16)
            a_ref[jj] = act.astype(BF16)
            part = _dot(a_ref[jj], wd_v[slot])
            y = part if y is None else y + part
        acc[...] += y

        @pl.when(j == nq // SLOTS_PER_STEP - 1)
        def _():
            xo_ref[...] = x_ref[...] + FFN_RES_WEIGHT * acc[...]

    tok = pl.BlockSpec((tm, d), lambda i, j: (i, 0))
    hid = pl.BlockSpec((SLOTS_PER_STEP, tm, fq), lambda i, j: (j, i, 0))
    outs, _ = _call(
        body, name=name, grid=(t_len // tm, nq // SLOTS_PER_STEP),
        in_specs=[tok, pl.BlockSpec((1, d), lambda i, j: (0, 0)), HBM, HBM, HBM],
        out_specs=[tok, tok, hid, hid, hid],
        out_shape=[_sds((t_len, d), F32), _sds((t_len, d), BF16)] + [_sds((nq, t_len, fq), BF16)] * 3,
        scratch_shapes=[pltpu.VMEM((tm, d), BF16), pltpu.VMEM((tm, d), F32)]
        + [pltpu.VMEM((nq, fq, d), BF16)] * 3 + [pltpu.SemaphoreType.DMA((3,))],
        args=[x, gain, wg_t, wu_t, wd])
    return outs


def _ffn_bwd(dy, x_in, gain, silu, dgate_du, wg_t, wu_t, wd, name):
    t_len, d = dy.shape
    nq, fq, _ = wd.shape
    tm = min(TM_FFN, t_len)

    def body(dy_ref, x_ref, g_ref, s_ref, p_ref, wg_hbm, wu_hbm, wd_hbm,
             dx_ref, dgain_ref, df_ref, dg_ref, du_ref, df_s, dh_acc, dact_s, wg_v, wu_v, wd_v, load_sems):
        i = pl.program_id(0)
        j = pl.program_id(1)
        _load_once((wg_hbm, wu_hbm, wd_hbm), (wg_v, wu_v, wd_v), load_sems, (i == 0) & (j == 0))

        @pl.when((i == 0) & (j == 0))
        def _():
            dgain_ref[...] = jnp.zeros_like(dgain_ref)

        @pl.when(j == 0)
        def _():
            df = (FFN_RES_WEIGHT * dy_ref[...]).astype(BF16)
            df_s[...] = df
            df_ref[...] = df
            dh_acc[...] = jnp.zeros_like(dh_acc)

        half = tm // 2
        for r0 in (0, half):
            dact_s[r0:r0 + half, :] = _dot_nt(df_s[r0:r0 + half, :], wd_v[j])

        for r0 in range(0, tm, STRIP):
            dact = dact_s[r0:r0 + STRIP, :]
            dg_ref[r0:r0 + STRIP, :] = (dact * p_ref[r0:r0 + STRIP, :].astype(F32)).astype(BF16)
            du_ref[r0:r0 + STRIP, :] = (dact * s_ref[r0:r0 + STRIP, :].astype(F32)).astype(BF16)

        for r0 in (0, half):
            rows = slice(r0, r0 + half)
            dh_acc[rows, :] += _dot(dg_ref[rows, :], wg_v[j]) + _dot(du_ref[rows, :], wu_v[j])

        @pl.when(j == nq - 1)
        def _():
            r, n = _rms_stats(x_ref[...])
            dh = dh_acc[...]
            dgain_ref[...] += jnp.sum(dh * n, axis=0, keepdims=True)
            dx_ref[...] = dy_ref[...] + _rms_bwd(dh, n, r, g_ref[...])

    tok = pl.BlockSpec((tm, d), lambda i, j: (i, 0))
    vec = pl.BlockSpec((1, d), lambda i, j: (0, 0))
    hid = pl.BlockSpec((None, tm, fq), lambda i, j: (j, i, 0))
    outs, _ = _call(
        body, name=name, grid=(t_len // tm, nq),
        in_specs=[tok, tok, vec, hid, hid, HBM, HBM, HBM],
        out_specs=[tok, vec, tok, hid, hid],
        out_shape=[_sds((t_len, d), F32), _sds((1, d), F32), _sds((t_len, d), BF16),
                   _sds((nq, t_len, fq), BF16), _sds((nq, t_len, fq), BF16)],
        scratch_shapes=[pltpu.VMEM((tm, d), BF16), pltpu.VMEM((tm, d), F32), pltpu.VMEM((tm, fq), F32)]
        + [pltpu.VMEM((nq, fq, d), BF16)] * 3 + [pltpu.SemaphoreType.DMA((3,))],
        args=[dy, x_in, gain, silu, dgate_du, wg_t, wu_t, wd])
    return outs


def _wgrad(lhs, rhs, l_spec, r_spec, out_shape, out_spec, acc_shape, grid, name, cargos=()):
    n_t = grid[-1]
    t_axis = len(grid) - 1

    def body(l_ref, r_ref, o_ref, acc):
        t = pl.program_id(t_axis)

        @pl.when(t == 0)
        def _():
            acc[...] = jnp.zeros_like(acc)

        acc[...] += _dot_tn(l_ref[...].astype(BF16), r_ref[...].astype(BF16))

        @pl.when(t == n_t - 1)
        def _():
            o_ref[...] = acc[...].astype(o_ref.dtype)

    (out,), cargo_outs = _call(
        body, name=name, grid=grid, in_specs=[l_spec, r_spec], out_specs=[out_spec], out_shape=[out_shape],
        scratch_shapes=[pltpu.VMEM(acc_shape, F32)], args=[lhs, rhs], cargos=cargos)
    return out, cargo_outs


def _wgrad_hid_tok_scatter(hids, tok, name, cargos=()):
    t_len, d = tok.shape
    n_w = len(hids)
    nq, _, fq = hids[0].shape
    half = fq // 2
    tt = min(TT_WGRAD, t_len)
    n_t = t_len // tt
    per_w = 4
    n_sem = 6

    def body(*refs):
        l_refs, r_ref, parts_refs = refs[:n_w], refs[n_w], refs[n_w + 1:2 * n_w + 1]
        scr = refs[2 * n_w + 1:]
        bufs = [scr[per_w * w:per_w * (w + 1)] for w in range(n_w)]
        zeros = scr[per_w * n_w]
        sems = [scr[per_w * n_w + 1 + n_sem * w:per_w * n_w + 1 + n_sem * (w + 1)] for w in range(n_w)]
        g = pl.program_id(0)
        t = pl.program_id(1)
        x_, y_, c_, chips = _place()
        mine = pl.ds(pl.multiple_of(c_ * half, STRIP), half)
        theirs = pl.ds(pl.multiple_of((1 - c_) * half, STRIP), half)

        def to_sibling(w, slot):
            return pltpu.make_async_remote_copy(
                src_ref=bufs[w][1].at[theirs], dst_ref=bufs[w][2].at[slot], send_sem=sems[w][0].at[slot],
                recv_sem=sems[w][1].at[slot], device_id=(x_, y_, 1 - c_), device_id_type=MESH)

        def to_peer(w, j):
            return pltpu.make_async_remote_copy(
                src_ref=bufs[w][3].at[j + 1], dst_ref=parts_refs[w].at[j + 1, mine], send_sem=sems[w][2].at[j],
                recv_sem=sems[w][3].at[j], device_id=(*chips[j], c_), device_id_type=MESH)

        def keep(w):
            return pltpu.make_async_copy(bufs[w][3].at[0], parts_refs[w].at[0, mine], sems[w][4])

        def blank(w, slot):
            return pltpu.make_async_copy(zeros, parts_refs[w].at[slot, theirs], sems[w][5].at[slot])

        @pl.when((g == 0) & (t == 0))
        def _():
            zeros[...] = jnp.zeros_like(zeros)
            for w in range(n_w):
                for slot in range(nq):
                    blank(w, slot).start()

        @pl.when(t == 0)
        def _():
            for w in range(n_w):
                bufs[w][0][...] = jnp.zeros_like(bufs[w][0])

        rhs = r_ref[...]
        for w in range(n_w):
            bufs[w][0][...] += _dot_tn(l_refs[w][...], rhs)

        for step in range(nq):
            slot = (step + 1) % nq

            @pl.when((g == step) & (t == n_t - 1))
            def _():
                for w in range(n_w):
                    acc, stage, _, _ = bufs[w]
                    if step > 0:
                        to_sibling(w, step).wait_send()
                    stage[...] = acc[...].astype(BF16)
                    to_sibling(w, slot).start()
                for w in range(n_w):
                    _, stage, pair, summed = bufs[w]
                    to_sibling(w, slot).wait_recv()
                    summed[slot] = (stage[mine, :].astype(F32) + pair[slot].astype(F32)).astype(BF16)
                    if slot > 0:
                        to_peer(w, slot - 1).start()
                    else:
                        keep(w).start()

        @pl.when((g == nq - 1) & (t == n_t - 1))
        def _():
            for w in range(n_w):
                for j in range(N_CHIPS - 1):
                    to_peer(w, j).wait()
                keep(w).wait()
                to_sibling(w, 0).wait_send()
                for slot in range(nq):
                    blank(w, slot).wait()

    dma = pltpu.SemaphoreType.DMA
    scratch = []
    for _ in range(n_w):
        scratch += [pltpu.VMEM((fq, d), F32), pltpu.VMEM((fq, d), BF16), pltpu.VMEM((nq, half, d), BF16),
                    pltpu.VMEM((nq, half, d), BF16)]
    scratch.append(pltpu.VMEM((half, d), BF16))
    for _ in range(n_w):
        scratch += [dma((nq,)), dma((nq,)), dma((N_CHIPS - 1,)), dma((N_CHIPS - 1,)), dma(()), dma((nq,))]
    parts, cargo_outs = _call(
        body, name=name, grid=(nq, n_t),
        in_specs=[pl.BlockSpec((None, tt, fq), lambda g, t: ((g + 1) % nq, t, 0))] * n_w
        + [pl.BlockSpec((tt, d), lambda g, t: (t, 0))],
        out_specs=[HBM] * n_w, out_shape=[_sds((nq, fq, d), BF16)] * n_w,
        scratch_shapes=scratch, args=[*hids, tok], cargos=cargos)
    return parts, cargo_outs


def _wgrad_2d(lhs, rhs, n_col_blocks, out_dtype, name, group_diag=False, cargos=()):
    t_len, k = lhs.shape
    n = rhs.shape[1]
    nb = n // n_col_blocks
    kb = k // n_col_blocks if group_diag else k
    tt = min(TT_WGRAD, t_len)
    l_map = (lambda q, t: (t, q)) if group_diag else (lambda q, t: (t, 0))
    return _wgrad(lhs, rhs,
                  pl.BlockSpec((tt, kb), l_map),
                  pl.BlockSpec((tt, nb), lambda q, t: (t, q)),
                  _sds((n_col_blocks, kb, nb), out_dtype),
                  pl.BlockSpec((None, kb, nb), lambda q, t: (q, 0, 0)),
                  (kb, nb), (n_col_blocks, t_len // tt), name, cargos)


def _layernorm_stats(u1):
    mu = jnp.mean(u1, axis=-1, keepdims=True)
    xc = u1 - mu
    rstd = lax.rsqrt(jnp.mean(xc * xc, axis=-1, keepdims=True) + LN_EPS)
    return rstd, xc * rstd


def _positions(i, tm, rows, offset=0):
    return (lax.broadcasted_iota(jnp.int32, (rows, 1), 0) + (i * tm + offset)).astype(F32)


SHIFT_ROWS = HALO - SUBLANES


def _fill_shifted(ext_s, sh_s, tm):
    for b in range(1, SUBLANES):
        sh_s[b - 1] = ext_s[pl.ds(b, tm + SHIFT_ROWS), :]


def _window(ext_s, sh_s, shift, tm):
    a, b = divmod(shift, SUBLANES)
    if b == 0:
        return ext_s[pl.ds(shift, tm), :]
    return sh_s[b - 1, pl.ds(a * SUBLANES, tm), :]


def _window_sums(ext_s, lv_a, lv_b, tm, ahead):
    g = POOL_GROUP
    sign = 1 if ahead else -1
    for n, (dst, src, c0) in enumerate(((lv_a, ext_s, 0), (lv_b, lv_a, g), (lv_a, lv_b, 2 * g)), start=1):
        lo = 0 if ahead else n * SUBLANES
        rows = tm + HALO - n * SUBLANES
        shift = sign * 2 ** (n - 1)
        dst[pl.ds(lo, rows), c0:] = src[pl.ds(lo, rows), c0:] + src[pl.ds(lo + shift, rows), c0:]
    base = 0 if ahead else HALO
    rows = pl.ds(base, tm)
    far = pl.ds(base + sign * SUBLANES, tm)
    return [lv_a[rows, 0:g], lv_b[rows, g:2 * g], lv_a[rows, 2 * g:3 * g],
            lv_a[rows, 3 * g:] + lv_a[far, 3 * g:]]


def _tile(tm, cols):
    return pl.BlockSpec((tm, cols), lambda i: (i, 0))


def _whole(shape):
    return pl.BlockSpec(shape, lambda i: (0,) * len(shape))


def _mix_fwd(x1, gain, w_in, conv_dw, conv_b, ln_g, ln_b, conv_pw, pool_w, pool_scale, w_out, name, cargos=()):
    t_len, d = x1.shape
    nq, _, nb = w_in.shape
    tm = min(TM_MIX, t_len)

    def body(x_ref, g_ref, wi_ref, dw_ref, cb_ref, lg_ref, lb_ref, pw_ref, plw_ref, ps_ref, wo_ref,
             x2_ref, h_ref, p_ref, u1_ref, u3_ref, mx_ref, cat_ref, ext_s, pext_s, sh_s, tail_s, lva_s, lvb_s):
        i = pl.program_id(0)

        @pl.when(i == 0)
        def _():
            tail_s[...] = jnp.zeros_like(tail_s)

        _, n = _rms_stats(x_ref[...])
        h = (n * g_ref[...]).astype(BF16)
        h_ref[...] = h
        for q in range(nq):
            p_ref[:, q * nb:(q + 1) * nb] = _dot(h, wi_ref[q])

        a = p_ref[:, 0:D_CONV]
        g = p_ref[:, D_CONV:2 * D_CONV]
        p = p_ref[:, 2 * D_CONV:]
        ext_s[0:HALO, :] = tail_s[:, 0:D_CONV] * jax.nn.sigmoid(tail_s[:, D_CONV:2 * D_CONV])
        ext_s[HALO:, :] = a * jax.nn.sigmoid(g)
        pext_s[0:HALO, :] = tail_s[:, 2 * D_CONV:]
        pext_s[HALO:, :] = p
        tail_s[...] = p_ref[tm - HALO:tm, :]

        _fill_shifted(ext_s, sh_s, tm)
        u1 = jnp.broadcast_to(cb_ref[...], (tm, D_CONV))
        for k in range(CONV_WIDTH):
            u1 = u1 + dw_ref[k:k + 1, :] * _window(ext_s, sh_s, HALO - (CONV_WIDTH - 1) + k, tm)
        u1_ref[...] = u1
        _, nhat = _layernorm_stats(u1)
        u2 = nhat * lg_ref[...] + lb_ref[...]
        u3 = (u2 * jax.nn.sigmoid(u2)).astype(BF16)
        u3_ref[...] = u3
        cat_ref[:, 0:D_CONV] = _dot(u3, pw_ref[...]).astype(BF16)

        pos1 = _positions(i, tm, tm) + 1.0
        sums = _window_sums(pext_s, lva_s, lvb_s, tm, ahead=False)
        for gi, w in enumerate(POOL_WINDOWS):
            cols = slice(gi * POOL_GROUP, (gi + 1) * POOL_GROUP)
            mixed = (sums[gi] / jnp.minimum(pos1, float(w)) - p[:, cols]).astype(BF16)
            mx_ref[:, cols] = mixed
            out = _dot(mixed, plw_ref[gi]) * ps_ref[:, cols]
            cat_ref[:, D_CONV + gi * POOL_GROUP:D_CONV + (gi + 1) * POOL_GROUP] = out.astype(BF16)

        x2_ref[...] = x_ref[...] + _dot(cat_ref[...], wo_ref[...])

    return _call(
        body, name=name, grid=(t_len // tm,),
        in_specs=[_tile(tm, d), _whole((1, d)), _whole((nq, d, nb)), _whole((CONV_WIDTH + 1, D_CONV)),
                  _whole((1, D_CONV)), _whole((1, D_CONV)), _whole((1, D_CONV)), _whole((D_CONV, D_CONV)),
                  _whole((4, POOL_GROUP, POOL_GROUP)), _whole((1, D_POOL)), _whole((D_CONV + D_POOL, d))],
        out_specs=[_tile(tm, d), _tile(tm, d), _tile(tm, D_IN), _tile(tm, D_CONV), _tile(tm, D_CONV),
                   _tile(tm, D_POOL), _tile(tm, D_CONV + D_POOL)],
        out_shape=[_sds((t_len, d), F32), _sds((t_len, d), BF16), _sds((t_len, D_IN), F32),
                   _sds((t_len, D_CONV), F32), _sds((t_len, D_CONV), BF16), _sds((t_len, D_POOL), BF16),
                   _sds((t_len, D_CONV + D_POOL), BF16)],
        scratch_shapes=[pltpu.VMEM((tm + HALO, D_CONV), F32), pltpu.VMEM((tm + HALO, D_POOL), F32),
                        pltpu.VMEM((SUBLANES - 1, tm + SHIFT_ROWS, D_CONV), F32), pltpu.VMEM((HALO, D_IN), F32)]
        + [pltpu.VMEM((tm + HALO, D_POOL), F32)] * 2,
        args=[x1, gain, w_in, conv_dw, conv_b, ln_g, ln_b, conv_pw, pool_w, pool_scale, w_out], cargos=cargos)


def _mix_bwd(dx2, u1, mixed, proj, x1, gain, conv_dw, ln_g, ln_b, conv_pw, pool_w, pool_scale, w_out, w_in,
             name, cargos=()):
    t_len, d = x1.shape
    nq, _, nb = w_in.shape
    tm = min(TM_MIX, t_len)
    hb = tm // HALO
    n_tiles = t_len // tm

    def body(dxn_ref, u1_ref, mx_ref, p_ref, tail_ref, x_ref, dx2_ref, g_ref, dw_ref, lg_ref, lb_ref, pw_ref,
             plw_ref, ps_ref, wo_ref, wi_ref,
             dx1_ref, dp_ref, dco_ref, dpo_ref, ddw_ref, dcb_ref, dlg_ref, dlb_ref, dps_ref, dgain_ref,
             du_s, dm_s, uext_s, dext_s, mext_s, ush_s, dsh_s, lva_s, lvb_s):
        k = pl.program_id(0)

        @pl.when(k == 0)
        def _():
            for ref in (ddw_ref, dcb_ref, dlg_ref, dlb_ref, dps_ref, dgain_ref, du_s, dm_s):
                ref[...] = jnp.zeros_like(ref)

        counts = jnp.where(k < n_tiles, 1.0, 0.0)
        dcat = _dot_nt(dxn_ref[...].astype(BF16), wo_ref[...])
        dco = dcat[:, 0:D_CONV].astype(BF16)
        dco_ref[...] = dco
        du3 = _dot_nt(dco, pw_ref[...])
        rstd, nhat = _layernorm_stats(u1_ref[...])
        u2 = nhat * lg_ref[...] + lb_ref[...]
        sig = jax.nn.sigmoid(u2)
        du2 = du3 * (sig * (1.0 + u2 * (1.0 - sig)))
        dlg_ref[...] += counts * jnp.sum(du2 * nhat, axis=0, keepdims=True)
        dlb_ref[...] += counts * jnp.sum(du2, axis=0, keepdims=True)
        dnhat = du2 * lg_ref[...]
        du_s[k % 2] = rstd * (dnhat - jnp.mean(dnhat, axis=-1, keepdims=True)
                              - nhat * jnp.mean(dnhat * nhat, axis=-1, keepdims=True))
        for gi in range(len(POOL_WINDOWS)):
            cols = slice(gi * POOL_GROUP, (gi + 1) * POOL_GROUP)
            dpo = dcat[:, D_CONV + gi * POOL_GROUP:D_CONV + (gi + 1) * POOL_GROUP]
            pre = _dot(mx_ref[:, cols], plw_ref[gi])
            dps_ref[:, cols] += counts * jnp.sum(dpo * pre, axis=0, keepdims=True)
            dout = (dpo * ps_ref[:, cols]).astype(BF16)
            dpo_ref[:, cols] = dout
            dm_s[k % 2, :, cols] = _dot_nt(dout, plw_ref[gi])

        i = jnp.maximum(k - 1, 0)
        cur, nxt = (k + 1) % 2, k % 2
        first = k <= 1
        last = (k == n_tiles) | (k == 0)
        a = p_ref[:, 0:D_CONV]
        g = p_ref[:, D_CONV:2 * D_CONV]
        sg = jax.nn.sigmoid(g)
        ta = tail_ref[:, 0:D_CONV]
        tg = tail_ref[:, D_CONV:2 * D_CONV]
        uext_s[0:HALO, :] = jnp.where(first, 0.0, ta * jax.nn.sigmoid(tg))
        uext_s[HALO:, :] = a * sg
        du1 = du_s[cur]
        dext_s[0:tm, :] = du1
        dext_s[tm:, :] = jnp.where(last, 0.0, du_s[nxt, 0:HALO, :])

        _fill_shifted(uext_s, ush_s, tm)
        _fill_shifted(dext_s, dsh_s, tm)
        du0 = jnp.zeros((tm, D_CONV), F32)
        for tap in range(CONV_WIDTH):
            du0 = du0 + dw_ref[tap:tap + 1, :] * _window(dext_s, dsh_s, CONV_WIDTH - 1 - tap, tm)
            ddw_ref[tap:tap + 1, :] += jnp.sum(
                du1 * _window(uext_s, ush_s, HALO - (CONV_WIDTH - 1) + tap, tm), axis=0, keepdims=True)
        dcb_ref[...] += jnp.sum(du1, axis=0, keepdims=True)
        dp_ref[:, 0:D_CONV] = (du0 * sg).astype(BF16)
        dp_ref[:, D_CONV:2 * D_CONV] = (du0 * a * sg * (1.0 - sg)).astype(BF16)

        pos1 = _positions(i, tm, tm) + 1.0
        pos1_next = _positions(i, tm, HALO, offset=tm) + 1.0
        for gi, w in enumerate(POOL_WINDOWS):
            cols = slice(gi * POOL_GROUP, (gi + 1) * POOL_GROUP)
            dm = dm_s[cur, :, cols]
            mext_s[0:tm, cols] = dm / jnp.minimum(pos1, float(w))
            mext_s[tm:, cols] = jnp.where(last, 0.0, dm_s[nxt, 0:HALO, cols] / jnp.minimum(pos1_next, float(w)))
        sums = _window_sums(mext_s, lva_s, lvb_s, tm, ahead=True)
        for gi in range(len(POOL_WINDOWS)):
            cols = slice(gi * POOL_GROUP, (gi + 1) * POOL_GROUP)
            dp_ref[:, 2 * D_CONV + gi * POOL_GROUP:2 * D_CONV + (gi + 1) * POOL_GROUP] = (
                sums[gi] - dm_s[cur, :, cols]).astype(BF16)

        dh = _dot_nt(dp_ref[:, 0:nb], wi_ref[0])
        for q in range(1, nq):
            dh = dh + _dot_nt(dp_ref[:, q * nb:(q + 1) * nb], wi_ref[q])
        r, n = _rms_stats(x_ref[...])
        dgain_ref[...] += jnp.sum(dh * n, axis=0, keepdims=True)
        dx1_ref[...] = dx2_ref[...] + _rms_bwd(dh, n, r, g_ref[...])

    def ahead(cols):
        return pl.BlockSpec((tm, cols), lambda k: (jnp.minimum(k, n_tiles - 1), 0))

    def behind(cols):
        return pl.BlockSpec((tm, cols), lambda k: (jnp.maximum(k - 1, 0), 0))

    vec = _whole((1, D_CONV))
    return _call(
        body, name=name, grid=(n_tiles + 1,),
        in_specs=[ahead(d), ahead(D_CONV), ahead(D_POOL), behind(D_IN),
                  pl.BlockSpec((HALO, D_IN), lambda k: (jnp.maximum(jnp.maximum(k - 1, 0) * hb - 1, 0), 0)),
                  behind(d), behind(d), _whole((1, d)), _whole((CONV_WIDTH + 1, D_CONV)), vec, vec,
                  _whole((D_CONV, D_CONV)), _whole((4, POOL_GROUP, POOL_GROUP)), vec,
                  _whole((D_CONV + D_POOL, d)), _whole((nq, d, nb))],
        out_specs=[behind(d), behind(D_IN), ahead(D_CONV), ahead(D_POOL), _whole((CONV_WIDTH + 1, D_CONV)), vec,
                   vec, vec, vec, _whole((1, d))],
        out_shape=[_sds((t_len, d), F32), _sds((t_len, D_IN), BF16), _sds((t_len, D_CONV), BF16),
                   _sds((t_len, D_POOL), BF16), _sds((CONV_WIDTH + 1, D_CONV), F32), _sds((1, D_CONV), F32),
                   _sds((1, D_CONV), F32), _sds((1, D_CONV), F32), _sds((1, D_POOL), F32), _sds((1, d), F32)],
        scratch_shapes=[pltpu.VMEM((2, tm, D_CONV), F32), pltpu.VMEM((2, tm, D_POOL), F32),
                        pltpu.VMEM((tm + HALO, D_CONV), F32), pltpu.VMEM((tm + HALO, D_CONV), F32),
                        pltpu.VMEM((tm + HALO, D_POOL), F32),
                        pltpu.VMEM((SUBLANES - 1, tm + SHIFT_ROWS, D_CONV), F32),
                        pltpu.VMEM((SUBLANES - 1, tm + SHIFT_ROWS, D_CONV), F32)]
        + [pltpu.VMEM((tm + HALO, D_POOL), F32)] * 2,
        args=[dx2, u1, mixed, proj, proj, x1, dx2, gain, conv_dw, ln_g, ln_b, conv_pw, pool_w, pool_scale, w_out,
              w_in], cargos=cargos)


def _final_norm_loss(x3, target, gain, name):
    t_len, d = x3.shape
    tm = min(2 * TM_FFN, t_len)

    def body(x_ref, t_ref, g_ref, dx_ref, loss_ref, dgain_ref):
        @pl.when(pl.program_id(0) == 0)
        def _():
            loss_ref[...] = jnp.zeros_like(loss_ref)
            dgain_ref[...] = jnp.zeros_like(dgain_ref)

        r, n = _rms_stats(x_ref[...])
        err = n * g_ref[...] - t_ref[...]
        per_tok = jnp.sum(err * err, axis=-1, keepdims=True) * (1.0 / d)
        loss_ref[...] += 0.5 * jnp.sum(per_tok, axis=0, keepdims=True)
        dy = err * (1.0 / d)
        dgain_ref[...] += jnp.sum(dy * n, axis=0, keepdims=True)
        dx_ref[...] = _rms_bwd(dy, n, r, g_ref[...])

    tok = pl.BlockSpec((tm, d), lambda i: (i, 0))
    outs, _ = _call(
        body, name=name, grid=(t_len // tm,),
        in_specs=[tok, tok, pl.BlockSpec((1, d), lambda i: (0, 0))],
        out_specs=[tok, pl.BlockSpec((1, 128), lambda i: (0, 0)), pl.BlockSpec((1, d), lambda i: (0, 0))],
        out_shape=[_sds((t_len, d), F32), _sds((1, 128), F32), _sds((1, d), F32)],
        args=[x3, target, gain])
    return outs


def _row_tile(rows):
    return rows // 4 if rows % 64 == 0 else rows


def _adamw_math(w, g, m, v):
    m = ADAM_B1 * m + (1.0 - ADAM_B1) * g
    v = ADAM_B2 * v + (1.0 - ADAM_B2) * (g * g)
    m_hat = m / (1.0 - ADAM_B1 ** ADAM_STEP)
    v_hat = v / (1.0 - ADAM_B2 ** ADAM_STEP)
    delta = -ADAM_LR * (m_hat / (jnp.sqrt(v_hat) + ADAM_EPS) + ADAM_WD * w)
    return delta, m, v


def _adamw(parts, w, m, v, name):
    r, c = w.shape
    n = len(parts)
    tr = _row_tile(r)

    def body(*refs):
        g = None
        for p_ref in refs[:n]:
            s = p_ref[0].astype(F32)
            for k in range(1, p_ref.shape[0]):
                s = s + p_ref[k].astype(F32)
            g = s if g is None else g + s
        w_ref, m_ref, v_ref, g_out, d_out, m_out, v_out = refs[n:]
        delta, nm, nv = _adamw_math(w_ref[...], g, m_ref[...], v_ref[...])
        g_out[...] = g
        d_out[...] = delta
        m_out[...] = nm
        v_out[...] = nv

    blk = pl.BlockSpec((tr, c), lambda i: (i, 0))
    p_specs = [pl.BlockSpec((p.shape[0], tr, c), lambda i: (0, i, 0)) for p in parts]
    outs, _ = _call(body, name=name, grid=(r // tr,), in_specs=p_specs + [blk, blk, blk],
                    out_specs=[blk] * 4, out_shape=[_sds((r, c), F32)] * 4, args=[*parts, w, m, v])
    return outs


FFN_W = ("w_gate", "w_up", "w_down")
MID = ("w_in", "conv_dw", "conv_pw", "w_out")
SMALL_1024 = ("ffn1_norm", "mix_norm", "ffn2_norm", "final_norm")
SMALL_512 = ("conv_dw_b", "conv_ln_g", "conv_ln_b", "pool_scale")
WEIGHTS = ("ffn1_norm", "ffn1_w_gate", "ffn1_w_up", "ffn1_w_down", "mix_norm", "w_in", "conv_dw", "conv_dw_b",
           "conv_ln_g", "conv_ln_b", "conv_pw", "pool_w", "pool_scale", "w_out", "ffn2_norm", "ffn2_w_gate",
           "ffn2_w_up", "ffn2_w_down", "final_norm")
PACK_ROWS = 72
PACK_LOSS_ROW = 70


def _pad_rows(a, rows):
    return jnp.pad(a, ((0, rows - a.shape[0]), (0, 0)))


def _pack_small(t, spare=None):
    rows = [t[k].reshape(1, D_MODEL) for k in SMALL_1024]
    rows.append(jnp.concatenate([t["conv_dw_b"].reshape(1, -1), t["conv_ln_g"].reshape(1, -1)], axis=1))
    rows.append(jnp.concatenate([t["conv_ln_b"].reshape(1, -1), t["pool_scale"].reshape(1, -1)], axis=1))
    rows.append(t["pool_w"].reshape(64, D_MODEL))
    if spare is not None:
        rows.append(jnp.pad(spare, ((0, 0), (0, D_MODEL - spare.shape[1]))))
    return _pad_rows(jnp.concatenate(rows, axis=0), PACK_ROWS)


def _unpack_small(p):
    out = {k: p[i] for i, k in enumerate(SMALL_1024)}
    out["conv_dw_b"], out["conv_ln_g"] = p[4, :D_CONV], p[4, D_CONV:]
    out["conv_ln_b"], out["pool_scale"] = p[5, :D_CONV], p[5, D_CONV:]
    out["pool_w"] = p[6:70].reshape(4, POOL_GROUP, POOL_GROUP)
    return out


def _as_stored(name, a):
    if name.endswith(("w_gate", "w_up")):
        return a.T
    if name == "conv_dw":
        return _pad_rows(a, CONV_WIDTH + 1)
    return a


def _as_given(name, a):
    if name.endswith(("w_gate", "w_up")):
        return a.T
    if name == "conv_dw":
        return a[:CONV_WIDTH]
    return a


def kernel(x, ffn1_norm, ffn1_w_gate, ffn1_w_up, ffn1_w_down, mix_norm, w_in, conv_dw, conv_dw_b, conv_ln_g, conv_ln_b, conv_pw, pool_w, pool_scale, w_out, ffn2_norm, ffn2_w_gate, ffn2_w_up, ffn2_w_down, final_norm, loss_target, m_ffn1_norm, m_ffn1_w_gate, m_ffn1_w_up, m_ffn1_w_down, m_mix_norm, m_w_in, m_conv_dw, m_conv_dw_b, m_conv_ln_g, m_conv_ln_b, m_conv_pw, m_pool_w, m_pool_scale, m_w_out, m_ffn2_norm, m_ffn2_w_gate, m_ffn2_w_up, m_ffn2_w_down, m_final_norm, v_ffn1_norm, v_ffn1_w_gate, v_ffn1_w_up, v_ffn1_w_down, v_mix_norm, v_w_in, v_conv_dw, v_conv_dw_b, v_conv_ln_g, v_conv_ln_b, v_conv_pw, v_pool_w, v_pool_scale, v_w_out, v_ffn2_norm, v_ffn2_w_gate, v_ffn2_w_up, v_ffn2_w_down, v_final_norm):
    given = dict(locals())
    wts = {k: given[k] for k in WEIGHTS}
    mom_m = {k: given["m_" + k] for k in WEIGHTS}
    mom_v = {k: given["v_" + k] for k in WEIGHTS}
    xt, target = x[0], loss_target[0]

    shard = {k: _as_stored(k, wts[k]) if k == "conv_dw" else _as_stored(k, wts[k]).astype(BF16)
             for k in WEIGHTS if k.endswith(FFN_W) or k in MID}
    w = {k: wts[k].reshape(1, -1) for k in SMALL_1024 + SMALL_512}
    w["pool_w"] = wts["pool_w"].astype(BF16)

    (h1, s1, p1, a1, w["ffn1_w_gate"], w["ffn1_w_up"]), ((w["ffn1_w_down"],),) = _ffn_up_gather(
        xt, w["ffn1_norm"], shard["ffn1_w_gate"], shard["ffn1_w_up"], "ffn1_up_gather",
        cargos=[Cargo("gather_slots", [shard["ffn1_w_down"]])])
    x1, (mid, (w["ffn2_w_down"],)) = _ffn_down(
        xt, a1, w["ffn1_w_down"], "ffn1_down",
        cargos=[Cargo("gather_chips", [shard[k] for k in MID]), Cargo("gather_slots", [shard["ffn2_w_down"]])])
    w["w_in"] = mid[0]
    w["conv_dw"] = mid[1].transpose(1, 0, 2).reshape(CONV_WIDTH + 1, D_CONV)
    w["conv_pw"] = mid[2].reshape(D_CONV, D_CONV)
    w["w_out"] = mid[3].reshape(D_CONV + D_POOL, D_MODEL)
    (x2, h2, proj, u1, u3, mixed, cat), ((w["ffn2_w_gate"], w["ffn2_w_up"]),) = _mix_fwd(
        x1, w["mix_norm"], w["w_in"], w["conv_dw"], w["conv_dw_b"], w["conv_ln_g"], w["conv_ln_b"], w["conv_pw"],
        w["pool_w"], w["pool_scale"], w["w_out"], "mix_fwd",
        cargos=[Cargo("gather_slots", [shard["ffn2_w_gate"], shard["ffn2_w_up"]])])
    x3, h3, s2, p2, a2 = _ffn_fwd(x2, w["ffn2_norm"], w["ffn2_w_gate"], w["ffn2_w_up"], w["ffn2_w_down"], "ffn2_fwd")
    dx3, loss_share, d_final = _final_norm_loss(x3, target, w["final_norm"], "final_norm_loss")

    g = {"final_norm": d_final}
    sums = {}

    def landed(names, parts):
        sums.update(zip(names, parts))

    dx2, g["ffn2_norm"], df2, dg2, du2 = _ffn_bwd(dx3, x2, w["ffn2_norm"], s2, p2, w["ffn2_w_gate"],
                                                   w["ffn2_w_up"], w["ffn2_w_down"], "ffn2_bwd")
    def ffn_wgrad(names, hids, tok, kernel_name, cargos=()):
        parts, cargo_outs = _wgrad_hid_tok_scatter(hids, tok, kernel_name, cargos=cargos)
        landed(names, parts)
        return cargo_outs

    ffn_wgrad(["ffn2_w_gate", "ffn2_w_up"], [dg2, du2], h3, "ffn2_dw_gate_up")
    ffn_wgrad(["ffn2_w_down"], [a2], df2, "ffn2_dw_down")
    (dx1, dproj, dco, dpo, g_dw, g["conv_dw_b"], g["conv_ln_g"], g["conv_ln_b"], g["pool_scale"],
     g["mix_norm"]), (swapped2,) = _mix_bwd(
        dx2, u1, mixed, proj, x1, w["mix_norm"], w["conv_dw"], w["conv_ln_g"], w["conv_ln_b"], w["conv_pw"],
        w["pool_w"], w["pool_scale"], w["w_out"], w["w_in"], "mix_bwd",
        cargos=[Cargo("swap", [sums["ffn2_" + k] for k in FFN_W])])
    g_out, _ = _wgrad_2d(cat, dx2, 1, BF16, "dw_out")
    g_pw, _ = _wgrad_2d(u3, dco, 1, BF16, "dconv_pw")
    g["pool_w"], _ = _wgrad_2d(mixed, dpo, 4, F32, "dpool_w", group_diag=True)
    slabs = [g_pw.reshape(N_CHIPS, D_CONV // N_CHIPS, D_CONV),
             g_out.reshape(N_CHIPS, (D_CONV + D_POOL) // N_CHIPS, D_MODEL)]
    g_in, (parts,) = _wgrad_2d(h2, dproj, N_CHIPS, BF16, "dw_in", cargos=[Cargo("scatter_chips", slabs)])
    landed(["conv_pw", "w_out"], parts)
    dx, g["ffn1_norm"], df1, dg1, du1_ = _ffn_bwd(dx1, xt, w["ffn1_norm"], s1, p1, w["ffn1_w_gate"],
                                                   w["ffn1_w_up"], w["ffn1_w_down"], "ffn1_bwd")
    slabs = [g_in, g_dw.reshape(CONV_WIDTH + 1, N_CHIPS, D_CONV // N_CHIPS).transpose(1, 0, 2)]
    (parts,) = ffn_wgrad(["ffn1_w_gate", "ffn1_w_up"], [dg1, du1_], h1, "ffn1_dw_gate_up",
                         cargos=[Cargo("scatter_chips", slabs)])
    landed(["w_in", "conv_dw"], parts)
    swapped_mid, swapped_gate_up, small_parts = ffn_wgrad(
        ["ffn1_w_down"], [a1], df1, "ffn1_dw_down",
        cargos=[Cargo("swap", [sums[k] for k in MID]), Cargo("swap", [sums["ffn1_w_gate"], sums["ffn1_w_up"]]),
                Cargo("gather_devices", [_pack_small(g, spare=loss_share)])])
    swapped_down = _exchange(Cargo("swap", [sums["ffn1_w_down"]]), "swap_last")

    theirs = dict(zip(["ffn2_" + k for k in FFN_W], swapped2))
    theirs.update(zip(MID, swapped_mid))
    theirs.update(ffn1_w_gate=swapped_gate_up[0], ffn1_w_up=swapped_gate_up[1], ffn1_w_down=swapped_down[0])
    grads, deltas, new_m, new_v = {}, {}, {}, {}
    for k in theirs:
        res = _adamw([sums[k], theirs[k]], _as_stored(k, wts[k]), _as_stored(k, mom_m[k]),
                     _as_stored(k, mom_v[k]), "adamw_" + k)
        grads[k], deltas[k], new_m[k], new_v[k] = [_as_given(k, t) for t in res]
    res = _adamw(small_parts, _pack_small(wts), _pack_small(mom_m), _pack_small(mom_v), "adamw_small")
    for dst, packed in zip((grads, deltas, new_m, new_v), res):
        dst.update(_unpack_small(packed))
    loss = res[0][PACK_LOSS_ROW, 0]

    out = [loss, dx[None]]
    for group in (grads, deltas, new_m, new_v):
        out += [group[k] for k in WEIGHTS]
    return tuple(out)
```

```python
import functools

import jax
import jax.numpy as jnp
from jax import lax
from jax.experimental import pallas as pl
from jax.experimental.pallas import tpu as pltpu

F32 = jnp.float32
BF16 = jnp.bfloat16
MESH = pl.DeviceIdType.MESH

N_CHIPS = 4
N_DEV = 8
D_MODEL = 1024
D_CONV = 512
D_POOL = 512
CONV_WIDTH = 31
POOL_WINDOWS = (2, 4, 8, 16)
POOL_GROUP = 128
D_IN = 2 * D_CONV + D_POOL
HALO = 32
RMS_EPS = 1e-6
LN_EPS = 1e-5
FFN_RES_WEIGHT = 0.5
ADAM_LR = 0.001
ADAM_B1 = 0.9
ADAM_B2 = 0.999
ADAM_EPS = 1e-08
ADAM_WD = 0.01
ADAM_STEP = 10
VMEM_LIMIT_BYTES = 58 * 1024 * 1024
TM_FFN = 512
TM_MIX = 256
TT_WGRAD = 2048
STRIP = 16
SLOTS_PER_STEP = 2
SUBLANES = 8
RELAY_AT_EIGHTHS = 7

HBM = pl.BlockSpec(memory_space=pl.ANY)


def _dot(a, b):
    return jnp.dot(a, b, preferred_element_type=F32)


def _dot_nt(a, b):
    return lax.dot_general(a, b, (((1,), (1,)), ((), ())), preferred_element_type=F32)


def _dot_tn(a, b):
    return lax.dot_general(a, b, (((0,), (0,)), ((), ())), preferred_element_type=F32)


def _sds(shape, dtype):
    return jax.ShapeDtypeStruct(shape, dtype)


def _rms_stats(xv):
    r = lax.rsqrt(jnp.mean(xv * xv, axis=-1, keepdims=True) + RMS_EPS)
    return r, xv * r


def _swiglu_saved(gate, up):
    sig = jax.nn.sigmoid(gate)
    silu = gate * sig
    return silu, up * (sig * (1.0 + gate * (1.0 - sig))), silu * up


def _rms_bwd(dh, n, r, gain):
    dn = dh * gain
    return r * (dn - n * jnp.mean(dn * n, axis=-1, keepdims=True))


def _place():
    x, y, c = lax.axis_index("x"), lax.axis_index("y"), lax.axis_index("c")
    return x, y, c, [(1 - x, y), (x, 1 - y), (1 - x, 1 - y)]


class Cargo:
    def __init__(self, kind, arrays):
        self.kind, self.arrays = kind, list(arrays)
        n = len(self.arrays)
        self.two_level = kind in ("gather_slots", "gather_chips")
        if self.two_level:
            self.out_shape = [_sds((N_CHIPS,) + a.shape, a.dtype) for a in self.arrays]
        elif kind == "gather_devices":
            self.out_shape = [_sds((N_DEV,) + a.shape, a.dtype) for a in self.arrays]
        else:
            self.out_shape = [_sds(a.shape, a.dtype) for a in self.arrays]
        n_remote = n * {"swap": 1, "gather_devices": N_DEV - 1}.get(kind, N_CHIPS - 1)
        n_own = 0 if kind == "swap" else n
        n_relay = n_remote if self.two_level else 0
        dma = pltpu.SemaphoreType.DMA
        self.scratch = [dma((n_remote,)), dma((n_remote,)), dma((max(n_own, 1),)),
                        dma((max(n_relay, 1),)), dma((max(n_relay, 1),))]

    def _plan(self, ins, outs):
        x, y, c, chips = _place()
        q = 2 * x + y
        sibling = (x, y, 1 - c)
        own, remote, relays = [], [], []
        for a, o in zip(ins, outs):
            if self.two_level:
                half = a.shape[0] // 2
                mine = pl.ds(pl.multiple_of(c * half, SUBLANES), half)
                theirs = pl.ds(pl.multiple_of((1 - c) * half, SUBLANES), half)
                own.append((a, o.at[0 if self.kind == "gather_slots" else q]))
                for j, (px, py) in enumerate(chips):
                    there, here = (j + 1, j + 1) if self.kind == "gather_slots" else (q, 2 * px + py)
                    remote.append((a.at[mine], o.at[there, mine], o.at[here, mine], (px, py, c)))
                    relays.append((o.at[here, mine], o.at[here, mine], o.at[here, theirs], sibling))
            elif self.kind == "scatter_chips":
                own.append((a.at[q], o.at[q]))
                remote += [(a.at[2 * px + py], o.at[q], o.at[2 * px + py], (px, py, c)) for px, py in chips]
            elif self.kind == "swap":
                remote.append((a, o, o, sibling))
            else:
                own.append((a, o.at[4 * x + 2 * y + c]))
                for k in range(1, N_DEV):
                    px, py, pc = x ^ (k >> 2 & 1), y ^ (k >> 1 & 1), c ^ (k & 1)
                    remote.append((a, o.at[4 * x + 2 * y + c], o.at[4 * px + 2 * py + pc], (px, py, pc)))
        return own, remote, relays

    @staticmethod
    def _copies(entries, send_sems, recv_sems):
        out = []
        for k, (src, dst, landed, peer) in enumerate(entries):
            def make(dst_ref, k=k, src=src, peer=peer):
                return pltpu.make_async_remote_copy(src_ref=src, dst_ref=dst_ref, send_sem=send_sems.at[k],
                                                    recv_sem=recv_sems.at[k], device_id=peer, device_id_type=MESH)
            out.append((make(dst), make(landed)))
        return out

    def start(self, ins, outs, sems):
        own, remote, _ = self._plan(ins, outs)
        for k, (src, dst) in enumerate(own):
            pltpu.make_async_copy(src, dst, sems[2].at[k]).start()
        for mine, _ in self._copies(remote, sems[0], sems[1]):
            mine.start()

    def relay(self, ins, outs, sems):
        _, remote, relays = self._plan(ins, outs)
        passed = self._copies(relays, sems[3], sems[4])
        for (_, arriving), (mine, _) in zip(self._copies(remote, sems[0], sems[1]), passed):
            arriving.wait_recv()
            mine.start()

    def wait(self, ins, outs, sems):
        own, remote, relays = self._plan(ins, outs)
        for mine, arriving in self._copies(remote, sems[0], sems[1]):
            mine.wait_send()
            if not self.two_level:
                arriving.wait_recv()
        for mine, arriving in self._copies(relays, sems[3], sems[4]):
            mine.wait_send()
            arriving.wait_recv()
        for k, (src, dst) in enumerate(own):
            pltpu.make_async_copy(src, dst, sems[2].at[k]).wait()


N_CARGO_SEMS = 5


def _call(body, *, name, grid, in_specs, out_specs, out_shape, args, scratch_shapes=(), cargos=()):
    n_in, n_out, n_scr = len(in_specs), len(out_specs), len(scratch_shapes)
    c_in = [len(cg.arrays) for cg in cargos]
    n_cin = sum(c_in)

    def wrapped(*refs):
        ins = refs[:n_in]
        cins = refs[n_in:n_in + n_cin]
        outs = refs[n_in + n_cin:n_in + n_cin + n_out]
        couts = refs[n_in + n_cin + n_out:n_in + 2 * n_cin + n_out]
        scr = refs[n_in + 2 * n_cin + n_out:n_in + 2 * n_cin + n_out + n_scr]
        sems = refs[n_in + 2 * n_cin + n_out + n_scr:]
        step, n_steps = 0, 1
        for ax, size in enumerate(grid):
            step = step * size + pl.program_id(ax)
            n_steps *= size

        def each(method, only_two_level=False):
            at = 0
            for k, cg in enumerate(cargos):
                if cg.two_level or not only_two_level:
                    getattr(cg, method)(cins[at:at + c_in[k]], couts[at:at + c_in[k]],
                                        sems[N_CARGO_SEMS * k:N_CARGO_SEMS * (k + 1)])
                at += c_in[k]

        body(*ins, *outs, *scr)
        if cargos:
            pl.when(step == 0)(lambda: each("start"))
        if any(cg.two_level for cg in cargos):
            pl.when(step == (RELAY_AT_EIGHTHS * n_steps) // 8)(lambda: each("relay", only_two_level=True))
        if cargos:
            pl.when(step == n_steps - 1)(lambda: each("wait"))

    res = pl.pallas_call(
        wrapped, name=name, grid=grid,
        in_specs=list(in_specs) + [HBM] * n_cin,
        out_specs=list(out_specs) + [HBM] * n_cin,
        out_shape=list(out_shape) + [s for cg in cargos for s in cg.out_shape],
        scratch_shapes=list(scratch_shapes) + [s for cg in cargos for s in cg.scratch],
        compiler_params=pltpu.CompilerParams(dimension_semantics=("arbitrary",) * len(grid),
                                             vmem_limit_bytes=VMEM_LIMIT_BYTES),
    )(*args, *[a for cg in cargos for a in cg.arrays])
    outs, rest = list(res[:n_out]), list(res[n_out:])
    cargo_outs = []
    for k in c_in:
        cargo_outs.append(rest[:k])
        rest = rest[k:]
    return outs, cargo_outs


def _exchange(cargo, name):
    _, (outs,) = _call(lambda: None, name=name, grid=(1,), in_specs=[], out_specs=[], out_shape=[], args=[],
                       cargos=[cargo])
    return outs


def _ffn_up_gather(x, gain, wg_t, wu_t, name, cargos=()):
    t_len, d = x.shape
    fq = wg_t.shape[0]
    tm = min(TM_FFN, t_len)
    n_tiles = t_len // tm
    relay_tile = n_tiles // 2
    fetch_tile = min(relay_tile + 1, n_tiles - 1)

    def body(x_ref, g_ref, wg_in, wu_in, h_ref, s_ref, p_ref, a_ref, wg_all, wu_all,
             wg_v, wu_v, h_all, send_sems, recv_sems, pass_send_sems, pass_recv_sems, own_sems, load_sems):
        s = pl.program_id(0)
        i = pl.program_id(1)
        x_, y_, c_, chips = _place()
        shards = ((wg_in, wg_all, wg_v), (wu_in, wu_all, wu_v))
        mine = pl.ds(pl.multiple_of(c_ * (fq // 2), SUBLANES), fq // 2)
        theirs = pl.ds(pl.multiple_of((1 - c_) * (fq // 2), SUBLANES), fq // 2)

        def to_peer(k, j):
            w_in, w_all, _ = shards[k]
            return pltpu.make_async_remote_copy(
                src_ref=w_in.at[mine], dst_ref=w_all.at[j + 1, mine], send_sem=send_sems.at[3 * k + j],
                recv_sem=recv_sems.at[3 * k + j], device_id=(*chips[j], c_), device_id_type=MESH)

        def to_sibling(k, j, landing=False):
            w_all = shards[k][1]
            return pltpu.make_async_remote_copy(
                src_ref=w_all.at[j + 1, mine], dst_ref=w_all.at[j + 1, theirs if landing else mine],
                send_sem=pass_send_sems.at[3 * k + j], recv_sem=pass_recv_sems.at[3 * k + j],
                device_id=(x_, y_, 1 - c_), device_id_type=MESH)

        def keep(k):
            return pltpu.make_async_copy(shards[k][0], shards[k][1].at[0], own_sems.at[k])

        @pl.when((s == 0) & (i == 0))
        def _():
            for j in range(N_CHIPS - 1):
                for k in range(2):
                    to_peer(k, j).start()
            for k in range(2):
                keep(k).start()

        def load(k, slot):
            src = shards[k][0] if slot == 0 else shards[k][1].at[slot]
            return pltpu.make_async_copy(src, shards[k][2].at[slot % 2], load_sems.at[k])

        @pl.when((s == 0) & (i == 0))
        def _():
            for k in range(2):
                load(k, 0).start()
            for k in range(2):
                load(k, 0).wait()

        def pass_on(slot):
            for k in range(2):
                to_peer(k, slot - 1).wait_recv()
                to_sibling(k, slot - 1).start()

        def fetch(slot):
            for k in range(2):
                to_sibling(k, slot - 1, landing=True).wait_recv()
                load(k, slot).start()

        for slot in range(1, N_CHIPS):
            pl.when((s == slot - 1) & (i == relay_tile))(functools.partial(pass_on, slot))
            pl.when((s == slot - 1) & (i == fetch_tile))(functools.partial(fetch, slot))

            @pl.when((s == slot) & (i == 0))
            def _():
                for k in range(2):
                    load(k, slot).wait()

        @pl.when(s == 0)
        def _():
            _, n = _rms_stats(x_ref[...])
            h_new = (n * g_ref[...]).astype(BF16)
            h_ref[...] = h_new
            h_all[i] = h_new

        h = h_all[i]
        silu, dgate, act = _swiglu_saved(_dot_nt(h, wg_v[s % 2]), _dot_nt(h, wu_v[s % 2]))
        s_ref[...] = silu.astype(BF16)
        p_ref[...] = dgate.astype(BF16)
        a_ref[...] = act.astype(BF16)

        @pl.when((s == N_CHIPS - 1) & (i == n_tiles - 1))
        def _():
            for k in range(2):
                for j in range(N_CHIPS - 1):
                    to_peer(k, j).wait_send()
                    to_sibling(k, j).wait_send()
                keep(k).wait()

    tok = pl.BlockSpec((tm, d), lambda s, i: (jnp.where(s == 0, i, n_tiles - 1), 0))
    hid = pl.BlockSpec((None, tm, fq), lambda s, i: (s, i, 0))
    outs, cargo_outs = _call(
        body, name=name, grid=(N_CHIPS, n_tiles),
        in_specs=[tok, pl.BlockSpec((1, d), lambda s, i: (0, 0)), HBM, HBM],
        out_specs=[tok, hid, hid, hid, HBM, HBM],
        out_shape=[_sds((t_len, d), BF16)] + [_sds((N_CHIPS, t_len, fq), BF16)] * 3
        + [_sds((N_CHIPS, fq, d), BF16)] * 2,
        scratch_shapes=[pltpu.VMEM((2, fq, d), BF16), pltpu.VMEM((2, fq, d), BF16),
                        pltpu.VMEM((n_tiles, tm, d), BF16)]
        + [pltpu.SemaphoreType.DMA((6,))] * 4 + [pltpu.SemaphoreType.DMA((2,))] * 2,
        args=[x, gain, wg_t, wu_t], cargos=cargos)
    return outs, cargo_outs


def _load_once(hbm_refs, vmem_refs, sems, first):
    @pl.when(first)
    def _():
        copies = [pltpu.make_async_copy(src, dst, sems.at[k]) for k, (src, dst) in enumerate(zip(hbm_refs, vmem_refs))]
        for cp in copies:
            cp.start()
        for cp in copies:
            cp.wait()


def _ffn_down(x, act, wd, name, cargos=()):
    t_len, d = x.shape
    nq, fq, _ = wd.shape
    tm = min(TM_FFN, t_len)

    def body(x_ref, a_ref, wd_ref, xo_ref):
        y = _dot(a_ref[0], wd_ref[0])
        for j in range(1, nq):
            y = y + _dot(a_ref[j], wd_ref[j])
        xo_ref[...] = x_ref[...] + FFN_RES_WEIGHT * y

    tok = pl.BlockSpec((tm, d), lambda i: (i, 0))
    (xo,), cargo_outs = _call(
        body, name=name, grid=(t_len // tm,),
        in_specs=[tok, pl.BlockSpec((nq, tm, fq), lambda i: (0, i, 0)), pl.BlockSpec((nq, fq, d), lambda i: (0, 0, 0))],
        out_specs=[tok], out_shape=[_sds((t_len, d), F32)], args=[x, act, wd], cargos=cargos)
    return xo, cargo_outs


def _ffn_fwd(x, gain, wg_t, wu_t, wd, name):
    t_len, d = x.shape
    nq, fq, _ = wd.shape
    tm = min(TM_FFN, t_len)

    def body(x_ref, g_ref, wg_hbm, wu_hbm, wd_hbm, xo_ref, h_ref, s_ref, p_ref, a_ref,
             h_s, acc, wg_v, wu_v, wd_v, load_sems):
        i = pl.program_id(0)
        j = pl.program_id(1)
        _load_once((wg_hbm, wu_hbm, wd_hbm), (wg_v, wu_v, wd_v), load_sems, (i == 0) & (j == 0))

        @pl.when(j == 0)
        def _():
            _, n = _rms_stats(x_ref[...])
            h = (n * g_ref[...]).astype(BF16)
            h_s[...] = h
            h_ref[...] = h
            acc[...] = jnp.zeros_like(acc)

        h = h_s[...]
        y = None
        for jj in range(SLOTS_PER_STEP):
            slot = j * SLOTS_PER_STEP + jj
            silu, dgate, act = _swiglu_saved(_dot_nt(h, wg_v[slot]), _dot_nt(h, wu_v[slot]))
            s_ref[jj] = silu.astype(BF16)
            p_ref[jj] = dgate.astype(BF16)
            a_ref[jj] = act.astype(BF16)
            part = _dot(a_ref[jj], wd_v[slot])
            y = part if y is None else y + part
        acc[...] += y

        @pl.when(j == nq // SLOTS_PER_STEP - 1)
        def _():
            xo_ref[...] = x_ref[...] + FFN_RES_WEIGHT * acc[...]

    tok = pl.BlockSpec((tm, d), lambda i, j: (i, 0))
    hid = pl.BlockSpec((SLOTS_PER_STEP, tm, fq), lambda i, j: (j, i, 0))
    outs, _ = _call(
        body, name=name, grid=(t_len // tm, nq // SLOTS_PER_STEP),
        in_specs=[tok, pl.BlockSpec((1, d), lambda i, j: (0, 0)), HBM, HBM, HBM],
        out_specs=[tok, tok, hid, hid, hid],
        out_shape=[_sds((t_len, d), F32), _sds((t_len, d), BF16)] + [_sds((nq, t_len, fq), BF16)] * 3,
        scratch_shapes=[pltpu.VMEM((tm, d), BF16), pltpu.VMEM((tm, d), F32)]
        + [pltpu.VMEM((nq, fq, d), BF16)] * 3 + [pltpu.SemaphoreType.DMA((3,))],
        args=[x, gain, wg_t, wu_t, wd])
    return outs


def _ffn_bwd(dy, x_in, gain, silu, dgate_du, wg_t, wu_t, wd, name):
    t_len, d = dy.shape
    nq, fq, _ = wd.shape
    tm = min(TM_FFN, t_len)

    def body(dy_ref, x_ref, g_ref, s_ref, p_ref, wg_hbm, wu_hbm, wd_hbm,
             dx_ref, dgain_ref, df_ref, dg_ref, du_ref, df_s, dh_acc, dact_s, wg_v, wu_v, wd_v, load_sems):
        i = pl.program_id(0)
        j = pl.program_id(1)
        _load_once((wg_hbm, wu_hbm, wd_hbm), (wg_v, wu_v, wd_v), load_sems, (i == 0) & (j == 0))

        @pl.when((i == 0) & (j == 0))
        def _():
            dgain_ref[...] = jnp.zeros_like(dgain_ref)

        @pl.when(j == 0)
        def _():
            df = (FFN_RES_WEIGHT * dy_ref[...]).astype(BF16)
            df_s[...] = df
            df_ref[...] = df
            dh_acc[...] = jnp.zeros_like(dh_acc)

        slots = [j * SLOTS_PER_STEP + jj for jj in range(SLOTS_PER_STEP)]
        for jj, slot in enumerate(slots):
            dact_s[jj] = _dot_nt(df_s[...], wd_v[slot])

        for jj in range(SLOTS_PER_STEP):
            for r0 in range(0, tm, STRIP):
                rows = slice(r0, r0 + STRIP)
                dact = dact_s[jj, rows, :]
                dg_ref[jj, rows, :] = (dact * p_ref[jj, rows, :].astype(F32)).astype(BF16)
                du_ref[jj, rows, :] = (dact * s_ref[jj, rows, :].astype(F32)).astype(BF16)

        dh = None
        for jj, slot in enumerate(slots):
            part = _dot(dg_ref[jj], wg_v[slot]) + _dot(du_ref[jj], wu_v[slot])
            dh = part if dh is None else dh + part
        dh_acc[...] += dh

        @pl.when(j == nq // SLOTS_PER_STEP - 1)
        def _():
            r, n = _rms_stats(x_ref[...])
            dh = dh_acc[...]
            dgain_ref[...] += jnp.sum(dh * n, axis=0, keepdims=True)
            dx_ref[...] = dy_ref[...] + _rms_bwd(dh, n, r, g_ref[...])

    tok = pl.BlockSpec((tm, d), lambda i, j: (i, 0))
    vec = pl.BlockSpec((1, d), lambda i, j: (0, 0))
    hid = pl.BlockSpec((SLOTS_PER_STEP, tm, fq), lambda i, j: (j, i, 0))
    outs, _ = _call(
        body, name=name, grid=(t_len // tm, nq // SLOTS_PER_STEP),
        in_specs=[tok, tok, vec, hid, hid, HBM, HBM, HBM],
        out_specs=[tok, vec, tok, hid, hid],
        out_shape=[_sds((t_len, d), F32), _sds((1, d), F32), _sds((t_len, d), BF16),
                   _sds((nq, t_len, fq), BF16), _sds((nq, t_len, fq), BF16)],
        scratch_shapes=[pltpu.VMEM((tm, d), BF16), pltpu.VMEM((tm, d), F32),
                        pltpu.VMEM((SLOTS_PER_STEP, tm, fq), F32)]
        + [pltpu.VMEM((nq, fq, d), BF16)] * 3 + [pltpu.SemaphoreType.DMA((3,))],
        args=[dy, x_in, gain, silu, dgate_du, wg_t, wu_t, wd])
    return outs


def _wgrad(lhs, rhs, l_spec, r_spec, out_shape, out_spec, acc_shape, grid, name, cargos=()):
    n_t = grid[-1]
    t_axis = len(grid) - 1

    def body(l_ref, r_ref, o_ref, acc):
        t = pl.program_id(t_axis)

        @pl.when(t == 0)
        def _():
            acc[...] = jnp.zeros_like(acc)

        acc[...] += _dot_tn(l_ref[...].astype(BF16), r_ref[...].astype(BF16))

        @pl.when(t == n_t - 1)
        def _():
            o_ref[...] = acc[...].astype(o_ref.dtype)

    (out,), cargo_outs = _call(
        body, name=name, grid=grid, in_specs=[l_spec, r_spec], out_specs=[out_spec], out_shape=[out_shape],
        scratch_shapes=[pltpu.VMEM(acc_shape, F32)], args=[lhs, rhs], cargos=cargos)
    return out, cargo_outs


def _wgrad_hid_tok_scatter(hids, tok, name, cargos=()):
    t_len, d = tok.shape
    n_w = len(hids)
    nq, _, fq = hids[0].shape
    half = fq // 2
    tt = min(TT_WGRAD, t_len)
    n_t = t_len // tt
    per_w = 4
    n_sem = 6

    def body(*refs):
        l_refs, r_ref, parts_refs = refs[:n_w], refs[n_w], refs[n_w + 1:2 * n_w + 1]
        scr = refs[2 * n_w + 1:]
        bufs = [scr[per_w * w:per_w * (w + 1)] for w in range(n_w)]
        zeros = scr[per_w * n_w]
        sems = [scr[per_w * n_w + 1 + n_sem * w:per_w * n_w + 1 + n_sem * (w + 1)] for w in range(n_w)]
        g = pl.program_id(0)
        t = pl.program_id(1)
        x_, y_, c_, chips = _place()
        mine = pl.ds(pl.multiple_of(c_ * half, STRIP), half)
        theirs = pl.ds(pl.multiple_of((1 - c_) * half, STRIP), half)

        def to_sibling(w, slot):
            return pltpu.make_async_remote_copy(
                src_ref=bufs[w][1].at[theirs], dst_ref=bufs[w][2].at[slot], send_sem=sems[w][0].at[slot],
                recv_sem=sems[w][1].at[slot], device_id=(x_, y_, 1 - c_), device_id_type=MESH)

        def to_peer(w, j):
            return pltpu.make_async_remote_copy(
                src_ref=bufs[w][3].at[j + 1], dst_ref=parts_refs[w].at[j + 1, mine], send_sem=sems[w][2].at[j],
                recv_sem=sems[w][3].at[j], device_id=(*chips[j], c_), device_id_type=MESH)

        def keep(w):
            return pltpu.make_async_copy(bufs[w][3].at[0], parts_refs[w].at[0, mine], sems[w][4])

        def blank(w, slot):
            return pltpu.make_async_copy(zeros, parts_refs[w].at[slot, theirs], sems[w][5].at[slot])

        @pl.when((g == 0) & (t == 0))
        def _():
            zeros[...] = jnp.zeros_like(zeros)
            for w in range(n_w):
                for slot in range(nq):
                    blank(w, slot).start()

        @pl.when(t == 0)
        def _():
            for w in range(n_w):
                bufs[w][0][...] = jnp.zeros_like(bufs[w][0])

        rhs = r_ref[...]
        for w in range(n_w):
            bufs[w][0][...] += _dot_tn(l_refs[w][...], rhs)

        for step in range(nq):
            slot = (step + 1) % nq

            @pl.when((g == step) & (t == n_t - 1))
            def _():
                for w in range(n_w):
                    acc, stage, _, _ = bufs[w]
                    if step > 0:
                        to_sibling(w, step).wait_send()
                    stage[...] = acc[...].astype(BF16)
                    to_sibling(w, slot).start()
                for w in range(n_w):
                    _, stage, pair, summed = bufs[w]
                    to_sibling(w, slot).wait_recv()
                    summed[slot] = (stage[mine, :].astype(F32) + pair[slot].astype(F32)).astype(BF16)
                    if slot > 0:
                        to_peer(w, slot - 1).start()
                    else:
                        keep(w).start()

        @pl.when((g == nq - 1) & (t == n_t - 1))
        def _():
            for w in range(n_w):
                for j in range(N_CHIPS - 1):
                    to_peer(w, j).wait()
                keep(w).wait()
                to_sibling(w, 0).wait_send()
                for slot in range(nq):
                    blank(w, slot).wait()

    dma = pltpu.SemaphoreType.DMA
    scratch = []
    for _ in range(n_w):
        scratch += [pltpu.VMEM((fq, d), F32), pltpu.VMEM((fq, d), BF16), pltpu.VMEM((nq, half, d), BF16),
                    pltpu.VMEM((nq, half, d), BF16)]
    scratch.append(pltpu.VMEM((half, d), BF16))
    for _ in range(n_w):
        scratch += [dma((nq,)), dma((nq,)), dma((N_CHIPS - 1,)), dma((N_CHIPS - 1,)), dma(()), dma((nq,))]
    parts, cargo_outs = _call(
        body, name=name, grid=(nq, n_t),
        in_specs=[pl.BlockSpec((None, tt, fq), lambda g, t: ((g + 1) % nq, t, 0))] * n_w
        + [pl.BlockSpec((tt, d), lambda g, t: (t, 0))],
        out_specs=[HBM] * n_w, out_shape=[_sds((nq, fq, d), BF16)] * n_w,
        scratch_shapes=scratch, args=[*hids, tok], cargos=cargos)
    return parts, cargo_outs


def _wgrad_2d(lhs, rhs, n_col_blocks, out_dtype, name, group_diag=False, cargos=()):
    t_len, k = lhs.shape
    n = rhs.shape[1]
    nb = n // n_col_blocks
    kb = k // n_col_blocks if group_diag else k
    tt = min(TT_WGRAD, t_len)
    l_map = (lambda q, t: (t, q)) if group_diag else (lambda q, t: (t, 0))
    return _wgrad(lhs, rhs,
                  pl.BlockSpec((tt, kb), l_map),
                  pl.BlockSpec((tt, nb), lambda q, t: (t, q)),
                  _sds((n_col_blocks, kb, nb), out_dtype),
                  pl.BlockSpec((None, kb, nb), lambda q, t: (q, 0, 0)),
                  (kb, nb), (n_col_blocks, t_len // tt), name, cargos)


def _layernorm_stats(u1):
    mu = jnp.mean(u1, axis=-1, keepdims=True)
    xc = u1 - mu
    rstd = lax.rsqrt(jnp.mean(xc * xc, axis=-1, keepdims=True) + LN_EPS)
    return rstd, xc * rstd


def _positions(i, tm, rows, offset=0):
    return (lax.broadcasted_iota(jnp.int32, (rows, 1), 0) + (i * tm + offset)).astype(F32)


SHIFT_ROWS = HALO - SUBLANES


def _fill_shifted(ext_s, sh_s, tm):
    for b in range(1, SUBLANES):
        sh_s[b - 1] = ext_s[pl.ds(b, tm + SHIFT_ROWS), :]


def _window(ext_s, sh_s, shift, tm):
    a, b = divmod(shift, SUBLANES)
    if b == 0:
        return ext_s[pl.ds(shift, tm), :]
    return sh_s[b - 1, pl.ds(a * SUBLANES, tm), :]


def _window_sums(ext_s, lv_a, lv_b, tm, ahead):
    g = POOL_GROUP
    sign = 1 if ahead else -1
    for n, (dst, src, c0) in enumerate(((lv_a, ext_s, 0), (lv_b, lv_a, g), (lv_a, lv_b, 2 * g)), start=1):
        lo = 0 if ahead else n * SUBLANES
        rows = tm + HALO - n * SUBLANES
        shift = sign * 2 ** (n - 1)
        dst[pl.ds(lo, rows), c0:] = src[pl.ds(lo, rows), c0:] + src[pl.ds(lo + shift, rows), c0:]
    base = 0 if ahead else HALO
    rows = pl.ds(base, tm)
    far = pl.ds(base + sign * SUBLANES, tm)
    return [lv_a[rows, 0:g], lv_b[rows, g:2 * g], lv_a[rows, 2 * g:3 * g],
            lv_a[rows, 3 * g:] + lv_a[far, 3 * g:]]


def _tile(tm, cols):
    return pl.BlockSpec((tm, cols), lambda i: (i, 0))


def _whole(shape):
    return pl.BlockSpec(shape, lambda i: (0,) * len(shape))


def _mix_fwd(x1, gain, w_in, conv_dw, conv_b, ln_g, ln_b, conv_pw, pool_w, pool_scale, w_out, name, cargos=()):
    t_len, d = x1.shape
    nq, _, nb = w_in.shape
    tm = min(TM_MIX, t_len)

    def body(x_ref, g_ref, wi_ref, dw_ref, cb_ref, lg_ref, lb_ref, pw_ref, plw_ref, ps_ref, wo_ref,
             x2_ref, h_ref, p_ref, u1_ref, u3_ref, mx_ref, cat_ref, ext_s, pext_s, sh_s, tail_s, lva_s, lvb_s):
        i = pl.program_id(0)

        @pl.when(i == 0)
        def _():
            tail_s[...] = jnp.zeros_like(tail_s)

        _, n = _rms_stats(x_ref[...])
        h = (n * g_ref[...]).astype(BF16)
        h_ref[...] = h
        for q in range(nq):
            p_ref[:, q * nb:(q + 1) * nb] = _dot(h, wi_ref[q])

        a = p_ref[:, 0:D_CONV]
        g = p_ref[:, D_CONV:2 * D_CONV]
        p = p_ref[:, 2 * D_CONV:]
        ext_s[0:HALO, :] = tail_s[:, 0:D_CONV] * jax.nn.sigmoid(tail_s[:, D_CONV:2 * D_CONV])
        ext_s[HALO:, :] = a * jax.nn.sigmoid(g)
        pext_s[0:HALO, :] = tail_s[:, 2 * D_CONV:]
        pext_s[HALO:, :] = p
        tail_s[...] = p_ref[tm - HALO:tm, :]

        _fill_shifted(ext_s, sh_s, tm)
        u1 = jnp.broadcast_to(cb_ref[...], (tm, D_CONV))
        for k in range(CONV_WIDTH):
            u1 = u1 + dw_ref[k:k + 1, :] * _window(ext_s, sh_s, HALO - (CONV_WIDTH - 1) + k, tm)
        u1_ref[...] = u1
        _, nhat = _layernorm_stats(u1)
        u2 = nhat * lg_ref[...] + lb_ref[...]
        u3 = (u2 * jax.nn.sigmoid(u2)).astype(BF16)
        u3_ref[...] = u3
        cat_ref[:, 0:D_CONV] = _dot(u3, pw_ref[...]).astype(BF16)

        pos1 = _positions(i, tm, tm) + 1.0
        sums = _window_sums(pext_s, lva_s, lvb_s, tm, ahead=False)
        for gi, w in enumerate(POOL_WINDOWS):
            cols = slice(gi * POOL_GROUP, (gi + 1) * POOL_GROUP)
            mixed = (sums[gi] / jnp.minimum(pos1, float(w)) - p[:, cols]).astype(BF16)
            mx_ref[:, cols] = mixed
            out = _dot(mixed, plw_ref[gi]) * ps_ref[:, cols]
            cat_ref[:, D_CONV + gi * POOL_GROUP:D_CONV + (gi + 1) * POOL_GROUP] = out.astype(BF16)

        x2_ref[...] = x_ref[...] + _dot(cat_ref[...], wo_ref[...])

    return _call(
        body, name=name, grid=(t_len // tm,),
        in_specs=[_tile(tm, d), _whole((1, d)), _whole((nq, d, nb)), _whole((CONV_WIDTH + 1, D_CONV)),
                  _whole((1, D_CONV)), _whole((1, D_CONV)), _whole((1, D_CONV)), _whole((D_CONV, D_CONV)),
                  _whole((4, POOL_GROUP, POOL_GROUP)), _whole((1, D_POOL)), _whole((D_CONV + D_POOL, d))],
        out_specs=[_tile(tm, d), _tile(tm, d), _tile(tm, D_IN), _tile(tm, D_CONV), _tile(tm, D_CONV),
                   _tile(tm, D_POOL), _tile(tm, D_CONV + D_POOL)],
        out_shape=[_sds((t_len, d), F32), _sds((t_len, d), BF16), _sds((t_len, D_IN), F32),
                   _sds((t_len, D_CONV), F32), _sds((t_len, D_CONV), BF16), _sds((t_len, D_POOL), BF16),
                   _sds((t_len, D_CONV + D_POOL), BF16)],
        scratch_shapes=[pltpu.VMEM((tm + HALO, D_CONV), F32), pltpu.VMEM((tm + HALO, D_POOL), F32),
                        pltpu.VMEM((SUBLANES - 1, tm + SHIFT_ROWS, D_CONV), F32), pltpu.VMEM((HALO, D_IN), F32)]
        + [pltpu.VMEM((tm + HALO, D_POOL), F32)] * 2,
        args=[x1, gain, w_in, conv_dw, conv_b, ln_g, ln_b, conv_pw, pool_w, pool_scale, w_out], cargos=cargos)


def _mix_bwd(dx2, u1, mixed, proj, x1, gain, conv_dw, ln_g, ln_b, conv_pw, pool_w, pool_scale, w_out, w_in,
             name, cargos=()):
    t_len, d = x1.shape
    nq, _, nb = w_in.shape
    tm = min(TM_MIX, t_len)
    hb = tm // HALO
    n_tiles = t_len // tm

    def body(dxn_ref, u1_ref, mx_ref, p_ref, tail_ref, x_ref, dx2_ref, g_ref, dw_ref, lg_ref, lb_ref, pw_ref,
             plw_ref, ps_ref, wo_ref, wi_ref,
             dx1_ref, dp_ref, dco_ref, dpo_ref, ddw_ref, dcb_ref, dlg_ref, dlb_ref, dps_ref, dgain_ref,
             du_s, dm_s, uext_s, dext_s, mext_s, ush_s, dsh_s, lva_s, lvb_s):
        k = pl.program_id(0)

        @pl.when(k == 0)
        def _():
            for ref in (ddw_ref, dcb_ref, dlg_ref, dlb_ref, dps_ref, dgain_ref, du_s, dm_s):
                ref[...] = jnp.zeros_like(ref)

        counts = jnp.where(k < n_tiles, 1.0, 0.0)
        dcat = _dot_nt(dxn_ref[...].astype(BF16), wo_ref[...])
        dco = dcat[:, 0:D_CONV].astype(BF16)
        dco_ref[...] = dco
        du3 = _dot_nt(dco, pw_ref[...])
        rstd, nhat = _layernorm_stats(u1_ref[...])
        u2 = nhat * lg_ref[...] + lb_ref[...]
        sig = jax.nn.sigmoid(u2)
        du2 = du3 * (sig * (1.0 + u2 * (1.0 - sig)))
        dlg_ref[...] += counts * jnp.sum(du2 * nhat, axis=0, keepdims=True)
        dlb_ref[...] += counts * jnp.sum(du2, axis=0, keepdims=True)
        dnhat = du2 * lg_ref[...]
        du_s[k % 2] = rstd * (dnhat - jnp.mean(dnhat, axis=-1, keepdims=True)
                              - nhat * jnp.mean(dnhat * nhat, axis=-1, keepdims=True))
        for gi in range(len(POOL_WINDOWS)):
            cols = slice(gi * POOL_GROUP, (gi + 1) * POOL_GROUP)
            dpo = dcat[:, D_CONV + gi * POOL_GROUP:D_CONV + (gi + 1) * POOL_GROUP]
            pre = _dot(mx_ref[:, cols], plw_ref[gi])
            dps_ref[:, cols] += counts * jnp.sum(dpo * pre, axis=0, keepdims=True)
            dout = (dpo * ps_ref[:, cols]).astype(BF16)
            dpo_ref[:, cols] = dout
            dm_s[k % 2, :, cols] = _dot_nt(dout, plw_ref[gi])

        i = jnp.maximum(k - 1, 0)
        cur, nxt = (k + 1) % 2, k % 2
        first = k <= 1
        last = (k == n_tiles) | (k == 0)
        a = p_ref[:, 0:D_CONV]
        g = p_ref[:, D_CONV:2 * D_CONV]
        sg = jax.nn.sigmoid(g)
        ta = tail_ref[:, 0:D_CONV]
        tg = tail_ref[:, D_CONV:2 * D_CONV]
        uext_s[0:HALO, :] = jnp.where(first, 0.0, ta * jax.nn.sigmoid(tg))
        uext_s[HALO:, :] = a * sg
        du1 = du_s[cur]
        dext_s[0:tm, :] = du1
        dext_s[tm:, :] = jnp.where(last, 0.0, du_s[nxt, 0:HALO, :])

        _fill_shifted(uext_s, ush_s, tm)
        _fill_shifted(dext_s, dsh_s, tm)
        du0 = jnp.zeros((tm, D_CONV), F32)
        for tap in range(CONV_WIDTH):
            du0 = du0 + dw_ref[tap:tap + 1, :] * _window(dext_s, dsh_s, CONV_WIDTH - 1 - tap, tm)
            ddw_ref[tap:tap + 1, :] += jnp.sum(
                du1 * _window(uext_s, ush_s, HALO - (CONV_WIDTH - 1) + tap, tm), axis=0, keepdims=True)
        dcb_ref[...] += jnp.sum(du1, axis=0, keepdims=True)
        dp_ref[:, 0:D_CONV] = (du0 * sg).astype(BF16)
        dp_ref[:, D_CONV:2 * D_CONV] = (du0 * a * sg * (1.0 - sg)).astype(BF16)

        pos1 = _positions(i, tm, tm) + 1.0
        pos1_next = _positions(i, tm, HALO, offset=tm) + 1.0
        for gi, w in enumerate(POOL_WINDOWS):
            cols = slice(gi * POOL_GROUP, (gi + 1) * POOL_GROUP)
            dm = dm_s[cur, :, cols]
            mext_s[0:tm, cols] = dm / jnp.minimum(pos1, float(w))
            mext_s[tm:, cols] = jnp.where(last, 0.0, dm_s[nxt, 0:HALO, cols] / jnp.minimum(pos1_next, float(w)))
        sums = _window_sums(mext_s, lva_s, lvb_s, tm, ahead=True)
        for gi in range(len(POOL_WINDOWS)):
            cols = slice(gi * POOL_GROUP, (gi + 1) * POOL_GROUP)
            dp_ref[:, 2 * D_CONV + gi * POOL_GROUP:2 * D_CONV + (gi + 1) * POOL_GROUP] = (
                sums[gi] - dm_s[cur, :, cols]).astype(BF16)

        dh = _dot_nt(dp_ref[:, 0:nb], wi_ref[0])
        for q in range(1, nq):
            dh = dh + _dot_nt(dp_ref[:, q * nb:(q + 1) * nb], wi_ref[q])
        r, n = _rms_stats(x_ref[...])
        dgain_ref[...] += jnp.sum(dh * n, axis=0, keepdims=True)
        dx1_ref[...] = dx2_ref[...] + _rms_bwd(dh, n, r, g_ref[...])

    def ahead(cols):
        return pl.BlockSpec((tm, cols), lambda k: (jnp.minimum(k, n_tiles - 1), 0))

    def behind(cols):
        return pl.BlockSpec((tm, cols), lambda k: (jnp.maximum(k - 1, 0), 0))

    vec = _whole((1, D_CONV))
    return _call(
        body, name=name, grid=(n_tiles + 1,),
        in_specs=[ahead(d), ahead(D_CONV), ahead(D_POOL), behind(D_IN),
                  pl.BlockSpec((HALO, D_IN), lambda k: (jnp.maximum(jnp.maximum(k - 1, 0) * hb - 1, 0), 0)),
                  behind(d), behind(d), _whole((1, d)), _whole((CONV_WIDTH + 1, D_CONV)), vec, vec,
                  _whole((D_CONV, D_CONV)), _whole((4, POOL_GROUP, POOL_GROUP)), vec,
                  _whole((D_CONV + D_POOL, d)), _whole((nq, d, nb))],
        out_specs=[behind(d), behind(D_IN), ahead(D_CONV), ahead(D_POOL), _whole((CONV_WIDTH + 1, D_CONV)), vec,
                   vec, vec, vec, _whole((1, d))],
        out_shape=[_sds((t_len, d), F32), _sds((t_len, D_IN), BF16), _sds((t_len, D_CONV), BF16),
                   _sds((t_len, D_POOL), BF16), _sds((CONV_WIDTH + 1, D_CONV), F32), _sds((1, D_CONV), F32),
                   _sds((1, D_CONV), F32), _sds((1, D_CONV), F32), _sds((1, D_POOL), F32), _sds((1, d), F32)],
        scratch_shapes=[pltpu.VMEM((2, tm, D_CONV), F32), pltpu.VMEM((2, tm, D_POOL), F32),
                        pltpu.VMEM((tm + HALO, D_CONV), F32), pltpu.VMEM((tm + HALO, D_CONV), F32),
                        pltpu.VMEM((tm + HALO, D_POOL), F32),
                        pltpu.VMEM((SUBLANES - 1, tm + SHIFT_ROWS, D_CONV), F32),
                        pltpu.VMEM((SUBLANES - 1, tm + SHIFT_ROWS, D_CONV), F32)]
        + [pltpu.VMEM((tm + HALO, D_POOL), F32)] * 2,
        args=[dx2, u1, mixed, proj, proj, x1, dx2, gain, conv_dw, ln_g, ln_b, conv_pw, pool_w, pool_scale, w_out,
              w_in], cargos=cargos)


def _final_norm_loss(x3, target, gain, name):
    t_len, d = x3.shape
    tm = min(2 * TM_FFN, t_len)

    def body(x_ref, t_ref, g_ref, dx_ref, loss_ref, dgain_ref):
        @pl.when(pl.program_id(0) == 0)
        def _():
            loss_ref[...] = jnp.zeros_like(loss_ref)
            dgain_ref[...] = jnp.zeros_like(dgain_ref)

        r, n = _rms_stats(x_ref[...])
        err = n * g_ref[...] - t_ref[...]
        per_tok = jnp.sum(err * err, axis=-1, keepdims=True) * (1.0 / d)
        loss_ref[...] += 0.5 * jnp.sum(per_tok, axis=0, keepdims=True)
        dy = err * (1.0 / d)
        dgain_ref[...] += jnp.sum(dy * n, axis=0, keepdims=True)
        dx_ref[...] = _rms_bwd(dy, n, r, g_ref[...])

    tok = pl.BlockSpec((tm, d), lambda i: (i, 0))
    outs, _ = _call(
        body, name=name, grid=(t_len // tm,),
        in_specs=[tok, tok, pl.BlockSpec((1, d), lambda i: (0, 0))],
        out_specs=[tok, pl.BlockSpec((1, 128), lambda i: (0, 0)), pl.BlockSpec((1, d), lambda i: (0, 0))],
        out_shape=[_sds((t_len, d), F32), _sds((1, 128), F32), _sds((1, d), F32)],
        args=[x3, target, gain])
    return outs


def _row_tile(rows):
    return rows // 4 if rows % 64 == 0 else rows


def _adamw_math(w, g, m, v):
    m = ADAM_B1 * m + (1.0 - ADAM_B1) * g
    v = ADAM_B2 * v + (1.0 - ADAM_B2) * (g * g)
    m_hat = m / (1.0 - ADAM_B1 ** ADAM_STEP)
    v_hat = v / (1.0 - ADAM_B2 ** ADAM_STEP)
    delta = -ADAM_LR * (m_hat / (jnp.sqrt(v_hat) + ADAM_EPS) + ADAM_WD * w)
    return delta, m, v


def _adamw(parts, w, m, v, name):
    r, c = w.shape
    n = len(parts)
    tr = _row_tile(r)

    def body(*refs):
        g = None
        for p_ref in refs[:n]:
            s = p_ref[0].astype(F32)
            for k in range(1, p_ref.shape[0]):
                s = s + p_ref[k].astype(F32)
            g = s if g is None else g + s
        w_ref, m_ref, v_ref, g_out, d_out, m_out, v_out = refs[n:]
        delta, nm, nv = _adamw_math(w_ref[...], g, m_ref[...], v_ref[...])
        g_out[...] = g
        d_out[...] = delta
        m_out[...] = nm
        v_out[...] = nv

    blk = pl.BlockSpec((tr, c), lambda i: (i, 0))
    p_specs = [pl.BlockSpec((p.shape[0], tr, c), lambda i: (0, i, 0)) for p in parts]
    outs, _ = _call(body, name=name, grid=(r // tr,), in_specs=p_specs + [blk, blk, blk],
                    out_specs=[blk] * 4, out_shape=[_sds((r, c), F32)] * 4, args=[*parts, w, m, v])
    return outs


FFN_W = ("w_gate", "w_up", "w_down")
MID = ("w_in", "conv_dw", "conv_pw", "w_out")
SMALL_1024 = ("ffn1_norm", "mix_norm", "ffn2_norm", "final_norm")
SMALL_512 = ("conv_dw_b", "conv_ln_g", "conv_ln_b", "pool_scale")
WEIGHTS = ("ffn1_norm", "ffn1_w_gate", "ffn1_w_up", "ffn1_w_down", "mix_norm", "w_in", "conv_dw", "conv_dw_b",
           "conv_ln_g", "conv_ln_b", "conv_pw", "pool_w", "pool_scale", "w_out", "ffn2_norm", "ffn2_w_gate",
           "ffn2_w_up", "ffn2_w_down", "final_norm")
PACK_ROWS = 72
PACK_LOSS_ROW = 70


def _pad_rows(a, rows):
    return jnp.pad(a, ((0, rows - a.shape[0]), (0, 0)))


def _pack_small(t, spare=None):
    rows = [t[k].reshape(1, D_MODEL) for k in SMALL_1024]
    rows.append(jnp.concatenate([t["conv_dw_b"].reshape(1, -1), t["conv_ln_g"].reshape(1, -1)], axis=1))
    rows.append(jnp.concatenate([t["conv_ln_b"].reshape(1, -1), t["pool_scale"].reshape(1, -1)], axis=1))
    rows.append(t["pool_w"].reshape(64, D_MODEL))
    if spare is not None:
        rows.append(jnp.pad(spare, ((0, 0), (0, D_MODEL - spare.shape[1]))))
    return _pad_rows(jnp.concatenate(rows, axis=0), PACK_ROWS)


def _unpack_small(p):
    out = {k: p[i] for i, k in enumerate(SMALL_1024)}
    out["conv_dw_b"], out["conv_ln_g"] = p[4, :D_CONV], p[4, D_CONV:]
    out["conv_ln_b"], out["pool_scale"] = p[5, :D_CONV], p[5, D_CONV:]
    out["pool_w"] = p[6:70].reshape(4, POOL_GROUP, POOL_GROUP)
    return out


def _as_stored(name, a):
    if name.endswith(("w_gate", "w_up")):
        return a.T
    if name == "conv_dw":
        return _pad_rows(a, CONV_WIDTH + 1)
    return a


def _as_given(name, a):
    if name.endswith(("w_gate", "w_up")):
        return a.T
    if name == "conv_dw":
        return a[:CONV_WIDTH]
    return a


def kernel(x, ffn1_norm, ffn1_w_gate, ffn1_w_up, ffn1_w_down, mix_norm, w_in, conv_dw, conv_dw_b, conv_ln_g, conv_ln_b, conv_pw, pool_w, pool_scale, w_out, ffn2_norm, ffn2_w_gate, ffn2_w_up, ffn2_w_down, final_norm, loss_target, m_ffn1_norm, m_ffn1_w_gate, m_ffn1_w_up, m_ffn1_w_down, m_mix_norm, m_w_in, m_conv_dw, m_conv_dw_b, m_conv_ln_g, m_conv_ln_b, m_conv_pw, m_pool_w, m_pool_scale, m_w_out, m_ffn2_norm, m_ffn2_w_gate, m_ffn2_w_up, m_ffn2_w_down, m_final_norm, v_ffn1_norm, v_ffn1_w_gate, v_ffn1_w_up, v_ffn1_w_down, v_mix_norm, v_w_in, v_conv_dw, v_conv_dw_b, v_conv_ln_g, v_conv_ln_b, v_conv_pw, v_pool_w, v_pool_scale, v_w_out, v_ffn2_norm, v_ffn2_w_gate, v_ffn2_w_up, v_ffn2_w_down, v_final_norm):
    given = dict(locals())
    wts = {k: given[k] for k in WEIGHTS}
    mom_m = {k: given["m_" + k] for k in WEIGHTS}
    mom_v = {k: given["v_" + k] for k in WEIGHTS}
    xt, target = x[0], loss_target[0]

    shard = {k: _as_stored(k, wts[k]) if k == "conv_dw" else _as_stored(k, wts[k]).astype(BF16)
             for k in WEIGHTS if k.endswith(FFN_W) or k in MID}
    w = {k: wts[k].reshape(1, -1) for k in SMALL_1024 + SMALL_512}
    w["pool_w"] = wts["pool_w"].astype(BF16)

    (h1, s1, p1, a1, w["ffn1_w_gate"], w["ffn1_w_up"]), ((w["ffn1_w_down"],),) = _ffn_up_gather(
        xt, w["ffn1_norm"], shard["ffn1_w_gate"], shard["ffn1_w_up"], "ffn1_up_gather",
        cargos=[Cargo("gather_slots", [shard["ffn1_w_down"]])])
    x1, (mid, (w["ffn2_w_down"],)) = _ffn_down(
        xt, a1, w["ffn1_w_down"], "ffn1_down",
        cargos=[Cargo("gather_chips", [shard[k] for k in MID]), Cargo("gather_slots", [shard["ffn2_w_down"]])])
    w["w_in"] = mid[0]
    w["conv_dw"] = mid[1].transpose(1, 0, 2).reshape(CONV_WIDTH + 1, D_CONV)
    w["conv_pw"] = mid[2].reshape(D_CONV, D_CONV)
    w["w_out"] = mid[3].reshape(D_CONV + D_POOL, D_MODEL)
    (x2, h2, proj, u1, u3, mixed, cat), ((w["ffn2_w_gate"], w["ffn2_w_up"]),) = _mix_fwd(
        x1, w["mix_norm"], w["w_in"], w["conv_dw"], w["conv_dw_b"], w["conv_ln_g"], w["conv_ln_b"], w["conv_pw"],
        w["pool_w"], w["pool_scale"], w["w_out"], "mix_fwd",
        cargos=[Cargo("gather_slots", [shard["ffn2_w_gate"], shard["ffn2_w_up"]])])
    x3, h3, s2, p2, a2 = _ffn_fwd(x2, w["ffn2_norm"], w["ffn2_w_gate"], w["ffn2_w_up"], w["ffn2_w_down"], "ffn2_fwd")
    dx3, loss_share, d_final = _final_norm_loss(x3, target, w["final_norm"], "final_norm_loss")

    g = {"final_norm": d_final}
    sums = {}

    def landed(names, parts):
        sums.update(zip(names, parts))

    dx2, g["ffn2_norm"], df2, dg2, du2 = _ffn_bwd(dx3, x2, w["ffn2_norm"], s2, p2, w["ffn2_w_gate"],
                                                   w["ffn2_w_up"], w["ffn2_w_down"], "ffn2_bwd")
    def ffn_wgrad(names, hids, tok, kernel_name, cargos=()):
        parts, cargo_outs = _wgrad_hid_tok_scatter(hids, tok, kernel_name, cargos=cargos)
        landed(names, parts)
        return cargo_outs

    ffn_wgrad(["ffn2_w_gate", "ffn2_w_up"], [dg2, du2], h3, "ffn2_dw_gate_up")
    ffn_wgrad(["ffn2_w_down"], [a2], df2, "ffn2_dw_down")
    (dx1, dproj, dco, dpo, g_dw, g["conv_dw_b"], g["conv_ln_g"], g["conv_ln_b"], g["pool_scale"],
     g["mix_norm"]), (swapped2,) = _mix_bwd(
        dx2, u1, mixed, proj, x1, w["mix_norm"], w["conv_dw"], w["conv_ln_g"], w["conv_ln_b"], w["conv_pw"],
        w["pool_w"], w["pool_scale"], w["w_out"], w["w_in"], "mix_bwd",
        cargos=[Cargo("swap", [sums["ffn2_" + k] for k in FFN_W])])
    g_out, _ = _wgrad_2d(cat, dx2, 1, BF16, "dw_out")
    g_pw, _ = _wgrad_2d(u3, dco, 1, BF16, "dconv_pw")
    g["pool_w"], _ = _wgrad_2d(mixed, dpo, 4, F32, "dpool_w", group_diag=True)
    slabs = [g_pw.reshape(N_CHIPS, D_CONV // N_CHIPS, D_CONV),
             g_out.reshape(N_CHIPS, (D_CONV + D_POOL) // N_CHIPS, D_MODEL)]
    g_in, (parts,) = _wgrad_2d(h2, dproj, N_CHIPS, BF16, "dw_in", cargos=[Cargo("scatter_chips", slabs)])
    landed(["conv_pw", "w_out"], parts)
    dx, g["ffn1_norm"], df1, dg1, du1_ = _ffn_bwd(dx1, xt, w["ffn1_norm"], s1, p1, w["ffn1_w_gate"],
                                                   w["ffn1_w_up"], w["ffn1_w_down"], "ffn1_bwd")
    slabs = [g_in, g_dw.reshape(CONV_WIDTH + 1, N_CHIPS, D_CONV // N_CHIPS).transpose(1, 0, 2)]
    (parts,) = ffn_wgrad(["ffn1_w_gate", "ffn1_w_up"], [dg1, du1_], h1, "ffn1_dw_gate_up",
                         cargos=[Cargo("scatter_chips", slabs)])
    landed(["w_in", "conv_dw"], parts)
    swapped_mid, swapped_gate_up, small_parts = ffn_wgrad(
        ["ffn1_w_down"], [a1], df1, "ffn1_dw_down",
        cargos=[Cargo("swap", [sums[k] for k in MID]), Cargo("swap", [sums["ffn1_w_gate"], sums["ffn1_w_up"]]),
                Cargo("gather_devices", [_pack_small(g, spare=loss_share)])])
    swapped_down = _exchange(Cargo("swap", [sums["ffn1_w_down"]]), "swap_last")

    theirs = dict(zip(["ffn2_" + k for k in FFN_W], swapped2))
    theirs.update(zip(MID, swapped_mid))
    theirs.update(ffn1_w_gate=swapped_gate_up[0], ffn1_w_up=swapped_gate_up[1], ffn1_w_down=swapped_down[0])
    grads, deltas, new_m, new_v = {}, {}, {}, {}
    for k in theirs:
        res = _adamw([sums[k], theirs[k]], _as_stored(k, wts[k]), _as_stored(k, mom_m[k]),
                     _as_stored(k, mom_v[k]), "adamw_" + k)
        grads[k], deltas[k], new_m[k], new_v[k] = [_as_given(k, t) for t in res]
    res = _adamw(small_parts, _pack_small(wts), _pack_small(mom_m), _pack_small(mom_v), "adamw_small")
    for dst, packed in zip((grads, deltas, new_m, new_v), res):
        dst.update(_unpack_small(packed))
    loss = res[0][PACK_LOSS_ROW, 0]

    out = [loss, dx[None]]
    for group in (grads, deltas, new_m, new_v):
        out += [group[k] for k in WEIGHTS]
    return tuple(out)
```

```python
import functools

import jax
import jax.numpy as jnp
from jax import lax
from jax.experimental import pallas as pl
from jax.experimental.pallas import tpu as pltpu

F32 = jnp.float32
BF16 = jnp.bfloat16
MESH = pl.DeviceIdType.MESH

N_CHIPS = 4
N_DEV = 8
D_MODEL = 1024
D_CONV = 512
D_POOL = 512
CONV_WIDTH = 31
POOL_WINDOWS = (2, 4, 8, 16)
POOL_GROUP = 128
D_IN = 2 * D_CONV + D_POOL
HALO = 32
RMS_EPS = 1e-6
LN_EPS = 1e-5
FFN_RES_WEIGHT = 0.5
ADAM_LR = 0.001
ADAM_B1 = 0.9
ADAM_B2 = 0.999
ADAM_EPS = 1e-08
ADAM_WD = 0.01
ADAM_STEP = 10
VMEM_LIMIT_BYTES = 52 * 1024 * 1024
VMEM_LIMIT_BYTES_BWD = 58 * 1024 * 1024
TM_FFN = 512
TM_MIX = 256
TT_WGRAD = 2048
STRIP = 16
SLOTS_PER_STEP = 2
SUBLANES = 8
RELAY_AT_EIGHTHS = 7

HBM = pl.BlockSpec(memory_space=pl.ANY)


def _dot(a, b):
    return jnp.dot(a, b, preferred_element_type=F32)


def _dot_nt(a, b):
    return lax.dot_general(a, b, (((1,), (1,)), ((), ())), preferred_element_type=F32)


def _dot_tn(a, b):
    return lax.dot_general(a, b, (((0,), (0,)), ((), ())), preferred_element_type=F32)


def _sds(shape, dtype):
    return jax.ShapeDtypeStruct(shape, dtype)


def _rms_stats(xv):
    r = lax.rsqrt(jnp.mean(xv * xv, axis=-1, keepdims=True) + RMS_EPS)
    return r, xv * r


def _swiglu_saved(gate, up):
    sig = jax.nn.sigmoid(gate)
    silu = gate * sig
    return silu, up * (sig * (1.0 + gate * (1.0 - sig))), silu * up


def _rms_bwd(dh, n, r, gain):
    dn = dh * gain
    return r * (dn - n * jnp.mean(dn * n, axis=-1, keepdims=True))


def _place():
    x, y, c = lax.axis_index("x"), lax.axis_index("y"), lax.axis_index("c")
    return x, y, c, [(1 - x, y), (x, 1 - y), (1 - x, 1 - y)]


class Cargo:
    def __init__(self, kind, arrays):
        self.kind, self.arrays = kind, list(arrays)
        n = len(self.arrays)
        self.two_level = kind in ("gather_slots", "gather_chips")
        if self.two_level:
            self.out_shape = [_sds((N_CHIPS,) + a.shape, a.dtype) for a in self.arrays]
        elif kind == "gather_devices":
            self.out_shape = [_sds((N_DEV,) + a.shape, a.dtype) for a in self.arrays]
        else:
            self.out_shape = [_sds(a.shape, a.dtype) for a in self.arrays]
        n_remote = n * {"swap": 1, "gather_devices": N_DEV - 1}.get(kind, N_CHIPS - 1)
        n_own = 0 if kind == "swap" else n
        n_relay = n_remote if self.two_level else 0
        dma = pltpu.SemaphoreType.DMA
        self.scratch = [dma((n_remote,)), dma((n_remote,)), dma((max(n_own, 1),)),
                        dma((max(n_relay, 1),)), dma((max(n_relay, 1),))]

    def _plan(self, ins, outs):
        x, y, c, chips = _place()
        q = 2 * x + y
        sibling = (x, y, 1 - c)
        own, remote, relays = [], [], []
        for a, o in zip(ins, outs):
            if self.two_level:
                half = a.shape[0] // 2
                mine = pl.ds(pl.multiple_of(c * half, SUBLANES), half)
                theirs = pl.ds(pl.multiple_of((1 - c) * half, SUBLANES), half)
                own.append((a, o.at[0 if self.kind == "gather_slots" else q]))
                for j, (px, py) in enumerate(chips):
                    there, here = (j + 1, j + 1) if self.kind == "gather_slots" else (q, 2 * px + py)
                    remote.append((a.at[mine], o.at[there, mine], o.at[here, mine], (px, py, c)))
                    relays.append((o.at[here, mine], o.at[here, mine], o.at[here, theirs], sibling))
            elif self.kind == "scatter_chips":
                own.append((a.at[q], o.at[q]))
                remote += [(a.at[2 * px + py], o.at[q], o.at[2 * px + py], (px, py, c)) for px, py in chips]
            elif self.kind == "swap":
                remote.append((a, o, o, sibling))
            else:
                own.append((a, o.at[4 * x + 2 * y + c]))
                for k in range(1, N_DEV):
                    px, py, pc = x ^ (k >> 2 & 1), y ^ (k >> 1 & 1), c ^ (k & 1)
                    remote.append((a, o.at[4 * x + 2 * y + c], o.at[4 * px + 2 * py + pc], (px, py, pc)))
        return own, remote, relays

    @staticmethod
    def _copies(entries, send_sems, recv_sems):
        out = []
        for k, (src, dst, landed, peer) in enumerate(entries):
            def make(dst_ref, k=k, src=src, peer=peer):
                return pltpu.make_async_remote_copy(src_ref=src, dst_ref=dst_ref, send_sem=send_sems.at[k],
                                                    recv_sem=recv_sems.at[k], device_id=peer, device_id_type=MESH)
            out.append((make(dst), make(landed)))
        return out

    def start(self, ins, outs, sems):
        own, remote, _ = self._plan(ins, outs)
        for k, (src, dst) in enumerate(own):
            pltpu.make_async_copy(src, dst, sems[2].at[k]).start()
        for mine, _ in self._copies(remote, sems[0], sems[1]):
            mine.start()

    def relay(self, ins, outs, sems):
        _, remote, relays = self._plan(ins, outs)
        passed = self._copies(relays, sems[3], sems[4])
        for (_, arriving), (mine, _) in zip(self._copies(remote, sems[0], sems[1]), passed):
            arriving.wait_recv()
            mine.start()

    def wait(self, ins, outs, sems):
        own, remote, relays = self._plan(ins, outs)
        for mine, arriving in self._copies(remote, sems[0], sems[1]):
            mine.wait_send()
            if not self.two_level:
                arriving.wait_recv()
        for mine, arriving in self._copies(relays, sems[3], sems[4]):
            mine.wait_send()
            arriving.wait_recv()
        for k, (src, dst) in enumerate(own):
            pltpu.make_async_copy(src, dst, sems[2].at[k]).wait()


N_CARGO_SEMS = 5


def _call(body, *, name, grid, in_specs, out_specs, out_shape, args, scratch_shapes=(), cargos=(),
          vmem_limit_bytes=VMEM_LIMIT_BYTES):
    n_in, n_out, n_scr = len(in_specs), len(out_specs), len(scratch_shapes)
    c_in = [len(cg.arrays) for cg in cargos]
    n_cin = sum(c_in)

    def wrapped(*refs):
        ins = refs[:n_in]
        cins = refs[n_in:n_in + n_cin]
        outs = refs[n_in + n_cin:n_in + n_cin + n_out]
        couts = refs[n_in + n_cin + n_out:n_in + 2 * n_cin + n_out]
        scr = refs[n_in + 2 * n_cin + n_out:n_in + 2 * n_cin + n_out + n_scr]
        sems = refs[n_in + 2 * n_cin + n_out + n_scr:]
        step, n_steps = 0, 1
        for ax, size in enumerate(grid):
            step = step * size + pl.program_id(ax)
            n_steps *= size

        def each(method, only_two_level=False):
            at = 0
            for k, cg in enumerate(cargos):
                if cg.two_level or not only_two_level:
                    getattr(cg, method)(cins[at:at + c_in[k]], couts[at:at + c_in[k]],
                                        sems[N_CARGO_SEMS * k:N_CARGO_SEMS * (k + 1)])
                at += c_in[k]

        body(*ins, *outs, *scr)
        if cargos:
            pl.when(step == 0)(lambda: each("start"))
        if any(cg.two_level for cg in cargos):
            pl.when(step == (RELAY_AT_EIGHTHS * n_steps) // 8)(lambda: each("relay", only_two_level=True))
        if cargos:
            pl.when(step == n_steps - 1)(lambda: each("wait"))

    res = pl.pallas_call(
        wrapped, name=name, grid=grid,
        in_specs=list(in_specs) + [HBM] * n_cin,
        out_specs=list(out_specs) + [HBM] * n_cin,
        out_shape=list(out_shape) + [s for cg in cargos for s in cg.out_shape],
        scratch_shapes=list(scratch_shapes) + [s for cg in cargos for s in cg.scratch],
        compiler_params=pltpu.CompilerParams(dimension_semantics=("arbitrary",) * len(grid),
                                             vmem_limit_bytes=vmem_limit_bytes),
    )(*args, *[a for cg in cargos for a in cg.arrays])
    outs, rest = list(res[:n_out]), list(res[n_out:])
    cargo_outs = []
    for k in c_in:
        cargo_outs.append(rest[:k])
        rest = rest[k:]
    return outs, cargo_outs


def _exchange(cargo, name):
    _, (outs,) = _call(lambda: None, name=name, grid=(1,), in_specs=[], out_specs=[], out_shape=[], args=[],
                       cargos=[cargo])
    return outs


def _ffn_up_gather(x, gain, wg_t, wu_t, name, cargos=()):
    t_len, d = x.shape
    fq = wg_t.shape[0]
    tm = min(TM_FFN, t_len)
    n_tiles = t_len // tm
    relay_tile = n_tiles // 2
    fetch_tile = min(relay_tile + 1, n_tiles - 1)

    def body(x_ref, g_ref, wg_in, wu_in, h_ref, s_ref, p_ref, a_ref, wg_all, wu_all,
             wg_v, wu_v, h_all, send_sems, recv_sems, pass_send_sems, pass_recv_sems, own_sems, load_sems):
        s = pl.program_id(0)
        i = pl.program_id(1)
        x_, y_, c_, chips = _place()
        shards = ((wg_in, wg_all, wg_v), (wu_in, wu_all, wu_v))
        mine = pl.ds(pl.multiple_of(c_ * (fq // 2), SUBLANES), fq // 2)
        theirs = pl.ds(pl.multiple_of((1 - c_) * (fq // 2), SUBLANES), fq // 2)

        def to_peer(k, j):
            w_in, w_all, _ = shards[k]
            return pltpu.make_async_remote_copy(
                src_ref=w_in.at[mine], dst_ref=w_all.at[j + 1, mine], send_sem=send_sems.at[3 * k + j],
                recv_sem=recv_sems.at[3 * k + j], device_id=(*chips[j], c_), device_id_type=MESH)

        def to_sibling(k, j, landing=False):
            w_all = shards[k][1]
            return pltpu.make_async_remote_copy(
                src_ref=w_all.at[j + 1, mine], dst_ref=w_all.at[j + 1, theirs if landing else mine],
                send_sem=pass_send_sems.at[3 * k + j], recv_sem=pass_recv_sems.at[3 * k + j],
                device_id=(x_, y_, 1 - c_), device_id_type=MESH)

        def keep(k):
            return pltpu.make_async_copy(shards[k][0], shards[k][1].at[0], own_sems.at[k])

        @pl.when((s == 0) & (i == 0))
        def _():
            for j in range(N_CHIPS - 1):
                for k in range(2):
                    to_peer(k, j).start()
            for k in range(2):
                keep(k).start()

        def load(k, slot):
            src = shards[k][0] if slot == 0 else shards[k][1].at[slot]
            return pltpu.make_async_copy(src, shards[k][2].at[slot % 2], load_sems.at[k])

        @pl.when((s == 0) & (i == 0))
        def _():
            for k in range(2):
                load(k, 0).start()
            for k in range(2):
                load(k, 0).wait()

        def pass_on(slot):
            for k in range(2):
                to_peer(k, slot - 1).wait_recv()
                to_sibling(k, slot - 1).start()

        def fetch(slot):
            for k in range(2):
                to_sibling(k, slot - 1, landing=True).wait_recv()
                load(k, slot).start()

        for slot in range(1, N_CHIPS):
            pl.when((s == slot - 1) & (i == relay_tile))(functools.partial(pass_on, slot))
            pl.when((s == slot - 1) & (i == fetch_tile))(functools.partial(fetch, slot))

            @pl.when((s == slot) & (i == 0))
            def _():
                for k in range(2):
                    load(k, slot).wait()

        @pl.when(s == 0)
        def _():
            _, n = _rms_stats(x_ref[...])
            h_new = (n * g_ref[...]).astype(BF16)
            h_ref[...] = h_new
            h_all[i] = h_new

        h = h_all[i]
        silu, dgate, act = _swiglu_saved(_dot_nt(h, wg_v[s % 2]), _dot_nt(h, wu_v[s % 2]))
        s_ref[...] = silu.astype(BF16)
        p_ref[...] = dgate.astype(BF16)
        a_ref[...] = act.astype(BF16)

        @pl.when((s == N_CHIPS - 1) & (i == n_tiles - 1))
        def _():
            for k in range(2):
                for j in range(N_CHIPS - 1):
                    to_peer(k, j).wait_send()
                    to_sibling(k, j).wait_send()
                keep(k).wait()

    tok = pl.BlockSpec((tm, d), lambda s, i: (jnp.where(s == 0, i, n_tiles - 1), 0))
    hid = pl.BlockSpec((None, tm, fq), lambda s, i: (s, i, 0))
    outs, cargo_outs = _call(
        body, name=name, grid=(N_CHIPS, n_tiles),
        in_specs=[tok, pl.BlockSpec((1, d), lambda s, i: (0, 0)), HBM, HBM],
        out_specs=[tok, hid, hid, hid, HBM, HBM],
        out_shape=[_sds((t_len, d), BF16)] + [_sds((N_CHIPS, t_len, fq), BF16)] * 3
        + [_sds((N_CHIPS, fq, d), BF16)] * 2,
        scratch_shapes=[pltpu.VMEM((2, fq, d), BF16), pltpu.VMEM((2, fq, d), BF16),
                        pltpu.VMEM((n_tiles, tm, d), BF16)]
        + [pltpu.SemaphoreType.DMA((6,))] * 4 + [pltpu.SemaphoreType.DMA((2,))] * 2,
        args=[x, gain, wg_t, wu_t], cargos=cargos)
    return outs, cargo_outs


def _load_once(hbm_refs, vmem_refs, sems, first):
    @pl.when(first)
    def _():
        copies = [pltpu.make_async_copy(src, dst, sems.at[k]) for k, (src, dst) in enumerate(zip(hbm_refs, vmem_refs))]
        for cp in copies:
            cp.start()
        for cp in copies:
            cp.wait()


def _ffn_down(x, act, wd, name, cargos=()):
    t_len, d = x.shape
    nq, fq, _ = wd.shape
    tm = min(TM_FFN, t_len)

    def body(x_ref, a_ref, wd_ref, xo_ref):
        y = _dot(a_ref[0], wd_ref[0])
        for j in range(1, nq):
            y = y + _dot(a_ref[j], wd_ref[j])
        xo_ref[...] = x_ref[...] + FFN_RES_WEIGHT * y

    tok = pl.BlockSpec((tm, d), lambda i: (i, 0))
    (xo,), cargo_outs = _call(
        body, name=name, grid=(t_len // tm,),
        in_specs=[tok, pl.BlockSpec((nq, tm, fq), lambda i: (0, i, 0)), pl.BlockSpec((nq, fq, d), lambda i: (0, 0, 0))],
        out_specs=[tok], out_shape=[_sds((t_len, d), F32)], args=[x, act, wd], cargos=cargos)
    return xo, cargo_outs


def _ffn_fwd(x, gain, wg_t, wu_t, wd, name):
    t_len, d = x.shape
    nq, fq, _ = wd.shape
    tm = min(TM_FFN, t_len)

    def body(x_ref, g_ref, wg_hbm, wu_hbm, wd_hbm, xo_ref, h_ref, s_ref, p_ref, a_ref,
             h_s, acc, wg_v, wu_v, wd_v, load_sems):
        i = pl.program_id(0)
        j = pl.program_id(1)
        _load_once((wg_hbm, wu_hbm, wd_hbm), (wg_v, wu_v, wd_v), load_sems, (i == 0) & (j == 0))

        @pl.when(j == 0)
        def _():
            _, n = _rms_stats(x_ref[...])
            h = (n * g_ref[...]).astype(BF16)
            h_s[...] = h
            h_ref[...] = h
            acc[...] = jnp.zeros_like(acc)

        h = h_s[...]
        y = None
        for jj in range(SLOTS_PER_STEP):
            slot = j * SLOTS_PER_STEP + jj
            silu, dgate, act = _swiglu_saved(_dot_nt(h, wg_v[slot]), _dot_nt(h, wu_v[slot]))
            s_ref[jj] = silu.astype(BF16)
            p_ref[jj] = dgate.astype(BF16)
            a_ref[jj] = act.astype(BF16)
            part = _dot(a_ref[jj], wd_v[slot])
            y = part if y is None else y + part
        acc[...] += y

        @pl.when(j == nq // SLOTS_PER_STEP - 1)
        def _():
            xo_ref[...] = x_ref[...] + FFN_RES_WEIGHT * acc[...]

    tok = pl.BlockSpec((tm, d), lambda i, j: (i, 0))
    hid = pl.BlockSpec((SLOTS_PER_STEP, tm, fq), lambda i, j: (j, i, 0))
    outs, _ = _call(
        body, name=name, grid=(t_len // tm, nq // SLOTS_PER_STEP),
        in_specs=[tok, pl.BlockSpec((1, d), lambda i, j: (0, 0)), HBM, HBM, HBM],
        out_specs=[tok, tok, hid, hid, hid],
        out_shape=[_sds((t_len, d), F32), _sds((t_len, d), BF16)] + [_sds((nq, t_len, fq), BF16)] * 3,
        scratch_shapes=[pltpu.VMEM((tm, d), BF16), pltpu.VMEM((tm, d), F32)]
        + [pltpu.VMEM((nq, fq, d), BF16)] * 3 + [pltpu.SemaphoreType.DMA((3,))],
        args=[x, gain, wg_t, wu_t, wd])
    return outs


def _ffn_bwd(dy, x_in, gain, silu, dgate_du, wg_t, wu_t, wd, name):
    t_len, d = dy.shape
    nq, fq, _ = wd.shape
    tm = min(TM_FFN, t_len)

    def body(dy_ref, x_ref, g_ref, s_ref, p_ref, wg_hbm, wu_hbm, wd_hbm,
             dx_ref, dgain_ref, df_ref, dg_ref, du_ref, df_s, dh_acc, dact_s, wg_v, wu_v, wd_v, load_sems):
        i = pl.program_id(0)
        j = pl.program_id(1)
        _load_once((wg_hbm, wu_hbm, wd_hbm), (wg_v, wu_v, wd_v), load_sems, (i == 0) & (j == 0))

        @pl.when((i == 0) & (j == 0))
        def _():
            dgain_ref[...] = jnp.zeros_like(dgain_ref)

        @pl.when(j == 0)
        def _():
            df = (FFN_RES_WEIGHT * dy_ref[...]).astype(BF16)
            df_s[...] = df
            df_ref[...] = df
            dh_acc[...] = jnp.zeros_like(dh_acc)

        slots = [j * SLOTS_PER_STEP + jj for jj in range(SLOTS_PER_STEP)]
        for jj, slot in enumerate(slots):
            dact_s[jj] = _dot_nt(df_s[...], wd_v[slot])

        for jj in range(SLOTS_PER_STEP):
            for r0 in range(0, tm, STRIP):
                rows = slice(r0, r0 + STRIP)
                dact = dact_s[jj, rows, :]
                dg_ref[jj, rows, :] = (dact * p_ref[jj, rows, :].astype(F32)).astype(BF16)
                du_ref[jj, rows, :] = (dact * s_ref[jj, rows, :].astype(F32)).astype(BF16)

        dh = None
        for jj, slot in enumerate(slots):
            part = _dot(dg_ref[jj], wg_v[slot]) + _dot(du_ref[jj], wu_v[slot])
            dh = part if dh is None else dh + part
        dh_acc[...] += dh

        @pl.when(j == nq // SLOTS_PER_STEP - 1)
        def _():
            r, n = _rms_stats(x_ref[...])
            dh = dh_acc[...]
            dgain_ref[...] += jnp.sum(dh * n, axis=0, keepdims=True)
            dx_ref[...] = dy_ref[...] + _rms_bwd(dh, n, r, g_ref[...])

    tok = pl.BlockSpec((tm, d), lambda i, j: (i, 0))
    vec = pl.BlockSpec((1, d), lambda i, j: (0, 0))
    hid = pl.BlockSpec((SLOTS_PER_STEP, tm, fq), lambda i, j: (j, i, 0))
    outs, _ = _call(
        body, name=name, grid=(t_len // tm, nq // SLOTS_PER_STEP),
        in_specs=[tok, tok, vec, hid, hid, HBM, HBM, HBM],
        out_specs=[tok, vec, tok, hid, hid],
        out_shape=[_sds((t_len, d), F32), _sds((1, d), F32), _sds((t_len, d), BF16),
                   _sds((nq, t_len, fq), BF16), _sds((nq, t_len, fq), BF16)],
        scratch_shapes=[pltpu.VMEM((tm, d), BF16), pltpu.VMEM((tm, d), F32),
                        pltpu.VMEM((SLOTS_PER_STEP, tm, fq), F32)]
        + [pltpu.VMEM((nq, fq, d), BF16)] * 3 + [pltpu.SemaphoreType.DMA((3,))],
        args=[dy, x_in, gain, silu, dgate_du, wg_t, wu_t, wd], vmem_limit_bytes=VMEM_LIMIT_BYTES_BWD)
    return outs


def _wgrad(lhs, rhs, l_spec, r_spec, out_shape, out_spec, acc_shape, grid, name, cargos=()):
    n_t = grid[-1]
    t_axis = len(grid) - 1

    def body(l_ref, r_ref, o_ref, acc):
        t = pl.program_id(t_axis)

        @pl.when(t == 0)
        def _():
            acc[...] = jnp.zeros_like(acc)

        acc[...] += _dot_tn(l_ref[...].astype(BF16), r_ref[...].astype(BF16))

        @pl.when(t == n_t - 1)
        def _():
            o_ref[...] = acc[...].astype(o_ref.dtype)

    (out,), cargo_outs = _call(
        body, name=name, grid=grid, in_specs=[l_spec, r_spec], out_specs=[out_spec], out_shape=[out_shape],
        scratch_shapes=[pltpu.VMEM(acc_shape, F32)], args=[lhs, rhs], cargos=cargos)
    return out, cargo_outs


def _wgrad_hid_tok_scatter(hids, tok, name, cargos=()):
    t_len, d = tok.shape
    n_w = len(hids)
    nq, _, fq = hids[0].shape
    half = fq // 2
    tt = min(TT_WGRAD, t_len)
    n_t = t_len // tt
    per_w = 4
    n_sem = 6

    def body(*refs):
        l_refs, r_ref, parts_refs = refs[:n_w], refs[n_w], refs[n_w + 1:2 * n_w + 1]
        scr = refs[2 * n_w + 1:]
        bufs = [scr[per_w * w:per_w * (w + 1)] for w in range(n_w)]
        zeros = scr[per_w * n_w]
        sems = [scr[per_w * n_w + 1 + n_sem * w:per_w * n_w + 1 + n_sem * (w + 1)] for w in range(n_w)]
        g = pl.program_id(0)
        t = pl.program_id(1)
        x_, y_, c_, chips = _place()
        mine = pl.ds(pl.multiple_of(c_ * half, STRIP), half)
        theirs = pl.ds(pl.multiple_of((1 - c_) * half, STRIP), half)

        def to_sibling(w, slot):
            return pltpu.make_async_remote_copy(
                src_ref=bufs[w][1].at[theirs], dst_ref=bufs[w][2].at[slot], send_sem=sems[w][0].at[slot],
                recv_sem=sems[w][1].at[slot], device_id=(x_, y_, 1 - c_), device_id_type=MESH)

        def to_peer(w, j):
            return pltpu.make_async_remote_copy(
                src_ref=bufs[w][3].at[j + 1], dst_ref=parts_refs[w].at[j + 1, mine], send_sem=sems[w][2].at[j],
                recv_sem=sems[w][3].at[j], device_id=(*chips[j], c_), device_id_type=MESH)

        def keep(w):
            return pltpu.make_async_copy(bufs[w][3].at[0], parts_refs[w].at[0, mine], sems[w][4])

        def blank(w, slot):
            return pltpu.make_async_copy(zeros, parts_refs[w].at[slot, theirs], sems[w][5].at[slot])

        @pl.when((g == 0) & (t == 0))
        def _():
            zeros[...] = jnp.zeros_like(zeros)
            for w in range(n_w):
                for slot in range(nq):
                    blank(w, slot).start()

        @pl.when(t == 0)
        def _():
            for w in range(n_w):
                bufs[w][0][...] = jnp.zeros_like(bufs[w][0])

        rhs = r_ref[...]
        for w in range(n_w):
            bufs[w][0][...] += _dot_tn(l_refs[w][...], rhs)

        for step in range(nq):
            slot = (step + 1) % nq

            @pl.when((g == step) & (t == n_t - 1))
            def _():
                for w in range(n_w):
                    acc, stage, _, _ = bufs[w]
                    if step > 0:
                        to_sibling(w, step).wait_send()
                    stage[...] = acc[...].astype(BF16)
                    to_sibling(w, slot).start()
                for w in range(n_w):
                    _, stage, pair, summed = bufs[w]
                    to_sibling(w, slot).wait_recv()
                    summed[slot] = (stage[mine, :].astype(F32) + pair[slot].astype(F32)).astype(BF16)
                    if slot > 0:
                        to_peer(w, slot - 1).start()
                    else:
                        keep(w).start()

        @pl.when((g == nq - 1) & (t == n_t - 1))
        def _():
            for w in range(n_w):
                for j in range(N_CHIPS - 1):
                    to_peer(w, j).wait()
                keep(w).wait()
                to_sibling(w, 0).wait_send()
                for slot in range(nq):
                    blank(w, slot).wait()

    dma = pltpu.SemaphoreType.DMA
    scratch = []
    for _ in range(n_w):
        scratch += [pltpu.VMEM((fq, d), F32), pltpu.VMEM((fq, d), BF16), pltpu.VMEM((nq, half, d), BF16),
                    pltpu.VMEM((nq, half, d), BF16)]
    scratch.append(pltpu.VMEM((half, d), BF16))
    for _ in range(n_w):
        scratch += [dma((nq,)), dma((nq,)), dma((N_CHIPS - 1,)), dma((N_CHIPS - 1,)), dma(()), dma((nq,))]
    parts, cargo_outs = _call(
        body, name=name, grid=(nq, n_t),
        in_specs=[pl.BlockSpec((None, tt, fq), lambda g, t: ((g + 1) % nq, t, 0))] * n_w
        + [pl.BlockSpec((tt, d), lambda g, t: (t, 0))],
        out_specs=[HBM] * n_w, out_shape=[_sds((nq, fq, d), BF16)] * n_w,
        scratch_shapes=scratch, args=[*hids, tok], cargos=cargos)
    return parts, cargo_outs


def _wgrad_2d(lhs, rhs, n_col_blocks, out_dtype, name, group_diag=False, cargos=()):
    t_len, k = lhs.shape
    n = rhs.shape[1]
    nb = n // n_col_blocks
    kb = k // n_col_blocks if group_diag else k
    tt = min(TT_WGRAD, t_len)
    l_map = (lambda q, t: (t, q)) if group_diag else (lambda q, t: (t, 0))
    return _wgrad(lhs, rhs,
                  pl.BlockSpec((tt, kb), l_map),
                  pl.BlockSpec((tt, nb), lambda q, t: (t, q)),
                  _sds((n_col_blocks, kb, nb), out_dtype),
                  pl.BlockSpec((None, kb, nb), lambda q, t: (q, 0, 0)),
                  (kb, nb), (n_col_blocks, t_len // tt), name, cargos)


def _layernorm_stats(u1):
    mu = jnp.mean(u1, axis=-1, keepdims=True)
    xc = u1 - mu
    rstd = lax.rsqrt(jnp.mean(xc * xc, axis=-1, keepdims=True) + LN_EPS)
    return rstd, xc * rstd


def _positions(i, tm, rows, offset=0):
    return (lax.broadcasted_iota(jnp.int32, (rows, 1), 0) + (i * tm + offset)).astype(F32)


SHIFT_ROWS = HALO - SUBLANES


def _fill_shifted(ext_s, sh_s, tm):
    for b in range(1, SUBLANES):
        sh_s[b - 1] = ext_s[pl.ds(b, tm + SHIFT_ROWS), :]


def _window(ext_s, sh_s, shift, tm):
    a, b = divmod(shift, SUBLANES)
    if b == 0:
        return ext_s[pl.ds(shift, tm), :]
    return sh_s[b - 1, pl.ds(a * SUBLANES, tm), :]


def _window_sums(ext_s, lv_a, lv_b, tm, ahead):
    g = POOL_GROUP
    sign = 1 if ahead else -1
    for n, (dst, src, c0) in enumerate(((lv_a, ext_s, 0), (lv_b, lv_a, g), (lv_a, lv_b, 2 * g)), start=1):
        lo = 0 if ahead else n * SUBLANES
        rows = tm + HALO - n * SUBLANES
        shift = sign * 2 ** (n - 1)
        dst[pl.ds(lo, rows), c0:] = src[pl.ds(lo, rows), c0:] + src[pl.ds(lo + shift, rows), c0:]
    base = 0 if ahead else HALO
    rows = pl.ds(base, tm)
    far = pl.ds(base + sign * SUBLANES, tm)
    return [lv_a[rows, 0:g], lv_b[rows, g:2 * g], lv_a[rows, 2 * g:3 * g],
            lv_a[rows, 3 * g:] + lv_a[far, 3 * g:]]


def _tile(tm, cols):
    return pl.BlockSpec((tm, cols), lambda i: (i, 0))


def _whole(shape):
    return pl.BlockSpec(shape, lambda i: (0,) * len(shape))


def _mix_fwd(x1, gain, w_in, conv_dw, conv_b, ln_g, ln_b, conv_pw, pool_w, pool_scale, w_out, name, cargos=()):
    t_len, d = x1.shape
    nq, _, nb = w_in.shape
    tm = min(TM_MIX, t_len)

    def body(x_ref, g_ref, wi_ref, dw_ref, cb_ref, lg_ref, lb_ref, pw_ref, plw_ref, ps_ref, wo_ref,
             x2_ref, h_ref, p_ref, u1_ref, u3_ref, mx_ref, cat_ref, ext_s, pext_s, sh_s, tail_s, lva_s, lvb_s):
        i = pl.program_id(0)

        @pl.when(i == 0)
        def _():
            tail_s[...] = jnp.zeros_like(tail_s)

        _, n = _rms_stats(x_ref[...])
        h = (n * g_ref[...]).astype(BF16)
        h_ref[...] = h
        for q in range(nq):
            p_ref[:, q * nb:(q + 1) * nb] = _dot(h, wi_ref[q])

        a = p_ref[:, 0:D_CONV]
        g = p_ref[:, D_CONV:2 * D_CONV]
        p = p_ref[:, 2 * D_CONV:]
        ext_s[0:HALO, :] = tail_s[:, 0:D_CONV] * jax.nn.sigmoid(tail_s[:, D_CONV:2 * D_CONV])
        ext_s[HALO:, :] = a * jax.nn.sigmoid(g)
        pext_s[0:HALO, :] = tail_s[:, 2 * D_CONV:]
        pext_s[HALO:, :] = p
        tail_s[...] = p_ref[tm - HALO:tm, :]

        _fill_shifted(ext_s, sh_s, tm)
        u1 = jnp.broadcast_to(cb_ref[...], (tm, D_CONV))
        for k in range(CONV_WIDTH):
            u1 = u1 + dw_ref[k:k + 1, :] * _window(ext_s, sh_s, HALO - (CONV_WIDTH - 1) + k, tm)
        u1_ref[...] = u1
        _, nhat = _layernorm_stats(u1)
        u2 = nhat * lg_ref[...] + lb_ref[...]
        u3 = (u2 * jax.nn.sigmoid(u2)).astype(BF16)
        u3_ref[...] = u3
        cat_ref[:, 0:D_CONV] = _dot(u3, pw_ref[...]).astype(BF16)

        pos1 = _positions(i, tm, tm) + 1.0
        sums = _window_sums(pext_s, lva_s, lvb_s, tm, ahead=False)
        for gi, w in enumerate(POOL_WINDOWS):
            cols = slice(gi * POOL_GROUP, (gi + 1) * POOL_GROUP)
            mixed = (sums[gi] / jnp.minimum(pos1, float(w)) - p[:, cols]).astype(BF16)
            mx_ref[:, cols] = mixed
            out = _dot(mixed, plw_ref[gi]) * ps_ref[:, cols]
            cat_ref[:, D_CONV + gi * POOL_GROUP:D_CONV + (gi + 1) * POOL_GROUP] = out.astype(BF16)

        x2_ref[...] = x_ref[...] + _dot(cat_ref[...], wo_ref[...])

    return _call(
        body, name=name, grid=(t_len // tm,),
        in_specs=[_tile(tm, d), _whole((1, d)), _whole((nq, d, nb)), _whole((CONV_WIDTH + 1, D_CONV)),
                  _whole((1, D_CONV)), _whole((1, D_CONV)), _whole((1, D_CONV)), _whole((D_CONV, D_CONV)),
                  _whole((4, POOL_GROUP, POOL_GROUP)), _whole((1, D_POOL)), _whole((D_CONV + D_POOL, d))],
        out_specs=[_tile(tm, d), _tile(tm, d), _tile(tm, D_IN), _tile(tm, D_CONV), _tile(tm, D_CONV),
                   _tile(tm, D_POOL), _tile(tm, D_CONV + D_POOL)],
        out_shape=[_sds((t_len, d), F32), _sds((t_len, d), BF16), _sds((t_len, D_IN), F32),
                   _sds((t_len, D_CONV), F32), _sds((t_len, D_CONV), BF16), _sds((t_len, D_POOL), BF16),
                   _sds((t_len, D_CONV + D_POOL), BF16)],
        scratch_shapes=[pltpu.VMEM((tm + HALO, D_CONV), F32), pltpu.VMEM((tm + HALO, D_POOL), F32),
                        pltpu.VMEM((SUBLANES - 1, tm + SHIFT_ROWS, D_CONV), F32), pltpu.VMEM((HALO, D_IN), F32)]
        + [pltpu.VMEM((tm + HALO, D_POOL), F32)] * 2,
        args=[x1, gain, w_in, conv_dw, conv_b, ln_g, ln_b, conv_pw, pool_w, pool_scale, w_out], cargos=cargos)


def _mix_bwd(dx2, u1, mixed, proj, x1, gain, conv_dw, ln_g, ln_b, conv_pw, pool_w, pool_scale, w_out, w_in,
             name, cargos=()):
    t_len, d = x1.shape
    nq, _, nb = w_in.shape
    tm = min(TM_MIX, t_len)
    hb = tm // HALO
    n_tiles = t_len // tm

    def body(dxn_ref, u1_ref, mx_ref, p_ref, tail_ref, x_ref, dx2_ref, g_ref, dw_ref, lg_ref, lb_ref, pw_ref,
             plw_ref, ps_ref, wo_ref, wi_ref,
             dx1_ref, dp_ref, dco_ref, dpo_ref, ddw_ref, dcb_ref, dlg_ref, dlb_ref, dps_ref, dgain_ref,
             du_s, dm_s, uext_s, dext_s, mext_s, ush_s, dsh_s, lva_s, lvb_s):
        k = pl.program_id(0)

        @pl.when(k == 0)
        def _():
            for ref in (ddw_ref, dcb_ref, dlg_ref, dlb_ref, dps_ref, dgain_ref, du_s, dm_s):
                ref[...] = jnp.zeros_like(ref)

        counts = jnp.where(k < n_tiles, 1.0, 0.0)
        dcat = _dot_nt(dxn_ref[...].astype(BF16), wo_ref[...])
        dco = dcat[:, 0:D_CONV].astype(BF16)
        dco_ref[...] = dco
        du3 = _dot_nt(dco, pw_ref[...])
        rstd, nhat = _layernorm_stats(u1_ref[...])
        u2 = nhat * lg_ref[...] + lb_ref[...]
        sig = jax.nn.sigmoid(u2)
        du2 = du3 * (sig * (1.0 + u2 * (1.0 - sig)))
        dlg_ref[...] += counts * jnp.sum(du2 * nhat, axis=0, keepdims=True)
        dlb_ref[...] += counts * jnp.sum(du2, axis=0, keepdims=True)
        dnhat = du2 * lg_ref[...]
        du_s[k % 2] = rstd * (dnhat - jnp.mean(dnhat, axis=-1, keepdims=True)
                              - nhat * jnp.mean(dnhat * nhat, axis=-1, keepdims=True))
        for gi in range(len(POOL_WINDOWS)):
            cols = slice(gi * POOL_GROUP, (gi + 1) * POOL_GROUP)
            dpo = dcat[:, D_CONV + gi * POOL_GROUP:D_CONV + (gi + 1) * POOL_GROUP]
            pre = _dot(mx_ref[:, cols], plw_ref[gi])
            dps_ref[:, cols] += counts * jnp.sum(dpo * pre, axis=0, keepdims=True)
            dout = (dpo * ps_ref[:, cols]).astype(BF16)
            dpo_ref[:, cols] = dout
            dm_s[k % 2, :, cols] = _dot_nt(dout, plw_ref[gi])

        i = jnp.maximum(k - 1, 0)
        cur, nxt = (k + 1) % 2, k % 2
        first = k <= 1
        last = (k == n_tiles) | (k == 0)
        a = p_ref[:, 0:D_CONV]
        g = p_ref[:, D_CONV:2 * D_CONV]
        sg = jax.nn.sigmoid(g)
        ta = tail_ref[:, 0:D_CONV]
        tg = tail_ref[:, D_CONV:2 * D_CONV]
        uext_s[0:HALO, :] = jnp.where(first, 0.0, ta * jax.nn.sigmoid(tg))
        uext_s[HALO:, :] = a * sg
        du1 = du_s[cur]
        dext_s[0:tm, :] = du1
        dext_s[tm:, :] = jnp.where(last, 0.0, du_s[nxt, 0:HALO, :])

        _fill_shifted(uext_s, ush_s, tm)
        _fill_shifted(dext_s, dsh_s, tm)
        du0 = jnp.zeros((tm, D_CONV), F32)
        for tap in range(CONV_WIDTH):
            du0 = du0 + dw_ref[tap:tap + 1, :] * _window(dext_s, dsh_s, CONV_WIDTH - 1 - tap, tm)
            ddw_ref[tap:tap + 1, :] += jnp.sum(
                du1 * _window(uext_s, ush_s, HALO - (CONV_WIDTH - 1) + tap, tm), axis=0, keepdims=True)
        dcb_ref[...] += jnp.sum(du1, axis=0, keepdims=True)
        dp_ref[:, 0:D_CONV] = (du0 * sg).astype(BF16)
        dp_ref[:, D_CONV:2 * D_CONV] = (du0 * a * sg * (1.0 - sg)).astype(BF16)

        pos1 = _positions(i, tm, tm) + 1.0
        pos1_next = _positions(i, tm, HALO, offset=tm) + 1.0
        for gi, w in enumerate(POOL_WINDOWS):
            cols = slice(gi * POOL_GROUP, (gi + 1) * POOL_GROUP)
            dm = dm_s[cur, :, cols]
            mext_s[0:tm, cols] = dm / jnp.minimum(pos1, float(w))
            mext_s[tm:, cols] = jnp.where(last, 0.0, dm_s[nxt, 0:HALO, cols] / jnp.minimum(pos1_next, float(w)))
        sums = _window_sums(mext_s, lva_s, lvb_s, tm, ahead=True)
        for gi in range(len(POOL_WINDOWS)):
            cols = slice(gi * POOL_GROUP, (gi + 1) * POOL_GROUP)
            dp_ref[:, 2 * D_CONV + gi * POOL_GROUP:2 * D_CONV + (gi + 1) * POOL_GROUP] = (
                sums[gi] - dm_s[cur, :, cols]).astype(BF16)

        dh = _dot_nt(dp_ref[:, 0:nb], wi_ref[0])
        for q in range(1, nq):
            dh = dh + _dot_nt(dp_ref[:, q * nb:(q + 1) * nb], wi_ref[q])
        r, n = _rms_stats(x_ref[...])
        dgain_ref[...] += jnp.sum(dh * n, axis=0, keepdims=True)
        dx1_ref[...] = dx2_ref[...] + _rms_bwd(dh, n, r, g_ref[...])

    def ahead(cols):
        return pl.BlockSpec((tm, cols), lambda k: (jnp.minimum(k, n_tiles - 1), 0))

    def behind(cols):
        return pl.BlockSpec((tm, cols), lambda k: (jnp.maximum(k - 1, 0), 0))

    vec = _whole((1, D_CONV))
    return _call(
        body, name=name, grid=(n_tiles + 1,),
        in_specs=[ahead(d), ahead(D_CONV), ahead(D_POOL), behind(D_IN),
                  pl.BlockSpec((HALO, D_IN), lambda k: (jnp.maximum(jnp.maximum(k - 1, 0) * hb - 1, 0), 0)),
                  behind(d), behind(d), _whole((1, d)), _whole((CONV_WIDTH + 1, D_CONV)), vec, vec,
                  _whole((D_CONV, D_CONV)), _whole((4, POOL_GROUP, POOL_GROUP)), vec,
                  _whole((D_CONV + D_POOL, d)), _whole((nq, d, nb))],
        out_specs=[behind(d), behind(D_IN), ahead(D_CONV), ahead(D_POOL), _whole((CONV_WIDTH + 1, D_CONV)), vec,
                   vec, vec, vec, _whole((1, d))],
        out_shape=[_sds((t_len, d), F32), _sds((t_len, D_IN), BF16), _sds((t_len, D_CONV), BF16),
                   _sds((t_len, D_POOL), BF16), _sds((CONV_WIDTH + 1, D_CONV), F32), _sds((1, D_CONV), F32),
                   _sds((1, D_CONV), F32), _sds((1, D_CONV), F32), _sds((1, D_POOL), F32), _sds((1, d), F32)],
        scratch_shapes=[pltpu.VMEM((2, tm, D_CONV), F32), pltpu.VMEM((2, tm, D_POOL), F32),
                        pltpu.VMEM((tm + HALO, D_CONV), F32), pltpu.VMEM((tm + HALO, D_CONV), F32),
                        pltpu.VMEM((tm + HALO, D_POOL), F32),
                        pltpu.VMEM((SUBLANES - 1, tm + SHIFT_ROWS, D_CONV), F32),
                        pltpu.VMEM((SUBLANES - 1, tm + SHIFT_ROWS, D_CONV), F32)]
        + [pltpu.VMEM((tm + HALO, D_POOL), F32)] * 2,
        args=[dx2, u1, mixed, proj, proj, x1, dx2, gain, conv_dw, ln_g, ln_b, conv_pw, pool_w, pool_scale, w_out,
              w_in], cargos=cargos)


def _final_norm_loss(x3, target, gain, name):
    t_len, d = x3.shape
    tm = min(2 * TM_FFN, t_len)

    def body(x_ref, t_ref, g_ref, dx_ref, loss_ref, dgain_ref):
        @pl.when(pl.program_id(0) == 0)
        def _():
            loss_ref[...] = jnp.zeros_like(loss_ref)
            dgain_ref[...] = jnp.zeros_like(dgain_ref)

        r, n = _rms_stats(x_ref[...])
        err = n * g_ref[...] - t_ref[...]
        per_tok = jnp.sum(err * err, axis=-1, keepdims=True) * (1.0 / d)
        loss_ref[...] += 0.5 * jnp.sum(per_tok, axis=0, keepdims=True)
        dy = err * (1.0 / d)
        dgain_ref[...] += jnp.sum(dy * n, axis=0, keepdims=True)
        dx_ref[...] = _rms_bwd(dy, n, r, g_ref[...])

    tok = pl.BlockSpec((tm, d), lambda i: (i, 0))
    outs, _ = _call(
        body, name=name, grid=(t_len // tm,),
        in_specs=[tok, tok, pl.BlockSpec((1, d), lambda i: (0, 0))],
        out_specs=[tok, pl.BlockSpec((1, 128), lambda i: (0, 0)), pl.BlockSpec((1, d), lambda i: (0, 0))],
        out_shape=[_sds((t_len, d), F32), _sds((1, 128), F32), _sds((1, d), F32)],
        args=[x3, target, gain])
    return outs


def _row_tile(rows):
    return rows // 4 if rows % 64 == 0 else rows


def _adamw_math(w, g, m, v):
    m = ADAM_B1 * m + (1.0 - ADAM_B1) * g
    v = ADAM_B2 * v + (1.0 - ADAM_B2) * (g * g)
    m_hat = m / (1.0 - ADAM_B1 ** ADAM_STEP)
    v_hat = v / (1.0 - ADAM_B2 ** ADAM_STEP)
    delta = -ADAM_LR * (m_hat / (jnp.sqrt(v_hat) + ADAM_EPS) + ADAM_WD * w)
    return delta, m, v


def _adamw(parts, w, m, v, name):
    r, c = w.shape
    n = len(parts)
    tr = _row_tile(r)

    def body(*refs):
        g = None
        for p_ref in refs[:n]:
            s = p_ref[0].astype(F32)
            for k in range(1, p_ref.shape[0]):
                s = s + p_ref[k].astype(F32)
            g = s if g is None else g + s
        w_ref, m_ref, v_ref, g_out, d_out, m_out, v_out = refs[n:]
        delta, nm, nv = _adamw_math(w_ref[...], g, m_ref[...], v_ref[...])
        g_out[...] = g
        d_out[...] = delta
        m_out[...] = nm
        v_out[...] = nv

    blk = pl.BlockSpec((tr, c), lambda i: (i, 0))
    p_specs = [pl.BlockSpec((p.shape[0], tr, c), lambda i: (0, i, 0)) for p in parts]
    outs, _ = _call(body, name=name, grid=(r // tr,), in_specs=p_specs + [blk, blk, blk],
                    out_specs=[blk] * 4, out_shape=[_sds((r, c), F32)] * 4, args=[*parts, w, m, v])
    return outs


FFN_W = ("w_gate", "w_up", "w_down")
MID = ("w_in", "conv_dw", "conv_pw", "w_out")
SMALL_1024 = ("ffn1_norm", "mix_norm", "ffn2_norm", "final_norm")
SMALL_512 = ("conv_dw_b", "conv_ln_g", "conv_ln_b", "pool_scale")
WEIGHTS = ("ffn1_norm", "ffn1_w_gate", "ffn1_w_up", "ffn1_w_down", "mix_norm", "w_in", "conv_dw", "conv_dw_b",
           "conv_ln_g", "conv_ln_b", "conv_pw", "pool_w", "pool_scale", "w_out", "ffn2_norm", "ffn2_w_gate",
           "ffn2_w_up", "ffn2_w_down", "final_norm")
PACK_ROWS = 72
PACK_LOSS_ROW = 70


def _pad_rows(a, rows):
    return jnp.pad(a, ((0, rows - a.shape[0]), (0, 0)))


def _pack_small(t, spare=None):
    rows = [t[k].reshape(1, D_MODEL) for k in SMALL_1024]
    rows.append(jnp.concatenate([t["conv_dw_b"].reshape(1, -1), t["conv_ln_g"].reshape(1, -1)], axis=1))
    rows.append(jnp.concatenate([t["conv_ln_b"].reshape(1, -1), t["pool_scale"].reshape(1, -1)], axis=1))
    rows.append(t["pool_w"].reshape(64, D_MODEL))
    if spare is not None:
        rows.append(jnp.pad(spare, ((0, 0), (0, D_MODEL - spare.shape[1]))))
    return _pad_rows(jnp.concatenate(rows, axis=0), PACK_ROWS)


def _unpack_small(p):
    out = {k: p[i] for i, k in enumerate(SMALL_1024)}
    out["conv_dw_b"], out["conv_ln_g"] = p[4, :D_CONV], p[4, D_CONV:]
    out["conv_ln_b"], out["pool_scale"] = p[5, :D_CONV], p[5, D_CONV:]
    out["pool_w"] = p[6:70].reshape(4, POOL_GROUP, POOL_GROUP)
    return out


def _as_stored(name, a):
    if name.endswith(("w_gate", "w_up")):
        return a.T
    if name == "conv_dw":
        return _pad_rows(a, CONV_WIDTH + 1)
    return a


def _as_given(name, a):
    if name.endswith(("w_gate", "w_up")):
        return a.T
    if name == "conv_dw":
        return a[:CONV_WIDTH]
    return a


def kernel(x, ffn1_norm, ffn1_w_gate, ffn1_w_up, ffn1_w_down, mix_norm, w_in, conv_dw, conv_dw_b, conv_ln_g, conv_ln_b, conv_pw, pool_w, pool_scale, w_out, ffn2_norm, ffn2_w_gate, ffn2_w_up, ffn2_w_down, final_norm, loss_target, m_ffn1_norm, m_ffn1_w_gate, m_ffn1_w_up, m_ffn1_w_down, m_mix_norm, m_w_in, m_conv_dw, m_conv_dw_b, m_conv_ln_g, m_conv_ln_b, m_conv_pw, m_pool_w, m_pool_scale, m_w_out, m_ffn2_norm, m_ffn2_w_gate, m_ffn2_w_up, m_ffn2_w_down, m_final_norm, v_ffn1_norm, v_ffn1_w_gate, v_ffn1_w_up, v_ffn1_w_down, v_mix_norm, v_w_in, v_conv_dw, v_conv_dw_b, v_conv_ln_g, v_conv_ln_b, v_conv_pw, v_pool_w, v_pool_scale, v_w_out, v_ffn2_norm, v_ffn2_w_gate, v_ffn2_w_up, v_ffn2_w_down, v_final_norm):
    given = dict(locals())
    wts = {k: given[k] for k in WEIGHTS}
    mom_m = {k: given["m_" + k] for k in WEIGHTS}
    mom_v = {k: given["v_" + k] for k in WEIGHTS}
    xt, target = x[0], loss_target[0]

    shard = {k: _as_stored(k, wts[k]) if k == "conv_dw" else _as_stored(k, wts[k]).astype(BF16)
             for k in WEIGHTS if k.endswith(FFN_W) or k in MID}
    w = {k: wts[k].reshape(1, -1) for k in SMALL_1024 + SMALL_512}
    w["pool_w"] = wts["pool_w"].astype(BF16)

    (h1, s1, p1, a1, w["ffn1_w_gate"], w["ffn1_w_up"]), ((w["ffn1_w_down"],),) = _ffn_up_gather(
        xt, w["ffn1_norm"], shard["ffn1_w_gate"], shard["ffn1_w_up"], "ffn1_up_gather",
        cargos=[Cargo("gather_slots", [shard["ffn1_w_down"]])])
    x1, (mid, (w["ffn2_w_down"],)) = _ffn_down(
        xt, a1, w["ffn1_w_down"], "ffn1_down",
        cargos=[Cargo("gather_chips", [shard[k] for k in MID]), Cargo("gather_slots", [shard["ffn2_w_down"]])])
    w["w_in"] = mid[0]
    w["conv_dw"] = mid[1].transpose(1, 0, 2).reshape(CONV_WIDTH + 1, D_CONV)
    w["conv_pw"] = mid[2].reshape(D_CONV, D_CONV)
    w["w_out"] = mid[3].reshape(D_CONV + D_POOL, D_MODEL)
    (x2, h2, proj, u1, u3, mixed, cat), ((w["ffn2_w_gate"], w["ffn2_w_up"]),) = _mix_fwd(
        x1, w["mix_norm"], w["w_in"], w["conv_dw"], w["conv_dw_b"], w["conv_ln_g"], w["conv_ln_b"], w["conv_pw"],
        w["pool_w"], w["pool_scale"], w["w_out"], "mix_fwd",
        cargos=[Cargo("gather_slots", [shard["ffn2_w_gate"], shard["ffn2_w_up"]])])
    x3, h3, s2, p2, a2 = _ffn_fwd(x2, w["ffn2_norm"], w["ffn2_w_gate"], w["ffn2_w_up"], w["ffn2_w_down"], "ffn2_fwd")
    dx3, loss_share, d_final = _final_norm_loss(x3, target, w["final_norm"], "final_norm_loss")

    g = {"final_norm": d_final}
    sums = {}

    def landed(names, parts):
        sums.update(zip(names, parts))

    dx2, g["ffn2_norm"], df2, dg2, du2 = _ffn_bwd(dx3, x2, w["ffn2_norm"], s2, p2, w["ffn2_w_gate"],
                                                   w["ffn2_w_up"], w["ffn2_w_down"], "ffn2_bwd")
    def ffn_wgrad(names, hids, tok, kernel_name, cargos=()):
        parts, cargo_outs = _wgrad_hid_tok_scatter(hids, tok, kernel_name, cargos=cargos)
        landed(names, parts)
        return cargo_outs

    ffn_wgrad(["ffn2_w_gate", "ffn2_w_up"], [dg2, du2], h3, "ffn2_dw_gate_up")
    ffn_wgrad(["ffn2_w_down"], [a2], df2, "ffn2_dw_down")
    (dx1, dproj, dco, dpo, g_dw, g["conv_dw_b"], g["conv_ln_g"], g["conv_ln_b"], g["pool_scale"],
     g["mix_norm"]), (swapped2,) = _mix_bwd(
        dx2, u1, mixed, proj, x1, w["mix_norm"], w["conv_dw"], w["conv_ln_g"], w["conv_ln_b"], w["conv_pw"],
        w["pool_w"], w["pool_scale"], w["w_out"], w["w_in"], "mix_bwd",
        cargos=[Cargo("swap", [sums["ffn2_" + k] for k in FFN_W])])
    g_out, _ = _wgrad_2d(cat, dx2, 1, BF16, "dw_out")
    g_pw, _ = _wgrad_2d(u3, dco, 1, BF16, "dconv_pw")
    g["pool_w"], _ = _wgrad_2d(mixed, dpo, 4, F32, "dpool_w", group_diag=True)
    slabs = [g_pw.reshape(N_CHIPS, D_CONV // N_CHIPS, D_CONV),
             g_out.reshape(N_CHIPS, (D_CONV + D_POOL) // N_CHIPS, D_MODEL)]
    g_in, (parts,) = _wgrad_2d(h2, dproj, N_CHIPS, BF16, "dw_in", cargos=[Cargo("scatter_chips", slabs)])
    landed(["conv_pw", "w_out"], parts)
    dx, g["ffn1_norm"], df1, dg1, du1_ = _ffn_bwd(dx1, xt, w["ffn1_norm"], s1, p1, w["ffn1_w_gate"],
                                                   w["ffn1_w_up"], w["ffn1_w_down"], "ffn1_bwd")
    slabs = [g_in, g_dw.reshape(CONV_WIDTH + 1, N_CHIPS, D_CONV // N_CHIPS).transpose(1, 0, 2)]
    (parts,) = ffn_wgrad(["ffn1_w_gate", "ffn1_w_up"], [dg1, du1_], h1, "ffn1_dw_gate_up",
                         cargos=[Cargo("scatter_chips", slabs)])
    landed(["w_in", "conv_dw"], parts)
    swapped_mid, swapped_gate_up, small_parts = ffn_wgrad(
        ["ffn1_w_down"], [a1], df1, "ffn1_dw_down",
        cargos=[Cargo("swap", [sums[k] for k in MID]), Cargo("swap", [sums["ffn1_w_gate"], sums["ffn1_w_up"]]),
                Cargo("gather_devices", [_pack_small(g, spare=loss_share)])])
    swapped_down = _exchange(Cargo("swap", [sums["ffn1_w_down"]]), "swap_last")

    theirs = dict(zip(["ffn2_" + k for k in FFN_W], swapped2))
    theirs.update(zip(MID, swapped_mid))
    theirs.update(ffn1_w_gate=swapped_gate_up[0], ffn1_w_up=swapped_gate_up[1], ffn1_w_down=swapped_down[0])
    grads, deltas, new_m, new_v = {}, {}, {}, {}
    for k in theirs:
        res = _adamw([sums[k], theirs[k]], _as_stored(k, wts[k]), _as_stored(k, mom_m[k]),
                     _as_stored(k, mom_v[k]), "adamw_" + k)
        grads[k], deltas[k], new_m[k], new_v[k] = [_as_given(k, t) for t in res]
    res = _adamw(small_parts, _pack_small(wts), _pack_small(mom_m), _pack_small(mom_v), "adamw_small")
    for dst, packed in zip((grads, deltas, new_m, new_v), res):
        dst.update(_unpack_small(packed))
    loss = res[0][PACK_LOSS_ROW, 0]

    out = [loss, dx[None]]
    for group in (grads, deltas, new_m, new_v):
        out += [group[k] for k in WEIGHTS]
    return tuple(out)
```

```python
import functools

import jax
import jax.numpy as jnp
from jax import lax
from jax.experimental import pallas as pl
from jax.experimental.pallas import tpu as pltpu

F32 = jnp.float32
BF16 = jnp.bfloat16
MESH = pl.DeviceIdType.MESH

N_CHIPS = 4
N_DEV = 8
D_MODEL = 1024
D_CONV = 512
D_POOL = 512
CONV_WIDTH = 31
POOL_WINDOWS = (2, 4, 8, 16)
POOL_GROUP = 128
D_IN = 2 * D_CONV + D_POOL
HALO = 32
RMS_EPS = 1e-6
LN_EPS = 1e-5
FFN_RES_WEIGHT = 0.5
ADAM_LR = 0.001
ADAM_B1 = 0.9
ADAM_B2 = 0.999
ADAM_EPS = 1e-08
ADAM_WD = 0.01
ADAM_STEP = 10
VMEM_LIMIT_BYTES = 52 * 1024 * 1024
VMEM_LIMIT_BYTES_BWD = 58 * 1024 * 1024
TM_FFN = 512
TM_MIX = 256
TT_WGRAD = 2048
STRIP = 16
SLOTS_PER_STEP = 2
SLOTS_PER_STEP_FWD = 4
SUBLANES = 8
RELAY_AT_EIGHTHS = 7

HBM = pl.BlockSpec(memory_space=pl.ANY)


def _dot(a, b):
    return jnp.dot(a, b, preferred_element_type=F32)


def _dot_nt(a, b):
    return lax.dot_general(a, b, (((1,), (1,)), ((), ())), preferred_element_type=F32)


def _dot_tn(a, b):
    return lax.dot_general(a, b, (((0,), (0,)), ((), ())), preferred_element_type=F32)


def _sds(shape, dtype):
    return jax.ShapeDtypeStruct(shape, dtype)


def _rms_stats(xv):
    r = lax.rsqrt(jnp.mean(xv * xv, axis=-1, keepdims=True) + RMS_EPS)
    return r, xv * r


def _swiglu_saved(gate, up):
    sig = jax.nn.sigmoid(gate)
    silu = gate * sig
    return silu, up * (sig * (1.0 + gate * (1.0 - sig))), silu * up


def _rms_bwd(dh, n, r, gain):
    dn = dh * gain
    return r * (dn - n * jnp.mean(dn * n, axis=-1, keepdims=True))


def _place():
    x, y, c = lax.axis_index("x"), lax.axis_index("y"), lax.axis_index("c")
    return x, y, c, [(1 - x, y), (x, 1 - y), (1 - x, 1 - y)]


class Cargo:
    def __init__(self, kind, arrays):
        self.kind, self.arrays = kind, list(arrays)
        n = len(self.arrays)
        self.two_level = kind in ("gather_slots", "gather_chips")
        if self.two_level:
            self.out_shape = [_sds((N_CHIPS,) + a.shape, a.dtype) for a in self.arrays]
        elif kind == "gather_devices":
            self.out_shape = [_sds((N_DEV,) + a.shape, a.dtype) for a in self.arrays]
        else:
            self.out_shape = [_sds(a.shape, a.dtype) for a in self.arrays]
        n_remote = n * {"swap": 1, "gather_devices": N_DEV - 1}.get(kind, N_CHIPS - 1)
        n_own = 0 if kind == "swap" else n
        n_relay = n_remote if self.two_level else 0
        dma = pltpu.SemaphoreType.DMA
        self.scratch = [dma((n_remote,)), dma((n_remote,)), dma((max(n_own, 1),)),
                        dma((max(n_relay, 1),)), dma((max(n_relay, 1),))]

    def _plan(self, ins, outs):
        x, y, c, chips = _place()
        q = 2 * x + y
        sibling = (x, y, 1 - c)
        own, remote, relays = [], [], []
        for a, o in zip(ins, outs):
            if self.two_level:
                half = a.shape[0] // 2
                mine = pl.ds(pl.multiple_of(c * half, SUBLANES), half)
                theirs = pl.ds(pl.multiple_of((1 - c) * half, SUBLANES), half)
                own.append((a, o.at[0 if self.kind == "gather_slots" else q]))
                for j, (px, py) in enumerate(chips):
                    there, here = (j + 1, j + 1) if self.kind == "gather_slots" else (q, 2 * px + py)
                    remote.append((a.at[mine], o.at[there, mine], o.at[here, mine], (px, py, c)))
                    relays.append((o.at[here, mine], o.at[here, mine], o.at[here, theirs], sibling))
            elif self.kind == "scatter_chips":
                own.append((a.at[q], o.at[q]))
                remote += [(a.at[2 * px + py], o.at[q], o.at[2 * px + py], (px, py, c)) for px, py in chips]
            elif self.kind == "swap":
                remote.append((a, o, o, sibling))
            else:
                own.append((a, o.at[4 * x + 2 * y + c]))
                for k in range(1, N_DEV):
                    px, py, pc = x ^ (k >> 2 & 1), y ^ (k >> 1 & 1), c ^ (k & 1)
                    remote.append((a, o.at[4 * x + 2 * y + c], o.at[4 * px + 2 * py + pc], (px, py, pc)))
        return own, remote, relays

    @staticmethod
    def _copies(entries, send_sems, recv_sems):
        out = []
        for k, (src, dst, landed, peer) in enumerate(entries):
            def make(dst_ref, k=k, src=src, peer=peer):
                return pltpu.make_async_remote_copy(src_ref=src, dst_ref=dst_ref, send_sem=send_sems.at[k],
                                                    recv_sem=recv_sems.at[k], device_id=peer, device_id_type=MESH)
            out.append((make(dst), make(landed)))
        return out

    def start(self, ins, outs, sems):
        own, remote, _ = self._plan(ins, outs)
        for k, (src, dst) in enumerate(own):
            pltpu.make_async_copy(src, dst, sems[2].at[k]).start()
        for mine, _ in self._copies(remote, sems[0], sems[1]):
            mine.start()

    def relay(self, ins, outs, sems):
        _, remote, relays = self._plan(ins, outs)
        passed = self._copies(relays, sems[3], sems[4])
        for (_, arriving), (mine, _) in zip(self._copies(remote, sems[0], sems[1]), passed):
            arriving.wait_recv()
            mine.start()

    def wait(self, ins, outs, sems):
        own, remote, relays = self._plan(ins, outs)
        for mine, arriving in self._copies(remote, sems[0], sems[1]):
            mine.wait_send()
            if not self.two_level:
                arriving.wait_recv()
        for mine, arriving in self._copies(relays, sems[3], sems[4]):
            mine.wait_send()
            arriving.wait_recv()
        for k, (src, dst) in enumerate(own):
            pltpu.make_async_copy(src, dst, sems[2].at[k]).wait()


N_CARGO_SEMS = 5


def _call(body, *, name, grid, in_specs, out_specs, out_shape, args, scratch_shapes=(), cargos=(),
          vmem_limit_bytes=VMEM_LIMIT_BYTES):
    n_in, n_out, n_scr = len(in_specs), len(out_specs), len(scratch_shapes)
    c_in = [len(cg.arrays) for cg in cargos]
    n_cin = sum(c_in)

    def wrapped(*refs):
        ins = refs[:n_in]
        cins = refs[n_in:n_in + n_cin]
        outs = refs[n_in + n_cin:n_in + n_cin + n_out]
        couts = refs[n_in + n_cin + n_out:n_in + 2 * n_cin + n_out]
        scr = refs[n_in + 2 * n_cin + n_out:n_in + 2 * n_cin + n_out + n_scr]
        sems = refs[n_in + 2 * n_cin + n_out + n_scr:]
        step, n_steps = 0, 1
        for ax, size in enumerate(grid):
            step = step * size + pl.program_id(ax)
            n_steps *= size

        def each(method, only_two_level=False):
            at = 0
            for k, cg in enumerate(cargos):
                if cg.two_level or not only_two_level:
                    getattr(cg, method)(cins[at:at + c_in[k]], couts[at:at + c_in[k]],
                                        sems[N_CARGO_SEMS * k:N_CARGO_SEMS * (k + 1)])
                at += c_in[k]

        body(*ins, *outs, *scr)
        if cargos:
            pl.when(step == 0)(lambda: each("start"))
        if any(cg.two_level for cg in cargos):
            pl.when(step == (RELAY_AT_EIGHTHS * n_steps) // 8)(lambda: each("relay", only_two_level=True))
        if cargos:
            pl.when(step == n_steps - 1)(lambda: each("wait"))

    res = pl.pallas_call(
        wrapped, name=name, grid=grid,
        in_specs=list(in_specs) + [HBM] * n_cin,
        out_specs=list(out_specs) + [HBM] * n_cin,
        out_shape=list(out_shape) + [s for cg in cargos for s in cg.out_shape],
        scratch_shapes=list(scratch_shapes) + [s for cg in cargos for s in cg.scratch],
        compiler_params=pltpu.CompilerParams(dimension_semantics=("arbitrary",) * len(grid),
                                             vmem_limit_bytes=vmem_limit_bytes),
    )(*args, *[a for cg in cargos for a in cg.arrays])
    outs, rest = list(res[:n_out]), list(res[n_out:])
    cargo_outs = []
    for k in c_in:
        cargo_outs.append(rest[:k])
        rest = rest[k:]
    return outs, cargo_outs


def _exchange(cargo, name):
    _, (outs,) = _call(lambda: None, name=name, grid=(1,), in_specs=[], out_specs=[], out_shape=[], args=[],
                       cargos=[cargo])
    return outs


def _ffn_up_gather(x, gain, wg_t, wu_t, name, cargos=()):
    t_len, d = x.shape
    fq = wg_t.shape[0]
    tm = min(TM_FFN, t_len)
    n_tiles = t_len // tm
    relay_tile = n_tiles // 2
    fetch_tile = min(relay_tile + 1, n_tiles - 1)

    def body(x_ref, g_ref, wg_in, wu_in, h_ref, s_ref, p_ref, a_ref, wg_all, wu_all,
             wg_v, wu_v, h_all, send_sems, recv_sems, pass_send_sems, pass_recv_sems, own_sems, load_sems):
        s = pl.program_id(0)
        i = pl.program_id(1)
        x_, y_, c_, chips = _place()
        shards = ((wg_in, wg_all, wg_v), (wu_in, wu_all, wu_v))
        mine = pl.ds(pl.multiple_of(c_ * (fq // 2), SUBLANES), fq // 2)
        theirs = pl.ds(pl.multiple_of((1 - c_) * (fq // 2), SUBLANES), fq // 2)

        def to_peer(k, j):
            w_in, w_all, _ = shards[k]
            return pltpu.make_async_remote_copy(
                src_ref=w_in.at[mine], dst_ref=w_all.at[j + 1, mine], send_sem=send_sems.at[3 * k + j],
                recv_sem=recv_sems.at[3 * k + j], device_id=(*chips[j], c_), device_id_type=MESH)

        def to_sibling(k, j, landing=False):
            w_all = shards[k][1]
            return pltpu.make_async_remote_copy(
                src_ref=w_all.at[j + 1, mine], dst_ref=w_all.at[j + 1, theirs if landing else mine],
                send_sem=pass_send_sems.at[3 * k + j], recv_sem=pass_recv_sems.at[3 * k + j],
                device_id=(x_, y_, 1 - c_), device_id_type=MESH)

        def keep(k):
            return pltpu.make_async_copy(shards[k][0], shards[k][1].at[0], own_sems.at[k])

        @pl.when((s == 0) & (i == 0))
        def _():
            for j in range(N_CHIPS - 1):
                for k in range(2):
                    to_peer(k, j).start()
            for k in range(2):
                keep(k).start()

        def load(k, slot):
            src = shards[k][0] if slot == 0 else shards[k][1].at[slot]
            return pltpu.make_async_copy(src, shards[k][2].at[slot % 2], load_sems.at[k])

        @pl.when((s == 0) & (i == 0))
        def _():
            for k in range(2):
                load(k, 0).start()
            for k in range(2):
                load(k, 0).wait()

        def pass_on(slot):
            for k in range(2):
                to_peer(k, slot - 1).wait_recv()
                to_sibling(k, slot - 1).start()

        def fetch(slot):
            for k in range(2):
                to_sibling(k, slot - 1, landing=True).wait_recv()
                load(k, slot).start()

        for slot in range(1, N_CHIPS):
            pl.when((s == slot - 1) & (i == relay_tile))(functools.partial(pass_on, slot))
            pl.when((s == slot - 1) & (i == fetch_tile))(functools.partial(fetch, slot))

            @pl.when((s == slot) & (i == 0))
            def _():
                for k in range(2):
                    load(k, slot).wait()

        @pl.when(s == 0)
        def _():
            _, n = _rms_stats(x_ref[...])
            h_new = (n * g_ref[...]).astype(BF16)
            h_ref[...] = h_new
            h_all[i] = h_new

        h = h_all[i]
        silu, dgate, act = _swiglu_saved(_dot_nt(h, wg_v[s % 2]), _dot_nt(h, wu_v[s % 2]))
        s_ref[...] = silu.astype(BF16)
        p_ref[...] = dgate.astype(BF16)
        a_ref[...] = act.astype(BF16)

        @pl.when((s == N_CHIPS - 1) & (i == n_tiles - 1))
        def _():
            for k in range(2):
                for j in range(N_CHIPS - 1):
                    to_peer(k, j).wait_send()
                    to_sibling(k, j).wait_send()
                keep(k).wait()

    tok = pl.BlockSpec((tm, d), lambda s, i: (jnp.where(s == 0, i, n_tiles - 1), 0))
    hid = pl.BlockSpec((None, tm, fq), lambda s, i: (s, i, 0))
    outs, cargo_outs = _call(
        body, name=name, grid=(N_CHIPS, n_tiles),
        in_specs=[tok, pl.BlockSpec((1, d), lambda s, i: (0, 0)), HBM, HBM],
        out_specs=[tok, hid, hid, hid, HBM, HBM],
        out_shape=[_sds((t_len, d), BF16)] + [_sds((N_CHIPS, t_len, fq), BF16)] * 3
        + [_sds((N_CHIPS, fq, d), BF16)] * 2,
        scratch_shapes=[pltpu.VMEM((2, fq, d), BF16), pltpu.VMEM((2, fq, d), BF16),
                        pltpu.VMEM((n_tiles, tm, d), BF16)]
        + [pltpu.SemaphoreType.DMA((6,))] * 4 + [pltpu.SemaphoreType.DMA((2,))] * 2,
        args=[x, gain, wg_t, wu_t], cargos=cargos)
    return outs, cargo_outs


def _load_once(hbm_refs, vmem_refs, sems, first):
    @pl.when(first)
    def _():
        copies = [pltpu.make_async_copy(src, dst, sems.at[k]) for k, (src, dst) in enumerate(zip(hbm_refs, vmem_refs))]
        for cp in copies:
            cp.start()
        for cp in copies:
            cp.wait()


def _ffn_down(x, act, wd, name, cargos=()):
    t_len, d = x.shape
    nq, fq, _ = wd.shape
    tm = min(TM_FFN, t_len)

    def body(x_ref, a_ref, wd_ref, xo_ref):
        y = _dot(a_ref[0], wd_ref[0])
        for j in range(1, nq):
            y = y + _dot(a_ref[j], wd_ref[j])
        xo_ref[...] = x_ref[...] + FFN_RES_WEIGHT * y

    tok = pl.BlockSpec((tm, d), lambda i: (i, 0))
    (xo,), cargo_outs = _call(
        body, name=name, grid=(t_len // tm,),
        in_specs=[tok, pl.BlockSpec((nq, tm, fq), lambda i: (0, i, 0)), pl.BlockSpec((nq, fq, d), lambda i: (0, 0, 0))],
        out_specs=[tok], out_shape=[_sds((t_len, d), F32)], args=[x, act, wd], cargos=cargos)
    return xo, cargo_outs


def _ffn_fwd(x, gain, wg_t, wu_t, wd, name):
    t_len, d = x.shape
    nq, fq, _ = wd.shape
    tm = min(TM_FFN, t_len)

    def body(x_ref, g_ref, wg_hbm, wu_hbm, wd_hbm, xo_ref, h_ref, s_ref, p_ref, a_ref,
             h_s, acc, wg_v, wu_v, wd_v, load_sems):
        i = pl.program_id(0)
        j = pl.program_id(1)
        _load_once((wg_hbm, wu_hbm, wd_hbm), (wg_v, wu_v, wd_v), load_sems, (i == 0) & (j == 0))

        @pl.when(j == 0)
        def _():
            _, n = _rms_stats(x_ref[...])
            h = (n * g_ref[...]).astype(BF16)
            h_s[...] = h
            h_ref[...] = h
            acc[...] = jnp.zeros_like(acc)

        h = h_s[...]
        y = None
        for jj in range(SLOTS_PER_STEP_FWD):
            slot = j * SLOTS_PER_STEP_FWD + jj
            silu, dgate, act = _swiglu_saved(_dot_nt(h, wg_v[slot]), _dot_nt(h, wu_v[slot]))
            s_ref[jj] = silu.astype(BF16)
            p_ref[jj] = dgate.astype(BF16)
            a_ref[jj] = act.astype(BF16)
            part = _dot(a_ref[jj], wd_v[slot])
            y = part if y is None else y + part
        acc[...] += y

        @pl.when(j == nq // SLOTS_PER_STEP_FWD - 1)
        def _():
            xo_ref[...] = x_ref[...] + FFN_RES_WEIGHT * acc[...]

    tok = pl.BlockSpec((tm, d), lambda i, j: (i, 0))
    hid = pl.BlockSpec((SLOTS_PER_STEP_FWD, tm, fq), lambda i, j: (j, i, 0))
    outs, _ = _call(
        body, name=name, grid=(t_len // tm, nq // SLOTS_PER_STEP_FWD),
        in_specs=[tok, pl.BlockSpec((1, d), lambda i, j: (0, 0)), HBM, HBM, HBM],
        out_specs=[tok, tok, hid, hid, hid],
        out_shape=[_sds((t_len, d), F32), _sds((t_len, d), BF16)] + [_sds((nq, t_len, fq), BF16)] * 3,
        scratch_shapes=[pltpu.VMEM((tm, d), BF16), pltpu.VMEM((tm, d), F32)]
        + [pltpu.VMEM((nq, fq, d), BF16)] * 3 + [pltpu.SemaphoreType.DMA((3,))],
        args=[x, gain, wg_t, wu_t, wd], vmem_limit_bytes=VMEM_LIMIT_BYTES_BWD)
    return outs


def _ffn_bwd(dy, x_in, gain, silu, dgate_du, wg_t, wu_t, wd, name):
    t_len, d = dy.shape
    nq, fq, _ = wd.shape
    tm = min(TM_FFN, t_len)

    def body(dy_ref, x_ref, g_ref, s_ref, p_ref, wg_hbm, wu_hbm, wd_hbm,
             dx_ref, dgain_ref, df_ref, dg_ref, du_ref, df_s, dh_acc, dact_s, wg_v, wu_v, wd_v, load_sems):
        i = pl.program_id(0)
        j = pl.program_id(1)
        _load_once((wg_hbm, wu_hbm, wd_hbm), (wg_v, wu_v, wd_v), load_sems, (i == 0) & (j == 0))

        @pl.when((i == 0) & (j == 0))
        def _():
            dgain_ref[...] = jnp.zeros_like(dgain_ref)

        @pl.when(j == 0)
        def _():
            df = (FFN_RES_WEIGHT * dy_ref[...]).astype(BF16)
            df_s[...] = df
            df_ref[...] = df
            dh_acc[...] = jnp.zeros_like(dh_acc)

        slots = [j * SLOTS_PER_STEP + jj for jj in range(SLOTS_PER_STEP)]
        for jj, slot in enumerate(slots):
            dact_s[jj] = _dot_nt(df_s[...], wd_v[slot])

        for jj in range(SLOTS_PER_STEP):
            for r0 in range(0, tm, STRIP):
                rows = slice(r0, r0 + STRIP)
                dact = dact_s[jj, rows, :]
                dg_ref[jj, rows, :] = (dact * p_ref[jj, rows, :].astype(F32)).astype(BF16)
                du_ref[jj, rows, :] = (dact * s_ref[jj, rows, :].astype(F32)).astype(BF16)

        dh = None
        for jj, slot in enumerate(slots):
            part = _dot(dg_ref[jj], wg_v[slot]) + _dot(du_ref[jj], wu_v[slot])
            dh = part if dh is None else dh + part
        dh_acc[...] += dh

        @pl.when(j == nq // SLOTS_PER_STEP - 1)
        def _():
            r, n = _rms_stats(x_ref[...])
            dh = dh_acc[...]
            dgain_ref[...] += jnp.sum(dh * n, axis=0, keepdims=True)
            dx_ref[...] = dy_ref[...] + _rms_bwd(dh, n, r, g_ref[...])

    tok = pl.BlockSpec((tm, d), lambda i, j: (i, 0))
    vec = pl.BlockSpec((1, d), lambda i, j: (0, 0))
    hid = pl.BlockSpec((SLOTS_PER_STEP, tm, fq), lambda i, j: (j, i, 0))
    outs, _ = _call(
        body, name=name, grid=(t_len // tm, nq // SLOTS_PER_STEP),
        in_specs=[tok, tok, vec, hid, hid, HBM, HBM, HBM],
        out_specs=[tok, vec, tok, hid, hid],
        out_shape=[_sds((t_len, d), F32), _sds((1, d), F32), _sds((t_len, d), BF16),
                   _sds((nq, t_len, fq), BF16), _sds((nq, t_len, fq), BF16)],
        scratch_shapes=[pltpu.VMEM((tm, d), BF16), pltpu.VMEM((tm, d), F32),
                        pltpu.VMEM((SLOTS_PER_STEP, tm, fq), F32)]
        + [pltpu.VMEM((nq, fq, d), BF16)] * 3 + [pltpu.SemaphoreType.DMA((3,))],
        args=[dy, x_in, gain, silu, dgate_du, wg_t, wu_t, wd], vmem_limit_bytes=VMEM_LIMIT_BYTES_BWD)
    return outs


def _wgrad(lhs, rhs, l_spec, r_spec, out_shape, out_spec, acc_shape, grid, name, cargos=()):
    n_t = grid[-1]
    t_axis = len(grid) - 1

    def body(l_ref, r_ref, o_ref, acc):
        t = pl.program_id(t_axis)

        @pl.when(t == 0)
        def _():
            acc[...] = jnp.zeros_like(acc)

        acc[...] += _dot_tn(l_ref[...].astype(BF16), r_ref[...].astype(BF16))

        @pl.when(t == n_t - 1)
        def _():
            o_ref[...] = acc[...].astype(o_ref.dtype)

    (out,), cargo_outs = _call(
        body, name=name, grid=grid, in_specs=[l_spec, r_spec], out_specs=[out_spec], out_shape=[out_shape],
        scratch_shapes=[pltpu.VMEM(acc_shape, F32)], args=[lhs, rhs], cargos=cargos)
    return out, cargo_outs


def _wgrad_hid_tok_scatter(hids, tok, name, cargos=()):
    t_len, d = tok.shape
    n_w = len(hids)
    nq, _, fq = hids[0].shape
    half = fq // 2
    tt = min(TT_WGRAD, t_len)
    n_t = t_len // tt
    per_w = 4
    n_sem = 6

    def body(*refs):
        l_refs, r_ref, parts_refs = refs[:n_w], refs[n_w], refs[n_w + 1:2 * n_w + 1]
        scr = refs[2 * n_w + 1:]
        bufs = [scr[per_w * w:per_w * (w + 1)] for w in range(n_w)]
        zeros = scr[per_w * n_w]
        sems = [scr[per_w * n_w + 1 + n_sem * w:per_w * n_w + 1 + n_sem * (w + 1)] for w in range(n_w)]
        g = pl.program_id(0)
        t = pl.program_id(1)
        x_, y_, c_, chips = _place()
        mine = pl.ds(pl.multiple_of(c_ * half, STRIP), half)
        theirs = pl.ds(pl.multiple_of((1 - c_) * half, STRIP), half)

        def to_sibling(w, slot):
            return pltpu.make_async_remote_copy(
                src_ref=bufs[w][1].at[theirs], dst_ref=bufs[w][2].at[slot], send_sem=sems[w][0].at[slot],
                recv_sem=sems[w][1].at[slot], device_id=(x_, y_, 1 - c_), device_id_type=MESH)

        def to_peer(w, j):
            return pltpu.make_async_remote_copy(
                src_ref=bufs[w][3].at[j + 1], dst_ref=parts_refs[w].at[j + 1, mine], send_sem=sems[w][2].at[j],
                recv_sem=sems[w][3].at[j], device_id=(*chips[j], c_), device_id_type=MESH)

        def keep(w):
            return pltpu.make_async_copy(bufs[w][3].at[0], parts_refs[w].at[0, mine], sems[w][4])

        def blank(w, slot):
            return pltpu.make_async_copy(zeros, parts_refs[w].at[slot, theirs], sems[w][5].at[slot])

        @pl.when((g == 0) & (t == 0))
        def _():
            zeros[...] = jnp.zeros_like(zeros)
            for w in range(n_w):
                for slot in range(nq):
                    blank(w, slot).start()

        @pl.when(t == 0)
        def _():
            for w in range(n_w):
                bufs[w][0][...] = jnp.zeros_like(bufs[w][0])

        rhs = r_ref[...]
        for w in range(n_w):
            bufs[w][0][...] += _dot_tn(l_refs[w][...], rhs)

        for step in range(nq):
            slot = (step + 1) % nq

            @pl.when((g == step) & (t == n_t - 1))
            def _():
                for w in range(n_w):
                    acc, stage, _, _ = bufs[w]
                    if step > 0:
                        to_sibling(w, step).wait_send()
                    stage[...] = acc[...].astype(BF16)
                    to_sibling(w, slot).start()
                for w in range(n_w):
                    _, stage, pair, summed = bufs[w]
                    to_sibling(w, slot).wait_recv()
                    summed[slot] = (stage[mine, :].astype(F32) + pair[slot].astype(F32)).astype(BF16)
                    if slot > 0:
                        to_peer(w, slot - 1).start()
                    else:
                        keep(w).start()

        @pl.when((g == nq - 1) & (t == n_t - 1))
        def _():
            for w in range(n_w):
                for j in range(N_CHIPS - 1):
                    to_peer(w, j).wait()
                keep(w).wait()
                to_sibling(w, 0).wait_send()
                for slot in range(nq):
                    blank(w, slot).wait()

    dma = pltpu.SemaphoreType.DMA
    scratch = []
    for _ in range(n_w):
        scratch += [pltpu.VMEM((fq, d), F32), pltpu.VMEM((fq, d), BF16), pltpu.VMEM((nq, half, d), BF16),
                    pltpu.VMEM((nq, half, d), BF16)]
    scratch.append(pltpu.VMEM((half, d), BF16))
    for _ in range(n_w):
        scratch += [dma((nq,)), dma((nq,)), dma((N_CHIPS - 1,)), dma((N_CHIPS - 1,)), dma(()), dma((nq,))]
    parts, cargo_outs = _call(
        body, name=name, grid=(nq, n_t),
        in_specs=[pl.BlockSpec((None, tt, fq), lambda g, t: ((g + 1) % nq, t, 0))] * n_w
        + [pl.BlockSpec((tt, d), lambda g, t: (t, 0))],
        out_specs=[HBM] * n_w, out_shape=[_sds((nq, fq, d), BF16)] * n_w,
        scratch_shapes=scratch, args=[*hids, tok], cargos=cargos)
    return parts, cargo_outs


def _wgrad_2d(lhs, rhs, n_col_blocks, out_dtype, name, group_diag=False, cargos=()):
    t_len, k = lhs.shape
    n = rhs.shape[1]
    nb = n // n_col_blocks
    kb = k // n_col_blocks if group_diag else k
    tt = min(TT_WGRAD, t_len)
    l_map = (lambda q, t: (t, q)) if group_diag else (lambda q, t: (t, 0))
    return _wgrad(lhs, rhs,
                  pl.BlockSpec((tt, kb), l_map),
                  pl.BlockSpec((tt, nb), lambda q, t: (t, q)),
                  _sds((n_col_blocks, kb, nb), out_dtype),
                  pl.BlockSpec((None, kb, nb), lambda q, t: (q, 0, 0)),
                  (kb, nb), (n_col_blocks, t_len // tt), name, cargos)


def _layernorm_stats(u1):
    mu = jnp.mean(u1, axis=-1, keepdims=True)
    xc = u1 - mu
    rstd = lax.rsqrt(jnp.mean(xc * xc, axis=-1, keepdims=True) + LN_EPS)
    return rstd, xc * rstd


def _positions(i, tm, rows, offset=0):
    return (lax.broadcasted_iota(jnp.int32, (rows, 1), 0) + (i * tm + offset)).astype(F32)


SHIFT_ROWS = HALO - SUBLANES


def _fill_shifted(ext_s, sh_s, tm):
    for b in range(1, SUBLANES):
        sh_s[b - 1] = ext_s[pl.ds(b, tm + SHIFT_ROWS), :]


def _window(ext_s, sh_s, shift, tm):
    a, b = divmod(shift, SUBLANES)
    if b == 0:
        return ext_s[pl.ds(shift, tm), :]
    return sh_s[b - 1, pl.ds(a * SUBLANES, tm), :]


def _window_sums(ext_s, lv_a, lv_b, tm, ahead):
    g = POOL_GROUP
    sign = 1 if ahead else -1
    for n, (dst, src, c0) in enumerate(((lv_a, ext_s, 0), (lv_b, lv_a, g), (lv_a, lv_b, 2 * g)), start=1):
        lo = 0 if ahead else n * SUBLANES
        rows = tm + HALO - n * SUBLANES
        shift = sign * 2 ** (n - 1)
        dst[pl.ds(lo, rows), c0:] = src[pl.ds(lo, rows), c0:] + src[pl.ds(lo + shift, rows), c0:]
    base = 0 if ahead else HALO
    rows = pl.ds(base, tm)
    far = pl.ds(base + sign * SUBLANES, tm)
    return [lv_a[rows, 0:g], lv_b[rows, g:2 * g], lv_a[rows, 2 * g:3 * g],
            lv_a[rows, 3 * g:] + lv_a[far, 3 * g:]]


def _tile(tm, cols):
    return pl.BlockSpec((tm, cols), lambda i: (i, 0))


def _whole(shape):
    return pl.BlockSpec(shape, lambda i: (0,) * len(shape))


def _mix_fwd(x1, gain, w_in, conv_dw, conv_b, ln_g, ln_b, conv_pw, pool_w, pool_scale, w_out, name, cargos=()):
    t_len, d = x1.shape
    nq, _, nb = w_in.shape
    tm = min(TM_MIX, t_len)

    def body(x_ref, g_ref, wi_ref, dw_ref, cb_ref, lg_ref, lb_ref, pw_ref, plw_ref, ps_ref, wo_ref,
             x2_ref, h_ref, p_ref, u1_ref, u3_ref, mx_ref, cat_ref, ext_s, pext_s, sh_s, tail_s, lva_s, lvb_s):
        i = pl.program_id(0)

        @pl.when(i == 0)
        def _():
            tail_s[...] = jnp.zeros_like(tail_s)

        _, n = _rms_stats(x_ref[...])
        h = (n * g_ref[...]).astype(BF16)
        h_ref[...] = h
        for q in range(nq):
            p_ref[:, q * nb:(q + 1) * nb] = _dot(h, wi_ref[q])

        a = p_ref[:, 0:D_CONV]
        g = p_ref[:, D_CONV:2 * D_CONV]
        p = p_ref[:, 2 * D_CONV:]
        ext_s[0:HALO, :] = tail_s[:, 0:D_CONV] * jax.nn.sigmoid(tail_s[:, D_CONV:2 * D_CONV])
        ext_s[HALO:, :] = a * jax.nn.sigmoid(g)
        pext_s[0:HALO, :] = tail_s[:, 2 * D_CONV:]
        pext_s[HALO:, :] = p
        tail_s[...] = p_ref[tm - HALO:tm, :]

        _fill_shifted(ext_s, sh_s, tm)
        u1 = jnp.broadcast_to(cb_ref[...], (tm, D_CONV))
        for k in range(CONV_WIDTH):
            u1 = u1 + dw_ref[k:k + 1, :] * _window(ext_s, sh_s, HALO - (CONV_WIDTH - 1) + k, tm)
        u1_ref[...] = u1
        _, nhat = _layernorm_stats(u1)
        u2 = nhat * lg_ref[...] + lb_ref[...]
        u3 = (u2 * jax.nn.sigmoid(u2)).astype(BF16)
        u3_ref[...] = u3
        cat_ref[:, 0:D_CONV] = _dot(u3, pw_ref[...]).astype(BF16)

        pos1 = _positions(i, tm, tm) + 1.0
        sums = _window_sums(pext_s, lva_s, lvb_s, tm, ahead=False)
        for gi, w in enumerate(POOL_WINDOWS):
            cols = slice(gi * POOL_GROUP, (gi + 1) * POOL_GROUP)
            mixed = (sums[gi] / jnp.minimum(pos1, float(w)) - p[:, cols]).astype(BF16)
            mx_ref[:, cols] = mixed
            out = _dot(mixed, plw_ref[gi]) * ps_ref[:, cols]
            cat_ref[:, D_CONV + gi * POOL_GROUP:D_CONV + (gi + 1) * POOL_GROUP] = out.astype(BF16)

        x2_ref[...] = x_ref[...] + _dot(cat_ref[...], wo_ref[...])

    return _call(
        body, name=name, grid=(t_len // tm,),
        in_specs=[_tile(tm, d), _whole((1, d)), _whole((nq, d, nb)), _whole((CONV_WIDTH + 1, D_CONV)),
                  _whole((1, D_CONV)), _whole((1, D_CONV)), _whole((1, D_CONV)), _whole((D_CONV, D_CONV)),
                  _whole((4, POOL_GROUP, POOL_GROUP)), _whole((1, D_POOL)), _whole((D_CONV + D_POOL, d))],
        out_specs=[_tile(tm, d), _tile(tm, d), _tile(tm, D_IN), _tile(tm, D_CONV), _tile(tm, D_CONV),
                   _tile(tm, D_POOL), _tile(tm, D_CONV + D_POOL)],
        out_shape=[_sds((t_len, d), F32), _sds((t_len, d), BF16), _sds((t_len, D_IN), F32),
                   _sds((t_len, D_CONV), F32), _sds((t_len, D_CONV), BF16), _sds((t_len, D_POOL), BF16),
                   _sds((t_len, D_CONV + D_POOL), BF16)],
        scratch_shapes=[pltpu.VMEM((tm + HALO, D_CONV), F32), pltpu.VMEM((tm + HALO, D_POOL), F32),
                        pltpu.VMEM((SUBLANES - 1, tm + SHIFT_ROWS, D_CONV), F32), pltpu.VMEM((HALO, D_IN), F32)]
        + [pltpu.VMEM((tm + HALO, D_POOL), F32)] * 2,
        args=[x1, gain, w_in, conv_dw, conv_b, ln_g, ln_b, conv_pw, pool_w, pool_scale, w_out], cargos=cargos)


def _mix_bwd(dx2, u1, mixed, proj, x1, gain, conv_dw, ln_g, ln_b, conv_pw, pool_w, pool_scale, w_out, w_in,
             name, cargos=()):
    t_len, d = x1.shape
    nq, _, nb = w_in.shape
    tm = min(TM_MIX, t_len)
    hb = tm // HALO
    n_tiles = t_len // tm

    def body(dxn_ref, u1_ref, mx_ref, p_ref, tail_ref, x_ref, dx2_ref, g_ref, dw_ref, lg_ref, lb_ref, pw_ref,
             plw_ref, ps_ref, wo_ref, wi_ref,
             dx1_ref, dp_ref, dco_ref, dpo_ref, ddw_ref, dcb_ref, dlg_ref, dlb_ref, dps_ref, dgain_ref,
             du_s, dm_s, uext_s, dext_s, mext_s, ush_s, dsh_s, lva_s, lvb_s):
        k = pl.program_id(0)

        @pl.when(k == 0)
        def _():
            for ref in (ddw_ref, dcb_ref, dlg_ref, dlb_ref, dps_ref, dgain_ref, du_s, dm_s):
                ref[...] = jnp.zeros_like(ref)

        counts = jnp.where(k < n_tiles, 1.0, 0.0)
        dcat = _dot_nt(dxn_ref[...].astype(BF16), wo_ref[...])
        dco = dcat[:, 0:D_CONV].astype(BF16)
        dco_ref[...] = dco
        du3 = _dot_nt(dco, pw_ref[...])
        rstd, nhat = _layernorm_stats(u1_ref[...])
        u2 = nhat * lg_ref[...] + lb_ref[...]
        sig = jax.nn.sigmoid(u2)
        du2 = du3 * (sig * (1.0 + u2 * (1.0 - sig)))
        dlg_ref[...] += counts * jnp.sum(du2 * nhat, axis=0, keepdims=True)
        dlb_ref[...] += counts * jnp.sum(du2, axis=0, keepdims=True)
        dnhat = du2 * lg_ref[...]
        du_s[k % 2] = rstd * (dnhat - jnp.mean(dnhat, axis=-1, keepdims=True)
                              - nhat * jnp.mean(dnhat * nhat, axis=-1, keepdims=True))
        for gi in range(len(POOL_WINDOWS)):
            cols = slice(gi * POOL_GROUP, (gi + 1) * POOL_GROUP)
            dpo = dcat[:, D_CONV + gi * POOL_GROUP:D_CONV + (gi + 1) * POOL_GROUP]
            pre = _dot(mx_ref[:, cols], plw_ref[gi])
            dps_ref[:, cols] += counts * jnp.sum(dpo * pre, axis=0, keepdims=True)
            dout = (dpo * ps_ref[:, cols]).astype(BF16)
            dpo_ref[:, cols] = dout
            dm_s[k % 2, :, cols] = _dot_nt(dout, plw_ref[gi])

        i = jnp.maximum(k - 1, 0)
        cur, nxt = (k + 1) % 2, k % 2
        first = k <= 1
        last = (k == n_tiles) | (k == 0)
        a = p_ref[:, 0:D_CONV]
        g = p_ref[:, D_CONV:2 * D_CONV]
        sg = jax.nn.sigmoid(g)
        ta = tail_ref[:, 0:D_CONV]
        tg = tail_ref[:, D_CONV:2 * D_CONV]
        uext_s[0:HALO, :] = jnp.where(first, 0.0, ta * jax.nn.sigmoid(tg))
        uext_s[HALO:, :] = a * sg
        du1 = du_s[cur]
        dext_s[0:tm, :] = du1
        dext_s[tm:, :] = jnp.where(last, 0.0, du_s[nxt, 0:HALO, :])

        _fill_shifted(uext_s, ush_s, tm)
        _fill_shifted(dext_s, dsh_s, tm)
        du0 = jnp.zeros((tm, D_CONV), F32)
        for tap in range(CONV_WIDTH):
            du0 = du0 + dw_ref[tap:tap + 1, :] * _window(dext_s, dsh_s, CONV_WIDTH - 1 - tap, tm)
            ddw_ref[tap:tap + 1, :] += jnp.sum(
                du1 * _window(uext_s, ush_s, HALO - (CONV_WIDTH - 1) + tap, tm), axis=0, keepdims=True)
        dcb_ref[...] += jnp.sum(du1, axis=0, keepdims=True)
        dp_ref[:, 0:D_CONV] = (du0 * sg).astype(BF16)
        dp_ref[:, D_CONV:2 * D_CONV] = (du0 * a * sg * (1.0 - sg)).astype(BF16)

        pos1 = _positions(i, tm, tm) + 1.0
        pos1_next = _positions(i, tm, HALO, offset=tm) + 1.0
        for gi, w in enumerate(POOL_WINDOWS):
            cols = slice(gi * POOL_GROUP, (gi + 1) * POOL_GROUP)
            dm = dm_s[cur, :, cols]
            mext_s[0:tm, cols] = dm / jnp.minimum(pos1, float(w))
            mext_s[tm:, cols] = jnp.where(last, 0.0, dm_s[nxt, 0:HALO, cols] / jnp.minimum(pos1_next, float(w)))
        sums = _window_sums(mext_s, lva_s, lvb_s, tm, ahead=True)
        for gi in range(len(POOL_WINDOWS)):
            cols = slice(gi * POOL_GROUP, (gi + 1) * POOL_GROUP)
            dp_ref[:, 2 * D_CONV + gi * POOL_GROUP:2 * D_CONV + (gi + 1) * POOL_GROUP] = (
                sums[gi] - dm_s[cur, :, cols]).astype(BF16)

        dh = _dot_nt(dp_ref[:, 0:nb], wi_ref[0])
        for q in range(1, nq):
            dh = dh + _dot_nt(dp_ref[:, q * nb:(q + 1) * nb], wi_ref[q])
        r, n = _rms_stats(x_ref[...])
        dgain_ref[...] += jnp.sum(dh * n, axis=0, keepdims=True)
        dx1_ref[...] = dx2_ref[...] + _rms_bwd(dh, n, r, g_ref[...])

    def ahead(cols):
        return pl.BlockSpec((tm, cols), lambda k: (jnp.minimum(k, n_tiles - 1), 0))

    def behind(cols):
        return pl.BlockSpec((tm, cols), lambda k: (jnp.maximum(k - 1, 0), 0))

    vec = _whole((1, D_CONV))
    return _call(
        body, name=name, grid=(n_tiles + 1,),
        in_specs=[ahead(d), ahead(D_CONV), ahead(D_POOL), behind(D_IN),
                  pl.BlockSpec((HALO, D_IN), lambda k: (jnp.maximum(jnp.maximum(k - 1, 0) * hb - 1, 0), 0)),
                  behind(d), behind(d), _whole((1, d)), _whole((CONV_WIDTH + 1, D_CONV)), vec, vec,
                  _whole((D_CONV, D_CONV)), _whole((4, POOL_GROUP, POOL_GROUP)), vec,
                  _whole((D_CONV + D_POOL, d)), _whole((nq, d, nb))],
        out_specs=[behind(d), behind(D_IN), ahead(D_CONV), ahead(D_POOL), _whole((CONV_WIDTH + 1, D_CONV)), vec,
                   vec, vec, vec, _whole((1, d))],
        out_shape=[_sds((t_len, d), F32), _sds((t_len, D_IN), BF16), _sds((t_len, D_CONV), BF16),
                   _sds((t_len, D_POOL), BF16), _sds((CONV_WIDTH + 1, D_CONV), F32), _sds((1, D_CONV), F32),
                   _sds((1, D_CONV), F32), _sds((1, D_CONV), F32), _sds((1, D_POOL), F32), _sds((1, d), F32)],
        scratch_shapes=[pltpu.VMEM((2, tm, D_CONV), F32), pltpu.VMEM((2, tm, D_POOL), F32),
                        pltpu.VMEM((tm + HALO, D_CONV), F32), pltpu.VMEM((tm + HALO, D_CONV), F32),
                        pltpu.VMEM((tm + HALO, D_POOL), F32),
                        pltpu.VMEM((SUBLANES - 1, tm + SHIFT_ROWS, D_CONV), F32),
                        pltpu.VMEM((SUBLANES - 1, tm + SHIFT_ROWS, D_CONV), F32)]
        + [pltpu.VMEM((tm + HALO, D_POOL), F32)] * 2,
        args=[dx2, u1, mixed, proj, proj, x1, dx2, gain, conv_dw, ln_g, ln_b, conv_pw, pool_w, pool_scale, w_out,
              w_in], cargos=cargos)


def _final_norm_loss(x3, target, gain, name):
    t_len, d = x3.shape
    tm = min(2 * TM_FFN, t_len)

    def body(x_ref, t_ref, g_ref, dx_ref, loss_ref, dgain_ref):
        @pl.when(pl.program_id(0) == 0)
        def _():
            loss_ref[...] = jnp.zeros_like(loss_ref)
            dgain_ref[...] = jnp.zeros_like(dgain_ref)

        r, n = _rms_stats(x_ref[...])
        err = n * g_ref[...] - t_ref[...]
        per_tok = jnp.sum(err * err, axis=-1, keepdims=True) * (1.0 / d)
        loss_ref[...] += 0.5 * jnp.sum(per_tok, axis=0, keepdims=True)
        dy = err * (1.0 / d)
        dgain_ref[...] += jnp.sum(dy * n, axis=0, keepdims=True)
        dx_ref[...] = _rms_bwd(dy, n, r, g_ref[...])

    tok = pl.BlockSpec((tm, d), lambda i: (i, 0))
    outs, _ = _call(
        body, name=name, grid=(t_len // tm,),
        in_specs=[tok, tok, pl.BlockSpec((1, d), lambda i: (0, 0))],
        out_specs=[tok, pl.BlockSpec((1, 128), lambda i: (0, 0)), pl.BlockSpec((1, d), lambda i: (0, 0))],
        out_shape=[_sds((t_len, d), F32), _sds((1, 128), F32), _sds((1, d), F32)],
        args=[x3, target, gain])
    return outs


def _row_tile(rows):
    return rows // 4 if rows % 64 == 0 else rows


def _adamw_math(w, g, m, v):
    m = ADAM_B1 * m + (1.0 - ADAM_B1) * g
    v = ADAM_B2 * v + (1.0 - ADAM_B2) * (g * g)
    m_hat = m / (1.0 - ADAM_B1 ** ADAM_STEP)
    v_hat = v / (1.0 - ADAM_B2 ** ADAM_STEP)
    delta = -ADAM_LR * (m_hat / (jnp.sqrt(v_hat) + ADAM_EPS) + ADAM_WD * w)
    return delta, m, v


def _adamw(parts, w, m, v, name):
    r, c = w.shape
    n = len(parts)
    tr = _row_tile(r)

    def body(*refs):
        g = None
        for p_ref in refs[:n]:
            s = p_ref[0].astype(F32)
            for k in range(1, p_ref.shape[0]):
                s = s + p_ref[k].astype(F32)
            g = s if g is None else g + s
        w_ref, m_ref, v_ref, g_out, d_out, m_out, v_out = refs[n:]
        delta, nm, nv = _adamw_math(w_ref[...], g, m_ref[...], v_ref[...])
        g_out[...] = g
        d_out[...] = delta
        m_out[...] = nm
        v_out[...] = nv

    blk = pl.BlockSpec((tr, c), lambda i: (i, 0))
    p_specs = [pl.BlockSpec((p.shape[0], tr, c), lambda i: (0, i, 0)) for p in parts]
    outs, _ = _call(body, name=name, grid=(r // tr,), in_specs=p_specs + [blk, blk, blk],
                    out_specs=[blk] * 4, out_shape=[_sds((r, c), F32)] * 4, args=[*parts, w, m, v])
    return outs


FFN_W = ("w_gate", "w_up", "w_down")
MID = ("w_in", "conv_dw", "conv_pw", "w_out")
SMALL_1024 = ("ffn1_norm", "mix_norm", "ffn2_norm", "final_norm")
SMALL_512 = ("conv_dw_b", "conv_ln_g", "conv_ln_b", "pool_scale")
WEIGHTS = ("ffn1_norm", "ffn1_w_gate", "ffn1_w_up", "ffn1_w_down", "mix_norm", "w_in", "conv_dw", "conv_dw_b",
           "conv_ln_g", "conv_ln_b", "conv_pw", "pool_w", "pool_scale", "w_out", "ffn2_norm", "ffn2_w_gate",
           "ffn2_w_up", "ffn2_w_down", "final_norm")
PACK_ROWS = 72
PACK_LOSS_ROW = 70


def _pad_rows(a, rows):
    return jnp.pad(a, ((0, rows - a.shape[0]), (0, 0)))


def _pack_small(t, spare=None):
    rows = [t[k].reshape(1, D_MODEL) for k in SMALL_1024]
    rows.append(jnp.concatenate([t["conv_dw_b"].reshape(1, -1), t["conv_ln_g"].reshape(1, -1)], axis=1))
    rows.append(jnp.concatenate([t["conv_ln_b"].reshape(1, -1), t["pool_scale"].reshape(1, -1)], axis=1))
    rows.append(t["pool_w"].reshape(64, D_MODEL))
    if spare is not None:
        rows.append(jnp.pad(spare, ((0, 0), (0, D_MODEL - spare.shape[1]))))
    return _pad_rows(jnp.concatenate(rows, axis=0), PACK_ROWS)


def _unpack_small(p):
    out = {k: p[i] for i, k in enumerate(SMALL_1024)}
    out["conv_dw_b"], out["conv_ln_g"] = p[4, :D_CONV], p[4, D_CONV:]
    out["conv_ln_b"], out["pool_scale"] = p[5, :D_CONV], p[5, D_CONV:]
    out["pool_w"] = p[6:70].reshape(4, POOL_GROUP, POOL_GROUP)
    return out


def _as_stored(name, a):
    if name.endswith(("w_gate", "w_up")):
        return a.T
    if name == "conv_dw":
        return _pad_rows(a, CONV_WIDTH + 1)
    return a


def _as_given(name, a):
    if name.endswith(("w_gate", "w_up")):
        return a.T
    if name == "conv_dw":
        return a[:CONV_WIDTH]
    return a


def kernel(x, ffn1_norm, ffn1_w_gate, ffn1_w_up, ffn1_w_down, mix_norm, w_in, conv_dw, conv_dw_b, conv_ln_g, conv_ln_b, conv_pw, pool_w, pool_scale, w_out, ffn2_norm, ffn2_w_gate, ffn2_w_up, ffn2_w_down, final_norm, loss_target, m_ffn1_norm, m_ffn1_w_gate, m_ffn1_w_up, m_ffn1_w_down, m_mix_norm, m_w_in, m_conv_dw, m_conv_dw_b, m_conv_ln_g, m_conv_ln_b, m_conv_pw, m_pool_w, m_pool_scale, m_w_out, m_ffn2_norm, m_ffn2_w_gate, m_ffn2_w_up, m_ffn2_w_down, m_final_norm, v_ffn1_norm, v_ffn1_w_gate, v_ffn1_w_up, v_ffn1_w_down, v_mix_norm, v_w_in, v_conv_dw, v_conv_dw_b, v_conv_ln_g, v_conv_ln_b, v_conv_pw, v_pool_w, v_pool_scale, v_w_out, v_ffn2_norm, v_ffn2_w_gate, v_ffn2_w_up, v_ffn2_w_down, v_final_norm):
    given = dict(locals())
    wts = {k: given[k] for k in WEIGHTS}
    mom_m = {k: given["m_" + k] for k in WEIGHTS}
    mom_v = {k: given["v_" + k] for k in WEIGHTS}
    xt, target = x[0], loss_target[0]

    shard = {k: _as_stored(k, wts[k]) if k == "conv_dw" else _as_stored(k, wts[k]).astype(BF16)
             for k in WEIGHTS if k.endswith(FFN_W) or k in MID}
    w = {k: wts[k].reshape(1, -1) for k in SMALL_1024 + SMALL_512}
    w["pool_w"] = wts["pool_w"].astype(BF16)

    (h1, s1, p1, a1, w["ffn1_w_gate"], w["ffn1_w_up"]), ((w["ffn1_w_down"],),) = _ffn_up_gather(
        xt, w["ffn1_norm"], shard["ffn1_w_gate"], shard["ffn1_w_up"], "ffn1_up_gather",
        cargos=[Cargo("gather_slots", [shard["ffn1_w_down"]])])
    x1, (mid, (w["ffn2_w_down"],)) = _ffn_down(
        xt, a1, w["ffn1_w_down"], "ffn1_down",
        cargos=[Cargo("gather_chips", [shard[k] for k in MID]), Cargo("gather_slots", [shard["ffn2_w_down"]])])
    w["w_in"] = mid[0]
    w["conv_dw"] = mid[1].transpose(1, 0, 2).reshape(CONV_WIDTH + 1, D_CONV)
    w["conv_pw"] = mid[2].reshape(D_CONV, D_CONV)
    w["w_out"] = mid[3].reshape(D_CONV + D_POOL, D_MODEL)
    (x2, h2, proj, u1, u3, mixed, cat), ((w["ffn2_w_gate"], w["ffn2_w_up"]),) = _mix_fwd(
        x1, w["mix_norm"], w["w_in"], w["conv_dw"], w["conv_dw_b"], w["conv_ln_g"], w["conv_ln_b"], w["conv_pw"],
        w["pool_w"], w["pool_scale"], w["w_out"], "mix_fwd",
        cargos=[Cargo("gather_slots", [shard["ffn2_w_gate"], shard["ffn2_w_up"]])])
    x3, h3, s2, p2, a2 = _ffn_fwd(x2, w["ffn2_norm"], w["ffn2_w_gate"], w["ffn2_w_up"], w["ffn2_w_down"], "ffn2_fwd")
    dx3, loss_share, d_final = _final_norm_loss(x3, target, w["final_norm"], "final_norm_loss")

    g = {"final_norm": d_final}
    sums = {}

    def landed(names, parts):
        sums.update(zip(names, parts))

    dx2, g["ffn2_norm"], df2, dg2, du2 = _ffn_bwd(dx3, x2, w["ffn2_norm"], s2, p2, w["ffn2_w_gate"],
                                                   w["ffn2_w_up"], w["ffn2_w_down"], "ffn2_bwd")
    def ffn_wgrad(names, hids, tok, kernel_name, cargos=()):
        parts, cargo_outs = _wgrad_hid_tok_scatter(hids, tok, kernel_name, cargos=cargos)
        landed(names, parts)
        return cargo_outs

    ffn_wgrad(["ffn2_w_gate", "ffn2_w_up"], [dg2, du2], h3, "ffn2_dw_gate_up")
    ffn_wgrad(["ffn2_w_down"], [a2], df2, "ffn2_dw_down")
    (dx1, dproj, dco, dpo, g_dw, g["conv_dw_b"], g["conv_ln_g"], g["conv_ln_b"], g["pool_scale"],
     g["mix_norm"]), (swapped2,) = _mix_bwd(
        dx2, u1, mixed, proj, x1, w["mix_norm"], w["conv_dw"], w["conv_ln_g"], w["conv_ln_b"], w["conv_pw"],
        w["pool_w"], w["pool_scale"], w["w_out"], w["w_in"], "mix_bwd",
        cargos=[Cargo("swap", [sums["ffn2_" + k] for k in FFN_W])])
    g_out, _ = _wgrad_2d(cat, dx2, 1, BF16, "dw_out")
    g_pw, _ = _wgrad_2d(u3, dco, 1, BF16, "dconv_pw")
    g["pool_w"], _ = _wgrad_2d(mixed, dpo, 4, F32, "dpool_w", group_diag=True)
    slabs = [g_pw.reshape(N_CHIPS, D_CONV // N_CHIPS, D_CONV),
             g_out.reshape(N_CHIPS, (D_CONV + D_POOL) // N_CHIPS, D_MODEL)]
    g_in, (parts,) = _wgrad_2d(h2, dproj, N_CHIPS, BF16, "dw_in", cargos=[Cargo("scatter_chips", slabs)])
    landed(["conv_pw", "w_out"], parts)
    dx, g["ffn1_norm"], df1, dg1, du1_ = _ffn_bwd(dx1, xt, w["ffn1_norm"], s1, p1, w["ffn1_w_gate"],
                                                   w["ffn1_w_up"], w["ffn1_w_down"], "ffn1_bwd")
    slabs = [g_in, g_dw.reshape(CONV_WIDTH + 1, N_CHIPS, D_CONV // N_CHIPS).transpose(1, 0, 2)]
    (parts,) = ffn_wgrad(["ffn1_w_gate", "ffn1_w_up"], [dg1, du1_], h1, "ffn1_dw_gate_up",
                         cargos=[Cargo("scatter_chips", slabs)])
    landed(["w_in", "conv_dw"], parts)
    swapped_mid, swapped_gate_up, small_parts = ffn_wgrad(
        ["ffn1_w_down"], [a1], df1, "ffn1_dw_down",
        cargos=[Cargo("swap", [sums[k] for k in MID]), Cargo("swap", [sums["ffn1_w_gate"], sums["ffn1_w_up"]]),
                Cargo("gather_devices", [_pack_small(g, spare=loss_share)])])
    swapped_down = _exchange(Cargo("swap", [sums["ffn1_w_down"]]), "swap_last")

    theirs = dict(zip(["ffn2_" + k for k in FFN_W], swapped2))
    theirs.update(zip(MID, swapped_mid))
    theirs.update(ffn1_w_gate=swapped_gate_up[0], ffn1_w_up=swapped_gate_up[1], ffn1_w_down=swapped_down[0])
    grads, deltas, new_m, new_v = {}, {}, {}, {}
    for k in theirs:
        res = _adamw([sums[k], theirs[k]], _as_stored(k, wts[k]), _as_stored(k, mom_m[k]),
                     _as_stored(k, mom_v[k]), "adamw_" + k)
        grads[k], deltas[k], new_m[k], new_v[k] = [_as_given(k, t) for t in res]
    res = _adamw(small_parts, _pack_small(wts), _pack_small(mom_m), _pack_small(mom_v), "adamw_small")
    for dst, packed in zip((grads, deltas, new_m, new_v), res):
        dst.update(_unpack_small(packed))
    loss = res[0][PACK_LOSS_ROW, 0]

    out = [loss, dx[None]]
    for group in (grads, deltas, new_m, new_v):
        out += [group[k] for k in WEIGHTS]
    return tuple(out)
```

```python
import functools

import jax
import jax.numpy as jnp
from jax import lax
from jax.experimental import pallas as pl
from jax.experimental.pallas import tpu as pltpu

F32 = jnp.float32
BF16 = jnp.bfloat16
MESH = pl.DeviceIdType.MESH

N_CHIPS = 4
N_DEV = 8
D_MODEL = 1024
D_CONV = 512
D_POOL = 512
CONV_WIDTH = 31
POOL_WINDOWS = (2, 4, 8, 16)
POOL_GROUP = 128
D_IN = 2 * D_CONV + D_POOL
HALO = 32
RMS_EPS = 1e-6
LN_EPS = 1e-5
FFN_RES_WEIGHT = 0.5
ADAM_LR = 0.001
ADAM_B1 = 0.9
ADAM_B2 = 0.999
ADAM_EPS = 1e-08
ADAM_WD = 0.01
ADAM_STEP = 10
VMEM_LIMIT_BYTES = 52 * 1024 * 1024
VMEM_LIMIT_BYTES_BWD = 58 * 1024 * 1024
TM_FFN = 512
TM_MIX = 256
TT_WGRAD = 2048
STRIP = 16
SLOTS_PER_STEP = 2
SLOTS_PER_STEP_FWD = 4
SUBLANES = 8
RELAY_AT_EIGHTHS = 7

HBM = pl.BlockSpec(memory_space=pl.ANY)


def _dot(a, b):
    return jnp.dot(a, b, preferred_element_type=F32)


def _dot_nt(a, b):
    return lax.dot_general(a, b, (((1,), (1,)), ((), ())), preferred_element_type=F32)


def _dot_tn(a, b):
    return lax.dot_general(a, b, (((0,), (0,)), ((), ())), preferred_element_type=F32)


def _sds(shape, dtype):
    return jax.ShapeDtypeStruct(shape, dtype)


def _rms_stats(xv):
    r = lax.rsqrt(jnp.mean(xv * xv, axis=-1, keepdims=True) + RMS_EPS)
    return r, xv * r


def _swiglu_saved(gate, up):
    sig = jax.nn.sigmoid(gate)
    silu = gate * sig
    return silu, up * (sig * (1.0 + gate * (1.0 - sig))), silu * up


def _rms_bwd(dh, n, r, gain):
    dn = dh * gain
    return r * (dn - n * jnp.mean(dn * n, axis=-1, keepdims=True))


def _place():
    x, y, c = lax.axis_index("x"), lax.axis_index("y"), lax.axis_index("c")
    return x, y, c, [(1 - x, y), (x, 1 - y), (1 - x, 1 - y)]


class Cargo:
    def __init__(self, kind, arrays):
        self.kind, self.arrays = kind, list(arrays)
        n = len(self.arrays)
        self.two_level = kind in ("gather_slots", "gather_chips")
        if self.two_level:
            self.out_shape = [_sds((N_CHIPS,) + a.shape, a.dtype) for a in self.arrays]
        elif kind == "gather_devices":
            self.out_shape = [_sds((N_DEV,) + a.shape, a.dtype) for a in self.arrays]
        else:
            self.out_shape = [_sds(a.shape, a.dtype) for a in self.arrays]
        n_remote = n * {"swap": 1, "gather_devices": N_DEV - 1}.get(kind, N_CHIPS - 1)
        n_own = 0 if kind == "swap" else n
        n_relay = n_remote if self.two_level else 0
        dma = pltpu.SemaphoreType.DMA
        self.scratch = [dma((n_remote,)), dma((n_remote,)), dma((max(n_own, 1),)),
                        dma((max(n_relay, 1),)), dma((max(n_relay, 1),))]

    def _plan(self, ins, outs):
        x, y, c, chips = _place()
        q = 2 * x + y
        sibling = (x, y, 1 - c)
        own, remote, relays = [], [], []
        for a, o in zip(ins, outs):
            if self.two_level:
                half = a.shape[0] // 2
                mine = pl.ds(pl.multiple_of(c * half, SUBLANES), half)
                theirs = pl.ds(pl.multiple_of((1 - c) * half, SUBLANES), half)
                own.append((a, o.at[0 if self.kind == "gather_slots" else q]))
                for j, (px, py) in enumerate(chips):
                    there, here = (j + 1, j + 1) if self.kind == "gather_slots" else (q, 2 * px + py)
                    remote.append((a.at[mine], o.at[there, mine], o.at[here, mine], (px, py, c)))
                    relays.append((o.at[here, mine], o.at[here, mine], o.at[here, theirs], sibling))
            elif self.kind == "scatter_chips":
                own.append((a.at[q], o.at[q]))
                remote += [(a.at[2 * px + py], o.at[q], o.at[2 * px + py], (px, py, c)) for px, py in chips]
            elif self.kind == "swap":
                remote.append((a, o, o, sibling))
            else:
                own.append((a, o.at[4 * x + 2 * y + c]))
                for k in range(1, N_DEV):
                    px, py, pc = x ^ (k >> 2 & 1), y ^ (k >> 1 & 1), c ^ (k & 1)
                    remote.append((a, o.at[4 * x + 2 * y + c], o.at[4 * px + 2 * py + pc], (px, py, pc)))
        return own, remote, relays

    @staticmethod
    def _copies(entries, send_sems, recv_sems):
        out = []
        for k, (src, dst, landed, peer) in enumerate(entries):
            def make(dst_ref, k=k, src=src, peer=peer):
                return pltpu.make_async_remote_copy(src_ref=src, dst_ref=dst_ref, send_sem=send_sems.at[k],
                                                    recv_sem=recv_sems.at[k], device_id=peer, device_id_type=MESH)
            out.append((make(dst), make(landed)))
        return out

    def start(self, ins, outs, sems):
        own, remote, _ = self._plan(ins, outs)
        for k, (src, dst) in enumerate(own):
            pltpu.make_async_copy(src, dst, sems[2].at[k]).start()
        for mine, _ in self._copies(remote, sems[0], sems[1]):
            mine.start()

    def relay(self, ins, outs, sems):
        _, remote, relays = self._plan(ins, outs)
        passed = self._copies(relays, sems[3], sems[4])
        for (_, arriving), (mine, _) in zip(self._copies(remote, sems[0], sems[1]), passed):
            arriving.wait_recv()
            mine.start()

    def wait(self, ins, outs, sems):
        own, remote, relays = self._plan(ins, outs)
        for mine, arriving in self._copies(remote, sems[0], sems[1]):
            mine.wait_send()
            if not self.two_level:
                arriving.wait_recv()
        for mine, arriving in self._copies(relays, sems[3], sems[4]):
            mine.wait_send()
            arriving.wait_recv()
        for k, (src, dst) in enumerate(own):
            pltpu.make_async_copy(src, dst, sems[2].at[k]).wait()


N_CARGO_SEMS = 5


def _call(body, *, name, grid, in_specs, out_specs, out_shape, args, scratch_shapes=(), cargos=(),
          vmem_limit_bytes=VMEM_LIMIT_BYTES):
    n_in, n_out, n_scr = len(in_specs), len(out_specs), len(scratch_shapes)
    c_in = [len(cg.arrays) for cg in cargos]
    n_cin = sum(c_in)

    def wrapped(*refs):
        ins = refs[:n_in]
        cins = refs[n_in:n_in + n_cin]
        outs = refs[n_in + n_cin:n_in + n_cin + n_out]
        couts = refs[n_in + n_cin + n_out:n_in + 2 * n_cin + n_out]
        scr = refs[n_in + 2 * n_cin + n_out:n_in + 2 * n_cin + n_out + n_scr]
        sems = refs[n_in + 2 * n_cin + n_out + n_scr:]
        step, n_steps = 0, 1
        for ax, size in enumerate(grid):
            step = step * size + pl.program_id(ax)
            n_steps *= size

        def each(method, only_two_level=False):
            at = 0
            for k, cg in enumerate(cargos):
                if cg.two_level or not only_two_level:
                    getattr(cg, method)(cins[at:at + c_in[k]], couts[at:at + c_in[k]],
                                        sems[N_CARGO_SEMS * k:N_CARGO_SEMS * (k + 1)])
                at += c_in[k]

        body(*ins, *outs, *scr)
        if cargos:
            pl.when(step == 0)(lambda: each("start"))
        if any(cg.two_level for cg in cargos):
            pl.when(step == (RELAY_AT_EIGHTHS * n_steps) // 8)(lambda: each("relay", only_two_level=True))
        if cargos:
            pl.when(step == n_steps - 1)(lambda: each("wait"))

    res = pl.pallas_call(
        wrapped, name=name, grid=grid,
        in_specs=list(in_specs) + [HBM] * n_cin,
        out_specs=list(out_specs) + [HBM] * n_cin,
        out_shape=list(out_shape) + [s for cg in cargos for s in cg.out_shape],
        scratch_shapes=list(scratch_shapes) + [s for cg in cargos for s in cg.scratch],
        compiler_params=pltpu.CompilerParams(dimension_semantics=("arbitrary",) * len(grid),
                                             vmem_limit_bytes=vmem_limit_bytes),
    )(*args, *[a for cg in cargos for a in cg.arrays])
    outs, rest = list(res[:n_out]), list(res[n_out:])
    cargo_outs = []
    for k in c_in:
        cargo_outs.append(rest[:k])
        rest = rest[k:]
    return outs, cargo_outs


def _exchange(cargo, name):
    _, (outs,) = _call(lambda: None, name=name, grid=(1,), in_specs=[], out_specs=[], out_shape=[], args=[],
                       cargos=[cargo])
    return outs


def _ffn_up_gather(x, gain, wg_t, wu_t, name, cargos=()):
    t_len, d = x.shape
    fq = wg_t.shape[0]
    tm = min(TM_FFN, t_len)
    n_tiles = t_len // tm
    relay_tile = n_tiles // 2
    fetch_tile = min(relay_tile + 1, n_tiles - 1)

    def body(x_ref, g_ref, wg_in, wu_in, h_ref, s_ref, p_ref, a_ref, wg_all, wu_all,
             wg_v, wu_v, h_all, send_sems, recv_sems, pass_send_sems, pass_recv_sems, own_sems, load_sems):
        s = pl.program_id(0)
        i = pl.program_id(1)
        x_, y_, c_, chips = _place()
        shards = ((wg_in, wg_all, wg_v), (wu_in, wu_all, wu_v))
        mine = pl.ds(pl.multiple_of(c_ * (fq // 2), SUBLANES), fq // 2)
        theirs = pl.ds(pl.multiple_of((1 - c_) * (fq // 2), SUBLANES), fq // 2)

        def to_peer(k, j):
            w_in, w_all, _ = shards[k]
            return pltpu.make_async_remote_copy(
                src_ref=w_in.at[mine], dst_ref=w_all.at[j + 1, mine], send_sem=send_sems.at[3 * k + j],
                recv_sem=recv_sems.at[3 * k + j], device_id=(*chips[j], c_), device_id_type=MESH)

        def to_sibling(k, j, landing=False):
            w_all = shards[k][1]
            return pltpu.make_async_remote_copy(
                src_ref=w_all.at[j + 1, mine], dst_ref=w_all.at[j + 1, theirs if landing else mine],
                send_sem=pass_send_sems.at[3 * k + j], recv_sem=pass_recv_sems.at[3 * k + j],
                device_id=(x_, y_, 1 - c_), device_id_type=MESH)

        def keep(k):
            return pltpu.make_async_copy(shards[k][0], shards[k][1].at[0], own_sems.at[k])

        @pl.when((s == 0) & (i == 0))
        def _():
            for j in range(N_CHIPS - 1):
                for k in range(2):
                    to_peer(k, j).start()
            for k in range(2):
                keep(k).start()

        def load(k, slot):
            src = shards[k][0] if slot == 0 else shards[k][1].at[slot]
            return pltpu.make_async_copy(src, shards[k][2].at[slot % 2], load_sems.at[k])

        @pl.when((s == 0) & (i == 0))
        def _():
            for k in range(2):
                load(k, 0).start()
            for k in range(2):
                load(k, 0).wait()

        def pass_on(slot):
            for k in range(2):
                to_peer(k, slot - 1).wait_recv()
                to_sibling(k, slot - 1).start()

        def fetch(slot):
            for k in range(2):
                to_sibling(k, slot - 1, landing=True).wait_recv()
                load(k, slot).start()

        for slot in range(1, N_CHIPS):
            pl.when((s == slot - 1) & (i == relay_tile))(functools.partial(pass_on, slot))
            pl.when((s == slot - 1) & (i == fetch_tile))(functools.partial(fetch, slot))

            @pl.when((s == slot) & (i == 0))
            def _():
                for k in range(2):
                    load(k, slot).wait()

        @pl.when(s == 0)
        def _():
            _, n = _rms_stats(x_ref[...])
            h_new = (n * g_ref[...]).astype(BF16)
            h_ref[...] = h_new
            h_all[i] = h_new

        h = h_all[i]
        silu, dgate, act = _swiglu_saved(_dot_nt(h, wg_v[s % 2]), _dot_nt(h, wu_v[s % 2]))
        s_ref[...] = silu.astype(BF16)
        p_ref[...] = dgate.astype(BF16)
        a_ref[...] = act.astype(BF16)

        @pl.when((s == N_CHIPS - 1) & (i == n_tiles - 1))
        def _():
            for k in range(2):
                for j in range(N_CHIPS - 1):
                    to_peer(k, j).wait_send()
                    to_sibling(k, j).wait_send()
                keep(k).wait()

    tok = pl.BlockSpec((tm, d), lambda s, i: (jnp.where(s == 0, i, n_tiles - 1), 0))
    hid = pl.BlockSpec((None, tm, fq), lambda s, i: (s, i, 0))
    outs, cargo_outs = _call(
        body, name=name, grid=(N_CHIPS, n_tiles),
        in_specs=[tok, pl.BlockSpec((1, d), lambda s, i: (0, 0)), HBM, HBM],
        out_specs=[tok, hid, hid, hid, HBM, HBM],
        out_shape=[_sds((t_len, d), BF16)] + [_sds((N_CHIPS, t_len, fq), BF16)] * 3
        + [_sds((N_CHIPS, fq, d), BF16)] * 2,
        scratch_shapes=[pltpu.VMEM((2, fq, d), BF16), pltpu.VMEM((2, fq, d), BF16),
                        pltpu.VMEM((n_tiles, tm, d), BF16)]
        + [pltpu.SemaphoreType.DMA((6,))] * 4 + [pltpu.SemaphoreType.DMA((2,))] * 2,
        args=[x, gain, wg_t, wu_t], cargos=cargos)
    return outs, cargo_outs


def _load_once(hbm_refs, vmem_refs, sems, first):
    @pl.when(first)
    def _():
        copies = [pltpu.make_async_copy(src, dst, sems.at[k]) for k, (src, dst) in enumerate(zip(hbm_refs, vmem_refs))]
        for cp in copies:
            cp.start()
        for cp in copies:
            cp.wait()


def _ffn_down(x, act, wd, name, cargos=()):
    t_len, d = x.shape
    nq, fq, _ = wd.shape
    tm = min(TM_FFN, t_len)

    def body(x_ref, a_ref, wd_ref, xo_ref):
        y = _dot(a_ref[0], wd_ref[0])
        for j in range(1, nq):
            y = y + _dot(a_ref[j], wd_ref[j])
        xo_ref[...] = x_ref[...] + FFN_RES_WEIGHT * y

    tok = pl.BlockSpec((tm, d), lambda i: (i, 0))
    (xo,), cargo_outs = _call(
        body, name=name, grid=(t_len // tm,),
        in_specs=[tok, pl.BlockSpec((nq, tm, fq), lambda i: (0, i, 0)), pl.BlockSpec((nq, fq, d), lambda i: (0, 0, 0))],
        out_specs=[tok], out_shape=[_sds((t_len, d), F32)], args=[x, act, wd], cargos=cargos)
    return xo, cargo_outs


def _ffn_fwd(x, gain, wg_t, wu_t, wd, name):
    t_len, d = x.shape
    nq, fq, _ = wd.shape
    tm = min(TM_FFN, t_len)

    def body(x_ref, g_ref, wg_hbm, wu_hbm, wd_hbm, xo_ref, h_ref, s_ref, p_ref, a_ref,
             h_s, acc, wg_v, wu_v, wd_v, load_sems):
        i = pl.program_id(0)
        j = pl.program_id(1)
        _load_once((wg_hbm, wu_hbm, wd_hbm), (wg_v, wu_v, wd_v), load_sems, (i == 0) & (j == 0))

        @pl.when(j == 0)
        def _():
            _, n = _rms_stats(x_ref[...])
            h = (n * g_ref[...]).astype(BF16)
            h_s[...] = h
            h_ref[...] = h
            acc[...] = jnp.zeros_like(acc)

        h = h_s[...]
        y = None
        for jj in range(SLOTS_PER_STEP_FWD):
            slot = j * SLOTS_PER_STEP_FWD + jj
            silu, dgate, act = _swiglu_saved(_dot_nt(h, wg_v[slot]), _dot_nt(h, wu_v[slot]))
            s_ref[jj] = silu.astype(BF16)
            p_ref[jj] = dgate.astype(BF16)
            a_ref[jj] = act.astype(BF16)
            part = _dot(a_ref[jj], wd_v[slot])
            y = part if y is None else y + part
        acc[...] += y

        @pl.when(j == nq // SLOTS_PER_STEP_FWD - 1)
        def _():
            xo_ref[...] = x_ref[...] + FFN_RES_WEIGHT * acc[...]

    tok = pl.BlockSpec((tm, d), lambda i, j: (i, 0))
    hid = pl.BlockSpec((SLOTS_PER_STEP_FWD, tm, fq), lambda i, j: (j, i, 0))
    outs, _ = _call(
        body, name=name, grid=(t_len // tm, nq // SLOTS_PER_STEP_FWD),
        in_specs=[tok, pl.BlockSpec((1, d), lambda i, j: (0, 0)), HBM, HBM, HBM],
        out_specs=[tok, tok, hid, hid, hid],
        out_shape=[_sds((t_len, d), F32), _sds((t_len, d), BF16)] + [_sds((nq, t_len, fq), BF16)] * 3,
        scratch_shapes=[pltpu.VMEM((tm, d), BF16), pltpu.VMEM((tm, d), F32)]
        + [pltpu.VMEM((nq, fq, d), BF16)] * 3 + [pltpu.SemaphoreType.DMA((3,))],
        args=[x, gain, wg_t, wu_t, wd], vmem_limit_bytes=VMEM_LIMIT_BYTES_BWD)
    return outs


def _ffn_bwd(dy, x_in, gain, silu, dgate_du, wg_t, wu_t, wd, name):
    t_len, d = dy.shape
    nq, fq, _ = wd.shape
    tm = min(TM_FFN, t_len)

    def body(dy_ref, x_ref, g_ref, s_ref, p_ref, wg_hbm, wu_hbm, wd_hbm,
             dx_ref, dgain_ref, df_ref, dg_ref, du_ref, df_s, dh_acc, dact_s, wg_v, wu_v, wd_v, load_sems):
        i = pl.program_id(0)
        j = pl.program_id(1)
        _load_once((wg_hbm, wu_hbm, wd_hbm), (wg_v, wu_v, wd_v), load_sems, (i == 0) & (j == 0))

        @pl.when((i == 0) & (j == 0))
        def _():
            dgain_ref[...] = jnp.zeros_like(dgain_ref)

        @pl.when(j == 0)
        def _():
            df = (FFN_RES_WEIGHT * dy_ref[...]).astype(BF16)
            df_s[...] = df
            df_ref[...] = df
            dh_acc[...] = jnp.zeros_like(dh_acc)

        slots = [j * SLOTS_PER_STEP + jj for jj in range(SLOTS_PER_STEP)]
        for jj, slot in enumerate(slots):
            dact_s[jj] = _dot_nt(df_s[...], wd_v[slot])

        for jj in range(SLOTS_PER_STEP):
            for r0 in range(0, tm, STRIP):
                rows = slice(r0, r0 + STRIP)
                dact = dact_s[jj, rows, :]
                dg_ref[jj, rows, :] = (dact * p_ref[jj, rows, :].astype(F32)).astype(BF16)
                du_ref[jj, rows, :] = (dact * s_ref[jj, rows, :].astype(F32)).astype(BF16)

        dh = None
        for jj, slot in enumerate(slots):
            part = _dot(dg_ref[jj], wg_v[slot]) + _dot(du_ref[jj], wu_v[slot])
            dh = part if dh is None else dh + part
        dh_acc[...] += dh

        @pl.when(j == nq // SLOTS_PER_STEP - 1)
        def _():
            r, n = _rms_stats(x_ref[...])
            dh = dh_acc[...]
            dgain_ref[...] += jnp.sum(dh * n, axis=0, keepdims=True)
            dx_ref[...] = dy_ref[...] + _rms_bwd(dh, n, r, g_ref[...])

    tok = pl.BlockSpec((tm, d), lambda i, j: (i, 0))
    vec = pl.BlockSpec((1, d), lambda i, j: (0, 0))
    hid = pl.BlockSpec((SLOTS_PER_STEP, tm, fq), lambda i, j: (j, i, 0))
    outs, _ = _call(
        body, name=name, grid=(t_len // tm, nq // SLOTS_PER_STEP),
        in_specs=[tok, tok, vec, hid, hid, HBM, HBM, HBM],
        out_specs=[tok, vec, tok, hid, hid],
        out_shape=[_sds((t_len, d), F32), _sds((1, d), F32), _sds((t_len, d), BF16),
                   _sds((nq, t_len, fq), BF16), _sds((nq, t_len, fq), BF16)],
        scratch_shapes=[pltpu.VMEM((tm, d), BF16), pltpu.VMEM((tm, d), F32),
                        pltpu.VMEM((SLOTS_PER_STEP, tm, fq), F32)]
        + [pltpu.VMEM((nq, fq, d), BF16)] * 3 + [pltpu.SemaphoreType.DMA((3,))],
        args=[dy, x_in, gain, silu, dgate_du, wg_t, wu_t, wd], vmem_limit_bytes=VMEM_LIMIT_BYTES_BWD)
    return outs


def _wgrad(lhs, rhs, l_spec, r_spec, out_shape, out_spec, acc_shape, grid, name, cargos=()):
    n_t = grid[-1]
    t_axis = len(grid) - 1

    def body(l_ref, r_ref, o_ref, acc):
        t = pl.program_id(t_axis)

        @pl.when(t == 0)
        def _():
            acc[...] = jnp.zeros_like(acc)

        acc[...] += _dot_tn(l_ref[...].astype(BF16), r_ref[...].astype(BF16))

        @pl.when(t == n_t - 1)
        def _():
            o_ref[...] = acc[...].astype(o_ref.dtype)

    (out,), cargo_outs = _call(
        body, name=name, grid=grid, in_specs=[l_spec, r_spec], out_specs=[out_spec], out_shape=[out_shape],
        scratch_shapes=[pltpu.VMEM(acc_shape, F32)], args=[lhs, rhs], cargos=cargos)
    return out, cargo_outs


def _wgrad_hid_tok_scatter(hids, tok, name, cargos=()):
    t_len, d = tok.shape
    n_w = len(hids)
    nq, _, fq = hids[0].shape
    half = fq // 2
    tt = min(TT_WGRAD, t_len)
    n_t = t_len // tt
    per_w = 4
    n_sem = 6

    def body(*refs):
        l_refs, r_ref, parts_refs = refs[:n_w], refs[n_w], refs[n_w + 1:2 * n_w + 1]
        scr = refs[2 * n_w + 1:]
        bufs = [scr[per_w * w:per_w * (w + 1)] for w in range(n_w)]
        zeros = scr[per_w * n_w]
        sems = [scr[per_w * n_w + 1 + n_sem * w:per_w * n_w + 1 + n_sem * (w + 1)] for w in range(n_w)]
        g = pl.program_id(0)
        t = pl.program_id(1)
        x_, y_, c_, chips = _place()
        mine = pl.ds(pl.multiple_of(c_ * half, STRIP), half)
        theirs = pl.ds(pl.multiple_of((1 - c_) * half, STRIP), half)

        def to_sibling(w, slot):
            return pltpu.make_async_remote_copy(
                src_ref=bufs[w][1].at[theirs], dst_ref=bufs[w][2].at[slot], send_sem=sems[w][0].at[slot],
                recv_sem=sems[w][1].at[slot], device_id=(x_, y_, 1 - c_), device_id_type=MESH)

        def to_peer(w, j):
            return pltpu.make_async_remote_copy(
                src_ref=bufs[w][3].at[j + 1], dst_ref=parts_refs[w].at[j + 1, mine], send_sem=sems[w][2].at[j],
                recv_sem=sems[w][3].at[j], device_id=(*chips[j], c_), device_id_type=MESH)

        def keep(w):
            return pltpu.make_async_copy(bufs[w][3].at[0], parts_refs[w].at[0, mine], sems[w][4])

        def blank(w, slot):
            return pltpu.make_async_copy(zeros, parts_refs[w].at[slot, theirs], sems[w][5].at[slot])

        @pl.when((g == 0) & (t == 0))
        def _():
            zeros[...] = jnp.zeros_like(zeros)
            for w in range(n_w):
                for slot in range(nq):
                    blank(w, slot).start()

        @pl.when(t == 0)
        def _():
            for w in range(n_w):
                bufs[w][0][...] = jnp.zeros_like(bufs[w][0])

        rhs = r_ref[...]
        for w in range(n_w):
            bufs[w][0][...] += _dot_tn(l_refs[w][...], rhs)

        for step in range(nq):
            slot = (step + 1) % nq

            @pl.when((g == step) & (t == n_t - 1))
            def _():
                for w in range(n_w):
                    acc, stage, _, _ = bufs[w]
                    if step > 0:
                        to_sibling(w, step).wait_send()
                    stage[...] = acc[...].astype(BF16)
                    to_sibling(w, slot).start()
                for w in range(n_w):
                    _, stage, pair, summed = bufs[w]
                    to_sibling(w, slot).wait_recv()
                    summed[slot] = (stage[mine, :].astype(F32) + pair[slot].astype(F32)).astype(BF16)
                    if slot > 0:
                        to_peer(w, slot - 1).start()
                    else:
                        keep(w).start()

        @pl.when((g == nq - 1) & (t == n_t - 1))
        def _():
            for w in range(n_w):
                for j in range(N_CHIPS - 1):
                    to_peer(w, j).wait()
                keep(w).wait()
                to_sibling(w, 0).wait_send()
                for slot in range(nq):
                    blank(w, slot).wait()

    dma = pltpu.SemaphoreType.DMA
    scratch = []
    for _ in range(n_w):
        scratch += [pltpu.VMEM((fq, d), F32), pltpu.VMEM((fq, d), BF16), pltpu.VMEM((nq, half, d), BF16),
                    pltpu.VMEM((nq, half, d), BF16)]
    scratch.append(pltpu.VMEM((half, d), BF16))
    for _ in range(n_w):
        scratch += [dma((nq,)), dma((nq,)), dma((N_CHIPS - 1,)), dma((N_CHIPS - 1,)), dma(()), dma((nq,))]
    parts, cargo_outs = _call(
        body, name=name, grid=(nq, n_t),
        in_specs=[pl.BlockSpec((None, tt, fq), lambda g, t: ((g + 1) % nq, t, 0))] * n_w
        + [pl.BlockSpec((tt, d), lambda g, t: (t, 0))],
        out_specs=[HBM] * n_w, out_shape=[_sds((nq, fq, d), BF16)] * n_w,
        scratch_shapes=scratch, args=[*hids, tok], cargos=cargos)
    return parts, cargo_outs


def _wgrad_2d(lhs, rhs, n_col_blocks, out_dtype, name, group_diag=False, cargos=()):
    t_len, k = lhs.shape
    n = rhs.shape[1]
    nb = n // n_col_blocks
    kb = k // n_col_blocks if group_diag else k
    tt = min(TT_WGRAD, t_len)
    l_map = (lambda q, t: (t, q)) if group_diag else (lambda q, t: (t, 0))
    return _wgrad(lhs, rhs,
                  pl.BlockSpec((tt, kb), l_map),
                  pl.BlockSpec((tt, nb), lambda q, t: (t, q)),
                  _sds((n_col_blocks, kb, nb), out_dtype),
                  pl.BlockSpec((None, kb, nb), lambda q, t: (q, 0, 0)),
                  (kb, nb), (n_col_blocks, t_len // tt), name, cargos)


def _layernorm_stats(u1):
    mu = jnp.mean(u1, axis=-1, keepdims=True)
    xc = u1 - mu
    rstd = lax.rsqrt(jnp.mean(xc * xc, axis=-1, keepdims=True) + LN_EPS)
    return rstd, xc * rstd


def _positions(i, tm, rows, offset=0):
    return (lax.broadcasted_iota(jnp.int32, (rows, 1), 0) + (i * tm + offset)).astype(F32)


SHIFT_ROWS = HALO - SUBLANES


def _fill_shifted(ext_s, sh_s, tm):
    for b in range(1, SUBLANES):
        sh_s[b - 1] = ext_s[pl.ds(b, tm + SHIFT_ROWS), :]


def _window(ext_s, sh_s, shift, tm):
    a, b = divmod(shift, SUBLANES)
    if b == 0:
        return ext_s[pl.ds(shift, tm), :]
    return sh_s[b - 1, pl.ds(a * SUBLANES, tm), :]


def _window_sums(ext_s, lv_a, lv_b, tm, ahead):
    g = POOL_GROUP
    sign = 1 if ahead else -1
    for n, (dst, src, c0) in enumerate(((lv_a, ext_s, 0), (lv_b, lv_a, g), (lv_a, lv_b, 2 * g)), start=1):
        lo = 0 if ahead else n * SUBLANES
        rows = tm + HALO - n * SUBLANES
        shift = sign * 2 ** (n - 1)
        dst[pl.ds(lo, rows), c0:] = src[pl.ds(lo, rows), c0:] + src[pl.ds(lo + shift, rows), c0:]
    base = 0 if ahead else HALO
    rows = pl.ds(base, tm)
    far = pl.ds(base + sign * SUBLANES, tm)
    return [lv_a[rows, 0:g], lv_b[rows, g:2 * g], lv_a[rows, 2 * g:3 * g],
            lv_a[rows, 3 * g:] + lv_a[far, 3 * g:]]


def _tile(tm, cols):
    return pl.BlockSpec((tm, cols), lambda i: (i, 0))


def _whole(shape):
    return pl.BlockSpec(shape, lambda i: (0,) * len(shape))


def _mix_fwd(x1, gain, w_in, conv_dw, conv_b, ln_g, ln_b, conv_pw, pool_w, pool_scale, w_out, name, cargos=()):
    t_len, d = x1.shape
    nq, _, nb = w_in.shape
    tm = min(TM_MIX, t_len)

    def body(x_ref, g_ref, wi_ref, dw_ref, cb_ref, lg_ref, lb_ref, pw_ref, plw_ref, ps_ref, wo_ref,
             x2_ref, h_ref, p_ref, u1_ref, u3_ref, mx_ref, cat_ref, ext_s, pext_s, sh_s, tail_s, lva_s, lvb_s):
        i = pl.program_id(0)

        @pl.when(i == 0)
        def _():
            tail_s[...] = jnp.zeros_like(tail_s)

        _, n = _rms_stats(x_ref[...])
        h = (n * g_ref[...]).astype(BF16)
        h_ref[...] = h
        for q in range(nq):
            p_ref[:, q * nb:(q + 1) * nb] = _dot(h, wi_ref[q])

        a = p_ref[:, 0:D_CONV]
        g = p_ref[:, D_CONV:2 * D_CONV]
        p = p_ref[:, 2 * D_CONV:]
        ext_s[0:HALO, :] = tail_s[:, 0:D_CONV] * jax.nn.sigmoid(tail_s[:, D_CONV:2 * D_CONV])
        ext_s[HALO:, :] = a * jax.nn.sigmoid(g)
        pext_s[0:HALO, :] = tail_s[:, 2 * D_CONV:]
        pext_s[HALO:, :] = p
        tail_s[...] = p_ref[tm - HALO:tm, :]

        _fill_shifted(ext_s, sh_s, tm)
        u1 = jnp.broadcast_to(cb_ref[...], (tm, D_CONV))
        for k in range(CONV_WIDTH):
            u1 = u1 + dw_ref[k:k + 1, :] * _window(ext_s, sh_s, HALO - (CONV_WIDTH - 1) + k, tm)
        u1_ref[...] = u1
        _, nhat = _layernorm_stats(u1)
        u2 = nhat * lg_ref[...] + lb_ref[...]
        u3 = (u2 * jax.nn.sigmoid(u2)).astype(BF16)
        u3_ref[...] = u3
        cat_ref[:, 0:D_CONV] = _dot(u3, pw_ref[...]).astype(BF16)

        pos1 = _positions(i, tm, tm) + 1.0
        sums = _window_sums(pext_s, lva_s, lvb_s, tm, ahead=False)
        for gi, w in enumerate(POOL_WINDOWS):
            cols = slice(gi * POOL_GROUP, (gi + 1) * POOL_GROUP)
            mixed = (sums[gi] / jnp.minimum(pos1, float(w)) - p[:, cols]).astype(BF16)
            mx_ref[:, cols] = mixed
            out = _dot(mixed, plw_ref[gi]) * ps_ref[:, cols]
            cat_ref[:, D_CONV + gi * POOL_GROUP:D_CONV + (gi + 1) * POOL_GROUP] = out.astype(BF16)

        x2_ref[...] = x_ref[...] + _dot(cat_ref[...], wo_ref[...])

    return _call(
        body, name=name, grid=(t_len // tm,),
        in_specs=[_tile(tm, d), _whole((1, d)), _whole((nq, d, nb)), _whole((CONV_WIDTH + 1, D_CONV)),
                  _whole((1, D_CONV)), _whole((1, D_CONV)), _whole((1, D_CONV)), _whole((D_CONV, D_CONV)),
                  _whole((4, POOL_GROUP, POOL_GROUP)), _whole((1, D_POOL)), _whole((D_CONV + D_POOL, d))],
        out_specs=[_tile(tm, d), _tile(tm, d), _tile(tm, D_IN), _tile(tm, D_CONV), _tile(tm, D_CONV),
                   _tile(tm, D_POOL), _tile(tm, D_CONV + D_POOL)],
        out_shape=[_sds((t_len, d), F32), _sds((t_len, d), BF16), _sds((t_len, D_IN), F32),
                   _sds((t_len, D_CONV), F32), _sds((t_len, D_CONV), BF16), _sds((t_len, D_POOL), BF16),
                   _sds((t_len, D_CONV + D_POOL), BF16)],
        scratch_shapes=[pltpu.VMEM((tm + HALO, D_CONV), F32), pltpu.VMEM((tm + HALO, D_POOL), F32),
                        pltpu.VMEM((SUBLANES - 1, tm + SHIFT_ROWS, D_CONV), F32), pltpu.VMEM((HALO, D_IN), F32)]
        + [pltpu.VMEM((tm + HALO, D_POOL), F32)] * 2,
        args=[x1, gain, w_in, conv_dw, conv_b, ln_g, ln_b, conv_pw, pool_w, pool_scale, w_out], cargos=cargos)


def _mix_bwd(dx2, u1, u3, mixed, proj, x1, gain, conv_dw, ln_g, ln_b, conv_pw, pool_w, pool_scale, w_out, w_in,
             name, cargos=()):
    t_len, d = x1.shape
    nq, _, nb = w_in.shape
    tm = min(TM_MIX, t_len)
    hb = tm // HALO
    n_tiles = t_len // tm

    def body(dxn_ref, u1_ref, u3_ref, mx_ref, p_ref, tail_ref, x_ref, dx2_ref, g_ref, dw_ref, lg_ref, lb_ref, pw_ref,
             plw_ref, ps_ref, wo_ref, wi_ref,
             dx1_ref, dp_ref, dpw_ref, dplw_ref, ddw_ref, dcb_ref, dlg_ref, dlb_ref, dps_ref, dgain_ref,
             du_s, dm_s, uext_s, dext_s, mext_s, ush_s, dsh_s, lva_s, lvb_s):
        k = pl.program_id(0)

        @pl.when(k == 0)
        def _():
            for ref in (dpw_ref, dplw_ref, ddw_ref, dcb_ref, dlg_ref, dlb_ref, dps_ref, dgain_ref, du_s, dm_s):
                ref[...] = jnp.zeros_like(ref)

        counts = jnp.where(k < n_tiles, 1.0, 0.0)
        dcat = _dot_nt(dxn_ref[...].astype(BF16), wo_ref[...])
        dco = dcat[:, 0:D_CONV].astype(BF16)
        dpw_ref[...] += _dot_tn(u3_ref[...], (dcat[:, 0:D_CONV] * counts).astype(BF16))
        du3 = _dot_nt(dco, pw_ref[...])
        rstd, nhat = _layernorm_stats(u1_ref[...])
        u2 = nhat * lg_ref[...] + lb_ref[...]
        sig = jax.nn.sigmoid(u2)
        du2 = du3 * (sig * (1.0 + u2 * (1.0 - sig)))
        dlg_ref[...] += counts * jnp.sum(du2 * nhat, axis=0, keepdims=True)
        dlb_ref[...] += counts * jnp.sum(du2, axis=0, keepdims=True)
        dnhat = du2 * lg_ref[...]
        du_s[k % 2] = rstd * (dnhat - jnp.mean(dnhat, axis=-1, keepdims=True)
                              - nhat * jnp.mean(dnhat * nhat, axis=-1, keepdims=True))
        for gi in range(len(POOL_WINDOWS)):
            cols = slice(gi * POOL_GROUP, (gi + 1) * POOL_GROUP)
            dpo = dcat[:, D_CONV + gi * POOL_GROUP:D_CONV + (gi + 1) * POOL_GROUP]
            pre = _dot(mx_ref[:, cols], plw_ref[gi])
            dps_ref[:, cols] += counts * jnp.sum(dpo * pre, axis=0, keepdims=True)
            dout = dpo * ps_ref[:, cols]
            dplw_ref[gi] += _dot_tn(mx_ref[:, cols], (dout * counts).astype(BF16))
            dm_s[k % 2, :, cols] = _dot_nt(dout.astype(BF16), plw_ref[gi])

        i = jnp.maximum(k - 1, 0)
        cur, nxt = (k + 1) % 2, k % 2
        first = k <= 1
        last = (k == n_tiles) | (k == 0)
        a = p_ref[:, 0:D_CONV]
        g = p_ref[:, D_CONV:2 * D_CONV]
        sg = jax.nn.sigmoid(g)
        ta = tail_ref[:, 0:D_CONV]
        tg = tail_ref[:, D_CONV:2 * D_CONV]
        uext_s[0:HALO, :] = jnp.where(first, 0.0, ta * jax.nn.sigmoid(tg))
        uext_s[HALO:, :] = a * sg
        du1 = du_s[cur]
        dext_s[0:tm, :] = du1
        dext_s[tm:, :] = jnp.where(last, 0.0, du_s[nxt, 0:HALO, :])

        _fill_shifted(uext_s, ush_s, tm)
        _fill_shifted(dext_s, dsh_s, tm)
        du0 = jnp.zeros((tm, D_CONV), F32)
        for tap in range(CONV_WIDTH):
            du0 = du0 + dw_ref[tap:tap + 1, :] * _window(dext_s, dsh_s, CONV_WIDTH - 1 - tap, tm)
            ddw_ref[tap:tap + 1, :] += jnp.sum(
                du1 * _window(uext_s, ush_s, HALO - (CONV_WIDTH - 1) + tap, tm), axis=0, keepdims=True)
        dcb_ref[...] += jnp.sum(du1, axis=0, keepdims=True)
        dp_ref[:, 0:D_CONV] = (du0 * sg).astype(BF16)
        dp_ref[:, D_CONV:2 * D_CONV] = (du0 * a * sg * (1.0 - sg)).astype(BF16)

        pos1 = _positions(i, tm, tm) + 1.0
        pos1_next = _positions(i, tm, HALO, offset=tm) + 1.0
        for gi, w in enumerate(POOL_WINDOWS):
            cols = slice(gi * POOL_GROUP, (gi + 1) * POOL_GROUP)
            dm = dm_s[cur, :, cols]
            mext_s[0:tm, cols] = dm / jnp.minimum(pos1, float(w))
            mext_s[tm:, cols] = jnp.where(last, 0.0, dm_s[nxt, 0:HALO, cols] / jnp.minimum(pos1_next, float(w)))
        sums = _window_sums(mext_s, lva_s, lvb_s, tm, ahead=True)
        for gi in range(len(POOL_WINDOWS)):
            cols = slice(gi * POOL_GROUP, (gi + 1) * POOL_GROUP)
            dp_ref[:, 2 * D_CONV + gi * POOL_GROUP:2 * D_CONV + (gi + 1) * POOL_GROUP] = (
                sums[gi] - dm_s[cur, :, cols]).astype(BF16)

        dh = _dot_nt(dp_ref[:, 0:nb], wi_ref[0])
        for q in range(1, nq):
            dh = dh + _dot_nt(dp_ref[:, q * nb:(q + 1) * nb], wi_ref[q])
        r, n = _rms_stats(x_ref[...])
        dgain_ref[...] += jnp.sum(dh * n, axis=0, keepdims=True)
        dx1_ref[...] = dx2_ref[...] + _rms_bwd(dh, n, r, g_ref[...])

    def ahead(cols):
        return pl.BlockSpec((tm, cols), lambda k: (jnp.minimum(k, n_tiles - 1), 0))

    def behind(cols):
        return pl.BlockSpec((tm, cols), lambda k: (jnp.maximum(k - 1, 0), 0))

    vec = _whole((1, D_CONV))
    return _call(
        body, name=name, grid=(n_tiles + 1,),
        in_specs=[ahead(d), ahead(D_CONV), ahead(D_CONV), ahead(D_POOL), behind(D_IN),
                  pl.BlockSpec((HALO, D_IN), lambda k: (jnp.maximum(jnp.maximum(k - 1, 0) * hb - 1, 0), 0)),
                  behind(d), behind(d), _whole((1, d)), _whole((CONV_WIDTH + 1, D_CONV)), vec, vec,
                  _whole((D_CONV, D_CONV)), _whole((4, POOL_GROUP, POOL_GROUP)), vec,
                  _whole((D_CONV + D_POOL, d)), _whole((nq, d, nb))],
        out_specs=[behind(d), behind(D_IN), _whole((D_CONV, D_CONV)), _whole((4, POOL_GROUP, POOL_GROUP)),
                   _whole((CONV_WIDTH + 1, D_CONV)), vec, vec, vec, vec, _whole((1, d))],
        out_shape=[_sds((t_len, d), F32), _sds((t_len, D_IN), BF16), _sds((D_CONV, D_CONV), F32),
                   _sds((4, POOL_GROUP, POOL_GROUP), F32), _sds((CONV_WIDTH + 1, D_CONV), F32), _sds((1, D_CONV), F32),
                   _sds((1, D_CONV), F32), _sds((1, D_CONV), F32), _sds((1, D_POOL), F32), _sds((1, d), F32)],
        scratch_shapes=[pltpu.VMEM((2, tm, D_CONV), F32), pltpu.VMEM((2, tm, D_POOL), F32),
                        pltpu.VMEM((tm + HALO, D_CONV), F32), pltpu.VMEM((tm + HALO, D_CONV), F32),
                        pltpu.VMEM((tm + HALO, D_POOL), F32),
                        pltpu.VMEM((SUBLANES - 1, tm + SHIFT_ROWS, D_CONV), F32),
                        pltpu.VMEM((SUBLANES - 1, tm + SHIFT_ROWS, D_CONV), F32)]
        + [pltpu.VMEM((tm + HALO, D_POOL), F32)] * 2,
        args=[dx2, u1, u3, mixed, proj, proj, x1, dx2, gain, conv_dw, ln_g, ln_b, conv_pw, pool_w, pool_scale,
              w_out, w_in], cargos=cargos)


def _final_norm_loss(x3, target, gain, name):
    t_len, d = x3.shape
    tm = min(2 * TM_FFN, t_len)

    def body(x_ref, t_ref, g_ref, dx_ref, loss_ref, dgain_ref):
        @pl.when(pl.program_id(0) == 0)
        def _():
            loss_ref[...] = jnp.zeros_like(loss_ref)
            dgain_ref[...] = jnp.zeros_like(dgain_ref)

        r, n = _rms_stats(x_ref[...])
        err = n * g_ref[...] - t_ref[...]
        per_tok = jnp.sum(err * err, axis=-1, keepdims=True) * (1.0 / d)
        loss_ref[...] += 0.5 * jnp.sum(per_tok, axis=0, keepdims=True)
        dy = err * (1.0 / d)
        dgain_ref[...] += jnp.sum(dy * n, axis=0, keepdims=True)
        dx_ref[...] = _rms_bwd(dy, n, r, g_ref[...])

    tok = pl.BlockSpec((tm, d), lambda i: (i, 0))
    outs, _ = _call(
        body, name=name, grid=(t_len // tm,),
        in_specs=[tok, tok, pl.BlockSpec((1, d), lambda i: (0, 0))],
        out_specs=[tok, pl.BlockSpec((1, 128), lambda i: (0, 0)), pl.BlockSpec((1, d), lambda i: (0, 0))],
        out_shape=[_sds((t_len, d), F32), _sds((1, 128), F32), _sds((1, d), F32)],
        args=[x3, target, gain])
    return outs


def _row_tile(rows):
    return rows // 4 if rows % 64 == 0 else rows


def _adamw_math(w, g, m, v):
    m = ADAM_B1 * m + (1.0 - ADAM_B1) * g
    v = ADAM_B2 * v + (1.0 - ADAM_B2) * (g * g)
    m_hat = m / (1.0 - ADAM_B1 ** ADAM_STEP)
    v_hat = v / (1.0 - ADAM_B2 ** ADAM_STEP)
    delta = -ADAM_LR * (m_hat / (jnp.sqrt(v_hat) + ADAM_EPS) + ADAM_WD * w)
    return delta, m, v


def _adamw(parts, w, m, v, name):
    r, c = w.shape
    n = len(parts)
    tr = _row_tile(r)

    def body(*refs):
        g = None
        for p_ref in refs[:n]:
            s = p_ref[0].astype(F32)
            for k in range(1, p_ref.shape[0]):
                s = s + p_ref[k].astype(F32)
            g = s if g is None else g + s
        w_ref, m_ref, v_ref, g_out, d_out, m_out, v_out = refs[n:]
        delta, nm, nv = _adamw_math(w_ref[...], g, m_ref[...], v_ref[...])
        g_out[...] = g
        d_out[...] = delta
        m_out[...] = nm
        v_out[...] = nv

    blk = pl.BlockSpec((tr, c), lambda i: (i, 0))
    p_specs = [pl.BlockSpec((p.shape[0], tr, c), lambda i: (0, i, 0)) for p in parts]
    outs, _ = _call(body, name=name, grid=(r // tr,), in_specs=p_specs + [blk, blk, blk],
                    out_specs=[blk] * 4, out_shape=[_sds((r, c), F32)] * 4, args=[*parts, w, m, v])
    return outs


FFN_W = ("w_gate", "w_up", "w_down")
MID = ("w_in", "conv_dw", "conv_pw", "w_out")
SMALL_1024 = ("ffn1_norm", "mix_norm", "ffn2_norm", "final_norm")
SMALL_512 = ("conv_dw_b", "conv_ln_g", "conv_ln_b", "pool_scale")
WEIGHTS = ("ffn1_norm", "ffn1_w_gate", "ffn1_w_up", "ffn1_w_down", "mix_norm", "w_in", "conv_dw", "conv_dw_b",
           "conv_ln_g", "conv_ln_b", "conv_pw", "pool_w", "pool_scale", "w_out", "ffn2_norm", "ffn2_w_gate",
           "ffn2_w_up", "ffn2_w_down", "final_norm")
PACK_ROWS = 72
PACK_LOSS_ROW = 70


def _pad_rows(a, rows):
    return jnp.pad(a, ((0, rows - a.shape[0]), (0, 0)))


def _pack_small(t, spare=None):
    rows = [t[k].reshape(1, D_MODEL) for k in SMALL_1024]
    rows.append(jnp.concatenate([t["conv_dw_b"].reshape(1, -1), t["conv_ln_g"].reshape(1, -1)], axis=1))
    rows.append(jnp.concatenate([t["conv_ln_b"].reshape(1, -1), t["pool_scale"].reshape(1, -1)], axis=1))
    rows.append(t["pool_w"].reshape(64, D_MODEL))
    if spare is not None:
        rows.append(jnp.pad(spare, ((0, 0), (0, D_MODEL - spare.shape[1]))))
    return _pad_rows(jnp.concatenate(rows, axis=0), PACK_ROWS)


def _unpack_small(p):
    out = {k: p[i] for i, k in enumerate(SMALL_1024)}
    out["conv_dw_b"], out["conv_ln_g"] = p[4, :D_CONV], p[4, D_CONV:]
    out["conv_ln_b"], out["pool_scale"] = p[5, :D_CONV], p[5, D_CONV:]
    out["pool_w"] = p[6:70].reshape(4, POOL_GROUP, POOL_GROUP)
    return out


def _as_stored(name, a):
    if name.endswith(("w_gate", "w_up")):
        return a.T
    if name == "conv_dw":
        return _pad_rows(a, CONV_WIDTH + 1)
    return a


def _as_given(name, a):
    if name.endswith(("w_gate", "w_up")):
        return a.T
    if name == "conv_dw":
        return a[:CONV_WIDTH]
    return a


def kernel(x, ffn1_norm, ffn1_w_gate, ffn1_w_up, ffn1_w_down, mix_norm, w_in, conv_dw, conv_dw_b, conv_ln_g, conv_ln_b, conv_pw, pool_w, pool_scale, w_out, ffn2_norm, ffn2_w_gate, ffn2_w_up, ffn2_w_down, final_norm, loss_target, m_ffn1_norm, m_ffn1_w_gate, m_ffn1_w_up, m_ffn1_w_down, m_mix_norm, m_w_in, m_conv_dw, m_conv_dw_b, m_conv_ln_g, m_conv_ln_b, m_conv_pw, m_pool_w, m_pool_scale, m_w_out, m_ffn2_norm, m_ffn2_w_gate, m_ffn2_w_up, m_ffn2_w_down, m_final_norm, v_ffn1_norm, v_ffn1_w_gate, v_ffn1_w_up, v_ffn1_w_down, v_mix_norm, v_w_in, v_conv_dw, v_conv_dw_b, v_conv_ln_g, v_conv_ln_b, v_conv_pw, v_pool_w, v_pool_scale, v_w_out, v_ffn2_norm, v_ffn2_w_gate, v_ffn2_w_up, v_ffn2_w_down, v_final_norm):
    given = dict(locals())
    wts = {k: given[k] for k in WEIGHTS}
    mom_m = {k: given["m_" + k] for k in WEIGHTS}
    mom_v = {k: given["v_" + k] for k in WEIGHTS}
    xt, target = x[0], loss_target[0]

    shard = {k: _as_stored(k, wts[k]) if k == "conv_dw" else _as_stored(k, wts[k]).astype(BF16)
             for k in WEIGHTS if k.endswith(FFN_W) or k in MID}
    w = {k: wts[k].reshape(1, -1) for k in SMALL_1024 + SMALL_512}
    w["pool_w"] = wts["pool_w"].astype(BF16)

    (h1, s1, p1, a1, w["ffn1_w_gate"], w["ffn1_w_up"]), ((w["ffn1_w_down"],),) = _ffn_up_gather(
        xt, w["ffn1_norm"], shard["ffn1_w_gate"], shard["ffn1_w_up"], "ffn1_up_gather",
        cargos=[Cargo("gather_slots", [shard["ffn1_w_down"]])])
    x1, (mid, (w["ffn2_w_down"],)) = _ffn_down(
        xt, a1, w["ffn1_w_down"], "ffn1_down",
        cargos=[Cargo("gather_chips", [shard[k] for k in MID]), Cargo("gather_slots", [shard["ffn2_w_down"]])])
    w["w_in"] = mid[0]
    w["conv_dw"] = mid[1].transpose(1, 0, 2).reshape(CONV_WIDTH + 1, D_CONV)
    w["conv_pw"] = mid[2].reshape(D_CONV, D_CONV)
    w["w_out"] = mid[3].reshape(D_CONV + D_POOL, D_MODEL)
    (x2, h2, proj, u1, u3, mixed, cat), ((w["ffn2_w_gate"], w["ffn2_w_up"]),) = _mix_fwd(
        x1, w["mix_norm"], w["w_in"], w["conv_dw"], w["conv_dw_b"], w["conv_ln_g"], w["conv_ln_b"], w["conv_pw"],
        w["pool_w"], w["pool_scale"], w["w_out"], "mix_fwd",
        cargos=[Cargo("gather_slots", [shard["ffn2_w_gate"], shard["ffn2_w_up"]])])
    x3, h3, s2, p2, a2 = _ffn_fwd(x2, w["ffn2_norm"], w["ffn2_w_gate"], w["ffn2_w_up"], w["ffn2_w_down"], "ffn2_fwd")
    dx3, loss_share, d_final = _final_norm_loss(x3, target, w["final_norm"], "final_norm_loss")

    g = {"final_norm": d_final}
    sums = {}

    def landed(names, parts):
        sums.update(zip(names, parts))

    dx2, g["ffn2_norm"], df2, dg2, du2 = _ffn_bwd(dx3, x2, w["ffn2_norm"], s2, p2, w["ffn2_w_gate"],
                                                   w["ffn2_w_up"], w["ffn2_w_down"], "ffn2_bwd")
    def ffn_wgrad(names, hids, tok, kernel_name, cargos=()):
        parts, cargo_outs = _wgrad_hid_tok_scatter(hids, tok, kernel_name, cargos=cargos)
        landed(names, parts)
        return cargo_outs

    ffn_wgrad(["ffn2_w_gate", "ffn2_w_up"], [dg2, du2], h3, "ffn2_dw_gate_up")
    ffn_wgrad(["ffn2_w_down"], [a2], df2, "ffn2_dw_down")
    (dx1, dproj, g_pw, g["pool_w"], g_dw, g["conv_dw_b"], g["conv_ln_g"], g["conv_ln_b"], g["pool_scale"],
     g["mix_norm"]), (swapped2,) = _mix_bwd(
        dx2, u1, u3, mixed, proj, x1, w["mix_norm"], w["conv_dw"], w["conv_ln_g"], w["conv_ln_b"], w["conv_pw"],
        w["pool_w"], w["pool_scale"], w["w_out"], w["w_in"], "mix_bwd",
        cargos=[Cargo("swap", [sums["ffn2_" + k] for k in FFN_W])])
    g_out, _ = _wgrad_2d(cat, dx2, 1, BF16, "dw_out")
    slabs = [g_pw.reshape(N_CHIPS, D_CONV // N_CHIPS, D_CONV),
             g_out.reshape(N_CHIPS, (D_CONV + D_POOL) // N_CHIPS, D_MODEL)]
    g_in, (parts,) = _wgrad_2d(h2, dproj, N_CHIPS, BF16, "dw_in", cargos=[Cargo("scatter_chips", slabs)])
    landed(["conv_pw", "w_out"], parts)
    dx, g["ffn1_norm"], df1, dg1, du1_ = _ffn_bwd(dx1, xt, w["ffn1_norm"], s1, p1, w["ffn1_w_gate"],
                                                   w["ffn1_w_up"], w["ffn1_w_down"], "ffn1_bwd")
    slabs = [g_in, g_dw.reshape(CONV_WIDTH + 1, N_CHIPS, D_CONV // N_CHIPS).transpose(1, 0, 2)]
    (parts,) = ffn_wgrad(["ffn1_w_gate", "ffn1_w_up"], [dg1, du1_], h1, "ffn1_dw_gate_up",
                         cargos=[Cargo("scatter_chips", slabs)])
    landed(["w_in", "conv_dw"], parts)
    swapped_mid, swapped_gate_up, small_parts = ffn_wgrad(
        ["ffn1_w_down"], [a1], df1, "ffn1_dw_down",
        cargos=[Cargo("swap", [sums[k] for k in MID]), Cargo("swap", [sums["ffn1_w_gate"], sums["ffn1_w_up"]]),
                Cargo("gather_devices", [_pack_small(g, spare=loss_share)])])
    swapped_down = _exchange(Cargo("swap", [sums["ffn1_w_down"]]), "swap_last")

    theirs = dict(zip(["ffn2_" + k for k in FFN_W], swapped2))
    theirs.update(zip(MID, swapped_mid))
    theirs.update(ffn1_w_gate=swapped_gate_up[0], ffn1_w_up=swapped_gate_up[1], ffn1_w_down=swapped_down[0])
    grads, deltas, new_m, new_v = {}, {}, {}, {}
    for k in theirs:
        res = _adamw([sums[k], theirs[k]], _as_stored(k, wts[k]), _as_stored(k, mom_m[k]),
                     _as_stored(k, mom_v[k]), "adamw_" + k)
        grads[k], deltas[k], new_m[k], new_v[k] = [_as_given(k, t) for t in res]
    res = _adamw(small_parts, _pack_small(wts), _pack_small(mom_m), _pack_small(mom_v), "adamw_small")
    for dst, packed in zip((grads, deltas, new_m, new_v), res):
        dst.update(_unpack_small(packed))
    loss = res[0][PACK_LOSS_ROW, 0]

    out = [loss, dx[None]]
    for group in (grads, deltas, new_m, new_v):
        out += [group[k] for k in WEIGHTS]
    return tuple(out)
```

```python
import functools

import jax
import jax.numpy as jnp
from jax import lax
from jax.experimental import pallas as pl
from jax.experimental.pallas import tpu as pltpu

F32 = jnp.float32
BF16 = jnp.bfloat16
MESH = pl.DeviceIdType.MESH

N_CHIPS = 4
N_DEV = 8
D_MODEL = 1024
D_CONV = 512
D_POOL = 512
CONV_WIDTH = 31
POOL_WINDOWS = (2, 4, 8, 16)
POOL_GROUP = 128
D_IN = 2 * D_CONV + D_POOL
HALO = 32
RMS_EPS = 1e-6
LN_EPS = 1e-5
FFN_RES_WEIGHT = 0.5
ADAM_LR = 0.001
ADAM_B1 = 0.9
ADAM_B2 = 0.999
ADAM_EPS = 1e-08
ADAM_WD = 0.01
ADAM_STEP = 10
VMEM_LIMIT_BYTES = 52 * 1024 * 1024
VMEM_LIMIT_BYTES_LARGE = 58 * 1024 * 1024
TM_FFN = 512
TM_MIX = 256
TM_MIX_FWD = 512
TT_WGRAD = 2048
STRIP = 16
SLOTS_PER_STEP = 2
SLOTS_PER_STEP_FWD = 4
SUBLANES = 8
RELAY_AT_EIGHTHS = 7

HBM = pl.BlockSpec(memory_space=pl.ANY)


def _dot(a, b):
    return jnp.dot(a, b, preferred_element_type=F32)


def _dot_nt(a, b):
    return lax.dot_general(a, b, (((1,), (1,)), ((), ())), preferred_element_type=F32)


def _dot_tn(a, b):
    return lax.dot_general(a, b, (((0,), (0,)), ((), ())), preferred_element_type=F32)


def _sds(shape, dtype):
    return jax.ShapeDtypeStruct(shape, dtype)


def _rms_stats(xv):
    r = lax.rsqrt(jnp.mean(xv * xv, axis=-1, keepdims=True) + RMS_EPS)
    return r, xv * r


def _swiglu_saved(gate, up):
    sig = jax.nn.sigmoid(gate)
    silu = gate * sig
    return silu, up * (sig * (1.0 + gate * (1.0 - sig))), silu * up


def _rms_bwd(dh, n, r, gain):
    dn = dh * gain
    return r * (dn - n * jnp.mean(dn * n, axis=-1, keepdims=True))


def _place():
    x, y, c = lax.axis_index("x"), lax.axis_index("y"), lax.axis_index("c")
    return x, y, c, [(1 - x, y), (x, 1 - y), (1 - x, 1 - y)]


class Cargo:
    def __init__(self, kind, arrays):
        self.kind, self.arrays = kind, list(arrays)
        n = len(self.arrays)
        self.two_level = kind in ("gather_slots", "gather_chips")
        if self.two_level:
            self.out_shape = [_sds((N_CHIPS,) + a.shape, a.dtype) for a in self.arrays]
        elif kind == "gather_devices":
            self.out_shape = [_sds((N_DEV,) + a.shape, a.dtype) for a in self.arrays]
        else:
            self.out_shape = [_sds(a.shape, a.dtype) for a in self.arrays]
        n_remote = n * {"swap": 1, "gather_devices": N_DEV - 1}.get(kind, N_CHIPS - 1)
        n_own = 0 if kind == "swap" else n
        n_relay = n_remote if self.two_level else 0
        dma = pltpu.SemaphoreType.DMA
        self.scratch = [dma((n_remote,)), dma((n_remote,)), dma((max(n_own, 1),)),
                        dma((max(n_relay, 1),)), dma((max(n_relay, 1),))]

    def _plan(self, ins, outs):
        x, y, c, chips = _place()
        q = 2 * x + y
        sibling = (x, y, 1 - c)
        own, remote, relays = [], [], []
        for a, o in zip(ins, outs):
            if self.two_level:
                half = a.shape[0] // 2
                mine = pl.ds(pl.multiple_of(c * half, SUBLANES), half)
                theirs = pl.ds(pl.multiple_of((1 - c) * half, SUBLANES), half)
                own.append((a, o.at[0 if self.kind == "gather_slots" else q]))
                for j, (px, py) in enumerate(chips):
                    there, here = (j + 1, j + 1) if self.kind == "gather_slots" else (q, 2 * px + py)
                    remote.append((a.at[mine], o.at[there, mine], o.at[here, mine], (px, py, c)))
                    relays.append((o.at[here, mine], o.at[here, mine], o.at[here, theirs], sibling))
            elif self.kind == "scatter_chips":
                own.append((a.at[q], o.at[q]))
                remote += [(a.at[2 * px + py], o.at[q], o.at[2 * px + py], (px, py, c)) for px, py in chips]
            elif self.kind == "swap":
                remote.append((a, o, o, sibling))
            else:
                own.append((a, o.at[4 * x + 2 * y + c]))
                for k in range(1, N_DEV):
                    px, py, pc = x ^ (k >> 2 & 1), y ^ (k >> 1 & 1), c ^ (k & 1)
                    remote.append((a, o.at[4 * x + 2 * y + c], o.at[4 * px + 2 * py + pc], (px, py, pc)))
        return own, remote, relays

    @staticmethod
    def _copies(entries, send_sems, recv_sems):
        out = []
        for k, (src, dst, landed, peer) in enumerate(entries):
            def make(dst_ref, k=k, src=src, peer=peer):
                return pltpu.make_async_remote_copy(src_ref=src, dst_ref=dst_ref, send_sem=send_sems.at[k],
                                                    recv_sem=recv_sems.at[k], device_id=peer, device_id_type=MESH)
            out.append((make(dst), make(landed)))
        return out

    def start(self, ins, outs, sems):
        own, remote, _ = self._plan(ins, outs)
        for k, (src, dst) in enumerate(own):
            pltpu.make_async_copy(src, dst, sems[2].at[k]).start()
        for mine, _ in self._copies(remote, sems[0], sems[1]):
            mine.start()

    def relay(self, ins, outs, sems):
        _, remote, relays = self._plan(ins, outs)
        passed = self._copies(relays, sems[3], sems[4])
        for (_, arriving), (mine, _) in zip(self._copies(remote, sems[0], sems[1]), passed):
            arriving.wait_recv()
            mine.start()

    def wait(self, ins, outs, sems):
        own, remote, relays = self._plan(ins, outs)
        for mine, arriving in self._copies(remote, sems[0], sems[1]):
            mine.wait_send()
            if not self.two_level:
                arriving.wait_recv()
        for mine, arriving in self._copies(relays, sems[3], sems[4]):
            mine.wait_send()
            arriving.wait_recv()
        for k, (src, dst) in enumerate(own):
            pltpu.make_async_copy(src, dst, sems[2].at[k]).wait()


N_CARGO_SEMS = 5


def _call(body, *, name, grid, in_specs, out_specs, out_shape, args, scratch_shapes=(), cargos=(),
          vmem_limit_bytes=VMEM_LIMIT_BYTES):
    n_in, n_out, n_scr = len(in_specs), len(out_specs), len(scratch_shapes)
    c_in = [len(cg.arrays) for cg in cargos]
    n_cin = sum(c_in)

    def wrapped(*refs):
        ins = refs[:n_in]
        cins = refs[n_in:n_in + n_cin]
        outs = refs[n_in + n_cin:n_in + n_cin + n_out]
        couts = refs[n_in + n_cin + n_out:n_in + 2 * n_cin + n_out]
        scr = refs[n_in + 2 * n_cin + n_out:n_in + 2 * n_cin + n_out + n_scr]
        sems = refs[n_in + 2 * n_cin + n_out + n_scr:]
        step, n_steps = 0, 1
        for ax, size in enumerate(grid):
            step = step * size + pl.program_id(ax)
            n_steps *= size

        def each(method, only_two_level=False):
            at = 0
            for k, cg in enumerate(cargos):
                if cg.two_level or not only_two_level:
                    getattr(cg, method)(cins[at:at + c_in[k]], couts[at:at + c_in[k]],
                                        sems[N_CARGO_SEMS * k:N_CARGO_SEMS * (k + 1)])
                at += c_in[k]

        body(*ins, *outs, *scr)
        if cargos:
            pl.when(step == 0)(lambda: each("start"))
        if any(cg.two_level for cg in cargos):
            pl.when(step == (RELAY_AT_EIGHTHS * n_steps) // 8)(lambda: each("relay", only_two_level=True))
        if cargos:
            pl.when(step == n_steps - 1)(lambda: each("wait"))

    res = pl.pallas_call(
        wrapped, name=name, grid=grid,
        in_specs=list(in_specs) + [HBM] * n_cin,
        out_specs=list(out_specs) + [HBM] * n_cin,
        out_shape=list(out_shape) + [s for cg in cargos for s in cg.out_shape],
        scratch_shapes=list(scratch_shapes) + [s for cg in cargos for s in cg.scratch],
        compiler_params=pltpu.CompilerParams(dimension_semantics=("arbitrary",) * len(grid),
                                             vmem_limit_bytes=vmem_limit_bytes),
    )(*args, *[a for cg in cargos for a in cg.arrays])
    outs, rest = list(res[:n_out]), list(res[n_out:])
    cargo_outs = []
    for k in c_in:
        cargo_outs.append(rest[:k])
        rest = rest[k:]
    return outs, cargo_outs


def _exchange(cargo, name):
    _, (outs,) = _call(lambda: None, name=name, grid=(1,), in_specs=[], out_specs=[], out_shape=[], args=[],
                       cargos=[cargo])
    return outs


def _ffn_up_gather(x, gain, wg_t, wu_t, name, cargos=()):
    t_len, d = x.shape
    fq = wg_t.shape[0]
    tm = min(TM_FFN, t_len)
    n_tiles = t_len // tm
    relay_tile = n_tiles // 2
    fetch_tile = min(relay_tile + 1, n_tiles - 1)

    def body(x_ref, g_ref, wg_in, wu_in, h_ref, s_ref, p_ref, a_ref, wg_all, wu_all,
             wg_v, wu_v, h_all, send_sems, recv_sems, pass_send_sems, pass_recv_sems, own_sems, load_sems):
        s = pl.program_id(0)
        i = pl.program_id(1)
        x_, y_, c_, chips = _place()
        shards = ((wg_in, wg_all, wg_v), (wu_in, wu_all, wu_v))
        mine = pl.ds(pl.multiple_of(c_ * (fq // 2), SUBLANES), fq // 2)
        theirs = pl.ds(pl.multiple_of((1 - c_) * (fq // 2), SUBLANES), fq // 2)

        def to_peer(k, j):
            w_in, w_all, _ = shards[k]
            return pltpu.make_async_remote_copy(
                src_ref=w_in.at[mine], dst_ref=w_all.at[j + 1, mine], send_sem=send_sems.at[3 * k + j],
                recv_sem=recv_sems.at[3 * k + j], device_id=(*chips[j], c_), device_id_type=MESH)

        def to_sibling(k, j, landing=False):
            w_all = shards[k][1]
            return pltpu.make_async_remote_copy(
                src_ref=w_all.at[j + 1, mine], dst_ref=w_all.at[j + 1, theirs if landing else mine],
                send_sem=pass_send_sems.at[3 * k + j], recv_sem=pass_recv_sems.at[3 * k + j],
                device_id=(x_, y_, 1 - c_), device_id_type=MESH)

        def keep(k):
            return pltpu.make_async_copy(shards[k][0], shards[k][1].at[0], own_sems.at[k])

        @pl.when((s == 0) & (i == 0))
        def _():
            for j in range(N_CHIPS - 1):
                for k in range(2):
                    to_peer(k, j).start()
            for k in range(2):
                keep(k).start()

        def load(k, slot):
            src = shards[k][0] if slot == 0 else shards[k][1].at[slot]
            return pltpu.make_async_copy(src, shards[k][2].at[slot % 2], load_sems.at[k])

        @pl.when((s == 0) & (i == 0))
        def _():
            for k in range(2):
                load(k, 0).start()
            for k in range(2):
                load(k, 0).wait()

        def pass_on(slot):
            for k in range(2):
                to_peer(k, slot - 1).wait_recv()
                to_sibling(k, slot - 1).start()

        def fetch(slot):
            for k in range(2):
                to_sibling(k, slot - 1, landing=True).wait_recv()
                load(k, slot).start()

        for slot in range(1, N_CHIPS):
            pl.when((s == slot - 1) & (i == relay_tile))(functools.partial(pass_on, slot))
            pl.when((s == slot - 1) & (i == fetch_tile))(functools.partial(fetch, slot))

            @pl.when((s == slot) & (i == 0))
            def _():
                for k in range(2):
                    load(k, slot).wait()

        @pl.when(s == 0)
        def _():
            _, n = _rms_stats(x_ref[...])
            h_new = (n * g_ref[...]).astype(BF16)
            h_ref[...] = h_new
            h_all[i] = h_new

        h = h_all[i]
        silu, dgate, act = _swiglu_saved(_dot_nt(h, wg_v[s % 2]), _dot_nt(h, wu_v[s % 2]))
        s_ref[...] = silu.astype(BF16)
        p_ref[...] = dgate.astype(BF16)
        a_ref[...] = act.astype(BF16)

        @pl.when((s == N_CHIPS - 1) & (i == n_tiles - 1))
        def _():
            for k in range(2):
                for j in range(N_CHIPS - 1):
                    to_peer(k, j).wait_send()
                    to_sibling(k, j).wait_send()
                keep(k).wait()

    tok = pl.BlockSpec((tm, d), lambda s, i: (jnp.where(s == 0, i, n_tiles - 1), 0))
    hid = pl.BlockSpec((None, tm, fq), lambda s, i: (s, i, 0))
    outs, cargo_outs = _call(
        body, name=name, grid=(N_CHIPS, n_tiles),
        in_specs=[tok, pl.BlockSpec((1, d), lambda s, i: (0, 0)), HBM, HBM],
        out_specs=[tok, hid, hid, hid, HBM, HBM],
        out_shape=[_sds((t_len, d), BF16)] + [_sds((N_CHIPS, t_len, fq), BF16)] * 3
        + [_sds((N_CHIPS, fq, d), BF16)] * 2,
        scratch_shapes=[pltpu.VMEM((2, fq, d), BF16), pltpu.VMEM((2, fq, d), BF16),
                        pltpu.VMEM((n_tiles, tm, d), BF16)]
        + [pltpu.SemaphoreType.DMA((6,))] * 4 + [pltpu.SemaphoreType.DMA((2,))] * 2,
        args=[x, gain, wg_t, wu_t], cargos=cargos)
    return outs, cargo_outs


def _load_once(hbm_refs, vmem_refs, sems, first):
    @pl.when(first)
    def _():
        copies = [pltpu.make_async_copy(src, dst, sems.at[k]) for k, (src, dst) in enumerate(zip(hbm_refs, vmem_refs))]
        for cp in copies:
            cp.start()
        for cp in copies:
            cp.wait()


def _ffn_down(x, act, wd, name, cargos=()):
    t_len, d = x.shape
    nq, fq, _ = wd.shape
    tm = min(TM_FFN, t_len)

    def body(x_ref, a_ref, wd_ref, xo_ref):
        y = _dot(a_ref[0], wd_ref[0])
        for j in range(1, nq):
            y = y + _dot(a_ref[j], wd_ref[j])
        xo_ref[...] = x_ref[...] + FFN_RES_WEIGHT * y

    tok = pl.BlockSpec((tm, d), lambda i: (i, 0))
    (xo,), cargo_outs = _call(
        body, name=name, grid=(t_len // tm,),
        in_specs=[tok, pl.BlockSpec((nq, tm, fq), lambda i: (0, i, 0)), pl.BlockSpec((nq, fq, d), lambda i: (0, 0, 0))],
        out_specs=[tok], out_shape=[_sds((t_len, d), F32)], args=[x, act, wd], cargos=cargos)
    return xo, cargo_outs


def _ffn_fwd(x, gain, wg_t, wu_t, wd, name):
    t_len, d = x.shape
    nq, fq, _ = wd.shape
    tm = min(TM_FFN, t_len)

    def body(x_ref, g_ref, wg_hbm, wu_hbm, wd_hbm, xo_ref, h_ref, s_ref, p_ref, a_ref,
             h_s, acc, wg_v, wu_v, wd_v, load_sems):
        i = pl.program_id(0)
        j = pl.program_id(1)
        _load_once((wg_hbm, wu_hbm, wd_hbm), (wg_v, wu_v, wd_v), load_sems, (i == 0) & (j == 0))

        @pl.when(j == 0)
        def _():
            _, n = _rms_stats(x_ref[...])
            h = (n * g_ref[...]).astype(BF16)
            h_s[...] = h
            h_ref[...] = h
            acc[...] = jnp.zeros_like(acc)

        h = h_s[...]
        y = None
        for jj in range(SLOTS_PER_STEP_FWD):
            slot = j * SLOTS_PER_STEP_FWD + jj
            silu, dgate, act = _swiglu_saved(_dot_nt(h, wg_v[slot]), _dot_nt(h, wu_v[slot]))
            s_ref[jj] = silu.astype(BF16)
            p_ref[jj] = dgate.astype(BF16)
            a_ref[jj] = act.astype(BF16)
            part = _dot(a_ref[jj], wd_v[slot])
            y = part if y is None else y + part
        acc[...] += y

        @pl.when(j == nq // SLOTS_PER_STEP_FWD - 1)
        def _():
            xo_ref[...] = x_ref[...] + FFN_RES_WEIGHT * acc[...]

    tok = pl.BlockSpec((tm, d), lambda i, j: (i, 0))
    hid = pl.BlockSpec((SLOTS_PER_STEP_FWD, tm, fq), lambda i, j: (j, i, 0))
    outs, _ = _call(
        body, name=name, grid=(t_len // tm, nq // SLOTS_PER_STEP_FWD),
        in_specs=[tok, pl.BlockSpec((1, d), lambda i, j: (0, 0)), HBM, HBM, HBM],
        out_specs=[tok, tok, hid, hid, hid],
        out_shape=[_sds((t_len, d), F32), _sds((t_len, d), BF16)] + [_sds((nq, t_len, fq), BF16)] * 3,
        scratch_shapes=[pltpu.VMEM((tm, d), BF16), pltpu.VMEM((tm, d), F32)]
        + [pltpu.VMEM((nq, fq, d), BF16)] * 3 + [pltpu.SemaphoreType.DMA((3,))],
        args=[x, gain, wg_t, wu_t, wd], vmem_limit_bytes=VMEM_LIMIT_BYTES_LARGE)
    return outs


def _ffn_bwd(dy, x_in, gain, silu, dgate_du, wg_t, wu_t, wd, name):
    t_len, d = dy.shape
    nq, fq, _ = wd.shape
    tm = min(TM_FFN, t_len)

    def body(dy_ref, x_ref, g_ref, s_ref, p_ref, wg_hbm, wu_hbm, wd_hbm,
             dx_ref, dgain_ref, df_ref, dg_ref, du_ref, df_s, dh_acc, dact_s, wg_v, wu_v, wd_v, load_sems):
        i = pl.program_id(0)
        j = pl.program_id(1)
        _load_once((wg_hbm, wu_hbm, wd_hbm), (wg_v, wu_v, wd_v), load_sems, (i == 0) & (j == 0))

        @pl.when((i == 0) & (j == 0))
        def _():
            dgain_ref[...] = jnp.zeros_like(dgain_ref)

        @pl.when(j == 0)
        def _():
            df = (FFN_RES_WEIGHT * dy_ref[...]).astype(BF16)
            df_s[...] = df
            df_ref[...] = df
            dh_acc[...] = jnp.zeros_like(dh_acc)

        slots = [j * SLOTS_PER_STEP + jj for jj in range(SLOTS_PER_STEP)]
        for jj, slot in enumerate(slots):
            dact_s[jj] = _dot_nt(df_s[...], wd_v[slot])

        for jj in range(SLOTS_PER_STEP):
            for r0 in range(0, tm, STRIP):
                rows = slice(r0, r0 + STRIP)
                dact = dact_s[jj, rows, :]
                dg_ref[jj, rows, :] = (dact * p_ref[jj, rows, :].astype(F32)).astype(BF16)
                du_ref[jj, rows, :] = (dact * s_ref[jj, rows, :].astype(F32)).astype(BF16)

        dh = None
        for jj, slot in enumerate(slots):
            part = _dot(dg_ref[jj], wg_v[slot]) + _dot(du_ref[jj], wu_v[slot])
            dh = part if dh is None else dh + part
        dh_acc[...] += dh

        @pl.when(j == nq // SLOTS_PER_STEP - 1)
        def _():
            r, n = _rms_stats(x_ref[...])
            dh = dh_acc[...]
            dgain_ref[...] += jnp.sum(dh * n, axis=0, keepdims=True)
            dx_ref[...] = dy_ref[...] + _rms_bwd(dh, n, r, g_ref[...])

    tok = pl.BlockSpec((tm, d), lambda i, j: (i, 0))
    vec = pl.BlockSpec((1, d), lambda i, j: (0, 0))
    hid = pl.BlockSpec((SLOTS_PER_STEP, tm, fq), lambda i, j: (j, i, 0))
    outs, _ = _call(
        body, name=name, grid=(t_len // tm, nq // SLOTS_PER_STEP),
        in_specs=[tok, tok, vec, hid, hid, HBM, HBM, HBM],
        out_specs=[tok, vec, tok, hid, hid],
        out_shape=[_sds((t_len, d), F32), _sds((1, d), F32), _sds((t_len, d), BF16),
                   _sds((nq, t_len, fq), BF16), _sds((nq, t_len, fq), BF16)],
        scratch_shapes=[pltpu.VMEM((tm, d), BF16), pltpu.VMEM((tm, d), F32),
                        pltpu.VMEM((SLOTS_PER_STEP, tm, fq), F32)]
        + [pltpu.VMEM((nq, fq, d), BF16)] * 3 + [pltpu.SemaphoreType.DMA((3,))],
        args=[dy, x_in, gain, silu, dgate_du, wg_t, wu_t, wd], vmem_limit_bytes=VMEM_LIMIT_BYTES_LARGE)
    return outs


def _wgrad(lhs, rhs, l_spec, r_spec, out_shape, out_spec, acc_shape, grid, name, cargos=()):
    n_t = grid[-1]
    t_axis = len(grid) - 1

    def body(l_ref, r_ref, o_ref, acc):
        t = pl.program_id(t_axis)

        @pl.when(t == 0)
        def _():
            acc[...] = jnp.zeros_like(acc)

        acc[...] += _dot_tn(l_ref[...].astype(BF16), r_ref[...].astype(BF16))

        @pl.when(t == n_t - 1)
        def _():
            o_ref[...] = acc[...].astype(o_ref.dtype)

    (out,), cargo_outs = _call(
        body, name=name, grid=grid, in_specs=[l_spec, r_spec], out_specs=[out_spec], out_shape=[out_shape],
        scratch_shapes=[pltpu.VMEM(acc_shape, F32)], args=[lhs, rhs], cargos=cargos)
    return out, cargo_outs


def _wgrad_hid_tok_scatter(hids, tok, name, cargos=()):
    t_len, d = tok.shape
    n_w = len(hids)
    nq, _, fq = hids[0].shape
    half = fq // 2
    tt = min(TT_WGRAD, t_len)
    n_t = t_len // tt
    per_w = 4
    n_sem = 6

    def body(*refs):
        l_refs, r_ref, parts_refs = refs[:n_w], refs[n_w], refs[n_w + 1:2 * n_w + 1]
        scr = refs[2 * n_w + 1:]
        bufs = [scr[per_w * w:per_w * (w + 1)] for w in range(n_w)]
        zeros = scr[per_w * n_w]
        sems = [scr[per_w * n_w + 1 + n_sem * w:per_w * n_w + 1 + n_sem * (w + 1)] for w in range(n_w)]
        g = pl.program_id(0)
        t = pl.program_id(1)
        x_, y_, c_, chips = _place()
        mine = pl.ds(pl.multiple_of(c_ * half, STRIP), half)
        theirs = pl.ds(pl.multiple_of((1 - c_) * half, STRIP), half)

        def to_sibling(w, slot):
            return pltpu.make_async_remote_copy(
                src_ref=bufs[w][1].at[theirs], dst_ref=bufs[w][2].at[slot], send_sem=sems[w][0].at[slot],
                recv_sem=sems[w][1].at[slot], device_id=(x_, y_, 1 - c_), device_id_type=MESH)

        def to_peer(w, j):
            return pltpu.make_async_remote_copy(
                src_ref=bufs[w][3].at[j + 1], dst_ref=parts_refs[w].at[j + 1, mine], send_sem=sems[w][2].at[j],
                recv_sem=sems[w][3].at[j], device_id=(*chips[j], c_), device_id_type=MESH)

        def keep(w):
            return pltpu.make_async_copy(bufs[w][3].at[0], parts_refs[w].at[0, mine], sems[w][4])

        def blank(w, slot):
            return pltpu.make_async_copy(zeros, parts_refs[w].at[slot, theirs], sems[w][5].at[slot])

        @pl.when((g == 0) & (t == 0))
        def _():
            zeros[...] = jnp.zeros_like(zeros)
            for w in range(n_w):
                for slot in range(nq):
                    blank(w, slot).start()

        @pl.when(t == 0)
        def _():
            for w in range(n_w):
                bufs[w][0][...] = jnp.zeros_like(bufs[w][0])

        rhs = r_ref[...]
        for w in range(n_w):
            bufs[w][0][...] += _dot_tn(l_refs[w][...], rhs)

        for step in range(nq):
            slot = (step + 1) % nq

            @pl.when((g == step) & (t == n_t - 1))
            def _():
                for w in range(n_w):
                    acc, stage, _, _ = bufs[w]
                    if step > 0:
                        to_sibling(w, step).wait_send()
                    stage[...] = acc[...].astype(BF16)
                    to_sibling(w, slot).start()
                for w in range(n_w):
                    _, stage, pair, summed = bufs[w]
                    to_sibling(w, slot).wait_recv()
                    summed[slot] = (stage[mine, :].astype(F32) + pair[slot].astype(F32)).astype(BF16)
                    if slot > 0:
                        to_peer(w, slot - 1).start()
                    else:
                        keep(w).start()

        @pl.when((g == nq - 1) & (t == n_t - 1))
        def _():
            for w in range(n_w):
                for j in range(N_CHIPS - 1):
                    to_peer(w, j).wait()
                keep(w).wait()
                to_sibling(w, 0).wait_send()
                for slot in range(nq):
                    blank(w, slot).wait()

    dma = pltpu.SemaphoreType.DMA
    scratch = []
    for _ in range(n_w):
        scratch += [pltpu.VMEM((fq, d), F32), pltpu.VMEM((fq, d), BF16), pltpu.VMEM((nq, half, d), BF16),
                    pltpu.VMEM((nq, half, d), BF16)]
    scratch.append(pltpu.VMEM((half, d), BF16))
    for _ in range(n_w):
        scratch += [dma((nq,)), dma((nq,)), dma((N_CHIPS - 1,)), dma((N_CHIPS - 1,)), dma(()), dma((nq,))]
    parts, cargo_outs = _call(
        body, name=name, grid=(nq, n_t),
        in_specs=[pl.BlockSpec((None, tt, fq), lambda g, t: ((g + 1) % nq, t, 0))] * n_w
        + [pl.BlockSpec((tt, d), lambda g, t: (t, 0))],
        out_specs=[HBM] * n_w, out_shape=[_sds((nq, fq, d), BF16)] * n_w,
        scratch_shapes=scratch, args=[*hids, tok], cargos=cargos)
    return parts, cargo_outs


def _wgrad_2d(lhs, rhs, n_col_blocks, out_dtype, name, cargos=()):
    t_len, k = lhs.shape
    n = rhs.shape[1]
    nb = n // n_col_blocks
    tt = min(TT_WGRAD, t_len)
    return _wgrad(lhs, rhs,
                  pl.BlockSpec((tt, k), lambda q, t: (t, 0)),
                  pl.BlockSpec((tt, nb), lambda q, t: (t, q)),
                  _sds((n_col_blocks, k, nb), out_dtype),
                  pl.BlockSpec((None, k, nb), lambda q, t: (q, 0, 0)),
                  (k, nb), (n_col_blocks, t_len // tt), name, cargos)


def _layernorm_stats(u1):
    mu = jnp.mean(u1, axis=-1, keepdims=True)
    xc = u1 - mu
    rstd = lax.rsqrt(jnp.mean(xc * xc, axis=-1, keepdims=True) + LN_EPS)
    return rstd, xc * rstd


def _positions(i, tm, rows, offset=0):
    return (lax.broadcasted_iota(jnp.int32, (rows, 1), 0) + (i * tm + offset)).astype(F32)


SHIFT_ROWS = HALO - SUBLANES


def _fill_shifted(ext_s, sh_s, tm):
    for b in range(1, SUBLANES):
        sh_s[b - 1] = ext_s[pl.ds(b, tm + SHIFT_ROWS), :]


def _window(ext_s, sh_s, shift, tm):
    a, b = divmod(shift, SUBLANES)
    if b == 0:
        return ext_s[pl.ds(shift, tm), :]
    return sh_s[b - 1, pl.ds(a * SUBLANES, tm), :]


def _window_sums(ext_s, lv_a, lv_b, tm, ahead):
    g = POOL_GROUP
    sign = 1 if ahead else -1
    for n, (dst, src, c0) in enumerate(((lv_a, ext_s, 0), (lv_b, lv_a, g), (lv_a, lv_b, 2 * g)), start=1):
        lo = 0 if ahead else n * SUBLANES
        rows = tm + HALO - n * SUBLANES
        shift = sign * 2 ** (n - 1)
        dst[pl.ds(lo, rows), c0:] = src[pl.ds(lo, rows), c0:] + src[pl.ds(lo + shift, rows), c0:]
    base = 0 if ahead else HALO
    rows = pl.ds(base, tm)
    far = pl.ds(base + sign * SUBLANES, tm)
    return [lv_a[rows, 0:g], lv_b[rows, g:2 * g], lv_a[rows, 2 * g:3 * g],
            lv_a[rows, 3 * g:] + lv_a[far, 3 * g:]]


def _tile(tm, cols):
    return pl.BlockSpec((tm, cols), lambda i: (i, 0))


def _whole(shape):
    return pl.BlockSpec(shape, lambda i: (0,) * len(shape))


def _mix_fwd(x1, gain, w_in, conv_dw, conv_b, ln_g, ln_b, conv_pw, pool_w, pool_scale, w_out, name, cargos=()):
    t_len, d = x1.shape
    nq, _, nb = w_in.shape
    tm = min(TM_MIX_FWD, t_len)

    def body(x_ref, g_ref, wi_ref, dw_ref, cb_ref, lg_ref, lb_ref, pw_ref, plw_ref, ps_ref, wo_ref,
             x2_ref, h_ref, p_ref, u1_ref, u3_ref, mx_ref, cat_ref, ext_s, pext_s, sh_s, tail_s, lva_s, lvb_s):
        i = pl.program_id(0)

        @pl.when(i == 0)
        def _():
            tail_s[...] = jnp.zeros_like(tail_s)

        _, n = _rms_stats(x_ref[...])
        h = (n * g_ref[...]).astype(BF16)
        h_ref[...] = h
        for q in range(nq):
            p_ref[:, q * nb:(q + 1) * nb] = _dot(h, wi_ref[q])

        a = p_ref[:, 0:D_CONV]
        g = p_ref[:, D_CONV:2 * D_CONV]
        p = p_ref[:, 2 * D_CONV:]
        ext_s[0:HALO, :] = tail_s[:, 0:D_CONV] * jax.nn.sigmoid(tail_s[:, D_CONV:2 * D_CONV])
        ext_s[HALO:, :] = a * jax.nn.sigmoid(g)
        pext_s[0:HALO, :] = tail_s[:, 2 * D_CONV:]
        pext_s[HALO:, :] = p
        tail_s[...] = p_ref[tm - HALO:tm, :]

        _fill_shifted(ext_s, sh_s, tm)
        u1 = jnp.broadcast_to(cb_ref[...], (tm, D_CONV))
        for k in range(CONV_WIDTH):
            u1 = u1 + dw_ref[k:k + 1, :] * _window(ext_s, sh_s, HALO - (CONV_WIDTH - 1) + k, tm)
        u1_ref[...] = u1
        _, nhat = _layernorm_stats(u1)
        u2 = nhat * lg_ref[...] + lb_ref[...]
        u3 = (u2 * jax.nn.sigmoid(u2)).astype(BF16)
        u3_ref[...] = u3
        cat_ref[:, 0:D_CONV] = _dot(u3, pw_ref[...]).astype(BF16)

        pos1 = _positions(i, tm, tm) + 1.0
        sums = _window_sums(pext_s, lva_s, lvb_s, tm, ahead=False)
        for gi, w in enumerate(POOL_WINDOWS):
            cols = slice(gi * POOL_GROUP, (gi + 1) * POOL_GROUP)
            mixed = (sums[gi] / jnp.minimum(pos1, float(w)) - p[:, cols]).astype(BF16)
            mx_ref[:, cols] = mixed
            out = _dot(mixed, plw_ref[gi]) * ps_ref[:, cols]
            cat_ref[:, D_CONV + gi * POOL_GROUP:D_CONV + (gi + 1) * POOL_GROUP] = out.astype(BF16)

        x2_ref[...] = x_ref[...] + _dot(cat_ref[...], wo_ref[...])

    return _call(
        body, name=name, grid=(t_len // tm,),
        in_specs=[_tile(tm, d), _whole((1, d)), _whole((nq, d, nb)), _whole((CONV_WIDTH + 1, D_CONV)),
                  _whole((1, D_CONV)), _whole((1, D_CONV)), _whole((1, D_CONV)), _whole((D_CONV, D_CONV)),
                  _whole((4, POOL_GROUP, POOL_GROUP)), _whole((1, D_POOL)), _whole((D_CONV + D_POOL, d))],
        out_specs=[_tile(tm, d), _tile(tm, d), _tile(tm, D_IN), _tile(tm, D_CONV), _tile(tm, D_CONV),
                   _tile(tm, D_POOL), _tile(tm, D_CONV + D_POOL)],
        out_shape=[_sds((t_len, d), F32), _sds((t_len, d), BF16), _sds((t_len, D_IN), F32),
                   _sds((t_len, D_CONV), F32), _sds((t_len, D_CONV), BF16), _sds((t_len, D_POOL), BF16),
                   _sds((t_len, D_CONV + D_POOL), BF16)],
        scratch_shapes=[pltpu.VMEM((tm + HALO, D_CONV), F32), pltpu.VMEM((tm + HALO, D_POOL), F32),
                        pltpu.VMEM((SUBLANES - 1, tm + SHIFT_ROWS, D_CONV), F32), pltpu.VMEM((HALO, D_IN), F32)]
        + [pltpu.VMEM((tm + HALO, D_POOL), F32)] * 2,
        args=[x1, gain, w_in, conv_dw, conv_b, ln_g, ln_b, conv_pw, pool_w, pool_scale, w_out], cargos=cargos,
        vmem_limit_bytes=VMEM_LIMIT_BYTES_LARGE)


def _mix_bwd(dx2, u1, u3, mixed, proj, x1, gain, conv_dw, ln_g, ln_b, conv_pw, pool_w, pool_scale, w_out, w_in,
             name, cargos=()):
    t_len, d = x1.shape
    nq, _, nb = w_in.shape
    tm = min(TM_MIX, t_len)
    hb = tm // HALO
    n_tiles = t_len // tm

    def body(dxn_ref, u1_ref, u3_ref, mx_ref, p_ref, tail_ref, x_ref, dx2_ref, g_ref, dw_ref, lg_ref, lb_ref, pw_ref,
             plw_ref, ps_ref, wo_ref, wi_ref,
             dx1_ref, dp_ref, dpw_ref, dplw_ref, ddw_ref, dcb_ref, dlg_ref, dlb_ref, dps_ref, dgain_ref,
             du_s, dm_s, uext_s, dext_s, mext_s, ush_s, dsh_s, lva_s, lvb_s):
        k = pl.program_id(0)

        @pl.when(k == 0)
        def _():
            for ref in (dpw_ref, dplw_ref, ddw_ref, dcb_ref, dlg_ref, dlb_ref, dps_ref, dgain_ref, du_s, dm_s):
                ref[...] = jnp.zeros_like(ref)

        counts = jnp.where(k < n_tiles, 1.0, 0.0)
        dcat = _dot_nt(dxn_ref[...].astype(BF16), wo_ref[...])
        dco = dcat[:, 0:D_CONV].astype(BF16)
        dpw_ref[...] += _dot_tn(u3_ref[...], (dcat[:, 0:D_CONV] * counts).astype(BF16))
        du3 = _dot_nt(dco, pw_ref[...])
        rstd, nhat = _layernorm_stats(u1_ref[...])
        u2 = nhat * lg_ref[...] + lb_ref[...]
        sig = jax.nn.sigmoid(u2)
        du2 = du3 * (sig * (1.0 + u2 * (1.0 - sig)))
        dlg_ref[...] += counts * jnp.sum(du2 * nhat, axis=0, keepdims=True)
        dlb_ref[...] += counts * jnp.sum(du2, axis=0, keepdims=True)
        dnhat = du2 * lg_ref[...]
        du_s[k % 2] = rstd * (dnhat - jnp.mean(dnhat, axis=-1, keepdims=True)
                              - nhat * jnp.mean(dnhat * nhat, axis=-1, keepdims=True))
        for gi in range(len(POOL_WINDOWS)):
            cols = slice(gi * POOL_GROUP, (gi + 1) * POOL_GROUP)
            dpo = dcat[:, D_CONV + gi * POOL_GROUP:D_CONV + (gi + 1) * POOL_GROUP]
            pre = _dot(mx_ref[:, cols], plw_ref[gi])
            dps_ref[:, cols] += counts * jnp.sum(dpo * pre, axis=0, keepdims=True)
            dout = dpo * ps_ref[:, cols]
            dplw_ref[gi] += _dot_tn(mx_ref[:, cols], (dout * counts).astype(BF16))
            dm_s[k % 2, :, cols] = _dot_nt(dout.astype(BF16), plw_ref[gi])

        i = jnp.maximum(k - 1, 0)
        cur, nxt = (k + 1) % 2, k % 2
        first = k <= 1
        last = (k == n_tiles) | (k == 0)
        a = p_ref[:, 0:D_CONV]
        g = p_ref[:, D_CONV:2 * D_CONV]
        sg = jax.nn.sigmoid(g)
        ta = tail_ref[:, 0:D_CONV]
        tg = tail_ref[:, D_CONV:2 * D_CONV]
        uext_s[0:HALO, :] = jnp.where(first, 0.0, ta * jax.nn.sigmoid(tg))
        uext_s[HALO:, :] = a * sg
        du1 = du_s[cur]
        dext_s[0:tm, :] = du1
        dext_s[tm:, :] = jnp.where(last, 0.0, du_s[nxt, 0:HALO, :])

        _fill_shifted(uext_s, ush_s, tm)
        _fill_shifted(dext_s, dsh_s, tm)
        du0 = jnp.zeros((tm, D_CONV), F32)
        for tap in range(CONV_WIDTH):
            du0 = du0 + dw_ref[tap:tap + 1, :] * _window(dext_s, dsh_s, CONV_WIDTH - 1 - tap, tm)
            ddw_ref[tap:tap + 1, :] += jnp.sum(
                du1 * _window(uext_s, ush_s, HALO - (CONV_WIDTH - 1) + tap, tm), axis=0, keepdims=True)
        dcb_ref[...] += jnp.sum(du1, axis=0, keepdims=True)
        dp_ref[:, 0:D_CONV] = (du0 * sg).astype(BF16)
        dp_ref[:, D_CONV:2 * D_CONV] = (du0 * a * sg * (1.0 - sg)).astype(BF16)

        pos1 = _positions(i, tm, tm) + 1.0
        pos1_next = _positions(i, tm, HALO, offset=tm) + 1.0
        for gi, w in enumerate(POOL_WINDOWS):
            cols = slice(gi * POOL_GROUP, (gi + 1) * POOL_GROUP)
            dm = dm_s[cur, :, cols]
            mext_s[0:tm, cols] = dm / jnp.minimum(pos1, float(w))
            mext_s[tm:, cols] = jnp.where(last, 0.0, dm_s[nxt, 0:HALO, cols] / jnp.minimum(pos1_next, float(w)))
        sums = _window_sums(mext_s, lva_s, lvb_s, tm, ahead=True)
        for gi in range(len(POOL_WINDOWS)):
            cols = slice(gi * POOL_GROUP, (gi + 1) * POOL_GROUP)
            dp_ref[:, 2 * D_CONV + gi * POOL_GROUP:2 * D_CONV + (gi + 1) * POOL_GROUP] = (
                sums[gi] - dm_s[cur, :, cols]).astype(BF16)

        dh = _dot_nt(dp_ref[:, 0:nb], wi_ref[0])
        for q in range(1, nq):
            dh = dh + _dot_nt(dp_ref[:, q * nb:(q + 1) * nb], wi_ref[q])
        r, n = _rms_stats(x_ref[...])
        dgain_ref[...] += jnp.sum(dh * n, axis=0, keepdims=True)
        dx1_ref[...] = dx2_ref[...] + _rms_bwd(dh, n, r, g_ref[...])

    def ahead(cols):
        return pl.BlockSpec((tm, cols), lambda k: (jnp.minimum(k, n_tiles - 1), 0))

    def behind(cols):
        return pl.BlockSpec((tm, cols), lambda k: (jnp.maximum(k - 1, 0), 0))

    vec = _whole((1, D_CONV))
    return _call(
        body, name=name, grid=(n_tiles + 1,),
        in_specs=[ahead(d), ahead(D_CONV), ahead(D_CONV), ahead(D_POOL), behind(D_IN),
                  pl.BlockSpec((HALO, D_IN), lambda k: (jnp.maximum(jnp.maximum(k - 1, 0) * hb - 1, 0), 0)),
                  behind(d), behind(d), _whole((1, d)), _whole((CONV_WIDTH + 1, D_CONV)), vec, vec,
                  _whole((D_CONV, D_CONV)), _whole((4, POOL_GROUP, POOL_GROUP)), vec,
                  _whole((D_CONV + D_POOL, d)), _whole((nq, d, nb))],
        out_specs=[behind(d), behind(D_IN), _whole((D_CONV, D_CONV)), _whole((4, POOL_GROUP, POOL_GROUP)),
                   _whole((CONV_WIDTH + 1, D_CONV)), vec, vec, vec, vec, _whole((1, d))],
        out_shape=[_sds((t_len, d), F32), _sds((t_len, D_IN), BF16), _sds((D_CONV, D_CONV), F32),
                   _sds((4, POOL_GROUP, POOL_GROUP), F32), _sds((CONV_WIDTH + 1, D_CONV), F32), _sds((1, D_CONV), F32),
                   _sds((1, D_CONV), F32), _sds((1, D_CONV), F32), _sds((1, D_POOL), F32), _sds((1, d), F32)],
        scratch_shapes=[pltpu.VMEM((2, tm, D_CONV), F32), pltpu.VMEM((2, tm, D_POOL), F32),
                        pltpu.VMEM((tm + HALO, D_CONV), F32), pltpu.VMEM((tm + HALO, D_CONV), F32),
                        pltpu.VMEM((tm + HALO, D_POOL), F32),
                        pltpu.VMEM((SUBLANES - 1, tm + SHIFT_ROWS, D_CONV), F32),
                        pltpu.VMEM((SUBLANES - 1, tm + SHIFT_ROWS, D_CONV), F32)]
        + [pltpu.VMEM((tm + HALO, D_POOL), F32)] * 2,
        args=[dx2, u1, u3, mixed, proj, proj, x1, dx2, gain, conv_dw, ln_g, ln_b, conv_pw, pool_w, pool_scale,
              w_out, w_in], cargos=cargos)


def _final_norm_loss(x3, target, gain, name):
    t_len, d = x3.shape
    tm = min(2 * TM_FFN, t_len)

    def body(x_ref, t_ref, g_ref, dx_ref, loss_ref, dgain_ref):
        @pl.when(pl.program_id(0) == 0)
        def _():
            loss_ref[...] = jnp.zeros_like(loss_ref)
            dgain_ref[...] = jnp.zeros_like(dgain_ref)

        r, n = _rms_stats(x_ref[...])
        err = n * g_ref[...] - t_ref[...]
        per_tok = jnp.sum(err * err, axis=-1, keepdims=True) * (1.0 / d)
        loss_ref[...] += 0.5 * jnp.sum(per_tok, axis=0, keepdims=True)
        dy = err * (1.0 / d)
        dgain_ref[...] += jnp.sum(dy * n, axis=0, keepdims=True)
        dx_ref[...] = _rms_bwd(dy, n, r, g_ref[...])

    tok = pl.BlockSpec((tm, d), lambda i: (i, 0))
    outs, _ = _call(
        body, name=name, grid=(t_len // tm,),
        in_specs=[tok, tok, pl.BlockSpec((1, d), lambda i: (0, 0))],
        out_specs=[tok, pl.BlockSpec((1, 128), lambda i: (0, 0)), pl.BlockSpec((1, d), lambda i: (0, 0))],
        out_shape=[_sds((t_len, d), F32), _sds((1, 128), F32), _sds((1, d), F32)],
        args=[x3, target, gain])
    return outs


def _row_tile(rows):
    return rows // 2 if rows % 64 == 0 else rows


def _adamw_math(w, g, m, v):
    m = ADAM_B1 * m + (1.0 - ADAM_B1) * g
    v = ADAM_B2 * v + (1.0 - ADAM_B2) * (g * g)
    m_hat = m / (1.0 - ADAM_B1 ** ADAM_STEP)
    v_hat = v / (1.0 - ADAM_B2 ** ADAM_STEP)
    delta = -ADAM_LR * (m_hat / (jnp.sqrt(v_hat) + ADAM_EPS) + ADAM_WD * w)
    return delta, m, v


def _adamw(parts, w, m, v, name):
    r, c = w.shape
    n = len(parts)
    tr = _row_tile(r)

    def body(*refs):
        g = None
        for p_ref in refs[:n]:
            s = p_ref[0].astype(F32)
            for k in range(1, p_ref.shape[0]):
                s = s + p_ref[k].astype(F32)
            g = s if g is None else g + s
        w_ref, m_ref, v_ref, g_out, d_out, m_out, v_out = refs[n:]
        delta, nm, nv = _adamw_math(w_ref[...], g, m_ref[...], v_ref[...])
        g_out[...] = g
        d_out[...] = delta
        m_out[...] = nm
        v_out[...] = nv

    blk = pl.BlockSpec((tr, c), lambda i: (i, 0))
    p_specs = [pl.BlockSpec((p.shape[0], tr, c), lambda i: (0, i, 0)) for p in parts]
    outs, _ = _call(body, name=name, grid=(r // tr,), in_specs=p_specs + [blk, blk, blk],
                    out_specs=[blk] * 4, out_shape=[_sds((r, c), F32)] * 4, args=[*parts, w, m, v])
    return outs


FFN_W = ("w_gate", "w_up", "w_down")
MID = ("w_in", "conv_dw", "conv_pw", "w_out")
SMALL_1024 = ("ffn1_norm", "mix_norm", "ffn2_norm", "final_norm")
SMALL_512 = ("conv_dw_b", "conv_ln_g", "conv_ln_b", "pool_scale")
WEIGHTS = ("ffn1_norm", "ffn1_w_gate", "ffn1_w_up", "ffn1_w_down", "mix_norm", "w_in", "conv_dw", "conv_dw_b",
           "conv_ln_g", "conv_ln_b", "conv_pw", "pool_w", "pool_scale", "w_out", "ffn2_norm", "ffn2_w_gate",
           "ffn2_w_up", "ffn2_w_down", "final_norm")
PACK_ROWS = 72
PACK_LOSS_ROW = 70


def _pad_rows(a, rows):
    return jnp.pad(a, ((0, rows - a.shape[0]), (0, 0)))


def _pack_small(t, spare=None):
    rows = [t[k].reshape(1, D_MODEL) for k in SMALL_1024]
    rows.append(jnp.concatenate([t["conv_dw_b"].reshape(1, -1), t["conv_ln_g"].reshape(1, -1)], axis=1))
    rows.append(jnp.concatenate([t["conv_ln_b"].reshape(1, -1), t["pool_scale"].reshape(1, -1)], axis=1))
    rows.append(t["pool_w"].reshape(64, D_MODEL))
    if spare is not None:
        rows.append(jnp.pad(spare, ((0, 0), (0, D_MODEL - spare.shape[1]))))
    return _pad_rows(jnp.concatenate(rows, axis=0), PACK_ROWS)


def _unpack_small(p):
    out = {k: p[i] for i, k in enumerate(SMALL_1024)}
    out["conv_dw_b"], out["conv_ln_g"] = p[4, :D_CONV], p[4, D_CONV:]
    out["conv_ln_b"], out["pool_scale"] = p[5, :D_CONV], p[5, D_CONV:]
    out["pool_w"] = p[6:70].reshape(4, POOL_GROUP, POOL_GROUP)
    return out


def _as_stored(name, a):
    if name.endswith(("w_gate", "w_up")):
        return a.T
    if name == "conv_dw":
        return _pad_rows(a, CONV_WIDTH + 1)
    return a


def _as_given(name, a):
    if name.endswith(("w_gate", "w_up")):
        return a.T
    if name == "conv_dw":
        return a[:CONV_WIDTH]
    return a


def kernel(x, ffn1_norm, ffn1_w_gate, ffn1_w_up, ffn1_w_down, mix_norm, w_in, conv_dw, conv_dw_b, conv_ln_g, conv_ln_b, conv_pw, pool_w, pool_scale, w_out, ffn2_norm, ffn2_w_gate, ffn2_w_up, ffn2_w_down, final_norm, loss_target, m_ffn1_norm, m_ffn1_w_gate, m_ffn1_w_up, m_ffn1_w_down, m_mix_norm, m_w_in, m_conv_dw, m_conv_dw_b, m_conv_ln_g, m_conv_ln_b, m_conv_pw, m_pool_w, m_pool_scale, m_w_out, m_ffn2_norm, m_ffn2_w_gate, m_ffn2_w_up, m_ffn2_w_down, m_final_norm, v_ffn1_norm, v_ffn1_w_gate, v_ffn1_w_up, v_ffn1_w_down, v_mix_norm, v_w_in, v_conv_dw, v_conv_dw_b, v_conv_ln_g, v_conv_ln_b, v_conv_pw, v_pool_w, v_pool_scale, v_w_out, v_ffn2_norm, v_ffn2_w_gate, v_ffn2_w_up, v_ffn2_w_down, v_final_norm):
    given = dict(locals())
    wts = {k: given[k] for k in WEIGHTS}
    mom_m = {k: given["m_" + k] for k in WEIGHTS}
    mom_v = {k: given["v_" + k] for k in WEIGHTS}
    xt, target = x[0], loss_target[0]

    shard = {k: _as_stored(k, wts[k]) if k == "conv_dw" else _as_stored(k, wts[k]).astype(BF16)
             for k in WEIGHTS if k.endswith(FFN_W) or k in MID}
    w = {k: wts[k].reshape(1, -1) for k in SMALL_1024 + SMALL_512}
    w["pool_w"] = wts["pool_w"].astype(BF16)

    (h1, s1, p1, a1, w["ffn1_w_gate"], w["ffn1_w_up"]), ((w["ffn1_w_down"],),) = _ffn_up_gather(
        xt, w["ffn1_norm"], shard["ffn1_w_gate"], shard["ffn1_w_up"], "ffn1_up_gather",
        cargos=[Cargo("gather_slots", [shard["ffn1_w_down"]])])
    x1, (mid, (w["ffn2_w_down"],)) = _ffn_down(
        xt, a1, w["ffn1_w_down"], "ffn1_down",
        cargos=[Cargo("gather_chips", [shard[k] for k in MID]), Cargo("gather_slots", [shard["ffn2_w_down"]])])
    w["w_in"] = mid[0]
    w["conv_dw"] = mid[1].transpose(1, 0, 2).reshape(CONV_WIDTH + 1, D_CONV)
    w["conv_pw"] = mid[2].reshape(D_CONV, D_CONV)
    w["w_out"] = mid[3].reshape(D_CONV + D_POOL, D_MODEL)
    (x2, h2, proj, u1, u3, mixed, cat), ((w["ffn2_w_gate"], w["ffn2_w_up"]),) = _mix_fwd(
        x1, w["mix_norm"], w["w_in"], w["conv_dw"], w["conv_dw_b"], w["conv_ln_g"], w["conv_ln_b"], w["conv_pw"],
        w["pool_w"], w["pool_scale"], w["w_out"], "mix_fwd",
        cargos=[Cargo("gather_slots", [shard["ffn2_w_gate"], shard["ffn2_w_up"]])])
    x3, h3, s2, p2, a2 = _ffn_fwd(x2, w["ffn2_norm"], w["ffn2_w_gate"], w["ffn2_w_up"], w["ffn2_w_down"], "ffn2_fwd")
    dx3, loss_share, d_final = _final_norm_loss(x3, target, w["final_norm"], "final_norm_loss")

    g = {"final_norm": d_final}
    sums = {}

    def landed(names, parts):
        sums.update(zip(names, parts))

    dx2, g["ffn2_norm"], df2, dg2, du2 = _ffn_bwd(dx3, x2, w["ffn2_norm"], s2, p2, w["ffn2_w_gate"],
                                                   w["ffn2_w_up"], w["ffn2_w_down"], "ffn2_bwd")
    def ffn_wgrad(names, hids, tok, kernel_name, cargos=()):
        parts, cargo_outs = _wgrad_hid_tok_scatter(hids, tok, kernel_name, cargos=cargos)
        landed(names, parts)
        return cargo_outs

    ffn_wgrad(["ffn2_w_gate", "ffn2_w_up"], [dg2, du2], h3, "ffn2_dw_gate_up")
    ffn_wgrad(["ffn2_w_down"], [a2], df2, "ffn2_dw_down")
    (dx1, dproj, g_pw, g["pool_w"], g_dw, g["conv_dw_b"], g["conv_ln_g"], g["conv_ln_b"], g["pool_scale"],
     g["mix_norm"]), (swapped2,) = _mix_bwd(
        dx2, u1, u3, mixed, proj, x1, w["mix_norm"], w["conv_dw"], w["conv_ln_g"], w["conv_ln_b"], w["conv_pw"],
        w["pool_w"], w["pool_scale"], w["w_out"], w["w_in"], "mix_bwd",
        cargos=[Cargo("swap", [sums["ffn2_" + k] for k in FFN_W])])
    g_out, _ = _wgrad_2d(cat, dx2, 1, BF16, "dw_out")
    slabs = [g_pw.reshape(N_CHIPS, D_CONV // N_CHIPS, D_CONV),
             g_out.reshape(N_CHIPS, (D_CONV + D_POOL) // N_CHIPS, D_MODEL)]
    g_in, (parts,) = _wgrad_2d(h2, dproj, N_CHIPS, BF16, "dw_in", cargos=[Cargo("scatter_chips", slabs)])
    landed(["conv_pw", "w_out"], parts)
    dx, g["ffn1_norm"], df1, dg1, du1_ = _ffn_bwd(dx1, xt, w["ffn1_norm"], s1, p1, w["ffn1_w_gate"],
                                                   w["ffn1_w_up"], w["ffn1_w_down"], "ffn1_bwd")
    slabs = [g_in, g_dw.reshape(CONV_WIDTH + 1, N_CHIPS, D_CONV // N_CHIPS).transpose(1, 0, 2)]
    (parts,) = ffn_wgrad(["ffn1_w_gate", "ffn1_w_up"], [dg1, du1_], h1, "ffn1_dw_gate_up",
                         cargos=[Cargo("scatter_chips", slabs)])
    landed(["w_in", "conv_dw"], parts)
    swapped_mid, swapped_gate_up, small_parts = ffn_wgrad(
        ["ffn1_w_down"], [a1], df1, "ffn1_dw_down",
        cargos=[Cargo("swap", [sums[k] for k in MID]), Cargo("swap", [sums["ffn1_w_gate"], sums["ffn1_w_up"]]),
                Cargo("gather_devices", [_pack_small(g, spare=loss_share)])])
    swapped_down = _exchange(Cargo("swap", [sums["ffn1_w_down"]]), "swap_last")

    theirs = dict(zip(["ffn2_" + k for k in FFN_W], swapped2))
    theirs.update(zip(MID, swapped_mid))
    theirs.update(ffn1_w_gate=swapped_gate_up[0], ffn1_w_up=swapped_gate_up[1], ffn1_w_down=swapped_down[0])
    grads, deltas, new_m, new_v = {}, {}, {}, {}
    for k in theirs:
        res = _adamw([sums[k], theirs[k]], _as_stored(k, wts[k]), _as_stored(k, mom_m[k]),
                     _as_stored(k, mom_v[k]), "adamw_" + k)
        grads[k], deltas[k], new_m[k], new_v[k] = [_as_given(k, t) for t in res]
    res = _adamw(small_parts, _pack_small(wts), _pack_small(mom_m), _pack_small(mom_v), "adamw_small")
    for dst, packed in zip((grads, deltas, new_m, new_v), res):
        dst.update(_unpack_small(packed))
    loss = res[0][PACK_LOSS_ROW, 0]

    out = [loss, dx[None]]
    for group in (grads, deltas, new_m, new_v):
        out += [group[k] for k in WEIGHTS]
    return tuple(out)
```

```python
import functools

import jax
import jax.numpy as jnp
from jax import lax
from jax.experimental import pallas as pl
from jax.experimental.pallas import tpu as pltpu

F32 = jnp.float32
BF16 = jnp.bfloat16
MESH = pl.DeviceIdType.MESH

N_CHIPS = 4
N_DEV = 8
D_MODEL = 1024
D_CONV = 512
D_POOL = 512
CONV_WIDTH = 31
POOL_WINDOWS = (2, 4, 8, 16)
POOL_GROUP = 128
D_IN = 2 * D_CONV + D_POOL
HALO = 32
RMS_EPS = 1e-6
LN_EPS = 1e-5
FFN_RES_WEIGHT = 0.5
ADAM_LR = 0.001
ADAM_B1 = 0.9
ADAM_B2 = 0.999
ADAM_EPS = 1e-08
ADAM_WD = 0.01
ADAM_STEP = 10
VMEM_LIMIT_BYTES = 52 * 1024 * 1024
VMEM_LIMIT_BYTES_LARGE = 58 * 1024 * 1024
TM_FFN = 512
TM_MIX = 256
TM_MIX_FWD = 512
TT_WGRAD = 2048
STRIP = 16
SLOTS_PER_STEP = 2
SLOTS_PER_STEP_FWD = 4
SUBLANES = 8
RELAY_AT_EIGHTHS = 7

HBM = pl.BlockSpec(memory_space=pl.ANY)


def _dot(a, b):
    return jnp.dot(a, b, preferred_element_type=F32)


def _dot_nt(a, b):
    return lax.dot_general(a, b, (((1,), (1,)), ((), ())), preferred_element_type=F32)


def _dot_tn(a, b):
    return lax.dot_general(a, b, (((0,), (0,)), ((), ())), preferred_element_type=F32)


def _sds(shape, dtype):
    return jax.ShapeDtypeStruct(shape, dtype)


def _rms_stats(xv):
    r = lax.rsqrt(jnp.mean(xv * xv, axis=-1, keepdims=True) + RMS_EPS)
    return r, xv * r


def _swiglu_saved(gate, up):
    sig = jax.nn.sigmoid(gate)
    silu = gate * sig
    return silu, up * (sig * (1.0 + gate * (1.0 - sig))), silu * up


def _rms_bwd(dh, n, r, gain):
    dn = dh * gain
    return r * (dn - n * jnp.mean(dn * n, axis=-1, keepdims=True))


def _place():
    x, y, c = lax.axis_index("x"), lax.axis_index("y"), lax.axis_index("c")
    return x, y, c, [(1 - x, y), (x, 1 - y), (1 - x, 1 - y)]


class Cargo:
    def __init__(self, kind, arrays):
        self.kind, self.arrays = kind, list(arrays)
        n = len(self.arrays)
        self.two_level = kind in ("gather_slots", "gather_chips")
        if self.two_level:
            self.out_shape = [_sds((N_CHIPS,) + a.shape, a.dtype) for a in self.arrays]
        elif kind == "gather_devices":
            self.out_shape = [_sds((N_DEV,) + a.shape, a.dtype) for a in self.arrays]
        else:
            self.out_shape = [_sds(a.shape, a.dtype) for a in self.arrays]
        n_remote = n * {"swap": 1, "gather_devices": N_DEV - 1}.get(kind, N_CHIPS - 1)
        n_own = 0 if kind == "swap" else n
        n_relay = n_remote if self.two_level else 0
        dma = pltpu.SemaphoreType.DMA
        self.scratch = [dma((n_remote,)), dma((n_remote,)), dma((max(n_own, 1),)),
                        dma((max(n_relay, 1),)), dma((max(n_relay, 1),))]

    def _plan(self, ins, outs):
        x, y, c, chips = _place()
        q = 2 * x + y
        sibling = (x, y, 1 - c)
        own, remote, relays = [], [], []
        for a, o in zip(ins, outs):
            if self.two_level:
                half = a.shape[0] // 2
                mine = pl.ds(pl.multiple_of(c * half, SUBLANES), half)
                theirs = pl.ds(pl.multiple_of((1 - c) * half, SUBLANES), half)
                own.append((a, o.at[0 if self.kind == "gather_slots" else q]))
                for j, (px, py) in enumerate(chips):
                    there, here = (j + 1, j + 1) if self.kind == "gather_slots" else (q, 2 * px + py)
                    remote.append((a.at[mine], o.at[there, mine], o.at[here, mine], (px, py, c)))
                    relays.append((o.at[here, mine], o.at[here, mine], o.at[here, theirs], sibling))
            elif self.kind == "scatter_chips":
                own.append((a.at[q], o.at[q]))
                remote += [(a.at[2 * px + py], o.at[q], o.at[2 * px + py], (px, py, c)) for px, py in chips]
            elif self.kind == "swap":
                remote.append((a, o, o, sibling))
            else:
                own.append((a, o.at[4 * x + 2 * y + c]))
                for k in range(1, N_DEV):
                    px, py, pc = x ^ (k >> 2 & 1), y ^ (k >> 1 & 1), c ^ (k & 1)
                    remote.append((a, o.at[4 * x + 2 * y + c], o.at[4 * px + 2 * py + pc], (px, py, pc)))
        return own, remote, relays

    @staticmethod
    def _copies(entries, send_sems, recv_sems):
        out = []
        for k, (src, dst, landed, peer) in enumerate(entries):
            def make(dst_ref, k=k, src=src, peer=peer):
                return pltpu.make_async_remote_copy(src_ref=src, dst_ref=dst_ref, send_sem=send_sems.at[k],
                                                    recv_sem=recv_sems.at[k], device_id=peer, device_id_type=MESH)
            out.append((make(dst), make(landed)))
        return out

    def start(self, ins, outs, sems):
        own, remote, _ = self._plan(ins, outs)
        for k, (src, dst) in enumerate(own):
            pltpu.make_async_copy(src, dst, sems[2].at[k]).start()
        for mine, _ in self._copies(remote, sems[0], sems[1]):
            mine.start()

    def relay(self, ins, outs, sems):
        _, remote, relays = self._plan(ins, outs)
        passed = self._copies(relays, sems[3], sems[4])
        for (_, arriving), (mine, _) in zip(self._copies(remote, sems[0], sems[1]), passed):
            arriving.wait_recv()
            mine.start()

    def wait(self, ins, outs, sems):
        own, remote, relays = self._plan(ins, outs)
        for mine, arriving in self._copies(remote, sems[0], sems[1]):
            mine.wait_send()
            if not self.two_level:
                arriving.wait_recv()
        for mine, arriving in self._copies(relays, sems[3], sems[4]):
            mine.wait_send()
            arriving.wait_recv()
        for k, (src, dst) in enumerate(own):
            pltpu.make_async_copy(src, dst, sems[2].at[k]).wait()


N_CARGO_SEMS = 5


def _call(body, *, name, grid, in_specs, out_specs, out_shape, args, scratch_shapes=(), cargos=(),
          vmem_limit_bytes=VMEM_LIMIT_BYTES):
    n_in, n_out, n_scr = len(in_specs), len(out_specs), len(scratch_shapes)
    c_in = [len(cg.arrays) for cg in cargos]
    n_cin = sum(c_in)

    def wrapped(*refs):
        ins = refs[:n_in]
        cins = refs[n_in:n_in + n_cin]
        outs = refs[n_in + n_cin:n_in + n_cin + n_out]
        couts = refs[n_in + n_cin + n_out:n_in + 2 * n_cin + n_out]
        scr = refs[n_in + 2 * n_cin + n_out:n_in + 2 * n_cin + n_out + n_scr]
        sems = refs[n_in + 2 * n_cin + n_out + n_scr:]
        step, n_steps = 0, 1
        for ax, size in enumerate(grid):
            step = step * size + pl.program_id(ax)
            n_steps *= size

        def each(method, only_two_level=False):
            at = 0
            for k, cg in enumerate(cargos):
                if cg.two_level or not only_two_level:
                    getattr(cg, method)(cins[at:at + c_in[k]], couts[at:at + c_in[k]],
                                        sems[N_CARGO_SEMS * k:N_CARGO_SEMS * (k + 1)])
                at += c_in[k]

        body(*ins, *outs, *scr)
        if cargos:
            pl.when(step == 0)(lambda: each("start"))
        if any(cg.two_level for cg in cargos):
            pl.when(step == (RELAY_AT_EIGHTHS * n_steps) // 8)(lambda: each("relay", only_two_level=True))
        if cargos:
            pl.when(step == n_steps - 1)(lambda: each("wait"))

    res = pl.pallas_call(
        wrapped, name=name, grid=grid,
        in_specs=list(in_specs) + [HBM] * n_cin,
        out_specs=list(out_specs) + [HBM] * n_cin,
        out_shape=list(out_shape) + [s for cg in cargos for s in cg.out_shape],
        scratch_shapes=list(scratch_shapes) + [s for cg in cargos for s in cg.scratch],
        compiler_params=pltpu.CompilerParams(dimension_semantics=("arbitrary",) * len(grid),
                                             vmem_limit_bytes=vmem_limit_bytes),
    )(*args, *[a for cg in cargos for a in cg.arrays])
    outs, rest = list(res[:n_out]), list(res[n_out:])
    cargo_outs = []
    for k in c_in:
        cargo_outs.append(rest[:k])
        rest = rest[k:]
    return outs, cargo_outs


def _exchange(cargo, name):
    _, (outs,) = _call(lambda: None, name=name, grid=(1,), in_specs=[], out_specs=[], out_shape=[], args=[],
                       cargos=[cargo])
    return outs


def _ffn_up_gather(x, gain, wg_t, wu_t, name, cargos=()):
    t_len, d = x.shape
    fq = wg_t.shape[0]
    tm = min(TM_FFN, t_len)
    n_tiles = t_len // tm
    relay_tile = n_tiles // 2
    fetch_tile = min(relay_tile + 1, n_tiles - 1)

    def body(x_ref, g_ref, wg_in, wu_in, h_ref, s_ref, p_ref, a_ref, wg_all, wu_all,
             wg_v, wu_v, h_all, send_sems, recv_sems, pass_send_sems, pass_recv_sems, own_sems, load_sems):
        s = pl.program_id(0)
        i = pl.program_id(1)
        x_, y_, c_, chips = _place()
        shards = ((wg_in, wg_all, wg_v), (wu_in, wu_all, wu_v))
        mine = pl.ds(pl.multiple_of(c_ * (fq // 2), SUBLANES), fq // 2)
        theirs = pl.ds(pl.multiple_of((1 - c_) * (fq // 2), SUBLANES), fq // 2)

        def to_peer(k, j):
            w_in, w_all, _ = shards[k]
            return pltpu.make_async_remote_copy(
                src_ref=w_in.at[mine], dst_ref=w_all.at[j + 1, mine], send_sem=send_sems.at[3 * k + j],
                recv_sem=recv_sems.at[3 * k + j], device_id=(*chips[j], c_), device_id_type=MESH)

        def to_sibling(k, j, landing=False):
            w_all = shards[k][1]
            return pltpu.make_async_remote_copy(
                src_ref=w_all.at[j + 1, mine], dst_ref=w_all.at[j + 1, theirs if landing else mine],
                send_sem=pass_send_sems.at[3 * k + j], recv_sem=pass_recv_sems.at[3 * k + j],
                device_id=(x_, y_, 1 - c_), device_id_type=MESH)

        def keep(k):
            return pltpu.make_async_copy(shards[k][0], shards[k][1].at[0], own_sems.at[k])

        @pl.when((s == 0) & (i == 0))
        def _():
            for j in range(N_CHIPS - 1):
                for k in range(2):
                    to_peer(k, j).start()
            for k in range(2):
                keep(k).start()

        def load(k, slot):
            src = shards[k][0] if slot == 0 else shards[k][1].at[slot]
            return pltpu.make_async_copy(src, shards[k][2].at[slot % 2], load_sems.at[k])

        @pl.when((s == 0) & (i == 0))
        def _():
            for k in range(2):
                load(k, 0).start()
            for k in range(2):
                load(k, 0).wait()

        def pass_on(slot):
            for k in range(2):
                to_peer(k, slot - 1).wait_recv()
                to_sibling(k, slot - 1).start()

        def fetch(slot):
            for k in range(2):
                to_sibling(k, slot - 1, landing=True).wait_recv()
                load(k, slot).start()

        for slot in range(1, N_CHIPS):
            pl.when((s == slot - 1) & (i == relay_tile))(functools.partial(pass_on, slot))
            pl.when((s == slot - 1) & (i == fetch_tile))(functools.partial(fetch, slot))

            @pl.when((s == slot) & (i == 0))
            def _():
                for k in range(2):
                    load(k, slot).wait()

        @pl.when(s == 0)
        def _():
            _, n = _rms_stats(x_ref[...])
            h_new = (n * g_ref[...]).astype(BF16)
            h_ref[...] = h_new
            h_all[i] = h_new

        h = h_all[i]
        silu, dgate, act = _swiglu_saved(_dot_nt(h, wg_v[s % 2]), _dot_nt(h, wu_v[s % 2]))
        s_ref[...] = silu.astype(BF16)
        p_ref[...] = dgate.astype(BF16)
        a_ref[...] = act.astype(BF16)

        @pl.when((s == N_CHIPS - 1) & (i == n_tiles - 1))
        def _():
            for k in range(2):
                for j in range(N_CHIPS - 1):
                    to_peer(k, j).wait_send()
                    to_sibling(k, j).wait_send()
                keep(k).wait()

    tok = pl.BlockSpec((tm, d), lambda s, i: (jnp.where(s == 0, i, n_tiles - 1), 0))
    hid = pl.BlockSpec((None, tm, fq), lambda s, i: (s, i, 0))
    outs, cargo_outs = _call(
        body, name=name, grid=(N_CHIPS, n_tiles),
        in_specs=[tok, pl.BlockSpec((1, d), lambda s, i: (0, 0)), HBM, HBM],
        out_specs=[tok, hid, hid, hid, HBM, HBM],
        out_shape=[_sds((t_len, d), BF16)] + [_sds((N_CHIPS, t_len, fq), BF16)] * 3
        + [_sds((N_CHIPS, fq, d), BF16)] * 2,
        scratch_shapes=[pltpu.VMEM((2, fq, d), BF16), pltpu.VMEM((2, fq, d), BF16),
                        pltpu.VMEM((n_tiles, tm, d), BF16)]
        + [pltpu.SemaphoreType.DMA((6,))] * 4 + [pltpu.SemaphoreType.DMA((2,))] * 2,
        args=[x, gain, wg_t, wu_t], cargos=cargos)
    return outs, cargo_outs


def _load_once(hbm_refs, vmem_refs, sems, first):
    @pl.when(first)
    def _():
        copies = [pltpu.make_async_copy(src, dst, sems.at[k]) for k, (src, dst) in enumerate(zip(hbm_refs, vmem_refs))]
        for cp in copies:
            cp.start()
        for cp in copies:
            cp.wait()


def _ffn_down(x, act, wd, name, cargos=()):
    t_len, d = x.shape
    nq, fq, _ = wd.shape
    tm = min(TM_FFN, t_len)

    def body(x_ref, a_ref, wd_ref, xo_ref):
        y = _dot(a_ref[0], wd_ref[0])
        for j in range(1, nq):
            y = y + _dot(a_ref[j], wd_ref[j])
        xo_ref[...] = x_ref[...] + FFN_RES_WEIGHT * y

    tok = pl.BlockSpec((tm, d), lambda i: (i, 0))
    (xo,), cargo_outs = _call(
        body, name=name, grid=(t_len // tm,),
        in_specs=[tok, pl.BlockSpec((nq, tm, fq), lambda i: (0, i, 0)), pl.BlockSpec((nq, fq, d), lambda i: (0, 0, 0))],
        out_specs=[tok], out_shape=[_sds((t_len, d), F32)], args=[x, act, wd], cargos=cargos)
    return xo, cargo_outs


def _ffn_fwd(x, gain, wg_t, wu_t, wd, name):
    t_len, d = x.shape
    nq, fq, _ = wd.shape
    tm = min(TM_FFN, t_len)

    def body(x_ref, g_ref, wg_hbm, wu_hbm, wd_hbm, xo_ref, h_ref, s_ref, p_ref, a_ref,
             h_s, acc, wg_v, wu_v, wd_v, load_sems):
        i = pl.program_id(0)
        j = pl.program_id(1)
        _load_once((wg_hbm, wu_hbm, wd_hbm), (wg_v, wu_v, wd_v), load_sems, (i == 0) & (j == 0))

        @pl.when(j == 0)
        def _():
            _, n = _rms_stats(x_ref[...])
            h = (n * g_ref[...]).astype(BF16)
            h_s[...] = h
            h_ref[...] = h
            acc[...] = jnp.zeros_like(acc)

        h = h_s[...]
        y = None
        for jj in range(SLOTS_PER_STEP_FWD):
            slot = j * SLOTS_PER_STEP_FWD + jj
            silu, dgate, act = _swiglu_saved(_dot_nt(h, wg_v[slot]), _dot_nt(h, wu_v[slot]))
            s_ref[jj] = silu.astype(BF16)
            p_ref[jj] = dgate.astype(BF16)
            a_ref[jj] = act.astype(BF16)
            part = _dot(a_ref[jj], wd_v[slot])
            y = part if y is None else y + part
        acc[...] += y

        @pl.when(j == nq // SLOTS_PER_STEP_FWD - 1)
        def _():
            xo_ref[...] = x_ref[...] + FFN_RES_WEIGHT * acc[...]

    tok = pl.BlockSpec((tm, d), lambda i, j: (i, 0))
    hid = pl.BlockSpec((SLOTS_PER_STEP_FWD, tm, fq), lambda i, j: (j, i, 0))
    outs, _ = _call(
        body, name=name, grid=(t_len // tm, nq // SLOTS_PER_STEP_FWD),
        in_specs=[tok, pl.BlockSpec((1, d), lambda i, j: (0, 0)), HBM, HBM, HBM],
        out_specs=[tok, tok, hid, hid, hid],
        out_shape=[_sds((t_len, d), F32), _sds((t_len, d), BF16)] + [_sds((nq, t_len, fq), BF16)] * 3,
        scratch_shapes=[pltpu.VMEM((tm, d), BF16), pltpu.VMEM((tm, d), F32)]
        + [pltpu.VMEM((nq, fq, d), BF16)] * 3 + [pltpu.SemaphoreType.DMA((3,))],
        args=[x, gain, wg_t, wu_t, wd], vmem_limit_bytes=VMEM_LIMIT_BYTES_LARGE)
    return outs


def _ffn_bwd(dy, x_in, gain, silu, dgate_du, wg_t, wu_t, wd, name):
    t_len, d = dy.shape
    nq, fq, _ = wd.shape
    tm = min(TM_FFN, t_len)

    def body(dy_ref, x_ref, g_ref, s_ref, p_ref, wg_hbm, wu_hbm, wd_hbm,
             dx_ref, dgain_ref, df_ref, dg_ref, du_ref, df_s, dh_acc, dact_s, wg_v, wu_v, wd_v, load_sems):
        i = pl.program_id(0)
        j = pl.program_id(1)
        _load_once((wg_hbm, wu_hbm, wd_hbm), (wg_v, wu_v, wd_v), load_sems, (i == 0) & (j == 0))

        @pl.when((i == 0) & (j == 0))
        def _():
            dgain_ref[...] = jnp.zeros_like(dgain_ref)

        @pl.when(j == 0)
        def _():
            df = (FFN_RES_WEIGHT * dy_ref[...]).astype(BF16)
            df_s[...] = df
            df_ref[...] = df
            dh_acc[...] = jnp.zeros_like(dh_acc)

        slots = [j * SLOTS_PER_STEP + jj for jj in range(SLOTS_PER_STEP)]
        for jj, slot in enumerate(slots):
            dact_s[jj] = _dot_nt(df_s[...], wd_v[slot])

        for jj in range(SLOTS_PER_STEP):
            for r0 in range(0, tm, STRIP):
                rows = slice(r0, r0 + STRIP)
                dact = dact_s[jj, rows, :]
                dg_ref[jj, rows, :] = (dact * p_ref[jj, rows, :].astype(F32)).astype(BF16)
                du_ref[jj, rows, :] = (dact * s_ref[jj, rows, :].astype(F32)).astype(BF16)

        dh = None
        for jj, slot in enumerate(slots):
            part = _dot(dg_ref[jj], wg_v[slot]) + _dot(du_ref[jj], wu_v[slot])
            dh = part if dh is None else dh + part
        dh_acc[...] += dh

        @pl.when(j == nq // SLOTS_PER_STEP - 1)
        def _():
            r, n = _rms_stats(x_ref[...])
            dh = dh_acc[...]
            dgain_ref[...] += jnp.sum(dh * n, axis=0, keepdims=True)
            dx_ref[...] = dy_ref[...] + _rms_bwd(dh, n, r, g_ref[...])

    tok = pl.BlockSpec((tm, d), lambda i, j: (i, 0))
    vec = pl.BlockSpec((1, d), lambda i, j: (0, 0))
    hid = pl.BlockSpec((SLOTS_PER_STEP, tm, fq), lambda i, j: (j, i, 0))
    outs, _ = _call(
        body, name=name, grid=(t_len // tm, nq // SLOTS_PER_STEP),
        in_specs=[tok, tok, vec, hid, hid, HBM, HBM, HBM],
        out_specs=[tok, vec, tok, hid, hid],
        out_shape=[_sds((t_len, d), F32), _sds((1, d), F32), _sds((t_len, d), BF16),
                   _sds((nq, t_len, fq), BF16), _sds((nq, t_len, fq), BF16)],
        scratch_shapes=[pltpu.VMEM((tm, d), BF16), pltpu.VMEM((tm, d), F32),
                        pltpu.VMEM((SLOTS_PER_STEP, tm, fq), F32)]
        + [pltpu.VMEM((nq, fq, d), BF16)] * 3 + [pltpu.SemaphoreType.DMA((3,))],
        args=[dy, x_in, gain, silu, dgate_du, wg_t, wu_t, wd], vmem_limit_bytes=VMEM_LIMIT_BYTES_LARGE)
    return outs


def _wgrad(lhs, rhs, l_spec, r_spec, out_shape, out_spec, acc_shape, grid, name, cargos=()):
    n_t = grid[-1]
    t_axis = len(grid) - 1

    def body(l_ref, r_ref, o_ref, acc):
        t = pl.program_id(t_axis)

        @pl.when(t == 0)
        def _():
            acc[...] = jnp.zeros_like(acc)

        acc[...] += _dot_tn(l_ref[...].astype(BF16), r_ref[...].astype(BF16))

        @pl.when(t == n_t - 1)
        def _():
            o_ref[...] = acc[...].astype(o_ref.dtype)

    (out,), cargo_outs = _call(
        body, name=name, grid=grid, in_specs=[l_spec, r_spec], out_specs=[out_spec], out_shape=[out_shape],
        scratch_shapes=[pltpu.VMEM(acc_shape, F32)], args=[lhs, rhs], cargos=cargos)
    return out, cargo_outs


def _wgrad_hid_tok_scatter(hids, tok, name, cargos=()):
    t_len, d = tok.shape
    n_w = len(hids)
    nq, _, fq = hids[0].shape
    half = fq // 2
    tt = min(TT_WGRAD, t_len)
    n_t = t_len // tt
    per_w = 4
    n_sem = 6

    def body(*refs):
        l_refs, r_ref, parts_refs = refs[:n_w], refs[n_w], refs[n_w + 1:2 * n_w + 1]
        scr = refs[2 * n_w + 1:]
        bufs = [scr[per_w * w:per_w * (w + 1)] for w in range(n_w)]
        zeros = scr[per_w * n_w]
        sems = [scr[per_w * n_w + 1 + n_sem * w:per_w * n_w + 1 + n_sem * (w + 1)] for w in range(n_w)]
        g = pl.program_id(0)
        t = pl.program_id(1)
        x_, y_, c_, chips = _place()
        mine = pl.ds(pl.multiple_of(c_ * half, STRIP), half)
        theirs = pl.ds(pl.multiple_of((1 - c_) * half, STRIP), half)

        def to_sibling(w, slot):
            return pltpu.make_async_remote_copy(
                src_ref=bufs[w][1].at[theirs], dst_ref=bufs[w][2].at[slot], send_sem=sems[w][0].at[slot],
                recv_sem=sems[w][1].at[slot], device_id=(x_, y_, 1 - c_), device_id_type=MESH)

        def to_peer(w, j):
            return pltpu.make_async_remote_copy(
                src_ref=bufs[w][3].at[j + 1], dst_ref=parts_refs[w].at[j + 1, mine], send_sem=sems[w][2].at[j],
                recv_sem=sems[w][3].at[j], device_id=(*chips[j], c_), device_id_type=MESH)

        def keep(w):
            return pltpu.make_async_copy(bufs[w][3].at[0], parts_refs[w].at[0, mine], sems[w][4])

        def blank(w, slot):
            return pltpu.make_async_copy(zeros, parts_refs[w].at[slot, theirs], sems[w][5].at[slot])

        @pl.when((g == 0) & (t == 0))
        def _():
            zeros[...] = jnp.zeros_like(zeros)
            for w in range(n_w):
                for slot in range(nq):
                    blank(w, slot).start()

        @pl.when(t == 0)
        def _():
            for w in range(n_w):
                bufs[w][0][...] = jnp.zeros_like(bufs[w][0])

        rhs = r_ref[...]
        for w in range(n_w):
            bufs[w][0][...] += _dot_tn(l_refs[w][...], rhs)

        for step in range(nq):
            slot = (step + 1) % nq

            @pl.when((g == step) & (t == n_t - 1))
            def _():
                for w in range(n_w):
                    acc, stage, _, _ = bufs[w]
                    if step > 0:
                        to_sibling(w, step).wait_send()
                    stage[...] = acc[...].astype(BF16)
                    to_sibling(w, slot).start()
                for w in range(n_w):
                    _, stage, pair, summed = bufs[w]
                    to_sibling(w, slot).wait_recv()
                    summed[slot] = (stage[mine, :].astype(F32) + pair[slot].astype(F32)).astype(BF16)
                    if slot > 0:
                        to_peer(w, slot - 1).start()
                    else:
                        keep(w).start()

        @pl.when((g == nq - 1) & (t == n_t - 1))
        def _():
            for w in range(n_w):
                for j in range(N_CHIPS - 1):
                    to_peer(w, j).wait()
                keep(w).wait()
                to_sibling(w, 0).wait_send()
                for slot in range(nq):
                    blank(w, slot).wait()

    dma = pltpu.SemaphoreType.DMA
    scratch = []
    for _ in range(n_w):
        scratch += [pltpu.VMEM((fq, d), F32), pltpu.VMEM((fq, d), BF16), pltpu.VMEM((nq, half, d), BF16),
                    pltpu.VMEM((nq, half, d), BF16)]
    scratch.append(pltpu.VMEM((half, d), BF16))
    for _ in range(n_w):
        scratch += [dma((nq,)), dma((nq,)), dma((N_CHIPS - 1,)), dma((N_CHIPS - 1,)), dma(()), dma((nq,))]
    parts, cargo_outs = _call(
        body, name=name, grid=(nq, n_t),
        in_specs=[pl.BlockSpec((None, tt, fq), lambda g, t: ((g + 1) % nq, t, 0))] * n_w
        + [pl.BlockSpec((tt, d), lambda g, t: (t, 0))],
        out_specs=[HBM] * n_w, out_shape=[_sds((nq, fq, d), BF16)] * n_w,
        scratch_shapes=scratch, args=[*hids, tok], cargos=cargos)
    return parts, cargo_outs


def _wgrad_2d(lhs, rhs, n_col_blocks, out_dtype, name, cargos=()):
    t_len, k = lhs.shape
    n = rhs.shape[1]
    nb = n // n_col_blocks
    tt = min(TT_WGRAD, t_len)
    return _wgrad(lhs, rhs,
                  pl.BlockSpec((tt, k), lambda q, t: (t, 0)),
                  pl.BlockSpec((tt, nb), lambda q, t: (t, q)),
                  _sds((n_col_blocks, k, nb), out_dtype),
                  pl.BlockSpec((None, k, nb), lambda q, t: (q, 0, 0)),
                  (k, nb), (n_col_blocks, t_len // tt), name, cargos)


def _layernorm_stats(u1):
    mu = jnp.mean(u1, axis=-1, keepdims=True)
    xc = u1 - mu
    rstd = lax.rsqrt(jnp.mean(xc * xc, axis=-1, keepdims=True) + LN_EPS)
    return rstd, xc * rstd


def _positions(i, tm, rows, offset=0):
    return (lax.broadcasted_iota(jnp.int32, (rows, 1), 0) + (i * tm + offset)).astype(F32)


SHIFT_ROWS = HALO - SUBLANES


def _fill_shifted(ext_s, sh_s, tm):
    for b in range(1, SUBLANES):
        sh_s[b - 1] = ext_s[pl.ds(b, tm + SHIFT_ROWS), :]


def _window(ext_s, sh_s, shift, tm):
    a, b = divmod(shift, SUBLANES)
    if b == 0:
        return ext_s[pl.ds(shift, tm), :]
    return sh_s[b - 1, pl.ds(a * SUBLANES, tm), :]


def _window_sums(ext_s, lv_a, lv_b, tm, ahead):
    g = POOL_GROUP
    sign = 1 if ahead else -1
    for n, (dst, src, c0) in enumerate(((lv_a, ext_s, 0), (lv_b, lv_a, g), (lv_a, lv_b, 2 * g)), start=1):
        lo = 0 if ahead else n * SUBLANES
        rows = tm + HALO - n * SUBLANES
        shift = sign * 2 ** (n - 1)
        dst[pl.ds(lo, rows), c0:] = src[pl.ds(lo, rows), c0:] + src[pl.ds(lo + shift, rows), c0:]
    base = 0 if ahead else HALO
    rows = pl.ds(base, tm)
    far = pl.ds(base + sign * SUBLANES, tm)
    return [lv_a[rows, 0:g], lv_b[rows, g:2 * g], lv_a[rows, 2 * g:3 * g],
            lv_a[rows, 3 * g:] + lv_a[far, 3 * g:]]


def _tile(tm, cols):
    return pl.BlockSpec((tm, cols), lambda i: (i, 0))


def _whole(shape):
    return pl.BlockSpec(shape, lambda i: (0,) * len(shape))


def _mix_fwd(x1, gain, w_in, conv_dw, conv_b, ln_g, ln_b, conv_pw, pool_w, pool_scale, w_out, name, cargos=()):
    t_len, d = x1.shape
    nq, _, nb = w_in.shape
    tm = min(TM_MIX_FWD, t_len)

    def body(x_ref, g_ref, wi_ref, dw_ref, cb_ref, lg_ref, lb_ref, pw_ref, plw_ref, ps_ref, wo_ref,
             x2_ref, h_ref, p_ref, u1_ref, u3_ref, mx_ref, cat_ref, ext_s, pext_s, sh_s, tail_s, lva_s, lvb_s):
        i = pl.program_id(0)

        @pl.when(i == 0)
        def _():
            tail_s[...] = jnp.zeros_like(tail_s)

        _, n = _rms_stats(x_ref[...])
        h = (n * g_ref[...]).astype(BF16)
        h_ref[...] = h
        for q in range(nq):
            p_ref[:, q * nb:(q + 1) * nb] = _dot(h, wi_ref[q])

        a = p_ref[:, 0:D_CONV]
        g = p_ref[:, D_CONV:2 * D_CONV]
        p = p_ref[:, 2 * D_CONV:]
        ext_s[0:HALO, :] = tail_s[:, 0:D_CONV] * jax.nn.sigmoid(tail_s[:, D_CONV:2 * D_CONV])
        ext_s[HALO:, :] = a * jax.nn.sigmoid(g)
        pext_s[0:HALO, :] = tail_s[:, 2 * D_CONV:]
        pext_s[HALO:, :] = p
        tail_s[...] = p_ref[tm - HALO:tm, :]

        _fill_shifted(ext_s, sh_s, tm)
        u1 = jnp.broadcast_to(cb_ref[...], (tm, D_CONV))
        for k in range(CONV_WIDTH):
            u1 = u1 + dw_ref[k:k + 1, :] * _window(ext_s, sh_s, HALO - (CONV_WIDTH - 1) + k, tm)
        u1_ref[...] = u1
        _, nhat = _layernorm_stats(u1)
        u2 = nhat * lg_ref[...] + lb_ref[...]
        u3 = (u2 * jax.nn.sigmoid(u2)).astype(BF16)
        u3_ref[...] = u3
        cat_ref[:, 0:D_CONV] = _dot(u3, pw_ref[...]).astype(BF16)

        pos1 = _positions(i, tm, tm) + 1.0
        sums = _window_sums(pext_s, lva_s, lvb_s, tm, ahead=False)
        for gi, w in enumerate(POOL_WINDOWS):
            cols = slice(gi * POOL_GROUP, (gi + 1) * POOL_GROUP)
            mixed = (sums[gi] / jnp.minimum(pos1, float(w)) - p[:, cols]).astype(BF16)
            mx_ref[:, cols] = mixed
            out = _dot(mixed, plw_ref[gi]) * ps_ref[:, cols]
            cat_ref[:, D_CONV + gi * POOL_GROUP:D_CONV + (gi + 1) * POOL_GROUP] = out.astype(BF16)

        x2_ref[...] = x_ref[...] + _dot(cat_ref[...], wo_ref[...])

    return _call(
        body, name=name, grid=(t_len // tm,),
        in_specs=[_tile(tm, d), _whole((1, d)), _whole((nq, d, nb)), _whole((CONV_WIDTH + 1, D_CONV)),
                  _whole((1, D_CONV)), _whole((1, D_CONV)), _whole((1, D_CONV)), _whole((D_CONV, D_CONV)),
                  _whole((4, POOL_GROUP, POOL_GROUP)), _whole((1, D_POOL)), _whole((D_CONV + D_POOL, d))],
        out_specs=[_tile(tm, d), _tile(tm, d), _tile(tm, D_IN), _tile(tm, D_CONV), _tile(tm, D_CONV),
                   _tile(tm, D_POOL), _tile(tm, D_CONV + D_POOL)],
        out_shape=[_sds((t_len, d), F32), _sds((t_len, d), BF16), _sds((t_len, D_IN), F32),
                   _sds((t_len, D_CONV), F32), _sds((t_len, D_CONV), BF16), _sds((t_len, D_POOL), BF16),
                   _sds((t_len, D_CONV + D_POOL), BF16)],
        scratch_shapes=[pltpu.VMEM((tm + HALO, D_CONV), F32), pltpu.VMEM((tm + HALO, D_POOL), F32),
                        pltpu.VMEM((SUBLANES - 1, tm + SHIFT_ROWS, D_CONV), F32), pltpu.VMEM((HALO, D_IN), F32)]
        + [pltpu.VMEM((tm + HALO, D_POOL), F32)] * 2,
        args=[x1, gain, w_in, conv_dw, conv_b, ln_g, ln_b, conv_pw, pool_w, pool_scale, w_out], cargos=cargos,
        vmem_limit_bytes=VMEM_LIMIT_BYTES_LARGE)


def _mix_bwd(dx2, u1, u3, mixed, proj, x1, gain, conv_dw, ln_g, ln_b, conv_pw, pool_w, pool_scale, w_out, w_in,
             name, cargos=()):
    t_len, d = x1.shape
    nq, _, nb = w_in.shape
    tm = min(TM_MIX, t_len)
    hb = tm // HALO
    n_tiles = t_len // tm

    def body(dxn_ref, u1_ref, u3_ref, mx_ref, p_ref, tail_ref, x_ref, dx2_ref, g_ref, dw_ref, lg_ref, lb_ref, pw_ref,
             plw_ref, ps_ref, wo_ref, wi_ref,
             dx1_ref, dp_ref, dpw_ref, dplw_ref, ddw_ref, dcb_ref, dlg_ref, dlb_ref, dps_ref, dgain_ref,
             du_s, dm_s, uext_s, dext_s, mext_s, ush_s, dsh_s, lva_s, lvb_s):
        k = pl.program_id(0)

        @pl.when(k == 0)
        def _():
            for ref in (dpw_ref, dplw_ref, ddw_ref, dcb_ref, dlg_ref, dlb_ref, dps_ref, dgain_ref, du_s, dm_s):
                ref[...] = jnp.zeros_like(ref)

        counts = jnp.where(k < n_tiles, 1.0, 0.0)
        dcat = _dot_nt(dxn_ref[...].astype(BF16), wo_ref[...])
        dco = dcat[:, 0:D_CONV].astype(BF16)
        dpw_ref[...] += _dot_tn(u3_ref[...], (dcat[:, 0:D_CONV] * counts).astype(BF16))
        du3 = _dot_nt(dco, pw_ref[...])
        rstd, nhat = _layernorm_stats(u1_ref[...])
        u2 = nhat * lg_ref[...] + lb_ref[...]
        sig = jax.nn.sigmoid(u2)
        du2 = du3 * (sig * (1.0 + u2 * (1.0 - sig)))
        dlg_ref[...] += counts * jnp.sum(du2 * nhat, axis=0, keepdims=True)
        dlb_ref[...] += counts * jnp.sum(du2, axis=0, keepdims=True)
        dnhat = du2 * lg_ref[...]
        du_s[k % 2] = rstd * (dnhat - jnp.mean(dnhat, axis=-1, keepdims=True)
                              - nhat * jnp.mean(dnhat * nhat, axis=-1, keepdims=True))
        for gi in range(len(POOL_WINDOWS)):
            cols = slice(gi * POOL_GROUP, (gi + 1) * POOL_GROUP)
            dpo = dcat[:, D_CONV + gi * POOL_GROUP:D_CONV + (gi + 1) * POOL_GROUP]
            pre = _dot(mx_ref[:, cols], plw_ref[gi])
            dps_ref[:, cols] += counts * jnp.sum(dpo * pre, axis=0, keepdims=True)
            dout = dpo * ps_ref[:, cols]
            dplw_ref[gi] += _dot_tn(mx_ref[:, cols], (dout * counts).astype(BF16))
            dm_s[k % 2, :, cols] = _dot_nt(dout.astype(BF16), plw_ref[gi])

        i = jnp.maximum(k - 1, 0)
        cur, nxt = (k + 1) % 2, k % 2
        first = k <= 1
        last = (k == n_tiles) | (k == 0)
        a = p_ref[:, 0:D_CONV]
        g = p_ref[:, D_CONV:2 * D_CONV]
        sg = jax.nn.sigmoid(g)
        ta = tail_ref[:, 0:D_CONV]
        tg = tail_ref[:, D_CONV:2 * D_CONV]
        uext_s[0:HALO, :] = jnp.where(first, 0.0, ta * jax.nn.sigmoid(tg))
        uext_s[HALO:, :] = a * sg
        du1 = du_s[cur]
        dext_s[0:tm, :] = du1
        dext_s[tm:, :] = jnp.where(last, 0.0, du_s[nxt, 0:HALO, :])

        _fill_shifted(uext_s, ush_s, tm)
        _fill_shifted(dext_s, dsh_s, tm)
        du0 = jnp.zeros((tm, D_CONV), F32)
        for tap in range(CONV_WIDTH):
            du0 = du0 + dw_ref[tap:tap + 1, :] * _window(dext_s, dsh_s, CONV_WIDTH - 1 - tap, tm)
            ddw_ref[tap:tap + 1, :] += jnp.sum(
                du1 * _window(uext_s, ush_s, HALO - (CONV_WIDTH - 1) + tap, tm), axis=0, keepdims=True)
        dcb_ref[...] += jnp.sum(du1, axis=0, keepdims=True)
        dp_ref[:, 0:D_CONV] = (du0 * sg).astype(BF16)
        dp_ref[:, D_CONV:2 * D_CONV] = (du0 * a * sg * (1.0 - sg)).astype(BF16)

        pos1 = _positions(i, tm, tm) + 1.0
        pos1_next = _positions(i, tm, HALO, offset=tm) + 1.0
        for gi, w in enumerate(POOL_WINDOWS):
            cols = slice(gi * POOL_GROUP, (gi + 1) * POOL_GROUP)
            dm = dm_s[cur, :, cols]
            mext_s[0:tm, cols] = dm / jnp.minimum(pos1, float(w))
            mext_s[tm:, cols] = jnp.where(last, 0.0, dm_s[nxt, 0:HALO, cols] / jnp.minimum(pos1_next, float(w)))
        sums = _window_sums(mext_s, lva_s, lvb_s, tm, ahead=True)
        for gi in range(len(POOL_WINDOWS)):
            cols = slice(gi * POOL_GROUP, (gi + 1) * POOL_GROUP)
            dp_ref[:, 2 * D_CONV + gi * POOL_GROUP:2 * D_CONV + (gi + 1) * POOL_GROUP] = (
                sums[gi] - dm_s[cur, :, cols]).astype(BF16)

        dh = _dot_nt(dp_ref[:, 0:nb], wi_ref[0])
        for q in range(1, nq):
            dh = dh + _dot_nt(dp_ref[:, q * nb:(q + 1) * nb], wi_ref[q])
        r, n = _rms_stats(x_ref[...])
        dgain_ref[...] += jnp.sum(dh * n, axis=0, keepdims=True)
        dx1_ref[...] = dx2_ref[...] + _rms_bwd(dh, n, r, g_ref[...])

    def ahead(cols):
        return pl.BlockSpec((tm, cols), lambda k: (jnp.minimum(k, n_tiles - 1), 0))

    def behind(cols):
        return pl.BlockSpec((tm, cols), lambda k: (jnp.maximum(k - 1, 0), 0))

    vec = _whole((1, D_CONV))
    return _call(
        body, name=name, grid=(n_tiles + 1,),
        in_specs=[ahead(d), ahead(D_CONV), ahead(D_CONV), ahead(D_POOL), behind(D_IN),
                  pl.BlockSpec((HALO, D_IN), lambda k: (jnp.maximum(jnp.maximum(k - 1, 0) * hb - 1, 0), 0)),
                  behind(d), behind(d), _whole((1, d)), _whole((CONV_WIDTH + 1, D_CONV)), vec, vec,
                  _whole((D_CONV, D_CONV)), _whole((4, POOL_GROUP, POOL_GROUP)), vec,
                  _whole((D_CONV + D_POOL, d)), _whole((nq, d, nb))],
        out_specs=[behind(d), behind(D_IN), _whole((D_CONV, D_CONV)), _whole((4, POOL_GROUP, POOL_GROUP)),
                   _whole((CONV_WIDTH + 1, D_CONV)), vec, vec, vec, vec, _whole((1, d))],
        out_shape=[_sds((t_len, d), F32), _sds((t_len, D_IN), BF16), _sds((D_CONV, D_CONV), F32),
                   _sds((4, POOL_GROUP, POOL_GROUP), F32), _sds((CONV_WIDTH + 1, D_CONV), F32), _sds((1, D_CONV), F32),
                   _sds((1, D_CONV), F32), _sds((1, D_CONV), F32), _sds((1, D_POOL), F32), _sds((1, d), F32)],
        scratch_shapes=[pltpu.VMEM((2, tm, D_CONV), F32), pltpu.VMEM((2, tm, D_POOL), F32),
                        pltpu.VMEM((tm + HALO, D_CONV), F32), pltpu.VMEM((tm + HALO, D_CONV), F32),
                        pltpu.VMEM((tm + HALO, D_POOL), F32),
                        pltpu.VMEM((SUBLANES - 1, tm + SHIFT_ROWS, D_CONV), F32),
                        pltpu.VMEM((SUBLANES - 1, tm + SHIFT_ROWS, D_CONV), F32)]
        + [pltpu.VMEM((tm + HALO, D_POOL), F32)] * 2,
        args=[dx2, u1, u3, mixed, proj, proj, x1, dx2, gain, conv_dw, ln_g, ln_b, conv_pw, pool_w, pool_scale,
              w_out, w_in], cargos=cargos, vmem_limit_bytes=VMEM_LIMIT_BYTES_LARGE)


def _final_norm_loss(x3, target, gain, name):
    t_len, d = x3.shape
    tm = min(2 * TM_FFN, t_len)

    def body(x_ref, t_ref, g_ref, dx_ref, loss_ref, dgain_ref):
        @pl.when(pl.program_id(0) == 0)
        def _():
            loss_ref[...] = jnp.zeros_like(loss_ref)
            dgain_ref[...] = jnp.zeros_like(dgain_ref)

        r, n = _rms_stats(x_ref[...])
        err = n * g_ref[...] - t_ref[...]
        per_tok = jnp.sum(err * err, axis=-1, keepdims=True) * (1.0 / d)
        loss_ref[...] += 0.5 * jnp.sum(per_tok, axis=0, keepdims=True)
        dy = err * (1.0 / d)
        dgain_ref[...] += jnp.sum(dy * n, axis=0, keepdims=True)
        dx_ref[...] = _rms_bwd(dy, n, r, g_ref[...])

    tok = pl.BlockSpec((tm, d), lambda i: (i, 0))
    outs, _ = _call(
        body, name=name, grid=(t_len // tm,),
        in_specs=[tok, tok, pl.BlockSpec((1, d), lambda i: (0, 0))],
        out_specs=[tok, pl.BlockSpec((1, 128), lambda i: (0, 0)), pl.BlockSpec((1, d), lambda i: (0, 0))],
        out_shape=[_sds((t_len, d), F32), _sds((1, 128), F32), _sds((1, d), F32)],
        args=[x3, target, gain])
    return outs


def _row_tile(rows):
    return rows // 2 if rows % 64 == 0 else rows


def _adamw_math(w, g, m, v):
    m = ADAM_B1 * m + (1.0 - ADAM_B1) * g
    v = ADAM_B2 * v + (1.0 - ADAM_B2) * (g * g)
    m_hat = m / (1.0 - ADAM_B1 ** ADAM_STEP)
    v_hat = v / (1.0 - ADAM_B2 ** ADAM_STEP)
    delta = -ADAM_LR * (m_hat / (jnp.sqrt(v_hat) + ADAM_EPS) + ADAM_WD * w)
    return delta, m, v


def _adamw(parts, w, m, v, name):
    r, c = w.shape
    n = len(parts)
    tr = _row_tile(r)

    def body(*refs):
        g = None
        for p_ref in refs[:n]:
            s = p_ref[0].astype(F32)
            for k in range(1, p_ref.shape[0]):
                s = s + p_ref[k].astype(F32)
            g = s if g is None else g + s
        w_ref, m_ref, v_ref, g_out, d_out, m_out, v_out = refs[n:]
        delta, nm, nv = _adamw_math(w_ref[...], g, m_ref[...], v_ref[...])
        g_out[...] = g
        d_out[...] = delta
        m_out[...] = nm
        v_out[...] = nv

    blk = pl.BlockSpec((tr, c), lambda i: (i, 0))
    p_specs = [pl.BlockSpec((p.shape[0], tr, c), lambda i: (0, i, 0)) for p in parts]
    outs, _ = _call(body, name=name, grid=(r // tr,), in_specs=p_specs + [blk, blk, blk],
                    out_specs=[blk] * 4, out_shape=[_sds((r, c), F32)] * 4, args=[*parts, w, m, v])
    return outs


FFN_W = ("w_gate", "w_up", "w_down")
MID = ("w_in", "conv_dw", "conv_pw", "w_out")
SMALL_1024 = ("ffn1_norm", "mix_norm", "ffn2_norm", "final_norm")
SMALL_512 = ("conv_dw_b", "conv_ln_g", "conv_ln_b", "pool_scale")
WEIGHTS = ("ffn1_norm", "ffn1_w_gate", "ffn1_w_up", "ffn1_w_down", "mix_norm", "w_in", "conv_dw", "conv_dw_b",
           "conv_ln_g", "conv_ln_b", "conv_pw", "pool_w", "pool_scale", "w_out", "ffn2_norm", "ffn2_w_gate",
           "ffn2_w_up", "ffn2_w_down", "final_norm")
PACK_ROWS = 72
PACK_LOSS_ROW = 70


def _pad_rows(a, rows):
    return jnp.pad(a, ((0, rows - a.shape[0]), (0, 0)))


def _pack_small(t, spare=None):
    rows = [t[k].reshape(1, D_MODEL) for k in SMALL_1024]
    rows.append(jnp.concatenate([t["conv_dw_b"].reshape(1, -1), t["conv_ln_g"].reshape(1, -1)], axis=1))
    rows.append(jnp.concatenate([t["conv_ln_b"].reshape(1, -1), t["pool_scale"].reshape(1, -1)], axis=1))
    rows.append(t["pool_w"].reshape(64, D_MODEL))
    if spare is not None:
        rows.append(jnp.pad(spare, ((0, 0), (0, D_MODEL - spare.shape[1]))))
    return _pad_rows(jnp.concatenate(rows, axis=0), PACK_ROWS)


def _unpack_small(p):
    out = {k: p[i] for i, k in enumerate(SMALL_1024)}
    out["conv_dw_b"], out["conv_ln_g"] = p[4, :D_CONV], p[4, D_CONV:]
    out["conv_ln_b"], out["pool_scale"] = p[5, :D_CONV], p[5, D_CONV:]
    out["pool_w"] = p[6:70].reshape(4, POOL_GROUP, POOL_GROUP)
    return out


def _as_stored(name, a):
    if name.endswith(("w_gate", "w_up")):
        return a.T
    if name == "conv_dw":
        return _pad_rows(a, CONV_WIDTH + 1)
    return a


def _as_given(name, a):
    if name.endswith(("w_gate", "w_up")):
        return a.T
    if name == "conv_dw":
        return a[:CONV_WIDTH]
    return a


def kernel(x, ffn1_norm, ffn1_w_gate, ffn1_w_up, ffn1_w_down, mix_norm, w_in, conv_dw, conv_dw_b, conv_ln_g, conv_ln_b, conv_pw, pool_w, pool_scale, w_out, ffn2_norm, ffn2_w_gate, ffn2_w_up, ffn2_w_down, final_norm, loss_target, m_ffn1_norm, m_ffn1_w_gate, m_ffn1_w_up, m_ffn1_w_down, m_mix_norm, m_w_in, m_conv_dw, m_conv_dw_b, m_conv_ln_g, m_conv_ln_b, m_conv_pw, m_pool_w, m_pool_scale, m_w_out, m_ffn2_norm, m_ffn2_w_gate, m_ffn2_w_up, m_ffn2_w_down, m_final_norm, v_ffn1_norm, v_ffn1_w_gate, v_ffn1_w_up, v_ffn1_w_down, v_mix_norm, v_w_in, v_conv_dw, v_conv_dw_b, v_conv_ln_g, v_conv_ln_b, v_conv_pw, v_pool_w, v_pool_scale, v_w_out, v_ffn2_norm, v_ffn2_w_gate, v_ffn2_w_up, v_ffn2_w_down, v_final_norm):
    given = dict(locals())
    wts = {k: given[k] for k in WEIGHTS}
    mom_m = {k: given["m_" + k] for k in WEIGHTS}
    mom_v = {k: given["v_" + k] for k in WEIGHTS}
    xt, target = x[0], loss_target[0]

    shard = {k: _as_stored(k, wts[k]) if k == "conv_dw" else _as_stored(k, wts[k]).astype(BF16)
             for k in WEIGHTS if k.endswith(FFN_W) or k in MID}
    w = {k: wts[k].reshape(1, -1) for k in SMALL_1024 + SMALL_512}
    w["pool_w"] = wts["pool_w"].astype(BF16)

    (h1, s1, p1, a1, w["ffn1_w_gate"], w["ffn1_w_up"]), ((w["ffn1_w_down"],),) = _ffn_up_gather(
        xt, w["ffn1_norm"], shard["ffn1_w_gate"], shard["ffn1_w_up"], "ffn1_up_gather",
        cargos=[Cargo("gather_slots", [shard["ffn1_w_down"]])])
    x1, (mid, (w["ffn2_w_down"],)) = _ffn_down(
        xt, a1, w["ffn1_w_down"], "ffn1_down",
        cargos=[Cargo("gather_chips", [shard[k] for k in MID]), Cargo("gather_slots", [shard["ffn2_w_down"]])])
    w["w_in"] = mid[0]
    w["conv_dw"] = mid[1].transpose(1, 0, 2).reshape(CONV_WIDTH + 1, D_CONV)
    w["conv_pw"] = mid[2].reshape(D_CONV, D_CONV)
    w["w_out"] = mid[3].reshape(D_CONV + D_POOL, D_MODEL)
    (x2, h2, proj, u1, u3, mixed, cat), ((w["ffn2_w_gate"], w["ffn2_w_up"]),) = _mix_fwd(
        x1, w["mix_norm"], w["w_in"], w["conv_dw"], w["conv_dw_b"], w["conv_ln_g"], w["conv_ln_b"], w["conv_pw"],
        w["pool_w"], w["pool_scale"], w["w_out"], "mix_fwd",
        cargos=[Cargo("gather_slots", [shard["ffn2_w_gate"], shard["ffn2_w_up"]])])
    x3, h3, s2, p2, a2 = _ffn_fwd(x2, w["ffn2_norm"], w["ffn2_w_gate"], w["ffn2_w_up"], w["ffn2_w_down"], "ffn2_fwd")
    dx3, loss_share, d_final = _final_norm_loss(x3, target, w["final_norm"], "final_norm_loss")

    g = {"final_norm": d_final}
    sums = {}

    def landed(names, parts):
        sums.update(zip(names, parts))

    dx2, g["ffn2_norm"], df2, dg2, du2 = _ffn_bwd(dx3, x2, w["ffn2_norm"], s2, p2, w["ffn2_w_gate"],
                                                   w["ffn2_w_up"], w["ffn2_w_down"], "ffn2_bwd")
    def ffn_wgrad(names, hids, tok, kernel_name, cargos=()):
        parts, cargo_outs = _wgrad_hid_tok_scatter(hids, tok, kernel_name, cargos=cargos)
        landed(names, parts)
        return cargo_outs

    ffn_wgrad(["ffn2_w_gate", "ffn2_w_up"], [dg2, du2], h3, "ffn2_dw_gate_up")
    ffn_wgrad(["ffn2_w_down"], [a2], df2, "ffn2_dw_down")
    (dx1, dproj, g_pw, g["pool_w"], g_dw, g["conv_dw_b"], g["conv_ln_g"], g["conv_ln_b"], g["pool_scale"],
     g["mix_norm"]), (swapped2,) = _mix_bwd(
        dx2, u1, u3, mixed, proj, x1, w["mix_norm"], w["conv_dw"], w["conv_ln_g"], w["conv_ln_b"], w["conv_pw"],
        w["pool_w"], w["pool_scale"], w["w_out"], w["w_in"], "mix_bwd",
        cargos=[Cargo("swap", [sums["ffn2_" + k] for k in FFN_W])])
    g_out, _ = _wgrad_2d(cat, dx2, 1, BF16, "dw_out")
    slabs = [g_pw.reshape(N_CHIPS, D_CONV // N_CHIPS, D_CONV),
             g_out.reshape(N_CHIPS, (D_CONV + D_POOL) // N_CHIPS, D_MODEL)]
    g_in, (parts,) = _wgrad_2d(h2, dproj, N_CHIPS, BF16, "dw_in", cargos=[Cargo("scatter_chips", slabs)])
    landed(["conv_pw", "w_out"], parts)
    dx, g["ffn1_norm"], df1, dg1, du1_ = _ffn_bwd(dx1, xt, w["ffn1_norm"], s1, p1, w["ffn1_w_gate"],
                                                   w["ffn1_w_up"], w["ffn1_w_down"], "ffn1_bwd")
    slabs = [g_in, g_dw.reshape(CONV_WIDTH + 1, N_CHIPS, D_CONV // N_CHIPS).transpose(1, 0, 2)]
    (parts,) = ffn_wgrad(["ffn1_w_gate", "ffn1_w_up"], [dg1, du1_], h1, "ffn1_dw_gate_up",
                         cargos=[Cargo("scatter_chips", slabs)])
    landed(["w_in", "conv_dw"], parts)
    swapped_mid, swapped_gate_up, small_parts = ffn_wgrad(
        ["ffn1_w_down"], [a1], df1, "ffn1_dw_down",
        cargos=[Cargo("swap", [sums[k] for k in MID]), Cargo("swap", [sums["ffn1_w_gate"], sums["ffn1_w_up"]]),
                Cargo("gather_devices", [_pack_small(g, spare=loss_share)])])
    swapped_down = _exchange(Cargo("swap", [sums["ffn1_w_down"]]), "swap_last")

    theirs = dict(zip(["ffn2_" + k for k in FFN_W], swapped2))
    theirs.update(zip(MID, swapped_mid))
    theirs.update(ffn1_w_gate=swapped_gate_up[0], ffn1_w_up=swapped_gate_up[1], ffn1_w_down=swapped_down[0])
    grads, deltas, new_m, new_v = {}, {}, {}, {}
    for k in theirs:
        res = _adamw([sums[k], theirs[k]], _as_stored(k, wts[k]), _as_stored(k, mom_m[k]),
                     _as_stored(k, mom_v[k]), "adamw_" + k)
        grads[k], deltas[k], new_m[k], new_v[k] = [_as_given(k, t) for t in res]
    res = _adamw(small_parts, _pack_small(wts), _pack_small(mom_m), _pack_small(mom_v), "adamw_small")
    for dst, packed in zip((grads, deltas, new_m, new_v), res):
        dst.update(_unpack_small(packed))
    loss = res[0][PACK_LOSS_ROW, 0]

    out = [loss, dx[None]]
    for group in (grads, deltas, new_m, new_v):
        out += [group[k] for k in WEIGHTS]
    return tuple(out)
```

```python
import functools

import jax
import jax.numpy as jnp
from jax import lax
from jax.experimental import pallas as pl
from jax.experimental.pallas import tpu as pltpu

F32 = jnp.float32
BF16 = jnp.bfloat16
MESH = pl.DeviceIdType.MESH

N_CHIPS = 4
N_DEV = 8
D_MODEL = 1024
D_CONV = 512
D_POOL = 512
CONV_WIDTH = 31
POOL_WINDOWS = (2, 4, 8, 16)
POOL_GROUP = 128
D_IN = 2 * D_CONV + D_POOL
HALO = 32
RMS_EPS = 1e-6
LN_EPS = 1e-5
FFN_RES_WEIGHT = 0.5
ADAM_LR = 0.001
ADAM_B1 = 0.9
ADAM_B2 = 0.999
ADAM_EPS = 1e-08
ADAM_WD = 0.01
ADAM_STEP = 10
VMEM_LIMIT_BYTES = 52 * 1024 * 1024
VMEM_LIMIT_BYTES_LARGE = 58 * 1024 * 1024
TM_FFN = 512
TM_MIX = 256
TM_MIX_FWD = 512
TT_WGRAD = 2048
STRIP = 16
SLOTS_PER_STEP = 2
SLOTS_PER_STEP_FWD = 4
SUBLANES = 8
RELAY_AT_EIGHTHS = 7

HBM = pl.BlockSpec(memory_space=pl.ANY)


def _dot(a, b):
    return jnp.dot(a, b, preferred_element_type=F32)


def _dot_nt(a, b):
    return lax.dot_general(a, b, (((1,), (1,)), ((), ())), preferred_element_type=F32)


def _dot_tn(a, b):
    return lax.dot_general(a, b, (((0,), (0,)), ((), ())), preferred_element_type=F32)


def _sds(shape, dtype):
    return jax.ShapeDtypeStruct(shape, dtype)


def _rms_stats(xv):
    r = lax.rsqrt(jnp.mean(xv * xv, axis=-1, keepdims=True) + RMS_EPS)
    return r, xv * r


def _swiglu_saved(gate, up):
    sig = jax.nn.sigmoid(gate)
    silu = gate * sig
    return silu, up * (sig * (1.0 + gate * (1.0 - sig))), silu * up


def _rms_bwd(dh, n, r, gain):
    dn = dh * gain
    return r * (dn - n * jnp.mean(dn * n, axis=-1, keepdims=True))


def _place():
    x, y, c = lax.axis_index("x"), lax.axis_index("y"), lax.axis_index("c")
    return x, y, c, [(1 - x, y), (x, 1 - y), (1 - x, 1 - y)]


class Cargo:
    def __init__(self, kind, arrays):
        self.kind, self.arrays = kind, list(arrays)
        n = len(self.arrays)
        self.two_level = kind in ("gather_slots", "gather_chips")
        if self.two_level:
            self.out_shape = [_sds((N_CHIPS,) + a.shape, a.dtype) for a in self.arrays]
        elif kind == "gather_devices":
            self.out_shape = [_sds((N_DEV,) + a.shape, a.dtype) for a in self.arrays]
        else:
            self.out_shape = [_sds(a.shape, a.dtype) for a in self.arrays]
        n_remote = n * {"swap": 1, "gather_devices": N_DEV - 1}.get(kind, N_CHIPS - 1)
        n_own = 0 if kind == "swap" else n
        n_relay = n_remote if self.two_level else 0
        dma = pltpu.SemaphoreType.DMA
        self.scratch = [dma((n_remote,)), dma((n_remote,)), dma((max(n_own, 1),)),
                        dma((max(n_relay, 1),)), dma((max(n_relay, 1),))]

    def _plan(self, ins, outs):
        x, y, c, chips = _place()
        q = 2 * x + y
        sibling = (x, y, 1 - c)
        own, remote, relays = [], [], []
        for a, o in zip(ins, outs):
            if self.two_level:
                half = a.shape[0] // 2
                mine = pl.ds(pl.multiple_of(c * half, SUBLANES), half)
                theirs = pl.ds(pl.multiple_of((1 - c) * half, SUBLANES), half)
                own.append((a, o.at[0 if self.kind == "gather_slots" else q]))
                for j, (px, py) in enumerate(chips):
                    there, here = (j + 1, j + 1) if self.kind == "gather_slots" else (q, 2 * px + py)
                    remote.append((a.at[mine], o.at[there, mine], o.at[here, mine], (px, py, c)))
                    relays.append((o.at[here, mine], o.at[here, mine], o.at[here, theirs], sibling))
            elif self.kind == "scatter_chips":
                own.append((a.at[q], o.at[q]))
                remote += [(a.at[2 * px + py], o.at[q], o.at[2 * px + py], (px, py, c)) for px, py in chips]
            elif self.kind == "swap":
                remote.append((a, o, o, sibling))
            else:
                own.append((a, o.at[4 * x + 2 * y + c]))
                for k in range(1, N_DEV):
                    px, py, pc = x ^ (k >> 2 & 1), y ^ (k >> 1 & 1), c ^ (k & 1)
                    remote.append((a, o.at[4 * x + 2 * y + c], o.at[4 * px + 2 * py + pc], (px, py, pc)))
        return own, remote, relays

    @staticmethod
    def _copies(entries, send_sems, recv_sems):
        out = []
        for k, (src, dst, landed, peer) in enumerate(entries):
            def make(dst_ref, k=k, src=src, peer=peer):
                return pltpu.make_async_remote_copy(src_ref=src, dst_ref=dst_ref, send_sem=send_sems.at[k],
                                                    recv_sem=recv_sems.at[k], device_id=peer, device_id_type=MESH)
            out.append((make(dst), make(landed)))
        return out

    def start(self, ins, outs, sems):
        own, remote, _ = self._plan(ins, outs)
        for k, (src, dst) in enumerate(own):
            pltpu.make_async_copy(src, dst, sems[2].at[k]).start()
        for mine, _ in self._copies(remote, sems[0], sems[1]):
            mine.start()

    def relay(self, ins, outs, sems):
        _, remote, relays = self._plan(ins, outs)
        passed = self._copies(relays, sems[3], sems[4])
        for (_, arriving), (mine, _) in zip(self._copies(remote, sems[0], sems[1]), passed):
            arriving.wait_recv()
            mine.start()

    def wait(self, ins, outs, sems):
        own, remote, relays = self._plan(ins, outs)
        for mine, arriving in self._copies(remote, sems[0], sems[1]):
            mine.wait_send()
            if not self.two_level:
                arriving.wait_recv()
        for mine, arriving in self._copies(relays, sems[3], sems[4]):
            mine.wait_send()
            arriving.wait_recv()
        for k, (src, dst) in enumerate(own):
            pltpu.make_async_copy(src, dst, sems[2].at[k]).wait()


N_CARGO_SEMS = 5


def _call(body, *, name, grid, in_specs, out_specs, out_shape, args, scratch_shapes=(), cargos=(),
          vmem_limit_bytes=VMEM_LIMIT_BYTES):
    n_in, n_out, n_scr = len(in_specs), len(out_specs), len(scratch_shapes)
    c_in = [len(cg.arrays) for cg in cargos]
    n_cin = sum(c_in)

    def wrapped(*refs):
        ins = refs[:n_in]
        cins = refs[n_in:n_in + n_cin]
        outs = refs[n_in + n_cin:n_in + n_cin + n_out]
        couts = refs[n_in + n_cin + n_out:n_in + 2 * n_cin + n_out]
        scr = refs[n_in + 2 * n_cin + n_out:n_in + 2 * n_cin + n_out + n_scr]
        sems = refs[n_in + 2 * n_cin + n_out + n_scr:]
        step, n_steps = 0, 1
        for ax, size in enumerate(grid):
            step = step * size + pl.program_id(ax)
            n_steps *= size

        def each(method, only_two_level=False):
            at = 0
            for k, cg in enumerate(cargos):
                if cg.two_level or not only_two_level:
                    getattr(cg, method)(cins[at:at + c_in[k]], couts[at:at + c_in[k]],
                                        sems[N_CARGO_SEMS * k:N_CARGO_SEMS * (k + 1)])
                at += c_in[k]

        body(*ins, *outs, *scr)
        if cargos:
            pl.when(step == 0)(lambda: each("start"))
        if any(cg.two_level for cg in cargos):
            pl.when(step == (RELAY_AT_EIGHTHS * n_steps) // 8)(lambda: each("relay", only_two_level=True))
        if cargos:
            pl.when(step == n_steps - 1)(lambda: each("wait"))

    res = pl.pallas_call(
        wrapped, name=name, grid=grid,
        in_specs=list(in_specs) + [HBM] * n_cin,
        out_specs=list(out_specs) + [HBM] * n_cin,
        out_shape=list(out_shape) + [s for cg in cargos for s in cg.out_shape],
        scratch_shapes=list(scratch_shapes) + [s for cg in cargos for s in cg.scratch],
        compiler_params=pltpu.CompilerParams(dimension_semantics=("arbitrary",) * len(grid),
                                             vmem_limit_bytes=vmem_limit_bytes),
    )(*args, *[a for cg in cargos for a in cg.arrays])
    outs, rest = list(res[:n_out]), list(res[n_out:])
    cargo_outs = []
    for k in c_in:
        cargo_outs.append(rest[:k])
        rest = rest[k:]
    return outs, cargo_outs


def _exchange(cargo, name):
    _, (outs,) = _call(lambda: None, name=name, grid=(1,), in_specs=[], out_specs=[], out_shape=[], args=[],
                       cargos=[cargo])
    return outs


def _ffn_up_gather(x, gain, wg_t, wu_t, name, cargos=()):
    t_len, d = x.shape
    fq = wg_t.shape[0]
    tm = min(TM_FFN, t_len)
    n_tiles = t_len // tm
    relay_tile = n_tiles // 2
    fetch_tile = min(relay_tile + 1, n_tiles - 1)

    def body(x_ref, g_ref, wg_in, wu_in, h_ref, s_ref, p_ref, a_ref, wg_all, wu_all,
             wg_v, wu_v, h_all, send_sems, recv_sems, pass_send_sems, pass_recv_sems, own_sems, load_sems):
        s = pl.program_id(0)
        i = pl.program_id(1)
        x_, y_, c_, chips = _place()
        shards = ((wg_in, wg_all, wg_v), (wu_in, wu_all, wu_v))
        mine = pl.ds(pl.multiple_of(c_ * (fq // 2), SUBLANES), fq // 2)
        theirs = pl.ds(pl.multiple_of((1 - c_) * (fq // 2), SUBLANES), fq // 2)

        def to_peer(k, j):
            w_in, w_all, _ = shards[k]
            return pltpu.make_async_remote_copy(
                src_ref=w_in.at[mine], dst_ref=w_all.at[j + 1, mine], send_sem=send_sems.at[3 * k + j],
                recv_sem=recv_sems.at[3 * k + j], device_id=(*chips[j], c_), device_id_type=MESH)

        def to_sibling(k, j, landing=False):
            w_all = shards[k][1]
            return pltpu.make_async_remote_copy(
                src_ref=w_all.at[j + 1, mine], dst_ref=w_all.at[j + 1, theirs if landing else mine],
                send_sem=pass_send_sems.at[3 * k + j], recv_sem=pass_recv_sems.at[3 * k + j],
                device_id=(x_, y_, 1 - c_), device_id_type=MESH)

        def keep(k):
            return pltpu.make_async_copy(shards[k][0], shards[k][1].at[0], own_sems.at[k])

        @pl.when((s == 0) & (i == 0))
        def _():
            for j in range(N_CHIPS - 1):
                for k in range(2):
                    to_peer(k, j).start()
            for k in range(2):
                keep(k).start()

        def load(k, slot):
            src = shards[k][0] if slot == 0 else shards[k][1].at[slot]
            return pltpu.make_async_copy(src, shards[k][2].at[slot % 2], load_sems.at[k])

        @pl.when((s == 0) & (i == 0))
        def _():
            for k in range(2):
                load(k, 0).start()
            for k in range(2):
                load(k, 0).wait()

        def pass_on(slot):
            for k in range(2):
                to_peer(k, slot - 1).wait_recv()
                to_sibling(k, slot - 1).start()

        def fetch(slot):
            for k in range(2):
                to_sibling(k, slot - 1, landing=True).wait_recv()
                load(k, slot).start()

        for slot in range(1, N_CHIPS):
            pl.when((s == slot - 1) & (i == relay_tile))(functools.partial(pass_on, slot))
            pl.when((s == slot - 1) & (i == fetch_tile))(functools.partial(fetch, slot))

            @pl.when((s == slot) & (i == 0))
            def _():
                for k in range(2):
                    load(k, slot).wait()

        @pl.when(s == 0)
        def _():
            _, n = _rms_stats(x_ref[...])
            h_new = (n * g_ref[...]).astype(BF16)
            h_ref[...] = h_new
            h_all[i] = h_new

        h = h_all[i]
        silu, dgate, act = _swiglu_saved(_dot_nt(h, wg_v[s % 2]), _dot_nt(h, wu_v[s % 2]))
        s_ref[...] = silu.astype(BF16)
        p_ref[...] = dgate.astype(BF16)
        a_ref[...] = act.astype(BF16)

        @pl.when((s == N_CHIPS - 1) & (i == n_tiles - 1))
        def _():
            for k in range(2):
                for j in range(N_CHIPS - 1):
                    to_peer(k, j).wait_send()
                    to_sibling(k, j).wait_send()
                keep(k).wait()

    tok = pl.BlockSpec((tm, d), lambda s, i: (jnp.where(s == 0, i, n_tiles - 1), 0))
    hid = pl.BlockSpec((None, tm, fq), lambda s, i: (s, i, 0))
    outs, cargo_outs = _call(
        body, name=name, grid=(N_CHIPS, n_tiles),
        in_specs=[tok, pl.BlockSpec((1, d), lambda s, i: (0, 0)), HBM, HBM],
        out_specs=[tok, hid, hid, hid, HBM, HBM],
        out_shape=[_sds((t_len, d), BF16)] + [_sds((N_CHIPS, t_len, fq), BF16)] * 3
        + [_sds((N_CHIPS, fq, d), BF16)] * 2,
        scratch_shapes=[pltpu.VMEM((2, fq, d), BF16), pltpu.VMEM((2, fq, d), BF16),
                        pltpu.VMEM((n_tiles, tm, d), BF16)]
        + [pltpu.SemaphoreType.DMA((6,))] * 4 + [pltpu.SemaphoreType.DMA((2,))] * 2,
        args=[x, gain, wg_t, wu_t], cargos=cargos)
    return outs, cargo_outs


def _load_once(hbm_refs, vmem_refs, sems, first):
    @pl.when(first)
    def _():
        copies = [pltpu.make_async_copy(src, dst, sems.at[k]) for k, (src, dst) in enumerate(zip(hbm_refs, vmem_refs))]
        for cp in copies:
            cp.start()
        for cp in copies:
            cp.wait()


def _ffn_down(x, act, wd, name, cargos=()):
    t_len, d = x.shape
    nq, fq, _ = wd.shape
    tm = min(TM_FFN, t_len)

    def body(x_ref, a_ref, wd_ref, xo_ref):
        y = _dot(a_ref[0], wd_ref[0])
        for j in range(1, nq):
            y = y + _dot(a_ref[j], wd_ref[j])
        xo_ref[...] = x_ref[...] + FFN_RES_WEIGHT * y

    tok = pl.BlockSpec((tm, d), lambda i: (i, 0))
    (xo,), cargo_outs = _call(
        body, name=name, grid=(t_len // tm,),
        in_specs=[tok, pl.BlockSpec((nq, tm, fq), lambda i: (0, i, 0)), pl.BlockSpec((nq, fq, d), lambda i: (0, 0, 0))],
        out_specs=[tok], out_shape=[_sds((t_len, d), F32)], args=[x, act, wd], cargos=cargos)
    return xo, cargo_outs


def _ffn_fwd(x, gain, wg_t, wu_t, wd, name):
    t_len, d = x.shape
    nq, fq, _ = wd.shape
    tm = min(TM_FFN, t_len)

    def body(x_ref, g_ref, wg_hbm, wu_hbm, wd_hbm, xo_ref, h_ref, s_ref, p_ref, a_ref,
             h_s, acc, wg_v, wu_v, wd_v, load_sems):
        i = pl.program_id(0)
        j = pl.program_id(1)
        _load_once((wg_hbm, wu_hbm, wd_hbm), (wg_v, wu_v, wd_v), load_sems, (i == 0) & (j == 0))

        @pl.when(j == 0)
        def _():
            _, n = _rms_stats(x_ref[...])
            h = (n * g_ref[...]).astype(BF16)
            h_s[...] = h
            h_ref[...] = h
            acc[...] = jnp.zeros_like(acc)

        h = h_s[...]
        y = None
        for jj in range(SLOTS_PER_STEP_FWD):
            slot = j * SLOTS_PER_STEP_FWD + jj
            silu, dgate, act = _swiglu_saved(_dot_nt(h, wg_v[slot]), _dot_nt(h, wu_v[slot]))
            s_ref[jj] = silu.astype(BF16)
            p_ref[jj] = dgate.astype(BF16)
            a_ref[jj] = act.astype(BF16)
            part = _dot(a_ref[jj], wd_v[slot])
            y = part if y is None else y + part
        acc[...] += y

        @pl.when(j == nq // SLOTS_PER_STEP_FWD - 1)
        def _():
            xo_ref[...] = x_ref[...] + FFN_RES_WEIGHT * acc[...]

    tok = pl.BlockSpec((tm, d), lambda i, j: (i, 0))
    hid = pl.BlockSpec((SLOTS_PER_STEP_FWD, tm, fq), lambda i, j: (j, i, 0))
    outs, _ = _call(
        body, name=name, grid=(t_len // tm, nq // SLOTS_PER_STEP_FWD),
        in_specs=[tok, pl.BlockSpec((1, d), lambda i, j: (0, 0)), HBM, HBM, HBM],
        out_specs=[tok, tok, hid, hid, hid],
        out_shape=[_sds((t_len, d), F32), _sds((t_len, d), BF16)] + [_sds((nq, t_len, fq), BF16)] * 3,
        scratch_shapes=[pltpu.VMEM((tm, d), BF16), pltpu.VMEM((tm, d), F32)]
        + [pltpu.VMEM((nq, fq, d), BF16)] * 3 + [pltpu.SemaphoreType.DMA((3,))],
        args=[x, gain, wg_t, wu_t, wd], vmem_limit_bytes=VMEM_LIMIT_BYTES_LARGE)
    return outs


def _ffn_bwd(dy, x_in, gain, silu, dgate_du, wg_t, wu_t, wd, name):
    t_len, d = dy.shape
    nq, fq, _ = wd.shape
    tm = min(TM_FFN, t_len)

    def body(dy_ref, x_ref, g_ref, s_ref, p_ref, wg_hbm, wu_hbm, wd_hbm,
             dx_ref, dgain_ref, df_ref, dg_ref, du_ref, df_s, dh_acc, dact_s, wg_v, wu_v, wd_v, load_sems):
        i = pl.program_id(0)
        j = pl.program_id(1)
        _load_once((wg_hbm, wu_hbm, wd_hbm), (wg_v, wu_v, wd_v), load_sems, (i == 0) & (j == 0))

        @pl.when((i == 0) & (j == 0))
        def _():
            dgain_ref[...] = jnp.zeros_like(dgain_ref)

        @pl.when(j == 0)
        def _():
            df = (FFN_RES_WEIGHT * dy_ref[...]).astype(BF16)
            df_s[...] = df
            df_ref[...] = df
            dh_acc[...] = jnp.zeros_like(dh_acc)

        slots = [j * SLOTS_PER_STEP + jj for jj in range(SLOTS_PER_STEP)]
        for jj, slot in enumerate(slots):
            dact_s[jj] = _dot_nt(df_s[...], wd_v[slot])

        for jj in range(SLOTS_PER_STEP):
            for r0 in range(0, tm, STRIP):
                rows = slice(r0, r0 + STRIP)
                dact = dact_s[jj, rows, :]
                dg_ref[jj, rows, :] = (dact * p_ref[jj, rows, :].astype(F32)).astype(BF16)
                du_ref[jj, rows, :] = (dact * s_ref[jj, rows, :].astype(F32)).astype(BF16)

        dh = None
        for jj, slot in enumerate(slots):
            part = _dot(dg_ref[jj], wg_v[slot]) + _dot(du_ref[jj], wu_v[slot])
            dh = part if dh is None else dh + part
        dh_acc[...] += dh

        @pl.when(j == nq // SLOTS_PER_STEP - 1)
        def _():
            r, n = _rms_stats(x_ref[...])
            dh = dh_acc[...]
            dgain_ref[...] += jnp.sum(dh * n, axis=0, keepdims=True)
            dx_ref[...] = dy_ref[...] + _rms_bwd(dh, n, r, g_ref[...])

    tok = pl.BlockSpec((tm, d), lambda i, j: (i, 0))
    vec = pl.BlockSpec((1, d), lambda i, j: (0, 0))
    hid = pl.BlockSpec((SLOTS_PER_STEP, tm, fq), lambda i, j: (j, i, 0))
    outs, _ = _call(
        body, name=name, grid=(t_len // tm, nq // SLOTS_PER_STEP),
        in_specs=[tok, tok, vec, hid, hid, HBM, HBM, HBM],
        out_specs=[tok, vec, tok, hid, hid],
        out_shape=[_sds((t_len, d), F32), _sds((1, d), F32), _sds((t_len, d), BF16),
                   _sds((nq, t_len, fq), BF16), _sds((nq, t_len, fq), BF16)],
        scratch_shapes=[pltpu.VMEM((tm, d), BF16), pltpu.VMEM((tm, d), F32),
                        pltpu.VMEM((SLOTS_PER_STEP, tm, fq), F32)]
        + [pltpu.VMEM((nq, fq, d), BF16)] * 3 + [pltpu.SemaphoreType.DMA((3,))],
        args=[dy, x_in, gain, silu, dgate_du, wg_t, wu_t, wd], vmem_limit_bytes=VMEM_LIMIT_BYTES_LARGE)
    return outs


def _wgrad(lhs, rhs, l_spec, r_spec, out_shape, out_spec, acc_shape, grid, name, cargos=()):
    n_t = grid[-1]
    t_axis = len(grid) - 1

    def body(l_ref, r_ref, o_ref, acc):
        t = pl.program_id(t_axis)

        @pl.when(t == 0)
        def _():
            acc[...] = jnp.zeros_like(acc)

        acc[...] += _dot_tn(l_ref[...].astype(BF16), r_ref[...].astype(BF16))

        @pl.when(t == n_t - 1)
        def _():
            o_ref[...] = acc[...].astype(o_ref.dtype)

    (out,), cargo_outs = _call(
        body, name=name, grid=grid, in_specs=[l_spec, r_spec], out_specs=[out_spec], out_shape=[out_shape],
        scratch_shapes=[pltpu.VMEM(acc_shape, F32)], args=[lhs, rhs], cargos=cargos)
    return out, cargo_outs


def _wgrad_hid_tok_scatter(hids, tok, name, cargos=()):
    t_len, d = tok.shape
    n_w = len(hids)
    nq, _, fq = hids[0].shape
    half = fq // 2
    tt = min(TT_WGRAD, t_len)
    n_t = t_len // tt
    per_w = 4
    n_sem = 6

    def body(*refs):
        l_refs, r_ref, parts_refs = refs[:n_w], refs[n_w], refs[n_w + 1:2 * n_w + 1]
        scr = refs[2 * n_w + 1:]
        bufs = [scr[per_w * w:per_w * (w + 1)] for w in range(n_w)]
        zeros = scr[per_w * n_w]
        sems = [scr[per_w * n_w + 1 + n_sem * w:per_w * n_w + 1 + n_sem * (w + 1)] for w in range(n_w)]
        g = pl.program_id(0)
        t = pl.program_id(1)
        x_, y_, c_, chips = _place()
        mine = pl.ds(pl.multiple_of(c_ * half, STRIP), half)
        theirs = pl.ds(pl.multiple_of((1 - c_) * half, STRIP), half)

        def to_sibling(w, slot):
            return pltpu.make_async_remote_copy(
                src_ref=bufs[w][1].at[theirs], dst_ref=bufs[w][2].at[slot], send_sem=sems[w][0].at[slot],
                recv_sem=sems[w][1].at[slot], device_id=(x_, y_, 1 - c_), device_id_type=MESH)

        def to_peer(w, j):
            return pltpu.make_async_remote_copy(
                src_ref=bufs[w][3].at[j + 1], dst_ref=parts_refs[w].at[j + 1, mine], send_sem=sems[w][2].at[j],
                recv_sem=sems[w][3].at[j], device_id=(*chips[j], c_), device_id_type=MESH)

        def keep(w):
            return pltpu.make_async_copy(bufs[w][3].at[0], parts_refs[w].at[0, mine], sems[w][4])

        def blank(w, slot):
            return pltpu.make_async_copy(zeros, parts_refs[w].at[slot, theirs], sems[w][5].at[slot])

        @pl.when((g == 0) & (t == 0))
        def _():
            zeros[...] = jnp.zeros_like(zeros)
            for w in range(n_w):
                for slot in range(nq):
                    blank(w, slot).start()

        @pl.when(t == 0)
        def _():
            for w in range(n_w):
                bufs[w][0][...] = jnp.zeros_like(bufs[w][0])

        rhs = r_ref[...]
        for w in range(n_w):
            bufs[w][0][...] += _dot_tn(l_refs[w][...], rhs)

        for step in range(nq):
            slot = (step + 1) % nq

            @pl.when((g == step) & (t == n_t - 1))
            def _():
                for w in range(n_w):
                    acc, stage, _, _ = bufs[w]
                    if step > 0:
                        to_sibling(w, step).wait_send()
                    stage[...] = acc[...].astype(BF16)
                    to_sibling(w, slot).start()
                for w in range(n_w):
                    _, stage, pair, summed = bufs[w]
                    to_sibling(w, slot).wait_recv()
                    summed[slot] = (stage[mine, :].astype(F32) + pair[slot].astype(F32)).astype(BF16)
                    if slot > 0:
                        to_peer(w, slot - 1).start()
                    else:
                        keep(w).start()

        @pl.when((g == nq - 1) & (t == n_t - 1))
        def _():
            for w in range(n_w):
                for j in range(N_CHIPS - 1):
                    to_peer(w, j).wait()
                keep(w).wait()
                to_sibling(w, 0).wait_send()
                for slot in range(nq):
                    blank(w, slot).wait()

    dma = pltpu.SemaphoreType.DMA
    scratch = []
    for _ in range(n_w):
        scratch += [pltpu.VMEM((fq, d), F32), pltpu.VMEM((fq, d), BF16), pltpu.VMEM((nq, half, d), BF16),
                    pltpu.VMEM((nq, half, d), BF16)]
    scratch.append(pltpu.VMEM((half, d), BF16))
    for _ in range(n_w):
        scratch += [dma((nq,)), dma((nq,)), dma((N_CHIPS - 1,)), dma((N_CHIPS - 1,)), dma(()), dma((nq,))]
    parts, cargo_outs = _call(
        body, name=name, grid=(nq, n_t),
        in_specs=[pl.BlockSpec((None, tt, fq), lambda g, t: ((g + 1) % nq, t, 0))] * n_w
        + [pl.BlockSpec((tt, d), lambda g, t: (t, 0))],
        out_specs=[HBM] * n_w, out_shape=[_sds((nq, fq, d), BF16)] * n_w,
        scratch_shapes=scratch, args=[*hids, tok], cargos=cargos, vmem_limit_bytes=VMEM_LIMIT_BYTES_LARGE)
    return parts, cargo_outs


def _wgrad_2d(lhs, rhs, n_col_blocks, out_dtype, name, cargos=()):
    t_len, k = lhs.shape
    n = rhs.shape[1]
    nb = n // n_col_blocks
    tt = min(TT_WGRAD, t_len)
    return _wgrad(lhs, rhs,
                  pl.BlockSpec((tt, k), lambda q, t: (t, 0)),
                  pl.BlockSpec((tt, nb), lambda q, t: (t, q)),
                  _sds((n_col_blocks, k, nb), out_dtype),
                  pl.BlockSpec((None, k, nb), lambda q, t: (q, 0, 0)),
                  (k, nb), (n_col_blocks, t_len // tt), name, cargos)


def _layernorm_stats(u1):
    mu = jnp.mean(u1, axis=-1, keepdims=True)
    xc = u1 - mu
    rstd = lax.rsqrt(jnp.mean(xc * xc, axis=-1, keepdims=True) + LN_EPS)
    return rstd, xc * rstd


def _positions(i, tm, rows, offset=0):
    return (lax.broadcasted_iota(jnp.int32, (rows, 1), 0) + (i * tm + offset)).astype(F32)


SHIFT_ROWS = HALO - SUBLANES


def _fill_shifted(ext_s, sh_s, tm):
    for b in range(1, SUBLANES):
        sh_s[b - 1] = ext_s[pl.ds(b, tm + SHIFT_ROWS), :]


def _window(ext_s, sh_s, shift, tm):
    a, b = divmod(shift, SUBLANES)
    if b == 0:
        return ext_s[pl.ds(shift, tm), :]
    return sh_s[b - 1, pl.ds(a * SUBLANES, tm), :]


def _window_sums(ext_s, lv_a, lv_b, tm, ahead):
    g = POOL_GROUP
    sign = 1 if ahead else -1
    for n, (dst, src, c0) in enumerate(((lv_a, ext_s, 0), (lv_b, lv_a, g), (lv_a, lv_b, 2 * g)), start=1):
        lo = 0 if ahead else n * SUBLANES
        rows = tm + HALO - n * SUBLANES
        shift = sign * 2 ** (n - 1)
        dst[pl.ds(lo, rows), c0:] = src[pl.ds(lo, rows), c0:] + src[pl.ds(lo + shift, rows), c0:]
    base = 0 if ahead else HALO
    rows = pl.ds(base, tm)
    far = pl.ds(base + sign * SUBLANES, tm)
    return [lv_a[rows, 0:g], lv_b[rows, g:2 * g], lv_a[rows, 2 * g:3 * g],
            lv_a[rows, 3 * g:] + lv_a[far, 3 * g:]]


def _tile(tm, cols):
    return pl.BlockSpec((tm, cols), lambda i: (i, 0))


def _whole(shape):
    return pl.BlockSpec(shape, lambda i: (0,) * len(shape))


def _mix_fwd(x1, gain, w_in, conv_dw, conv_b, ln_g, ln_b, conv_pw, pool_w, pool_scale, w_out, name, cargos=()):
    t_len, d = x1.shape
    nq, _, nb = w_in.shape
    tm = min(TM_MIX_FWD, t_len)

    def body(x_ref, g_ref, wi_ref, dw_ref, cb_ref, lg_ref, lb_ref, pw_ref, plw_ref, ps_ref, wo_ref,
             x2_ref, h_ref, p_ref, u1_ref, u3_ref, mx_ref, cat_ref, ext_s, pext_s, sh_s, tail_s, lva_s, lvb_s):
        i = pl.program_id(0)

        @pl.when(i == 0)
        def _():
            tail_s[...] = jnp.zeros_like(tail_s)

        _, n = _rms_stats(x_ref[...])
        h = (n * g_ref[...]).astype(BF16)
        h_ref[...] = h
        for q in range(nq):
            p_ref[:, q * nb:(q + 1) * nb] = _dot(h, wi_ref[q])

        a = p_ref[:, 0:D_CONV]
        g = p_ref[:, D_CONV:2 * D_CONV]
        p = p_ref[:, 2 * D_CONV:]
        ext_s[0:HALO, :] = tail_s[:, 0:D_CONV] * jax.nn.sigmoid(tail_s[:, D_CONV:2 * D_CONV])
        ext_s[HALO:, :] = a * jax.nn.sigmoid(g)
        pext_s[0:HALO, :] = tail_s[:, 2 * D_CONV:]
        pext_s[HALO:, :] = p
        tail_s[...] = p_ref[tm - HALO:tm, :]

        _fill_shifted(ext_s, sh_s, tm)
        u1 = jnp.broadcast_to(cb_ref[...], (tm, D_CONV))
        for k in range(CONV_WIDTH):
            u1 = u1 + dw_ref[k:k + 1, :] * _window(ext_s, sh_s, HALO - (CONV_WIDTH - 1) + k, tm)
        u1_ref[...] = u1
        _, nhat = _layernorm_stats(u1)
        u2 = nhat * lg_ref[...] + lb_ref[...]
        u3 = (u2 * jax.nn.sigmoid(u2)).astype(BF16)
        u3_ref[...] = u3
        cat_ref[:, 0:D_CONV] = _dot(u3, pw_ref[...]).astype(BF16)

        pos1 = _positions(i, tm, tm) + 1.0
        sums = _window_sums(pext_s, lva_s, lvb_s, tm, ahead=False)
        for gi, w in enumerate(POOL_WINDOWS):
            cols = slice(gi * POOL_GROUP, (gi + 1) * POOL_GROUP)
            mixed = (sums[gi] / jnp.minimum(pos1, float(w)) - p[:, cols]).astype(BF16)
            mx_ref[:, cols] = mixed
            out = _dot(mixed, plw_ref[gi]) * ps_ref[:, cols]
            cat_ref[:, D_CONV + gi * POOL_GROUP:D_CONV + (gi + 1) * POOL_GROUP] = out.astype(BF16)

        x2_ref[...] = x_ref[...] + _dot(cat_ref[...], wo_ref[...])

    return _call(
        body, name=name, grid=(t_len // tm,),
        in_specs=[_tile(tm, d), _whole((1, d)), _whole((nq, d, nb)), _whole((CONV_WIDTH + 1, D_CONV)),
                  _whole((1, D_CONV)), _whole((1, D_CONV)), _whole((1, D_CONV)), _whole((D_CONV, D_CONV)),
                  _whole((4, POOL_GROUP, POOL_GROUP)), _whole((1, D_POOL)), _whole((D_CONV + D_POOL, d))],
        out_specs=[_tile(tm, d), _tile(tm, d), _tile(tm, D_IN), _tile(tm, D_CONV), _tile(tm, D_CONV),
                   _tile(tm, D_POOL), _tile(tm, D_CONV + D_POOL)],
        out_shape=[_sds((t_len, d), F32), _sds((t_len, d), BF16), _sds((t_len, D_IN), F32),
                   _sds((t_len, D_CONV), F32), _sds((t_len, D_CONV), BF16), _sds((t_len, D_POOL), BF16),
                   _sds((t_len, D_CONV + D_POOL), BF16)],
        scratch_shapes=[pltpu.VMEM((tm + HALO, D_CONV), F32), pltpu.VMEM((tm + HALO, D_POOL), F32),
                        pltpu.VMEM((SUBLANES - 1, tm + SHIFT_ROWS, D_CONV), F32), pltpu.VMEM((HALO, D_IN), F32)]
        + [pltpu.VMEM((tm + HALO, D_POOL), F32)] * 2,
        args=[x1, gain, w_in, conv_dw, conv_b, ln_g, ln_b, conv_pw, pool_w, pool_scale, w_out], cargos=cargos,
        vmem_limit_bytes=VMEM_LIMIT_BYTES_LARGE)


def _mix_bwd(dx2, u1, u3, mixed, proj, x1, gain, conv_dw, ln_g, ln_b, conv_pw, pool_w, pool_scale, w_out, w_in,
             name, cargos=()):
    t_len, d = x1.shape
    nq, _, nb = w_in.shape
    tm = min(TM_MIX, t_len)
    hb = tm // HALO
    n_tiles = t_len // tm

    def body(dxn_ref, u1_ref, u3_ref, mx_ref, p_ref, tail_ref, x_ref, dx2_ref, g_ref, dw_ref, lg_ref, lb_ref, pw_ref,
             plw_ref, ps_ref, wo_ref, wi_ref,
             dx1_ref, dp_ref, dpw_ref, dplw_ref, ddw_ref, dcb_ref, dlg_ref, dlb_ref, dps_ref, dgain_ref,
             du_s, dm_s, uext_s, dext_s, mext_s, ush_s, dsh_s, lva_s, lvb_s):
        k = pl.program_id(0)

        @pl.when(k == 0)
        def _():
            for ref in (dpw_ref, dplw_ref, ddw_ref, dcb_ref, dlg_ref, dlb_ref, dps_ref, dgain_ref, du_s, dm_s):
                ref[...] = jnp.zeros_like(ref)

        counts = jnp.where(k < n_tiles, 1.0, 0.0)
        dcat = _dot_nt(dxn_ref[...].astype(BF16), wo_ref[...])
        dco = dcat[:, 0:D_CONV].astype(BF16)
        dpw_ref[...] += _dot_tn(u3_ref[...], (dcat[:, 0:D_CONV] * counts).astype(BF16))
        du3 = _dot_nt(dco, pw_ref[...])
        rstd, nhat = _layernorm_stats(u1_ref[...])
        u2 = nhat * lg_ref[...] + lb_ref[...]
        sig = jax.nn.sigmoid(u2)
        du2 = du3 * (sig * (1.0 + u2 * (1.0 - sig)))
        dlg_ref[...] += counts * jnp.sum(du2 * nhat, axis=0, keepdims=True)
        dlb_ref[...] += counts * jnp.sum(du2, axis=0, keepdims=True)
        dnhat = du2 * lg_ref[...]
        du_s[k % 2] = rstd * (dnhat - jnp.mean(dnhat, axis=-1, keepdims=True)
                              - nhat * jnp.mean(dnhat * nhat, axis=-1, keepdims=True))
        for gi in range(len(POOL_WINDOWS)):
            cols = slice(gi * POOL_GROUP, (gi + 1) * POOL_GROUP)
            dpo = dcat[:, D_CONV + gi * POOL_GROUP:D_CONV + (gi + 1) * POOL_GROUP]
            pre = _dot(mx_ref[:, cols], plw_ref[gi])
            dps_ref[:, cols] += counts * jnp.sum(dpo * pre, axis=0, keepdims=True)
            dout = dpo * ps_ref[:, cols]
            dplw_ref[gi] += _dot_tn(mx_ref[:, cols], (dout * counts).astype(BF16))
            dm_s[k % 2, :, cols] = _dot_nt(dout.astype(BF16), plw_ref[gi])

        i = jnp.maximum(k - 1, 0)
        cur, nxt = (k + 1) % 2, k % 2
        first = k <= 1
        last = (k == n_tiles) | (k == 0)
        a = p_ref[:, 0:D_CONV]
        g = p_ref[:, D_CONV:2 * D_CONV]
        sg = jax.nn.sigmoid(g)
        ta = tail_ref[:, 0:D_CONV]
        tg = tail_ref[:, D_CONV:2 * D_CONV]
        uext_s[0:HALO, :] = jnp.where(first, 0.0, ta * jax.nn.sigmoid(tg))
        uext_s[HALO:, :] = a * sg
        du1 = du_s[cur]
        dext_s[0:tm, :] = du1
        dext_s[tm:, :] = jnp.where(last, 0.0, du_s[nxt, 0:HALO, :])

        _fill_shifted(uext_s, ush_s, tm)
        _fill_shifted(dext_s, dsh_s, tm)
        du0 = jnp.zeros((tm, D_CONV), F32)
        for tap in range(CONV_WIDTH):
            du0 = du0 + dw_ref[tap:tap + 1, :] * _window(dext_s, dsh_s, CONV_WIDTH - 1 - tap, tm)
            ddw_ref[tap:tap + 1, :] += jnp.sum(
                du1 * _window(uext_s, ush_s, HALO - (CONV_WIDTH - 1) + tap, tm), axis=0, keepdims=True)
        dcb_ref[...] += jnp.sum(du1, axis=0, keepdims=True)
        dp_ref[:, 0:D_CONV] = (du0 * sg).astype(BF16)
        dp_ref[:, D_CONV:2 * D_CONV] = (du0 * a * sg * (1.0 - sg)).astype(BF16)

        pos1 = _positions(i, tm, tm) + 1.0
        pos1_next = _positions(i, tm, HALO, offset=tm) + 1.0
        for gi, w in enumerate(POOL_WINDOWS):
            cols = slice(gi * POOL_GROUP, (gi + 1) * POOL_GROUP)
            dm = dm_s[cur, :, cols]
            mext_s[0:tm, cols] = dm / jnp.minimum(pos1, float(w))
            mext_s[tm:, cols] = jnp.where(last, 0.0, dm_s[nxt, 0:HALO, cols] / jnp.minimum(pos1_next, float(w)))
        sums = _window_sums(mext_s, lva_s, lvb_s, tm, ahead=True)
        for gi in range(len(POOL_WINDOWS)):
            cols = slice(gi * POOL_GROUP, (gi + 1) * POOL_GROUP)
            dp_ref[:, 2 * D_CONV + gi * POOL_GROUP:2 * D_CONV + (gi + 1) * POOL_GROUP] = (
                sums[gi] - dm_s[cur, :, cols]).astype(BF16)

        dh = _dot_nt(dp_ref[:, 0:nb], wi_ref[0])
        for q in range(1, nq):
            dh = dh + _dot_nt(dp_ref[:, q * nb:(q + 1) * nb], wi_ref[q])
        r, n = _rms_stats(x_ref[...])
        dgain_ref[...] += jnp.sum(dh * n, axis=0, keepdims=True)
        dx1_ref[...] = dx2_ref[...] + _rms_bwd(dh, n, r, g_ref[...])

    def ahead(cols):
        return pl.BlockSpec((tm, cols), lambda k: (jnp.minimum(k, n_tiles - 1), 0))

    def behind(cols):
        return pl.BlockSpec((tm, cols), lambda k: (jnp.maximum(k - 1, 0), 0))

    vec = _whole((1, D_CONV))
    return _call(
        body, name=name, grid=(n_tiles + 1,),
        in_specs=[ahead(d), ahead(D_CONV), ahead(D_CONV), ahead(D_POOL), behind(D_IN),
                  pl.BlockSpec((HALO, D_IN), lambda k: (jnp.maximum(jnp.maximum(k - 1, 0) * hb - 1, 0), 0)),
                  behind(d), behind(d), _whole((1, d)), _whole((CONV_WIDTH + 1, D_CONV)), vec, vec,
                  _whole((D_CONV, D_CONV)), _whole((4, POOL_GROUP, POOL_GROUP)), vec,
                  _whole((D_CONV + D_POOL, d)), _whole((nq, d, nb))],
        out_specs=[behind(d), behind(D_IN), _whole((D_CONV, D_CONV)), _whole((4, POOL_GROUP, POOL_GROUP)),
                   _whole((CONV_WIDTH + 1, D_CONV)), vec, vec, vec, vec, _whole((1, d))],
        out_shape=[_sds((t_len, d), F32), _sds((t_len, D_IN), BF16), _sds((D_CONV, D_CONV), F32),
                   _sds((4, POOL_GROUP, POOL_GROUP), F32), _sds((CONV_WIDTH + 1, D_CONV), F32), _sds((1, D_CONV), F32),
                   _sds((1, D_CONV), F32), _sds((1, D_CONV), F32), _sds((1, D_POOL), F32), _sds((1, d), F32)],
        scratch_shapes=[pltpu.VMEM((2, tm, D_CONV), F32), pltpu.VMEM((2, tm, D_POOL), F32),
                        pltpu.VMEM((tm + HALO, D_CONV), F32), pltpu.VMEM((tm + HALO, D_CONV), F32),
                        pltpu.VMEM((tm + HALO, D_POOL), F32),
                        pltpu.VMEM((SUBLANES - 1, tm + SHIFT_ROWS, D_CONV), F32),
                        pltpu.VMEM((SUBLANES - 1, tm + SHIFT_ROWS, D_CONV), F32)]
        + [pltpu.VMEM((tm + HALO, D_POOL), F32)] * 2,
        args=[dx2, u1, u3, mixed, proj, proj, x1, dx2, gain, conv_dw, ln_g, ln_b, conv_pw, pool_w, pool_scale,
              w_out, w_in], cargos=cargos, vmem_limit_bytes=VMEM_LIMIT_BYTES_LARGE)


def _final_norm_loss(x3, target, gain, name):
    t_len, d = x3.shape
    tm = min(2 * TM_FFN, t_len)

    def body(x_ref, t_ref, g_ref, dx_ref, loss_ref, dgain_ref):
        @pl.when(pl.program_id(0) == 0)
        def _():
            loss_ref[...] = jnp.zeros_like(loss_ref)
            dgain_ref[...] = jnp.zeros_like(dgain_ref)

        r, n = _rms_stats(x_ref[...])
        err = n * g_ref[...] - t_ref[...]
        per_tok = jnp.sum(err * err, axis=-1, keepdims=True) * (1.0 / d)
        loss_ref[...] += 0.5 * jnp.sum(per_tok, axis=0, keepdims=True)
        dy = err * (1.0 / d)
        dgain_ref[...] += jnp.sum(dy * n, axis=0, keepdims=True)
        dx_ref[...] = _rms_bwd(dy, n, r, g_ref[...])

    tok = pl.BlockSpec((tm, d), lambda i: (i, 0))
    outs, _ = _call(
        body, name=name, grid=(t_len // tm,),
        in_specs=[tok, tok, pl.BlockSpec((1, d), lambda i: (0, 0))],
        out_specs=[tok, pl.BlockSpec((1, 128), lambda i: (0, 0)), pl.BlockSpec((1, d), lambda i: (0, 0))],
        out_shape=[_sds((t_len, d), F32), _sds((1, 128), F32), _sds((1, d), F32)],
        args=[x3, target, gain])
    return outs


def _row_tile(rows):
    return rows // 2 if rows % 64 == 0 else rows


def _adamw_math(w, g, m, v):
    m = ADAM_B1 * m + (1.0 - ADAM_B1) * g
    v = ADAM_B2 * v + (1.0 - ADAM_B2) * (g * g)
    m_hat = m / (1.0 - ADAM_B1 ** ADAM_STEP)
    v_hat = v / (1.0 - ADAM_B2 ** ADAM_STEP)
    delta = -ADAM_LR * (m_hat / (jnp.sqrt(v_hat) + ADAM_EPS) + ADAM_WD * w)
    return delta, m, v


def _adamw(parts, w, m, v, name):
    r, c = w.shape
    n = len(parts)
    tr = _row_tile(r)

    def body(*refs):
        g = None
        for p_ref in refs[:n]:
            s = p_ref[0].astype(F32)
            for k in range(1, p_ref.shape[0]):
                s = s + p_ref[k].astype(F32)
            g = s if g is None else g + s
        w_ref, m_ref, v_ref, g_out, d_out, m_out, v_out = refs[n:]
        delta, nm, nv = _adamw_math(w_ref[...], g, m_ref[...], v_ref[...])
        g_out[...] = g
        d_out[...] = delta
        m_out[...] = nm
        v_out[...] = nv

    blk = pl.BlockSpec((tr, c), lambda i: (i, 0))
    p_specs = [pl.BlockSpec((p.shape[0], tr, c), lambda i: (0, i, 0)) for p in parts]
    outs, _ = _call(body, name=name, grid=(r // tr,), in_specs=p_specs + [blk, blk, blk],
                    out_specs=[blk] * 4, out_shape=[_sds((r, c), F32)] * 4, args=[*parts, w, m, v])
    return outs


FFN_W = ("w_gate", "w_up", "w_down")
MID = ("w_in", "conv_dw", "conv_pw", "w_out")
SMALL_1024 = ("ffn1_norm", "mix_norm", "ffn2_norm", "final_norm")
SMALL_512 = ("conv_dw_b", "conv_ln_g", "conv_ln_b", "pool_scale")
WEIGHTS = ("ffn1_norm", "ffn1_w_gate", "ffn1_w_up", "ffn1_w_down", "mix_norm", "w_in", "conv_dw", "conv_dw_b",
           "conv_ln_g", "conv_ln_b", "conv_pw", "pool_w", "pool_scale", "w_out", "ffn2_norm", "ffn2_w_gate",
           "ffn2_w_up", "ffn2_w_down", "final_norm")
PACK_ROWS = 72
PACK_LOSS_ROW = 70


def _pad_rows(a, rows):
    return jnp.pad(a, ((0, rows - a.shape[0]), (0, 0)))


def _pack_small(t, spare=None):
    rows = [t[k].reshape(1, D_MODEL) for k in SMALL_1024]
    rows.append(jnp.concatenate([t["conv_dw_b"].reshape(1, -1), t["conv_ln_g"].reshape(1, -1)], axis=1))
    rows.append(jnp.concatenate([t["conv_ln_b"].reshape(1, -1), t["pool_scale"].reshape(1, -1)], axis=1))
    rows.append(t["pool_w"].reshape(64, D_MODEL))
    if spare is not None:
        rows.append(jnp.pad(spare, ((0, 0), (0, D_MODEL - spare.shape[1]))))
    return _pad_rows(jnp.concatenate(rows, axis=0), PACK_ROWS)


def _unpack_small(p):
    out = {k: p[i] for i, k in enumerate(SMALL_1024)}
    out["conv_dw_b"], out["conv_ln_g"] = p[4, :D_CONV], p[4, D_CONV:]
    out["conv_ln_b"], out["pool_scale"] = p[5, :D_CONV], p[5, D_CONV:]
    out["pool_w"] = p[6:70].reshape(4, POOL_GROUP, POOL_GROUP)
    return out


def _as_stored(name, a):
    if name.endswith(("w_gate", "w_up")):
        return a.T
    if name == "conv_dw":
        return _pad_rows(a, CONV_WIDTH + 1)
    return a


def _as_given(name, a):
    if name.endswith(("w_gate", "w_up")):
        return a.T
    if name == "conv_dw":
        return a[:CONV_WIDTH]
    return a


def kernel(x, ffn1_norm, ffn1_w_gate, ffn1_w_up, ffn1_w_down, mix_norm, w_in, conv_dw, conv_dw_b, conv_ln_g, conv_ln_b, conv_pw, pool_w, pool_scale, w_out, ffn2_norm, ffn2_w_gate, ffn2_w_up, ffn2_w_down, final_norm, loss_target, m_ffn1_norm, m_ffn1_w_gate, m_ffn1_w_up, m_ffn1_w_down, m_mix_norm, m_w_in, m_conv_dw, m_conv_dw_b, m_conv_ln_g, m_conv_ln_b, m_conv_pw, m_pool_w, m_pool_scale, m_w_out, m_ffn2_norm, m_ffn2_w_gate, m_ffn2_w_up, m_ffn2_w_down, m_final_norm, v_ffn1_norm, v_ffn1_w_gate, v_ffn1_w_up, v_ffn1_w_down, v_mix_norm, v_w_in, v_conv_dw, v_conv_dw_b, v_conv_ln_g, v_conv_ln_b, v_conv_pw, v_pool_w, v_pool_scale, v_w_out, v_ffn2_norm, v_ffn2_w_gate, v_ffn2_w_up, v_ffn2_w_down, v_final_norm):
    given = dict(locals())
    wts = {k: given[k] for k in WEIGHTS}
    mom_m = {k: given["m_" + k] for k in WEIGHTS}
    mom_v = {k: given["v_" + k] for k in WEIGHTS}
    xt, target = x[0], loss_target[0]

    shard = {k: _as_stored(k, wts[k]) if k == "conv_dw" else _as_stored(k, wts[k]).astype(BF16)
             for k in WEIGHTS if k.endswith(FFN_W) or k in MID}
    w = {k: wts[k].reshape(1, -1) for k in SMALL_1024 + SMALL_512}
    w["pool_w"] = wts["pool_w"].astype(BF16)

    (h1, s1, p1, a1, w["ffn1_w_gate"], w["ffn1_w_up"]), ((w["ffn1_w_down"],),) = _ffn_up_gather(
        xt, w["ffn1_norm"], shard["ffn1_w_gate"], shard["ffn1_w_up"], "ffn1_up_gather",
        cargos=[Cargo("gather_slots", [shard["ffn1_w_down"]])])
    x1, (mid, (w["ffn2_w_down"],)) = _ffn_down(
        xt, a1, w["ffn1_w_down"], "ffn1_down",
        cargos=[Cargo("gather_chips", [shard[k] for k in MID]), Cargo("gather_slots", [shard["ffn2_w_down"]])])
    w["w_in"] = mid[0]
    w["conv_dw"] = mid[1].transpose(1, 0, 2).reshape(CONV_WIDTH + 1, D_CONV)
    w["conv_pw"] = mid[2].reshape(D_CONV, D_CONV)
    w["w_out"] = mid[3].reshape(D_CONV + D_POOL, D_MODEL)
    (x2, h2, proj, u1, u3, mixed, cat), ((w["ffn2_w_gate"], w["ffn2_w_up"]),) = _mix_fwd(
        x1, w["mix_norm"], w["w_in"], w["conv_dw"], w["conv_dw_b"], w["conv_ln_g"], w["conv_ln_b"], w["conv_pw"],
        w["pool_w"], w["pool_scale"], w["w_out"], "mix_fwd",
        cargos=[Cargo("gather_slots", [shard["ffn2_w_gate"], shard["ffn2_w_up"]])])
    x3, h3, s2, p2, a2 = _ffn_fwd(x2, w["ffn2_norm"], w["ffn2_w_gate"], w["ffn2_w_up"], w["ffn2_w_down"], "ffn2_fwd")
    dx3, loss_share, d_final = _final_norm_loss(x3, target, w["final_norm"], "final_norm_loss")

    g = {"final_norm": d_final}
    sums = {}

    def landed(names, parts):
        sums.update(zip(names, parts))

    dx2, g["ffn2_norm"], df2, dg2, du2 = _ffn_bwd(dx3, x2, w["ffn2_norm"], s2, p2, w["ffn2_w_gate"],
                                                   w["ffn2_w_up"], w["ffn2_w_down"], "ffn2_bwd")
    def ffn_wgrad(names, hids, tok, kernel_name, cargos=()):
        parts, cargo_outs = _wgrad_hid_tok_scatter(hids, tok, kernel_name, cargos=cargos)
        landed(names, parts)
        return cargo_outs

    ffn_wgrad(["ffn2_w_gate", "ffn2_w_up"], [dg2, du2], h3, "ffn2_dw_gate_up")
    ffn_wgrad(["ffn2_w_down"], [a2], df2, "ffn2_dw_down")
    (dx1, dproj, g_pw, g["pool_w"], g_dw, g["conv_dw_b"], g["conv_ln_g"], g["conv_ln_b"], g["pool_scale"],
     g["mix_norm"]), (swapped2,) = _mix_bwd(
        dx2, u1, u3, mixed, proj, x1, w["mix_norm"], w["conv_dw"], w["conv_ln_g"], w["conv_ln_b"], w["conv_pw"],
        w["pool_w"], w["pool_scale"], w["w_out"], w["w_in"], "mix_bwd",
        cargos=[Cargo("swap", [sums["ffn2_" + k] for k in FFN_W])])
    g_out, _ = _wgrad_2d(cat, dx2, 1, BF16, "dw_out")
    slabs = [g_pw.reshape(N_CHIPS, D_CONV // N_CHIPS, D_CONV),
             g_out.reshape(N_CHIPS, (D_CONV + D_POOL) // N_CHIPS, D_MODEL)]
    g_in, (parts,) = _wgrad_2d(h2, dproj, N_CHIPS, BF16, "dw_in", cargos=[Cargo("scatter_chips", slabs)])
    landed(["conv_pw", "w_out"], parts)
    dx, g["ffn1_norm"], df1, dg1, du1_ = _ffn_bwd(dx1, xt, w["ffn1_norm"], s1, p1, w["ffn1_w_gate"],
                                                   w["ffn1_w_up"], w["ffn1_w_down"], "ffn1_bwd")
    slabs = [g_in, g_dw.reshape(CONV_WIDTH + 1, N_CHIPS, D_CONV // N_CHIPS).transpose(1, 0, 2)]
    (parts,) = ffn_wgrad(["ffn1_w_gate", "ffn1_w_up"], [dg1, du1_], h1, "ffn1_dw_gate_up",
                         cargos=[Cargo("scatter_chips", slabs)])
    landed(["w_in", "conv_dw"], parts)
    swapped_mid, swapped_gate_up, small_parts = ffn_wgrad(
        ["ffn1_w_down"], [a1], df1, "ffn1_dw_down",
        cargos=[Cargo("swap", [sums[k] for k in MID]), Cargo("swap", [sums["ffn1_w_gate"], sums["ffn1_w_up"]]),
                Cargo("gather_devices", [_pack_small(g, spare=loss_share)])])
    swapped_down = _exchange(Cargo("swap", [sums["ffn1_w_down"]]), "swap_last")

    theirs = dict(zip(["ffn2_" + k for k in FFN_W], swapped2))
    theirs.update(zip(MID, swapped_mid))
    theirs.update(ffn1_w_gate=swapped_gate_up[0], ffn1_w_up=swapped_gate_up[1], ffn1_w_down=swapped_down[0])
    grads, deltas, new_m, new_v = {}, {}, {}, {}
    for k in theirs:
        res = _adamw([sums[k], theirs[k]], _as_stored(k, wts[k]), _as_stored(k, mom_m[k]),
                     _as_stored(k, mom_v[k]), "adamw_" + k)
        grads[k], deltas[k], new_m[k], new_v[k] = [_as_given(k, t) for t in res]
    res = _adamw(small_parts, _pack_small(wts), _pack_small(mom_m), _pack_small(mom_v), "adamw_small")
    for dst, packed in zip((grads, deltas, new_m, new_v), res):
        dst.update(_unpack_small(packed))
    loss = res[0][PACK_LOSS_ROW, 0]

    out = [loss, dx[None]]
    for group in (grads, deltas, new_m, new_v):
        out += [group[k] for k in WEIGHTS]
    return tuple(out)
```

```python
import functools

import jax
import jax.numpy as jnp
from jax import lax
from jax.experimental import pallas as pl
from jax.experimental.pallas import tpu as pltpu

F32 = jnp.float32
BF16 = jnp.bfloat16
MESH = pl.DeviceIdType.MESH

N_CHIPS = 4
N_DEV = 8
D_MODEL = 1024
D_CONV = 512
D_POOL = 512
CONV_WIDTH = 31
POOL_WINDOWS = (2, 4, 8, 16)
POOL_GROUP = 128
D_IN = 2 * D_CONV + D_POOL
HALO = 32
RMS_EPS = 1e-6
LN_EPS = 1e-5
FFN_RES_WEIGHT = 0.5
ADAM_LR = 0.001
ADAM_B1 = 0.9
ADAM_B2 = 0.999
ADAM_EPS = 1e-08
ADAM_WD = 0.01
ADAM_STEP = 10
VMEM_LIMIT_BYTES = 52 * 1024 * 1024
VMEM_LIMIT_BYTES_LARGE = 58 * 1024 * 1024
TM_FFN = 512
TM_MIX = 256
TM_MIX_FWD = 512
TT_WGRAD = 2048
STRIP = 16
SLOTS_PER_STEP = 2
SLOTS_PER_STEP_FWD = 4
SUBLANES = 8
RELAY_AT_EIGHTHS = 7

HBM = pl.BlockSpec(memory_space=pl.ANY)


def _dot(a, b):
    return jnp.dot(a, b, preferred_element_type=F32)


def _dot_nt(a, b):
    return lax.dot_general(a, b, (((1,), (1,)), ((), ())), preferred_element_type=F32)


def _dot_tn(a, b):
    return lax.dot_general(a, b, (((0,), (0,)), ((), ())), preferred_element_type=F32)


def _sds(shape, dtype):
    return jax.ShapeDtypeStruct(shape, dtype)


def _rms_stats(xv):
    r = lax.rsqrt(jnp.mean(xv * xv, axis=-1, keepdims=True) + RMS_EPS)
    return r, xv * r


def _swiglu_saved(gate, up):
    sig = jax.nn.sigmoid(gate)
    silu = gate * sig
    return silu, up * (sig * (1.0 + gate * (1.0 - sig))), silu * up


def _rms_bwd(dh, n, r, gain):
    dn = dh * gain
    return r * (dn - n * jnp.mean(dn * n, axis=-1, keepdims=True))


def _place():
    x, y, c = lax.axis_index("x"), lax.axis_index("y"), lax.axis_index("c")
    return x, y, c, [(1 - x, y), (x, 1 - y), (1 - x, 1 - y)]


class Cargo:
    def __init__(self, kind, arrays):
        self.kind, self.arrays = kind, list(arrays)
        n = len(self.arrays)
        self.two_level = kind in ("gather_slots", "gather_chips")
        if self.two_level:
            self.out_shape = [_sds((N_CHIPS,) + a.shape, a.dtype) for a in self.arrays]
        elif kind == "gather_devices":
            self.out_shape = [_sds((N_DEV,) + a.shape, a.dtype) for a in self.arrays]
        else:
            self.out_shape = [_sds(a.shape, a.dtype) for a in self.arrays]
        n_remote = n * {"swap": 1, "gather_devices": N_DEV - 1}.get(kind, N_CHIPS - 1)
        n_own = 0 if kind == "swap" else n
        n_relay = n_remote if self.two_level else 0
        dma = pltpu.SemaphoreType.DMA
        self.scratch = [dma((n_remote,)), dma((n_remote,)), dma((max(n_own, 1),)),
                        dma((max(n_relay, 1),)), dma((max(n_relay, 1),))]

    def _plan(self, ins, outs):
        x, y, c, chips = _place()
        q = 2 * x + y
        sibling = (x, y, 1 - c)
        own, remote, relays = [], [], []
        for a, o in zip(ins, outs):
            if self.two_level:
                half = a.shape[0] // 2
                mine = pl.ds(pl.multiple_of(c * half, SUBLANES), half)
                theirs = pl.ds(pl.multiple_of((1 - c) * half, SUBLANES), half)
                own.append((a, o.at[0 if self.kind == "gather_slots" else q]))
                for j, (px, py) in enumerate(chips):
                    there, here = (j + 1, j + 1) if self.kind == "gather_slots" else (q, 2 * px + py)
                    remote.append((a.at[mine], o.at[there, mine], o.at[here, mine], (px, py, c)))
                    relays.append((o.at[here, mine], o.at[here, mine], o.at[here, theirs], sibling))
            elif self.kind == "scatter_chips":
                own.append((a.at[q], o.at[q]))
                remote += [(a.at[2 * px + py], o.at[q], o.at[2 * px + py], (px, py, c)) for px, py in chips]
            elif self.kind == "swap":
                remote.append((a, o, o, sibling))
            else:
                own.append((a, o.at[4 * x + 2 * y + c]))
                for k in range(1, N_DEV):
                    px, py, pc = x ^ (k >> 2 & 1), y ^ (k >> 1 & 1), c ^ (k & 1)
                    remote.append((a, o.at[4 * x + 2 * y + c], o.at[4 * px + 2 * py + pc], (px, py, pc)))
        return own, remote, relays

    @staticmethod
    def _copies(entries, send_sems, recv_sems):
        out = []
        for k, (src, dst, landed, peer) in enumerate(entries):
            def make(dst_ref, k=k, src=src, peer=peer):
                return pltpu.make_async_remote_copy(src_ref=src, dst_ref=dst_ref, send_sem=send_sems.at[k],
                                                    recv_sem=recv_sems.at[k], device_id=peer, device_id_type=MESH)
            out.append((make(dst), make(landed)))
        return out

    def start(self, ins, outs, sems):
        own, remote, _ = self._plan(ins, outs)
        for k, (src, dst) in enumerate(own):
            pltpu.make_async_copy(src, dst, sems[2].at[k]).start()
        for mine, _ in self._copies(remote, sems[0], sems[1]):
            mine.start()

    def relay(self, ins, outs, sems):
        _, remote, relays = self._plan(ins, outs)
        passed = self._copies(relays, sems[3], sems[4])
        for (_, arriving), (mine, _) in zip(self._copies(remote, sems[0], sems[1]), passed):
            arriving.wait_recv()
            mine.start()

    def wait(self, ins, outs, sems):
        own, remote, relays = self._plan(ins, outs)
        for mine, arriving in self._copies(remote, sems[0], sems[1]):
            mine.wait_send()
            if not self.two_level:
                arriving.wait_recv()
        for mine, arriving in self._copies(relays, sems[3], sems[4]):
            mine.wait_send()
            arriving.wait_recv()
        for k, (src, dst) in enumerate(own):
            pltpu.make_async_copy(src, dst, sems[2].at[k]).wait()


N_CARGO_SEMS = 5


def _call(body, *, name, grid, in_specs, out_specs, out_shape, args, scratch_shapes=(), cargos=(),
          vmem_limit_bytes=VMEM_LIMIT_BYTES):
    n_in, n_out, n_scr = len(in_specs), len(out_specs), len(scratch_shapes)
    c_in = [len(cg.arrays) for cg in cargos]
    n_cin = sum(c_in)

    def wrapped(*refs):
        ins = refs[:n_in]
        cins = refs[n_in:n_in + n_cin]
        outs = refs[n_in + n_cin:n_in + n_cin + n_out]
        couts = refs[n_in + n_cin + n_out:n_in + 2 * n_cin + n_out]
        scr = refs[n_in + 2 * n_cin + n_out:n_in + 2 * n_cin + n_out + n_scr]
        sems = refs[n_in + 2 * n_cin + n_out + n_scr:]
        step, n_steps = 0, 1
        for ax, size in enumerate(grid):
            step = step * size + pl.program_id(ax)
            n_steps *= size

        def each(method, only_two_level=False):
            at = 0
            for k, cg in enumerate(cargos):
                if cg.two_level or not only_two_level:
                    getattr(cg, method)(cins[at:at + c_in[k]], couts[at:at + c_in[k]],
                                        sems[N_CARGO_SEMS * k:N_CARGO_SEMS * (k + 1)])
                at += c_in[k]

        body(*ins, *outs, *scr)
        if cargos:
            pl.when(step == 0)(lambda: each("start"))
        if any(cg.two_level for cg in cargos):
            pl.when(step == (RELAY_AT_EIGHTHS * n_steps) // 8)(lambda: each("relay", only_two_level=True))
        if cargos:
            pl.when(step == n_steps - 1)(lambda: each("wait"))

    res = pl.pallas_call(
        wrapped, name=name, grid=grid,
        in_specs=list(in_specs) + [HBM] * n_cin,
        out_specs=list(out_specs) + [HBM] * n_cin,
        out_shape=list(out_shape) + [s for cg in cargos for s in cg.out_shape],
        scratch_shapes=list(scratch_shapes) + [s for cg in cargos for s in cg.scratch],
        compiler_params=pltpu.CompilerParams(dimension_semantics=("arbitrary",) * len(grid),
                                             vmem_limit_bytes=vmem_limit_bytes),
    )(*args, *[a for cg in cargos for a in cg.arrays])
    outs, rest = list(res[:n_out]), list(res[n_out:])
    cargo_outs = []
    for k in c_in:
        cargo_outs.append(rest[:k])
        rest = rest[k:]
    return outs, cargo_outs


def _exchange(cargo, name):
    _, (outs,) = _call(lambda: None, name=name, grid=(1,), in_specs=[], out_specs=[], out_shape=[], args=[],
                       cargos=[cargo])
    return outs


def _ffn_up_gather(x, gain, wg_t, wu_t, name, cargos=()):
    t_len, d = x.shape
    fq = wg_t.shape[0]
    tm = min(TM_FFN, t_len)
    n_tiles = t_len // tm
    relay_tile = n_tiles // 2
    fetch_tile = min(relay_tile + 1, n_tiles - 1)

    def body(x_ref, g_ref, wg_in, wu_in, h_ref, s_ref, p_ref, a_ref, wg_all, wu_all,
             wg_v, wu_v, h_all, send_sems, recv_sems, pass_send_sems, pass_recv_sems, own_sems, load_sems):
        s = pl.program_id(0)
        i = pl.program_id(1)
        x_, y_, c_, chips = _place()
        shards = ((wg_in, wg_all, wg_v), (wu_in, wu_all, wu_v))
        mine = pl.ds(pl.multiple_of(c_ * (fq // 2), SUBLANES), fq // 2)
        theirs = pl.ds(pl.multiple_of((1 - c_) * (fq // 2), SUBLANES), fq // 2)

        def to_peer(k, j):
            w_in, w_all, _ = shards[k]
            return pltpu.make_async_remote_copy(
                src_ref=w_in.at[mine], dst_ref=w_all.at[j + 1, mine], send_sem=send_sems.at[3 * k + j],
                recv_sem=recv_sems.at[3 * k + j], device_id=(*chips[j], c_), device_id_type=MESH)

        def to_sibling(k, j, landing=False):
            w_all = shards[k][1]
            return pltpu.make_async_remote_copy(
                src_ref=w_all.at[j + 1, mine], dst_ref=w_all.at[j + 1, theirs if landing else mine],
                send_sem=pass_send_sems.at[3 * k + j], recv_sem=pass_recv_sems.at[3 * k + j],
                device_id=(x_, y_, 1 - c_), device_id_type=MESH)

        def keep(k):
            return pltpu.make_async_copy(shards[k][0], shards[k][1].at[0], own_sems.at[k])

        @pl.when((s == 0) & (i == 0))
        def _():
            for j in range(N_CHIPS - 1):
                for k in range(2):
                    to_peer(k, j).start()
            for k in range(2):
                keep(k).start()

        def load(k, slot):
            src = shards[k][0] if slot == 0 else shards[k][1].at[slot]
            return pltpu.make_async_copy(src, shards[k][2].at[slot % 2], load_sems.at[k])

        @pl.when((s == 0) & (i == 0))
        def _():
            for k in range(2):
                load(k, 0).start()
            for k in range(2):
                load(k, 0).wait()

        def pass_on(slot):
            for k in range(2):
                to_peer(k, slot - 1).wait_recv()
                to_sibling(k, slot - 1).start()

        def fetch(slot):
            for k in range(2):
                to_sibling(k, slot - 1, landing=True).wait_recv()
                load(k, slot).start()

        for slot in range(1, N_CHIPS):
            pl.when((s == slot - 1) & (i == relay_tile))(functools.partial(pass_on, slot))
            pl.when((s == slot - 1) & (i == fetch_tile))(functools.partial(fetch, slot))

            @pl.when((s == slot) & (i == 0))
            def _():
                for k in range(2):
                    load(k, slot).wait()

        @pl.when(s == 0)
        def _():
            _, n = _rms_stats(x_ref[...])
            h_new = (n * g_ref[...]).astype(BF16)
            h_ref[...] = h_new
            h_all[i] = h_new

        h = h_all[i]
        silu, dgate, act = _swiglu_saved(_dot_nt(h, wg_v[s % 2]), _dot_nt(h, wu_v[s % 2]))
        s_ref[...] = silu.astype(BF16)
        p_ref[...] = dgate.astype(BF16)
        a_ref[...] = act.astype(BF16)

        @pl.when((s == N_CHIPS - 1) & (i == n_tiles - 1))
        def _():
            for k in range(2):
                for j in range(N_CHIPS - 1):
                    to_peer(k, j).wait_send()
                    to_sibling(k, j).wait_send()
                keep(k).wait()

    tok = pl.BlockSpec((tm, d), lambda s, i: (jnp.where(s == 0, i, n_tiles - 1), 0))
    hid = pl.BlockSpec((None, tm, fq), lambda s, i: (s, i, 0))
    outs, cargo_outs = _call(
        body, name=name, grid=(N_CHIPS, n_tiles),
        in_specs=[tok, pl.BlockSpec((1, d), lambda s, i: (0, 0)), HBM, HBM],
        out_specs=[tok, hid, hid, hid, HBM, HBM],
        out_shape=[_sds((t_len, d), BF16)] + [_sds((N_CHIPS, t_len, fq), BF16)] * 3
        + [_sds((N_CHIPS, fq, d), BF16)] * 2,
        scratch_shapes=[pltpu.VMEM((2, fq, d), BF16), pltpu.VMEM((2, fq, d), BF16),
                        pltpu.VMEM((n_tiles, tm, d), BF16)]
        + [pltpu.SemaphoreType.DMA((6,))] * 4 + [pltpu.SemaphoreType.DMA((2,))] * 2,
        args=[x, gain, wg_t, wu_t], cargos=cargos)
    return outs, cargo_outs


def _load_once(hbm_refs, vmem_refs, sems, first):
    @pl.when(first)
    def _():
        copies = [pltpu.make_async_copy(src, dst, sems.at[k]) for k, (src, dst) in enumerate(zip(hbm_refs, vmem_refs))]
        for cp in copies:
            cp.start()
        for cp in copies:
            cp.wait()


def _ffn_down(x, act, wd, name, cargos=()):
    t_len, d = x.shape
    nq, fq, _ = wd.shape
    tm = min(TM_FFN, t_len)

    def body(x_ref, a_ref, wd_ref, xo_ref):
        y = _dot(a_ref[0], wd_ref[0])
        for j in range(1, nq):
            y = y + _dot(a_ref[j], wd_ref[j])
        xo_ref[...] = x_ref[...] + FFN_RES_WEIGHT * y

    tok = pl.BlockSpec((tm, d), lambda i: (i, 0))
    (xo,), cargo_outs = _call(
        body, name=name, grid=(t_len // tm,),
        in_specs=[tok, pl.BlockSpec((nq, tm, fq), lambda i: (0, i, 0)), pl.BlockSpec((nq, fq, d), lambda i: (0, 0, 0))],
        out_specs=[tok], out_shape=[_sds((t_len, d), F32)], args=[x, act, wd], cargos=cargos)
    return xo, cargo_outs


def _ffn_fwd(x, gain, wg_t, wu_t, wd, name):
    t_len, d = x.shape
    nq, fq, _ = wd.shape
    tm = min(TM_FFN, t_len)

    def body(x_ref, g_ref, wg_hbm, wu_hbm, wd_hbm, xo_ref, h_ref, s_ref, p_ref, a_ref,
             h_s, acc, wg_v, wu_v, wd_v, load_sems):
        i = pl.program_id(0)
        j = pl.program_id(1)
        _load_once((wg_hbm, wu_hbm, wd_hbm), (wg_v, wu_v, wd_v), load_sems, (i == 0) & (j == 0))

        @pl.when(j == 0)
        def _():
            _, n = _rms_stats(x_ref[...])
            h = (n * g_ref[...]).astype(BF16)
            h_s[...] = h
            h_ref[...] = h
            acc[...] = jnp.zeros_like(acc)

        h = h_s[...]
        y = None
        for jj in range(SLOTS_PER_STEP_FWD):
            slot = j * SLOTS_PER_STEP_FWD + jj
            silu, dgate, act = _swiglu_saved(_dot_nt(h, wg_v[slot]), _dot_nt(h, wu_v[slot]))
            s_ref[jj] = silu.astype(BF16)
            p_ref[jj] = dgate.astype(BF16)
            a_ref[jj] = act.astype(BF16)
            part = _dot(a_ref[jj], wd_v[slot])
            y = part if y is None else y + part
        acc[...] += y

        @pl.when(j == nq // SLOTS_PER_STEP_FWD - 1)
        def _():
            xo_ref[...] = x_ref[...] + FFN_RES_WEIGHT * acc[...]

    tok = pl.BlockSpec((tm, d), lambda i, j: (i, 0))
    hid = pl.BlockSpec((SLOTS_PER_STEP_FWD, tm, fq), lambda i, j: (j, i, 0))
    outs, _ = _call(
        body, name=name, grid=(t_len // tm, nq // SLOTS_PER_STEP_FWD),
        in_specs=[tok, pl.BlockSpec((1, d), lambda i, j: (0, 0)), HBM, HBM, HBM],
        out_specs=[tok, tok, hid, hid, hid],
        out_shape=[_sds((t_len, d), F32), _sds((t_len, d), BF16)] + [_sds((nq, t_len, fq), BF16)] * 3,
        scratch_shapes=[pltpu.VMEM((tm, d), BF16), pltpu.VMEM((tm, d), F32)]
        + [pltpu.VMEM((nq, fq, d), BF16)] * 3 + [pltpu.SemaphoreType.DMA((3,))],
        args=[x, gain, wg_t, wu_t, wd], vmem_limit_bytes=VMEM_LIMIT_BYTES_LARGE)
    return outs


def _ffn_bwd(dy, x_in, gain, silu, dgate_du, wg_t, wu_t, wd, name):
    t_len, d = dy.shape
    nq, fq, _ = wd.shape
    tm = min(TM_FFN, t_len)

    def body(dy_ref, x_ref, g_ref, s_ref, p_ref, wg_hbm, wu_hbm, wd_hbm,
             dx_ref, dgain_ref, df_ref, dg_ref, du_ref, df_s, dh_acc, dact_s, wg_v, wu_v, wd_v, load_sems):
        i = pl.program_id(0)
        j = pl.program_id(1)
        _load_once((wg_hbm, wu_hbm, wd_hbm), (wg_v, wu_v, wd_v), load_sems, (i == 0) & (j == 0))

        @pl.when((i == 0) & (j == 0))
        def _():
            dgain_ref[...] = jnp.zeros_like(dgain_ref)

        @pl.when(j == 0)
        def _():
            df = (FFN_RES_WEIGHT * dy_ref[...]).astype(BF16)
            df_s[...] = df
            df_ref[...] = df
            dh_acc[...] = jnp.zeros_like(dh_acc)

        slots = [j * SLOTS_PER_STEP + jj for jj in range(SLOTS_PER_STEP)]
        for jj, slot in enumerate(slots):
            dact_s[jj] = _dot_nt(df_s[...], wd_v[slot])

        for jj in range(SLOTS_PER_STEP):
            for r0 in range(0, tm, STRIP):
                rows = slice(r0, r0 + STRIP)
                dact = dact_s[jj, rows, :]
                dg_ref[jj, rows, :] = (dact * p_ref[jj, rows, :].astype(F32)).astype(BF16)
                du_ref[jj, rows, :] = (dact * s_ref[jj, rows, :].astype(F32)).astype(BF16)

        dh = None
        for jj, slot in enumerate(slots):
            part = _dot(dg_ref[jj], wg_v[slot]) + _dot(du_ref[jj], wu_v[slot])
            dh = part if dh is None else dh + part
        dh_acc[...] += dh

        @pl.when(j == nq // SLOTS_PER_STEP - 1)
        def _():
            r, n = _rms_stats(x_ref[...])
            dh = dh_acc[...]
            dgain_ref[...] += jnp.sum(dh * n, axis=0, keepdims=True)
            dx_ref[...] = dy_ref[...] + _rms_bwd(dh, n, r, g_ref[...])

    tok = pl.BlockSpec((tm, d), lambda i, j: (i, 0))
    vec = pl.BlockSpec((1, d), lambda i, j: (0, 0))
    hid = pl.BlockSpec((SLOTS_PER_STEP, tm, fq), lambda i, j: (j, i, 0))
    outs, _ = _call(
        body, name=name, grid=(t_len // tm, nq // SLOTS_PER_STEP),
        in_specs=[tok, tok, vec, hid, hid, HBM, HBM, HBM],
        out_specs=[tok, vec, tok, hid, hid],
        out_shape=[_sds((t_len, d), F32), _sds((1, d), F32), _sds((t_len, d), BF16),
                   _sds((nq, t_len, fq), BF16), _sds((nq, t_len, fq), BF16)],
        scratch_shapes=[pltpu.VMEM((tm, d), BF16), pltpu.VMEM((tm, d), F32),
                        pltpu.VMEM((SLOTS_PER_STEP, tm, fq), F32)]
        + [pltpu.VMEM((nq, fq, d), BF16)] * 3 + [pltpu.SemaphoreType.DMA((3,))],
        args=[dy, x_in, gain, silu, dgate_du, wg_t, wu_t, wd], vmem_limit_bytes=VMEM_LIMIT_BYTES_LARGE)
    return outs


def _wgrad(lhs, rhs, l_spec, r_spec, out_shape, out_spec, acc_shape, grid, name, cargos=()):
    n_t = grid[-1]
    t_axis = len(grid) - 1

    def body(l_ref, r_ref, o_ref, acc):
        t = pl.program_id(t_axis)

        @pl.when(t == 0)
        def _():
            acc[...] = jnp.zeros_like(acc)

        acc[...] += _dot_tn(l_ref[...].astype(BF16), r_ref[...].astype(BF16))

        @pl.when(t == n_t - 1)
        def _():
            o_ref[...] = acc[...].astype(o_ref.dtype)

    (out,), cargo_outs = _call(
        body, name=name, grid=grid, in_specs=[l_spec, r_spec], out_specs=[out_spec], out_shape=[out_shape],
        scratch_shapes=[pltpu.VMEM(acc_shape, F32)], args=[lhs, rhs], cargos=cargos)
    return out, cargo_outs


def _wgrad_hid_tok_scatter(hids, tok, name, cargos=()):
    t_len, d = tok.shape
    n_w = len(hids)
    nq, _, fq = hids[0].shape
    half = fq // 2
    tt = min(TT_WGRAD * 2 // n_w, t_len)
    n_t = t_len // tt
    per_w = 4
    n_sem = 6

    def body(*refs):
        l_refs, r_ref, parts_refs = refs[:n_w], refs[n_w], refs[n_w + 1:2 * n_w + 1]
        scr = refs[2 * n_w + 1:]
        bufs = [scr[per_w * w:per_w * (w + 1)] for w in range(n_w)]
        zeros = scr[per_w * n_w]
        sems = [scr[per_w * n_w + 1 + n_sem * w:per_w * n_w + 1 + n_sem * (w + 1)] for w in range(n_w)]
        g = pl.program_id(0)
        t = pl.program_id(1)
        x_, y_, c_, chips = _place()
        mine = pl.ds(pl.multiple_of(c_ * half, STRIP), half)
        theirs = pl.ds(pl.multiple_of((1 - c_) * half, STRIP), half)

        def to_sibling(w, slot):
            return pltpu.make_async_remote_copy(
                src_ref=bufs[w][1].at[theirs], dst_ref=bufs[w][2].at[slot], send_sem=sems[w][0].at[slot],
                recv_sem=sems[w][1].at[slot], device_id=(x_, y_, 1 - c_), device_id_type=MESH)

        def to_peer(w, j):
            return pltpu.make_async_remote_copy(
                src_ref=bufs[w][3].at[j + 1], dst_ref=parts_refs[w].at[j + 1, mine], send_sem=sems[w][2].at[j],
                recv_sem=sems[w][3].at[j], device_id=(*chips[j], c_), device_id_type=MESH)

        def keep(w):
            return pltpu.make_async_copy(bufs[w][3].at[0], parts_refs[w].at[0, mine], sems[w][4])

        def blank(w, slot):
            return pltpu.make_async_copy(zeros, parts_refs[w].at[slot, theirs], sems[w][5].at[slot])

        @pl.when((g == 0) & (t == 0))
        def _():
            zeros[...] = jnp.zeros_like(zeros)
            for w in range(n_w):
                for slot in range(nq):
                    blank(w, slot).start()

        @pl.when(t == 0)
        def _():
            for w in range(n_w):
                bufs[w][0][...] = jnp.zeros_like(bufs[w][0])

        rhs = r_ref[...]
        for w in range(n_w):
            bufs[w][0][...] += _dot_tn(l_refs[w][...], rhs)

        for step in range(nq):
            slot = (step + 1) % nq

            @pl.when((g == step) & (t == n_t - 1))
            def _():
                for w in range(n_w):
                    acc, stage, _, _ = bufs[w]
                    if step > 0:
                        to_sibling(w, step).wait_send()
                    stage[...] = acc[...].astype(BF16)
                    to_sibling(w, slot).start()
                for w in range(n_w):
                    _, stage, pair, summed = bufs[w]
                    to_sibling(w, slot).wait_recv()
                    summed[slot] = (stage[mine, :].astype(F32) + pair[slot].astype(F32)).astype(BF16)
                    if slot > 0:
                        to_peer(w, slot - 1).start()
                    else:
                        keep(w).start()

        @pl.when((g == nq - 1) & (t == n_t - 1))
        def _():
            for w in range(n_w):
                for j in range(N_CHIPS - 1):
                    to_peer(w, j).wait()
                keep(w).wait()
                to_sibling(w, 0).wait_send()
                for slot in range(nq):
                    blank(w, slot).wait()

    dma = pltpu.SemaphoreType.DMA
    scratch = []
    for _ in range(n_w):
        scratch += [pltpu.VMEM((fq, d), F32), pltpu.VMEM((fq, d), BF16), pltpu.VMEM((nq, half, d), BF16),
                    pltpu.VMEM((nq, half, d), BF16)]
    scratch.append(pltpu.VMEM((half, d), BF16))
    for _ in range(n_w):
        scratch += [dma((nq,)), dma((nq,)), dma((N_CHIPS - 1,)), dma((N_CHIPS - 1,)), dma(()), dma((nq,))]
    parts, cargo_outs = _call(
        body, name=name, grid=(nq, n_t),
        in_specs=[pl.BlockSpec((None, tt, fq), lambda g, t: ((g + 1) % nq, t, 0))] * n_w
        + [pl.BlockSpec((tt, d), lambda g, t: (t, 0))],
        out_specs=[HBM] * n_w, out_shape=[_sds((nq, fq, d), BF16)] * n_w,
        scratch_shapes=scratch, args=[*hids, tok], cargos=cargos)
    return parts, cargo_outs


def _wgrad_2d(lhs, rhs, n_col_blocks, out_dtype, name, cargos=()):
    t_len, k = lhs.shape
    n = rhs.shape[1]
    nb = n // n_col_blocks
    tt = min(TT_WGRAD, t_len)
    return _wgrad(lhs, rhs,
                  pl.BlockSpec((tt, k), lambda q, t: (t, 0)),
                  pl.BlockSpec((tt, nb), lambda q, t: (t, q)),
                  _sds((n_col_blocks, k, nb), out_dtype),
                  pl.BlockSpec((None, k, nb), lambda q, t: (q, 0, 0)),
                  (k, nb), (n_col_blocks, t_len // tt), name, cargos)


def _layernorm_stats(u1):
    mu = jnp.mean(u1, axis=-1, keepdims=True)
    xc = u1 - mu
    rstd = lax.rsqrt(jnp.mean(xc * xc, axis=-1, keepdims=True) + LN_EPS)
    return rstd, xc * rstd


def _positions(i, tm, rows, offset=0):
    return (lax.broadcasted_iota(jnp.int32, (rows, 1), 0) + (i * tm + offset)).astype(F32)


SHIFT_ROWS = HALO - SUBLANES


def _fill_shifted(ext_s, sh_s, tm):
    for b in range(1, SUBLANES):
        sh_s[b - 1] = ext_s[pl.ds(b, tm + SHIFT_ROWS), :]


def _window(ext_s, sh_s, shift, tm):
    a, b = divmod(shift, SUBLANES)
    if b == 0:
        return ext_s[pl.ds(shift, tm), :]
    return sh_s[b - 1, pl.ds(a * SUBLANES, tm), :]


def _window_sums(ext_s, lv_a, lv_b, tm, ahead):
    g = POOL_GROUP
    sign = 1 if ahead else -1
    for n, (dst, src, c0) in enumerate(((lv_a, ext_s, 0), (lv_b, lv_a, g), (lv_a, lv_b, 2 * g)), start=1):
        lo = 0 if ahead else n * SUBLANES
        rows = tm + HALO - n * SUBLANES
        shift = sign * 2 ** (n - 1)
        dst[pl.ds(lo, rows), c0:] = src[pl.ds(lo, rows), c0:] + src[pl.ds(lo + shift, rows), c0:]
    base = 0 if ahead else HALO
    rows = pl.ds(base, tm)
    far = pl.ds(base + sign * SUBLANES, tm)
    return [lv_a[rows, 0:g], lv_b[rows, g:2 * g], lv_a[rows, 2 * g:3 * g],
            lv_a[rows, 3 * g:] + lv_a[far, 3 * g:]]


def _tile(tm, cols):
    return pl.BlockSpec((tm, cols), lambda i: (i, 0))


def _whole(shape):
    return pl.BlockSpec(shape, lambda i: (0,) * len(shape))


def _mix_fwd(x1, gain, w_in, conv_dw, conv_b, ln_g, ln_b, conv_pw, pool_w, pool_scale, w_out, name, cargos=()):
    t_len, d = x1.shape
    nq, _, nb = w_in.shape
    tm = min(TM_MIX_FWD, t_len)

    def body(x_ref, g_ref, wi_ref, dw_ref, cb_ref, lg_ref, lb_ref, pw_ref, plw_ref, ps_ref, wo_ref,
             x2_ref, h_ref, p_ref, u1_ref, u3_ref, mx_ref, cat_ref, ext_s, pext_s, sh_s, tail_s, lva_s, lvb_s):
        i = pl.program_id(0)

        @pl.when(i == 0)
        def _():
            tail_s[...] = jnp.zeros_like(tail_s)

        _, n = _rms_stats(x_ref[...])
        h = (n * g_ref[...]).astype(BF16)
        h_ref[...] = h
        for q in range(nq):
            p_ref[:, q * nb:(q + 1) * nb] = _dot(h, wi_ref[q])

        a = p_ref[:, 0:D_CONV]
        g = p_ref[:, D_CONV:2 * D_CONV]
        p = p_ref[:, 2 * D_CONV:]
        ext_s[0:HALO, :] = tail_s[:, 0:D_CONV] * jax.nn.sigmoid(tail_s[:, D_CONV:2 * D_CONV])
        ext_s[HALO:, :] = a * jax.nn.sigmoid(g)
        pext_s[0:HALO, :] = tail_s[:, 2 * D_CONV:]
        pext_s[HALO:, :] = p
        tail_s[...] = p_ref[tm - HALO:tm, :]

        _fill_shifted(ext_s, sh_s, tm)
        u1 = jnp.broadcast_to(cb_ref[...], (tm, D_CONV))
        for k in range(CONV_WIDTH):
            u1 = u1 + dw_ref[k:k + 1, :] * _window(ext_s, sh_s, HALO - (CONV_WIDTH - 1) + k, tm)
        u1_ref[...] = u1
        _, nhat = _layernorm_stats(u1)
        u2 = nhat * lg_ref[...] + lb_ref[...]
        u3 = (u2 * jax.nn.sigmoid(u2)).astype(BF16)
        u3_ref[...] = u3
        cat_ref[:, 0:D_CONV] = _dot(u3, pw_ref[...]).astype(BF16)

        pos1 = _positions(i, tm, tm) + 1.0
        sums = _window_sums(pext_s, lva_s, lvb_s, tm, ahead=False)
        for gi, w in enumerate(POOL_WINDOWS):
            cols = slice(gi * POOL_GROUP, (gi + 1) * POOL_GROUP)
            mixed = (sums[gi] / jnp.minimum(pos1, float(w)) - p[:, cols]).astype(BF16)
            mx_ref[:, cols] = mixed
            out = _dot(mixed, plw_ref[gi]) * ps_ref[:, cols]
            cat_ref[:, D_CONV + gi * POOL_GROUP:D_CONV + (gi + 1) * POOL_GROUP] = out.astype(BF16)

        x2_ref[...] = x_ref[...] + _dot(cat_ref[...], wo_ref[...])

    return _call(
        body, name=name, grid=(t_len // tm,),
        in_specs=[_tile(tm, d), _whole((1, d)), _whole((nq, d, nb)), _whole((CONV_WIDTH + 1, D_CONV)),
                  _whole((1, D_CONV)), _whole((1, D_CONV)), _whole((1, D_CONV)), _whole((D_CONV, D_CONV)),
                  _whole((4, POOL_GROUP, POOL_GROUP)), _whole((1, D_POOL)), _whole((D_CONV + D_POOL, d))],
        out_specs=[_tile(tm, d), _tile(tm, d), _tile(tm, D_IN), _tile(tm, D_CONV), _tile(tm, D_CONV),
                   _tile(tm, D_POOL), _tile(tm, D_CONV + D_POOL)],
        out_shape=[_sds((t_len, d), F32), _sds((t_len, d), BF16), _sds((t_len, D_IN), F32),
                   _sds((t_len, D_CONV), F32), _sds((t_len, D_CONV), BF16), _sds((t_len, D_POOL), BF16),
                   _sds((t_len, D_CONV + D_POOL), BF16)],
        scratch_shapes=[pltpu.VMEM((tm + HALO, D_CONV), F32), pltpu.VMEM((tm + HALO, D_POOL), F32),
                        pltpu.VMEM((SUBLANES - 1, tm + SHIFT_ROWS, D_CONV), F32), pltpu.VMEM((HALO, D_IN), F32)]
        + [pltpu.VMEM((tm + HALO, D_POOL), F32)] * 2,
        args=[x1, gain, w_in, conv_dw, conv_b, ln_g, ln_b, conv_pw, pool_w, pool_scale, w_out], cargos=cargos,
        vmem_limit_bytes=VMEM_LIMIT_BYTES_LARGE)


def _mix_bwd(dx2, u1, u3, mixed, proj, x1, gain, conv_dw, ln_g, ln_b, conv_pw, pool_w, pool_scale, w_out, w_in,
             name, cargos=()):
    t_len, d = x1.shape
    nq, _, nb = w_in.shape
    tm = min(TM_MIX, t_len)
    hb = tm // HALO
    n_tiles = t_len // tm

    def body(dxn_ref, u1_ref, u3_ref, mx_ref, p_ref, tail_ref, x_ref, dx2_ref, g_ref, dw_ref, lg_ref, lb_ref, pw_ref,
             plw_ref, ps_ref, wo_ref, wi_ref,
             dx1_ref, dp_ref, dpw_ref, dplw_ref, ddw_ref, dcb_ref, dlg_ref, dlb_ref, dps_ref, dgain_ref,
             du_s, dm_s, uext_s, dext_s, mext_s, ush_s, dsh_s, lva_s, lvb_s):
        k = pl.program_id(0)

        @pl.when(k == 0)
        def _():
            for ref in (dpw_ref, dplw_ref, ddw_ref, dcb_ref, dlg_ref, dlb_ref, dps_ref, dgain_ref, du_s, dm_s):
                ref[...] = jnp.zeros_like(ref)

        counts = jnp.where(k < n_tiles, 1.0, 0.0)
        dcat = _dot_nt(dxn_ref[...].astype(BF16), wo_ref[...])
        dco = dcat[:, 0:D_CONV].astype(BF16)
        dpw_ref[...] += _dot_tn(u3_ref[...], (dcat[:, 0:D_CONV] * counts).astype(BF16))
        du3 = _dot_nt(dco, pw_ref[...])
        rstd, nhat = _layernorm_stats(u1_ref[...])
        u2 = nhat * lg_ref[...] + lb_ref[...]
        sig = jax.nn.sigmoid(u2)
        du2 = du3 * (sig * (1.0 + u2 * (1.0 - sig)))
        dlg_ref[...] += counts * jnp.sum(du2 * nhat, axis=0, keepdims=True)
        dlb_ref[...] += counts * jnp.sum(du2, axis=0, keepdims=True)
        dnhat = du2 * lg_ref[...]
        du_s[k % 2] = rstd * (dnhat - jnp.mean(dnhat, axis=-1, keepdims=True)
                              - nhat * jnp.mean(dnhat * nhat, axis=-1, keepdims=True))
        for gi in range(len(POOL_WINDOWS)):
            cols = slice(gi * POOL_GROUP, (gi + 1) * POOL_GROUP)
            dpo = dcat[:, D_CONV + gi * POOL_GROUP:D_CONV + (gi + 1) * POOL_GROUP]
            pre = _dot(mx_ref[:, cols], plw_ref[gi])
            dps_ref[:, cols] += counts * jnp.sum(dpo * pre, axis=0, keepdims=True)
            dout = dpo * ps_ref[:, cols]
            dplw_ref[gi] += _dot_tn(mx_ref[:, cols], (dout * counts).astype(BF16))
            dm_s[k % 2, :, cols] = _dot_nt(dout.astype(BF16), plw_ref[gi])

        i = jnp.maximum(k - 1, 0)
        cur, nxt = (k + 1) % 2, k % 2
        first = k <= 1
        last = (k == n_tiles) | (k == 0)
        a = p_ref[:, 0:D_CONV]
        g = p_ref[:, D_CONV:2 * D_CONV]
        sg = jax.nn.sigmoid(g)
        ta = tail_ref[:, 0:D_CONV]
        tg = tail_ref[:, D_CONV:2 * D_CONV]
        uext_s[0:HALO, :] = jnp.where(first, 0.0, ta * jax.nn.sigmoid(tg))
        uext_s[HALO:, :] = a * sg
        du1 = du_s[cur]
        dext_s[0:tm, :] = du1
        dext_s[tm:, :] = jnp.where(last, 0.0, du_s[nxt, 0:HALO, :])

        _fill_shifted(uext_s, ush_s, tm)
        _fill_shifted(dext_s, dsh_s, tm)
        du0 = jnp.zeros((tm, D_CONV), F32)
        for tap in range(CONV_WIDTH):
            du0 = du0 + dw_ref[tap:tap + 1, :] * _window(dext_s, dsh_s, CONV_WIDTH - 1 - tap, tm)
            ddw_ref[tap:tap + 1, :] += jnp.sum(
                du1 * _window(uext_s, ush_s, HALO - (CONV_WIDTH - 1) + tap, tm), axis=0, keepdims=True)
        dcb_ref[...] += jnp.sum(du1, axis=0, keepdims=True)
        dp_ref[:, 0:D_CONV] = (du0 * sg).astype(BF16)
        dp_ref[:, D_CONV:2 * D_CONV] = (du0 * a * sg * (1.0 - sg)).astype(BF16)

        pos1 = _positions(i, tm, tm) + 1.0
        pos1_next = _positions(i, tm, HALO, offset=tm) + 1.0
        for gi, w in enumerate(POOL_WINDOWS):
            cols = slice(gi * POOL_GROUP, (gi + 1) * POOL_GROUP)
            dm = dm_s[cur, :, cols]
            mext_s[0:tm, cols] = dm / jnp.minimum(pos1, float(w))
            mext_s[tm:, cols] = jnp.where(last, 0.0, dm_s[nxt, 0:HALO, cols] / jnp.minimum(pos1_next, float(w)))
        sums = _window_sums(mext_s, lva_s, lvb_s, tm, ahead=True)
        for gi in range(len(POOL_WINDOWS)):
            cols = slice(gi * POOL_GROUP, (gi + 1) * POOL_GROUP)
            dp_ref[:, 2 * D_CONV + gi * POOL_GROUP:2 * D_CONV + (gi + 1) * POOL_GROUP] = (
                sums[gi] - dm_s[cur, :, cols]).astype(BF16)

        dh = _dot_nt(dp_ref[:, 0:nb], wi_ref[0])
        for q in range(1, nq):
            dh = dh + _dot_nt(dp_ref[:, q * nb:(q + 1) * nb], wi_ref[q])
        r, n = _rms_stats(x_ref[...])
        dgain_ref[...] += jnp.sum(dh * n, axis=0, keepdims=True)
        dx1_ref[...] = dx2_ref[...] + _rms_bwd(dh, n, r, g_ref[...])

    def ahead(cols):
        return pl.BlockSpec((tm, cols), lambda k: (jnp.minimum(k, n_tiles - 1), 0))

    def behind(cols):
        return pl.BlockSpec((tm, cols), lambda k: (jnp.maximum(k - 1, 0), 0))

    vec = _whole((1, D_CONV))
    return _call(
        body, name=name, grid=(n_tiles + 1,),
        in_specs=[ahead(d), ahead(D_CONV), ahead(D_CONV), ahead(D_POOL), behind(D_IN),
                  pl.BlockSpec((HALO, D_IN), lambda k: (jnp.maximum(jnp.maximum(k - 1, 0) * hb - 1, 0), 0)),
                  behind(d), behind(d), _whole((1, d)), _whole((CONV_WIDTH + 1, D_CONV)), vec, vec,
                  _whole((D_CONV, D_CONV)), _whole((4, POOL_GROUP, POOL_GROUP)), vec,
                  _whole((D_CONV + D_POOL, d)), _whole((nq, d, nb))],
        out_specs=[behind(d), behind(D_IN), _whole((D_CONV, D_CONV)), _whole((4, POOL_GROUP, POOL_GROUP)),
                   _whole((CONV_WIDTH + 1, D_CONV)), vec, vec, vec, vec, _whole((1, d))],
        out_shape=[_sds((t_len, d), F32), _sds((t_len, D_IN), BF16), _sds((D_CONV, D_CONV), F32),
                   _sds((4, POOL_GROUP, POOL_GROUP), F32), _sds((CONV_WIDTH + 1, D_CONV), F32), _sds((1, D_CONV), F32),
                   _sds((1, D_CONV), F32), _sds((1, D_CONV), F32), _sds((1, D_POOL), F32), _sds((1, d), F32)],
        scratch_shapes=[pltpu.VMEM((2, tm, D_CONV), F32), pltpu.VMEM((2, tm, D_POOL), F32),
                        pltpu.VMEM((tm + HALO, D_CONV), F32), pltpu.VMEM((tm + HALO, D_CONV), F32),
                        pltpu.VMEM((tm + HALO, D_POOL), F32),
                        pltpu.VMEM((SUBLANES - 1, tm + SHIFT_ROWS, D_CONV), F32),
                        pltpu.VMEM((SUBLANES - 1, tm + SHIFT_ROWS, D_CONV), F32)]
        + [pltpu.VMEM((tm + HALO, D_POOL), F32)] * 2,
        args=[dx2, u1, u3, mixed, proj, proj, x1, dx2, gain, conv_dw, ln_g, ln_b, conv_pw, pool_w, pool_scale,
              w_out, w_in], cargos=cargos, vmem_limit_bytes=VMEM_LIMIT_BYTES_LARGE)


def _final_norm_loss(x3, target, gain, name):
    t_len, d = x3.shape
    tm = min(2 * TM_FFN, t_len)

    def body(x_ref, t_ref, g_ref, dx_ref, loss_ref, dgain_ref):
        @pl.when(pl.program_id(0) == 0)
        def _():
            loss_ref[...] = jnp.zeros_like(loss_ref)
            dgain_ref[...] = jnp.zeros_like(dgain_ref)

        r, n = _rms_stats(x_ref[...])
        err = n * g_ref[...] - t_ref[...]
        per_tok = jnp.sum(err * err, axis=-1, keepdims=True) * (1.0 / d)
        loss_ref[...] += 0.5 * jnp.sum(per_tok, axis=0, keepdims=True)
        dy = err * (1.0 / d)
        dgain_ref[...] += jnp.sum(dy * n, axis=0, keepdims=True)
        dx_ref[...] = _rms_bwd(dy, n, r, g_ref[...])

    tok = pl.BlockSpec((tm, d), lambda i: (i, 0))
    outs, _ = _call(
        body, name=name, grid=(t_len // tm,),
        in_specs=[tok, tok, pl.BlockSpec((1, d), lambda i: (0, 0))],
        out_specs=[tok, pl.BlockSpec((1, 128), lambda i: (0, 0)), pl.BlockSpec((1, d), lambda i: (0, 0))],
        out_shape=[_sds((t_len, d), F32), _sds((1, 128), F32), _sds((1, d), F32)],
        args=[x3, target, gain])
    return outs


def _row_tile(rows):
    return rows // 2 if rows % 64 == 0 else rows


def _adamw_math(w, g, m, v):
    m = ADAM_B1 * m + (1.0 - ADAM_B1) * g
    v = ADAM_B2 * v + (1.0 - ADAM_B2) * (g * g)
    m_hat = m / (1.0 - ADAM_B1 ** ADAM_STEP)
    v_hat = v / (1.0 - ADAM_B2 ** ADAM_STEP)
    delta = -ADAM_LR * (m_hat / (jnp.sqrt(v_hat) + ADAM_EPS) + ADAM_WD * w)
    return delta, m, v


def _adamw(parts, w, m, v, name):
    r, c = w.shape
    n = len(parts)
    tr = _row_tile(r)

    def body(*refs):
        g = None
        for p_ref in refs[:n]:
            s = p_ref[0].astype(F32)
            for k in range(1, p_ref.shape[0]):
                s = s + p_ref[k].astype(F32)
            g = s if g is None else g + s
        w_ref, m_ref, v_ref, g_out, d_out, m_out, v_out = refs[n:]
        delta, nm, nv = _adamw_math(w_ref[...], g, m_ref[...], v_ref[...])
        g_out[...] = g
        d_out[...] = delta
        m_out[...] = nm
        v_out[...] = nv

    blk = pl.BlockSpec((tr, c), lambda i: (i, 0))
    p_specs = [pl.BlockSpec((p.shape[0], tr, c), lambda i: (0, i, 0)) for p in parts]
    outs, _ = _call(body, name=name, grid=(r // tr,), in_specs=p_specs + [blk, blk, blk],
                    out_specs=[blk] * 4, out_shape=[_sds((r, c), F32)] * 4, args=[*parts, w, m, v])
    return outs


FFN_W = ("w_gate", "w_up", "w_down")
MID = ("w_in", "conv_dw", "conv_pw", "w_out")
SMALL_1024 = ("ffn1_norm", "mix_norm", "ffn2_norm", "final_norm")
SMALL_512 = ("conv_dw_b", "conv_ln_g", "conv_ln_b", "pool_scale")
WEIGHTS = ("ffn1_norm", "ffn1_w_gate", "ffn1_w_up", "ffn1_w_down", "mix_norm", "w_in", "conv_dw", "conv_dw_b",
           "conv_ln_g", "conv_ln_b", "conv_pw", "pool_w", "pool_scale", "w_out", "ffn2_norm", "ffn2_w_gate",
           "ffn2_w_up", "ffn2_w_down", "final_norm")
PACK_ROWS = 72
PACK_LOSS_ROW = 70


def _pad_rows(a, rows):
    return jnp.pad(a, ((0, rows - a.shape[0]), (0, 0)))


def _pack_small(t, spare=None):
    rows = [t[k].reshape(1, D_MODEL) for k in SMALL_1024]
    rows.append(jnp.concatenate([t["conv_dw_b"].reshape(1, -1), t["conv_ln_g"].reshape(1, -1)], axis=1))
    rows.append(jnp.concatenate([t["conv_ln_b"].reshape(1, -1), t["pool_scale"].reshape(1, -1)], axis=1))
    rows.append(t["pool_w"].reshape(64, D_MODEL))
    if spare is not None:
        rows.append(jnp.pad(spare, ((0, 0), (0, D_MODEL - spare.shape[1]))))
    return _pad_rows(jnp.concatenate(rows, axis=0), PACK_ROWS)


def _unpack_small(p):
    out = {k: p[i] for i, k in enumerate(SMALL_1024)}
    out["conv_dw_b"], out["conv_ln_g"] = p[4, :D_CONV], p[4, D_CONV:]
    out["conv_ln_b"], out["pool_scale"] = p[5, :D_CONV], p[5, D_CONV:]
    out["pool_w"] = p[6:70].reshape(4, POOL_GROUP, POOL_GROUP)
    return out


def _as_stored(name, a):
    if name.endswith(("w_gate", "w_up")):
        return a.T
    if name == "conv_dw":
        return _pad_rows(a, CONV_WIDTH + 1)
    return a


def _as_given(name, a):
    if name.endswith(("w_gate", "w_up")):
        return a.T
    if name == "conv_dw":
        return a[:CONV_WIDTH]
    return a


def kernel(x, ffn1_norm, ffn1_w_gate, ffn1_w_up, ffn1_w_down, mix_norm, w_in, conv_dw, conv_dw_b, conv_ln_g, conv_ln_b, conv_pw, pool_w, pool_scale, w_out, ffn2_norm, ffn2_w_gate, ffn2_w_up, ffn2_w_down, final_norm, loss_target, m_ffn1_norm, m_ffn1_w_gate, m_ffn1_w_up, m_ffn1_w_down, m_mix_norm, m_w_in, m_conv_dw, m_conv_dw_b, m_conv_ln_g, m_conv_ln_b, m_conv_pw, m_pool_w, m_pool_scale, m_w_out, m_ffn2_norm, m_ffn2_w_gate, m_ffn2_w_up, m_ffn2_w_down, m_final_norm, v_ffn1_norm, v_ffn1_w_gate, v_ffn1_w_up, v_ffn1_w_down, v_mix_norm, v_w_in, v_conv_dw, v_conv_dw_b, v_conv_ln_g, v_conv_ln_b, v_conv_pw, v_pool_w, v_pool_scale, v_w_out, v_ffn2_norm, v_ffn2_w_gate, v_ffn2_w_up, v_ffn2_w_down, v_final_norm):
    given = dict(locals())
    wts = {k: given[k] for k in WEIGHTS}
    mom_m = {k: given["m_" + k] for k in WEIGHTS}
    mom_v = {k: given["v_" + k] for k in WEIGHTS}
    xt, target = x[0], loss_target[0]

    shard = {k: _as_stored(k, wts[k]) if k == "conv_dw" else _as_stored(k, wts[k]).astype(BF16)
             for k in WEIGHTS if k.endswith(FFN_W) or k in MID}
    w = {k: wts[k].reshape(1, -1) for k in SMALL_1024 + SMALL_512}
    w["pool_w"] = wts["pool_w"].astype(BF16)

    (h1, s1, p1, a1, w["ffn1_w_gate"], w["ffn1_w_up"]), ((w["ffn1_w_down"],),) = _ffn_up_gather(
        xt, w["ffn1_norm"], shard["ffn1_w_gate"], shard["ffn1_w_up"], "ffn1_up_gather",
        cargos=[Cargo("gather_slots", [shard["ffn1_w_down"]])])
    x1, (mid, (w["ffn2_w_down"],)) = _ffn_down(
        xt, a1, w["ffn1_w_down"], "ffn1_down",
        cargos=[Cargo("gather_chips", [shard[k] for k in MID]), Cargo("gather_slots", [shard["ffn2_w_down"]])])
    w["w_in"] = mid[0]
    w["conv_dw"] = mid[1].transpose(1, 0, 2).reshape(CONV_WIDTH + 1, D_CONV)
    w["conv_pw"] = mid[2].reshape(D_CONV, D_CONV)
    w["w_out"] = mid[3].reshape(D_CONV + D_POOL, D_MODEL)
    (x2, h2, proj, u1, u3, mixed, cat), ((w["ffn2_w_gate"], w["ffn2_w_up"]),) = _mix_fwd(
        x1, w["mix_norm"], w["w_in"], w["conv_dw"], w["conv_dw_b"], w["conv_ln_g"], w["conv_ln_b"], w["conv_pw"],
        w["pool_w"], w["pool_scale"], w["w_out"], "mix_fwd",
        cargos=[Cargo("gather_slots", [shard["ffn2_w_gate"], shard["ffn2_w_up"]])])
    x3, h3, s2, p2, a2 = _ffn_fwd(x2, w["ffn2_norm"], w["ffn2_w_gate"], w["ffn2_w_up"], w["ffn2_w_down"], "ffn2_fwd")
    dx3, loss_share, d_final = _final_norm_loss(x3, target, w["final_norm"], "final_norm_loss")

    g = {"final_norm": d_final}
    sums = {}

    def landed(names, parts):
        sums.update(zip(names, parts))

    dx2, g["ffn2_norm"], df2, dg2, du2 = _ffn_bwd(dx3, x2, w["ffn2_norm"], s2, p2, w["ffn2_w_gate"],
                                                   w["ffn2_w_up"], w["ffn2_w_down"], "ffn2_bwd")
    def ffn_wgrad(names, hids, tok, kernel_name, cargos=()):
        parts, cargo_outs = _wgrad_hid_tok_scatter(hids, tok, kernel_name, cargos=cargos)
        landed(names, parts)
        return cargo_outs

    ffn_wgrad(["ffn2_w_gate", "ffn2_w_up"], [dg2, du2], h3, "ffn2_dw_gate_up")
    ffn_wgrad(["ffn2_w_down"], [a2], df2, "ffn2_dw_down")
    (dx1, dproj, g_pw, g["pool_w"], g_dw, g["conv_dw_b"], g["conv_ln_g"], g["conv_ln_b"], g["pool_scale"],
     g["mix_norm"]), (swapped2,) = _mix_bwd(
        dx2, u1, u3, mixed, proj, x1, w["mix_norm"], w["conv_dw"], w["conv_ln_g"], w["conv_ln_b"], w["conv_pw"],
        w["pool_w"], w["pool_scale"], w["w_out"], w["w_in"], "mix_bwd",
        cargos=[Cargo("swap", [sums["ffn2_" + k] for k in FFN_W])])
    g_out, _ = _wgrad_2d(cat, dx2, 1, BF16, "dw_out")
    slabs = [g_pw.reshape(N_CHIPS, D_CONV // N_CHIPS, D_CONV),
             g_out.reshape(N_CHIPS, (D_CONV + D_POOL) // N_CHIPS, D_MODEL)]
    g_in, (parts,) = _wgrad_2d(h2, dproj, N_CHIPS, BF16, "dw_in", cargos=[Cargo("scatter_chips", slabs)])
    landed(["conv_pw", "w_out"], parts)
    dx, g["ffn1_norm"], df1, dg1, du1_ = _ffn_bwd(dx1, xt, w["ffn1_norm"], s1, p1, w["ffn1_w_gate"],
                                                   w["ffn1_w_up"], w["ffn1_w_down"], "ffn1_bwd")
    slabs = [g_in, g_dw.reshape(CONV_WIDTH + 1, N_CHIPS, D_CONV // N_CHIPS).transpose(1, 0, 2)]
    (parts,) = ffn_wgrad(["ffn1_w_gate", "ffn1_w_up"], [dg1, du1_], h1, "ffn1_dw_gate_up",
                         cargos=[Cargo("scatter_chips", slabs)])
    landed(["w_in", "conv_dw"], parts)
    swapped_mid, swapped_gate_up, small_parts = ffn_wgrad(
        ["ffn1_w_down"], [a1], df1, "ffn1_dw_down",
        cargos=[Cargo("swap", [sums[k] for k in MID]), Cargo("swap", [sums["ffn1_w_gate"], sums["ffn1_w_up"]]),
                Cargo("gather_devices", [_pack_small(g, spare=loss_share)])])
    swapped_down = _exchange(Cargo("swap", [sums["ffn1_w_down"]]), "swap_last")

    theirs = dict(zip(["ffn2_" + k for k in FFN_W], swapped2))
    theirs.update(zip(MID, swapped_mid))
    theirs.update(ffn1_w_gate=swapped_gate_up[0], ffn1_w_up=swapped_gate_up[1], ffn1_w_down=swapped_down[0])
    grads, deltas, new_m, new_v = {}, {}, {}, {}
    for k in theirs:
        res = _adamw([sums[k], theirs[k]], _as_stored(k, wts[k]), _as_stored(k, mom_m[k]),
                     _as_stored(k, mom_v[k]), "adamw_" + k)
        grads[k], deltas[k], new_m[k], new_v[k] = [_as_given(k, t) for t in res]
    res = _adamw(small_parts, _pack_small(wts), _pack_small(mom_m), _pack_small(mom_v), "adamw_small")
    for dst, packed in zip((grads, deltas, new_m, new_v), res):
        dst.update(_unpack_small(packed))
    loss = res[0][PACK_LOSS_ROW, 0]

    out = [loss, dx[None]]
    for group in (grads, deltas, new_m, new_v):
        out += [group[k] for k in WEIGHTS]
    return tuple(out)
```
